```python
import math
import functools
import jax
import jax.numpy as jnp
from jax import lax
import numpy as np

D_MODEL = 1024
BATCH = 16
SEQ = 2048
DEPTH = 2
DEC_BATCH = 16
DEC_SEQ = 16
PAST_LEN = 4096

CHUNK = 64
EPS = 1e-6
BRANCH_WIDTH = D_MODEL // 2
N_BRANCH = 3
GMLP_CHUNK = 128
GMLP_GROUPS = 4
GMLP_WIDTH = BRANCH_WIDTH
GMLP_GDIM = GMLP_WIDTH // GMLP_GROUPS
POOL_WINDOWS = (2, 4, 8, 16)
POOL_GROUPS = 4
POOL_WIDTH = BRANCH_WIDTH
POOL_GDIM = POOL_WIDTH // POOL_GROUPS
POOL_STATE = 15
ATT_HEADS = 8
ATT_WIDTH = BRANCH_WIDTH
ATT_HEAD_DIM = ATT_WIDTH // ATT_HEADS
BAND_CHUNKS = 8
BAND_PAST = BAND_CHUNKS * CHUNK
REL_CLIP = 128
IN_WIDTH = 6 * BRANCH_WIDTH + N_BRANCH * D_MODEL
PEER_HEADS = 8
PEER_NKEYS = 128
PEER_EXPERTS = PEER_NKEYS * PEER_NKEYS
PEER_QDIM = 256
PEER_HALF = PEER_QDIM // 2
PEER_TOPK = 16
PEER_BLOCK = 128
PLE_DIM = 256

kernel_name = 'chunk_stream_hybrid_encoder'


def rmsnorm(x, g):
    xf = x.astype(jnp.float32)
    y = xf * lax.rsqrt(jnp.mean(xf * xf, axis=-1, keepdims=True) + EPS)
    return (y * g.astype(jnp.float32)).astype(x.dtype)


def layernorm(x, g, b):
    xf = x.astype(jnp.float32)
    xc = xf - jnp.mean(xf, axis=-1, keepdims=True)
    var = jnp.mean(xc * xc, axis=-1, keepdims=True)
    y = xc * lax.rsqrt(var + EPS) * g.astype(jnp.float32) + b.astype(jnp.float32)
    return y.astype(x.dtype)


def gmlp_mix(u, v, ln_g, ln_b, ws, bs):
    B, L, W = v.shape
    vn = layernorm(v, ln_g, ln_b)
    lc = min(L, GMLP_CHUNK)
    nc = L // lc
    vc = vn.reshape(B, nc, lc, GMLP_GROUPS, GMLP_GDIM)
    pos = jnp.arange(lc)
    mask = (pos[None, :] // CHUNK) <= (pos[:, None] // CHUNK)
    w = jnp.where(mask[None], ws[:, :lc, :lc], jnp.zeros((), ws.dtype))
    s = jnp.einsum('gij,bcjgd->bcigd', w, vc) + bs[:, :lc].T[None, None, :, :, None]
    return u * s.reshape(B, L, W), vn


def pool_mix(u, prefix, start_pos, pool_w, pool_scale):
    B, L, W = u.shape
    full = jnp.concatenate([prefix.astype(u.dtype), u], axis=1)
    ff = full.astype(jnp.float32)
    cs = jnp.concatenate([jnp.zeros((B, 1, W), jnp.float32), jnp.cumsum(ff, axis=1)], axis=1)
    end = cs[:, POOL_STATE + 1:]
    pos = start_pos + jnp.arange(L)
    means = []
    for gi, win in enumerate(POOL_WINDOWS):
        sl = slice(gi * POOL_GDIM, (gi + 1) * POOL_GDIM)
        beg = cs[:, POOL_STATE + 1 - win:POOL_STATE + 1 - win + L, sl]
        cnt = jnp.minimum(pos + 1, win).astype(jnp.float32)[None, :, None]
        means.append((end[..., sl] - beg) / cnt)
    d = jnp.concatenate(means, axis=-1) - ff[:, POOL_STATE:]
    d = d.reshape(B, L, POOL_GROUPS, POOL_GDIM).astype(u.dtype)
    y = jnp.einsum('blgc,gcd->blgd', d, pool_w).reshape(B, L, W) * pool_scale
    return y, full[:, -POOL_STATE:]


def band_attention_core(q, k, v, qpos, kpos, valid, rel_bias):
    s = jnp.einsum('bqhd,bkhd->bhqk', q.astype(jnp.float32), k.astype(jnp.float32)) * (ATT_HEAD_DIM ** -0.5)
    rel = jnp.clip(qpos[:, None] - kpos[None, :], -REL_CLIP, REL_CLIP) + REL_CLIP
    s = s + rel_bias.astype(jnp.float32)[:, rel][None]
    s = jnp.where(valid[None, None, None, :], s, -1e30)
    p = jax.nn.softmax(s, axis=-1)
    return jnp.einsum('bhqk,bkhd->bqhd', p, v.astype(jnp.float32)).astype(q.dtype)


def attn_prompt(q, k, v, rel_bias):
    B, L, H, Dh = q.shape
    nc = L // CHUNK
    pad = ((0, 0), (BAND_PAST, 0), (0, 0), (0, 0))
    kp = jnp.pad(k, pad)
    vp = jnp.pad(v, pad)

    def one_chunk(c):
        q_c = lax.dynamic_slice_in_dim(q, c * CHUNK, CHUNK, axis=1)
        k_c = lax.dynamic_slice_in_dim(kp, c * CHUNK, BAND_PAST + CHUNK, axis=1)
        v_c = lax.dynamic_slice_in_dim(vp, c * CHUNK, BAND_PAST + CHUNK, axis=1)
        qpos = c * CHUNK + jnp.arange(CHUNK)
        kpos = c * CHUNK - BAND_PAST + jnp.arange(BAND_PAST + CHUNK)
        return band_attention_core(q_c, k_c, v_c, qpos, kpos, kpos >= 0, rel_bias)

    out = lax.map(one_chunk, jnp.arange(nc))
    return jnp.transpose(out, (1, 0, 2, 3, 4)).reshape(B, L, H, Dh)


def attn_sample(q, k, v, rel_bias, cache_k, cache_v):
    L = q.shape[1]
    n = cache_k.shape[1]
    k_all = jnp.concatenate([cache_k.astype(k.dtype), k], axis=1)
    v_all = jnp.concatenate([cache_v.astype(v.dtype), v], axis=1)
    qpos = PAST_LEN + jnp.arange(L)
    kpos = PAST_LEN - n + jnp.arange(n + L)
    return band_attention_core(q, k_all, v_all, qpos, kpos, kpos >= 0, rel_bias)


def peer(x, wq, subkeys, u_tab, v_tab):
    T, D = x.shape
    nb = -(-T // PEER_BLOCK)
    xp = jnp.pad(x, ((0, nb * PEER_BLOCK - T), (0, 0))).reshape(nb, PEER_BLOCK, D)

    def block(xb):
        q = (xb @ wq).reshape(PEER_BLOCK, PEER_HEADS, 2, PEER_HALF).astype(jnp.float32)
        s = jnp.einsum('thpc,hpnc->thpn', q, subkeys.astype(jnp.float32))
        sv, si = lax.top_k(s, PEER_TOPK)
        cand = sv[:, :, 0, :, None] + sv[:, :, 1, None, :]
        cidx = si[:, :, 0, :, None] * PEER_NKEYS + si[:, :, 1, None, :]
        cand = cand.reshape(PEER_BLOCK, PEER_HEADS, PEER_TOPK * PEER_TOPK)
        cidx = cidx.reshape(PEER_BLOCK, PEER_HEADS, PEER_TOPK * PEER_TOPK)
        top_v, top_i = lax.top_k(cand, PEER_TOPK)
        eidx = jnp.take_along_axis(cidx, top_i, axis=-1).reshape(PEER_BLOCK, PEER_HEADS * PEER_TOPK)
        g = jax.nn.softmax(top_v, axis=-1).reshape(PEER_BLOCK, PEER_HEADS * PEER_TOPK)
        u_e = jnp.take(u_tab, eidx, axis=0)
        a = jax.nn.gelu(jnp.einsum('ted,td->te', u_e, xb).astype(jnp.float32))
        coef = (g * a).astype(xb.dtype)
        v_e = jnp.take(v_tab, eidx, axis=0)
        return jnp.einsum('te,ted->td', coef, v_e)

    return lax.map(block, xp).reshape(nb * PEER_BLOCK, D)[:T]


def trunk_layer(x, ple, attn_fn, pool_prefix, pool_start, norm_mix, w_in, gmlp_ln_g, gmlp_ln_b,
                gmlp_ws, gmlp_bs, pool_w, pool_scale, rel_bias, w_branch, w_out, norm_ffn,
                peer_wq, peer_subkeys, peer_u, peer_v, norm_ple, ple_gate, ple_proj):
    B, L, D = x.shape
    bw = BRANCH_WIDTH
    h = rmsnorm(x, norm_mix)
    z = h @ w_in
    uv = jax.nn.gelu(z[..., :2 * bw])
    u_a, v_a = uv[..., :bw], uv[..., bw:]
    u_b = z[..., 2 * bw:3 * bw]
    q = z[..., 3 * bw:4 * bw].reshape(B, L, ATT_HEADS, ATT_HEAD_DIM)
    k = z[..., 4 * bw:5 * bw].reshape(B, L, ATT_HEADS, ATT_HEAD_DIM)
    v = z[..., 5 * bw:6 * bw].reshape(B, L, ATT_HEADS, ATT_HEAD_DIM)
    gates = jax.nn.sigmoid(z[..., 6 * bw:].reshape(B, L, N_BRANCH, D))
    y_a, vn_a = gmlp_mix(u_a, v_a, gmlp_ln_g, gmlp_ln_b, gmlp_ws, gmlp_bs)
    y_b, pool_state = pool_mix(u_b, pool_prefix, pool_start, pool_w, pool_scale)
    y_c = attn_fn(q, k, v, rel_bias).reshape(B, L, bw)
    branches = jnp.stack([y_a, y_b, y_c], axis=2)
    proj = jnp.einsum('blnc,ncd->blnd', branches, w_branch)
    x = x + jnp.sum(gates * proj, axis=2) @ w_out
    x = x + peer(rmsnorm(x, norm_ffn).reshape(B * L, D), peer_wq, peer_subkeys, peer_u, peer_v).reshape(B, L, D)
    x = x + jax.nn.sigmoid(rmsnorm(x, norm_ple) @ ple_gate) * (ple @ ple_proj)
    return x, k, v, pool_state, vn_a


def setup_inputs(seed: int = 0) -> dict:
    key = jax.random.key(seed)
    ks = jax.random.split(key, 32)
    nrm = jax.random.normal
    f = jnp.float32
    n_cache = min(BAND_PAST, PAST_LEN)
    return {
        'x_prompt': nrm(ks[0], (BATCH, SEQ, D_MODEL), f),
        'x_sample': nrm(ks[1], (DEC_BATCH, DEC_SEQ, D_MODEL), f),
        'cache_attn_k': nrm(ks[2], (DEPTH, DEC_BATCH, n_cache, ATT_HEADS, ATT_HEAD_DIM), f),
        'cache_attn_v': nrm(ks[3], (DEPTH, DEC_BATCH, n_cache, ATT_HEADS, ATT_HEAD_DIM), f),
        'state_pool': nrm(ks[4], (DEPTH, DEC_BATCH, POOL_STATE, POOL_WIDTH), f),
        'p_prompt': nrm(ks[5], (DEPTH, BATCH, SEQ, PLE_DIM), f),
        'p_sample': nrm(ks[6], (DEPTH, DEC_BATCH, DEC_SEQ, PLE_DIM), f),
        'norm_mix': 1.0 + 0.1 * nrm(ks[7], (DEPTH, D_MODEL), f),
        'w_in': nrm(ks[8], (DEPTH, D_MODEL, IN_WIDTH), f) * D_MODEL ** -0.5,
        'gmlp_ln_g': 1.0 + 0.1 * nrm(ks[9], (DEPTH, GMLP_WIDTH), f),
        'gmlp_ln_b': 0.1 * nrm(ks[10], (DEPTH, GMLP_WIDTH), f),
        'gmlp_ws': nrm(ks[11], (DEPTH, GMLP_GROUPS, GMLP_CHUNK, GMLP_CHUNK), f) * (0.5 * GMLP_CHUNK ** -0.5),
        'gmlp_bs': 1.0 + 0.1 * nrm(ks[12], (DEPTH, GMLP_GROUPS, GMLP_CHUNK), f),
        'pool_w': nrm(ks[13], (DEPTH, POOL_GROUPS, POOL_GDIM, POOL_GDIM), f) * POOL_GDIM ** -0.5,
        'pool_scale': 1.0 + 0.1 * nrm(ks[14], (DEPTH, POOL_WIDTH), f),
        'attn_rel_bias': 0.5 * nrm(ks[15], (DEPTH, ATT_HEADS, 2 * REL_CLIP + 1), f),
        'w_branch': nrm(ks[16], (DEPTH, N_BRANCH, BRANCH_WIDTH, D_MODEL), f) * BRANCH_WIDTH ** -0.5,
        'w_out': nrm(ks[17], (DEPTH, D_MODEL, D_MODEL), f) * D_MODEL ** -0.5,
        'norm_ffn': 1.0 + 0.1 * nrm(ks[18], (DEPTH, D_MODEL), f),
        'peer_wq': nrm(ks[19], (DEPTH, D_MODEL, PEER_HEADS * PEER_QDIM), f) * D_MODEL ** -0.5,
        'peer_subkeys': nrm(ks[20], (DEPTH, PEER_HEADS, 2, PEER_NKEYS, PEER_HALF), f) * PEER_HALF ** -0.5,
        'peer_u': nrm(ks[21], (DEPTH, PEER_EXPERTS, D_MODEL), f) * D_MODEL ** -0.5,
        'peer_v': nrm(ks[22], (DEPTH, PEER_EXPERTS, D_MODEL), f) * PEER_HEADS ** -0.5,
        'norm_ple': 1.0 + 0.1 * nrm(ks[23], (DEPTH, D_MODEL), f),
        'ple_gate': nrm(ks[24], (DEPTH, D_MODEL, D_MODEL), f) * D_MODEL ** -0.5,
        'ple_proj': nrm(ks[25], (DEPTH, PLE_DIM, D_MODEL), f) * PLE_DIM ** -0.5,
        'norm_final': 1.0 + 0.1 * nrm(ks[26], (D_MODEL,), f),
    }


def reference(x_prompt, x_sample, cache_attn_k, cache_attn_v, state_pool, p_prompt, p_sample,
              norm_mix, w_in, gmlp_ln_g, gmlp_ln_b, gmlp_ws, gmlp_bs, pool_w, pool_scale,
              attn_rel_bias, w_branch, w_out, norm_ffn, peer_wq, peer_subkeys, peer_u, peer_v,
              norm_ple, ple_gate, ple_proj, norm_final):
    xp = x_prompt
    xs = x_sample
    n_keep = min(BAND_PAST, x_prompt.shape[1])
    pk, pv, pps, sk, sv, sps, sgv = [], [], [], [], [], [], []
    for i in range(DEPTH):
        lw = (norm_mix[i], w_in[i], gmlp_ln_g[i], gmlp_ln_b[i], gmlp_ws[i], gmlp_bs[i], pool_w[i],
              pool_scale[i], attn_rel_bias[i], w_branch[i], w_out[i], norm_ffn[i], peer_wq[i],
              peer_subkeys[i], peer_u[i], peer_v[i], norm_ple[i], ple_gate[i], ple_proj[i])
        zero_prefix = jnp.zeros((xp.shape[0], POOL_STATE, POOL_WIDTH), xp.dtype)
        xp, k_p, v_p, ps_p, _ = trunk_layer(xp, p_prompt[i], attn_prompt, zero_prefix, 0, *lw)
        pk.append(k_p[:, -n_keep:])
        pv.append(v_p[:, -n_keep:])
        pps.append(ps_p)
        afn = functools.partial(attn_sample, cache_k=cache_attn_k[i], cache_v=cache_attn_v[i])
        xs, k_s, v_s, ps_s, vn_s = trunk_layer(xs, p_sample[i], afn, state_pool[i], PAST_LEN, *lw)
        sk.append(k_s)
        sv.append(v_s)
        sps.append(ps_s)
        sgv.append(vn_s)
    y_prompt = rmsnorm(xp, norm_final)
    y_sample = rmsnorm(xs, norm_final)
    return (y_prompt, y_sample, jnp.stack(pk), jnp.stack(pv), jnp.stack(pps),
            jnp.stack(sk), jnp.stack(sv), jnp.stack(sps), jnp.stack(sgv))
```

```python
import functools
import math

import jax
import jax.numpy as jnp
import numpy as np
from jax import lax
from jax.experimental import pallas as pl
from jax.experimental.pallas import tpu as pltpu
from jax.experimental.pallas import tpu_sc as plsc

F32 = jnp.float32
BF16 = jnp.bfloat16

D_MODEL = 1024
DEPTH = 2
CHUNK = 64
EPS = 1e-6
BW = D_MODEL // 2
GMLP_CHUNK = 128
GROUPS = 4
GDIM = BW // GROUPS
POOL_WINDOWS = (2, 4, 8, 16)
POOL_STATE = 15
POOL_PAD = 16
HEADS = 8
HEAD_DIM = BW // HEADS
BAND_CHUNKS = 8
BAND_PAST = BAND_CHUNKS * CHUNK
REL_CLIP = 128
PAST_LEN = 4096
PEER_HEADS = 8
PEER_NKEYS = 128
PEER_HALF = 128
PEER_TOPK = 16
PEER_SLOTS = PEER_HEADS * PEER_TOPK
PLE_DIM = 256

LANES = 128
VMEM_LIMIT = 56 * 1024 * 1024
NEG = -1e30


def _cparams(*sem):
    return pltpu.CompilerParams(dimension_semantics=sem, vmem_limit_bytes=VMEM_LIMIT)


def _rms(x, g):
    ms = jnp.mean(x * x, axis=-1, keepdims=True)
    return x * lax.rsqrt(ms + EPS) * g


def _gelu(x):
    c = math.sqrt(2.0 / math.pi)
    return 0.5 * x * (1.0 + jnp.tanh(c * (x + 0.044715 * (x * x * x))))


def _sigmoid(x):
    return 1.0 / (1.0 + jnp.exp(-x))


_ACTS = {"gelu": _gelu, "sigmoid": _sigmoid, "none": lambda z: z}


def _norm_matmul_kernel(x_ref, g_ref, w_ref, o_ref, h_ref, *, act):
    @pl.when(pl.program_id(1) == 0)
    def _():
        h_ref[...] = _rms(x_ref[...], g_ref[...]).astype(BF16)

    z = jnp.dot(h_ref[...], w_ref[...], preferred_element_type=F32)
    o_ref[...] = _ACTS[act](z).astype(o_ref.dtype)


def norm_matmul(x, g, w, act, out_dtype, tn):
    t, d = x.shape
    n = w.shape[1]
    tm = min(t, 512)
    return pl.pallas_call(
        functools.partial(_norm_matmul_kernel, act=act),
        grid=(t // tm, n // tn),
        in_specs=[
            pl.BlockSpec((tm, d), lambda i, j: (i, 0)),
            pl.BlockSpec((1, d), lambda i, j: (0, 0)),
            pl.BlockSpec((d, tn), lambda i, j: (0, j)),
        ],
        out_specs=pl.BlockSpec((tm, tn), lambda i, j: (i, j)),
        out_shape=jax.ShapeDtypeStruct((t, n), out_dtype),
        scratch_shapes=[pltpu.VMEM((tm, d), BF16)],
        compiler_params=_cparams("parallel", "arbitrary"),
        name="norm_matmul_" + act,
    )(x, g.reshape(1, d), w)


def _gmlp_kernel(uv_ref, lng_ref, lnb_ref, ws_ref, bst_ref, y_ref, vn_ref, *, lc):
    u = uv_ref[:, :BW]
    v = uv_ref[:, BW:]
    mu = jnp.mean(v, axis=-1, keepdims=True)
    vc = v - mu
    var = jnp.mean(vc * vc, axis=-1, keepdims=True)
    vn = vc * lax.rsqrt(var + EPS) * lng_ref[...] + lnb_ref[...]
    vn_ref[...] = vn
    row = lax.broadcasted_iota(jnp.int32, (lc, lc), 0) // CHUNK
    col = lax.broadcasted_iota(jnp.int32, (lc, lc), 1) // CHUNK
    causal = col <= row
    vnb = vn.astype(BF16)
    for g in range(GROUPS):
        w = jnp.where(causal, ws_ref[g], 0.0).astype(BF16)
        s = jnp.dot(w, vnb[:, g * GDIM:(g + 1) * GDIM], preferred_element_type=F32)
        s = s + bst_ref[:, g:g + 1]
        y_ref[:, g * GDIM:(g + 1) * GDIM] = (u[:, g * GDIM:(g + 1) * GDIM] * s).astype(y_ref.dtype)


def gmlp_mix(uv, ln_g, ln_b, ws, bs, lc):
    t = uv.shape[0]
    return pl.pallas_call(
        functools.partial(_gmlp_kernel, lc=lc),
        grid=(t // lc,),
        in_specs=[
            pl.BlockSpec((lc, 2 * BW), lambda i: (i, 0)),
            pl.BlockSpec((1, BW), lambda i: (0, 0)),
            pl.BlockSpec((1, BW), lambda i: (0, 0)),
            pl.BlockSpec((GROUPS, lc, lc), lambda i: (0, 0, 0)),
            pl.BlockSpec((lc, GROUPS), lambda i: (0, 0)),
        ],
        out_specs=[
            pl.BlockSpec((lc, BW), lambda i: (i, 0)),
            pl.BlockSpec((lc, BW), lambda i: (i, 0)),
        ],
        out_shape=[
            jax.ShapeDtypeStruct((t, BW), BF16),
            jax.ShapeDtypeStruct((t, BW), F32),
        ],
        compiler_params=_cparams("parallel"),
        name="gmlp_mix",
    )(uv, ln_g.reshape(1, BW), ln_b.reshape(1, BW), ws[:, :lc, :lc], bs[:, :lc].T)


def _pool_kernel(u_ref, pre_ref, w_ref, sc_ref, y_ref, st_ref, pad_ref, *, seq, start_pos):
    pad_ref[0:POOL_PAD, :] = pre_ref[...]
    pad_ref[POOL_PAD:, :] = u_ref[...]
    pos = lax.broadcasted_iota(jnp.int32, (seq, 1), 0) + start_pos
    for g, win in enumerate(POOL_WINDOWS):
        cols = slice(g * GDIM, (g + 1) * GDIM)
        tok = pad_ref[POOL_PAD:, cols]
        acc = tok
        for k in range(1, win):
            acc = acc + pad_ref[POOL_PAD - k:POOL_PAD - k + seq, cols]
        cnt = jnp.minimum(pos + 1, win).astype(F32)
        d = acc / cnt - tok
        y = jnp.dot(d.astype(BF16), w_ref[g], preferred_element_type=F32)
        y_ref[:, cols] = (y * sc_ref[:, cols]).astype(y_ref.dtype)
    st_ref[...] = pad_ref[seq + 1:seq + POOL_PAD, :]


def pool_mix(mid, prefix, start_pos, pool_w, pool_scale, batch, seq):
    mid3 = mid.reshape(batch, seq, 4 * BW)
    pre = jnp.concatenate([jnp.zeros((batch, 1, BW), F32), prefix], axis=1)
    y, st = pl.pallas_call(
        functools.partial(_pool_kernel, seq=seq, start_pos=start_pos),
        grid=(batch,),
        in_specs=[
            pl.BlockSpec((None, seq, BW), lambda b: (b, 0, 0)),
            pl.BlockSpec((None, POOL_PAD, BW), lambda b: (b, 0, 0)),
            pl.BlockSpec((GROUPS, GDIM, GDIM), lambda b: (0, 0, 0)),
            pl.BlockSpec((1, BW), lambda b: (0, 0)),
        ],
        out_specs=[
            pl.BlockSpec((None, seq, BW), lambda b: (b, 0, 0)),
            pl.BlockSpec((None, POOL_STATE, BW), lambda b: (b, 0, 0)),
        ],
        out_shape=[
            jax.ShapeDtypeStruct((batch, seq, BW), BF16),
            jax.ShapeDtypeStruct((batch, POOL_STATE, BW), F32),
        ],
        scratch_shapes=[pltpu.VMEM((seq + POOL_PAD, BW), F32)],
        compiler_params=_cparams("parallel"),
        name="pool_mix",
    )(mid3, pre, pool_w.astype(BF16), pool_scale.reshape(1, BW))
    return y.reshape(batch * seq, BW), st


def _attn_chunks(q_ref, kcat_ref, vcat_ref, bias_ref, o_ref, *, n_chunks, cq, band, first_block):
    scale = HEAD_DIM ** -0.5
    for h in range(HEADS):
        cols = slice(h * HEAD_DIM, (h + 1) * HEAD_DIM)
        bias = bias_ref[h]
        for i in range(n_chunks):
            q = (q_ref[i * cq:(i + 1) * cq, cols] * scale).astype(BF16)
            k = kcat_ref[i * cq:i * cq + band, cols]
            v = vcat_ref[i * cq:i * cq + band, cols]
            s = lax.dot_general(q, k, (((1,), (1,)), ((), ())), preferred_element_type=F32) + bias
            if first_block is not None:
                key = lax.broadcasted_iota(jnp.int32, (1, band), 1)
                s = jnp.where(key >= first_block * (BAND_PAST - i * cq), s, NEG)
            m = jnp.max(s, axis=-1, keepdims=True)
            p = jnp.exp(s - m)
            l = jnp.sum(p, axis=-1, keepdims=True)
            o = jnp.dot(p.astype(BF16), v, preferred_element_type=F32) / l
            o_ref[i * cq:(i + 1) * cq, cols] = o.astype(o_ref.dtype)


def _attn_prompt_kernel(q_ref, kp_ref, ko_ref, vp_ref, vo_ref, bias_ref, o_ref, kcat_ref, vcat_ref):
    kcat_ref[0:BAND_PAST, :] = kp_ref[...].astype(BF16)
    kcat_ref[BAND_PAST:, :] = ko_ref[...].astype(BF16)
    vcat_ref[0:BAND_PAST, :] = vp_ref[...].astype(BF16)
    vcat_ref[BAND_PAST:, :] = vo_ref[...].astype(BF16)
    _attn_chunks(q_ref, kcat_ref, vcat_ref, bias_ref, o_ref, n_chunks=BAND_CHUNKS, cq=CHUNK,
                 band=BAND_PAST + CHUNK, first_block=(pl.program_id(1) == 0).astype(jnp.int32))


def _rel_bias_tile(rel_bias, qpos, kpos):
    rel = np.clip(qpos[:, None] - kpos[None, :], -REL_CLIP, REL_CLIP) + REL_CLIP
    return rel_bias[:, rel]


def attn_prompt(mid, rel_bias, batch, seq):
    mid3 = mid.reshape(batch, seq, 4 * BW)
    blk = BAND_PAST
    bias = _rel_bias_tile(rel_bias, np.arange(CHUNK), np.arange(BAND_PAST + CHUNK) - BAND_PAST)
    prev = lambda b, j: jnp.maximum(j - 1, 0)
    y = pl.pallas_call(
        _attn_prompt_kernel,
        grid=(batch, seq // blk),
        in_specs=[
            pl.BlockSpec((None, blk, BW), lambda b, j: (b, j, 1)),
            pl.BlockSpec((None, blk, BW), lambda b, j: (b, prev(b, j), 2)),
            pl.BlockSpec((None, blk, BW), lambda b, j: (b, j, 2)),
            pl.BlockSpec((None, blk, BW), lambda b, j: (b, prev(b, j), 3)),
            pl.BlockSpec((None, blk, BW), lambda b, j: (b, j, 3)),
            pl.BlockSpec((HEADS, CHUNK, BAND_PAST + CHUNK), lambda b, j: (0, 0, 0)),
        ],
        out_specs=pl.BlockSpec((None, blk, BW), lambda b, j: (b, j, 0)),
        out_shape=jax.ShapeDtypeStruct((batch, seq, BW), BF16),
        scratch_shapes=[pltpu.VMEM((2 * blk, BW), BF16), pltpu.VMEM((2 * blk, BW), BF16)],
        compiler_params=_cparams("parallel", "parallel"),
        name="attn_prompt",
    )(mid3, mid3, mid3, mid3, mid3, bias)
    return y.reshape(batch * seq, BW)


def _attn_sample_kernel(q_ref, kc_ref, kn_ref, vc_ref, vn_ref, bias_ref, o_ref, kcat_ref, vcat_ref, *, n_cache):
    kcat_ref[0:n_cache, :] = kc_ref[...].astype(BF16)
    kcat_ref[n_cache:, :] = kn_ref[...].astype(BF16)
    vcat_ref[0:n_cache, :] = vc_ref[...].astype(BF16)
    vcat_ref[n_cache:, :] = vn_ref[...].astype(BF16)
    seq = q_ref.shape[0]
    _attn_chunks(q_ref, kcat_ref, vcat_ref, bias_ref, o_ref, n_chunks=1, cq=seq, band=n_cache + seq,
                 first_block=None)


def attn_sample(mid, cache_k, cache_v, rel_bias, batch, seq):
    n_cache = cache_k.shape[1]
    assert PAST_LEN >= n_cache
    mid3 = mid.reshape(batch, seq, 4 * BW)
    ck = cache_k.reshape(batch, n_cache, BW)
    cv = cache_v.reshape(batch, n_cache, BW)
    bias = _rel_bias_tile(rel_bias, PAST_LEN + np.arange(seq), PAST_LEN - n_cache + np.arange(n_cache + seq))
    y = pl.pallas_call(
        functools.partial(_attn_sample_kernel, n_cache=n_cache),
        grid=(batch,),
        in_specs=[
            pl.BlockSpec((None, seq, BW), lambda b: (b, 0, 1)),
            pl.BlockSpec((None, n_cache, BW), lambda b: (b, 0, 0)),
            pl.BlockSpec((None, seq, BW), lambda b: (b, 0, 2)),
            pl.BlockSpec((None, n_cache, BW), lambda b: (b, 0, 0)),
            pl.BlockSpec((None, seq, BW), lambda b: (b, 0, 3)),
            pl.BlockSpec((HEADS, seq, n_cache + seq), lambda b: (0, 0, 0)),
        ],
        out_specs=pl.BlockSpec((None, seq, BW), lambda b: (b, 0, 0)),
        out_shape=jax.ShapeDtypeStruct((batch, seq, BW), BF16),
        scratch_shapes=[pltpu.VMEM((n_cache + seq, BW), BF16), pltpu.VMEM((n_cache + seq, BW), BF16)],
        compiler_params=_cparams("parallel"),
        name="attn_sample",
    )(mid3, ck, mid3, cv, mid3, bias)
    return y.reshape(batch * seq, BW)


def _mixer_out_kernel(x_ref, ya_ref, yb_ref, yc_ref, gate_ref, wb_ref, wo_ref, o_ref):
    acc = None
    for n, y_ref in enumerate((ya_ref, yb_ref, yc_ref)):
        proj = jnp.dot(y_ref[...], wb_ref[n], preferred_element_type=F32)
        term = gate_ref[:, n * D_MODEL:(n + 1) * D_MODEL].astype(F32) * proj
        acc = term if acc is None else acc + term
    o_ref[...] = x_ref[...] + jnp.dot(acc.astype(BF16), wo_ref[...], preferred_element_type=F32)


def mixer_out(x, ya, yb, yc, gates, w_branch, w_out):
    t = x.shape[0]
    tm = min(t, 512)
    row = lambda i: (i, 0)
    return pl.pallas_call(
        _mixer_out_kernel,
        grid=(t // tm,),
        in_specs=[
            pl.BlockSpec((tm, D_MODEL), row),
            pl.BlockSpec((tm, BW), row),
            pl.BlockSpec((tm, BW), row),
            pl.BlockSpec((tm, BW), row),
            pl.BlockSpec((tm, 3 * D_MODEL), row),
            pl.BlockSpec((3, BW, D_MODEL), lambda i: (0, 0, 0)),
            pl.BlockSpec((D_MODEL, D_MODEL), lambda i: (0, 0)),
        ],
        out_specs=pl.BlockSpec((tm, D_MODEL), row),
        out_shape=jax.ShapeDtypeStruct((t, D_MODEL), F32),
        compiler_params=_cparams("parallel"),
        name="mixer_out",
    )(x, ya, yb, yc, gates, w_branch, w_out)


def _extract_top(s, payload, k):
    r = float(s.shape[0])
    rows = lax.broadcasted_iota(jnp.int32, s.shape, 0).astype(F32)
    vals, pays = [], []
    for _ in range(k):
        m = jnp.max(s, axis=0, keepdims=True)
        idx = jnp.min(jnp.where(s == m, rows, r), axis=0, keepdims=True)
        sel = rows == idx
        vals.append(m)
        pays.append(idx if payload is None else jnp.max(jnp.where(sel, payload, -1.0), axis=0, keepdims=True))
        s = jnp.where(sel, -jnp.inf, s)
    return jnp.concatenate(vals, axis=0), jnp.concatenate(pays, axis=0)


def _peer_query_kernel(x_ref, g_ref, wq_ref, sk_ref, h_ref, idx_ref, gate_ref, q_ref, *, tm):
    h = _rms(x_ref[...], g_ref[...])
    h_ref[...] = h
    q_ref[...] = jnp.dot(h.astype(BF16), wq_ref[...], preferred_element_type=F32).astype(BF16)

    def sub_block(sb, carry):
        tok = pl.ds(pl.multiple_of(sb * LANES, LANES), LANES)
        for hd in range(PEER_HEADS):
            sv, si = [], []
            for p in range(2):
                hp = hd * 2 + p
                q = q_ref[tok, hp * PEER_HALF:(hp + 1) * PEER_HALF]
                s = lax.dot_general(sk_ref[hp], q, (((1,), (1,)), ((), ())), preferred_element_type=F32)
                v, i = _extract_top(s, None, PEER_TOPK)
                sv.append(v)
                si.append(i)
            cand = jnp.concatenate([sv[0][a:a + 1] + sv[1] for a in range(PEER_TOPK)], axis=0)
            eid = jnp.concatenate([si[0][a:a + 1] * PEER_NKEYS + si[1] for a in range(PEER_TOPK)], axis=0)
            tv, te = _extract_top(cand, eid, PEER_TOPK)
            e = jnp.exp(tv - tv[0:1])
            gate = e / jnp.sum(e, axis=0, keepdims=True)
            idx_ref[sb, hd * PEER_TOPK:(hd + 1) * PEER_TOPK, :] = te.astype(jnp.int32)
            gate_ref[sb, hd * PEER_TOPK:(hd + 1) * PEER_TOPK, :] = gate
        return carry

    lax.fori_loop(0, tm // LANES, sub_block, 0)


def peer_query(x, norm_g, wq, subkeys):
    t = x.shape[0]
    tm = min(t, 512)
    nq = wq.shape[1]
    nsb = tm // LANES
    return pl.pallas_call(
        functools.partial(_peer_query_kernel, tm=tm),
        grid=(t // tm,),
        in_specs=[
            pl.BlockSpec((tm, D_MODEL), lambda i: (i, 0)),
            pl.BlockSpec((1, D_MODEL), lambda i: (0, 0)),
            pl.BlockSpec((D_MODEL, nq), lambda i: (0, 0)),
            pl.BlockSpec((2 * PEER_HEADS, PEER_NKEYS, PEER_HALF), lambda i: (0, 0, 0)),
        ],
        out_specs=[
            pl.BlockSpec((tm, D_MODEL), lambda i: (i, 0)),
            pl.BlockSpec((nsb, PEER_SLOTS, LANES), lambda i: (i, 0, 0)),
            pl.BlockSpec((nsb, PEER_SLOTS, LANES), lambda i: (i, 0, 0)),
        ],
        out_shape=[
            jax.ShapeDtypeStruct((t, D_MODEL), F32),
            jax.ShapeDtypeStruct((t // LANES, PEER_SLOTS, LANES), jnp.int32),
            jax.ShapeDtypeStruct((t // LANES, PEER_SLOTS, LANES), F32),
        ],
        scratch_shapes=[pltpu.VMEM((tm, nq), BF16)],
        compiler_params=_cparams("parallel"),
        name="peer_query",
    )(x, norm_g.reshape(1, D_MODEL), wq, subkeys)


SC_WINDOW = 16
SC_BUFFERS = 4


def sc_gather_rows(table, idx):
    n = idx.shape[0]
    width = table.shape[1]
    info = plsc.get_sparse_core_info()
    n_workers = info.num_cores * info.num_subcores
    rpw = n // n_workers
    n_win = rpw // SC_WINDOW
    assert n % (n_workers * SC_WINDOW * SC_BUFFERS) == 0
    mesh = plsc.VectorSubcoreMesh(core_axis_name="core", subcore_axis_name="subcore")

    @functools.partial(
        pl.kernel,
        out_type=jax.ShapeDtypeStruct((n, width), table.dtype),
        mesh=mesh,
        scratch_types=[
            pltpu.VMEM((rpw,), jnp.int32),
            pltpu.VMEM((SC_BUFFERS, SC_WINDOW, width), table.dtype),
            pltpu.SemaphoreType.DMA((SC_BUFFERS,)),
            pltpu.SemaphoreType.DMA((SC_BUFFERS,)),
        ],
        name="peer_sc_gather",
    )
    def gather_kernel(table_hbm, idx_hbm, out_hbm, idx_v, rows_v, gsem, wsem):
        wid = lax.axis_index("subcore") * info.num_cores + lax.axis_index("core")
        base = wid * rpw
        pltpu.sync_copy(idx_hbm.at[pl.ds(base, rpw)], idx_v)

        def gather(w, b):
            return pltpu.make_async_copy(table_hbm.at[idx_v.at[pl.ds(w * SC_WINDOW, SC_WINDOW)]], rows_v.at[b],
                                         gsem.at[b])

        def write(w, b):
            return pltpu.make_async_copy(rows_v.at[b], out_hbm.at[pl.ds(base + w * SC_WINDOW, SC_WINDOW)],
                                         wsem.at[b])

        gather(0, 0).start()

        @pl.loop(0, n_win, step=SC_BUFFERS)
        def _(w0):
            for b in range(SC_BUFFERS):
                w = w0 + b
                nb = (b + 1) % SC_BUFFERS

                @pl.when(w + 1 >= SC_BUFFERS)
                def _():
                    write(w + 1 - SC_BUFFERS, nb).wait()

                @pl.when(w + 1 < n_win)
                def _():
                    gather(w + 1, nb).start()

                gather(w, b).wait()
                write(w, b).start()

        for k in range(1, SC_BUFFERS):
            write(n_win - k, (n_win - k) % SC_BUFFERS).wait()

    return gather_kernel(table, idx)


def pack_expert_tables(peer_u, peer_v):
    ub = lax.bitcast_convert_type(peer_u.astype(BF16), jnp.uint16).astype(jnp.uint32)
    vb = lax.bitcast_convert_type(peer_v.astype(BF16), jnp.uint16).astype(jnp.uint32)
    return lax.bitcast_convert_type((ub << 16) | vb, jnp.int32)


def _peer_expert_kernel(rows_ref, h_ref, gate_ref, x_ref, o_ref, act_ref):
    h = h_ref[...]
    hi_mask = jnp.int32(-65536)
    for e in range(PEER_SLOTS):
        u = lax.bitcast_convert_type(rows_ref[e] & hi_mask, F32)
        act_ref[:, e:e + 1] = jnp.sum(u * h, axis=-1, keepdims=True)
    act_ref[...] = gate_ref[...] * _gelu(act_ref[...])
    acc = x_ref[...]
    for e in range(PEER_SLOTS):
        v = lax.bitcast_convert_type(rows_ref[e] << 16, F32)
        acc = acc + act_ref[:, e:e + 1] * v
    o_ref[...] = acc


def peer_experts(rows, h, gates, x, tb):
    t = x.shape[0]
    return pl.pallas_call(
        _peer_expert_kernel,
        grid=(t // tb,),
        in_specs=[
            pl.BlockSpec((PEER_SLOTS, tb, D_MODEL), lambda i: (0, i, 0)),
            pl.BlockSpec((tb, D_MODEL), lambda i: (i, 0)),
            pl.BlockSpec((tb, PEER_SLOTS), lambda i: (i, 0)),
            pl.BlockSpec((tb, D_MODEL), lambda i: (i, 0)),
        ],
        out_specs=pl.BlockSpec((tb, D_MODEL), lambda i: (i, 0)),
        out_shape=jax.ShapeDtypeStruct((t, D_MODEL), F32),
        scratch_shapes=[pltpu.VMEM((tb, PEER_SLOTS), F32)],
        compiler_params=_cparams("parallel"),
        name="peer_experts",
    )(rows, h, gates, x)


PEER_TOKEN_CHUNK = 2048
PEER_TOKEN_BLOCK = 16


def peer_block(x, norm_g, wq, subkeys, table):
    t = x.shape[0]
    h, idx3, gate3 = peer_query(x, norm_g, wq, subkeys)
    gates = jnp.transpose(gate3, (0, 2, 1)).reshape(t, PEER_SLOTS)
    tc = min(t, PEER_TOKEN_CHUNK)
    outs = []
    for c in range(t // tc):
        tok = slice(c * tc, (c + 1) * tc)
        idx_c = idx3[c * tc // LANES:(c + 1) * tc // LANES]
        idx_c = jnp.transpose(idx_c, (1, 0, 2)).reshape(PEER_SLOTS * tc)
        rows = sc_gather_rows(table, idx_c).reshape(PEER_SLOTS, tc, D_MODEL)
        outs.append(peer_experts(rows, h[tok], gates[tok], x[tok], PEER_TOKEN_BLOCK))
    return outs[0] if len(outs) == 1 else jnp.concatenate(outs, axis=0)


def _ple_kernel(x_ref, p_ref, g_ref, wg_ref, wp_ref, gf_ref, o_ref, *, final):
    x = x_ref[...]
    gate = _sigmoid(jnp.dot(_rms(x, g_ref[...]).astype(BF16), wg_ref[...], preferred_element_type=F32))
    emb = jnp.dot(p_ref[...].astype(BF16), wp_ref[...], preferred_element_type=F32)
    y = x + gate * emb
    o_ref[...] = _rms(y, gf_ref[...]) if final else y


def ple_block(x, p, norm_g, w_gate, w_proj, norm_final, final):
    t = x.shape[0]
    tm = min(t, 512)
    row = lambda i: (i, 0)
    full = lambda i: (0, 0)
    return pl.pallas_call(
        functools.partial(_ple_kernel, final=final),
        grid=(t // tm,),
        in_specs=[
            pl.BlockSpec((tm, D_MODEL), row),
            pl.BlockSpec((tm, PLE_DIM), row),
            pl.BlockSpec((1, D_MODEL), full),
            pl.BlockSpec((D_MODEL, D_MODEL), full),
            pl.BlockSpec((PLE_DIM, D_MODEL), full),
            pl.BlockSpec((1, D_MODEL), full),
        ],
        out_specs=pl.BlockSpec((tm, D_MODEL), row),
        out_shape=jax.ShapeDtypeStruct((t, D_MODEL), F32),
        compiler_params=_cparams("parallel"),
        name="ple_block",
    )(x, p, norm_g.reshape(1, D_MODEL), w_gate, w_proj, norm_final.reshape(1, D_MODEL))


def _trunk_layer(x, ple, lw, batch, seq, pool_prefix, pool_start, cache, final):
    uv = norm_matmul(x, lw["norm_mix"], lw["w_uv"], "gelu", F32, 2 * BW)
    mid = norm_matmul(x, lw["norm_mix"], lw["w_mid"], "none", F32, 4 * BW)
    gates = norm_matmul(x, lw["norm_mix"], lw["w_gates"], "sigmoid", BF16, D_MODEL)
    ya, vn = gmlp_mix(uv, lw["gmlp_ln_g"], lw["gmlp_ln_b"], lw["gmlp_ws"], lw["gmlp_bs"], min(seq, GMLP_CHUNK))
    yb, pool_state = pool_mix(mid, pool_prefix, pool_start, lw["pool_w"], lw["pool_scale"], batch, seq)
    if cache is None:
        yc = attn_prompt(mid, lw["rel_bias"], batch, seq)
    else:
        yc = attn_sample(mid, cache[0], cache[1], lw["rel_bias"], batch, seq)
    x = mixer_out(x, ya, yb, yc, gates, lw["w_branch"], lw["w_out"])
    x = peer_block(x, lw["norm_ffn"], lw["peer_wq"], lw["peer_subkeys"], lw["peer_table"])
    x = ple_block(x, ple, lw["norm_ple"], lw["ple_gate"], lw["ple_proj"], lw["norm_final"], final)
    return x, mid, pool_state, vn


def kernel(x_prompt, x_sample, cache_attn_k, cache_attn_v, state_pool, p_prompt, p_sample, norm_mix, w_in, gmlp_ln_g, gmlp_ln_b, gmlp_ws, gmlp_bs, pool_w, pool_scale, attn_rel_bias, w_branch, w_out, norm_ffn, peer_wq, peer_subkeys, peer_u, peer_v, norm_ple, ple_gate, ple_proj, norm_final):
    bp, lp, _ = x_prompt.shape
    bs, ls, _ = x_sample.shape
    assert lp % BAND_PAST == 0 and lp % GMLP_CHUNK == 0 and ls <= CHUNK
    n_keep = min(BAND_PAST, lp)
    xp = x_prompt.reshape(bp * lp, D_MODEL)
    xs = x_sample.reshape(bs * ls, D_MODEL)
    zero_prefix = jnp.zeros((bp, POOL_STATE, BW), F32)
    outs = {k: [] for k in ("pk", "pv", "pps", "sk", "sv", "sps", "sgv")}
    for i in range(DEPTH):
        w_in_b = w_in[i].astype(BF16)
        lw = dict(
            norm_mix=norm_mix[i],
            w_uv=w_in_b[:, :2 * BW],
            w_mid=w_in_b[:, 2 * BW:6 * BW],
            w_gates=w_in_b[:, 6 * BW:],
            gmlp_ln_g=gmlp_ln_g[i], gmlp_ln_b=gmlp_ln_b[i], gmlp_ws=gmlp_ws[i], gmlp_bs=gmlp_bs[i],
            pool_w=pool_w[i], pool_scale=pool_scale[i], rel_bias=attn_rel_bias[i],
            w_branch=w_branch[i].astype(BF16), w_out=w_out[i].astype(BF16),
            norm_ffn=norm_ffn[i], peer_wq=peer_wq[i].astype(BF16),
            peer_subkeys=peer_subkeys[i].reshape(2 * PEER_HEADS, PEER_NKEYS, PEER_HALF).astype(BF16),
            peer_table=pack_expert_tables(peer_u[i], peer_v[i]),
            norm_ple=norm_ple[i], ple_gate=ple_gate[i].astype(BF16), ple_proj=ple_proj[i].astype(BF16),
            norm_final=norm_final,
        )
        final = i == DEPTH - 1
        xp, mid_p, ps_p, _ = _trunk_layer(xp, p_prompt[i].reshape(bp * lp, PLE_DIM), lw, bp, lp, zero_prefix, 0,
                                          None, final)
        mid_p = mid_p.reshape(bp, lp, 4 * BW)
        outs["pk"].append(mid_p[:, lp - n_keep:, 2 * BW:3 * BW].reshape(bp, n_keep, HEADS, HEAD_DIM))
        outs["pv"].append(mid_p[:, lp - n_keep:, 3 * BW:].reshape(bp, n_keep, HEADS, HEAD_DIM))
        outs["pps"].append(ps_p)
        xs, mid_s, ps_s, vn_s = _trunk_layer(xs, p_sample[i].reshape(bs * ls, PLE_DIM), lw, bs, ls, state_pool[i],
                                             PAST_LEN, (cache_attn_k[i], cache_attn_v[i]), final)
        mid_s = mid_s.reshape(bs, ls, 4 * BW)
        outs["sk"].append(mid_s[:, :, 2 * BW:3 * BW].reshape(bs, ls, HEADS, HEAD_DIM))
        outs["sv"].append(mid_s[:, :, 3 * BW:].reshape(bs, ls, HEADS, HEAD_DIM))
        outs["sps"].append(ps_s)
        outs["sgv"].append(vn_s.reshape(bs, ls, BW))
    st = lambda k: jnp.stack(outs[k])
    return (xp.reshape(bp, lp, D_MODEL), xs.reshape(bs, ls, D_MODEL), st("pk"), st("pv"), st("pps"),
            st("sk"), st("sv"), st("sps"), st("sgv"))
```

```python
import functools
import math

import jax
import jax.numpy as jnp
import numpy as np
from jax import lax
from jax.experimental import pallas as pl
from jax.experimental.pallas import tpu as pltpu
from jax.experimental.pallas import tpu_sc as plsc

F32 = jnp.float32
BF16 = jnp.bfloat16

D_MODEL = 1024
DEPTH = 2
CHUNK = 64
EPS = 1e-6
BW = D_MODEL // 2
GMLP_CHUNK = 128
GROUPS = 4
GDIM = BW // GROUPS
POOL_WINDOWS = (2, 4, 8, 16)
POOL_STATE = 15
POOL_PAD = 16
HEADS = 8
HEAD_DIM = BW // HEADS
BAND_CHUNKS = 8
BAND_PAST = BAND_CHUNKS * CHUNK
REL_CLIP = 128
PAST_LEN = 4096
PEER_HEADS = 8
PEER_NKEYS = 128
PEER_HALF = 128
PEER_TOPK = 16
PEER_SLOTS = PEER_HEADS * PEER_TOPK
PLE_DIM = 256

LANES = 128
VMEM_LIMIT = 56 * 1024 * 1024
NEG = -1e30


def _cparams(*sem):
    return pltpu.CompilerParams(dimension_semantics=sem, vmem_limit_bytes=VMEM_LIMIT)


def _rms(x, g):
    ms = jnp.mean(x * x, axis=-1, keepdims=True)
    return x * lax.rsqrt(ms + EPS) * g


def _gelu(x):
    c = math.sqrt(2.0 / math.pi)
    return 0.5 * x * (1.0 + jnp.tanh(c * (x + 0.044715 * (x * x * x))))


def _sigmoid(x):
    return 1.0 / (1.0 + jnp.exp(-x))


_ACTS = {"gelu": _gelu, "sigmoid": _sigmoid, "none": lambda z: z}


def _norm_matmul_kernel(x_ref, g_ref, w_ref, o_ref, h_ref, *, act):
    @pl.when(pl.program_id(1) == 0)
    def _():
        h_ref[...] = _rms(x_ref[...], g_ref[...]).astype(BF16)

    z = jnp.dot(h_ref[...], w_ref[...], preferred_element_type=F32)
    o_ref[...] = _ACTS[act](z).astype(o_ref.dtype)


def norm_matmul(x, g, w, act, out_dtype, tn):
    t, d = x.shape
    n = w.shape[1]
    tm = min(t, 512)
    return pl.pallas_call(
        functools.partial(_norm_matmul_kernel, act=act),
        grid=(t // tm, n // tn),
        in_specs=[
            pl.BlockSpec((tm, d), lambda i, j: (i, 0)),
            pl.BlockSpec((1, d), lambda i, j: (0, 0)),
            pl.BlockSpec((d, tn), lambda i, j: (0, j)),
        ],
        out_specs=pl.BlockSpec((tm, tn), lambda i, j: (i, j)),
        out_shape=jax.ShapeDtypeStruct((t, n), out_dtype),
        scratch_shapes=[pltpu.VMEM((tm, d), BF16)],
        compiler_params=_cparams("parallel", "arbitrary"),
        name="norm_matmul_" + act,
    )(x, g.reshape(1, d), w)


def _gmlp_kernel(uv_ref, lng_ref, lnb_ref, ws_ref, bst_ref, y_ref, vn_ref, *, lc):
    u = uv_ref[:, :BW]
    v = uv_ref[:, BW:]
    mu = jnp.mean(v, axis=-1, keepdims=True)
    vc = v - mu
    var = jnp.mean(vc * vc, axis=-1, keepdims=True)
    vn = vc * lax.rsqrt(var + EPS) * lng_ref[...] + lnb_ref[...]
    vn_ref[...] = vn
    row = lax.broadcasted_iota(jnp.int32, (lc, lc), 0) // CHUNK
    col = lax.broadcasted_iota(jnp.int32, (lc, lc), 1) // CHUNK
    causal = col <= row
    vnb = vn.astype(BF16)
    for g in range(GROUPS):
        w = jnp.where(causal, ws_ref[g], 0.0).astype(BF16)
        s = jnp.dot(w, vnb[:, g * GDIM:(g + 1) * GDIM], preferred_element_type=F32)
        s = s + bst_ref[:, g:g + 1]
        y_ref[:, g * GDIM:(g + 1) * GDIM] = (u[:, g * GDIM:(g + 1) * GDIM] * s).astype(y_ref.dtype)


def gmlp_mix(uv, ln_g, ln_b, ws, bs, lc):
    t = uv.shape[0]
    return pl.pallas_call(
        functools.partial(_gmlp_kernel, lc=lc),
        grid=(t // lc,),
        in_specs=[
            pl.BlockSpec((lc, 2 * BW), lambda i: (i, 0)),
            pl.BlockSpec((1, BW), lambda i: (0, 0)),
            pl.BlockSpec((1, BW), lambda i: (0, 0)),
            pl.BlockSpec((GROUPS, lc, lc), lambda i: (0, 0, 0)),
            pl.BlockSpec((lc, GROUPS), lambda i: (0, 0)),
        ],
        out_specs=[
            pl.BlockSpec((lc, BW), lambda i: (i, 0)),
            pl.BlockSpec((lc, BW), lambda i: (i, 0)),
        ],
        out_shape=[
            jax.ShapeDtypeStruct((t, BW), BF16),
            jax.ShapeDtypeStruct((t, BW), F32),
        ],
        compiler_params=_cparams("parallel"),
        name="gmlp_mix",
    )(uv, ln_g.reshape(1, BW), ln_b.reshape(1, BW), ws[:, :lc, :lc], bs[:, :lc].T)


def _pool_kernel(u_ref, pre_ref, w_ref, sc_ref, y_ref, st_ref, pad_ref, *, seq, start_pos):
    pad_ref[0:POOL_PAD, :] = pre_ref[...]
    pad_ref[POOL_PAD:, :] = u_ref[...]
    pos = lax.broadcasted_iota(jnp.int32, (seq, 1), 0) + start_pos
    for g, win in enumerate(POOL_WINDOWS):
        cols = slice(g * GDIM, (g + 1) * GDIM)
        tok = pad_ref[POOL_PAD:, cols]
        acc = tok
        for k in range(1, win):
            acc = acc + pad_ref[POOL_PAD - k:POOL_PAD - k + seq, cols]
        cnt = jnp.minimum(pos + 1, win).astype(F32)
        d = acc / cnt - tok
        y = jnp.dot(d.astype(BF16), w_ref[g], preferred_element_type=F32)
        y_ref[:, cols] = (y * sc_ref[:, cols]).astype(y_ref.dtype)
    st_ref[...] = pad_ref[seq + 1:seq + POOL_PAD, :]


def pool_mix(mid, prefix, start_pos, pool_w, pool_scale, batch, seq):
    mid3 = mid.reshape(batch, seq, 4 * BW)
    pre = jnp.concatenate([jnp.zeros((batch, 1, BW), F32), prefix], axis=1)
    y, st = pl.pallas_call(
        functools.partial(_pool_kernel, seq=seq, start_pos=start_pos),
        grid=(batch,),
        in_specs=[
            pl.BlockSpec((None, seq, BW), lambda b: (b, 0, 0)),
            pl.BlockSpec((None, POOL_PAD, BW), lambda b: (b, 0, 0)),
            pl.BlockSpec((GROUPS, GDIM, GDIM), lambda b: (0, 0, 0)),
            pl.BlockSpec((1, BW), lambda b: (0, 0)),
        ],
        out_specs=[
            pl.BlockSpec((None, seq, BW), lambda b: (b, 0, 0)),
            pl.BlockSpec((None, POOL_STATE, BW), lambda b: (b, 0, 0)),
        ],
        out_shape=[
            jax.ShapeDtypeStruct((batch, seq, BW), BF16),
            jax.ShapeDtypeStruct((batch, POOL_STATE, BW), F32),
        ],
        scratch_shapes=[pltpu.VMEM((seq + POOL_PAD, BW), F32)],
        compiler_params=_cparams("parallel"),
        name="pool_mix",
    )(mid3, pre, pool_w.astype(BF16), pool_scale.reshape(1, BW))
    return y.reshape(batch * seq, BW), st


def _attn_chunks(q_ref, kcat_ref, vcat_ref, bias_ref, o_ref, *, n_chunks, cq, band, first_block):
    scale = HEAD_DIM ** -0.5
    for h in range(HEADS):
        cols = slice(h * HEAD_DIM, (h + 1) * HEAD_DIM)
        bias = bias_ref[h]
        for i in range(n_chunks):
            q = (q_ref[i * cq:(i + 1) * cq, cols] * scale).astype(BF16)
            k = kcat_ref[i * cq:i * cq + band, cols]
            v = vcat_ref[i * cq:i * cq + band, cols]
            s = lax.dot_general(q, k, (((1,), (1,)), ((), ())), preferred_element_type=F32) + bias
            if first_block is not None:
                key = lax.broadcasted_iota(jnp.int32, (1, band), 1)
                s = jnp.where(key >= first_block * (BAND_PAST - i * cq), s, NEG)
            m = jnp.max(s, axis=-1, keepdims=True)
            p = jnp.exp(s - m)
            l = jnp.sum(p, axis=-1, keepdims=True)
            o = jnp.dot(p.astype(BF16), v, preferred_element_type=F32) / l
            o_ref[i * cq:(i + 1) * cq, cols] = o.astype(o_ref.dtype)


def _attn_prompt_kernel(q_ref, kp_ref, ko_ref, vp_ref, vo_ref, bias_ref, o_ref, kcat_ref, vcat_ref):
    kcat_ref[0:BAND_PAST, :] = kp_ref[...].astype(BF16)
    kcat_ref[BAND_PAST:, :] = ko_ref[...].astype(BF16)
    vcat_ref[0:BAND_PAST, :] = vp_ref[...].astype(BF16)
    vcat_ref[BAND_PAST:, :] = vo_ref[...].astype(BF16)
    _attn_chunks(q_ref, kcat_ref, vcat_ref, bias_ref, o_ref, n_chunks=BAND_CHUNKS, cq=CHUNK,
                 band=BAND_PAST + CHUNK, first_block=(pl.program_id(1) == 0).astype(jnp.int32))


def _rel_bias_tile(rel_bias, qpos, kpos):
    rel = np.clip(qpos[:, None] - kpos[None, :], -REL_CLIP, REL_CLIP) + REL_CLIP
    return rel_bias[:, rel]


def attn_prompt(mid, rel_bias, batch, seq):
    mid3 = mid.reshape(batch, seq, 4 * BW)
    blk = BAND_PAST
    bias = _rel_bias_tile(rel_bias, np.arange(CHUNK), np.arange(BAND_PAST + CHUNK) - BAND_PAST)
    prev = lambda b, j: jnp.maximum(j - 1, 0)
    y = pl.pallas_call(
        _attn_prompt_kernel,
        grid=(batch, seq // blk),
        in_specs=[
            pl.BlockSpec((None, blk, BW), lambda b, j: (b, j, 1)),
            pl.BlockSpec((None, blk, BW), lambda b, j: (b, prev(b, j), 2)),
            pl.BlockSpec((None, blk, BW), lambda b, j: (b, j, 2)),
            pl.BlockSpec((None, blk, BW), lambda b, j: (b, prev(b, j), 3)),
            pl.BlockSpec((None, blk, BW), lambda b, j: (b, j, 3)),
            pl.BlockSpec((HEADS, CHUNK, BAND_PAST + CHUNK), lambda b, j: (0, 0, 0)),
        ],
        out_specs=pl.BlockSpec((None, blk, BW), lambda b, j: (b, j, 0)),
        out_shape=jax.ShapeDtypeStruct((batch, seq, BW), BF16),
        scratch_shapes=[pltpu.VMEM((2 * blk, BW), BF16), pltpu.VMEM((2 * blk, BW), BF16)],
        compiler_params=_cparams("parallel", "parallel"),
        name="attn_prompt",
    )(mid3, mid3, mid3, mid3, mid3, bias)
    return y.reshape(batch * seq, BW)


def _attn_sample_kernel(q_ref, kc_ref, kn_ref, vc_ref, vn_ref, bias_ref, o_ref, kcat_ref, vcat_ref, *, n_cache):
    kcat_ref[0:n_cache, :] = kc_ref[...].astype(BF16)
    kcat_ref[n_cache:, :] = kn_ref[...].astype(BF16)
    vcat_ref[0:n_cache, :] = vc_ref[...].astype(BF16)
    vcat_ref[n_cache:, :] = vn_ref[...].astype(BF16)
    seq = q_ref.shape[0]
    _attn_chunks(q_ref, kcat_ref, vcat_ref, bias_ref, o_ref, n_chunks=1, cq=seq, band=n_cache + seq,
                 first_block=None)


def attn_sample(mid, cache_k, cache_v, rel_bias, batch, seq):
    n_cache = cache_k.shape[1]
    assert PAST_LEN >= n_cache
    mid3 = mid.reshape(batch, seq, 4 * BW)
    ck = cache_k.reshape(batch, n_cache, BW)
    cv = cache_v.reshape(batch, n_cache, BW)
    bias = _rel_bias_tile(rel_bias, PAST_LEN + np.arange(seq), PAST_LEN - n_cache + np.arange(n_cache + seq))
    y = pl.pallas_call(
        functools.partial(_attn_sample_kernel, n_cache=n_cache),
        grid=(batch,),
        in_specs=[
            pl.BlockSpec((None, seq, BW), lambda b: (b, 0, 1)),
            pl.BlockSpec((None, n_cache, BW), lambda b: (b, 0, 0)),
            pl.BlockSpec((None, seq, BW), lambda b: (b, 0, 2)),
            pl.BlockSpec((None, n_cache, BW), lambda b: (b, 0, 0)),
            pl.BlockSpec((None, seq, BW), lambda b: (b, 0, 3)),
            pl.BlockSpec((HEADS, seq, n_cache + seq), lambda b: (0, 0, 0)),
        ],
        out_specs=pl.BlockSpec((None, seq, BW), lambda b: (b, 0, 0)),
        out_shape=jax.ShapeDtypeStruct((batch, seq, BW), BF16),
        scratch_shapes=[pltpu.VMEM((n_cache + seq, BW), BF16), pltpu.VMEM((n_cache + seq, BW), BF16)],
        compiler_params=_cparams("parallel"),
        name="attn_sample",
    )(mid3, ck, mid3, cv, mid3, bias)
    return y.reshape(batch * seq, BW)


def _mixer_out_kernel(x_ref, ya_ref, yb_ref, yc_ref, gate_ref, wb_ref, wo_ref, o_ref):
    acc = None
    for n, y_ref in enumerate((ya_ref, yb_ref, yc_ref)):
        proj = jnp.dot(y_ref[...], wb_ref[n], preferred_element_type=F32)
        term = gate_ref[:, n * D_MODEL:(n + 1) * D_MODEL].astype(F32) * proj
        acc = term if acc is None else acc + term
    o_ref[...] = x_ref[...] + jnp.dot(acc.astype(BF16), wo_ref[...], preferred_element_type=F32)


def mixer_out(x, ya, yb, yc, gates, w_branch, w_out):
    t = x.shape[0]
    tm = min(t, 512)
    row = lambda i: (i, 0)
    return pl.pallas_call(
        _mixer_out_kernel,
        grid=(t // tm,),
        in_specs=[
            pl.BlockSpec((tm, D_MODEL), row),
            pl.BlockSpec((tm, BW), row),
            pl.BlockSpec((tm, BW), row),
            pl.BlockSpec((tm, BW), row),
            pl.BlockSpec((tm, 3 * D_MODEL), row),
            pl.BlockSpec((3, BW, D_MODEL), lambda i: (0, 0, 0)),
            pl.BlockSpec((D_MODEL, D_MODEL), lambda i: (0, 0)),
        ],
        out_specs=pl.BlockSpec((tm, D_MODEL), row),
        out_shape=jax.ShapeDtypeStruct((t, D_MODEL), F32),
        compiler_params=_cparams("parallel"),
        name="mixer_out",
    )(x, ya, yb, yc, gates, w_branch, w_out)


def _extract_top(s, payload, k):
    r = float(s.shape[0])
    rows = lax.broadcasted_iota(jnp.int32, s.shape, 0).astype(F32)
    vals, pays = [], []
    for _ in range(k):
        m = jnp.max(s, axis=0, keepdims=True)
        idx = jnp.min(jnp.where(s == m, rows, r), axis=0, keepdims=True)
        sel = rows == idx
        vals.append(m)
        pays.append(idx if payload is None else jnp.max(jnp.where(sel, payload, -1.0), axis=0, keepdims=True))
        s = jnp.where(sel, -jnp.inf, s)
    return jnp.concatenate(vals, axis=0), jnp.concatenate(pays, axis=0)


def _peer_query_kernel(x_ref, g_ref, wq_ref, sk_ref, h_ref, idx_ref, gate_ref, q_ref, *, tm):
    h = _rms(x_ref[...], g_ref[...])
    h_ref[...] = h
    q_ref[...] = jnp.dot(h.astype(BF16), wq_ref[...], preferred_element_type=F32).astype(BF16)

    def sub_block(sb, carry):
        tok = pl.ds(pl.multiple_of(sb * LANES, LANES), LANES)
        for hd in range(PEER_HEADS):
            sv, si = [], []
            for p in range(2):
                hp = hd * 2 + p
                q = q_ref[tok, hp * PEER_HALF:(hp + 1) * PEER_HALF]
                s = lax.dot_general(sk_ref[hp], q, (((1,), (1,)), ((), ())), preferred_element_type=F32)
                v, i = _extract_top(s, None, PEER_TOPK)
                sv.append(v)
                si.append(i)
            cand = jnp.concatenate([sv[0][a:a + 1] + sv[1] for a in range(PEER_TOPK)], axis=0)
            eid = jnp.concatenate([si[0][a:a + 1] * PEER_NKEYS + si[1] for a in range(PEER_TOPK)], axis=0)
            tv, te = _extract_top(cand, eid, PEER_TOPK)
            e = jnp.exp(tv - tv[0:1])
            gate = e / jnp.sum(e, axis=0, keepdims=True)
            idx_ref[sb, hd * PEER_TOPK:(hd + 1) * PEER_TOPK, :] = te.astype(jnp.int32)
            gate_ref[sb, hd * PEER_TOPK:(hd + 1) * PEER_TOPK, :] = gate
        return carry

    lax.fori_loop(0, tm // LANES, sub_block, 0)


def peer_query(x, norm_g, wq, subkeys):
    t = x.shape[0]
    tm = min(t, 512)
    nq = wq.shape[1]
    nsb = tm // LANES
    return pl.pallas_call(
        functools.partial(_peer_query_kernel, tm=tm),
        grid=(t // tm,),
        in_specs=[
            pl.BlockSpec((tm, D_MODEL), lambda i: (i, 0)),
            pl.BlockSpec((1, D_MODEL), lambda i: (0, 0)),
            pl.BlockSpec((D_MODEL, nq), lambda i: (0, 0)),
            pl.BlockSpec((2 * PEER_HEADS, PEER_NKEYS, PEER_HALF), lambda i: (0, 0, 0)),
        ],
        out_specs=[
            pl.BlockSpec((tm, D_MODEL), lambda i: (i, 0)),
            pl.BlockSpec((nsb, PEER_SLOTS, LANES), lambda i: (i, 0, 0)),
            pl.BlockSpec((nsb, PEER_SLOTS, LANES), lambda i: (i, 0, 0)),
        ],
        out_shape=[
            jax.ShapeDtypeStruct((t, D_MODEL), F32),
            jax.ShapeDtypeStruct((t // LANES, PEER_SLOTS, LANES), jnp.int32),
            jax.ShapeDtypeStruct((t // LANES, PEER_SLOTS, LANES), F32),
        ],
        scratch_shapes=[pltpu.VMEM((tm, nq), BF16)],
        compiler_params=_cparams("parallel"),
        name="peer_query",
    )(x, norm_g.reshape(1, D_MODEL), wq, subkeys)


HALF_D = D_MODEL // 2
SC_LANES = 16
SC_UNIT_ROWS = 32
SC_UNITS_PER_TOKEN = PEER_SLOTS // SC_UNIT_ROWS
SC_TOKEN_BLOCK = 8
SC_ROW_GROUP = 4
SC_HALF_VECS = HALF_D // SC_LANES
SC_OUT_VECS = 8


def pack_expert_tables(peer_u, peer_v):
    def pack(x):
        b = lax.bitcast_convert_type(x.astype(BF16), jnp.uint16).astype(jnp.uint32)
        return (b[:, HALF_D:] << 16) | b[:, :HALF_D]

    return lax.bitcast_convert_type(jnp.concatenate([pack(peer_u), pack(peer_v)], axis=1), jnp.int32)


def sc_peer_experts(table, idx, gates, h):
    t = h.shape[0]
    info = plsc.get_sparse_core_info()
    n_workers = info.num_cores * info.num_subcores
    tpw = t // n_workers
    assert t % (n_workers * SC_TOKEN_BLOCK) == 0
    units = SC_TOKEN_BLOCK * SC_UNITS_PER_TOKEN
    mesh = plsc.VectorSubcoreMesh(core_axis_name="core", subcore_axis_name="subcore")
    hi_mask = jnp.int32(-65536)
    gelu_c = math.sqrt(2.0 / math.pi)

    @functools.partial(
        pl.kernel,
        out_type=jax.ShapeDtypeStruct((t, D_MODEL), F32),
        mesh=mesh,
        scratch_types=[
            pltpu.VMEM((SC_TOKEN_BLOCK * PEER_SLOTS,), jnp.int32),
            pltpu.VMEM((SC_TOKEN_BLOCK * PEER_SLOTS,), F32),
            pltpu.VMEM((SC_TOKEN_BLOCK, D_MODEL), F32),
            pltpu.VMEM((SC_TOKEN_BLOCK, D_MODEL), F32),
            pltpu.VMEM((2, SC_UNIT_ROWS, D_MODEL), jnp.int32),
            pltpu.VMEM((SC_UNIT_ROWS, SC_LANES), F32),
            pltpu.VMEM((SC_UNIT_ROWS,), F32),
            pltpu.SemaphoreType.DMA((2,)),
        ],
        compiler_params=pltpu.CompilerParams(needs_layout_passes=False),
        name="peer_sc_experts",
    )
    def kern(table_hbm, idx_hbm, gate_hbm, h_hbm, out_hbm, idx_v, gate_v, h_v, out_v, rows_v, part_v, coef_v, sem):
        wid = lax.axis_index("subcore") * info.num_cores + lax.axis_index("core")
        lane = lax.iota(jnp.int32, SC_LANES)
        zero = jnp.zeros((SC_LANES,), F32)

        def gather(unit, b):
            rows = idx_v.at[pl.ds(unit * SC_UNIT_ROWS, SC_UNIT_ROWS)]
            return pltpu.make_async_copy(table_hbm.at[rows], rows_v.at[b], sem.at[b])

        def unpack(w):
            return lax.bitcast_convert_type(w << 16, F32), lax.bitcast_convert_type(w & hi_mask, F32)

        def compute(unit, b):
            tl = unit // SC_UNITS_PER_TOKEN
            q = unit % SC_UNITS_PER_TOKEN

            def row_group(rg, carry):
                def kstep(k, accs):
                    off = pl.multiple_of(k * SC_LANES, SC_LANES)
                    h_lo = h_v[tl, pl.ds(off, SC_LANES)]
                    h_hi = h_v[tl, pl.ds(HALF_D + off, SC_LANES)]
                    new = []
                    for r in range(SC_ROW_GROUP):
                        lo, hi = unpack(rows_v[b, rg * SC_ROW_GROUP + r, pl.ds(off, SC_LANES)])
                        new.append(accs[r] + lo * h_lo + hi * h_hi)
                    return tuple(new)

                accs = lax.fori_loop(0, SC_HALF_VECS, kstep, (zero,) * SC_ROW_GROUP, unroll=4)
                for r in range(SC_ROW_GROUP):
                    part_v[rg * SC_ROW_GROUP + r, :] = accs[r]
                return carry

            lax.fori_loop(0, SC_UNIT_ROWS // SC_ROW_GROUP, row_group, 0)

            for i in range(SC_UNIT_ROWS // SC_LANES):
                rows = lane + i * SC_LANES
                a = zero
                for l in range(SC_LANES):
                    a = a + plsc.load_gather(part_v, [rows, jnp.full((SC_LANES,), l, jnp.int32)])
                z = gelu_c * (a + 0.044715 * (a * a * a))
                act = a / (1.0 + jnp.exp(-2.0 * z))
                slot = pl.multiple_of(tl * PEER_SLOTS + q * SC_UNIT_ROWS + i * SC_LANES, SC_LANES)
                coef_v[pl.ds(i * SC_LANES, SC_LANES)] = gate_v[pl.ds(slot, SC_LANES)] * act

            def out_pass(dq, carry):
                def row(r, accs):
                    c = plsc.load_gather(coef_v, [jnp.full((SC_LANES,), r, jnp.int32)])
                    new = []
                    for k in range(SC_OUT_VECS):
                        off = pl.multiple_of((dq * SC_OUT_VECS + k) * SC_LANES, SC_LANES)
                        lo, hi = unpack(rows_v[b, r, pl.ds(HALF_D + off, SC_LANES)])
                        new.append(accs[2 * k] + c * lo)
                        new.append(accs[2 * k + 1] + c * hi)
                    return tuple(new)

                accs = lax.fori_loop(0, SC_UNIT_ROWS, row, (zero,) * (2 * SC_OUT_VECS))
                for k in range(SC_OUT_VECS):
                    off = pl.multiple_of((dq * SC_OUT_VECS + k) * SC_LANES, SC_LANES)
                    plsc.addupdate(out_v.at[tl, pl.ds(off, SC_LANES)], accs[2 * k])
                    plsc.addupdate(out_v.at[tl, pl.ds(HALF_D + off, SC_LANES)], accs[2 * k + 1])
                return carry

            lax.fori_loop(0, SC_HALF_VECS // SC_OUT_VECS, out_pass, 0)

        @pl.loop(0, tpw // SC_TOKEN_BLOCK)
        def _(blk):
            tok0 = wid * tpw + blk * SC_TOKEN_BLOCK
            slots = pl.ds(tok0 * PEER_SLOTS, SC_TOKEN_BLOCK * PEER_SLOTS)
            pltpu.sync_copy(idx_hbm.at[slots], idx_v)
            pltpu.sync_copy(gate_hbm.at[slots], gate_v)
            pltpu.sync_copy(h_hbm.at[pl.ds(tok0, SC_TOKEN_BLOCK)], h_v)

            @pl.loop(0, SC_TOKEN_BLOCK)
            def _(tl):
                @pl.loop(0, D_MODEL // SC_LANES)
                def _(k):
                    out_v[tl, pl.ds(pl.multiple_of(k * SC_LANES, SC_LANES), SC_LANES)] = zero

            gather(0, 0).start()

            @pl.loop(0, units, step=2)
            def _(u0):
                for b in range(2):
                    unit = u0 + b

                    @pl.when(unit + 1 < units)
                    def _():
                        gather(unit + 1, 1 - b).start()

                    gather(unit, b).wait()
                    compute(unit, b)

            pltpu.sync_copy(out_v, out_hbm.at[pl.ds(tok0, SC_TOKEN_BLOCK)])

    return kern(table, idx, gates, h)


def peer_block(x, norm_g, wq, subkeys, table):
    t = x.shape[0]
    h, idx3, gate3 = peer_query(x, norm_g, wq, subkeys)
    idx = jnp.transpose(idx3, (0, 2, 1)).reshape(t * PEER_SLOTS)
    gates = jnp.transpose(gate3, (0, 2, 1)).reshape(t * PEER_SLOTS)
    return sc_peer_experts(table, idx, gates, h)


def _ple_kernel(x_ref, ffn_ref, p_ref, g_ref, wg_ref, wp_ref, gf_ref, o_ref, *, final):
    x = x_ref[...] + ffn_ref[...]
    gate = _sigmoid(jnp.dot(_rms(x, g_ref[...]).astype(BF16), wg_ref[...], preferred_element_type=F32))
    emb = jnp.dot(p_ref[...].astype(BF16), wp_ref[...], preferred_element_type=F32)
    y = x + gate * emb
    o_ref[...] = _rms(y, gf_ref[...]) if final else y


def ple_block(x, ffn, p, norm_g, w_gate, w_proj, norm_final, final):
    t = x.shape[0]
    tm = min(t, 512)
    row = lambda i: (i, 0)
    full = lambda i: (0, 0)
    return pl.pallas_call(
        functools.partial(_ple_kernel, final=final),
        grid=(t // tm,),
        in_specs=[
            pl.BlockSpec((tm, D_MODEL), row),
            pl.BlockSpec((tm, D_MODEL), row),
            pl.BlockSpec((tm, PLE_DIM), row),
            pl.BlockSpec((1, D_MODEL), full),
            pl.BlockSpec((D_MODEL, D_MODEL), full),
            pl.BlockSpec((PLE_DIM, D_MODEL), full),
            pl.BlockSpec((1, D_MODEL), full),
        ],
        out_specs=pl.BlockSpec((tm, D_MODEL), row),
        out_shape=jax.ShapeDtypeStruct((t, D_MODEL), F32),
        compiler_params=_cparams("parallel"),
        name="ple_block",
    )(x, ffn, p, norm_g.reshape(1, D_MODEL), w_gate, w_proj, norm_final.reshape(1, D_MODEL))


def _trunk_layer(x, ple, lw, batch, seq, pool_prefix, pool_start, cache, final):
    uv = norm_matmul(x, lw["norm_mix"], lw["w_uv"], "gelu", F32, 2 * BW)
    mid = norm_matmul(x, lw["norm_mix"], lw["w_mid"], "none", F32, 4 * BW)
    gates = norm_matmul(x, lw["norm_mix"], lw["w_gates"], "sigmoid", BF16, D_MODEL)
    ya, vn = gmlp_mix(uv, lw["gmlp_ln_g"], lw["gmlp_ln_b"], lw["gmlp_ws"], lw["gmlp_bs"], min(seq, GMLP_CHUNK))
    yb, pool_state = pool_mix(mid, pool_prefix, pool_start, lw["pool_w"], lw["pool_scale"], batch, seq)
    if cache is None:
        yc = attn_prompt(mid, lw["rel_bias"], batch, seq)
    else:
        yc = attn_sample(mid, cache[0], cache[1], lw["rel_bias"], batch, seq)
    x = mixer_out(x, ya, yb, yc, gates, lw["w_branch"], lw["w_out"])
    ffn = peer_block(x, lw["norm_ffn"], lw["peer_wq"], lw["peer_subkeys"], lw["peer_table"])
    x = ple_block(x, ffn, ple, lw["norm_ple"], lw["ple_gate"], lw["ple_proj"], lw["norm_final"], final)
    return x, mid, pool_state, vn


def kernel(x_prompt, x_sample, cache_attn_k, cache_attn_v, state_pool, p_prompt, p_sample, norm_mix, w_in, gmlp_ln_g, gmlp_ln_b, gmlp_ws, gmlp_bs, pool_w, pool_scale, attn_rel_bias, w_branch, w_out, norm_ffn, peer_wq, peer_subkeys, peer_u, peer_v, norm_ple, ple_gate, ple_proj, norm_final):
    bp, lp, _ = x_prompt.shape
    bs, ls, _ = x_sample.shape
    assert lp % BAND_PAST == 0 and lp % GMLP_CHUNK == 0 and ls <= CHUNK
    n_keep = min(BAND_PAST, lp)
    xp = x_prompt.reshape(bp * lp, D_MODEL)
    xs = x_sample.reshape(bs * ls, D_MODEL)
    zero_prefix = jnp.zeros((bp, POOL_STATE, BW), F32)
    outs = {k: [] for k in ("pk", "pv", "pps", "sk", "sv", "sps", "sgv")}
    for i in range(DEPTH):
        w_in_b = w_in[i].astype(BF16)
        lw = dict(
            norm_mix=norm_mix[i],
            w_uv=w_in_b[:, :2 * BW],
            w_mid=w_in_b[:, 2 * BW:6 * BW],
            w_gates=w_in_b[:, 6 * BW:],
            gmlp_ln_g=gmlp_ln_g[i], gmlp_ln_b=gmlp_ln_b[i], gmlp_ws=gmlp_ws[i], gmlp_bs=gmlp_bs[i],
            pool_w=pool_w[i], pool_scale=pool_scale[i], rel_bias=attn_rel_bias[i],
            w_branch=w_branch[i].astype(BF16), w_out=w_out[i].astype(BF16),
            norm_ffn=norm_ffn[i], peer_wq=peer_wq[i].astype(BF16),
            peer_subkeys=peer_subkeys[i].reshape(2 * PEER_HEADS, PEER_NKEYS, PEER_HALF).astype(BF16),
            peer_table=pack_expert_tables(peer_u[i], peer_v[i]),
            norm_ple=norm_ple[i], ple_gate=ple_gate[i].astype(BF16), ple_proj=ple_proj[i].astype(BF16),
            norm_final=norm_final,
        )
        final = i == DEPTH - 1
        xp, mid_p, ps_p, _ = _trunk_layer(xp, p_prompt[i].reshape(bp * lp, PLE_DIM), lw, bp, lp, zero_prefix, 0,
                                          None, final)
        mid_p = mid_p.reshape(bp, lp, 4 * BW)
        outs["pk"].append(mid_p[:, lp - n_keep:, 2 * BW:3 * BW].reshape(bp, n_keep, HEADS, HEAD_DIM))
        outs["pv"].append(mid_p[:, lp - n_keep:, 3 * BW:].reshape(bp, n_keep, HEADS, HEAD_DIM))
        outs["pps"].append(ps_p)
        xs, mid_s, ps_s, vn_s = _trunk_layer(xs, p_sample[i].reshape(bs * ls, PLE_DIM), lw, bs, ls, state_pool[i],
                                             PAST_LEN, (cache_attn_k[i], cache_attn_v[i]), final)
        mid_s = mid_s.reshape(bs, ls, 4 * BW)
        outs["sk"].append(mid_s[:, :, 2 * BW:3 * BW].reshape(bs, ls, HEADS, HEAD_DIM))
        outs["sv"].append(mid_s[:, :, 3 * BW:].reshape(bs, ls, HEADS, HEAD_DIM))
        outs["sps"].append(ps_s)
        outs["sgv"].append(vn_s.reshape(bs, ls, BW))
    st = lambda k: jnp.stack(outs[k])
    return (xp.reshape(bp, lp, D_MODEL), xs.reshape(bs, ls, D_MODEL), st("pk"), st("pv"), st("pps"),
            st("sk"), st("sv"), st("sps"), st("sgv"))
```

```python
import functools
import math

import jax
import jax.numpy as jnp
import numpy as np
from jax import lax
from jax.experimental import pallas as pl
from jax.experimental.pallas import tpu as pltpu
from jax.experimental.pallas import tpu_sc as plsc

F32 = jnp.float32
BF16 = jnp.bfloat16

D_MODEL = 1024
DEPTH = 2
CHUNK = 64
EPS = 1e-6
BW = D_MODEL // 2
GMLP_CHUNK = 128
GROUPS = 4
GDIM = BW // GROUPS
POOL_WINDOWS = (2, 4, 8, 16)
POOL_STATE = 15
POOL_PAD = 16
HEADS = 8
HEAD_DIM = BW // HEADS
BAND_CHUNKS = 8
BAND_PAST = BAND_CHUNKS * CHUNK
REL_CLIP = 128
PAST_LEN = 4096
PEER_HEADS = 8
PEER_NKEYS = 128
PEER_HALF = 128
PEER_TOPK = 16
PEER_SLOTS = PEER_HEADS * PEER_TOPK
PLE_DIM = 256

LANES = 128
VMEM_LIMIT = 56 * 1024 * 1024
NEG = -1e30


def _cparams(*sem):
    return pltpu.CompilerParams(dimension_semantics=sem, vmem_limit_bytes=VMEM_LIMIT)


def _rms(x, g):
    ms = jnp.mean(x * x, axis=-1, keepdims=True)
    return x * lax.rsqrt(ms + EPS) * g


def _gelu(x):
    c = math.sqrt(2.0 / math.pi)
    return 0.5 * x * (1.0 + jnp.tanh(c * (x + 0.044715 * (x * x * x))))


def _sigmoid(x):
    return 1.0 / (1.0 + jnp.exp(-x))


_ACTS = {"gelu": _gelu, "sigmoid": _sigmoid, "none": lambda z: z}


def _norm_matmul_kernel(x_ref, g_ref, w_ref, o_ref, h_ref, *, act):
    @pl.when(pl.program_id(1) == 0)
    def _():
        h_ref[...] = _rms(x_ref[...], g_ref[...]).astype(BF16)

    z = jnp.dot(h_ref[...], w_ref[...], preferred_element_type=F32)
    o_ref[...] = _ACTS[act](z).astype(o_ref.dtype)


def norm_matmul(x, g, w, act, out_dtype, tn):
    t, d = x.shape
    n = w.shape[1]
    tm = min(t, 512)
    return pl.pallas_call(
        functools.partial(_norm_matmul_kernel, act=act),
        grid=(t // tm, n // tn),
        in_specs=[
            pl.BlockSpec((tm, d), lambda i, j: (i, 0)),
            pl.BlockSpec((1, d), lambda i, j: (0, 0)),
            pl.BlockSpec((d, tn), lambda i, j: (0, j)),
        ],
        out_specs=pl.BlockSpec((tm, tn), lambda i, j: (i, j)),
        out_shape=jax.ShapeDtypeStruct((t, n), out_dtype),
        scratch_shapes=[pltpu.VMEM((tm, d), BF16)],
        compiler_params=_cparams("parallel", "arbitrary"),
        name="norm_matmul_" + act,
    )(x, g.reshape(1, d), w)


def _gmlp_kernel(uv_ref, lng_ref, lnb_ref, ws_ref, bst_ref, y_ref, vn_ref, *, lc):
    u = uv_ref[:, :BW]
    v = uv_ref[:, BW:]
    mu = jnp.mean(v, axis=-1, keepdims=True)
    vc = v - mu
    var = jnp.mean(vc * vc, axis=-1, keepdims=True)
    vn = vc * lax.rsqrt(var + EPS) * lng_ref[...] + lnb_ref[...]
    vn_ref[...] = vn
    row = lax.broadcasted_iota(jnp.int32, (lc, lc), 0) // CHUNK
    col = lax.broadcasted_iota(jnp.int32, (lc, lc), 1) // CHUNK
    causal = col <= row
    vnb = vn.astype(BF16)
    for g in range(GROUPS):
        w = jnp.where(causal, ws_ref[g], 0.0).astype(BF16)
        s = jnp.dot(w, vnb[:, g * GDIM:(g + 1) * GDIM], preferred_element_type=F32)
        s = s + bst_ref[:, g:g + 1]
        y_ref[:, g * GDIM:(g + 1) * GDIM] = (u[:, g * GDIM:(g + 1) * GDIM] * s).astype(y_ref.dtype)


def gmlp_mix(uv, ln_g, ln_b, ws, bs, lc):
    t = uv.shape[0]
    return pl.pallas_call(
        functools.partial(_gmlp_kernel, lc=lc),
        grid=(t // lc,),
        in_specs=[
            pl.BlockSpec((lc, 2 * BW), lambda i: (i, 0)),
            pl.BlockSpec((1, BW), lambda i: (0, 0)),
            pl.BlockSpec((1, BW), lambda i: (0, 0)),
            pl.BlockSpec((GROUPS, lc, lc), lambda i: (0, 0, 0)),
            pl.BlockSpec((lc, GROUPS), lambda i: (0, 0)),
        ],
        out_specs=[
            pl.BlockSpec((lc, BW), lambda i: (i, 0)),
            pl.BlockSpec((lc, BW), lambda i: (i, 0)),
        ],
        out_shape=[
            jax.ShapeDtypeStruct((t, BW), BF16),
            jax.ShapeDtypeStruct((t, BW), F32),
        ],
        compiler_params=_cparams("parallel"),
        name="gmlp_mix",
    )(uv, ln_g.reshape(1, BW), ln_b.reshape(1, BW), ws[:, :lc, :lc], bs[:, :lc].T)


def _pool_kernel(u_ref, pre_ref, w_ref, sc_ref, y_ref, st_ref, pad_ref, *, seq, start_pos):
    pad_ref[0:POOL_PAD, :] = pre_ref[...]
    pad_ref[POOL_PAD:, :] = u_ref[...]
    pos = lax.broadcasted_iota(jnp.int32, (seq, 1), 0) + start_pos
    for g, win in enumerate(POOL_WINDOWS):
        cols = slice(g * GDIM, (g + 1) * GDIM)
        tok = pad_ref[POOL_PAD:, cols]
        acc = tok
        for k in range(1, win):
            acc = acc + pad_ref[POOL_PAD - k:POOL_PAD - k + seq, cols]
        cnt = jnp.minimum(pos + 1, win).astype(F32)
        d = acc / cnt - tok
        y = jnp.dot(d.astype(BF16), w_ref[g], preferred_element_type=F32)
        y_ref[:, cols] = (y * sc_ref[:, cols]).astype(y_ref.dtype)
    st_ref[...] = pad_ref[seq + 1:seq + POOL_PAD, :]


def pool_mix(mid, prefix, start_pos, pool_w, pool_scale, batch, seq):
    mid3 = mid.reshape(batch, seq, 4 * BW)
    pre = jnp.concatenate([jnp.zeros((batch, 1, BW), F32), prefix], axis=1)
    y, st = pl.pallas_call(
        functools.partial(_pool_kernel, seq=seq, start_pos=start_pos),
        grid=(batch,),
        in_specs=[
            pl.BlockSpec((None, seq, BW), lambda b: (b, 0, 0)),
            pl.BlockSpec((None, POOL_PAD, BW), lambda b: (b, 0, 0)),
            pl.BlockSpec((GROUPS, GDIM, GDIM), lambda b: (0, 0, 0)),
            pl.BlockSpec((1, BW), lambda b: (0, 0)),
        ],
        out_specs=[
            pl.BlockSpec((None, seq, BW), lambda b: (b, 0, 0)),
            pl.BlockSpec((None, POOL_STATE, BW), lambda b: (b, 0, 0)),
        ],
        out_shape=[
            jax.ShapeDtypeStruct((batch, seq, BW), BF16),
            jax.ShapeDtypeStruct((batch, POOL_STATE, BW), F32),
        ],
        scratch_shapes=[pltpu.VMEM((seq + POOL_PAD, BW), F32)],
        compiler_params=_cparams("parallel"),
        name="pool_mix",
    )(mid3, pre, pool_w.astype(BF16), pool_scale.reshape(1, BW))
    return y.reshape(batch * seq, BW), st


def _attn_chunks(q_ref, kcat_ref, vcat_ref, bias_ref, o_ref, *, n_chunks, cq, band, first_block):
    scale = HEAD_DIM ** -0.5
    for h in range(HEADS):
        cols = slice(h * HEAD_DIM, (h + 1) * HEAD_DIM)
        bias = bias_ref[h]
        for i in range(n_chunks):
            q = (q_ref[i * cq:(i + 1) * cq, cols] * scale).astype(BF16)
            k = kcat_ref[i * cq:i * cq + band, cols]
            v = vcat_ref[i * cq:i * cq + band, cols]
            s = lax.dot_general(q, k, (((1,), (1,)), ((), ())), preferred_element_type=F32) + bias
            if first_block is not None:
                key = lax.broadcasted_iota(jnp.int32, (1, band), 1)
                s = jnp.where(key >= first_block * (BAND_PAST - i * cq), s, NEG)
            m = jnp.max(s, axis=-1, keepdims=True)
            p = jnp.exp(s - m)
            l = jnp.sum(p, axis=-1, keepdims=True)
            o = jnp.dot(p.astype(BF16), v, preferred_element_type=F32) / l
            o_ref[i * cq:(i + 1) * cq, cols] = o.astype(o_ref.dtype)


def _attn_prompt_kernel(q_ref, kp_ref, ko_ref, vp_ref, vo_ref, bias_ref, o_ref, kcat_ref, vcat_ref):
    kcat_ref[0:BAND_PAST, :] = kp_ref[...].astype(BF16)
    kcat_ref[BAND_PAST:, :] = ko_ref[...].astype(BF16)
    vcat_ref[0:BAND_PAST, :] = vp_ref[...].astype(BF16)
    vcat_ref[BAND_PAST:, :] = vo_ref[...].astype(BF16)
    _attn_chunks(q_ref, kcat_ref, vcat_ref, bias_ref, o_ref, n_chunks=BAND_CHUNKS, cq=CHUNK,
                 band=BAND_PAST + CHUNK, first_block=(pl.program_id(1) == 0).astype(jnp.int32))


def _rel_bias_tile(rel_bias, qpos, kpos):
    rel = np.clip(qpos[:, None] - kpos[None, :], -REL_CLIP, REL_CLIP) + REL_CLIP
    return rel_bias[:, rel]


def attn_prompt(mid, rel_bias, batch, seq):
    mid3 = mid.reshape(batch, seq, 4 * BW)
    blk = BAND_PAST
    bias = _rel_bias_tile(rel_bias, np.arange(CHUNK), np.arange(BAND_PAST + CHUNK) - BAND_PAST)
    prev = lambda b, j: jnp.maximum(j - 1, 0)
    y = pl.pallas_call(
        _attn_prompt_kernel,
        grid=(batch, seq // blk),
        in_specs=[
            pl.BlockSpec((None, blk, BW), lambda b, j: (b, j, 1)),
            pl.BlockSpec((None, blk, BW), lambda b, j: (b, prev(b, j), 2)),
            pl.BlockSpec((None, blk, BW), lambda b, j: (b, j, 2)),
            pl.BlockSpec((None, blk, BW), lambda b, j: (b, prev(b, j), 3)),
            pl.BlockSpec((None, blk, BW), lambda b, j: (b, j, 3)),
            pl.BlockSpec((HEADS, CHUNK, BAND_PAST + CHUNK), lambda b, j: (0, 0, 0)),
        ],
        out_specs=pl.BlockSpec((None, blk, BW), lambda b, j: (b, j, 0)),
        out_shape=jax.ShapeDtypeStruct((batch, seq, BW), BF16),
        scratch_shapes=[pltpu.VMEM((2 * blk, BW), BF16), pltpu.VMEM((2 * blk, BW), BF16)],
        compiler_params=_cparams("parallel", "parallel"),
        name="attn_prompt",
    )(mid3, mid3, mid3, mid3, mid3, bias)
    return y.reshape(batch * seq, BW)


def _attn_sample_kernel(q_ref, kc_ref, kn_ref, vc_ref, vn_ref, bias_ref, o_ref, kcat_ref, vcat_ref, *, n_cache):
    kcat_ref[0:n_cache, :] = kc_ref[...].astype(BF16)
    kcat_ref[n_cache:, :] = kn_ref[...].astype(BF16)
    vcat_ref[0:n_cache, :] = vc_ref[...].astype(BF16)
    vcat_ref[n_cache:, :] = vn_ref[...].astype(BF16)
    seq = q_ref.shape[0]
    _attn_chunks(q_ref, kcat_ref, vcat_ref, bias_ref, o_ref, n_chunks=1, cq=seq, band=n_cache + seq,
                 first_block=None)


def attn_sample(mid, cache_k, cache_v, rel_bias, batch, seq):
    n_cache = cache_k.shape[1]
    assert PAST_LEN >= n_cache
    mid3 = mid.reshape(batch, seq, 4 * BW)
    ck = cache_k.reshape(batch, n_cache, BW)
    cv = cache_v.reshape(batch, n_cache, BW)
    bias = _rel_bias_tile(rel_bias, PAST_LEN + np.arange(seq), PAST_LEN - n_cache + np.arange(n_cache + seq))
    y = pl.pallas_call(
        functools.partial(_attn_sample_kernel, n_cache=n_cache),
        grid=(batch,),
        in_specs=[
            pl.BlockSpec((None, seq, BW), lambda b: (b, 0, 1)),
            pl.BlockSpec((None, n_cache, BW), lambda b: (b, 0, 0)),
            pl.BlockSpec((None, seq, BW), lambda b: (b, 0, 2)),
            pl.BlockSpec((None, n_cache, BW), lambda b: (b, 0, 0)),
            pl.BlockSpec((None, seq, BW), lambda b: (b, 0, 3)),
            pl.BlockSpec((HEADS, seq, n_cache + seq), lambda b: (0, 0, 0)),
        ],
        out_specs=pl.BlockSpec((None, seq, BW), lambda b: (b, 0, 0)),
        out_shape=jax.ShapeDtypeStruct((batch, seq, BW), BF16),
        scratch_shapes=[pltpu.VMEM((n_cache + seq, BW), BF16), pltpu.VMEM((n_cache + seq, BW), BF16)],
        compiler_params=_cparams("parallel"),
        name="attn_sample",
    )(mid3, ck, mid3, cv, mid3, bias)
    return y.reshape(batch * seq, BW)


def _mixer_out_kernel(x_ref, ya_ref, yb_ref, yc_ref, gate_ref, wb_ref, wo_ref, o_ref):
    acc = None
    for n, y_ref in enumerate((ya_ref, yb_ref, yc_ref)):
        proj = jnp.dot(y_ref[...], wb_ref[n], preferred_element_type=F32)
        term = gate_ref[:, n * D_MODEL:(n + 1) * D_MODEL].astype(F32) * proj
        acc = term if acc is None else acc + term
    o_ref[...] = x_ref[...] + jnp.dot(acc.astype(BF16), wo_ref[...], preferred_element_type=F32)


def mixer_out(x, ya, yb, yc, gates, w_branch, w_out):
    t = x.shape[0]
    tm = min(t, 512)
    row = lambda i: (i, 0)
    return pl.pallas_call(
        _mixer_out_kernel,
        grid=(t // tm,),
        in_specs=[
            pl.BlockSpec((tm, D_MODEL), row),
            pl.BlockSpec((tm, BW), row),
            pl.BlockSpec((tm, BW), row),
            pl.BlockSpec((tm, BW), row),
            pl.BlockSpec((tm, 3 * D_MODEL), row),
            pl.BlockSpec((3, BW, D_MODEL), lambda i: (0, 0, 0)),
            pl.BlockSpec((D_MODEL, D_MODEL), lambda i: (0, 0)),
        ],
        out_specs=pl.BlockSpec((tm, D_MODEL), row),
        out_shape=jax.ShapeDtypeStruct((t, D_MODEL), F32),
        compiler_params=_cparams("parallel"),
        name="mixer_out",
    )(x, ya, yb, yc, gates, w_branch, w_out)


def _extract_top(s, payload, k):
    r = float(s.shape[0])
    rows = lax.broadcasted_iota(jnp.int32, s.shape, 0).astype(F32)
    vals, pays = [], []
    for _ in range(k):
        m = jnp.max(s, axis=0, keepdims=True)
        idx = jnp.min(jnp.where(s == m, rows, r), axis=0, keepdims=True)
        sel = rows == idx
        vals.append(m)
        pays.append(idx if payload is None else jnp.max(jnp.where(sel, payload, -1.0), axis=0, keepdims=True))
        s = jnp.where(sel, -jnp.inf, s)
    return jnp.concatenate(vals, axis=0), jnp.concatenate(pays, axis=0)


def _peer_query_kernel(x_ref, g_ref, wq_ref, sk_ref, h_ref, idx_ref, gate_ref, q_ref, *, tm):
    h = _rms(x_ref[...], g_ref[...])
    h_ref[...] = h
    q_ref[...] = jnp.dot(h.astype(BF16), wq_ref[...], preferred_element_type=F32).astype(BF16)

    def sub_block(sb, carry):
        tok = pl.ds(pl.multiple_of(sb * LANES, LANES), LANES)
        for hd in range(PEER_HEADS):
            sv, si = [], []
            for p in range(2):
                hp = hd * 2 + p
                q = q_ref[tok, hp * PEER_HALF:(hp + 1) * PEER_HALF]
                s = lax.dot_general(sk_ref[hp], q, (((1,), (1,)), ((), ())), preferred_element_type=F32)
                v, i = _extract_top(s, None, PEER_TOPK)
                sv.append(v)
                si.append(i)
            cand = jnp.concatenate([sv[0][a:a + 1] + sv[1] for a in range(PEER_TOPK)], axis=0)
            eid = jnp.concatenate([si[0][a:a + 1] * PEER_NKEYS + si[1] for a in range(PEER_TOPK)], axis=0)
            tv, te = _extract_top(cand, eid, PEER_TOPK)
            e = jnp.exp(tv - tv[0:1])
            gate = e / jnp.sum(e, axis=0, keepdims=True)
            idx_ref[sb, hd * PEER_TOPK:(hd + 1) * PEER_TOPK, :] = te.astype(jnp.int32)
            gate_ref[sb, hd * PEER_TOPK:(hd + 1) * PEER_TOPK, :] = gate
        return carry

    lax.fori_loop(0, tm // LANES, sub_block, 0)


def peer_query(x, norm_g, wq, subkeys):
    t = x.shape[0]
    tm = min(t, 512)
    nq = wq.shape[1]
    nsb = tm // LANES
    return pl.pallas_call(
        functools.partial(_peer_query_kernel, tm=tm),
        grid=(t // tm,),
        in_specs=[
            pl.BlockSpec((tm, D_MODEL), lambda i: (i, 0)),
            pl.BlockSpec((1, D_MODEL), lambda i: (0, 0)),
            pl.BlockSpec((D_MODEL, nq), lambda i: (0, 0)),
            pl.BlockSpec((2 * PEER_HEADS, PEER_NKEYS, PEER_HALF), lambda i: (0, 0, 0)),
        ],
        out_specs=[
            pl.BlockSpec((tm, D_MODEL), lambda i: (i, 0)),
            pl.BlockSpec((nsb, PEER_SLOTS, LANES), lambda i: (i, 0, 0)),
            pl.BlockSpec((nsb, PEER_SLOTS, LANES), lambda i: (i, 0, 0)),
        ],
        out_shape=[
            jax.ShapeDtypeStruct((t, D_MODEL), F32),
            jax.ShapeDtypeStruct((t // LANES, PEER_SLOTS, LANES), jnp.int32),
            jax.ShapeDtypeStruct((t // LANES, PEER_SLOTS, LANES), F32),
        ],
        scratch_shapes=[pltpu.VMEM((tm, nq), BF16)],
        compiler_params=_cparams("parallel"),
        name="peer_query",
    )(x, norm_g.reshape(1, D_MODEL), wq, subkeys)


HALF_D = D_MODEL // 2
SC_LANES = 16
SC_UNIT_ROWS = 32
SC_UNITS_PER_TOKEN = PEER_SLOTS // SC_UNIT_ROWS
SC_TOKEN_BLOCK = 8
SC_ROW_GROUP = 4
SC_HALF_VECS = HALF_D // SC_LANES
SC_OUT_VECS = 8


def pack_expert_tables(peer_u, peer_v):
    def pack(x):
        b = lax.bitcast_convert_type(x.astype(BF16), jnp.uint16).astype(jnp.uint32)
        return (b[:, HALF_D:] << 16) | b[:, :HALF_D]

    return lax.bitcast_convert_type(jnp.concatenate([pack(peer_u), pack(peer_v)], axis=1), jnp.int32)


def sc_peer_experts(table, idx, gates, h):
    t = h.shape[0]
    info = plsc.get_sparse_core_info()
    n_workers = info.num_cores * info.num_subcores
    tpw = t // n_workers
    assert t % (n_workers * SC_TOKEN_BLOCK) == 0
    units = SC_TOKEN_BLOCK * SC_UNITS_PER_TOKEN
    mesh = plsc.VectorSubcoreMesh(core_axis_name="core", subcore_axis_name="subcore")
    hi_mask = jnp.int32(-65536)
    gelu_c = math.sqrt(2.0 / math.pi)

    @functools.partial(
        pl.kernel,
        out_type=jax.ShapeDtypeStruct((t, D_MODEL), F32),
        mesh=mesh,
        scratch_types=[
            pltpu.VMEM((SC_TOKEN_BLOCK * PEER_SLOTS,), jnp.int32),
            pltpu.VMEM((SC_TOKEN_BLOCK * PEER_SLOTS,), F32),
            pltpu.VMEM((SC_TOKEN_BLOCK, D_MODEL), F32),
            pltpu.VMEM((SC_TOKEN_BLOCK, D_MODEL), F32),
            pltpu.VMEM((2, SC_UNIT_ROWS, D_MODEL), jnp.int32),
            pltpu.VMEM((SC_UNIT_ROWS, SC_LANES), F32),
            pltpu.VMEM((SC_UNIT_ROWS,), F32),
            pltpu.SemaphoreType.DMA((2,)),
        ],
        compiler_params=pltpu.CompilerParams(needs_layout_passes=False),
        name="peer_sc_experts",
    )
    def kern(table_hbm, idx_hbm, gate_hbm, h_hbm, out_hbm, idx_v, gate_v, h_v, out_v, rows_v, part_v, coef_v, sem):
        wid = lax.axis_index("subcore") * info.num_cores + lax.axis_index("core")
        lane = lax.iota(jnp.int32, SC_LANES)
        zero = jnp.zeros((SC_LANES,), F32)

        def gather(unit, b):
            rows = idx_v.at[pl.ds(unit * SC_UNIT_ROWS, SC_UNIT_ROWS)]
            return pltpu.make_async_copy(table_hbm.at[rows], rows_v.at[b], sem.at[b])

        def unpack(w):
            return lax.bitcast_convert_type(w << 16, F32), lax.bitcast_convert_type(w & hi_mask, F32)

        def compute(unit, b):
            tl = unit // SC_UNITS_PER_TOKEN
            q = unit % SC_UNITS_PER_TOKEN

            def row_group(rg, carry):
                def kstep(k, accs):
                    off = pl.multiple_of(k * SC_LANES, SC_LANES)
                    h_lo = h_v[tl, pl.ds(off, SC_LANES)]
                    h_hi = h_v[tl, pl.ds(HALF_D + off, SC_LANES)]
                    new = []
                    for r in range(SC_ROW_GROUP):
                        lo, hi = unpack(rows_v[b, rg * SC_ROW_GROUP + r, pl.ds(off, SC_LANES)])
                        new.append(accs[r] + lo * h_lo + hi * h_hi)
                    return tuple(new)

                accs = lax.fori_loop(0, SC_HALF_VECS, kstep, (zero,) * SC_ROW_GROUP, unroll=4)
                for r in range(SC_ROW_GROUP):
                    part_v[rg * SC_ROW_GROUP + r, :] = accs[r]
                return carry

            lax.fori_loop(0, SC_UNIT_ROWS // SC_ROW_GROUP, row_group, 0)

            for i in range(SC_UNIT_ROWS // SC_LANES):
                rows = lane + i * SC_LANES
                a = zero
                for l in range(SC_LANES):
                    a = a + plsc.load_gather(part_v, [rows, jnp.full((SC_LANES,), l, jnp.int32)])
                z = gelu_c * (a + 0.044715 * (a * a * a))
                act = a / (1.0 + jnp.exp(-2.0 * z))
                slot = pl.multiple_of(tl * PEER_SLOTS + q * SC_UNIT_ROWS + i * SC_LANES, SC_LANES)
                coef_v[pl.ds(i * SC_LANES, SC_LANES)] = gate_v[pl.ds(slot, SC_LANES)] * act

            def out_pass(dq, carry):
                def row(r, accs):
                    c = plsc.load_gather(coef_v, [jnp.full((SC_LANES,), r, jnp.int32)])
                    new = []
                    for k in range(SC_OUT_VECS):
                        off = pl.multiple_of((dq * SC_OUT_VECS + k) * SC_LANES, SC_LANES)
                        lo, hi = unpack(rows_v[b, r, pl.ds(HALF_D + off, SC_LANES)])
                        new.append(accs[2 * k] + c * lo)
                        new.append(accs[2 * k + 1] + c * hi)
                    return tuple(new)

                accs = lax.fori_loop(0, SC_UNIT_ROWS, row, (zero,) * (2 * SC_OUT_VECS))
                for k in range(SC_OUT_VECS):
                    off = pl.multiple_of((dq * SC_OUT_VECS + k) * SC_LANES, SC_LANES)
                    plsc.addupdate(out_v.at[tl, pl.ds(off, SC_LANES)], accs[2 * k])
                    plsc.addupdate(out_v.at[tl, pl.ds(HALF_D + off, SC_LANES)], accs[2 * k + 1])
                return carry

            lax.fori_loop(0, SC_HALF_VECS // SC_OUT_VECS, out_pass, 0)

        @pl.loop(0, tpw // SC_TOKEN_BLOCK)
        def _(blk):
            tok0 = wid * tpw + blk * SC_TOKEN_BLOCK
            slots = pl.ds(tok0 * PEER_SLOTS, SC_TOKEN_BLOCK * PEER_SLOTS)
            pltpu.sync_copy(idx_hbm.at[slots], idx_v)
            pltpu.sync_copy(gate_hbm.at[slots], gate_v)
            pltpu.sync_copy(h_hbm.at[pl.ds(tok0, SC_TOKEN_BLOCK)], h_v)

            @pl.loop(0, SC_TOKEN_BLOCK)
            def _(tl):
                @pl.loop(0, D_MODEL // SC_LANES)
                def _(k):
                    out_v[tl, pl.ds(pl.multiple_of(k * SC_LANES, SC_LANES), SC_LANES)] = zero

            gather(0, 0).start()

            @pl.loop(0, units, step=2)
            def _(u0):
                for b in range(2):
                    unit = u0 + b

                    @pl.when(unit + 1 < units)
                    def _():
                        gather(unit + 1, 1 - b).start()

                    gather(unit, b).wait()
                    compute(unit, b)

            pltpu.sync_copy(out_v, out_hbm.at[pl.ds(tok0, SC_TOKEN_BLOCK)])

    return kern(table, idx, gates, h)


def peer_block(x, norm_g, wq, subkeys, table):
    t = x.shape[0]
    h, idx3, gate3 = peer_query(x, norm_g, wq, subkeys)
    idx = jnp.transpose(idx3, (0, 2, 1)).reshape(t * PEER_SLOTS)
    gates = jnp.transpose(gate3, (0, 2, 1)).reshape(t * PEER_SLOTS)
    return sc_peer_experts(table, idx, gates, h)


def _ple_kernel(x_ref, ffn_ref, p_ref, g_ref, wg_ref, wp_ref, gf_ref, o_ref, *, final):
    x = x_ref[...] + ffn_ref[...]
    gate = _sigmoid(jnp.dot(_rms(x, g_ref[...]).astype(BF16), wg_ref[...], preferred_element_type=F32))
    emb = jnp.dot(p_ref[...].astype(BF16), wp_ref[...], preferred_element_type=F32)
    y = x + gate * emb
    o_ref[...] = _rms(y, gf_ref[...]) if final else y


def ple_block(x, ffn, p, norm_g, w_gate, w_proj, norm_final, final):
    t = x.shape[0]
    tm = min(t, 512)
    row = lambda i: (i, 0)
    full = lambda i: (0, 0)
    return pl.pallas_call(
        functools.partial(_ple_kernel, final=final),
        grid=(t // tm,),
        in_specs=[
            pl.BlockSpec((tm, D_MODEL), row),
            pl.BlockSpec((tm, D_MODEL), row),
            pl.BlockSpec((tm, PLE_DIM), row),
            pl.BlockSpec((1, D_MODEL), full),
            pl.BlockSpec((D_MODEL, D_MODEL), full),
            pl.BlockSpec((PLE_DIM, D_MODEL), full),
            pl.BlockSpec((1, D_MODEL), full),
        ],
        out_specs=pl.BlockSpec((tm, D_MODEL), row),
        out_shape=jax.ShapeDtypeStruct((t, D_MODEL), F32),
        compiler_params=_cparams("parallel"),
        name="ple_block",
    )(x, ffn, p, norm_g.reshape(1, D_MODEL), w_gate, w_proj, norm_final.reshape(1, D_MODEL))


PROMPT_GROUPS = 4
def _trunk_layer(x, ple, lw, batch, seq, pool_prefix, pool_start, cache, final):
    uv = norm_matmul(x, lw["norm_mix"], lw["w_uv"], "gelu", F32, 2 * BW)
    mid = norm_matmul(x, lw["norm_mix"], lw["w_mid"], "none", F32, 4 * BW)
    gates = norm_matmul(x, lw["norm_mix"], lw["w_gates"], "sigmoid", BF16, D_MODEL)
    ya, vn = gmlp_mix(uv, lw["gmlp_ln_g"], lw["gmlp_ln_b"], lw["gmlp_ws"], lw["gmlp_bs"], min(seq, GMLP_CHUNK))
    yb, pool_state = pool_mix(mid, pool_prefix, pool_start, lw["pool_w"], lw["pool_scale"], batch, seq)
    if cache is None:
        yc = attn_prompt(mid, lw["rel_bias"], batch, seq)
    else:
        yc = attn_sample(mid, cache[0], cache[1], lw["rel_bias"], batch, seq)
    x = mixer_out(x, ya, yb, yc, gates, lw["w_branch"], lw["w_out"])
    ffn = peer_block(x, lw["norm_ffn"], lw["peer_wq"], lw["peer_subkeys"], lw["peer_table"])
    x = ple_block(x, ffn, ple, lw["norm_ple"], lw["ple_gate"], lw["ple_proj"], lw["norm_final"], final)
    return x, mid, pool_state, vn


def kernel(x_prompt, x_sample, cache_attn_k, cache_attn_v, state_pool, p_prompt, p_sample, norm_mix, w_in, gmlp_ln_g, gmlp_ln_b, gmlp_ws, gmlp_bs, pool_w, pool_scale, attn_rel_bias, w_branch, w_out, norm_ffn, peer_wq, peer_subkeys, peer_u, peer_v, norm_ple, ple_gate, ple_proj, norm_final):
    bp, lp, _ = x_prompt.shape
    bs, ls, _ = x_sample.shape
    assert lp % BAND_PAST == 0 and lp % GMLP_CHUNK == 0 and ls <= CHUNK
    n_keep = min(BAND_PAST, lp)
    n_groups = PROMPT_GROUPS if bp % PROMPT_GROUPS == 0 else 1
    bg = bp // n_groups
    xg = [x_prompt[g * bg:(g + 1) * bg].reshape(bg * lp, D_MODEL) for g in range(n_groups)]
    xs = x_sample.reshape(bs * ls, D_MODEL)
    zero_prefix = jnp.zeros((bg, POOL_STATE, BW), F32)
    outs = {k: [] for k in ("pk", "pv", "pps", "sk", "sv", "sps", "sgv")}
    for i in range(DEPTH):
        w_in_b = w_in[i].astype(BF16)
        lw = dict(
            norm_mix=norm_mix[i],
            w_uv=w_in_b[:, :2 * BW],
            w_mid=w_in_b[:, 2 * BW:6 * BW],
            w_gates=w_in_b[:, 6 * BW:],
            gmlp_ln_g=gmlp_ln_g[i], gmlp_ln_b=gmlp_ln_b[i], gmlp_ws=gmlp_ws[i], gmlp_bs=gmlp_bs[i],
            pool_w=pool_w[i], pool_scale=pool_scale[i], rel_bias=attn_rel_bias[i],
            w_branch=w_branch[i].astype(BF16), w_out=w_out[i].astype(BF16),
            norm_ffn=norm_ffn[i], peer_wq=peer_wq[i].astype(BF16),
            peer_subkeys=peer_subkeys[i].reshape(2 * PEER_HEADS, PEER_NKEYS, PEER_HALF).astype(BF16),
            peer_table=pack_expert_tables(peer_u[i], peer_v[i]),
            norm_ple=norm_ple[i], ple_gate=ple_gate[i].astype(BF16), ple_proj=ple_proj[i].astype(BF16),
            norm_final=norm_final,
        )
        final = i == DEPTH - 1
        pk, pv, pps = [], [], []
        for g in range(n_groups):
            ple_g = p_prompt[i, g * bg:(g + 1) * bg].reshape(bg * lp, PLE_DIM)
            xg[g], mid_p, ps_p, _ = _trunk_layer(xg[g], ple_g, lw, bg, lp, zero_prefix, 0, None, final)
            mid_p = mid_p.reshape(bg, lp, 4 * BW)
            pk.append(mid_p[:, lp - n_keep:, 2 * BW:3 * BW].reshape(bg, n_keep, HEADS, HEAD_DIM))
            pv.append(mid_p[:, lp - n_keep:, 3 * BW:].reshape(bg, n_keep, HEADS, HEAD_DIM))
            pps.append(ps_p)
        outs["pk"].append(jnp.concatenate(pk, axis=0))
        outs["pv"].append(jnp.concatenate(pv, axis=0))
        outs["pps"].append(jnp.concatenate(pps, axis=0))
        xs, mid_s, ps_s, vn_s = _trunk_layer(xs, p_sample[i].reshape(bs * ls, PLE_DIM), lw, bs, ls, state_pool[i],
                                             PAST_LEN, (cache_attn_k[i], cache_attn_v[i]), final)
        mid_s = mid_s.reshape(bs, ls, 4 * BW)
        outs["sk"].append(mid_s[:, :, 2 * BW:3 * BW].reshape(bs, ls, HEADS, HEAD_DIM))
        outs["sv"].append(mid_s[:, :, 3 * BW:].reshape(bs, ls, HEADS, HEAD_DIM))
        outs["sps"].append(ps_s)
        outs["sgv"].append(vn_s.reshape(bs, ls, BW))
    st = lambda k: jnp.stack(outs[k])
    y_prompt = jnp.concatenate([x.reshape(bg, lp, D_MODEL) for x in xg], axis=0)
    return (y_prompt, xs.reshape(bs, ls, D_MODEL), st("pk"), st("pv"), st("pps"),
            st("sk"), st("sv"), st("sps"), st("sgv"))
```

```python
import functools
import math

import jax
import jax.numpy as jnp
import numpy as np
from jax import lax
from jax.experimental import pallas as pl
from jax.experimental.pallas import tpu as pltpu
from jax.experimental.pallas import tpu_sc as plsc

F32 = jnp.float32
BF16 = jnp.bfloat16

D_MODEL = 1024
DEPTH = 2
CHUNK = 64
EPS = 1e-6
BW = D_MODEL // 2
GMLP_CHUNK = 128
GROUPS = 4
GDIM = BW // GROUPS
POOL_WINDOWS = (2, 4, 8, 16)
POOL_STATE = 15
POOL_PAD = 16
HEADS = 8
HEAD_DIM = BW // HEADS
BAND_CHUNKS = 8
BAND_PAST = BAND_CHUNKS * CHUNK
REL_CLIP = 128
PAST_LEN = 4096
PEER_HEADS = 8
PEER_NKEYS = 128
PEER_HALF = 128
PEER_TOPK = 16
PEER_SLOTS = PEER_HEADS * PEER_TOPK
PLE_DIM = 256

LANES = 128
VMEM_LIMIT = 56 * 1024 * 1024
NEG = -1e30


def _cparams(*sem):
    return pltpu.CompilerParams(dimension_semantics=sem, vmem_limit_bytes=VMEM_LIMIT)


def _rms(x, g):
    ms = jnp.mean(x * x, axis=-1, keepdims=True)
    return x * lax.rsqrt(ms + EPS) * g


def _gelu(x):
    c = math.sqrt(2.0 / math.pi)
    return 0.5 * x * (1.0 + jnp.tanh(c * (x + 0.044715 * (x * x * x))))


def _sigmoid(x):
    return 1.0 / (1.0 + jnp.exp(-x))


_ACTS = {"gelu": _gelu, "sigmoid": _sigmoid, "none": lambda z: z}


def _norm_matmul_kernel(x_ref, g_ref, w_ref, o_ref, h_ref, *, act):
    @pl.when(pl.program_id(1) == 0)
    def _():
        h_ref[...] = _rms(x_ref[...], g_ref[...]).astype(BF16)

    z = jnp.dot(h_ref[...], w_ref[...], preferred_element_type=F32)
    o_ref[...] = _ACTS[act](z).astype(o_ref.dtype)


def norm_matmul(x, g, w, act, out_dtype, tn):
    t, d = x.shape
    n = w.shape[1]
    tm = min(t, 512)
    return pl.pallas_call(
        functools.partial(_norm_matmul_kernel, act=act),
        grid=(t // tm, n // tn),
        in_specs=[
            pl.BlockSpec((tm, d), lambda i, j: (i, 0)),
            pl.BlockSpec((1, d), lambda i, j: (0, 0)),
            pl.BlockSpec((d, tn), lambda i, j: (0, j)),
        ],
        out_specs=pl.BlockSpec((tm, tn), lambda i, j: (i, j)),
        out_shape=jax.ShapeDtypeStruct((t, n), out_dtype),
        scratch_shapes=[pltpu.VMEM((tm, d), BF16)],
        compiler_params=_cparams("parallel", "arbitrary"),
        name="norm_matmul_" + act,
    )(x, g.reshape(1, d), w)


def _gmlp_kernel(uv_ref, lng_ref, lnb_ref, ws_ref, bst_ref, y_ref, vn_ref, *, lc):
    u = uv_ref[:, :BW]
    v = uv_ref[:, BW:]
    mu = jnp.mean(v, axis=-1, keepdims=True)
    vc = v - mu
    var = jnp.mean(vc * vc, axis=-1, keepdims=True)
    vn = vc * lax.rsqrt(var + EPS) * lng_ref[...] + lnb_ref[...]
    vn_ref[...] = vn
    row = lax.broadcasted_iota(jnp.int32, (lc, lc), 0) // CHUNK
    col = lax.broadcasted_iota(jnp.int32, (lc, lc), 1) // CHUNK
    causal = col <= row
    vnb = vn.astype(BF16)
    for g in range(GROUPS):
        w = jnp.where(causal, ws_ref[g], 0.0).astype(BF16)
        s = jnp.dot(w, vnb[:, g * GDIM:(g + 1) * GDIM], preferred_element_type=F32)
        s = s + bst_ref[:, g:g + 1]
        y_ref[:, g * GDIM:(g + 1) * GDIM] = (u[:, g * GDIM:(g + 1) * GDIM] * s).astype(y_ref.dtype)


def gmlp_mix(uv, ln_g, ln_b, ws, bs, lc):
    t = uv.shape[0]
    return pl.pallas_call(
        functools.partial(_gmlp_kernel, lc=lc),
        grid=(t // lc,),
        in_specs=[
            pl.BlockSpec((lc, 2 * BW), lambda i: (i, 0)),
            pl.BlockSpec((1, BW), lambda i: (0, 0)),
            pl.BlockSpec((1, BW), lambda i: (0, 0)),
            pl.BlockSpec((GROUPS, lc, lc), lambda i: (0, 0, 0)),
            pl.BlockSpec((lc, GROUPS), lambda i: (0, 0)),
        ],
        out_specs=[
            pl.BlockSpec((lc, BW), lambda i: (i, 0)),
            pl.BlockSpec((lc, BW), lambda i: (i, 0)),
        ],
        out_shape=[
            jax.ShapeDtypeStruct((t, BW), BF16),
            jax.ShapeDtypeStruct((t, BW), F32),
        ],
        compiler_params=_cparams("parallel"),
        name="gmlp_mix",
    )(uv, ln_g.reshape(1, BW), ln_b.reshape(1, BW), ws[:, :lc, :lc], bs[:, :lc].T)


def _pool_kernel(u_ref, pre_ref, w_ref, sc_ref, y_ref, st_ref, pad_ref, *, seq, start_pos):
    pad_ref[0:POOL_PAD, :] = pre_ref[...]
    pad_ref[POOL_PAD:, :] = u_ref[...]
    pos = lax.broadcasted_iota(jnp.int32, (seq, 1), 0) + start_pos
    for g, win in enumerate(POOL_WINDOWS):
        cols = slice(g * GDIM, (g + 1) * GDIM)
        tok = pad_ref[POOL_PAD:, cols]
        acc = tok
        for k in range(1, win):
            acc = acc + pad_ref[POOL_PAD - k:POOL_PAD - k + seq, cols]
        cnt = jnp.minimum(pos + 1, win).astype(F32)
        d = acc / cnt - tok
        y = jnp.dot(d.astype(BF16), w_ref[g], preferred_element_type=F32)
        y_ref[:, cols] = (y * sc_ref[:, cols]).astype(y_ref.dtype)
    st_ref[...] = pad_ref[seq + 1:seq + POOL_PAD, :]


def pool_mix(mid, prefix, start_pos, pool_w, pool_scale, batch, seq):
    mid3 = mid.reshape(batch, seq, 4 * BW)
    pre = jnp.concatenate([jnp.zeros((batch, 1, BW), F32), prefix], axis=1)
    y, st = pl.pallas_call(
        functools.partial(_pool_kernel, seq=seq, start_pos=start_pos),
        grid=(batch,),
        in_specs=[
            pl.BlockSpec((None, seq, BW), lambda b: (b, 0, 0)),
            pl.BlockSpec((None, POOL_PAD, BW), lambda b: (b, 0, 0)),
            pl.BlockSpec((GROUPS, GDIM, GDIM), lambda b: (0, 0, 0)),
            pl.BlockSpec((1, BW), lambda b: (0, 0)),
        ],
        out_specs=[
            pl.BlockSpec((None, seq, BW), lambda b: (b, 0, 0)),
            pl.BlockSpec((None, POOL_STATE, BW), lambda b: (b, 0, 0)),
        ],
        out_shape=[
            jax.ShapeDtypeStruct((batch, seq, BW), BF16),
            jax.ShapeDtypeStruct((batch, POOL_STATE, BW), F32),
        ],
        scratch_shapes=[pltpu.VMEM((seq + POOL_PAD, BW), F32)],
        compiler_params=_cparams("parallel"),
        name="pool_mix",
    )(mid3, pre, pool_w.astype(BF16), pool_scale.reshape(1, BW))
    return y.reshape(batch * seq, BW), st


def _attn_chunks(q_ref, kcat_ref, vcat_ref, bias_ref, o_ref, *, n_chunks, cq, band, first_block):
    scale = HEAD_DIM ** -0.5
    for h in range(HEADS):
        cols = slice(h * HEAD_DIM, (h + 1) * HEAD_DIM)
        bias = bias_ref[h]
        for i in range(n_chunks):
            q = (q_ref[i * cq:(i + 1) * cq, cols] * scale).astype(BF16)
            k = kcat_ref[i * cq:i * cq + band, cols]
            v = vcat_ref[i * cq:i * cq + band, cols]
            s = lax.dot_general(q, k, (((1,), (1,)), ((), ())), preferred_element_type=F32) + bias
            if first_block is not None:
                key = lax.broadcasted_iota(jnp.int32, (1, band), 1)
                s = jnp.where(key >= first_block * (BAND_PAST - i * cq), s, NEG)
            m = jnp.max(s, axis=-1, keepdims=True)
            p = jnp.exp(s - m)
            l = jnp.sum(p, axis=-1, keepdims=True)
            o = jnp.dot(p.astype(BF16), v, preferred_element_type=F32) / l
            o_ref[i * cq:(i + 1) * cq, cols] = o.astype(o_ref.dtype)


def _attn_prompt_kernel(q_ref, kp_ref, ko_ref, vp_ref, vo_ref, bias_ref, o_ref, kcat_ref, vcat_ref):
    kcat_ref[0:BAND_PAST, :] = kp_ref[...].astype(BF16)
    kcat_ref[BAND_PAST:, :] = ko_ref[...].astype(BF16)
    vcat_ref[0:BAND_PAST, :] = vp_ref[...].astype(BF16)
    vcat_ref[BAND_PAST:, :] = vo_ref[...].astype(BF16)
    _attn_chunks(q_ref, kcat_ref, vcat_ref, bias_ref, o_ref, n_chunks=BAND_CHUNKS, cq=CHUNK,
                 band=BAND_PAST + CHUNK, first_block=(pl.program_id(1) == 0).astype(jnp.int32))


def _rel_bias_tile(rel_bias, qpos, kpos):
    rel = np.clip(qpos[:, None] - kpos[None, :], -REL_CLIP, REL_CLIP) + REL_CLIP
    return rel_bias[:, rel]


def attn_prompt(mid, rel_bias, batch, seq):
    mid3 = mid.reshape(batch, seq, 4 * BW)
    blk = BAND_PAST
    bias = _rel_bias_tile(rel_bias, np.arange(CHUNK), np.arange(BAND_PAST + CHUNK) - BAND_PAST)
    prev = lambda b, j: jnp.maximum(j - 1, 0)
    y = pl.pallas_call(
        _attn_prompt_kernel,
        grid=(batch, seq // blk),
        in_specs=[
            pl.BlockSpec((None, blk, BW), lambda b, j: (b, j, 1)),
            pl.BlockSpec((None, blk, BW), lambda b, j: (b, prev(b, j), 2)),
            pl.BlockSpec((None, blk, BW), lambda b, j: (b, j, 2)),
            pl.BlockSpec((None, blk, BW), lambda b, j: (b, prev(b, j), 3)),
            pl.BlockSpec((None, blk, BW), lambda b, j: (b, j, 3)),
            pl.BlockSpec((HEADS, CHUNK, BAND_PAST + CHUNK), lambda b, j: (0, 0, 0)),
        ],
        out_specs=pl.BlockSpec((None, blk, BW), lambda b, j: (b, j, 0)),
        out_shape=jax.ShapeDtypeStruct((batch, seq, BW), BF16),
        scratch_shapes=[pltpu.VMEM((2 * blk, BW), BF16), pltpu.VMEM((2 * blk, BW), BF16)],
        compiler_params=_cparams("parallel", "parallel"),
        name="attn_prompt",
    )(mid3, mid3, mid3, mid3, mid3, bias)
    return y.reshape(batch * seq, BW)


def _attn_sample_kernel(q_ref, kc_ref, kn_ref, vc_ref, vn_ref, bias_ref, o_ref, kcat_ref, vcat_ref, *, n_cache):
    kcat_ref[0:n_cache, :] = kc_ref[...].astype(BF16)
    kcat_ref[n_cache:, :] = kn_ref[...].astype(BF16)
    vcat_ref[0:n_cache, :] = vc_ref[...].astype(BF16)
    vcat_ref[n_cache:, :] = vn_ref[...].astype(BF16)
    seq = q_ref.shape[0]
    _attn_chunks(q_ref, kcat_ref, vcat_ref, bias_ref, o_ref, n_chunks=1, cq=seq, band=n_cache + seq,
                 first_block=None)


def attn_sample(mid, cache_k, cache_v, rel_bias, batch, seq):
    n_cache = cache_k.shape[1]
    assert PAST_LEN >= n_cache
    mid3 = mid.reshape(batch, seq, 4 * BW)
    ck = cache_k.reshape(batch, n_cache, BW)
    cv = cache_v.reshape(batch, n_cache, BW)
    bias = _rel_bias_tile(rel_bias, PAST_LEN + np.arange(seq), PAST_LEN - n_cache + np.arange(n_cache + seq))
    y = pl.pallas_call(
        functools.partial(_attn_sample_kernel, n_cache=n_cache),
        grid=(batch,),
        in_specs=[
            pl.BlockSpec((None, seq, BW), lambda b: (b, 0, 1)),
            pl.BlockSpec((None, n_cache, BW), lambda b: (b, 0, 0)),
            pl.BlockSpec((None, seq, BW), lambda b: (b, 0, 2)),
            pl.BlockSpec((None, n_cache, BW), lambda b: (b, 0, 0)),
            pl.BlockSpec((None, seq, BW), lambda b: (b, 0, 3)),
            pl.BlockSpec((HEADS, seq, n_cache + seq), lambda b: (0, 0, 0)),
        ],
        out_specs=pl.BlockSpec((None, seq, BW), lambda b: (b, 0, 0)),
        out_shape=jax.ShapeDtypeStruct((batch, seq, BW), BF16),
        scratch_shapes=[pltpu.VMEM((n_cache + seq, BW), BF16), pltpu.VMEM((n_cache + seq, BW), BF16)],
        compiler_params=_cparams("parallel"),
        name="attn_sample",
    )(mid3, ck, mid3, cv, mid3, bias)
    return y.reshape(batch * seq, BW)


def _mixer_out_kernel(x_ref, ya_ref, yb_ref, yc_ref, gate_ref, wb_ref, wo_ref, o_ref):
    acc = None
    for n, y_ref in enumerate((ya_ref, yb_ref, yc_ref)):
        proj = jnp.dot(y_ref[...], wb_ref[n], preferred_element_type=F32)
        term = gate_ref[:, n * D_MODEL:(n + 1) * D_MODEL].astype(F32) * proj
        acc = term if acc is None else acc + term
    o_ref[...] = x_ref[...] + jnp.dot(acc.astype(BF16), wo_ref[...], preferred_element_type=F32)


def mixer_out(x, ya, yb, yc, gates, w_branch, w_out):
    t = x.shape[0]
    tm = min(t, 512)
    row = lambda i: (i, 0)
    return pl.pallas_call(
        _mixer_out_kernel,
        grid=(t // tm,),
        in_specs=[
            pl.BlockSpec((tm, D_MODEL), row),
            pl.BlockSpec((tm, BW), row),
            pl.BlockSpec((tm, BW), row),
            pl.BlockSpec((tm, BW), row),
            pl.BlockSpec((tm, 3 * D_MODEL), row),
            pl.BlockSpec((3, BW, D_MODEL), lambda i: (0, 0, 0)),
            pl.BlockSpec((D_MODEL, D_MODEL), lambda i: (0, 0)),
        ],
        out_specs=pl.BlockSpec((tm, D_MODEL), row),
        out_shape=jax.ShapeDtypeStruct((t, D_MODEL), F32),
        compiler_params=_cparams("parallel"),
        name="mixer_out",
    )(x, ya, yb, yc, gates, w_branch, w_out)


def _extract_top(s, payload, k):
    r = float(s.shape[0])
    rows = lax.broadcasted_iota(jnp.int32, s.shape, 0).astype(F32)
    vals, pays = [], []
    for _ in range(k):
        m = jnp.max(s, axis=0, keepdims=True)
        idx = jnp.min(jnp.where(s == m, rows, r), axis=0, keepdims=True)
        sel = rows == idx
        vals.append(m)
        pays.append(idx if payload is None else jnp.max(jnp.where(sel, payload, -1.0), axis=0, keepdims=True))
        s = jnp.where(sel, -jnp.inf, s)
    return jnp.concatenate(vals, axis=0), jnp.concatenate(pays, axis=0)


def _peer_query_kernel(x_ref, g_ref, wq_ref, sk_ref, hp_ref, idx_ref, gate_ref, q_ref, *, tm):
    hb = _rms(x_ref[...], g_ref[...]).astype(BF16)
    q_ref[...] = jnp.dot(hb, wq_ref[...], preferred_element_type=F32).astype(BF16)
    bits = lax.bitcast_convert_type(hb.astype(F32), jnp.int32)
    half = D_MODEL // 2
    hp_ref[...] = bits[:, half:] | lax.shift_right_logical(bits[:, :half], 16)

    def sub_block(sb, carry):
        tok = pl.ds(pl.multiple_of(sb * LANES, LANES), LANES)
        for hd in range(PEER_HEADS):
            sv, si = [], []
            for p in range(2):
                hp = hd * 2 + p
                q = q_ref[tok, hp * PEER_HALF:(hp + 1) * PEER_HALF]
                s = lax.dot_general(sk_ref[hp], q, (((1,), (1,)), ((), ())), preferred_element_type=F32)
                v, i = _extract_top(s, None, PEER_TOPK)
                sv.append(v)
                si.append(i)
            cand = jnp.concatenate([sv[0][a:a + 1] + sv[1] for a in range(PEER_TOPK)], axis=0)
            eid = jnp.concatenate([si[0][a:a + 1] * PEER_NKEYS + si[1] for a in range(PEER_TOPK)], axis=0)
            tv, te = _extract_top(cand, eid, PEER_TOPK)
            e = jnp.exp(tv - tv[0:1])
            gate = e / jnp.sum(e, axis=0, keepdims=True)
            idx_ref[sb, hd * PEER_TOPK:(hd + 1) * PEER_TOPK, :] = te.astype(jnp.int32)
            gate_ref[sb, hd * PEER_TOPK:(hd + 1) * PEER_TOPK, :] = gate
        return carry

    lax.fori_loop(0, tm // LANES, sub_block, 0)


def peer_query(x, norm_g, wq, subkeys):
    t = x.shape[0]
    tm = min(t, 512)
    nq = wq.shape[1]
    nsb = tm // LANES
    return pl.pallas_call(
        functools.partial(_peer_query_kernel, tm=tm),
        grid=(t // tm,),
        in_specs=[
            pl.BlockSpec((tm, D_MODEL), lambda i: (i, 0)),
            pl.BlockSpec((1, D_MODEL), lambda i: (0, 0)),
            pl.BlockSpec((D_MODEL, nq), lambda i: (0, 0)),
            pl.BlockSpec((2 * PEER_HEADS, PEER_NKEYS, PEER_HALF), lambda i: (0, 0, 0)),
        ],
        out_specs=[
            pl.BlockSpec((tm, D_MODEL // 2), lambda i: (i, 0)),
            pl.BlockSpec((nsb, PEER_SLOTS, LANES), lambda i: (i, 0, 0)),
            pl.BlockSpec((nsb, PEER_SLOTS, LANES), lambda i: (i, 0, 0)),
        ],
        out_shape=[
            jax.ShapeDtypeStruct((t, D_MODEL // 2), jnp.int32),
            jax.ShapeDtypeStruct((t // LANES, PEER_SLOTS, LANES), jnp.int32),
            jax.ShapeDtypeStruct((t // LANES, PEER_SLOTS, LANES), F32),
        ],
        scratch_shapes=[pltpu.VMEM((tm, nq), BF16)],
        compiler_params=_cparams("parallel"),
        name="peer_query",
    )(x, norm_g.reshape(1, D_MODEL), wq, subkeys)


HALF_D = D_MODEL // 2
SC_LANES = 16
SC_UNIT_ROWS = 32
SC_UNITS_PER_TOKEN = PEER_SLOTS // SC_UNIT_ROWS
SC_TOKEN_BLOCK = 8
SC_ROW_GROUP = 8
SC_HALF_VECS = HALF_D // SC_LANES
SC_OUT_VECS = 8
SC_BF16_TERMS = 4


def pack_expert_tables(peer_u, peer_v):
    def pack(x):
        b = lax.bitcast_convert_type(x.astype(BF16), jnp.uint16).astype(jnp.uint32)
        return (b[:, HALF_D:] << 16) | b[:, :HALF_D]

    return lax.bitcast_convert_type(jnp.concatenate([pack(peer_u), pack(peer_v)], axis=1), jnp.int32)


def sc_peer_experts(table, idx, gates, hp):
    t = hp.shape[0]
    info = plsc.get_sparse_core_info()
    n_workers = info.num_cores * info.num_subcores
    tpw = t // n_workers
    assert t % (n_workers * SC_TOKEN_BLOCK) == 0
    units = SC_TOKEN_BLOCK * SC_UNITS_PER_TOKEN
    mesh = plsc.VectorSubcoreMesh(core_axis_name="core", subcore_axis_name="subcore")
    hi_mask = jnp.int32(-65536)
    gelu_c = math.sqrt(2.0 / math.pi)

    @functools.partial(
        pl.kernel,
        out_type=jax.ShapeDtypeStruct((t, D_MODEL), F32),
        mesh=mesh,
        scratch_types=[
            pltpu.VMEM((SC_TOKEN_BLOCK * PEER_SLOTS,), jnp.int32),
            pltpu.VMEM((SC_TOKEN_BLOCK * PEER_SLOTS,), F32),
            pltpu.VMEM((SC_TOKEN_BLOCK, HALF_D), jnp.int32),
            pltpu.VMEM((SC_TOKEN_BLOCK, D_MODEL), F32),
            pltpu.VMEM((2, SC_UNIT_ROWS, D_MODEL), jnp.int32),
            pltpu.VMEM((SC_UNIT_ROWS, SC_LANES), F32),
            pltpu.VMEM((SC_UNIT_ROWS,), jnp.int32),
            pltpu.SemaphoreType.DMA((2,)),
        ],
        compiler_params=pltpu.CompilerParams(needs_layout_passes=False),
        name="peer_sc_experts",
    )
    def kern(table_hbm, idx_hbm, gate_hbm, h_hbm, out_hbm, idx_v, gate_v, h_v, out_v, rows_v, part_v, coef_v, sem):
        wid = lax.axis_index("subcore") * info.num_cores + lax.axis_index("core")
        lane = lax.iota(jnp.int32, SC_LANES)
        zero = jnp.zeros((SC_LANES,), F32)

        def gather(unit, b):
            rows = idx_v.at[pl.ds(unit * SC_UNIT_ROWS, SC_UNIT_ROWS)]
            return pltpu.make_async_copy(table_hbm.at[rows], rows_v.at[b], sem.at[b])

        def unpack(w):
            return lax.bitcast_convert_type(w << 16, F32), lax.bitcast_convert_type(w & hi_mask, F32)

        def as_pairs(w):
            return plsc.bitcast(w, BF16)

        def compute(unit, b):
            tl = unit // SC_UNITS_PER_TOKEN
            q = unit % SC_UNITS_PER_TOKEN

            def row_group(rg, carry):
                def kstep(k, accs):
                    hs = []
                    for j in range(SC_BF16_TERMS):
                        off = pl.multiple_of((k * SC_BF16_TERMS + j) * SC_LANES, SC_LANES)
                        hs.append(as_pairs(h_v[tl, pl.ds(off, SC_LANES)]))
                    new = []
                    for r in range(SC_ROW_GROUP):
                        p = None
                        for j in range(SC_BF16_TERMS):
                            off = pl.multiple_of((k * SC_BF16_TERMS + j) * SC_LANES, SC_LANES)
                            term = as_pairs(rows_v[b, rg * SC_ROW_GROUP + r, pl.ds(off, SC_LANES)]) * hs[j]
                            p = term if p is None else p + term
                        lo, hi = unpack(plsc.bitcast(p, jnp.int32))
                        new.append(accs[r] + lo + hi)
                    return tuple(new)

                accs = lax.fori_loop(0, SC_HALF_VECS // SC_BF16_TERMS, kstep, (zero,) * SC_ROW_GROUP)
                for r in range(SC_ROW_GROUP):
                    part_v[rg * SC_ROW_GROUP + r, :] = accs[r]
                return carry

            lax.fori_loop(0, SC_UNIT_ROWS // SC_ROW_GROUP, row_group, 0)

            for i in range(SC_UNIT_ROWS // SC_LANES):
                rows = lane + i * SC_LANES
                a = zero
                for l in range(SC_LANES):
                    a = a + plsc.load_gather(part_v, [rows, jnp.full((SC_LANES,), l, jnp.int32)])
                z = gelu_c * (a + 0.044715 * (a * a * a))
                act = a / (1.0 + jnp.exp(-2.0 * z))
                slot = pl.multiple_of(tl * PEER_SLOTS + q * SC_UNIT_ROWS + i * SC_LANES, SC_LANES)
                bits = lax.bitcast_convert_type(gate_v[pl.ds(slot, SC_LANES)] * act, jnp.int32)
                top = (bits + 0x7FFF + ((bits >> 16) & 1)) & hi_mask
                coef_v[pl.ds(i * SC_LANES, SC_LANES)] = top | lax.shift_right_logical(top, 16)

            def out_pass(dq, carry):
                def row_quad(rq, accs):
                    cs = []
                    for j in range(SC_BF16_TERMS):
                        row = jnp.full((SC_LANES,), rq * SC_BF16_TERMS + j, jnp.int32)
                        cs.append(as_pairs(plsc.load_gather(coef_v, [row])))
                    new = []
                    for k in range(SC_OUT_VECS):
                        off = pl.multiple_of((dq * SC_OUT_VECS + k) * SC_LANES, SC_LANES)
                        p = None
                        for j in range(SC_BF16_TERMS):
                            w = rows_v[b, rq * SC_BF16_TERMS + j, pl.ds(HALF_D + off, SC_LANES)]
                            term = as_pairs(w) * cs[j]
                            p = term if p is None else p + term
                        lo, hi = unpack(plsc.bitcast(p, jnp.int32))
                        new.append(accs[2 * k] + lo)
                        new.append(accs[2 * k + 1] + hi)
                    return tuple(new)

                accs = lax.fori_loop(0, SC_UNIT_ROWS // SC_BF16_TERMS, row_quad, (zero,) * (2 * SC_OUT_VECS))
                for k in range(SC_OUT_VECS):
                    off = pl.multiple_of((dq * SC_OUT_VECS + k) * SC_LANES, SC_LANES)
                    plsc.addupdate(out_v.at[tl, pl.ds(off, SC_LANES)], accs[2 * k])
                    plsc.addupdate(out_v.at[tl, pl.ds(HALF_D + off, SC_LANES)], accs[2 * k + 1])
                return carry

            lax.fori_loop(0, SC_HALF_VECS // SC_OUT_VECS, out_pass, 0)

        @pl.loop(0, tpw // SC_TOKEN_BLOCK)
        def _(blk):
            tok0 = wid * tpw + blk * SC_TOKEN_BLOCK
            slots = pl.ds(tok0 * PEER_SLOTS, SC_TOKEN_BLOCK * PEER_SLOTS)
            pltpu.sync_copy(idx_hbm.at[slots], idx_v)
            pltpu.sync_copy(gate_hbm.at[slots], gate_v)
            pltpu.sync_copy(h_hbm.at[pl.ds(tok0, SC_TOKEN_BLOCK)], h_v)

            @pl.loop(0, SC_TOKEN_BLOCK)
            def _(tl):
                @pl.loop(0, D_MODEL // SC_LANES)
                def _(k):
                    out_v[tl, pl.ds(pl.multiple_of(k * SC_LANES, SC_LANES), SC_LANES)] = zero

            gather(0, 0).start()

            @pl.loop(0, units, step=2)
            def _(u0):
                for b in range(2):
                    unit = u0 + b

                    @pl.when(unit + 1 < units)
                    def _():
                        gather(unit + 1, 1 - b).start()

                    gather(unit, b).wait()
                    compute(unit, b)

            pltpu.sync_copy(out_v, out_hbm.at[pl.ds(tok0, SC_TOKEN_BLOCK)])

    return kern(table, idx, gates, hp)


def peer_block(x, norm_g, wq, subkeys, table):
    t = x.shape[0]
    hp, idx3, gate3 = peer_query(x, norm_g, wq, subkeys)
    idx = jnp.transpose(idx3, (0, 2, 1)).reshape(t * PEER_SLOTS)
    gates = jnp.transpose(gate3, (0, 2, 1)).reshape(t * PEER_SLOTS)
    return sc_peer_experts(table, idx, gates, hp)


def _ple_kernel(x_ref, ffn_ref, p_ref, g_ref, wg_ref, wp_ref, gf_ref, o_ref, *, final):
    x = x_ref[...] + ffn_ref[...]
    gate = _sigmoid(jnp.dot(_rms(x, g_ref[...]).astype(BF16), wg_ref[...], preferred_element_type=F32))
    emb = jnp.dot(p_ref[...].astype(BF16), wp_ref[...], preferred_element_type=F32)
    y = x + gate * emb
    o_ref[...] = _rms(y, gf_ref[...]) if final else y


def ple_block(x, ffn, p, norm_g, w_gate, w_proj, norm_final, final):
    t = x.shape[0]
    tm = min(t, 512)
    row = lambda i: (i, 0)
    full = lambda i: (0, 0)
    return pl.pallas_call(
        functools.partial(_ple_kernel, final=final),
        grid=(t // tm,),
        in_specs=[
            pl.BlockSpec((tm, D_MODEL), row),
            pl.BlockSpec((tm, D_MODEL), row),
            pl.BlockSpec((tm, PLE_DIM), row),
            pl.BlockSpec((1, D_MODEL), full),
            pl.BlockSpec((D_MODEL, D_MODEL), full),
            pl.BlockSpec((PLE_DIM, D_MODEL), full),
            pl.BlockSpec((1, D_MODEL), full),
        ],
        out_specs=pl.BlockSpec((tm, D_MODEL), row),
        out_shape=jax.ShapeDtypeStruct((t, D_MODEL), F32),
        compiler_params=_cparams("parallel"),
        name="ple_block",
    )(x, ffn, p, norm_g.reshape(1, D_MODEL), w_gate, w_proj, norm_final.reshape(1, D_MODEL))


PROMPT_GROUPS = 4
def _trunk_layer(x, ple, lw, batch, seq, pool_prefix, pool_start, cache, final):
    uv = norm_matmul(x, lw["norm_mix"], lw["w_uv"], "gelu", F32, 2 * BW)
    mid = norm_matmul(x, lw["norm_mix"], lw["w_mid"], "none", F32, 4 * BW)
    gates = norm_matmul(x, lw["norm_mix"], lw["w_gates"], "sigmoid", BF16, D_MODEL)
    ya, vn = gmlp_mix(uv, lw["gmlp_ln_g"], lw["gmlp_ln_b"], lw["gmlp_ws"], lw["gmlp_bs"], min(seq, GMLP_CHUNK))
    yb, pool_state = pool_mix(mid, pool_prefix, pool_start, lw["pool_w"], lw["pool_scale"], batch, seq)
    if cache is None:
        yc = attn_prompt(mid, lw["rel_bias"], batch, seq)
    else:
        yc = attn_sample(mid, cache[0], cache[1], lw["rel_bias"], batch, seq)
    x = mixer_out(x, ya, yb, yc, gates, lw["w_branch"], lw["w_out"])
    ffn = peer_block(x, lw["norm_ffn"], lw["peer_wq"], lw["peer_subkeys"], lw["peer_table"])
    x = ple_block(x, ffn, ple, lw["norm_ple"], lw["ple_gate"], lw["ple_proj"], lw["norm_final"], final)
    return x, mid, pool_state, vn


def kernel(x_prompt, x_sample, cache_attn_k, cache_attn_v, state_pool, p_prompt, p_sample, norm_mix, w_in, gmlp_ln_g, gmlp_ln_b, gmlp_ws, gmlp_bs, pool_w, pool_scale, attn_rel_bias, w_branch, w_out, norm_ffn, peer_wq, peer_subkeys, peer_u, peer_v, norm_ple, ple_gate, ple_proj, norm_final):
    bp, lp, _ = x_prompt.shape
    bs, ls, _ = x_sample.shape
    assert lp % BAND_PAST == 0 and lp % GMLP_CHUNK == 0 and ls <= CHUNK
    n_keep = min(BAND_PAST, lp)
    n_groups = PROMPT_GROUPS if bp % PROMPT_GROUPS == 0 else 1
    bg = bp // n_groups
    xg = [x_prompt[g * bg:(g + 1) * bg].reshape(bg * lp, D_MODEL) for g in range(n_groups)]
    xs = x_sample.reshape(bs * ls, D_MODEL)
    zero_prefix = jnp.zeros((bg, POOL_STATE, BW), F32)
    outs = {k: [] for k in ("pk", "pv", "pps", "sk", "sv", "sps", "sgv")}
    for i in range(DEPTH):
        w_in_b = w_in[i].astype(BF16)
        lw = dict(
            norm_mix=norm_mix[i],
            w_uv=w_in_b[:, :2 * BW],
            w_mid=w_in_b[:, 2 * BW:6 * BW],
            w_gates=w_in_b[:, 6 * BW:],
            gmlp_ln_g=gmlp_ln_g[i], gmlp_ln_b=gmlp_ln_b[i], gmlp_ws=gmlp_ws[i], gmlp_bs=gmlp_bs[i],
            pool_w=pool_w[i], pool_scale=pool_scale[i], rel_bias=attn_rel_bias[i],
            w_branch=w_branch[i].astype(BF16), w_out=w_out[i].astype(BF16),
            norm_ffn=norm_ffn[i], peer_wq=peer_wq[i].astype(BF16),
            peer_subkeys=peer_subkeys[i].reshape(2 * PEER_HEADS, PEER_NKEYS, PEER_HALF).astype(BF16),
            peer_table=pack_expert_tables(peer_u[i], peer_v[i]),
            norm_ple=norm_ple[i], ple_gate=ple_gate[i].astype(BF16), ple_proj=ple_proj[i].astype(BF16),
            norm_final=norm_final,
        )
        final = i == DEPTH - 1
        pk, pv, pps = [], [], []
        for g in range(n_groups):
            ple_g = p_prompt[i, g * bg:(g + 1) * bg].reshape(bg * lp, PLE_DIM)
            xg[g], mid_p, ps_p, _ = _trunk_layer(xg[g], ple_g, lw, bg, lp, zero_prefix, 0, None, final)
            mid_p = mid_p.reshape(bg, lp, 4 * BW)
            pk.append(mid_p[:, lp - n_keep:, 2 * BW:3 * BW].reshape(bg, n_keep, HEADS, HEAD_DIM))
            pv.append(mid_p[:, lp - n_keep:, 3 * BW:].reshape(bg, n_keep, HEADS, HEAD_DIM))
            pps.append(ps_p)
        outs["pk"].append(jnp.concatenate(pk, axis=0))
        outs["pv"].append(jnp.concatenate(pv, axis=0))
        outs["pps"].append(jnp.concatenate(pps, axis=0))
        xs, mid_s, ps_s, vn_s = _trunk_layer(xs, p_sample[i].reshape(bs * ls, PLE_DIM), lw, bs, ls, state_pool[i],
                                             PAST_LEN, (cache_attn_k[i], cache_attn_v[i]), final)
        mid_s = mid_s.reshape(bs, ls, 4 * BW)
        outs["sk"].append(mid_s[:, :, 2 * BW:3 * BW].reshape(bs, ls, HEADS, HEAD_DIM))
        outs["sv"].append(mid_s[:, :, 3 * BW:].reshape(bs, ls, HEADS, HEAD_DIM))
        outs["sps"].append(ps_s)
        outs["sgv"].append(vn_s.reshape(bs, ls, BW))
    st = lambda k: jnp.stack(outs[k])
    y_prompt = jnp.concatenate([x.reshape(bg, lp, D_MODEL) for x in xg], axis=0)
    return (y_prompt, xs.reshape(bs, ls, D_MODEL), st("pk"), st("pv"), st("pps"),
            st("sk"), st("sv"), st("sps"), st("sgv"))
```

```python
import functools
import math

import jax
import jax.numpy as jnp
import numpy as np
from jax import lax
from jax.experimental import pallas as pl
from jax.experimental.pallas import tpu as pltpu
from jax.experimental.pallas import tpu_sc as plsc

F32 = jnp.float32
BF16 = jnp.bfloat16

D_MODEL = 1024
DEPTH = 2
CHUNK = 64
EPS = 1e-6
BW = D_MODEL // 2
GMLP_CHUNK = 128
GROUPS = 4
GDIM = BW // GROUPS
POOL_WINDOWS = (2, 4, 8, 16)
POOL_STATE = 15
POOL_PAD = 16
HEADS = 8
HEAD_DIM = BW // HEADS
BAND_CHUNKS = 8
BAND_PAST = BAND_CHUNKS * CHUNK
REL_CLIP = 128
PAST_LEN = 4096
PEER_HEADS = 8
PEER_NKEYS = 128
PEER_HALF = 128
PEER_TOPK = 16
PEER_SLOTS = PEER_HEADS * PEER_TOPK
PLE_DIM = 256

LANES = 128
VMEM_LIMIT = 56 * 1024 * 1024
NEG = -1e30


def _cparams(*sem):
    return pltpu.CompilerParams(dimension_semantics=sem, vmem_limit_bytes=VMEM_LIMIT)


def _rms(x, g):
    ms = jnp.mean(x * x, axis=-1, keepdims=True)
    return x * lax.rsqrt(ms + EPS) * g


def _gelu(x):
    c = math.sqrt(2.0 / math.pi)
    return 0.5 * x * (1.0 + jnp.tanh(c * (x + 0.044715 * (x * x * x))))


def _sigmoid(x):
    return 1.0 / (1.0 + jnp.exp(-x))


_ACTS = {"gelu": _gelu, "sigmoid": _sigmoid, "none": lambda z: z}


def _norm_matmul_kernel(x_ref, g_ref, w_ref, o_ref, h_ref, *, act):
    @pl.when(pl.program_id(1) == 0)
    def _():
        h_ref[...] = _rms(x_ref[...], g_ref[...]).astype(BF16)

    z = jnp.dot(h_ref[...], w_ref[...], preferred_element_type=F32)
    o_ref[...] = _ACTS[act](z).astype(o_ref.dtype)


def norm_matmul(x, g, w, act, out_dtype, tn):
    t, d = x.shape
    n = w.shape[1]
    tm = min(t, 512)
    return pl.pallas_call(
        functools.partial(_norm_matmul_kernel, act=act),
        grid=(t // tm, n // tn),
        in_specs=[
            pl.BlockSpec((tm, d), lambda i, j: (i, 0)),
            pl.BlockSpec((1, d), lambda i, j: (0, 0)),
            pl.BlockSpec((d, tn), lambda i, j: (0, j)),
        ],
        out_specs=pl.BlockSpec((tm, tn), lambda i, j: (i, j)),
        out_shape=jax.ShapeDtypeStruct((t, n), out_dtype),
        scratch_shapes=[pltpu.VMEM((tm, d), BF16)],
        compiler_params=_cparams("parallel", "arbitrary"),
        name="norm_matmul_" + act,
    )(x, g.reshape(1, d), w)


def _gmlp_kernel(uv_ref, lng_ref, lnb_ref, ws_ref, bst_ref, y_ref, vn_ref, *, lc):
    u = uv_ref[:, :BW]
    v = uv_ref[:, BW:]
    mu = jnp.mean(v, axis=-1, keepdims=True)
    vc = v - mu
    var = jnp.mean(vc * vc, axis=-1, keepdims=True)
    vn = vc * lax.rsqrt(var + EPS) * lng_ref[...] + lnb_ref[...]
    vn_ref[...] = vn
    row = lax.broadcasted_iota(jnp.int32, (lc, lc), 0) // CHUNK
    col = lax.broadcasted_iota(jnp.int32, (lc, lc), 1) // CHUNK
    causal = col <= row
    vnb = vn.astype(BF16)
    for g in range(GROUPS):
        w = jnp.where(causal, ws_ref[g], 0.0).astype(BF16)
        s = jnp.dot(w, vnb[:, g * GDIM:(g + 1) * GDIM], preferred_element_type=F32)
        s = s + bst_ref[:, g:g + 1]
        y_ref[:, g * GDIM:(g + 1) * GDIM] = (u[:, g * GDIM:(g + 1) * GDIM] * s).astype(y_ref.dtype)


def gmlp_mix(uv, ln_g, ln_b, ws, bs, lc):
    t = uv.shape[0]
    return pl.pallas_call(
        functools.partial(_gmlp_kernel, lc=lc),
        grid=(t // lc,),
        in_specs=[
            pl.BlockSpec((lc, 2 * BW), lambda i: (i, 0)),
            pl.BlockSpec((1, BW), lambda i: (0, 0)),
            pl.BlockSpec((1, BW), lambda i: (0, 0)),
            pl.BlockSpec((GROUPS, lc, lc), lambda i: (0, 0, 0)),
            pl.BlockSpec((lc, GROUPS), lambda i: (0, 0)),
        ],
        out_specs=[
            pl.BlockSpec((lc, BW), lambda i: (i, 0)),
            pl.BlockSpec((lc, BW), lambda i: (i, 0)),
        ],
        out_shape=[
            jax.ShapeDtypeStruct((t, BW), BF16),
            jax.ShapeDtypeStruct((t, BW), F32),
        ],
        compiler_params=_cparams("parallel"),
        name="gmlp_mix",
    )(uv, ln_g.reshape(1, BW), ln_b.reshape(1, BW), ws[:, :lc, :lc], bs[:, :lc].T)


def _pool_kernel(u_ref, pre_ref, w_ref, sc_ref, y_ref, st_ref, pad_ref, *, seq, start_pos):
    pad_ref[0:POOL_PAD, :] = pre_ref[...]
    pad_ref[POOL_PAD:, :] = u_ref[...]
    pos = lax.broadcasted_iota(jnp.int32, (seq, 1), 0) + start_pos
    for g, win in enumerate(POOL_WINDOWS):
        cols = slice(g * GDIM, (g + 1) * GDIM)
        tok = pad_ref[POOL_PAD:, cols]
        acc = tok
        for k in range(1, win):
            acc = acc + pad_ref[POOL_PAD - k:POOL_PAD - k + seq, cols]
        cnt = jnp.minimum(pos + 1, win).astype(F32)
        d = acc / cnt - tok
        y = jnp.dot(d.astype(BF16), w_ref[g], preferred_element_type=F32)
        y_ref[:, cols] = (y * sc_ref[:, cols]).astype(y_ref.dtype)
    st_ref[...] = pad_ref[seq + 1:seq + POOL_PAD, :]


def pool_mix(mid, prefix, start_pos, pool_w, pool_scale, batch, seq):
    mid3 = mid.reshape(batch, seq, 4 * BW)
    pre = jnp.concatenate([jnp.zeros((batch, 1, BW), F32), prefix], axis=1)
    y, st = pl.pallas_call(
        functools.partial(_pool_kernel, seq=seq, start_pos=start_pos),
        grid=(batch,),
        in_specs=[
            pl.BlockSpec((None, seq, BW), lambda b: (b, 0, 0)),
            pl.BlockSpec((None, POOL_PAD, BW), lambda b: (b, 0, 0)),
            pl.BlockSpec((GROUPS, GDIM, GDIM), lambda b: (0, 0, 0)),
            pl.BlockSpec((1, BW), lambda b: (0, 0)),
        ],
        out_specs=[
            pl.BlockSpec((None, seq, BW), lambda b: (b, 0, 0)),
            pl.BlockSpec((None, POOL_STATE, BW), lambda b: (b, 0, 0)),
        ],
        out_shape=[
            jax.ShapeDtypeStruct((batch, seq, BW), BF16),
            jax.ShapeDtypeStruct((batch, POOL_STATE, BW), F32),
        ],
        scratch_shapes=[pltpu.VMEM((seq + POOL_PAD, BW), F32)],
        compiler_params=_cparams("parallel"),
        name="pool_mix",
    )(mid3, pre, pool_w.astype(BF16), pool_scale.reshape(1, BW))
    return y.reshape(batch * seq, BW), st


def _attn_chunks(q_ref, kcat_ref, vcat_ref, bias_ref, o_ref, *, n_chunks, cq, band, first_block):
    scale = HEAD_DIM ** -0.5
    for h in range(HEADS):
        cols = slice(h * HEAD_DIM, (h + 1) * HEAD_DIM)
        bias = bias_ref[h]
        for i in range(n_chunks):
            q = (q_ref[i * cq:(i + 1) * cq, cols] * scale).astype(BF16)
            k = kcat_ref[i * cq:i * cq + band, cols]
            v = vcat_ref[i * cq:i * cq + band, cols]
            s = lax.dot_general(q, k, (((1,), (1,)), ((), ())), preferred_element_type=F32) + bias
            if first_block is not None:
                key = lax.broadcasted_iota(jnp.int32, (1, band), 1)
                s = jnp.where(key >= first_block * (BAND_PAST - i * cq), s, NEG)
            m = jnp.max(s, axis=-1, keepdims=True)
            p = jnp.exp(s - m)
            l = jnp.sum(p, axis=-1, keepdims=True)
            o = jnp.dot(p.astype(BF16), v, preferred_element_type=F32) / l
            o_ref[i * cq:(i + 1) * cq, cols] = o.astype(o_ref.dtype)


def _attn_prompt_kernel(q_ref, kp_ref, ko_ref, vp_ref, vo_ref, bias_ref, o_ref, kcat_ref, vcat_ref):
    kcat_ref[0:BAND_PAST, :] = kp_ref[...].astype(BF16)
    kcat_ref[BAND_PAST:, :] = ko_ref[...].astype(BF16)
    vcat_ref[0:BAND_PAST, :] = vp_ref[...].astype(BF16)
    vcat_ref[BAND_PAST:, :] = vo_ref[...].astype(BF16)
    _attn_chunks(q_ref, kcat_ref, vcat_ref, bias_ref, o_ref, n_chunks=BAND_CHUNKS, cq=CHUNK,
                 band=BAND_PAST + CHUNK, first_block=(pl.program_id(1) == 0).astype(jnp.int32))


def _rel_bias_tile(rel_bias, qpos, kpos):
    rel = np.clip(qpos[:, None] - kpos[None, :], -REL_CLIP, REL_CLIP) + REL_CLIP
    return rel_bias[:, rel]


def attn_prompt(mid, rel_bias, batch, seq):
    mid3 = mid.reshape(batch, seq, 4 * BW)
    blk = BAND_PAST
    bias = _rel_bias_tile(rel_bias, np.arange(CHUNK), np.arange(BAND_PAST + CHUNK) - BAND_PAST)
    prev = lambda b, j: jnp.maximum(j - 1, 0)
    y = pl.pallas_call(
        _attn_prompt_kernel,
        grid=(batch, seq // blk),
        in_specs=[
            pl.BlockSpec((None, blk, BW), lambda b, j: (b, j, 1)),
            pl.BlockSpec((None, blk, BW), lambda b, j: (b, prev(b, j), 2)),
            pl.BlockSpec((None, blk, BW), lambda b, j: (b, j, 2)),
            pl.BlockSpec((None, blk, BW), lambda b, j: (b, prev(b, j), 3)),
            pl.BlockSpec((None, blk, BW), lambda b, j: (b, j, 3)),
            pl.BlockSpec((HEADS, CHUNK, BAND_PAST + CHUNK), lambda b, j: (0, 0, 0)),
        ],
        out_specs=pl.BlockSpec((None, blk, BW), lambda b, j: (b, j, 0)),
        out_shape=jax.ShapeDtypeStruct((batch, seq, BW), BF16),
        scratch_shapes=[pltpu.VMEM((2 * blk, BW), BF16), pltpu.VMEM((2 * blk, BW), BF16)],
        compiler_params=_cparams("parallel", "parallel"),
        name="attn_prompt",
    )(mid3, mid3, mid3, mid3, mid3, bias)
    return y.reshape(batch * seq, BW)


def _attn_sample_kernel(q_ref, kc_ref, kn_ref, vc_ref, vn_ref, bias_ref, o_ref, kcat_ref, vcat_ref, *, n_cache):
    kcat_ref[0:n_cache, :] = kc_ref[...].astype(BF16)
    kcat_ref[n_cache:, :] = kn_ref[...].astype(BF16)
    vcat_ref[0:n_cache, :] = vc_ref[...].astype(BF16)
    vcat_ref[n_cache:, :] = vn_ref[...].astype(BF16)
    seq = q_ref.shape[0]
    _attn_chunks(q_ref, kcat_ref, vcat_ref, bias_ref, o_ref, n_chunks=1, cq=seq, band=n_cache + seq,
                 first_block=None)


def attn_sample(mid, cache_k, cache_v, rel_bias, batch, seq):
    n_cache = cache_k.shape[1]
    assert PAST_LEN >= n_cache
    mid3 = mid.reshape(batch, seq, 4 * BW)
    ck = cache_k.reshape(batch, n_cache, BW)
    cv = cache_v.reshape(batch, n_cache, BW)
    bias = _rel_bias_tile(rel_bias, PAST_LEN + np.arange(seq), PAST_LEN - n_cache + np.arange(n_cache + seq))
    y = pl.pallas_call(
        functools.partial(_attn_sample_kernel, n_cache=n_cache),
        grid=(batch,),
        in_specs=[
            pl.BlockSpec((None, seq, BW), lambda b: (b, 0, 1)),
            pl.BlockSpec((None, n_cache, BW), lambda b: (b, 0, 0)),
            pl.BlockSpec((None, seq, BW), lambda b: (b, 0, 2)),
            pl.BlockSpec((None, n_cache, BW), lambda b: (b, 0, 0)),
            pl.BlockSpec((None, seq, BW), lambda b: (b, 0, 3)),
            pl.BlockSpec((HEADS, seq, n_cache + seq), lambda b: (0, 0, 0)),
        ],
        out_specs=pl.BlockSpec((None, seq, BW), lambda b: (b, 0, 0)),
        out_shape=jax.ShapeDtypeStruct((batch, seq, BW), BF16),
        scratch_shapes=[pltpu.VMEM((n_cache + seq, BW), BF16), pltpu.VMEM((n_cache + seq, BW), BF16)],
        compiler_params=_cparams("parallel"),
        name="attn_sample",
    )(mid3, ck, mid3, cv, mid3, bias)
    return y.reshape(batch * seq, BW)


def _mixer_out_kernel(x_ref, ya_ref, yb_ref, yc_ref, gate_ref, wb_ref, wo_ref, o_ref):
    acc = None
    for n, y_ref in enumerate((ya_ref, yb_ref, yc_ref)):
        proj = jnp.dot(y_ref[...], wb_ref[n], preferred_element_type=F32)
        term = gate_ref[:, n * D_MODEL:(n + 1) * D_MODEL].astype(F32) * proj
        acc = term if acc is None else acc + term
    o_ref[...] = x_ref[...] + jnp.dot(acc.astype(BF16), wo_ref[...], preferred_element_type=F32)


def mixer_out(x, ya, yb, yc, gates, w_branch, w_out):
    t = x.shape[0]
    tm = min(t, 512)
    row = lambda i: (i, 0)
    return pl.pallas_call(
        _mixer_out_kernel,
        grid=(t // tm,),
        in_specs=[
            pl.BlockSpec((tm, D_MODEL), row),
            pl.BlockSpec((tm, BW), row),
            pl.BlockSpec((tm, BW), row),
            pl.BlockSpec((tm, BW), row),
            pl.BlockSpec((tm, 3 * D_MODEL), row),
            pl.BlockSpec((3, BW, D_MODEL), lambda i: (0, 0, 0)),
            pl.BlockSpec((D_MODEL, D_MODEL), lambda i: (0, 0)),
        ],
        out_specs=pl.BlockSpec((tm, D_MODEL), row),
        out_shape=jax.ShapeDtypeStruct((t, D_MODEL), F32),
        compiler_params=_cparams("parallel"),
        name="mixer_out",
    )(x, ya, yb, yc, gates, w_branch, w_out)


def _extract_top(s, payload, k):
    r = float(s.shape[0])
    rows = lax.broadcasted_iota(jnp.int32, s.shape, 0).astype(F32)
    vals, pays = [], []
    for _ in range(k):
        m = jnp.max(s, axis=0, keepdims=True)
        idx = jnp.min(jnp.where(s == m, rows, r), axis=0, keepdims=True)
        sel = rows == idx
        vals.append(m)
        pays.append(idx if payload is None else jnp.max(jnp.where(sel, payload, -1.0), axis=0, keepdims=True))
        s = jnp.where(sel, -jnp.inf, s)
    return jnp.concatenate(vals, axis=0), jnp.concatenate(pays, axis=0)


def _peer_query_kernel(x_ref, g_ref, wq_ref, sk_ref, hp_ref, idx_ref, gate_ref, q_ref, *, tm):
    hb = _rms(x_ref[...], g_ref[...]).astype(BF16)
    q_ref[...] = jnp.dot(hb, wq_ref[...], preferred_element_type=F32).astype(BF16)
    bits = lax.bitcast_convert_type(hb.astype(F32), jnp.int32)
    half = D_MODEL // 2
    hp_ref[...] = bits[:, half:] | lax.shift_right_logical(bits[:, :half], 16)

    def sub_block(sb, carry):
        tok = pl.ds(pl.multiple_of(sb * LANES, LANES), LANES)
        for hd in range(PEER_HEADS):
            sv, si = [], []
            for p in range(2):
                hp = hd * 2 + p
                q = q_ref[tok, hp * PEER_HALF:(hp + 1) * PEER_HALF]
                s = lax.dot_general(sk_ref[hp], q, (((1,), (1,)), ((), ())), preferred_element_type=F32)
                v, i = _extract_top(s, None, PEER_TOPK)
                sv.append(v)
                si.append(i)
            cand = jnp.concatenate([sv[0][a:a + 1] + sv[1] for a in range(PEER_TOPK)], axis=0)
            eid = jnp.concatenate([si[0][a:a + 1] * PEER_NKEYS + si[1] for a in range(PEER_TOPK)], axis=0)
            tv, te = _extract_top(cand, eid, PEER_TOPK)
            e = jnp.exp(tv - tv[0:1])
            gate = e / jnp.sum(e, axis=0, keepdims=True)
            idx_ref[sb, hd * PEER_TOPK:(hd + 1) * PEER_TOPK, :] = te.astype(jnp.int32)
            gate_ref[sb, hd * PEER_TOPK:(hd + 1) * PEER_TOPK, :] = gate
        return carry

    lax.fori_loop(0, tm // LANES, sub_block, 0)


def peer_query(x, norm_g, wq, subkeys):
    t = x.shape[0]
    tm = min(t, 512)
    nq = wq.shape[1]
    nsb = tm // LANES
    return pl.pallas_call(
        functools.partial(_peer_query_kernel, tm=tm),
        grid=(t // tm,),
        in_specs=[
            pl.BlockSpec((tm, D_MODEL), lambda i: (i, 0)),
            pl.BlockSpec((1, D_MODEL), lambda i: (0, 0)),
            pl.BlockSpec((D_MODEL, nq), lambda i: (0, 0)),
            pl.BlockSpec((2 * PEER_HEADS, PEER_NKEYS, PEER_HALF), lambda i: (0, 0, 0)),
        ],
        out_specs=[
            pl.BlockSpec((tm, D_MODEL // 2), lambda i: (i, 0)),
            pl.BlockSpec((nsb, PEER_SLOTS, LANES), lambda i: (i, 0, 0)),
            pl.BlockSpec((nsb, PEER_SLOTS, LANES), lambda i: (i, 0, 0)),
        ],
        out_shape=[
            jax.ShapeDtypeStruct((t, D_MODEL // 2), jnp.int32),
            jax.ShapeDtypeStruct((t // LANES, PEER_SLOTS, LANES), jnp.int32),
            jax.ShapeDtypeStruct((t // LANES, PEER_SLOTS, LANES), F32),
        ],
        scratch_shapes=[pltpu.VMEM((tm, nq), BF16)],
        compiler_params=_cparams("parallel"),
        name="peer_query",
    )(x, norm_g.reshape(1, D_MODEL), wq, subkeys)


HALF_D = D_MODEL // 2
SC_LANES = 16
SC_UNIT_ROWS = 32
SC_UNITS_PER_TOKEN = PEER_SLOTS // SC_UNIT_ROWS
SC_MAX_TOKEN_BLOCK = 32
SC_ROW_GROUP = 8
SC_HALF_VECS = HALF_D // SC_LANES
SC_OUT_VECS = 8
SC_BF16_TERMS = 4


def pack_expert_tables(peer_u, peer_v):
    def pack(x):
        b = lax.bitcast_convert_type(x.astype(BF16), jnp.uint16).astype(jnp.uint32)
        return (b[:, HALF_D:] << 16) | b[:, :HALF_D]

    return lax.bitcast_convert_type(jnp.concatenate([pack(peer_u), pack(peer_v)], axis=1), jnp.int32)


def sc_peer_experts(table, idx, gates, hp):
    t = hp.shape[0]
    info = plsc.get_sparse_core_info()
    n_workers = info.num_cores * info.num_subcores
    tpw = t // n_workers
    token_block = min(tpw, SC_MAX_TOKEN_BLOCK)
    assert t % (n_workers * token_block) == 0
    units = token_block * SC_UNITS_PER_TOKEN
    mesh = plsc.VectorSubcoreMesh(core_axis_name="core", subcore_axis_name="subcore")
    hi_mask = jnp.int32(-65536)
    gelu_c = math.sqrt(2.0 / math.pi)

    @functools.partial(
        pl.kernel,
        out_type=jax.ShapeDtypeStruct((t, D_MODEL), F32),
        mesh=mesh,
        scratch_types=[
            pltpu.VMEM((token_block * PEER_SLOTS,), jnp.int32),
            pltpu.VMEM((token_block * PEER_SLOTS,), F32),
            pltpu.VMEM((token_block, HALF_D), jnp.int32),
            pltpu.VMEM((token_block, D_MODEL), F32),
            pltpu.VMEM((2, SC_UNIT_ROWS, D_MODEL), jnp.int32),
            pltpu.VMEM((SC_UNIT_ROWS, SC_LANES), F32),
            pltpu.VMEM((SC_UNIT_ROWS,), jnp.int32),
            pltpu.SemaphoreType.DMA((2,)),
        ],
        compiler_params=pltpu.CompilerParams(needs_layout_passes=False),
        name="peer_sc_experts",
    )
    def kern(table_hbm, idx_hbm, gate_hbm, h_hbm, out_hbm, idx_v, gate_v, h_v, out_v, rows_v, part_v, coef_v, sem):
        wid = lax.axis_index("subcore") * info.num_cores + lax.axis_index("core")
        lane = lax.iota(jnp.int32, SC_LANES)
        zero = jnp.zeros((SC_LANES,), F32)

        def gather(unit, b):
            rows = idx_v.at[pl.ds(unit * SC_UNIT_ROWS, SC_UNIT_ROWS)]
            return pltpu.make_async_copy(table_hbm.at[rows], rows_v.at[b], sem.at[b])

        def unpack(w):
            return lax.bitcast_convert_type(w << 16, F32), lax.bitcast_convert_type(w & hi_mask, F32)

        def as_pairs(w):
            return plsc.bitcast(w, BF16)

        def compute(unit, b):
            tl = unit // SC_UNITS_PER_TOKEN
            q = unit % SC_UNITS_PER_TOKEN

            def row_group(rg, carry):
                def kstep(k, accs):
                    hs = []
                    for j in range(SC_BF16_TERMS):
                        off = pl.multiple_of((k * SC_BF16_TERMS + j) * SC_LANES, SC_LANES)
                        hs.append(as_pairs(h_v[tl, pl.ds(off, SC_LANES)]))
                    new = []
                    for r in range(SC_ROW_GROUP):
                        p = None
                        for j in range(SC_BF16_TERMS):
                            off = pl.multiple_of((k * SC_BF16_TERMS + j) * SC_LANES, SC_LANES)
                            term = as_pairs(rows_v[b, rg * SC_ROW_GROUP + r, pl.ds(off, SC_LANES)]) * hs[j]
                            p = term if p is None else p + term
                        lo, hi = unpack(plsc.bitcast(p, jnp.int32))
                        new.append(accs[r] + lo + hi)
                    return tuple(new)

                accs = lax.fori_loop(0, SC_HALF_VECS // SC_BF16_TERMS, kstep, (zero,) * SC_ROW_GROUP)
                for r in range(SC_ROW_GROUP):
                    part_v[rg * SC_ROW_GROUP + r, :] = accs[r]
                return carry

            lax.fori_loop(0, SC_UNIT_ROWS // SC_ROW_GROUP, row_group, 0)

            for i in range(SC_UNIT_ROWS // SC_LANES):
                rows = lane + i * SC_LANES
                a = zero
                for l in range(SC_LANES):
                    a = a + plsc.load_gather(part_v, [rows, jnp.full((SC_LANES,), l, jnp.int32)])
                z = gelu_c * (a + 0.044715 * (a * a * a))
                act = a / (1.0 + jnp.exp(-2.0 * z))
                slot = pl.multiple_of(tl * PEER_SLOTS + q * SC_UNIT_ROWS + i * SC_LANES, SC_LANES)
                bits = lax.bitcast_convert_type(gate_v[pl.ds(slot, SC_LANES)] * act, jnp.int32)
                top = (bits + 0x7FFF + ((bits >> 16) & 1)) & hi_mask
                coef_v[pl.ds(i * SC_LANES, SC_LANES)] = top | lax.shift_right_logical(top, 16)

            def out_pass(dq, carry):
                def row_quad(rq, accs):
                    cs = []
                    for j in range(SC_BF16_TERMS):
                        row = jnp.full((SC_LANES,), rq * SC_BF16_TERMS + j, jnp.int32)
                        cs.append(as_pairs(plsc.load_gather(coef_v, [row])))
                    new = []
                    for k in range(SC_OUT_VECS):
                        off = pl.multiple_of((dq * SC_OUT_VECS + k) * SC_LANES, SC_LANES)
                        p = None
                        for j in range(SC_BF16_TERMS):
                            w = rows_v[b, rq * SC_BF16_TERMS + j, pl.ds(HALF_D + off, SC_LANES)]
                            term = as_pairs(w) * cs[j]
                            p = term if p is None else p + term
                        lo, hi = unpack(plsc.bitcast(p, jnp.int32))
                        new.append(accs[2 * k] + lo)
                        new.append(accs[2 * k + 1] + hi)
                    return tuple(new)

                accs = lax.fori_loop(0, SC_UNIT_ROWS // SC_BF16_TERMS, row_quad, (zero,) * (2 * SC_OUT_VECS))
                for k in range(SC_OUT_VECS):
                    off = pl.multiple_of((dq * SC_OUT_VECS + k) * SC_LANES, SC_LANES)
                    plsc.addupdate(out_v.at[tl, pl.ds(off, SC_LANES)], accs[2 * k])
                    plsc.addupdate(out_v.at[tl, pl.ds(HALF_D + off, SC_LANES)], accs[2 * k + 1])
                return carry

            lax.fori_loop(0, SC_HALF_VECS // SC_OUT_VECS, out_pass, 0)

        @pl.loop(0, tpw // token_block)
        def _(blk):
            tok0 = wid * tpw + blk * token_block
            slots = pl.ds(tok0 * PEER_SLOTS, token_block * PEER_SLOTS)
            pltpu.sync_copy(idx_hbm.at[slots], idx_v)
            pltpu.sync_copy(gate_hbm.at[slots], gate_v)
            pltpu.sync_copy(h_hbm.at[pl.ds(tok0, token_block)], h_v)

            @pl.loop(0, token_block)
            def _(tl):
                @pl.loop(0, D_MODEL // SC_LANES)
                def _(k):
                    out_v[tl, pl.ds(pl.multiple_of(k * SC_LANES, SC_LANES), SC_LANES)] = zero

            gather(0, 0).start()

            @pl.loop(0, units, step=2)
            def _(u0):
                for b in range(2):
                    unit = u0 + b

                    @pl.when(unit + 1 < units)
                    def _():
                        gather(unit + 1, 1 - b).start()

                    gather(unit, b).wait()
                    compute(unit, b)

            pltpu.sync_copy(out_v, out_hbm.at[pl.ds(tok0, token_block)])

    return kern(table, idx, gates, hp)


def peer_block(x, norm_g, wq, subkeys, table):
    t = x.shape[0]
    hp, idx3, gate3 = peer_query(x, norm_g, wq, subkeys)
    idx = jnp.transpose(idx3, (0, 2, 1)).reshape(t * PEER_SLOTS)
    gates = jnp.transpose(gate3, (0, 2, 1)).reshape(t * PEER_SLOTS)
    return sc_peer_experts(table, idx, gates, hp)


def _ple_kernel(x_ref, ffn_ref, p_ref, g_ref, wg_ref, wp_ref, gf_ref, o_ref, *, final):
    x = x_ref[...] + ffn_ref[...]
    gate = _sigmoid(jnp.dot(_rms(x, g_ref[...]).astype(BF16), wg_ref[...], preferred_element_type=F32))
    emb = jnp.dot(p_ref[...].astype(BF16), wp_ref[...], preferred_element_type=F32)
    y = x + gate * emb
    o_ref[...] = _rms(y, gf_ref[...]) if final else y


def ple_block(x, ffn, p, norm_g, w_gate, w_proj, norm_final, final):
    t = x.shape[0]
    tm = min(t, 512)
    row = lambda i: (i, 0)
    full = lambda i: (0, 0)
    return pl.pallas_call(
        functools.partial(_ple_kernel, final=final),
        grid=(t // tm,),
        in_specs=[
            pl.BlockSpec((tm, D_MODEL), row),
            pl.BlockSpec((tm, D_MODEL), row),
            pl.BlockSpec((tm, PLE_DIM), row),
            pl.BlockSpec((1, D_MODEL), full),
            pl.BlockSpec((D_MODEL, D_MODEL), full),
            pl.BlockSpec((PLE_DIM, D_MODEL), full),
            pl.BlockSpec((1, D_MODEL), full),
        ],
        out_specs=pl.BlockSpec((tm, D_MODEL), row),
        out_shape=jax.ShapeDtypeStruct((t, D_MODEL), F32),
        compiler_params=_cparams("parallel"),
        name="ple_block",
    )(x, ffn, p, norm_g.reshape(1, D_MODEL), w_gate, w_proj, norm_final.reshape(1, D_MODEL))


def _prompt_groups(batch):
    sizes = []
    while sum(sizes) < batch:
        nxt = 1 if len(sizes) < 2 else -(-sizes[-1] * 7 // 5)
        sizes.append(min(nxt, batch - sum(sizes)))
    return sizes
def _trunk_layer(x, ple, lw, batch, seq, pool_prefix, pool_start, cache, final):
    uv = norm_matmul(x, lw["norm_mix"], lw["w_uv"], "gelu", F32, 2 * BW)
    mid = norm_matmul(x, lw["norm_mix"], lw["w_mid"], "none", F32, 4 * BW)
    gates = norm_matmul(x, lw["norm_mix"], lw["w_gates"], "sigmoid", BF16, D_MODEL)
    ya, vn = gmlp_mix(uv, lw["gmlp_ln_g"], lw["gmlp_ln_b"], lw["gmlp_ws"], lw["gmlp_bs"], min(seq, GMLP_CHUNK))
    yb, pool_state = pool_mix(mid, pool_prefix, pool_start, lw["pool_w"], lw["pool_scale"], batch, seq)
    if cache is None:
        yc = attn_prompt(mid, lw["rel_bias"], batch, seq)
    else:
        yc = attn_sample(mid, cache[0], cache[1], lw["rel_bias"], batch, seq)
    x = mixer_out(x, ya, yb, yc, gates, lw["w_branch"], lw["w_out"])
    ffn = peer_block(x, lw["norm_ffn"], lw["peer_wq"], lw["peer_subkeys"], lw["peer_table"])
    x = ple_block(x, ffn, ple, lw["norm_ple"], lw["ple_gate"], lw["ple_proj"], lw["norm_final"], final)
    return x, mid, pool_state, vn


def kernel(x_prompt, x_sample, cache_attn_k, cache_attn_v, state_pool, p_prompt, p_sample, norm_mix, w_in, gmlp_ln_g, gmlp_ln_b, gmlp_ws, gmlp_bs, pool_w, pool_scale, attn_rel_bias, w_branch, w_out, norm_ffn, peer_wq, peer_subkeys, peer_u, peer_v, norm_ple, ple_gate, ple_proj, norm_final):
    bp, lp, _ = x_prompt.shape
    bs, ls, _ = x_sample.shape
    assert lp % BAND_PAST == 0 and lp % GMLP_CHUNK == 0 and ls <= CHUNK
    n_keep = min(BAND_PAST, lp)
    sizes = _prompt_groups(bp)
    starts = [sum(sizes[:g]) for g in range(len(sizes))]
    xg = [x_prompt[a:a + n].reshape(n * lp, D_MODEL) for a, n in zip(starts, sizes)]
    xs = x_sample.reshape(bs * ls, D_MODEL)
    outs = {k: [] for k in ("pk", "pv", "pps", "sk", "sv", "sps", "sgv")}
    for i in range(DEPTH):
        w_in_b = w_in[i].astype(BF16)
        lw = dict(
            norm_mix=norm_mix[i],
            w_uv=w_in_b[:, :2 * BW],
            w_mid=w_in_b[:, 2 * BW:6 * BW],
            w_gates=w_in_b[:, 6 * BW:],
            gmlp_ln_g=gmlp_ln_g[i], gmlp_ln_b=gmlp_ln_b[i], gmlp_ws=gmlp_ws[i], gmlp_bs=gmlp_bs[i],
            pool_w=pool_w[i], pool_scale=pool_scale[i], rel_bias=attn_rel_bias[i],
            w_branch=w_branch[i].astype(BF16), w_out=w_out[i].astype(BF16),
            norm_ffn=norm_ffn[i], peer_wq=peer_wq[i].astype(BF16),
            peer_subkeys=peer_subkeys[i].reshape(2 * PEER_HEADS, PEER_NKEYS, PEER_HALF).astype(BF16),
            peer_table=pack_expert_tables(peer_u[i], peer_v[i]),
            norm_ple=norm_ple[i], ple_gate=ple_gate[i].astype(BF16), ple_proj=ple_proj[i].astype(BF16),
            norm_final=norm_final,
        )
        final = i == DEPTH - 1
        pk, pv, pps = [], [], []
        for g, (a, bg) in enumerate(zip(starts, sizes)):
            ple_g = p_prompt[i, a:a + bg].reshape(bg * lp, PLE_DIM)
            zero_prefix = jnp.zeros((bg, POOL_STATE, BW), F32)
            xg[g], mid_p, ps_p, _ = _trunk_layer(xg[g], ple_g, lw, bg, lp, zero_prefix, 0, None, final)
            mid_p = mid_p.reshape(bg, lp, 4 * BW)
            pk.append(mid_p[:, lp - n_keep:, 2 * BW:3 * BW].reshape(bg, n_keep, HEADS, HEAD_DIM))
            pv.append(mid_p[:, lp - n_keep:, 3 * BW:].reshape(bg, n_keep, HEADS, HEAD_DIM))
            pps.append(ps_p)
        outs["pk"].append(jnp.concatenate(pk, axis=0))
        outs["pv"].append(jnp.concatenate(pv, axis=0))
        outs["pps"].append(jnp.concatenate(pps, axis=0))
        xs, mid_s, ps_s, vn_s = _trunk_layer(xs, p_sample[i].reshape(bs * ls, PLE_DIM), lw, bs, ls, state_pool[i],
                                             PAST_LEN, (cache_attn_k[i], cache_attn_v[i]), final)
        mid_s = mid_s.reshape(bs, ls, 4 * BW)
        outs["sk"].append(mid_s[:, :, 2 * BW:3 * BW].reshape(bs, ls, HEADS, HEAD_DIM))
        outs["sv"].append(mid_s[:, :, 3 * BW:].reshape(bs, ls, HEADS, HEAD_DIM))
        outs["sps"].append(ps_s)
        outs["sgv"].append(vn_s.reshape(bs, ls, BW))
    st = lambda k: jnp.stack(outs[k])
    y_prompt = jnp.concatenate(xg, axis=0).reshape(bp, lp, D_MODEL)
    return (y_prompt, xs.reshape(bs, ls, D_MODEL), st("pk"), st("pv"), st("pps"),
            st("sk"), st("sv"), st("sps"), st("sgv"))
```

```python
import functools
import math

import jax
import jax.numpy as jnp
import numpy as np
from jax import lax
from jax.experimental import pallas as pl
from jax.experimental.pallas import tpu as pltpu
from jax.experimental.pallas import tpu_sc as plsc

F32 = jnp.float32
BF16 = jnp.bfloat16

D_MODEL = 1024
DEPTH = 2
CHUNK = 64
EPS = 1e-6
BW = D_MODEL // 2
GMLP_CHUNK = 128
GROUPS = 4
GDIM = BW // GROUPS
POOL_WINDOWS = (2, 4, 8, 16)
POOL_STATE = 15
POOL_PAD = 16
HEADS = 8
HEAD_DIM = BW // HEADS
BAND_CHUNKS = 8
BAND_PAST = BAND_CHUNKS * CHUNK
REL_CLIP = 128
PAST_LEN = 4096
PEER_HEADS = 8
PEER_NKEYS = 128
PEER_HALF = 128
PEER_TOPK = 16
PEER_SLOTS = PEER_HEADS * PEER_TOPK
PLE_DIM = 256

LANES = 128
VMEM_LIMIT = 56 * 1024 * 1024
NEG = -1e30


def _cparams(*sem):
    return pltpu.CompilerParams(dimension_semantics=sem, vmem_limit_bytes=VMEM_LIMIT)


def _rms(x, g):
    ms = jnp.mean(x * x, axis=-1, keepdims=True)
    return x * lax.rsqrt(ms + EPS) * g


def _gelu(x):
    c = math.sqrt(2.0 / math.pi)
    return 0.5 * x * (1.0 + jnp.tanh(c * (x + 0.044715 * (x * x * x))))


def _sigmoid(x):
    return 1.0 / (1.0 + jnp.exp(-x))


_ACTS = {"gelu": _gelu, "sigmoid": _sigmoid, "none": lambda z: z}


def _norm_matmul_kernel(x_ref, g_ref, w_ref, o_ref, h_ref, *, act):
    @pl.when(pl.program_id(1) == 0)
    def _():
        h_ref[...] = _rms(x_ref[...], g_ref[...]).astype(BF16)

    z = jnp.dot(h_ref[...], w_ref[...], preferred_element_type=F32)
    o_ref[...] = _ACTS[act](z).astype(o_ref.dtype)


def norm_matmul(x, g, w, act, out_dtype, tn):
    t, d = x.shape
    n = w.shape[1]
    tm = min(t, 512)
    return pl.pallas_call(
        functools.partial(_norm_matmul_kernel, act=act),
        grid=(t // tm, n // tn),
        in_specs=[
            pl.BlockSpec((tm, d), lambda i, j: (i, 0)),
            pl.BlockSpec((1, d), lambda i, j: (0, 0)),
            pl.BlockSpec((d, tn), lambda i, j: (0, j)),
        ],
        out_specs=pl.BlockSpec((tm, tn), lambda i, j: (i, j)),
        out_shape=jax.ShapeDtypeStruct((t, n), out_dtype),
        scratch_shapes=[pltpu.VMEM((tm, d), BF16)],
        compiler_params=_cparams("parallel", "arbitrary"),
        name="norm_matmul_" + act,
    )(x, g.reshape(1, d), w)


def _gmlp_kernel(uv_ref, lng_ref, lnb_ref, ws_ref, bst_ref, y_ref, vn_ref, *, lc):
    u = uv_ref[:, :BW]
    v = uv_ref[:, BW:]
    mu = jnp.mean(v, axis=-1, keepdims=True)
    vc = v - mu
    var = jnp.mean(vc * vc, axis=-1, keepdims=True)
    vn = vc * lax.rsqrt(var + EPS) * lng_ref[...] + lnb_ref[...]
    vn_ref[...] = vn
    row = lax.broadcasted_iota(jnp.int32, (lc, lc), 0) // CHUNK
    col = lax.broadcasted_iota(jnp.int32, (lc, lc), 1) // CHUNK
    causal = col <= row
    vnb = vn.astype(BF16)
    for g in range(GROUPS):
        w = jnp.where(causal, ws_ref[g], 0.0).astype(BF16)
        s = jnp.dot(w, vnb[:, g * GDIM:(g + 1) * GDIM], preferred_element_type=F32)
        s = s + bst_ref[:, g:g + 1]
        y_ref[:, g * GDIM:(g + 1) * GDIM] = (u[:, g * GDIM:(g + 1) * GDIM] * s).astype(y_ref.dtype)


def gmlp_mix(uv, ln_g, ln_b, ws, bs, lc):
    t = uv.shape[0]
    return pl.pallas_call(
        functools.partial(_gmlp_kernel, lc=lc),
        grid=(t // lc,),
        in_specs=[
            pl.BlockSpec((lc, 2 * BW), lambda i: (i, 0)),
            pl.BlockSpec((1, BW), lambda i: (0, 0)),
            pl.BlockSpec((1, BW), lambda i: (0, 0)),
            pl.BlockSpec((GROUPS, lc, lc), lambda i: (0, 0, 0)),
            pl.BlockSpec((lc, GROUPS), lambda i: (0, 0)),
        ],
        out_specs=[
            pl.BlockSpec((lc, BW), lambda i: (i, 0)),
            pl.BlockSpec((lc, BW), lambda i: (i, 0)),
        ],
        out_shape=[
            jax.ShapeDtypeStruct((t, BW), BF16),
            jax.ShapeDtypeStruct((t, BW), F32),
        ],
        compiler_params=_cparams("parallel"),
        name="gmlp_mix",
    )(uv, ln_g.reshape(1, BW), ln_b.reshape(1, BW), ws[:, :lc, :lc], bs[:, :lc].T)


def _pool_kernel(u_ref, pre_ref, w_ref, sc_ref, y_ref, st_ref, pad_ref, *, seq, start_pos):
    pad_ref[0:POOL_PAD, :] = pre_ref[...]
    pad_ref[POOL_PAD:, :] = u_ref[...]
    pos = lax.broadcasted_iota(jnp.int32, (seq, 1), 0) + start_pos
    for g, win in enumerate(POOL_WINDOWS):
        cols = slice(g * GDIM, (g + 1) * GDIM)
        tok = pad_ref[POOL_PAD:, cols]
        acc = tok
        for k in range(1, win):
            acc = acc + pad_ref[POOL_PAD - k:POOL_PAD - k + seq, cols]
        cnt = jnp.minimum(pos + 1, win).astype(F32)
        d = acc / cnt - tok
        y = jnp.dot(d.astype(BF16), w_ref[g], preferred_element_type=F32)
        y_ref[:, cols] = (y * sc_ref[:, cols]).astype(y_ref.dtype)
    st_ref[...] = pad_ref[seq + 1:seq + POOL_PAD, :]


def pool_mix(mid, prefix, start_pos, pool_w, pool_scale, batch, seq):
    mid3 = mid.reshape(batch, seq, 4 * BW)
    pre = jnp.concatenate([jnp.zeros((batch, 1, BW), F32), prefix], axis=1)
    y, st = pl.pallas_call(
        functools.partial(_pool_kernel, seq=seq, start_pos=start_pos),
        grid=(batch,),
        in_specs=[
            pl.BlockSpec((None, seq, BW), lambda b: (b, 0, 0)),
            pl.BlockSpec((None, POOL_PAD, BW), lambda b: (b, 0, 0)),
            pl.BlockSpec((GROUPS, GDIM, GDIM), lambda b: (0, 0, 0)),
            pl.BlockSpec((1, BW), lambda b: (0, 0)),
        ],
        out_specs=[
            pl.BlockSpec((None, seq, BW), lambda b: (b, 0, 0)),
            pl.BlockSpec((None, POOL_STATE, BW), lambda b: (b, 0, 0)),
        ],
        out_shape=[
            jax.ShapeDtypeStruct((batch, seq, BW), BF16),
            jax.ShapeDtypeStruct((batch, POOL_STATE, BW), F32),
        ],
        scratch_shapes=[pltpu.VMEM((seq + POOL_PAD, BW), F32)],
        compiler_params=_cparams("parallel"),
        name="pool_mix",
    )(mid3, pre, pool_w.astype(BF16), pool_scale.reshape(1, BW))
    return y.reshape(batch * seq, BW), st


def _attn_chunks(q_ref, kcat_ref, vcat_ref, bias_ref, o_ref, *, n_chunks, cq, band, first_block):
    scale = HEAD_DIM ** -0.5
    for h in range(HEADS):
        cols = slice(h * HEAD_DIM, (h + 1) * HEAD_DIM)
        bias = bias_ref[h]
        for i in range(n_chunks):
            q = (q_ref[i * cq:(i + 1) * cq, cols] * scale).astype(BF16)
            k = kcat_ref[i * cq:i * cq + band, cols]
            v = vcat_ref[i * cq:i * cq + band, cols]
            s = lax.dot_general(q, k, (((1,), (1,)), ((), ())), preferred_element_type=F32) + bias
            if first_block is not None:
                key = lax.broadcasted_iota(jnp.int32, (1, band), 1)
                s = jnp.where(key >= first_block * (BAND_PAST - i * cq), s, NEG)
            m = jnp.max(s, axis=-1, keepdims=True)
            p = jnp.exp(s - m)
            l = jnp.sum(p, axis=-1, keepdims=True)
            o = jnp.dot(p.astype(BF16), v, preferred_element_type=F32) / l
            o_ref[i * cq:(i + 1) * cq, cols] = o.astype(o_ref.dtype)


def _attn_prompt_kernel(q_ref, kp_ref, ko_ref, vp_ref, vo_ref, bias_ref, o_ref, kcat_ref, vcat_ref):
    kcat_ref[0:BAND_PAST, :] = kp_ref[...].astype(BF16)
    kcat_ref[BAND_PAST:, :] = ko_ref[...].astype(BF16)
    vcat_ref[0:BAND_PAST, :] = vp_ref[...].astype(BF16)
    vcat_ref[BAND_PAST:, :] = vo_ref[...].astype(BF16)
    _attn_chunks(q_ref, kcat_ref, vcat_ref, bias_ref, o_ref, n_chunks=BAND_CHUNKS, cq=CHUNK,
                 band=BAND_PAST + CHUNK, first_block=(pl.program_id(1) == 0).astype(jnp.int32))


def _rel_bias_tile(rel_bias, qpos, kpos):
    rel = np.clip(qpos[:, None] - kpos[None, :], -REL_CLIP, REL_CLIP) + REL_CLIP
    return rel_bias[:, rel]


def attn_prompt(mid, rel_bias, batch, seq):
    mid3 = mid.reshape(batch, seq, 4 * BW)
    blk = BAND_PAST
    bias = _rel_bias_tile(rel_bias, np.arange(CHUNK), np.arange(BAND_PAST + CHUNK) - BAND_PAST)
    prev = lambda b, j: jnp.maximum(j - 1, 0)
    y = pl.pallas_call(
        _attn_prompt_kernel,
        grid=(batch, seq // blk),
        in_specs=[
            pl.BlockSpec((None, blk, BW), lambda b, j: (b, j, 1)),
            pl.BlockSpec((None, blk, BW), lambda b, j: (b, prev(b, j), 2)),
            pl.BlockSpec((None, blk, BW), lambda b, j: (b, j, 2)),
            pl.BlockSpec((None, blk, BW), lambda b, j: (b, prev(b, j), 3)),
            pl.BlockSpec((None, blk, BW), lambda b, j: (b, j, 3)),
            pl.BlockSpec((HEADS, CHUNK, BAND_PAST + CHUNK), lambda b, j: (0, 0, 0)),
        ],
        out_specs=pl.BlockSpec((None, blk, BW), lambda b, j: (b, j, 0)),
        out_shape=jax.ShapeDtypeStruct((batch, seq, BW), BF16),
        scratch_shapes=[pltpu.VMEM((2 * blk, BW), BF16), pltpu.VMEM((2 * blk, BW), BF16)],
        compiler_params=_cparams("parallel", "parallel"),
        name="attn_prompt",
    )(mid3, mid3, mid3, mid3, mid3, bias)
    return y.reshape(batch * seq, BW)


def _attn_sample_kernel(q_ref, kc_ref, kn_ref, vc_ref, vn_ref, bias_ref, o_ref, kcat_ref, vcat_ref, *, n_cache):
    kcat_ref[0:n_cache, :] = kc_ref[...].astype(BF16)
    kcat_ref[n_cache:, :] = kn_ref[...].astype(BF16)
    vcat_ref[0:n_cache, :] = vc_ref[...].astype(BF16)
    vcat_ref[n_cache:, :] = vn_ref[...].astype(BF16)
    seq = q_ref.shape[0]
    _attn_chunks(q_ref, kcat_ref, vcat_ref, bias_ref, o_ref, n_chunks=1, cq=seq, band=n_cache + seq,
                 first_block=None)


def attn_sample(mid, cache_k, cache_v, rel_bias, batch, seq):
    n_cache = cache_k.shape[1]
    assert PAST_LEN >= n_cache
    mid3 = mid.reshape(batch, seq, 4 * BW)
    ck = cache_k.reshape(batch, n_cache, BW)
    cv = cache_v.reshape(batch, n_cache, BW)
    bias = _rel_bias_tile(rel_bias, PAST_LEN + np.arange(seq), PAST_LEN - n_cache + np.arange(n_cache + seq))
    y = pl.pallas_call(
        functools.partial(_attn_sample_kernel, n_cache=n_cache),
        grid=(batch,),
        in_specs=[
            pl.BlockSpec((None, seq, BW), lambda b: (b, 0, 1)),
            pl.BlockSpec((None, n_cache, BW), lambda b: (b, 0, 0)),
            pl.BlockSpec((None, seq, BW), lambda b: (b, 0, 2)),
            pl.BlockSpec((None, n_cache, BW), lambda b: (b, 0, 0)),
            pl.BlockSpec((None, seq, BW), lambda b: (b, 0, 3)),
            pl.BlockSpec((HEADS, seq, n_cache + seq), lambda b: (0, 0, 0)),
        ],
        out_specs=pl.BlockSpec((None, seq, BW), lambda b: (b, 0, 0)),
        out_shape=jax.ShapeDtypeStruct((batch, seq, BW), BF16),
        scratch_shapes=[pltpu.VMEM((n_cache + seq, BW), BF16), pltpu.VMEM((n_cache + seq, BW), BF16)],
        compiler_params=_cparams("parallel"),
        name="attn_sample",
    )(mid3, ck, mid3, cv, mid3, bias)
    return y.reshape(batch * seq, BW)


def _mixer_out_kernel(x_ref, ya_ref, yb_ref, yc_ref, gate_ref, wb_ref, wo_ref, o_ref):
    acc = None
    for n, y_ref in enumerate((ya_ref, yb_ref, yc_ref)):
        proj = jnp.dot(y_ref[...], wb_ref[n], preferred_element_type=F32)
        term = gate_ref[:, n * D_MODEL:(n + 1) * D_MODEL].astype(F32) * proj
        acc = term if acc is None else acc + term
    o_ref[...] = x_ref[...] + jnp.dot(acc.astype(BF16), wo_ref[...], preferred_element_type=F32)


def mixer_out(x, ya, yb, yc, gates, w_branch, w_out):
    t = x.shape[0]
    tm = min(t, 512)
    row = lambda i: (i, 0)
    return pl.pallas_call(
        _mixer_out_kernel,
        grid=(t // tm,),
        in_specs=[
            pl.BlockSpec((tm, D_MODEL), row),
            pl.BlockSpec((tm, BW), row),
            pl.BlockSpec((tm, BW), row),
            pl.BlockSpec((tm, BW), row),
            pl.BlockSpec((tm, 3 * D_MODEL), row),
            pl.BlockSpec((3, BW, D_MODEL), lambda i: (0, 0, 0)),
            pl.BlockSpec((D_MODEL, D_MODEL), lambda i: (0, 0)),
        ],
        out_specs=pl.BlockSpec((tm, D_MODEL), row),
        out_shape=jax.ShapeDtypeStruct((t, D_MODEL), F32),
        compiler_params=_cparams("parallel"),
        name="mixer_out",
    )(x, ya, yb, yc, gates, w_branch, w_out)


def _extract_top(s, payload, k):
    r = float(s.shape[0])
    rows = lax.broadcasted_iota(jnp.int32, s.shape, 0).astype(F32)
    vals, pays = [], []
    for _ in range(k):
        m = jnp.max(s, axis=0, keepdims=True)
        idx = jnp.min(jnp.where(s == m, rows, r), axis=0, keepdims=True)
        sel = rows == idx
        vals.append(m)
        pays.append(idx if payload is None else jnp.max(jnp.where(sel, payload, -1.0), axis=0, keepdims=True))
        s = jnp.where(sel, -jnp.inf, s)
    return jnp.concatenate(vals, axis=0), jnp.concatenate(pays, axis=0)


def _peer_query_kernel(x_ref, g_ref, wq_ref, sk_ref, hp_ref, idx_ref, gate_ref, q_ref, *, tm):
    hb = _rms(x_ref[...], g_ref[...]).astype(BF16)
    q_ref[...] = jnp.dot(hb, wq_ref[...], preferred_element_type=F32).astype(BF16)
    bits = lax.bitcast_convert_type(hb.astype(F32), jnp.int32)
    half = D_MODEL // 2
    hp_ref[...] = bits[:, half:] | lax.shift_right_logical(bits[:, :half], 16)

    def sub_block(sb, carry):
        tok = pl.ds(pl.multiple_of(sb * LANES, LANES), LANES)
        for hd in range(PEER_HEADS):
            sv, si = [], []
            for p in range(2):
                hp = hd * 2 + p
                q = q_ref[tok, hp * PEER_HALF:(hp + 1) * PEER_HALF]
                s = lax.dot_general(sk_ref[hp], q, (((1,), (1,)), ((), ())), preferred_element_type=F32)
                v, i = _extract_top(s, None, PEER_TOPK)
                sv.append(v)
                si.append(i)
            cand = jnp.concatenate([sv[0][a:a + 1] + sv[1] for a in range(PEER_TOPK)], axis=0)
            eid = jnp.concatenate([si[0][a:a + 1] * PEER_NKEYS + si[1] for a in range(PEER_TOPK)], axis=0)
            tv, te = _extract_top(cand, eid, PEER_TOPK)
            e = jnp.exp(tv - tv[0:1])
            gate = e / jnp.sum(e, axis=0, keepdims=True)
            idx_ref[sb, hd * PEER_TOPK:(hd + 1) * PEER_TOPK, :] = te.astype(jnp.int32)
            gate_ref[sb, hd * PEER_TOPK:(hd + 1) * PEER_TOPK, :] = gate
        return carry

    lax.fori_loop(0, tm // LANES, sub_block, 0)


def peer_query(x, norm_g, wq, subkeys):
    t = x.shape[0]
    tm = min(t, 512)
    nq = wq.shape[1]
    nsb = tm // LANES
    return pl.pallas_call(
        functools.partial(_peer_query_kernel, tm=tm),
        grid=(t // tm,),
        in_specs=[
            pl.BlockSpec((tm, D_MODEL), lambda i: (i, 0)),
            pl.BlockSpec((1, D_MODEL), lambda i: (0, 0)),
            pl.BlockSpec((D_MODEL, nq), lambda i: (0, 0)),
            pl.BlockSpec((2 * PEER_HEADS, PEER_NKEYS, PEER_HALF), lambda i: (0, 0, 0)),
        ],
        out_specs=[
            pl.BlockSpec((tm, D_MODEL // 2), lambda i: (i, 0)),
            pl.BlockSpec((nsb, PEER_SLOTS, LANES), lambda i: (i, 0, 0)),
            pl.BlockSpec((nsb, PEER_SLOTS, LANES), lambda i: (i, 0, 0)),
        ],
        out_shape=[
            jax.ShapeDtypeStruct((t, D_MODEL // 2), jnp.int32),
            jax.ShapeDtypeStruct((t // LANES, PEER_SLOTS, LANES), jnp.int32),
            jax.ShapeDtypeStruct((t // LANES, PEER_SLOTS, LANES), F32),
        ],
        scratch_shapes=[pltpu.VMEM((tm, nq), BF16)],
        compiler_params=_cparams("parallel"),
        name="peer_query",
    )(x, norm_g.reshape(1, D_MODEL), wq, subkeys)


HALF_D = D_MODEL // 2
SC_LANES = 16
SC_UNIT_ROWS = 32
SC_UNITS_PER_TOKEN = PEER_SLOTS // SC_UNIT_ROWS
SC_MAX_TOKEN_BLOCK = 32
SC_ROW_GROUP = 8
SC_HALF_VECS = HALF_D // SC_LANES
SC_OUT_VECS = 8
SC_BF16_TERMS = 4
SC_ROW_TILES = D_MODEL // LANES


def pack_expert_tables(peer_u, peer_v):
    def pack(x):
        b = lax.bitcast_convert_type(x.astype(BF16), jnp.uint16).astype(jnp.uint32)
        return (b[:, HALF_D:] << 16) | b[:, :HALF_D]

    rows = lax.bitcast_convert_type(jnp.concatenate([pack(peer_u), pack(peer_v)], axis=1), jnp.int32)
    return rows.reshape(rows.shape[0], SC_ROW_TILES, LANES)


def sc_peer_experts(table, idx, gates, hp):
    t = hp.shape[0]
    info = plsc.get_sparse_core_info()
    n_workers = info.num_cores * info.num_subcores
    tpw = t // n_workers
    token_block = min(tpw, SC_MAX_TOKEN_BLOCK)
    assert t % (n_workers * token_block) == 0
    units = token_block * SC_UNITS_PER_TOKEN
    mesh = plsc.VectorSubcoreMesh(core_axis_name="core", subcore_axis_name="subcore")
    hi_mask = jnp.int32(-65536)
    gelu_c = math.sqrt(2.0 / math.pi)

    @functools.partial(
        pl.kernel,
        out_type=jax.ShapeDtypeStruct((t, D_MODEL), F32),
        mesh=mesh,
        scratch_types=[
            pltpu.VMEM((token_block * PEER_SLOTS,), jnp.int32),
            pltpu.VMEM((token_block * PEER_SLOTS,), F32),
            pltpu.VMEM((token_block, HALF_D), jnp.int32),
            pltpu.VMEM((token_block, D_MODEL), F32),
            pltpu.VMEM((2, SC_UNIT_ROWS, SC_ROW_TILES, LANES), jnp.int32),
            pltpu.VMEM((SC_UNIT_ROWS, SC_LANES), F32),
            pltpu.VMEM((SC_UNIT_ROWS,), jnp.int32),
            pltpu.SemaphoreType.DMA((2,)),
        ],
        compiler_params=pltpu.CompilerParams(needs_layout_passes=False),
        name="peer_sc_experts",
    )
    def kern(table_hbm, idx_hbm, gate_hbm, h_hbm, out_hbm, idx_v, gate_v, h_v, out_v, rows_v, part_v, coef_v, sem):
        wid = lax.axis_index("subcore") * info.num_cores + lax.axis_index("core")
        lane = lax.iota(jnp.int32, SC_LANES)
        zero = jnp.zeros((SC_LANES,), F32)

        def gather(unit, b):
            rows = idx_v.at[pl.ds(unit * SC_UNIT_ROWS, SC_UNIT_ROWS)]
            return pltpu.make_async_copy(table_hbm.at[rows], rows_v.at[b], sem.at[b])

        def row_vec(b, r, vec):
            per_tile_row = LANES // SC_LANES
            lane0 = pl.multiple_of((vec % per_tile_row) * SC_LANES, SC_LANES)
            return rows_v[b, r, vec // per_tile_row, pl.ds(lane0, SC_LANES)]

        def unpack(w):
            return lax.bitcast_convert_type(w << 16, F32), lax.bitcast_convert_type(w & hi_mask, F32)

        def as_pairs(w):
            return plsc.bitcast(w, BF16)

        def compute(unit, b):
            tl = unit // SC_UNITS_PER_TOKEN
            q = unit % SC_UNITS_PER_TOKEN

            def row_group(rg, carry):
                def kstep(k, accs):
                    hs = []
                    for j in range(SC_BF16_TERMS):
                        off = pl.multiple_of((k * SC_BF16_TERMS + j) * SC_LANES, SC_LANES)
                        hs.append(as_pairs(h_v[tl, pl.ds(off, SC_LANES)]))
                    new = []
                    for r in range(SC_ROW_GROUP):
                        p = None
                        for j in range(SC_BF16_TERMS):
                            term = as_pairs(row_vec(b, rg * SC_ROW_GROUP + r, k * SC_BF16_TERMS + j)) * hs[j]
                            p = term if p is None else p + term
                        lo, hi = unpack(plsc.bitcast(p, jnp.int32))
                        new.append(accs[r] + lo + hi)
                    return tuple(new)

                accs = lax.fori_loop(0, SC_HALF_VECS // SC_BF16_TERMS, kstep, (zero,) * SC_ROW_GROUP)
                for r in range(SC_ROW_GROUP):
                    part_v[rg * SC_ROW_GROUP + r, :] = accs[r]
                return carry

            lax.fori_loop(0, SC_UNIT_ROWS // SC_ROW_GROUP, row_group, 0)

            for i in range(SC_UNIT_ROWS // SC_LANES):
                rows = lane + i * SC_LANES
                a = zero
                for l in range(SC_LANES):
                    a = a + plsc.load_gather(part_v, [rows, jnp.full((SC_LANES,), l, jnp.int32)])
                z = gelu_c * (a + 0.044715 * (a * a * a))
                act = a / (1.0 + jnp.exp(-2.0 * z))
                slot = pl.multiple_of(tl * PEER_SLOTS + q * SC_UNIT_ROWS + i * SC_LANES, SC_LANES)
                bits = lax.bitcast_convert_type(gate_v[pl.ds(slot, SC_LANES)] * act, jnp.int32)
                top = (bits + 0x7FFF + ((bits >> 16) & 1)) & hi_mask
                coef_v[pl.ds(i * SC_LANES, SC_LANES)] = top | lax.shift_right_logical(top, 16)

            def out_pass(dq, carry):
                def row_quad(rq, accs):
                    cs = []
                    for j in range(SC_BF16_TERMS):
                        row = jnp.full((SC_LANES,), rq * SC_BF16_TERMS + j, jnp.int32)
                        cs.append(as_pairs(plsc.load_gather(coef_v, [row])))
                    new = []
                    for k in range(SC_OUT_VECS):
                        p = None
                        for j in range(SC_BF16_TERMS):
                            w = row_vec(b, rq * SC_BF16_TERMS + j, SC_HALF_VECS + dq * SC_OUT_VECS + k)
                            term = as_pairs(w) * cs[j]
                            p = term if p is None else p + term
                        lo, hi = unpack(plsc.bitcast(p, jnp.int32))
                        new.append(accs[2 * k] + lo)
                        new.append(accs[2 * k + 1] + hi)
                    return tuple(new)

                accs = lax.fori_loop(0, SC_UNIT_ROWS // SC_BF16_TERMS, row_quad, (zero,) * (2 * SC_OUT_VECS))
                for k in range(SC_OUT_VECS):
                    off = pl.multiple_of((dq * SC_OUT_VECS + k) * SC_LANES, SC_LANES)
                    plsc.addupdate(out_v.at[tl, pl.ds(off, SC_LANES)], accs[2 * k])
                    plsc.addupdate(out_v.at[tl, pl.ds(HALF_D + off, SC_LANES)], accs[2 * k + 1])
                return carry

            lax.fori_loop(0, SC_HALF_VECS // SC_OUT_VECS, out_pass, 0)

        @pl.loop(0, tpw // token_block)
        def _(blk):
            tok0 = wid * tpw + blk * token_block
            slots = pl.ds(tok0 * PEER_SLOTS, token_block * PEER_SLOTS)
            pltpu.sync_copy(idx_hbm.at[slots], idx_v)
            pltpu.sync_copy(gate_hbm.at[slots], gate_v)
            pltpu.sync_copy(h_hbm.at[pl.ds(tok0, token_block)], h_v)

            @pl.loop(0, token_block)
            def _(tl):
                @pl.loop(0, D_MODEL // SC_LANES)
                def _(k):
                    out_v[tl, pl.ds(pl.multiple_of(k * SC_LANES, SC_LANES), SC_LANES)] = zero

            gather(0, 0).start()

            @pl.loop(0, units, step=2)
            def _(u0):
                for b in range(2):
                    unit = u0 + b

                    @pl.when(unit + 1 < units)
                    def _():
                        gather(unit + 1, 1 - b).start()

                    gather(unit, b).wait()
                    compute(unit, b)

            pltpu.sync_copy(out_v, out_hbm.at[pl.ds(tok0, token_block)])

    return kern(table, idx, gates, hp)


def peer_block(x, norm_g, wq, subkeys, table):
    t = x.shape[0]
    hp, idx3, gate3 = peer_query(x, norm_g, wq, subkeys)
    idx = jnp.transpose(idx3, (0, 2, 1)).reshape(t * PEER_SLOTS)
    gates = jnp.transpose(gate3, (0, 2, 1)).reshape(t * PEER_SLOTS)
    return sc_peer_experts(table, idx, gates, hp)


def _ple_kernel(x_ref, ffn_ref, p_ref, g_ref, wg_ref, wp_ref, gf_ref, o_ref, *, final):
    x = x_ref[...] + ffn_ref[...]
    gate = _sigmoid(jnp.dot(_rms(x, g_ref[...]).astype(BF16), wg_ref[...], preferred_element_type=F32))
    emb = jnp.dot(p_ref[...].astype(BF16), wp_ref[...], preferred_element_type=F32)
    y = x + gate * emb
    o_ref[...] = _rms(y, gf_ref[...]) if final else y


def ple_block(x, ffn, p, norm_g, w_gate, w_proj, norm_final, final):
    t = x.shape[0]
    tm = min(t, 512)
    row = lambda i: (i, 0)
    full = lambda i: (0, 0)
    return pl.pallas_call(
        functools.partial(_ple_kernel, final=final),
        grid=(t // tm,),
        in_specs=[
            pl.BlockSpec((tm, D_MODEL), row),
            pl.BlockSpec((tm, D_MODEL), row),
            pl.BlockSpec((tm, PLE_DIM), row),
            pl.BlockSpec((1, D_MODEL), full),
            pl.BlockSpec((D_MODEL, D_MODEL), full),
            pl.BlockSpec((PLE_DIM, D_MODEL), full),
            pl.BlockSpec((1, D_MODEL), full),
        ],
        out_specs=pl.BlockSpec((tm, D_MODEL), row),
        out_shape=jax.ShapeDtypeStruct((t, D_MODEL), F32),
        compiler_params=_cparams("parallel"),
        name="ple_block",
    )(x, ffn, p, norm_g.reshape(1, D_MODEL), w_gate, w_proj, norm_final.reshape(1, D_MODEL))


def _prompt_groups(batch):
    sizes = []
    while sum(sizes) < batch:
        nxt = 1 if len(sizes) < 2 else -(-sizes[-1] * 7 // 5)
        sizes.append(min(nxt, batch - sum(sizes)))
    return sizes
def _trunk_layer(x, ple, lw, batch, seq, pool_prefix, pool_start, cache, final):
    uv = norm_matmul(x, lw["norm_mix"], lw["w_uv"], "gelu", F32, 2 * BW)
    mid = norm_matmul(x, lw["norm_mix"], lw["w_mid"], "none", F32, 4 * BW)
    gates = norm_matmul(x, lw["norm_mix"], lw["w_gates"], "sigmoid", BF16, D_MODEL)
    ya, vn = gmlp_mix(uv, lw["gmlp_ln_g"], lw["gmlp_ln_b"], lw["gmlp_ws"], lw["gmlp_bs"], min(seq, GMLP_CHUNK))
    yb, pool_state = pool_mix(mid, pool_prefix, pool_start, lw["pool_w"], lw["pool_scale"], batch, seq)
    if cache is None:
        yc = attn_prompt(mid, lw["rel_bias"], batch, seq)
    else:
        yc = attn_sample(mid, cache[0], cache[1], lw["rel_bias"], batch, seq)
    x = mixer_out(x, ya, yb, yc, gates, lw["w_branch"], lw["w_out"])
    ffn = peer_block(x, lw["norm_ffn"], lw["peer_wq"], lw["peer_subkeys"], lw["peer_table"])
    x = ple_block(x, ffn, ple, lw["norm_ple"], lw["ple_gate"], lw["ple_proj"], lw["norm_final"], final)
    return x, mid, pool_state, vn


def kernel(x_prompt, x_sample, cache_attn_k, cache_attn_v, state_pool, p_prompt, p_sample, norm_mix, w_in, gmlp_ln_g, gmlp_ln_b, gmlp_ws, gmlp_bs, pool_w, pool_scale, attn_rel_bias, w_branch, w_out, norm_ffn, peer_wq, peer_subkeys, peer_u, peer_v, norm_ple, ple_gate, ple_proj, norm_final):
    bp, lp, _ = x_prompt.shape
    bs, ls, _ = x_sample.shape
    assert lp % BAND_PAST == 0 and lp % GMLP_CHUNK == 0 and ls <= CHUNK
    n_keep = min(BAND_PAST, lp)
    sizes = _prompt_groups(bp)
    starts = [sum(sizes[:g]) for g in range(len(sizes))]
    xg = [x_prompt[a:a + n].reshape(n * lp, D_MODEL) for a, n in zip(starts, sizes)]
    xs = x_sample.reshape(bs * ls, D_MODEL)
    outs = {k: [] for k in ("pk", "pv", "pps", "sk", "sv", "sps", "sgv")}
    for i in range(DEPTH):
        w_in_b = w_in[i].astype(BF16)
        lw = dict(
            norm_mix=norm_mix[i],
            w_uv=w_in_b[:, :2 * BW],
            w_mid=w_in_b[:, 2 * BW:6 * BW],
            w_gates=w_in_b[:, 6 * BW:],
            gmlp_ln_g=gmlp_ln_g[i], gmlp_ln_b=gmlp_ln_b[i], gmlp_ws=gmlp_ws[i], gmlp_bs=gmlp_bs[i],
            pool_w=pool_w[i], pool_scale=pool_scale[i], rel_bias=attn_rel_bias[i],
            w_branch=w_branch[i].astype(BF16), w_out=w_out[i].astype(BF16),
            norm_ffn=norm_ffn[i], peer_wq=peer_wq[i].astype(BF16),
            peer_subkeys=peer_subkeys[i].reshape(2 * PEER_HEADS, PEER_NKEYS, PEER_HALF).astype(BF16),
            peer_table=pack_expert_tables(peer_u[i], peer_v[i]),
            norm_ple=norm_ple[i], ple_gate=ple_gate[i].astype(BF16), ple_proj=ple_proj[i].astype(BF16),
            norm_final=norm_final,
        )
        final = i == DEPTH - 1
        pk, pv, pps = [], [], []
        for g, (a, bg) in enumerate(zip(starts, sizes)):
            ple_g = p_prompt[i, a:a + bg].reshape(bg * lp, PLE_DIM)
            zero_prefix = jnp.zeros((bg, POOL_STATE, BW), F32)
            xg[g], mid_p, ps_p, _ = _trunk_layer(xg[g], ple_g, lw, bg, lp, zero_prefix, 0, None, final)
            mid_p = mid_p.reshape(bg, lp, 4 * BW)
            pk.append(mid_p[:, lp - n_keep:, 2 * BW:3 * BW].reshape(bg, n_keep, HEADS, HEAD_DIM))
            pv.append(mid_p[:, lp - n_keep:, 3 * BW:].reshape(bg, n_keep, HEADS, HEAD_DIM))
            pps.append(ps_p)
        outs["pk"].append(jnp.concatenate(pk, axis=0))
        outs["pv"].append(jnp.concatenate(pv, axis=0))
        outs["pps"].append(jnp.concatenate(pps, axis=0))
        xs, mid_s, ps_s, vn_s = _trunk_layer(xs, p_sample[i].reshape(bs * ls, PLE_DIM), lw, bs, ls, state_pool[i],
                                             PAST_LEN, (cache_attn_k[i], cache_attn_v[i]), final)
        mid_s = mid_s.reshape(bs, ls, 4 * BW)
        outs["sk"].append(mid_s[:, :, 2 * BW:3 * BW].reshape(bs, ls, HEADS, HEAD_DIM))
        outs["sv"].append(mid_s[:, :, 3 * BW:].reshape(bs, ls, HEADS, HEAD_DIM))
        outs["sps"].append(ps_s)
        outs["sgv"].append(vn_s.reshape(bs, ls, BW))
    st = lambda k: jnp.stack(outs[k])
    y_prompt = jnp.concatenate(xg, axis=0).reshape(bp, lp, D_MODEL)
    return (y_prompt, xs.reshape(bs, ls, D_MODEL), st("pk"), st("pv"), st("pps"),
            st("sk"), st("sv"), st("sps"), st("sgv"))
```

```python
import functools
import math

import jax
import jax.numpy as jnp
import numpy as np
from jax import lax
from jax.experimental import pallas as pl
from jax.experimental.pallas import tpu as pltpu
from jax.experimental.pallas import tpu_sc as plsc

F32 = jnp.float32
BF16 = jnp.bfloat16

D_MODEL = 1024
DEPTH = 2
CHUNK = 64
EPS = 1e-6
BW = D_MODEL // 2
GMLP_CHUNK = 128
GROUPS = 4
GDIM = BW // GROUPS
POOL_WINDOWS = (2, 4, 8, 16)
POOL_STATE = 15
POOL_PAD = 16
HEADS = 8
HEAD_DIM = BW // HEADS
BAND_CHUNKS = 8
BAND_PAST = BAND_CHUNKS * CHUNK
REL_CLIP = 128
PAST_LEN = 4096
PEER_HEADS = 8
PEER_NKEYS = 128
PEER_HALF = 128
PEER_TOPK = 16
PEER_SLOTS = PEER_HEADS * PEER_TOPK
PLE_DIM = 256

LANES = 128
VMEM_LIMIT = 56 * 1024 * 1024
NEG = -1e30


def _cparams(*sem):
    return pltpu.CompilerParams(dimension_semantics=sem, vmem_limit_bytes=VMEM_LIMIT)


def _rms(x, g):
    ms = jnp.mean(x * x, axis=-1, keepdims=True)
    return x * lax.rsqrt(ms + EPS) * g


def _gelu(x):
    c = math.sqrt(2.0 / math.pi)
    return 0.5 * x * (1.0 + jnp.tanh(c * (x + 0.044715 * (x * x * x))))


def _sigmoid(x):
    return 1.0 / (1.0 + jnp.exp(-x))


_ACTS = {"gelu": _gelu, "sigmoid": _sigmoid, "none": lambda z: z}


def _norm_matmul_kernel(x_ref, g_ref, w_ref, o_ref, h_ref, *, act):
    @pl.when(pl.program_id(1) == 0)
    def _():
        h_ref[...] = _rms(x_ref[...], g_ref[...]).astype(BF16)

    z = jnp.dot(h_ref[...], w_ref[...], preferred_element_type=F32)
    o_ref[...] = _ACTS[act](z).astype(o_ref.dtype)


def norm_matmul(x, g, w, act, out_dtype, tn):
    t, d = x.shape
    n = w.shape[1]
    tm = min(t, 512)
    return pl.pallas_call(
        functools.partial(_norm_matmul_kernel, act=act),
        grid=(t // tm, n // tn),
        in_specs=[
            pl.BlockSpec((tm, d), lambda i, j: (i, 0)),
            pl.BlockSpec((1, d), lambda i, j: (0, 0)),
            pl.BlockSpec((d, tn), lambda i, j: (0, j)),
        ],
        out_specs=pl.BlockSpec((tm, tn), lambda i, j: (i, j)),
        out_shape=jax.ShapeDtypeStruct((t, n), out_dtype),
        scratch_shapes=[pltpu.VMEM((tm, d), BF16)],
        compiler_params=_cparams("parallel", "arbitrary"),
        name="norm_matmul_" + act,
    )(x, g.reshape(1, d), w)


def _gmlp_kernel(uv_ref, lng_ref, lnb_ref, ws_ref, bst_ref, y_ref, vn_ref, *, lc):
    u = uv_ref[:, :BW]
    v = uv_ref[:, BW:]
    mu = jnp.mean(v, axis=-1, keepdims=True)
    vc = v - mu
    var = jnp.mean(vc * vc, axis=-1, keepdims=True)
    vn = vc * lax.rsqrt(var + EPS) * lng_ref[...] + lnb_ref[...]
    vn_ref[...] = vn
    row = lax.broadcasted_iota(jnp.int32, (lc, lc), 0) // CHUNK
    col = lax.broadcasted_iota(jnp.int32, (lc, lc), 1) // CHUNK
    causal = col <= row
    vnb = vn.astype(BF16)
    for g in range(GROUPS):
        w = jnp.where(causal, ws_ref[g], 0.0).astype(BF16)
        s = jnp.dot(w, vnb[:, g * GDIM:(g + 1) * GDIM], preferred_element_type=F32)
        s = s + bst_ref[:, g:g + 1]
        y_ref[:, g * GDIM:(g + 1) * GDIM] = (u[:, g * GDIM:(g + 1) * GDIM] * s).astype(y_ref.dtype)


def gmlp_mix(uv, ln_g, ln_b, ws, bs, lc):
    t = uv.shape[0]
    return pl.pallas_call(
        functools.partial(_gmlp_kernel, lc=lc),
        grid=(t // lc,),
        in_specs=[
            pl.BlockSpec((lc, 2 * BW), lambda i: (i, 0)),
            pl.BlockSpec((1, BW), lambda i: (0, 0)),
            pl.BlockSpec((1, BW), lambda i: (0, 0)),
            pl.BlockSpec((GROUPS, lc, lc), lambda i: (0, 0, 0)),
            pl.BlockSpec((lc, GROUPS), lambda i: (0, 0)),
        ],
        out_specs=[
            pl.BlockSpec((lc, BW), lambda i: (i, 0)),
            pl.BlockSpec((lc, BW), lambda i: (i, 0)),
        ],
        out_shape=[
            jax.ShapeDtypeStruct((t, BW), BF16),
            jax.ShapeDtypeStruct((t, BW), F32),
        ],
        compiler_params=_cparams("parallel"),
        name="gmlp_mix",
    )(uv, ln_g.reshape(1, BW), ln_b.reshape(1, BW), ws[:, :lc, :lc], bs[:, :lc].T)


def _pool_kernel(u_ref, pre_ref, w_ref, sc_ref, y_ref, st_ref, pad_ref, *, seq, start_pos):
    pad_ref[0:POOL_PAD, :] = pre_ref[...]
    pad_ref[POOL_PAD:, :] = u_ref[...]
    pos = lax.broadcasted_iota(jnp.int32, (seq, 1), 0) + start_pos
    for g, win in enumerate(POOL_WINDOWS):
        cols = slice(g * GDIM, (g + 1) * GDIM)
        tok = pad_ref[POOL_PAD:, cols]
        acc = tok
        for k in range(1, win):
            acc = acc + pad_ref[POOL_PAD - k:POOL_PAD - k + seq, cols]
        cnt = jnp.minimum(pos + 1, win).astype(F32)
        d = acc / cnt - tok
        y = jnp.dot(d.astype(BF16), w_ref[g], preferred_element_type=F32)
        y_ref[:, cols] = (y * sc_ref[:, cols]).astype(y_ref.dtype)
    st_ref[...] = pad_ref[seq + 1:seq + POOL_PAD, :]


def pool_mix(mid, prefix, start_pos, pool_w, pool_scale, batch, seq):
    mid3 = mid.reshape(batch, seq, 4 * BW)
    pre = jnp.concatenate([jnp.zeros((batch, 1, BW), F32), prefix], axis=1)
    y, st = pl.pallas_call(
        functools.partial(_pool_kernel, seq=seq, start_pos=start_pos),
        grid=(batch,),
        in_specs=[
            pl.BlockSpec((None, seq, BW), lambda b: (b, 0, 0)),
            pl.BlockSpec((None, POOL_PAD, BW), lambda b: (b, 0, 0)),
            pl.BlockSpec((GROUPS, GDIM, GDIM), lambda b: (0, 0, 0)),
            pl.BlockSpec((1, BW), lambda b: (0, 0)),
        ],
        out_specs=[
            pl.BlockSpec((None, seq, BW), lambda b: (b, 0, 0)),
            pl.BlockSpec((None, POOL_STATE, BW), lambda b: (b, 0, 0)),
        ],
        out_shape=[
            jax.ShapeDtypeStruct((batch, seq, BW), BF16),
            jax.ShapeDtypeStruct((batch, POOL_STATE, BW), F32),
        ],
        scratch_shapes=[pltpu.VMEM((seq + POOL_PAD, BW), F32)],
        compiler_params=_cparams("parallel"),
        name="pool_mix",
    )(mid3, pre, pool_w.astype(BF16), pool_scale.reshape(1, BW))
    return y.reshape(batch * seq, BW), st


def _attn_chunks(q_ref, kcat_ref, vcat_ref, bias_ref, o_ref, *, n_chunks, cq, band, first_block):
    scale = HEAD_DIM ** -0.5
    for h in range(HEADS):
        cols = slice(h * HEAD_DIM, (h + 1) * HEAD_DIM)
        bias = bias_ref[h]
        for i in range(n_chunks):
            q = (q_ref[i * cq:(i + 1) * cq, cols] * scale).astype(BF16)
            k = kcat_ref[i * cq:i * cq + band, cols]
            v = vcat_ref[i * cq:i * cq + band, cols]
            s = lax.dot_general(q, k, (((1,), (1,)), ((), ())), preferred_element_type=F32) + bias
            if first_block is not None:
                key = lax.broadcasted_iota(jnp.int32, (1, band), 1)
                s = jnp.where(key >= first_block * (BAND_PAST - i * cq), s, NEG)
            m = jnp.max(s, axis=-1, keepdims=True)
            p = jnp.exp(s - m)
            l = jnp.sum(p, axis=-1, keepdims=True)
            o = jnp.dot(p.astype(BF16), v, preferred_element_type=F32) / l
            o_ref[i * cq:(i + 1) * cq, cols] = o.astype(o_ref.dtype)


def _attn_prompt_kernel(q_ref, kp_ref, ko_ref, vp_ref, vo_ref, bias_ref, o_ref, kcat_ref, vcat_ref):
    kcat_ref[0:BAND_PAST, :] = kp_ref[...].astype(BF16)
    kcat_ref[BAND_PAST:, :] = ko_ref[...].astype(BF16)
    vcat_ref[0:BAND_PAST, :] = vp_ref[...].astype(BF16)
    vcat_ref[BAND_PAST:, :] = vo_ref[...].astype(BF16)
    _attn_chunks(q_ref, kcat_ref, vcat_ref, bias_ref, o_ref, n_chunks=BAND_CHUNKS, cq=CHUNK,
                 band=BAND_PAST + CHUNK, first_block=(pl.program_id(1) == 0).astype(jnp.int32))


def _rel_bias_tile(rel_bias, qpos, kpos):
    rel = np.clip(qpos[:, None] - kpos[None, :], -REL_CLIP, REL_CLIP) + REL_CLIP
    return rel_bias[:, rel]


def attn_prompt(mid, rel_bias, batch, seq):
    mid3 = mid.reshape(batch, seq, 4 * BW)
    blk = BAND_PAST
    bias = _rel_bias_tile(rel_bias, np.arange(CHUNK), np.arange(BAND_PAST + CHUNK) - BAND_PAST)
    prev = lambda b, j: jnp.maximum(j - 1, 0)
    y = pl.pallas_call(
        _attn_prompt_kernel,
        grid=(batch, seq // blk),
        in_specs=[
            pl.BlockSpec((None, blk, BW), lambda b, j: (b, j, 1)),
            pl.BlockSpec((None, blk, BW), lambda b, j: (b, prev(b, j), 2)),
            pl.BlockSpec((None, blk, BW), lambda b, j: (b, j, 2)),
            pl.BlockSpec((None, blk, BW), lambda b, j: (b, prev(b, j), 3)),
            pl.BlockSpec((None, blk, BW), lambda b, j: (b, j, 3)),
            pl.BlockSpec((HEADS, CHUNK, BAND_PAST + CHUNK), lambda b, j: (0, 0, 0)),
        ],
        out_specs=pl.BlockSpec((None, blk, BW), lambda b, j: (b, j, 0)),
        out_shape=jax.ShapeDtypeStruct((batch, seq, BW), BF16),
        scratch_shapes=[pltpu.VMEM((2 * blk, BW), BF16), pltpu.VMEM((2 * blk, BW), BF16)],
        compiler_params=_cparams("parallel", "parallel"),
        name="attn_prompt",
    )(mid3, mid3, mid3, mid3, mid3, bias)
    return y.reshape(batch * seq, BW)


def _attn_sample_kernel(q_ref, kc_ref, kn_ref, vc_ref, vn_ref, bias_ref, o_ref, kcat_ref, vcat_ref, *, n_cache):
    kcat_ref[0:n_cache, :] = kc_ref[...].astype(BF16)
    kcat_ref[n_cache:, :] = kn_ref[...].astype(BF16)
    vcat_ref[0:n_cache, :] = vc_ref[...].astype(BF16)
    vcat_ref[n_cache:, :] = vn_ref[...].astype(BF16)
    seq = q_ref.shape[0]
    _attn_chunks(q_ref, kcat_ref, vcat_ref, bias_ref, o_ref, n_chunks=1, cq=seq, band=n_cache + seq,
                 first_block=None)


def attn_sample(mid, cache_k, cache_v, rel_bias, batch, seq):
    n_cache = cache_k.shape[1]
    assert PAST_LEN >= n_cache
    mid3 = mid.reshape(batch, seq, 4 * BW)
    ck = cache_k.reshape(batch, n_cache, BW)
    cv = cache_v.reshape(batch, n_cache, BW)
    bias = _rel_bias_tile(rel_bias, PAST_LEN + np.arange(seq), PAST_LEN - n_cache + np.arange(n_cache + seq))
    y = pl.pallas_call(
        functools.partial(_attn_sample_kernel, n_cache=n_cache),
        grid=(batch,),
        in_specs=[
            pl.BlockSpec((None, seq, BW), lambda b: (b, 0, 1)),
            pl.BlockSpec((None, n_cache, BW), lambda b: (b, 0, 0)),
            pl.BlockSpec((None, seq, BW), lambda b: (b, 0, 2)),
            pl.BlockSpec((None, n_cache, BW), lambda b: (b, 0, 0)),
            pl.BlockSpec((None, seq, BW), lambda b: (b, 0, 3)),
            pl.BlockSpec((HEADS, seq, n_cache + seq), lambda b: (0, 0, 0)),
        ],
        out_specs=pl.BlockSpec((None, seq, BW), lambda b: (b, 0, 0)),
        out_shape=jax.ShapeDtypeStruct((batch, seq, BW), BF16),
        scratch_shapes=[pltpu.VMEM((n_cache + seq, BW), BF16), pltpu.VMEM((n_cache + seq, BW), BF16)],
        compiler_params=_cparams("parallel"),
        name="attn_sample",
    )(mid3, ck, mid3, cv, mid3, bias)
    return y.reshape(batch * seq, BW)


def _mixer_out_kernel(x_ref, ya_ref, yb_ref, yc_ref, gate_ref, wb_ref, wo_ref, o_ref):
    acc = None
    for n, y_ref in enumerate((ya_ref, yb_ref, yc_ref)):
        proj = jnp.dot(y_ref[...], wb_ref[n], preferred_element_type=F32)
        term = gate_ref[:, n * D_MODEL:(n + 1) * D_MODEL].astype(F32) * proj
        acc = term if acc is None else acc + term
    o_ref[...] = x_ref[...] + jnp.dot(acc.astype(BF16), wo_ref[...], preferred_element_type=F32)


def mixer_out(x, ya, yb, yc, gates, w_branch, w_out):
    t = x.shape[0]
    tm = min(t, 512)
    row = lambda i: (i, 0)
    return pl.pallas_call(
        _mixer_out_kernel,
        grid=(t // tm,),
        in_specs=[
            pl.BlockSpec((tm, D_MODEL), row),
            pl.BlockSpec((tm, BW), row),
            pl.BlockSpec((tm, BW), row),
            pl.BlockSpec((tm, BW), row),
            pl.BlockSpec((tm, 3 * D_MODEL), row),
            pl.BlockSpec((3, BW, D_MODEL), lambda i: (0, 0, 0)),
            pl.BlockSpec((D_MODEL, D_MODEL), lambda i: (0, 0)),
        ],
        out_specs=pl.BlockSpec((tm, D_MODEL), row),
        out_shape=jax.ShapeDtypeStruct((t, D_MODEL), F32),
        compiler_params=_cparams("parallel"),
        name="mixer_out",
    )(x, ya, yb, yc, gates, w_branch, w_out)


def _extract_top(s, payload, k):
    r = float(s.shape[0])
    rows = lax.broadcasted_iota(jnp.int32, s.shape, 0).astype(F32)
    vals, pays = [], []
    for _ in range(k):
        m = jnp.max(s, axis=0, keepdims=True)
        idx = jnp.min(jnp.where(s == m, rows, r), axis=0, keepdims=True)
        sel = rows == idx
        vals.append(m)
        pays.append(idx if payload is None else jnp.max(jnp.where(sel, payload, -1.0), axis=0, keepdims=True))
        s = jnp.where(sel, -jnp.inf, s)
    return jnp.concatenate(vals, axis=0), jnp.concatenate(pays, axis=0)


def _peer_query_kernel(x_ref, g_ref, wq_ref, sk_ref, hp_ref, idx_ref, gate_ref, q_ref, *, tm):
    hb = _rms(x_ref[...], g_ref[...]).astype(BF16)
    q_ref[...] = jnp.dot(hb, wq_ref[...], preferred_element_type=F32).astype(BF16)
    bits = lax.bitcast_convert_type(hb.astype(F32), jnp.int32)
    half = D_MODEL // 2
    hp_ref[...] = bits[:, half:] | lax.shift_right_logical(bits[:, :half], 16)

    def sub_block(sb, carry):
        tok = pl.ds(pl.multiple_of(sb * LANES, LANES), LANES)
        for hd in range(PEER_HEADS):
            sv, si = [], []
            for p in range(2):
                hp = hd * 2 + p
                q = q_ref[tok, hp * PEER_HALF:(hp + 1) * PEER_HALF]
                s = lax.dot_general(sk_ref[hp], q, (((1,), (1,)), ((), ())), preferred_element_type=F32)
                v, i = _extract_top(s, None, PEER_TOPK)
                sv.append(v)
                si.append(i)
            cand = jnp.concatenate([sv[0][a:a + 1] + sv[1] for a in range(PEER_TOPK)], axis=0)
            eid = jnp.concatenate([si[0][a:a + 1] * PEER_NKEYS + si[1] for a in range(PEER_TOPK)], axis=0)
            tv, te = _extract_top(cand, eid, PEER_TOPK)
            e = jnp.exp(tv - tv[0:1])
            gate = e / jnp.sum(e, axis=0, keepdims=True)
            idx_ref[sb, hd * PEER_TOPK:(hd + 1) * PEER_TOPK, :] = te.astype(jnp.int32)
            gate_ref[sb, hd * PEER_TOPK:(hd + 1) * PEER_TOPK, :] = gate
        return carry

    lax.fori_loop(0, tm // LANES, sub_block, 0)


def peer_query(x, norm_g, wq, subkeys):
    t = x.shape[0]
    tm = min(t, 512)
    nq = wq.shape[1]
    nsb = tm // LANES
    return pl.pallas_call(
        functools.partial(_peer_query_kernel, tm=tm),
        grid=(t // tm,),
        in_specs=[
            pl.BlockSpec((tm, D_MODEL), lambda i: (i, 0)),
            pl.BlockSpec((1, D_MODEL), lambda i: (0, 0)),
            pl.BlockSpec((D_MODEL, nq), lambda i: (0, 0)),
            pl.BlockSpec((2 * PEER_HEADS, PEER_NKEYS, PEER_HALF), lambda i: (0, 0, 0)),
        ],
        out_specs=[
            pl.BlockSpec((tm, D_MODEL // 2), lambda i: (i, 0)),
            pl.BlockSpec((nsb, PEER_SLOTS, LANES), lambda i: (i, 0, 0)),
            pl.BlockSpec((nsb, PEER_SLOTS, LANES), lambda i: (i, 0, 0)),
        ],
        out_shape=[
            jax.ShapeDtypeStruct((t, D_MODEL // 2), jnp.int32),
            jax.ShapeDtypeStruct((t // LANES, PEER_SLOTS, LANES), jnp.int32),
            jax.ShapeDtypeStruct((t // LANES, PEER_SLOTS, LANES), F32),
        ],
        scratch_shapes=[pltpu.VMEM((tm, nq), BF16)],
        compiler_params=_cparams("parallel"),
        name="peer_query",
    )(x, norm_g.reshape(1, D_MODEL), wq, subkeys)


HALF_D = D_MODEL // 2
SC_LANES = 16
SC_UNIT_ROWS = 32
SC_UNITS_PER_TOKEN = PEER_SLOTS // SC_UNIT_ROWS
SC_MAX_TOKEN_BLOCK = 32
SC_ROW_GROUP = 8
SC_HALF_VECS = HALF_D // SC_LANES
SC_OUT_VECS = 8
SC_BF16_TERMS = 4
SC_ROW_TILES = D_MODEL // LANES


def pack_expert_tables(peer_u, peer_v):
    def pack(x):
        b = lax.bitcast_convert_type(x.astype(BF16), jnp.uint16).astype(jnp.uint32)
        return (b[:, HALF_D:] << 16) | b[:, :HALF_D]

    rows = lax.bitcast_convert_type(jnp.concatenate([pack(peer_u), pack(peer_v)], axis=1), jnp.int32)
    return rows.reshape(rows.shape[0], SC_ROW_TILES, LANES)


def sc_peer_experts(table, idx, gates, hp):
    t = hp.shape[0]
    info = plsc.get_sparse_core_info()
    n_workers = info.num_cores * info.num_subcores
    tpw = t // n_workers
    token_block = min(tpw, SC_MAX_TOKEN_BLOCK)
    assert t % (n_workers * token_block) == 0
    units = token_block * SC_UNITS_PER_TOKEN
    mesh = plsc.VectorSubcoreMesh(core_axis_name="core", subcore_axis_name="subcore")
    hi_mask = jnp.int32(-65536)
    gelu_c = math.sqrt(2.0 / math.pi)

    @functools.partial(
        pl.kernel,
        out_type=jax.ShapeDtypeStruct((t, D_MODEL), F32),
        mesh=mesh,
        scratch_types=[
            pltpu.VMEM((token_block * PEER_SLOTS,), jnp.int32),
            pltpu.VMEM((token_block * PEER_SLOTS,), F32),
            pltpu.VMEM((token_block, HALF_D), jnp.int32),
            pltpu.VMEM((token_block, D_MODEL), F32),
            pltpu.VMEM((2, SC_UNIT_ROWS, SC_ROW_TILES, LANES), jnp.int32),
            pltpu.VMEM((SC_UNIT_ROWS, SC_LANES), F32),
            pltpu.VMEM((SC_UNIT_ROWS,), jnp.int32),
            pltpu.SemaphoreType.DMA((2,)),
        ],
        compiler_params=pltpu.CompilerParams(needs_layout_passes=False),
        name="peer_sc_experts",
    )
    def kern(table_hbm, idx_hbm, gate_hbm, h_hbm, out_hbm, idx_v, gate_v, h_v, out_v, rows_v, part_v, coef_v, sem):
        wid = lax.axis_index("subcore") * info.num_cores + lax.axis_index("core")
        lane = lax.iota(jnp.int32, SC_LANES)
        zero = jnp.zeros((SC_LANES,), F32)

        def gather(unit, b):
            rows = idx_v.at[pl.ds(unit * SC_UNIT_ROWS, SC_UNIT_ROWS)]
            return pltpu.make_async_copy(table_hbm.at[rows], rows_v.at[b], sem.at[b])

        def row_vec(b, r, vec):
            per_tile_row = LANES // SC_LANES
            lane0 = pl.multiple_of((vec % per_tile_row) * SC_LANES, SC_LANES)
            return rows_v[b, r, vec // per_tile_row, pl.ds(lane0, SC_LANES)]

        def unpack(w):
            return lax.bitcast_convert_type(w << 16, F32), lax.bitcast_convert_type(w & hi_mask, F32)

        def as_pairs(w):
            return plsc.bitcast(w, BF16)

        def compute(unit, b):
            tl = unit // SC_UNITS_PER_TOKEN
            q = unit % SC_UNITS_PER_TOKEN

            def row_group(rg, carry):
                def kstep(k, accs):
                    hs = []
                    for j in range(SC_BF16_TERMS):
                        off = pl.multiple_of((k * SC_BF16_TERMS + j) * SC_LANES, SC_LANES)
                        hs.append(as_pairs(h_v[tl, pl.ds(off, SC_LANES)]))
                    new = []
                    for r in range(SC_ROW_GROUP):
                        p = None
                        for j in range(SC_BF16_TERMS):
                            term = as_pairs(row_vec(b, rg * SC_ROW_GROUP + r, k * SC_BF16_TERMS + j)) * hs[j]
                            p = term if p is None else p + term
                        lo, hi = unpack(plsc.bitcast(p, jnp.int32))
                        new.append(accs[r] + lo + hi)
                    return tuple(new)

                accs = lax.fori_loop(0, SC_HALF_VECS // SC_BF16_TERMS, kstep, (zero,) * SC_ROW_GROUP)
                for r in range(SC_ROW_GROUP):
                    part_v[rg * SC_ROW_GROUP + r, :] = accs[r]
                return carry

            lax.fori_loop(0, SC_UNIT_ROWS // SC_ROW_GROUP, row_group, 0)

            for i in range(SC_UNIT_ROWS // SC_LANES):
                rows = lane + i * SC_LANES
                a = zero
                for l in range(SC_LANES):
                    a = a + plsc.load_gather(part_v, [rows, jnp.full((SC_LANES,), l, jnp.int32)])
                z = gelu_c * (a + 0.044715 * (a * a * a))
                act = a / (1.0 + jnp.exp(-2.0 * z))
                slot = pl.multiple_of(tl * PEER_SLOTS + q * SC_UNIT_ROWS + i * SC_LANES, SC_LANES)
                bits = lax.bitcast_convert_type(gate_v[pl.ds(slot, SC_LANES)] * act, jnp.int32)
                top = (bits + 0x7FFF + ((bits >> 16) & 1)) & hi_mask
                coef_v[pl.ds(i * SC_LANES, SC_LANES)] = top | lax.shift_right_logical(top, 16)

            def out_pass(dq, carry):
                def row_quad(rq, accs):
                    cs = []
                    for j in range(SC_BF16_TERMS):
                        row = jnp.full((SC_LANES,), rq * SC_BF16_TERMS + j, jnp.int32)
                        cs.append(as_pairs(plsc.load_gather(coef_v, [row])))
                    new = []
                    for k in range(SC_OUT_VECS):
                        p = None
                        for j in range(SC_BF16_TERMS):
                            w = row_vec(b, rq * SC_BF16_TERMS + j, SC_HALF_VECS + dq * SC_OUT_VECS + k)
                            term = as_pairs(w) * cs[j]
                            p = term if p is None else p + term
                        lo, hi = unpack(plsc.bitcast(p, jnp.int32))
                        new.append(accs[2 * k] + lo)
                        new.append(accs[2 * k + 1] + hi)
                    return tuple(new)

                accs = lax.fori_loop(0, SC_UNIT_ROWS // SC_BF16_TERMS, row_quad, (zero,) * (2 * SC_OUT_VECS))
                for k in range(SC_OUT_VECS):
                    off = pl.multiple_of((dq * SC_OUT_VECS + k) * SC_LANES, SC_LANES)
                    plsc.addupdate(out_v.at[tl, pl.ds(off, SC_LANES)], accs[2 * k])
                    plsc.addupdate(out_v.at[tl, pl.ds(HALF_D + off, SC_LANES)], accs[2 * k + 1])
                return carry

            lax.fori_loop(0, SC_HALF_VECS // SC_OUT_VECS, out_pass, 0)

        @pl.loop(0, tpw // token_block)
        def _(blk):
            tok0 = wid * tpw + blk * token_block
            slots = pl.ds(tok0 * PEER_SLOTS, token_block * PEER_SLOTS)
            pltpu.sync_copy(idx_hbm.at[slots], idx_v)
            pltpu.sync_copy(gate_hbm.at[slots], gate_v)
            pltpu.sync_copy(h_hbm.at[pl.ds(tok0, token_block)], h_v)

            @pl.loop(0, token_block)
            def _(tl):
                @pl.loop(0, D_MODEL // SC_LANES)
                def _(k):
                    out_v[tl, pl.ds(pl.multiple_of(k * SC_LANES, SC_LANES), SC_LANES)] = zero

            gather(0, 0).start()

            @pl.loop(0, units, step=2)
            def _(u0):
                for b in range(2):
                    unit = u0 + b

                    @pl.when(unit + 1 < units)
                    def _():
                        gather(unit + 1, 1 - b).start()

                    gather(unit, b).wait()
                    compute(unit, b)

            pltpu.sync_copy(out_v, out_hbm.at[pl.ds(tok0, token_block)])

    return kern(table, idx, gates, hp)


TC_PEER_TOKENS = 8
U_TILE_ROWS = SC_ROW_TILES // 2


def _tc_peer_kernel(idx_ref, idx_next_ref, gate_ref, hp_ref, table_hbm, o_ref, rows_buf, a_buf, c_buf, sem):
    step = pl.program_id(0)
    hi_mask = jnp.int32(-65536)

    def unpack(w):
        return lax.bitcast_convert_type(w << 16, F32), lax.bitcast_convert_type(w & hi_mask, F32)

    def row_copy(ids_ref, tok, e, slot):
        return pltpu.make_async_copy(table_hbm.at[ids_ref[tok * PEER_SLOTS + e]], rows_buf.at[slot, e], sem.at[slot])

    def wait_rows(slot):
        pltpu.make_async_copy(table_hbm.at[pl.ds(0, PEER_SLOTS)], rows_buf.at[slot], sem.at[slot]).wait()

    @pl.when(step == 0)
    def _():
        def first(e, c):
            row_copy(idx_ref, 0, e, 0).start()
            return c

        lax.fori_loop(0, PEER_SLOTS, first, 0, unroll=8)

    gates_t = gate_ref[...].T
    zeros_v = jnp.zeros((SC_ROW_TILES - U_TILE_ROWS, LANES), F32)
    for t in range(TC_PEER_TOKENS):
        slot = t % 2
        wait_rows(slot)
        h_lo, h_hi = unpack(hp_ref[t])
        h_lo = jnp.concatenate([h_lo, zeros_v], axis=0)
        h_hi = jnp.concatenate([h_hi, zeros_v], axis=0)
        last = t == TC_PEER_TOKENS - 1

        def u_body(e, c, slot=slot, t=t, last=last, h_lo=h_lo, h_hi=h_hi):
            if last:
                @pl.when(step + 1 < pl.num_programs(0))
                def _():
                    row_copy(idx_next_ref, 0, e, 1 - slot).start()
            else:
                row_copy(idx_ref, t + 1, e, 1 - slot).start()
            lo, hi = unpack(rows_buf[slot, e])
            a_buf[pl.ds(e, 1), :] = jnp.sum(lo * h_lo + hi * h_hi, axis=0, keepdims=True)
            return c

        lax.fori_loop(0, PEER_SLOTS, u_body, 0, unroll=8)
        a = jnp.sum(a_buf[...], axis=1, keepdims=True)
        c_buf[...] = jnp.broadcast_to(gates_t[:, t:t + 1] * _gelu(a), (PEER_SLOTS, LANES))

        def v_body(e, acc, slot=slot):
            lo, hi = unpack(rows_buf[slot, e])
            c = c_buf[pl.ds(e, 1), :]
            return acc[0] + c * lo, acc[1] + c * hi

        zero = jnp.zeros((SC_ROW_TILES, LANES), F32)
        acc_lo, acc_hi = lax.fori_loop(0, PEER_SLOTS, v_body, (zero, zero), unroll=8)
        o_ref[t, 0:U_TILE_ROWS, :] = acc_lo[U_TILE_ROWS:, :]
        o_ref[t, U_TILE_ROWS:, :] = acc_hi[U_TILE_ROWS:, :]


def tc_peer_experts(table, idx, gates, hp):
    t = hp.shape[0]
    n_steps = t // TC_PEER_TOKENS
    ids = TC_PEER_TOKENS * PEER_SLOTS
    smem = functools.partial(pl.BlockSpec, memory_space=pltpu.SMEM)
    out = pl.pallas_call(
        _tc_peer_kernel,
        grid=(n_steps,),
        in_specs=[
            smem((ids,), lambda i: (i,)),
            smem((ids,), lambda i: (jnp.minimum(i + 1, n_steps - 1),)),
            pl.BlockSpec((TC_PEER_TOKENS, PEER_SLOTS), lambda i: (i, 0)),
            pl.BlockSpec((TC_PEER_TOKENS, U_TILE_ROWS, LANES), lambda i: (i, 0, 0)),
            pl.BlockSpec(memory_space=pl.ANY),
        ],
        out_specs=pl.BlockSpec((TC_PEER_TOKENS, SC_ROW_TILES, LANES), lambda i: (i, 0, 0)),
        out_shape=jax.ShapeDtypeStruct((t, SC_ROW_TILES, LANES), F32),
        scratch_shapes=[
            pltpu.VMEM((2, PEER_SLOTS, SC_ROW_TILES, LANES), jnp.int32),
            pltpu.VMEM((PEER_SLOTS, LANES), F32),
            pltpu.VMEM((PEER_SLOTS, LANES), F32),
            pltpu.SemaphoreType.DMA((2,)),
        ],
        compiler_params=pltpu.CompilerParams(dimension_semantics=("arbitrary",), vmem_limit_bytes=VMEM_LIMIT,
                                             disable_bounds_checks=True),
        name="tc_peer_experts",
    )(idx, idx, gates, hp.reshape(t, U_TILE_ROWS, LANES), table)
    return out.reshape(t, D_MODEL)


TC_PEER_SHARE = 8


def peer_block(x, norm_g, wq, subkeys, table):
    t = x.shape[0]
    hp, idx3, gate3 = peer_query(x, norm_g, wq, subkeys)
    idx = jnp.transpose(idx3, (0, 2, 1)).reshape(t * PEER_SLOTS)
    gates = jnp.transpose(gate3, (0, 2, 1)).reshape(t * PEER_SLOTS)
    info = plsc.get_sparse_core_info()
    sc_round = info.num_cores * info.num_subcores * SC_MAX_TOKEN_BLOCK
    t_tc = (t // TC_PEER_SHARE) // sc_round * sc_round
    if t_tc == 0:
        return sc_peer_experts(table, idx, gates, hp)
    t_sc = t - t_tc
    on_tc = tc_peer_experts(table, idx[t_sc * PEER_SLOTS:], gates[t_sc * PEER_SLOTS:].reshape(t_tc, PEER_SLOTS),
                            hp[t_sc:])
    on_sc = sc_peer_experts(table, idx[:t_sc * PEER_SLOTS], gates[:t_sc * PEER_SLOTS], hp[:t_sc])
    return jnp.concatenate([on_sc, on_tc], axis=0)


def _ple_kernel(x_ref, ffn_ref, p_ref, g_ref, wg_ref, wp_ref, gf_ref, o_ref, *, final):
    x = x_ref[...] + ffn_ref[...]
    gate = _sigmoid(jnp.dot(_rms(x, g_ref[...]).astype(BF16), wg_ref[...], preferred_element_type=F32))
    emb = jnp.dot(p_ref[...].astype(BF16), wp_ref[...], preferred_element_type=F32)
    y = x + gate * emb
    o_ref[...] = _rms(y, gf_ref[...]) if final else y


def ple_block(x, ffn, p, norm_g, w_gate, w_proj, norm_final, final):
    t = x.shape[0]
    tm = min(t, 512)
    row = lambda i: (i, 0)
    full = lambda i: (0, 0)
    return pl.pallas_call(
        functools.partial(_ple_kernel, final=final),
        grid=(t // tm,),
        in_specs=[
            pl.BlockSpec((tm, D_MODEL), row),
            pl.BlockSpec((tm, D_MODEL), row),
            pl.BlockSpec((tm, PLE_DIM), row),
            pl.BlockSpec((1, D_MODEL), full),
            pl.BlockSpec((D_MODEL, D_MODEL), full),
            pl.BlockSpec((PLE_DIM, D_MODEL), full),
            pl.BlockSpec((1, D_MODEL), full),
        ],
        out_specs=pl.BlockSpec((tm, D_MODEL), row),
        out_shape=jax.ShapeDtypeStruct((t, D_MODEL), F32),
        compiler_params=_cparams("parallel"),
        name="ple_block",
    )(x, ffn, p, norm_g.reshape(1, D_MODEL), w_gate, w_proj, norm_final.reshape(1, D_MODEL))


def _prompt_groups(batch):
    sizes = []
    while sum(sizes) < batch:
        nxt = 1 if len(sizes) < 2 else -(-sizes[-1] * 7 // 5)
        sizes.append(min(nxt, batch - sum(sizes)))
    return sizes
def _trunk_layer(x, ple, lw, batch, seq, pool_prefix, pool_start, cache, final):
    uv = norm_matmul(x, lw["norm_mix"], lw["w_uv"], "gelu", F32, 2 * BW)
    mid = norm_matmul(x, lw["norm_mix"], lw["w_mid"], "none", F32, 4 * BW)
    gates = norm_matmul(x, lw["norm_mix"], lw["w_gates"], "sigmoid", BF16, D_MODEL)
    ya, vn = gmlp_mix(uv, lw["gmlp_ln_g"], lw["gmlp_ln_b"], lw["gmlp_ws"], lw["gmlp_bs"], min(seq, GMLP_CHUNK))
    yb, pool_state = pool_mix(mid, pool_prefix, pool_start, lw["pool_w"], lw["pool_scale"], batch, seq)
    if cache is None:
        yc = attn_prompt(mid, lw["rel_bias"], batch, seq)
    else:
        yc = attn_sample(mid, cache[0], cache[1], lw["rel_bias"], batch, seq)
    x = mixer_out(x, ya, yb, yc, gates, lw["w_branch"], lw["w_out"])
    ffn = peer_block(x, lw["norm_ffn"], lw["peer_wq"], lw["peer_subkeys"], lw["peer_table"])
    x = ple_block(x, ffn, ple, lw["norm_ple"], lw["ple_gate"], lw["ple_proj"], lw["norm_final"], final)
    return x, mid, pool_state, vn


def kernel(x_prompt, x_sample, cache_attn_k, cache_attn_v, state_pool, p_prompt, p_sample, norm_mix, w_in, gmlp_ln_g, gmlp_ln_b, gmlp_ws, gmlp_bs, pool_w, pool_scale, attn_rel_bias, w_branch, w_out, norm_ffn, peer_wq, peer_subkeys, peer_u, peer_v, norm_ple, ple_gate, ple_proj, norm_final):
    bp, lp, _ = x_prompt.shape
    bs, ls, _ = x_sample.shape
    assert lp % BAND_PAST == 0 and lp % GMLP_CHUNK == 0 and ls <= CHUNK
    n_keep = min(BAND_PAST, lp)
    sizes = _prompt_groups(bp)
    starts = [sum(sizes[:g]) for g in range(len(sizes))]
    xg = [x_prompt[a:a + n].reshape(n * lp, D_MODEL) for a, n in zip(starts, sizes)]
    xs = x_sample.reshape(bs * ls, D_MODEL)
    outs = {k: [] for k in ("pk", "pv", "pps", "sk", "sv", "sps", "sgv")}
    for i in range(DEPTH):
        w_in_b = w_in[i].astype(BF16)
        lw = dict(
            norm_mix=norm_mix[i],
            w_uv=w_in_b[:, :2 * BW],
            w_mid=w_in_b[:, 2 * BW:6 * BW],
            w_gates=w_in_b[:, 6 * BW:],
            gmlp_ln_g=gmlp_ln_g[i], gmlp_ln_b=gmlp_ln_b[i], gmlp_ws=gmlp_ws[i], gmlp_bs=gmlp_bs[i],
            pool_w=pool_w[i], pool_scale=pool_scale[i], rel_bias=attn_rel_bias[i],
            w_branch=w_branch[i].astype(BF16), w_out=w_out[i].astype(BF16),
            norm_ffn=norm_ffn[i], peer_wq=peer_wq[i].astype(BF16),
            peer_subkeys=peer_subkeys[i].reshape(2 * PEER_HEADS, PEER_NKEYS, PEER_HALF).astype(BF16),
            peer_table=pack_expert_tables(peer_u[i], peer_v[i]),
            norm_ple=norm_ple[i], ple_gate=ple_gate[i].astype(BF16), ple_proj=ple_proj[i].astype(BF16),
            norm_final=norm_final,
        )
        final = i == DEPTH - 1
        pk, pv, pps = [], [], []
        for g, (a, bg) in enumerate(zip(starts, sizes)):
            ple_g = p_prompt[i, a:a + bg].reshape(bg * lp, PLE_DIM)
            zero_prefix = jnp.zeros((bg, POOL_STATE, BW), F32)
            xg[g], mid_p, ps_p, _ = _trunk_layer(xg[g], ple_g, lw, bg, lp, zero_prefix, 0, None, final)
            mid_p = mid_p.reshape(bg, lp, 4 * BW)
            pk.append(mid_p[:, lp - n_keep:, 2 * BW:3 * BW].reshape(bg, n_keep, HEADS, HEAD_DIM))
            pv.append(mid_p[:, lp - n_keep:, 3 * BW:].reshape(bg, n_keep, HEADS, HEAD_DIM))
            pps.append(ps_p)
        outs["pk"].append(jnp.concatenate(pk, axis=0))
        outs["pv"].append(jnp.concatenate(pv, axis=0))
        outs["pps"].append(jnp.concatenate(pps, axis=0))
        xs, mid_s, ps_s, vn_s = _trunk_layer(xs, p_sample[i].reshape(bs * ls, PLE_DIM), lw, bs, ls, state_pool[i],
                                             PAST_LEN, (cache_attn_k[i], cache_attn_v[i]), final)
        mid_s = mid_s.reshape(bs, ls, 4 * BW)
        outs["sk"].append(mid_s[:, :, 2 * BW:3 * BW].reshape(bs, ls, HEADS, HEAD_DIM))
        outs["sv"].append(mid_s[:, :, 3 * BW:].reshape(bs, ls, HEADS, HEAD_DIM))
        outs["sps"].append(ps_s)
        outs["sgv"].append(vn_s.reshape(bs, ls, BW))
    st = lambda k: jnp.stack(outs[k])
    y_prompt = jnp.concatenate(xg, axis=0).reshape(bp, lp, D_MODEL)
    return (y_prompt, xs.reshape(bs, ls, D_MODEL), st("pk"), st("pv"), st("pps"),
            st("sk"), st("sv"), st("sps"), st("sgv"))
```

```python
import functools
import math

import jax
import jax.numpy as jnp
import numpy as np
from jax import lax
from jax.experimental import pallas as pl
from jax.experimental.pallas import tpu as pltpu
from jax.experimental.pallas import tpu_sc as plsc

F32 = jnp.float32
BF16 = jnp.bfloat16

D_MODEL = 1024
DEPTH = 2
CHUNK = 64
EPS = 1e-6
BW = D_MODEL // 2
GMLP_CHUNK = 128
GROUPS = 4
GDIM = BW // GROUPS
POOL_WINDOWS = (2, 4, 8, 16)
POOL_STATE = 15
POOL_PAD = 16
HEADS = 8
HEAD_DIM = BW // HEADS
BAND_CHUNKS = 8
BAND_PAST = BAND_CHUNKS * CHUNK
REL_CLIP = 128
PAST_LEN = 4096
PEER_HEADS = 8
PEER_NKEYS = 128
PEER_HALF = 128
PEER_TOPK = 16
PEER_SLOTS = PEER_HEADS * PEER_TOPK
PLE_DIM = 256

LANES = 128
VMEM_LIMIT = 56 * 1024 * 1024
NEG = -1e30


def _cparams(*sem):
    return pltpu.CompilerParams(dimension_semantics=sem, vmem_limit_bytes=VMEM_LIMIT)


def _rms(x, g):
    ms = jnp.mean(x * x, axis=-1, keepdims=True)
    return x * lax.rsqrt(ms + EPS) * g


def _gelu(x):
    c = math.sqrt(2.0 / math.pi)
    return 0.5 * x * (1.0 + jnp.tanh(c * (x + 0.044715 * (x * x * x))))


def _sigmoid(x):
    return 1.0 / (1.0 + jnp.exp(-x))


_ACTS = {"gelu": _gelu, "sigmoid": _sigmoid, "none": lambda z: z}


def _norm_matmul_kernel(x_ref, g_ref, w_ref, o_ref, h_ref, *, act):
    @pl.when(pl.program_id(1) == 0)
    def _():
        h_ref[...] = _rms(x_ref[...], g_ref[...]).astype(BF16)

    z = jnp.dot(h_ref[...], w_ref[...], preferred_element_type=F32)
    o_ref[...] = _ACTS[act](z).astype(o_ref.dtype)


def norm_matmul(x, g, w, act, out_dtype, tn):
    t, d = x.shape
    n = w.shape[1]
    tm = min(t, 512)
    return pl.pallas_call(
        functools.partial(_norm_matmul_kernel, act=act),
        grid=(t // tm, n // tn),
        in_specs=[
            pl.BlockSpec((tm, d), lambda i, j: (i, 0)),
            pl.BlockSpec((1, d), lambda i, j: (0, 0)),
            pl.BlockSpec((d, tn), lambda i, j: (0, j)),
        ],
        out_specs=pl.BlockSpec((tm, tn), lambda i, j: (i, j)),
        out_shape=jax.ShapeDtypeStruct((t, n), out_dtype),
        scratch_shapes=[pltpu.VMEM((tm, d), BF16)],
        compiler_params=_cparams("parallel", "arbitrary"),
        name="norm_matmul_" + act,
    )(x, g.reshape(1, d), w)


def _gmlp_kernel(uv_ref, lng_ref, lnb_ref, ws_ref, bst_ref, y_ref, vn_ref, *, lc):
    u = uv_ref[:, :BW]
    v = uv_ref[:, BW:]
    mu = jnp.mean(v, axis=-1, keepdims=True)
    vc = v - mu
    var = jnp.mean(vc * vc, axis=-1, keepdims=True)
    vn = vc * lax.rsqrt(var + EPS) * lng_ref[...] + lnb_ref[...]
    vn_ref[...] = vn
    row = lax.broadcasted_iota(jnp.int32, (lc, lc), 0) // CHUNK
    col = lax.broadcasted_iota(jnp.int32, (lc, lc), 1) // CHUNK
    causal = col <= row
    vnb = vn.astype(BF16)
    for g in range(GROUPS):
        w = jnp.where(causal, ws_ref[g], 0.0).astype(BF16)
        s = jnp.dot(w, vnb[:, g * GDIM:(g + 1) * GDIM], preferred_element_type=F32)
        s = s + bst_ref[:, g:g + 1]
        y_ref[:, g * GDIM:(g + 1) * GDIM] = (u[:, g * GDIM:(g + 1) * GDIM] * s).astype(y_ref.dtype)


def gmlp_mix(uv, ln_g, ln_b, ws, bs, lc):
    t = uv.shape[0]
    return pl.pallas_call(
        functools.partial(_gmlp_kernel, lc=lc),
        grid=(t // lc,),
        in_specs=[
            pl.BlockSpec((lc, 2 * BW), lambda i: (i, 0)),
            pl.BlockSpec((1, BW), lambda i: (0, 0)),
            pl.BlockSpec((1, BW), lambda i: (0, 0)),
            pl.BlockSpec((GROUPS, lc, lc), lambda i: (0, 0, 0)),
            pl.BlockSpec((lc, GROUPS), lambda i: (0, 0)),
        ],
        out_specs=[
            pl.BlockSpec((lc, BW), lambda i: (i, 0)),
            pl.BlockSpec((lc, BW), lambda i: (i, 0)),
        ],
        out_shape=[
            jax.ShapeDtypeStruct((t, BW), BF16),
            jax.ShapeDtypeStruct((t, BW), F32),
        ],
        compiler_params=_cparams("parallel"),
        name="gmlp_mix",
    )(uv, ln_g.reshape(1, BW), ln_b.reshape(1, BW), ws[:, :lc, :lc], bs[:, :lc].T)


def _pool_kernel(u_ref, pre_ref, w_ref, sc_ref, y_ref, st_ref, pad_ref, *, seq, start_pos):
    pad_ref[0:POOL_PAD, :] = pre_ref[...]
    pad_ref[POOL_PAD:, :] = u_ref[...]
    pos = lax.broadcasted_iota(jnp.int32, (seq, 1), 0) + start_pos
    for g, win in enumerate(POOL_WINDOWS):
        cols = slice(g * GDIM, (g + 1) * GDIM)
        tok = pad_ref[POOL_PAD:, cols]
        acc = tok
        for k in range(1, win):
            acc = acc + pad_ref[POOL_PAD - k:POOL_PAD - k + seq, cols]
        cnt = jnp.minimum(pos + 1, win).astype(F32)
        d = acc / cnt - tok
        y = jnp.dot(d.astype(BF16), w_ref[g], preferred_element_type=F32)
        y_ref[:, cols] = (y * sc_ref[:, cols]).astype(y_ref.dtype)
    st_ref[...] = pad_ref[seq + 1:seq + POOL_PAD, :]


def pool_mix(mid, prefix, start_pos, pool_w, pool_scale, batch, seq):
    mid3 = mid.reshape(batch, seq, 4 * BW)
    pre = jnp.concatenate([jnp.zeros((batch, 1, BW), F32), prefix], axis=1)
    y, st = pl.pallas_call(
        functools.partial(_pool_kernel, seq=seq, start_pos=start_pos),
        grid=(batch,),
        in_specs=[
            pl.BlockSpec((None, seq, BW), lambda b: (b, 0, 0)),
            pl.BlockSpec((None, POOL_PAD, BW), lambda b: (b, 0, 0)),
            pl.BlockSpec((GROUPS, GDIM, GDIM), lambda b: (0, 0, 0)),
            pl.BlockSpec((1, BW), lambda b: (0, 0)),
        ],
        out_specs=[
            pl.BlockSpec((None, seq, BW), lambda b: (b, 0, 0)),
            pl.BlockSpec((None, POOL_STATE, BW), lambda b: (b, 0, 0)),
        ],
        out_shape=[
            jax.ShapeDtypeStruct((batch, seq, BW), BF16),
            jax.ShapeDtypeStruct((batch, POOL_STATE, BW), F32),
        ],
        scratch_shapes=[pltpu.VMEM((seq + POOL_PAD, BW), F32)],
        compiler_params=_cparams("parallel"),
        name="pool_mix",
    )(mid3, pre, pool_w.astype(BF16), pool_scale.reshape(1, BW))
    return y.reshape(batch * seq, BW), st


def _attn_chunks(q_ref, kcat_ref, vcat_ref, bias_ref, o_ref, *, n_chunks, cq, band, first_block):
    scale = HEAD_DIM ** -0.5
    for h in range(HEADS):
        cols = slice(h * HEAD_DIM, (h + 1) * HEAD_DIM)
        bias = bias_ref[h]
        for i in range(n_chunks):
            q = (q_ref[i * cq:(i + 1) * cq, cols] * scale).astype(BF16)
            k = kcat_ref[i * cq:i * cq + band, cols]
            v = vcat_ref[i * cq:i * cq + band, cols]
            s = lax.dot_general(q, k, (((1,), (1,)), ((), ())), preferred_element_type=F32) + bias
            if first_block is not None:
                key = lax.broadcasted_iota(jnp.int32, (1, band), 1)
                s = jnp.where(key >= first_block * (BAND_PAST - i * cq), s, NEG)
            m = jnp.max(s, axis=-1, keepdims=True)
            p = jnp.exp(s - m)
            l = jnp.sum(p, axis=-1, keepdims=True)
            o = jnp.dot(p.astype(BF16), v, preferred_element_type=F32) / l
            o_ref[i * cq:(i + 1) * cq, cols] = o.astype(o_ref.dtype)


def _attn_prompt_kernel(q_ref, kp_ref, ko_ref, vp_ref, vo_ref, bias_ref, o_ref, kcat_ref, vcat_ref):
    kcat_ref[0:BAND_PAST, :] = kp_ref[...].astype(BF16)
    kcat_ref[BAND_PAST:, :] = ko_ref[...].astype(BF16)
    vcat_ref[0:BAND_PAST, :] = vp_ref[...].astype(BF16)
    vcat_ref[BAND_PAST:, :] = vo_ref[...].astype(BF16)
    _attn_chunks(q_ref, kcat_ref, vcat_ref, bias_ref, o_ref, n_chunks=BAND_CHUNKS, cq=CHUNK,
                 band=BAND_PAST + CHUNK, first_block=(pl.program_id(1) == 0).astype(jnp.int32))


def _rel_bias_tile(rel_bias, qpos, kpos):
    rel = np.clip(qpos[:, None] - kpos[None, :], -REL_CLIP, REL_CLIP) + REL_CLIP
    return rel_bias[:, rel]


def attn_prompt(mid, rel_bias, batch, seq):
    mid3 = mid.reshape(batch, seq, 4 * BW)
    blk = BAND_PAST
    bias = _rel_bias_tile(rel_bias, np.arange(CHUNK), np.arange(BAND_PAST + CHUNK) - BAND_PAST)
    prev = lambda b, j: jnp.maximum(j - 1, 0)
    y = pl.pallas_call(
        _attn_prompt_kernel,
        grid=(batch, seq // blk),
        in_specs=[
            pl.BlockSpec((None, blk, BW), lambda b, j: (b, j, 1)),
            pl.BlockSpec((None, blk, BW), lambda b, j: (b, prev(b, j), 2)),
            pl.BlockSpec((None, blk, BW), lambda b, j: (b, j, 2)),
            pl.BlockSpec((None, blk, BW), lambda b, j: (b, prev(b, j), 3)),
            pl.BlockSpec((None, blk, BW), lambda b, j: (b, j, 3)),
            pl.BlockSpec((HEADS, CHUNK, BAND_PAST + CHUNK), lambda b, j: (0, 0, 0)),
        ],
        out_specs=pl.BlockSpec((None, blk, BW), lambda b, j: (b, j, 0)),
        out_shape=jax.ShapeDtypeStruct((batch, seq, BW), BF16),
        scratch_shapes=[pltpu.VMEM((2 * blk, BW), BF16), pltpu.VMEM((2 * blk, BW), BF16)],
        compiler_params=_cparams("parallel", "parallel"),
        name="attn_prompt",
    )(mid3, mid3, mid3, mid3, mid3, bias)
    return y.reshape(batch * seq, BW)


def _attn_sample_kernel(q_ref, kc_ref, kn_ref, vc_ref, vn_ref, bias_ref, o_ref, kcat_ref, vcat_ref, *, n_cache):
    kcat_ref[0:n_cache, :] = kc_ref[...].astype(BF16)
    kcat_ref[n_cache:, :] = kn_ref[...].astype(BF16)
    vcat_ref[0:n_cache, :] = vc_ref[...].astype(BF16)
    vcat_ref[n_cache:, :] = vn_ref[...].astype(BF16)
    seq = q_ref.shape[0]
    _attn_chunks(q_ref, kcat_ref, vcat_ref, bias_ref, o_ref, n_chunks=1, cq=seq, band=n_cache + seq,
                 first_block=None)


def attn_sample(mid, cache_k, cache_v, rel_bias, batch, seq):
    n_cache = cache_k.shape[1]
    assert PAST_LEN >= n_cache
    mid3 = mid.reshape(batch, seq, 4 * BW)
    ck = cache_k.reshape(batch, n_cache, BW)
    cv = cache_v.reshape(batch, n_cache, BW)
    bias = _rel_bias_tile(rel_bias, PAST_LEN + np.arange(seq), PAST_LEN - n_cache + np.arange(n_cache + seq))
    y = pl.pallas_call(
        functools.partial(_attn_sample_kernel, n_cache=n_cache),
        grid=(batch,),
        in_specs=[
            pl.BlockSpec((None, seq, BW), lambda b: (b, 0, 1)),
            pl.BlockSpec((None, n_cache, BW), lambda b: (b, 0, 0)),
            pl.BlockSpec((None, seq, BW), lambda b: (b, 0, 2)),
            pl.BlockSpec((None, n_cache, BW), lambda b: (b, 0, 0)),
            pl.BlockSpec((None, seq, BW), lambda b: (b, 0, 3)),
            pl.BlockSpec((HEADS, seq, n_cache + seq), lambda b: (0, 0, 0)),
        ],
        out_specs=pl.BlockSpec((None, seq, BW), lambda b: (b, 0, 0)),
        out_shape=jax.ShapeDtypeStruct((batch, seq, BW), BF16),
        scratch_shapes=[pltpu.VMEM((n_cache + seq, BW), BF16), pltpu.VMEM((n_cache + seq, BW), BF16)],
        compiler_params=_cparams("parallel"),
        name="attn_sample",
    )(mid3, ck, mid3, cv, mid3, bias)
    return y.reshape(batch * seq, BW)


def _mixer_out_kernel(x_ref, ya_ref, yb_ref, yc_ref, gate_ref, wb_ref, wo_ref, o_ref):
    acc = None
    for n, y_ref in enumerate((ya_ref, yb_ref, yc_ref)):
        proj = jnp.dot(y_ref[...], wb_ref[n], preferred_element_type=F32)
        term = gate_ref[:, n * D_MODEL:(n + 1) * D_MODEL].astype(F32) * proj
        acc = term if acc is None else acc + term
    o_ref[...] = x_ref[...] + jnp.dot(acc.astype(BF16), wo_ref[...], preferred_element_type=F32)


def mixer_out(x, ya, yb, yc, gates, w_branch, w_out):
    t = x.shape[0]
    tm = min(t, 512)
    row = lambda i: (i, 0)
    return pl.pallas_call(
        _mixer_out_kernel,
        grid=(t // tm,),
        in_specs=[
            pl.BlockSpec((tm, D_MODEL), row),
            pl.BlockSpec((tm, BW), row),
            pl.BlockSpec((tm, BW), row),
            pl.BlockSpec((tm, BW), row),
            pl.BlockSpec((tm, 3 * D_MODEL), row),
            pl.BlockSpec((3, BW, D_MODEL), lambda i: (0, 0, 0)),
            pl.BlockSpec((D_MODEL, D_MODEL), lambda i: (0, 0)),
        ],
        out_specs=pl.BlockSpec((tm, D_MODEL), row),
        out_shape=jax.ShapeDtypeStruct((t, D_MODEL), F32),
        compiler_params=_cparams("parallel"),
        name="mixer_out",
    )(x, ya, yb, yc, gates, w_branch, w_out)


def _extract_top(s, payload, k):
    r = float(s.shape[0])
    rows = lax.broadcasted_iota(jnp.int32, s.shape, 0).astype(F32)
    vals, pays = [], []
    for _ in range(k):
        m = jnp.max(s, axis=0, keepdims=True)
        idx = jnp.min(jnp.where(s == m, rows, r), axis=0, keepdims=True)
        sel = rows == idx
        vals.append(m)
        pays.append(idx if payload is None else jnp.max(jnp.where(sel, payload, -1.0), axis=0, keepdims=True))
        s = jnp.where(sel, -jnp.inf, s)
    return jnp.concatenate(vals, axis=0), jnp.concatenate(pays, axis=0)


def _peer_query_kernel(x_ref, g_ref, wq_ref, sk_ref, hp_ref, idx_ref, gate_ref, q_ref, *, tm):
    hb = _rms(x_ref[...], g_ref[...]).astype(BF16)
    q_ref[...] = jnp.dot(hb, wq_ref[...], preferred_element_type=F32).astype(BF16)
    bits = lax.bitcast_convert_type(hb.astype(F32), jnp.int32)
    half = D_MODEL // 2
    hp_ref[...] = bits[:, half:] | lax.shift_right_logical(bits[:, :half], 16)

    def sub_block(sb, carry):
        tok = pl.ds(pl.multiple_of(sb * LANES, LANES), LANES)
        for hd in range(PEER_HEADS):
            sv, si = [], []
            for p in range(2):
                hp = hd * 2 + p
                q = q_ref[tok, hp * PEER_HALF:(hp + 1) * PEER_HALF]
                s = lax.dot_general(sk_ref[hp], q, (((1,), (1,)), ((), ())), preferred_element_type=F32)
                v, i = _extract_top(s, None, PEER_TOPK)
                sv.append(v)
                si.append(i)
            cand = jnp.concatenate([sv[0][a:a + 1] + sv[1] for a in range(PEER_TOPK)], axis=0)
            eid = jnp.concatenate([si[0][a:a + 1] * PEER_NKEYS + si[1] for a in range(PEER_TOPK)], axis=0)
            tv, te = _extract_top(cand, eid, PEER_TOPK)
            e = jnp.exp(tv - tv[0:1])
            gate = e / jnp.sum(e, axis=0, keepdims=True)
            idx_ref[sb, hd * PEER_TOPK:(hd + 1) * PEER_TOPK, :] = te.astype(jnp.int32)
            gate_ref[sb, hd * PEER_TOPK:(hd + 1) * PEER_TOPK, :] = gate
        return carry

    lax.fori_loop(0, tm // LANES, sub_block, 0)


def peer_query(x, norm_g, wq, subkeys):
    t = x.shape[0]
    tm = min(t, 512)
    nq = wq.shape[1]
    nsb = tm // LANES
    return pl.pallas_call(
        functools.partial(_peer_query_kernel, tm=tm),
        grid=(t // tm,),
        in_specs=[
            pl.BlockSpec((tm, D_MODEL), lambda i: (i, 0)),
            pl.BlockSpec((1, D_MODEL), lambda i: (0, 0)),
            pl.BlockSpec((D_MODEL, nq), lambda i: (0, 0)),
            pl.BlockSpec((2 * PEER_HEADS, PEER_NKEYS, PEER_HALF), lambda i: (0, 0, 0)),
        ],
        out_specs=[
            pl.BlockSpec((tm, D_MODEL // 2), lambda i: (i, 0)),
            pl.BlockSpec((nsb, PEER_SLOTS, LANES), lambda i: (i, 0, 0)),
            pl.BlockSpec((nsb, PEER_SLOTS, LANES), lambda i: (i, 0, 0)),
        ],
        out_shape=[
            jax.ShapeDtypeStruct((t, D_MODEL // 2), jnp.int32),
            jax.ShapeDtypeStruct((t // LANES, PEER_SLOTS, LANES), jnp.int32),
            jax.ShapeDtypeStruct((t // LANES, PEER_SLOTS, LANES), F32),
        ],
        scratch_shapes=[pltpu.VMEM((tm, nq), BF16)],
        compiler_params=_cparams("parallel"),
        name="peer_query",
    )(x, norm_g.reshape(1, D_MODEL), wq, subkeys)


HALF_D = D_MODEL // 2
SC_LANES = 16
SC_UNIT_ROWS = 32
SC_UNITS_PER_TOKEN = PEER_SLOTS // SC_UNIT_ROWS
SC_MAX_TOKEN_BLOCK = 32
SC_ROW_GROUP = 8
SC_HALF_VECS = HALF_D // SC_LANES
SC_OUT_VECS = 8
SC_BF16_TERMS = 4
SC_ROW_TILES = D_MODEL // LANES


def pack_expert_tables(peer_u, peer_v):
    def pack(x):
        b = lax.bitcast_convert_type(x.astype(BF16), jnp.uint16).astype(jnp.uint32)
        return (b[:, HALF_D:] << 16) | b[:, :HALF_D]

    rows = lax.bitcast_convert_type(jnp.concatenate([pack(peer_u), pack(peer_v)], axis=1), jnp.int32)
    return rows.reshape(rows.shape[0], SC_ROW_TILES, LANES)


def sc_peer_experts(table, idx, gates, hp):
    t = hp.shape[0]
    info = plsc.get_sparse_core_info()
    n_workers = info.num_cores * info.num_subcores
    tpw = t // n_workers
    token_block = min(tpw, SC_MAX_TOKEN_BLOCK)
    assert t % (n_workers * token_block) == 0
    units = token_block * SC_UNITS_PER_TOKEN
    mesh = plsc.VectorSubcoreMesh(core_axis_name="core", subcore_axis_name="subcore")
    hi_mask = jnp.int32(-65536)
    gelu_c = math.sqrt(2.0 / math.pi)

    @functools.partial(
        pl.kernel,
        out_type=jax.ShapeDtypeStruct((t, D_MODEL), F32),
        mesh=mesh,
        scratch_types=[
            pltpu.VMEM((token_block * PEER_SLOTS,), jnp.int32),
            pltpu.VMEM((token_block * PEER_SLOTS,), F32),
            pltpu.VMEM((token_block, HALF_D), jnp.int32),
            pltpu.VMEM((token_block, D_MODEL), F32),
            pltpu.VMEM((2, SC_UNIT_ROWS, SC_ROW_TILES, LANES), jnp.int32),
            pltpu.VMEM((SC_UNIT_ROWS, SC_LANES), F32),
            pltpu.VMEM((SC_UNIT_ROWS,), jnp.int32),
            pltpu.SemaphoreType.DMA((2,)),
        ],
        compiler_params=pltpu.CompilerParams(needs_layout_passes=False),
        name="peer_sc_experts",
    )
    def kern(table_hbm, idx_hbm, gate_hbm, h_hbm, out_hbm, idx_v, gate_v, h_v, out_v, rows_v, part_v, coef_v, sem):
        wid = lax.axis_index("subcore") * info.num_cores + lax.axis_index("core")
        lane = lax.iota(jnp.int32, SC_LANES)
        zero = jnp.zeros((SC_LANES,), F32)

        def gather(unit, b):
            rows = idx_v.at[pl.ds(unit * SC_UNIT_ROWS, SC_UNIT_ROWS)]
            return pltpu.make_async_copy(table_hbm.at[rows], rows_v.at[b], sem.at[b])

        def row_vec(b, r, vec):
            per_tile_row = LANES // SC_LANES
            lane0 = pl.multiple_of((vec % per_tile_row) * SC_LANES, SC_LANES)
            return rows_v[b, r, vec // per_tile_row, pl.ds(lane0, SC_LANES)]

        def unpack(w):
            return lax.bitcast_convert_type(w << 16, F32), lax.bitcast_convert_type(w & hi_mask, F32)

        def as_pairs(w):
            return plsc.bitcast(w, BF16)

        def compute(unit, b):
            tl = unit // SC_UNITS_PER_TOKEN
            q = unit % SC_UNITS_PER_TOKEN

            def row_group(rg, carry):
                def kstep(k, accs):
                    hs = []
                    for j in range(SC_BF16_TERMS):
                        off = pl.multiple_of((k * SC_BF16_TERMS + j) * SC_LANES, SC_LANES)
                        hs.append(as_pairs(h_v[tl, pl.ds(off, SC_LANES)]))
                    new = []
                    for r in range(SC_ROW_GROUP):
                        p = None
                        for j in range(SC_BF16_TERMS):
                            term = as_pairs(row_vec(b, rg * SC_ROW_GROUP + r, k * SC_BF16_TERMS + j)) * hs[j]
                            p = term if p is None else p + term
                        lo, hi = unpack(plsc.bitcast(p, jnp.int32))
                        new.append(accs[r] + lo + hi)
                    return tuple(new)

                accs = lax.fori_loop(0, SC_HALF_VECS // SC_BF16_TERMS, kstep, (zero,) * SC_ROW_GROUP)
                for r in range(SC_ROW_GROUP):
                    part_v[rg * SC_ROW_GROUP + r, :] = accs[r]
                return carry

            lax.fori_loop(0, SC_UNIT_ROWS // SC_ROW_GROUP, row_group, 0)

            for i in range(SC_UNIT_ROWS // SC_LANES):
                rows = lane + i * SC_LANES
                a = zero
                for l in range(SC_LANES):
                    a = a + plsc.load_gather(part_v, [rows, jnp.full((SC_LANES,), l, jnp.int32)])
                z = gelu_c * (a + 0.044715 * (a * a * a))
                act = a / (1.0 + jnp.exp(-2.0 * z))
                slot = pl.multiple_of(tl * PEER_SLOTS + q * SC_UNIT_ROWS + i * SC_LANES, SC_LANES)
                bits = lax.bitcast_convert_type(gate_v[pl.ds(slot, SC_LANES)] * act, jnp.int32)
                top = (bits + 0x7FFF + ((bits >> 16) & 1)) & hi_mask
                coef_v[pl.ds(i * SC_LANES, SC_LANES)] = top | lax.shift_right_logical(top, 16)

            def out_pass(dq, carry):
                def row_quad(rq, accs):
                    cs = []
                    for j in range(SC_BF16_TERMS):
                        row = jnp.full((SC_LANES,), rq * SC_BF16_TERMS + j, jnp.int32)
                        cs.append(as_pairs(plsc.load_gather(coef_v, [row])))
                    new = []
                    for k in range(SC_OUT_VECS):
                        p = None
                        for j in range(SC_BF16_TERMS):
                            w = row_vec(b, rq * SC_BF16_TERMS + j, SC_HALF_VECS + dq * SC_OUT_VECS + k)
                            term = as_pairs(w) * cs[j]
                            p = term if p is None else p + term
                        lo, hi = unpack(plsc.bitcast(p, jnp.int32))
                        new.append(accs[2 * k] + lo)
                        new.append(accs[2 * k + 1] + hi)
                    return tuple(new)

                accs = lax.fori_loop(0, SC_UNIT_ROWS // SC_BF16_TERMS, row_quad, (zero,) * (2 * SC_OUT_VECS))
                for k in range(SC_OUT_VECS):
                    off = pl.multiple_of((dq * SC_OUT_VECS + k) * SC_LANES, SC_LANES)
                    plsc.addupdate(out_v.at[tl, pl.ds(off, SC_LANES)], accs[2 * k])
                    plsc.addupdate(out_v.at[tl, pl.ds(HALF_D + off, SC_LANES)], accs[2 * k + 1])
                return carry

            lax.fori_loop(0, SC_HALF_VECS // SC_OUT_VECS, out_pass, 0)

        @pl.loop(0, tpw // token_block)
        def _(blk):
            tok0 = wid * tpw + blk * token_block
            slots = pl.ds(tok0 * PEER_SLOTS, token_block * PEER_SLOTS)
            pltpu.sync_copy(idx_hbm.at[slots], idx_v)
            pltpu.sync_copy(gate_hbm.at[slots], gate_v)
            pltpu.sync_copy(h_hbm.at[pl.ds(tok0, token_block)], h_v)

            @pl.loop(0, token_block)
            def _(tl):
                @pl.loop(0, D_MODEL // SC_LANES)
                def _(k):
                    out_v[tl, pl.ds(pl.multiple_of(k * SC_LANES, SC_LANES), SC_LANES)] = zero

            gather(0, 0).start()

            @pl.loop(0, units, step=2)
            def _(u0):
                for b in range(2):
                    unit = u0 + b

                    @pl.when(unit + 1 < units)
                    def _():
                        gather(unit + 1, 1 - b).start()

                    gather(unit, b).wait()
                    compute(unit, b)

            pltpu.sync_copy(out_v, out_hbm.at[pl.ds(tok0, token_block)])

    return kern(table, idx, gates, hp)


TC_PEER_TOKENS = 8
TC_PEER_BUFFERS = 4
U_TILE_ROWS = SC_ROW_TILES // 2


def _tc_peer_kernel(idx_ref, idx_next_ref, gate_ref, hp_ref, table_hbm, o_ref, rows_buf, a_buf, c_buf, sem):
    step = pl.program_id(0)
    hi_mask = jnp.int32(-65536)

    def unpack(w):
        return lax.bitcast_convert_type(w << 16, F32), lax.bitcast_convert_type(w & hi_mask, F32)

    def row_copy(ids_ref, tok, e, slot):
        return pltpu.make_async_copy(table_hbm.at[ids_ref[tok * PEER_SLOTS + e]], rows_buf.at[slot, e], sem.at[slot])

    def wait_rows(slot):
        pltpu.make_async_copy(table_hbm.at[pl.ds(0, PEER_SLOTS)], rows_buf.at[slot], sem.at[slot]).wait()

    ahead = TC_PEER_BUFFERS - 1

    @pl.when(step == 0)
    def _():
        for tok in range(ahead):
            def first(e, c, tok=tok):
                row_copy(idx_ref, tok, e, tok).start()
                return c

            lax.fori_loop(0, PEER_SLOTS, first, 0, unroll=8)

    gates_t = gate_ref[...].T
    zeros_v = jnp.zeros((SC_ROW_TILES - U_TILE_ROWS, LANES), F32)
    for t in range(TC_PEER_TOKENS):
        slot = t % TC_PEER_BUFFERS
        nxt = t + ahead
        nxt_slot = nxt % TC_PEER_BUFFERS
        wait_rows(slot)
        h_lo, h_hi = unpack(hp_ref[t])
        h_lo = jnp.concatenate([h_lo, zeros_v], axis=0)
        h_hi = jnp.concatenate([h_hi, zeros_v], axis=0)

        def u_body(e, c, slot=slot, nxt=nxt, nxt_slot=nxt_slot, h_lo=h_lo, h_hi=h_hi):
            if nxt >= TC_PEER_TOKENS:
                @pl.when(step + 1 < pl.num_programs(0))
                def _():
                    row_copy(idx_next_ref, nxt - TC_PEER_TOKENS, e, nxt_slot).start()
            else:
                row_copy(idx_ref, nxt, e, nxt_slot).start()
            lo, hi = unpack(rows_buf[slot, e])
            a_buf[pl.ds(e, 1), :] = jnp.sum(lo * h_lo + hi * h_hi, axis=0, keepdims=True)
            return c

        lax.fori_loop(0, PEER_SLOTS, u_body, 0, unroll=8)
        a = jnp.sum(a_buf[...], axis=1, keepdims=True)
        c_buf[...] = jnp.broadcast_to(gates_t[:, t:t + 1] * _gelu(a), (PEER_SLOTS, LANES))

        def v_body(e, acc, slot=slot):
            lo, hi = unpack(rows_buf[slot, e])
            c = c_buf[pl.ds(e, 1), :]
            return acc[0] + c * lo, acc[1] + c * hi

        zero = jnp.zeros((SC_ROW_TILES, LANES), F32)
        acc_lo, acc_hi = lax.fori_loop(0, PEER_SLOTS, v_body, (zero, zero), unroll=8)
        o_ref[t, 0:U_TILE_ROWS, :] = acc_lo[U_TILE_ROWS:, :]
        o_ref[t, U_TILE_ROWS:, :] = acc_hi[U_TILE_ROWS:, :]


def tc_peer_experts(table, idx, gates, hp):
    t = hp.shape[0]
    n_steps = t // TC_PEER_TOKENS
    ids = TC_PEER_TOKENS * PEER_SLOTS
    smem = functools.partial(pl.BlockSpec, memory_space=pltpu.SMEM)
    out = pl.pallas_call(
        _tc_peer_kernel,
        grid=(n_steps,),
        in_specs=[
            smem((ids,), lambda i: (i,)),
            smem((ids,), lambda i: (jnp.minimum(i + 1, n_steps - 1),)),
            pl.BlockSpec((TC_PEER_TOKENS, PEER_SLOTS), lambda i: (i, 0)),
            pl.BlockSpec((TC_PEER_TOKENS, U_TILE_ROWS, LANES), lambda i: (i, 0, 0)),
            pl.BlockSpec(memory_space=pl.ANY),
        ],
        out_specs=pl.BlockSpec((TC_PEER_TOKENS, SC_ROW_TILES, LANES), lambda i: (i, 0, 0)),
        out_shape=jax.ShapeDtypeStruct((t, SC_ROW_TILES, LANES), F32),
        scratch_shapes=[
            pltpu.VMEM((TC_PEER_BUFFERS, PEER_SLOTS, SC_ROW_TILES, LANES), jnp.int32),
            pltpu.VMEM((PEER_SLOTS, LANES), F32),
            pltpu.VMEM((PEER_SLOTS, LANES), F32),
            pltpu.SemaphoreType.DMA((TC_PEER_BUFFERS,)),
        ],
        compiler_params=pltpu.CompilerParams(dimension_semantics=("arbitrary",), vmem_limit_bytes=VMEM_LIMIT,
                                             disable_bounds_checks=True),
        name="tc_peer_experts",
    )(idx, idx, gates, hp.reshape(t, U_TILE_ROWS, LANES), table)
    return out.reshape(t, D_MODEL)


TC_PEER_SHARE = 8


def peer_block(x, norm_g, wq, subkeys, table):
    t = x.shape[0]
    hp, idx3, gate3 = peer_query(x, norm_g, wq, subkeys)
    idx = jnp.transpose(idx3, (0, 2, 1)).reshape(t * PEER_SLOTS)
    gates = jnp.transpose(gate3, (0, 2, 1)).reshape(t * PEER_SLOTS)
    info = plsc.get_sparse_core_info()
    sc_round = info.num_cores * info.num_subcores * SC_MAX_TOKEN_BLOCK
    t_tc = (t // TC_PEER_SHARE) // sc_round * sc_round
    if t_tc == 0:
        return sc_peer_experts(table, idx, gates, hp), hp
    t_sc = t - t_tc
    on_tc = tc_peer_experts(table, idx[t_sc * PEER_SLOTS:], gates[t_sc * PEER_SLOTS:].reshape(t_tc, PEER_SLOTS),
                            hp[t_sc:])
    on_sc = sc_peer_experts(table, idx[:t_sc * PEER_SLOTS], gates[:t_sc * PEER_SLOTS], hp[:t_sc])
    return jnp.concatenate([on_sc, on_tc], axis=0), hp


def _ple_kernel(x_ref, ffn_ref, p_ref, g_ref, wg_ref, wp_ref, gf_ref, o_ref, *, final):
    x = x_ref[...] + ffn_ref[...]
    gate = _sigmoid(jnp.dot(_rms(x, g_ref[...]).astype(BF16), wg_ref[...], preferred_element_type=F32))
    emb = jnp.dot(p_ref[...].astype(BF16), wp_ref[...], preferred_element_type=F32)
    y = x + gate * emb
    o_ref[...] = _rms(y, gf_ref[...]) if final else y


def ple_block(x, ffn, p, norm_g, w_gate, w_proj, norm_final, final):
    t = x.shape[0]
    tm = min(t, 512)
    row = lambda i: (i, 0)
    full = lambda i: (0, 0)
    return pl.pallas_call(
        functools.partial(_ple_kernel, final=final),
        grid=(t // tm,),
        in_specs=[
            pl.BlockSpec((tm, D_MODEL), row),
            pl.BlockSpec((tm, D_MODEL), row),
            pl.BlockSpec((tm, PLE_DIM), row),
            pl.BlockSpec((1, D_MODEL), full),
            pl.BlockSpec((D_MODEL, D_MODEL), full),
            pl.BlockSpec((PLE_DIM, D_MODEL), full),
            pl.BlockSpec((1, D_MODEL), full),
        ],
        out_specs=pl.BlockSpec((tm, D_MODEL), row),
        out_shape=jax.ShapeDtypeStruct((t, D_MODEL), F32),
        compiler_params=_cparams("parallel"),
        name="ple_block",
    )(x, ffn, p, norm_g.reshape(1, D_MODEL), w_gate, w_proj, norm_final.reshape(1, D_MODEL))


def _prompt_groups(batch):
    sizes = []
    while sum(sizes) < batch:
        nxt = 1 if len(sizes) < 2 else -(-sizes[-1] * 7 // 5)
        sizes.append(min(nxt, batch - sum(sizes)))
    return sizes
def _trunk_layer(x, ple, lw, batch, seq, pool_prefix, pool_start, cache, final, after=None):
    if after is not None:
        x, _ = lax.optimization_barrier((x, after))
    uv = norm_matmul(x, lw["norm_mix"], lw["w_uv"], "gelu", F32, 2 * BW)
    mid = norm_matmul(x, lw["norm_mix"], lw["w_mid"], "none", F32, 4 * BW)
    gates = norm_matmul(x, lw["norm_mix"], lw["w_gates"], "sigmoid", BF16, D_MODEL)
    ya, vn = gmlp_mix(uv, lw["gmlp_ln_g"], lw["gmlp_ln_b"], lw["gmlp_ws"], lw["gmlp_bs"], min(seq, GMLP_CHUNK))
    yb, pool_state = pool_mix(mid, pool_prefix, pool_start, lw["pool_w"], lw["pool_scale"], batch, seq)
    if cache is None:
        yc = attn_prompt(mid, lw["rel_bias"], batch, seq)
    else:
        yc = attn_sample(mid, cache[0], cache[1], lw["rel_bias"], batch, seq)
    x = mixer_out(x, ya, yb, yc, gates, lw["w_branch"], lw["w_out"])
    ffn, query_stage = peer_block(x, lw["norm_ffn"], lw["peer_wq"], lw["peer_subkeys"], lw["peer_table"])
    x = ple_block(x, ffn, ple, lw["norm_ple"], lw["ple_gate"], lw["ple_proj"], lw["norm_final"], final)
    return x, mid, pool_state, vn, query_stage


def kernel(x_prompt, x_sample, cache_attn_k, cache_attn_v, state_pool, p_prompt, p_sample, norm_mix, w_in, gmlp_ln_g, gmlp_ln_b, gmlp_ws, gmlp_bs, pool_w, pool_scale, attn_rel_bias, w_branch, w_out, norm_ffn, peer_wq, peer_subkeys, peer_u, peer_v, norm_ple, ple_gate, ple_proj, norm_final):
    bp, lp, _ = x_prompt.shape
    bs, ls, _ = x_sample.shape
    assert lp % BAND_PAST == 0 and lp % GMLP_CHUNK == 0 and ls <= CHUNK
    n_keep = min(BAND_PAST, lp)
    sizes = _prompt_groups(bp)
    starts = [sum(sizes[:g]) for g in range(len(sizes))]
    xg = [x_prompt[a:a + n].reshape(n * lp, D_MODEL) for a, n in zip(starts, sizes)]
    xs = x_sample.reshape(bs * ls, D_MODEL)
    outs = {k: [] for k in ("pk", "pv", "pps", "sk", "sv", "sps", "sgv")}
    after = None
    for i in range(DEPTH):
        w_in_b = w_in[i].astype(BF16)
        lw = dict(
            norm_mix=norm_mix[i],
            w_uv=w_in_b[:, :2 * BW],
            w_mid=w_in_b[:, 2 * BW:6 * BW],
            w_gates=w_in_b[:, 6 * BW:],
            gmlp_ln_g=gmlp_ln_g[i], gmlp_ln_b=gmlp_ln_b[i], gmlp_ws=gmlp_ws[i], gmlp_bs=gmlp_bs[i],
            pool_w=pool_w[i], pool_scale=pool_scale[i], rel_bias=attn_rel_bias[i],
            w_branch=w_branch[i].astype(BF16), w_out=w_out[i].astype(BF16),
            norm_ffn=norm_ffn[i], peer_wq=peer_wq[i].astype(BF16),
            peer_subkeys=peer_subkeys[i].reshape(2 * PEER_HEADS, PEER_NKEYS, PEER_HALF).astype(BF16),
            peer_table=pack_expert_tables(peer_u[i], peer_v[i]),
            norm_ple=norm_ple[i], ple_gate=ple_gate[i].astype(BF16), ple_proj=ple_proj[i].astype(BF16),
            norm_final=norm_final,
        )
        final = i == DEPTH - 1
        pk, pv, pps = [], [], []
        for g, (a, bg) in enumerate(zip(starts, sizes)):
            ple_g = p_prompt[i, a:a + bg].reshape(bg * lp, PLE_DIM)
            zero_prefix = jnp.zeros((bg, POOL_STATE, BW), F32)
            xg[g], mid_p, ps_p, _, after = _trunk_layer(xg[g], ple_g, lw, bg, lp, zero_prefix, 0, None, final, after)
            mid_p = mid_p.reshape(bg, lp, 4 * BW)
            pk.append(mid_p[:, lp - n_keep:, 2 * BW:3 * BW].reshape(bg, n_keep, HEADS, HEAD_DIM))
            pv.append(mid_p[:, lp - n_keep:, 3 * BW:].reshape(bg, n_keep, HEADS, HEAD_DIM))
            pps.append(ps_p)
        outs["pk"].append(jnp.concatenate(pk, axis=0))
        outs["pv"].append(jnp.concatenate(pv, axis=0))
        outs["pps"].append(jnp.concatenate(pps, axis=0))
        xs, mid_s, ps_s, vn_s, _ = _trunk_layer(xs, p_sample[i].reshape(bs * ls, PLE_DIM), lw, bs, ls, state_pool[i],
                                                PAST_LEN, (cache_attn_k[i], cache_attn_v[i]), final)
        mid_s = mid_s.reshape(bs, ls, 4 * BW)
        outs["sk"].append(mid_s[:, :, 2 * BW:3 * BW].reshape(bs, ls, HEADS, HEAD_DIM))
        outs["sv"].append(mid_s[:, :, 3 * BW:].reshape(bs, ls, HEADS, HEAD_DIM))
        outs["sps"].append(ps_s)
        outs["sgv"].append(vn_s.reshape(bs, ls, BW))
    st = lambda k: jnp.stack(outs[k])
    y_prompt = jnp.concatenate(xg, axis=0).reshape(bp, lp, D_MODEL)
    return (y_prompt, xs.reshape(bs, ls, D_MODEL), st("pk"), st("pv"), st("pps"),
            st("sk"), st("sv"), st("sps"), st("sgv"))
```

```python
import functools
import math

import jax
import jax.numpy as jnp
import numpy as np
from jax import lax
from jax.experimental import pallas as pl
from jax.experimental.pallas import tpu as pltpu
from jax.experimental.pallas import tpu_sc as plsc

F32 = jnp.float32
BF16 = jnp.bfloat16

D_MODEL = 1024
DEPTH = 2
CHUNK = 64
EPS = 1e-6
BW = D_MODEL // 2
GMLP_CHUNK = 128
GROUPS = 4
GDIM = BW // GROUPS
POOL_WINDOWS = (2, 4, 8, 16)
POOL_STATE = 15
POOL_PAD = 16
HEADS = 8
HEAD_DIM = BW // HEADS
BAND_CHUNKS = 8
BAND_PAST = BAND_CHUNKS * CHUNK
REL_CLIP = 128
PAST_LEN = 4096
PEER_HEADS = 8
PEER_NKEYS = 128
PEER_HALF = 128
PEER_TOPK = 16
PEER_SLOTS = PEER_HEADS * PEER_TOPK
PLE_DIM = 256

LANES = 128
VMEM_LIMIT = 56 * 1024 * 1024
NEG = -1e30


def _cparams(*sem):
    return pltpu.CompilerParams(dimension_semantics=sem, vmem_limit_bytes=VMEM_LIMIT)


def _rms(x, g):
    ms = jnp.mean(x * x, axis=-1, keepdims=True)
    return x * lax.rsqrt(ms + EPS) * g


def _gelu(x):
    c = math.sqrt(2.0 / math.pi)
    return 0.5 * x * (1.0 + jnp.tanh(c * (x + 0.044715 * (x * x * x))))


def _sigmoid(x):
    return 1.0 / (1.0 + jnp.exp(-x))


_ACTS = {"gelu": _gelu, "sigmoid": _sigmoid, "none": lambda z: z}


def _norm_matmul_kernel(x_ref, g_ref, w_ref, o_ref, h_ref, *, act):
    @pl.when(pl.program_id(1) == 0)
    def _():
        h_ref[...] = _rms(x_ref[...], g_ref[...]).astype(BF16)

    z = jnp.dot(h_ref[...], w_ref[...], preferred_element_type=F32)
    o_ref[...] = _ACTS[act](z).astype(o_ref.dtype)


def norm_matmul(x, g, w, act, out_dtype, tn):
    t, d = x.shape
    n = w.shape[1]
    tm = min(t, 512)
    return pl.pallas_call(
        functools.partial(_norm_matmul_kernel, act=act),
        grid=(t // tm, n // tn),
        in_specs=[
            pl.BlockSpec((tm, d), lambda i, j: (i, 0)),
            pl.BlockSpec((1, d), lambda i, j: (0, 0)),
            pl.BlockSpec((d, tn), lambda i, j: (0, j)),
        ],
        out_specs=pl.BlockSpec((tm, tn), lambda i, j: (i, j)),
        out_shape=jax.ShapeDtypeStruct((t, n), out_dtype),
        scratch_shapes=[pltpu.VMEM((tm, d), BF16)],
        compiler_params=_cparams("parallel", "arbitrary"),
        name="norm_matmul_" + act,
    )(x, g.reshape(1, d), w)


def _gmlp_kernel(uv_ref, lng_ref, lnb_ref, ws_ref, bst_ref, y_ref, vn_ref, *, lc):
    u = uv_ref[:, :BW]
    v = uv_ref[:, BW:]
    mu = jnp.mean(v, axis=-1, keepdims=True)
    vc = v - mu
    var = jnp.mean(vc * vc, axis=-1, keepdims=True)
    vn = vc * lax.rsqrt(var + EPS) * lng_ref[...] + lnb_ref[...]
    vn_ref[...] = vn
    row = lax.broadcasted_iota(jnp.int32, (lc, lc), 0) // CHUNK
    col = lax.broadcasted_iota(jnp.int32, (lc, lc), 1) // CHUNK
    causal = col <= row
    vnb = vn.astype(BF16)
    for g in range(GROUPS):
        w = jnp.where(causal, ws_ref[g], 0.0).astype(BF16)
        s = jnp.dot(w, vnb[:, g * GDIM:(g + 1) * GDIM], preferred_element_type=F32)
        s = s + bst_ref[:, g:g + 1]
        y_ref[:, g * GDIM:(g + 1) * GDIM] = (u[:, g * GDIM:(g + 1) * GDIM] * s).astype(y_ref.dtype)


def gmlp_mix(uv, ln_g, ln_b, ws, bs, lc):
    t = uv.shape[0]
    return pl.pallas_call(
        functools.partial(_gmlp_kernel, lc=lc),
        grid=(t // lc,),
        in_specs=[
            pl.BlockSpec((lc, 2 * BW), lambda i: (i, 0)),
            pl.BlockSpec((1, BW), lambda i: (0, 0)),
            pl.BlockSpec((1, BW), lambda i: (0, 0)),
            pl.BlockSpec((GROUPS, lc, lc), lambda i: (0, 0, 0)),
            pl.BlockSpec((lc, GROUPS), lambda i: (0, 0)),
        ],
        out_specs=[
            pl.BlockSpec((lc, BW), lambda i: (i, 0)),
            pl.BlockSpec((lc, BW), lambda i: (i, 0)),
        ],
        out_shape=[
            jax.ShapeDtypeStruct((t, BW), BF16),
            jax.ShapeDtypeStruct((t, BW), F32),
        ],
        compiler_params=_cparams("parallel"),
        name="gmlp_mix",
    )(uv, ln_g.reshape(1, BW), ln_b.reshape(1, BW), ws[:, :lc, :lc], bs[:, :lc].T)


def _pool_kernel(u_ref, pre_ref, w_ref, sc_ref, y_ref, st_ref, pad_ref, *, seq, start_pos):
    pad_ref[0:POOL_PAD, :] = pre_ref[...]
    pad_ref[POOL_PAD:, :] = u_ref[...]
    pos = lax.broadcasted_iota(jnp.int32, (seq, 1), 0) + start_pos
    for g, win in enumerate(POOL_WINDOWS):
        cols = slice(g * GDIM, (g + 1) * GDIM)
        tok = pad_ref[POOL_PAD:, cols]
        acc = tok
        for k in range(1, win):
            acc = acc + pad_ref[POOL_PAD - k:POOL_PAD - k + seq, cols]
        cnt = jnp.minimum(pos + 1, win).astype(F32)
        d = acc / cnt - tok
        y = jnp.dot(d.astype(BF16), w_ref[g], preferred_element_type=F32)
        y_ref[:, cols] = (y * sc_ref[:, cols]).astype(y_ref.dtype)
    st_ref[...] = pad_ref[seq + 1:seq + POOL_PAD, :]


def pool_mix(mid, prefix, start_pos, pool_w, pool_scale, batch, seq):
    mid3 = mid.reshape(batch, seq, 4 * BW)
    pre = jnp.concatenate([jnp.zeros((batch, 1, BW), F32), prefix], axis=1)
    y, st = pl.pallas_call(
        functools.partial(_pool_kernel, seq=seq, start_pos=start_pos),
        grid=(batch,),
        in_specs=[
            pl.BlockSpec((None, seq, BW), lambda b: (b, 0, 0)),
            pl.BlockSpec((None, POOL_PAD, BW), lambda b: (b, 0, 0)),
            pl.BlockSpec((GROUPS, GDIM, GDIM), lambda b: (0, 0, 0)),
            pl.BlockSpec((1, BW), lambda b: (0, 0)),
        ],
        out_specs=[
            pl.BlockSpec((None, seq, BW), lambda b: (b, 0, 0)),
            pl.BlockSpec((None, POOL_STATE, BW), lambda b: (b, 0, 0)),
        ],
        out_shape=[
            jax.ShapeDtypeStruct((batch, seq, BW), BF16),
            jax.ShapeDtypeStruct((batch, POOL_STATE, BW), F32),
        ],
        scratch_shapes=[pltpu.VMEM((seq + POOL_PAD, BW), F32)],
        compiler_params=_cparams("parallel"),
        name="pool_mix",
    )(mid3, pre, pool_w.astype(BF16), pool_scale.reshape(1, BW))
    return y.reshape(batch * seq, BW), st


def _attn_chunks(q_ref, kcat_ref, vcat_ref, bias_ref, o_ref, *, n_chunks, cq, band, first_block):
    scale = HEAD_DIM ** -0.5
    for h in range(HEADS):
        cols = slice(h * HEAD_DIM, (h + 1) * HEAD_DIM)
        bias = bias_ref[h]
        for i in range(n_chunks):
            q = (q_ref[i * cq:(i + 1) * cq, cols] * scale).astype(BF16)
            k = kcat_ref[i * cq:i * cq + band, cols]
            v = vcat_ref[i * cq:i * cq + band, cols]
            s = lax.dot_general(q, k, (((1,), (1,)), ((), ())), preferred_element_type=F32) + bias
            if first_block is not None:
                key = lax.broadcasted_iota(jnp.int32, (1, band), 1)
                s = jnp.where(key >= first_block * (BAND_PAST - i * cq), s, NEG)
            m = jnp.max(s, axis=-1, keepdims=True)
            p = jnp.exp(s - m)
            l = jnp.sum(p, axis=-1, keepdims=True)
            o = jnp.dot(p.astype(BF16), v, preferred_element_type=F32) / l
            o_ref[i * cq:(i + 1) * cq, cols] = o.astype(o_ref.dtype)


def _attn_prompt_kernel(q_ref, kp_ref, ko_ref, vp_ref, vo_ref, bias_ref, o_ref, kcat_ref, vcat_ref):
    kcat_ref[0:BAND_PAST, :] = kp_ref[...].astype(BF16)
    kcat_ref[BAND_PAST:, :] = ko_ref[...].astype(BF16)
    vcat_ref[0:BAND_PAST, :] = vp_ref[...].astype(BF16)
    vcat_ref[BAND_PAST:, :] = vo_ref[...].astype(BF16)
    _attn_chunks(q_ref, kcat_ref, vcat_ref, bias_ref, o_ref, n_chunks=BAND_CHUNKS, cq=CHUNK,
                 band=BAND_PAST + CHUNK, first_block=(pl.program_id(1) == 0).astype(jnp.int32))


def _rel_bias_tile(rel_bias, qpos, kpos):
    rel = np.clip(qpos[:, None] - kpos[None, :], -REL_CLIP, REL_CLIP) + REL_CLIP
    return rel_bias[:, rel]


def attn_prompt(mid, rel_bias, batch, seq):
    mid3 = mid.reshape(batch, seq, 4 * BW)
    blk = BAND_PAST
    bias = _rel_bias_tile(rel_bias, np.arange(CHUNK), np.arange(BAND_PAST + CHUNK) - BAND_PAST)
    prev = lambda b, j: jnp.maximum(j - 1, 0)
    y = pl.pallas_call(
        _attn_prompt_kernel,
        grid=(batch, seq // blk),
        in_specs=[
            pl.BlockSpec((None, blk, BW), lambda b, j: (b, j, 1)),
            pl.BlockSpec((None, blk, BW), lambda b, j: (b, prev(b, j), 2)),
            pl.BlockSpec((None, blk, BW), lambda b, j: (b, j, 2)),
            pl.BlockSpec((None, blk, BW), lambda b, j: (b, prev(b, j), 3)),
            pl.BlockSpec((None, blk, BW), lambda b, j: (b, j, 3)),
            pl.BlockSpec((HEADS, CHUNK, BAND_PAST + CHUNK), lambda b, j: (0, 0, 0)),
        ],
        out_specs=pl.BlockSpec((None, blk, BW), lambda b, j: (b, j, 0)),
        out_shape=jax.ShapeDtypeStruct((batch, seq, BW), BF16),
        scratch_shapes=[pltpu.VMEM((2 * blk, BW), BF16), pltpu.VMEM((2 * blk, BW), BF16)],
        compiler_params=_cparams("parallel", "parallel"),
        name="attn_prompt",
    )(mid3, mid3, mid3, mid3, mid3, bias)
    return y.reshape(batch * seq, BW)


def _attn_sample_kernel(q_ref, kc_ref, kn_ref, vc_ref, vn_ref, bias_ref, o_ref, kcat_ref, vcat_ref, *, n_cache):
    kcat_ref[0:n_cache, :] = kc_ref[...].astype(BF16)
    kcat_ref[n_cache:, :] = kn_ref[...].astype(BF16)
    vcat_ref[0:n_cache, :] = vc_ref[...].astype(BF16)
    vcat_ref[n_cache:, :] = vn_ref[...].astype(BF16)
    seq = q_ref.shape[0]
    _attn_chunks(q_ref, kcat_ref, vcat_ref, bias_ref, o_ref, n_chunks=1, cq=seq, band=n_cache + seq,
                 first_block=None)


def attn_sample(mid, cache_k, cache_v, rel_bias, batch, seq):
    n_cache = cache_k.shape[1]
    assert PAST_LEN >= n_cache
    mid3 = mid.reshape(batch, seq, 4 * BW)
    ck = cache_k.reshape(batch, n_cache, BW)
    cv = cache_v.reshape(batch, n_cache, BW)
    bias = _rel_bias_tile(rel_bias, PAST_LEN + np.arange(seq), PAST_LEN - n_cache + np.arange(n_cache + seq))
    y = pl.pallas_call(
        functools.partial(_attn_sample_kernel, n_cache=n_cache),
        grid=(batch,),
        in_specs=[
            pl.BlockSpec((None, seq, BW), lambda b: (b, 0, 1)),
            pl.BlockSpec((None, n_cache, BW), lambda b: (b, 0, 0)),
            pl.BlockSpec((None, seq, BW), lambda b: (b, 0, 2)),
            pl.BlockSpec((None, n_cache, BW), lambda b: (b, 0, 0)),
            pl.BlockSpec((None, seq, BW), lambda b: (b, 0, 3)),
            pl.BlockSpec((HEADS, seq, n_cache + seq), lambda b: (0, 0, 0)),
        ],
        out_specs=pl.BlockSpec((None, seq, BW), lambda b: (b, 0, 0)),
        out_shape=jax.ShapeDtypeStruct((batch, seq, BW), BF16),
        scratch_shapes=[pltpu.VMEM((n_cache + seq, BW), BF16), pltpu.VMEM((n_cache + seq, BW), BF16)],
        compiler_params=_cparams("parallel"),
        name="attn_sample",
    )(mid3, ck, mid3, cv, mid3, bias)
    return y.reshape(batch * seq, BW)


def _mixer_out_kernel(x_ref, ya_ref, yb_ref, yc_ref, gate_ref, wb_ref, wo_ref, o_ref):
    acc = None
    for n, y_ref in enumerate((ya_ref, yb_ref, yc_ref)):
        proj = jnp.dot(y_ref[...], wb_ref[n], preferred_element_type=F32)
        term = gate_ref[:, n * D_MODEL:(n + 1) * D_MODEL].astype(F32) * proj
        acc = term if acc is None else acc + term
    o_ref[...] = x_ref[...] + jnp.dot(acc.astype(BF16), wo_ref[...], preferred_element_type=F32)


def mixer_out(x, ya, yb, yc, gates, w_branch, w_out):
    t = x.shape[0]
    tm = min(t, 512)
    row = lambda i: (i, 0)
    return pl.pallas_call(
        _mixer_out_kernel,
        grid=(t // tm,),
        in_specs=[
            pl.BlockSpec((tm, D_MODEL), row),
            pl.BlockSpec((tm, BW), row),
            pl.BlockSpec((tm, BW), row),
            pl.BlockSpec((tm, BW), row),
            pl.BlockSpec((tm, 3 * D_MODEL), row),
            pl.BlockSpec((3, BW, D_MODEL), lambda i: (0, 0, 0)),
            pl.BlockSpec((D_MODEL, D_MODEL), lambda i: (0, 0)),
        ],
        out_specs=pl.BlockSpec((tm, D_MODEL), row),
        out_shape=jax.ShapeDtypeStruct((t, D_MODEL), F32),
        compiler_params=_cparams("parallel"),
        name="mixer_out",
    )(x, ya, yb, yc, gates, w_branch, w_out)


def _extract_top(s, payload, k):
    r = float(s.shape[0])
    rows = lax.broadcasted_iota(jnp.int32, s.shape, 0).astype(F32)
    vals, pays = [], []
    for _ in range(k):
        m = jnp.max(s, axis=0, keepdims=True)
        idx = jnp.min(jnp.where(s == m, rows, r), axis=0, keepdims=True)
        sel = rows == idx
        vals.append(m)
        pays.append(idx if payload is None else jnp.max(jnp.where(sel, payload, -1.0), axis=0, keepdims=True))
        s = jnp.where(sel, -jnp.inf, s)
    return jnp.concatenate(vals, axis=0), jnp.concatenate(pays, axis=0)


def _peer_query_kernel(x_ref, g_ref, wq_ref, sk_ref, hp_ref, idx_ref, gate_ref, q_ref, *, tm):
    hb = _rms(x_ref[...], g_ref[...]).astype(BF16)
    q_ref[...] = jnp.dot(hb, wq_ref[...], preferred_element_type=F32).astype(BF16)
    bits = lax.bitcast_convert_type(hb.astype(F32), jnp.int32)
    half = D_MODEL // 2
    hp_ref[...] = bits[:, half:] | lax.shift_right_logical(bits[:, :half], 16)

    def sub_block(sb, carry):
        tok = pl.ds(pl.multiple_of(sb * LANES, LANES), LANES)
        for hd in range(PEER_HEADS):
            sv, si = [], []
            for p in range(2):
                hp = hd * 2 + p
                q = q_ref[tok, hp * PEER_HALF:(hp + 1) * PEER_HALF]
                s = lax.dot_general(sk_ref[hp], q, (((1,), (1,)), ((), ())), preferred_element_type=F32)
                v, i = _extract_top(s, None, PEER_TOPK)
                sv.append(v)
                si.append(i)
            cand = jnp.concatenate([sv[0][a:a + 1] + sv[1] for a in range(PEER_TOPK)], axis=0)
            eid = jnp.concatenate([si[0][a:a + 1] * PEER_NKEYS + si[1] for a in range(PEER_TOPK)], axis=0)
            tv, te = _extract_top(cand, eid, PEER_TOPK)
            e = jnp.exp(tv - tv[0:1])
            gate = e / jnp.sum(e, axis=0, keepdims=True)
            idx_ref[sb, hd * PEER_TOPK:(hd + 1) * PEER_TOPK, :] = te.astype(jnp.int32)
            gate_ref[sb, hd * PEER_TOPK:(hd + 1) * PEER_TOPK, :] = gate
        return carry

    lax.fori_loop(0, tm // LANES, sub_block, 0)


def peer_query(x, norm_g, wq, subkeys):
    t = x.shape[0]
    tm = min(t, 512)
    nq = wq.shape[1]
    nsb = tm // LANES
    return pl.pallas_call(
        functools.partial(_peer_query_kernel, tm=tm),
        grid=(t // tm,),
        in_specs=[
            pl.BlockSpec((tm, D_MODEL), lambda i: (i, 0)),
            pl.BlockSpec((1, D_MODEL), lambda i: (0, 0)),
            pl.BlockSpec((D_MODEL, nq), lambda i: (0, 0)),
            pl.BlockSpec((2 * PEER_HEADS, PEER_NKEYS, PEER_HALF), lambda i: (0, 0, 0)),
        ],
        out_specs=[
            pl.BlockSpec((tm, D_MODEL // 2), lambda i: (i, 0)),
            pl.BlockSpec((nsb, PEER_SLOTS, LANES), lambda i: (i, 0, 0)),
            pl.BlockSpec((nsb, PEER_SLOTS, LANES), lambda i: (i, 0, 0)),
        ],
        out_shape=[
            jax.ShapeDtypeStruct((t, D_MODEL // 2), jnp.int32),
            jax.ShapeDtypeStruct((t // LANES, PEER_SLOTS, LANES), jnp.int32),
            jax.ShapeDtypeStruct((t // LANES, PEER_SLOTS, LANES), F32),
        ],
        scratch_shapes=[pltpu.VMEM((tm, nq), BF16)],
        compiler_params=_cparams("parallel"),
        name="peer_query",
    )(x, norm_g.reshape(1, D_MODEL), wq, subkeys)


HALF_D = D_MODEL // 2
SC_LANES = 16
SC_UNIT_ROWS = 32
SC_UNITS_PER_TOKEN = PEER_SLOTS // SC_UNIT_ROWS
SC_MAX_TOKEN_BLOCK = 32
SC_ROW_GROUP = 8
SC_HALF_VECS = HALF_D // SC_LANES
SC_OUT_VECS = 8
SC_BF16_TERMS = 4
SC_ROW_TILES = D_MODEL // LANES


def pack_expert_tables(peer_u, peer_v):
    def pack(x):
        b = lax.bitcast_convert_type(x.astype(BF16), jnp.uint16).astype(jnp.uint32)
        return (b[:, HALF_D:] << 16) | b[:, :HALF_D]

    rows = lax.bitcast_convert_type(jnp.concatenate([pack(peer_u), pack(peer_v)], axis=1), jnp.int32)
    return rows.reshape(rows.shape[0], SC_ROW_TILES, LANES)


def sc_peer_experts(table, idx, gates, hp):
    t = hp.shape[0]
    info = plsc.get_sparse_core_info()
    n_workers = info.num_cores * info.num_subcores
    tpw = t // n_workers
    token_block = min(tpw, SC_MAX_TOKEN_BLOCK)
    assert t % (n_workers * token_block) == 0
    units = token_block * SC_UNITS_PER_TOKEN
    mesh = plsc.VectorSubcoreMesh(core_axis_name="core", subcore_axis_name="subcore")
    hi_mask = jnp.int32(-65536)
    gelu_c = math.sqrt(2.0 / math.pi)

    @functools.partial(
        pl.kernel,
        out_type=jax.ShapeDtypeStruct((t, D_MODEL), F32),
        mesh=mesh,
        scratch_types=[
            pltpu.VMEM((token_block * PEER_SLOTS,), jnp.int32),
            pltpu.VMEM((token_block * PEER_SLOTS,), F32),
            pltpu.VMEM((token_block, HALF_D), jnp.int32),
            pltpu.VMEM((token_block, D_MODEL), F32),
            pltpu.VMEM((2, SC_UNIT_ROWS, SC_ROW_TILES, LANES), jnp.int32),
            pltpu.VMEM((SC_UNIT_ROWS, SC_LANES), F32),
            pltpu.VMEM((SC_UNIT_ROWS,), jnp.int32),
            pltpu.SemaphoreType.DMA((2,)),
        ],
        compiler_params=pltpu.CompilerParams(needs_layout_passes=False),
        name="peer_sc_experts",
    )
    def kern(table_hbm, idx_hbm, gate_hbm, h_hbm, out_hbm, idx_v, gate_v, h_v, out_v, rows_v, part_v, coef_v, sem):
        wid = lax.axis_index("subcore") * info.num_cores + lax.axis_index("core")
        lane = lax.iota(jnp.int32, SC_LANES)
        zero = jnp.zeros((SC_LANES,), F32)

        def gather(unit, b):
            rows = idx_v.at[pl.ds(unit * SC_UNIT_ROWS, SC_UNIT_ROWS)]
            return pltpu.make_async_copy(table_hbm.at[rows], rows_v.at[b], sem.at[b])

        def row_vec(b, r, vec):
            per_tile_row = LANES // SC_LANES
            lane0 = pl.multiple_of((vec % per_tile_row) * SC_LANES, SC_LANES)
            return rows_v[b, r, vec // per_tile_row, pl.ds(lane0, SC_LANES)]

        def unpack(w):
            return lax.bitcast_convert_type(w << 16, F32), lax.bitcast_convert_type(w & hi_mask, F32)

        def as_pairs(w):
            return plsc.bitcast(w, BF16)

        def compute(unit, b):
            tl = unit // SC_UNITS_PER_TOKEN
            q = unit % SC_UNITS_PER_TOKEN

            def row_group(rg, carry):
                def kstep(k, accs):
                    hs = []
                    for j in range(SC_BF16_TERMS):
                        off = pl.multiple_of((k * SC_BF16_TERMS + j) * SC_LANES, SC_LANES)
                        hs.append(as_pairs(h_v[tl, pl.ds(off, SC_LANES)]))
                    new = []
                    for r in range(SC_ROW_GROUP):
                        p = None
                        for j in range(SC_BF16_TERMS):
                            term = as_pairs(row_vec(b, rg * SC_ROW_GROUP + r, k * SC_BF16_TERMS + j)) * hs[j]
                            p = term if p is None else p + term
                        lo, hi = unpack(plsc.bitcast(p, jnp.int32))
                        new.append(accs[r] + lo + hi)
                    return tuple(new)

                accs = lax.fori_loop(0, SC_HALF_VECS // SC_BF16_TERMS, kstep, (zero,) * SC_ROW_GROUP)
                for r in range(SC_ROW_GROUP):
                    part_v[rg * SC_ROW_GROUP + r, :] = accs[r]
                return carry

            lax.fori_loop(0, SC_UNIT_ROWS // SC_ROW_GROUP, row_group, 0)

            for i in range(SC_UNIT_ROWS // SC_LANES):
                rows = lane + i * SC_LANES
                a = zero
                for l in range(SC_LANES):
                    a = a + plsc.load_gather(part_v, [rows, jnp.full((SC_LANES,), l, jnp.int32)])
                z = gelu_c * (a + 0.044715 * (a * a * a))
                act = a / (1.0 + jnp.exp(-2.0 * z))
                slot = pl.multiple_of(tl * PEER_SLOTS + q * SC_UNIT_ROWS + i * SC_LANES, SC_LANES)
                bits = lax.bitcast_convert_type(gate_v[pl.ds(slot, SC_LANES)] * act, jnp.int32)
                top = (bits + 0x7FFF + ((bits >> 16) & 1)) & hi_mask
                coef_v[pl.ds(i * SC_LANES, SC_LANES)] = top | lax.shift_right_logical(top, 16)

            def out_pass(dq, carry):
                def row_quad(rq, accs):
                    cs = []
                    for j in range(SC_BF16_TERMS):
                        row = jnp.full((SC_LANES,), rq * SC_BF16_TERMS + j, jnp.int32)
                        cs.append(as_pairs(plsc.load_gather(coef_v, [row])))
                    new = []
                    for k in range(SC_OUT_VECS):
                        p = None
                        for j in range(SC_BF16_TERMS):
                            w = row_vec(b, rq * SC_BF16_TERMS + j, SC_HALF_VECS + dq * SC_OUT_VECS + k)
                            term = as_pairs(w) * cs[j]
                            p = term if p is None else p + term
                        lo, hi = unpack(plsc.bitcast(p, jnp.int32))
                        new.append(accs[2 * k] + lo)
                        new.append(accs[2 * k + 1] + hi)
                    return tuple(new)

                accs = lax.fori_loop(0, SC_UNIT_ROWS // SC_BF16_TERMS, row_quad, (zero,) * (2 * SC_OUT_VECS))
                for k in range(SC_OUT_VECS):
                    off = pl.multiple_of((dq * SC_OUT_VECS + k) * SC_LANES, SC_LANES)
                    plsc.addupdate(out_v.at[tl, pl.ds(off, SC_LANES)], accs[2 * k])
                    plsc.addupdate(out_v.at[tl, pl.ds(HALF_D + off, SC_LANES)], accs[2 * k + 1])
                return carry

            lax.fori_loop(0, SC_HALF_VECS // SC_OUT_VECS, out_pass, 0)

        @pl.loop(0, tpw // token_block)
        def _(blk):
            tok0 = wid * tpw + blk * token_block
            slots = pl.ds(tok0 * PEER_SLOTS, token_block * PEER_SLOTS)
            pltpu.sync_copy(idx_hbm.at[slots], idx_v)
            pltpu.sync_copy(gate_hbm.at[slots], gate_v)
            pltpu.sync_copy(h_hbm.at[pl.ds(tok0, token_block)], h_v)

            @pl.loop(0, token_block)
            def _(tl):
                @pl.loop(0, D_MODEL // SC_LANES)
                def _(k):
                    out_v[tl, pl.ds(pl.multiple_of(k * SC_LANES, SC_LANES), SC_LANES)] = zero

            gather(0, 0).start()

            @pl.loop(0, units, step=2)
            def _(u0):
                for b in range(2):
                    unit = u0 + b

                    @pl.when(unit + 1 < units)
                    def _():
                        gather(unit + 1, 1 - b).start()

                    gather(unit, b).wait()
                    compute(unit, b)

            pltpu.sync_copy(out_v, out_hbm.at[pl.ds(tok0, token_block)])

    return kern(table, idx, gates, hp)


TC_PEER_TOKENS = 8
TC_PEER_BUFFERS = 4
U_TILE_ROWS = SC_ROW_TILES // 2


def _tc_peer_kernel(idx_ref, idx_next_ref, gate_ref, hp_ref, table_hbm, o_ref, rows_buf, a_buf, c_buf, sem):
    step = pl.program_id(0)
    hi_mask = jnp.int32(-65536)

    def unpack(w):
        return lax.bitcast_convert_type(w << 16, F32), lax.bitcast_convert_type(w & hi_mask, F32)

    def row_copy(ids_ref, tok, e, slot):
        return pltpu.make_async_copy(table_hbm.at[ids_ref[tok * PEER_SLOTS + e]], rows_buf.at[slot, e], sem.at[slot])

    def wait_rows(slot):
        pltpu.make_async_copy(table_hbm.at[pl.ds(0, PEER_SLOTS)], rows_buf.at[slot], sem.at[slot]).wait()

    ahead = TC_PEER_BUFFERS - 1

    @pl.when(step == 0)
    def _():
        for tok in range(ahead):
            def first(e, c, tok=tok):
                row_copy(idx_ref, tok, e, tok).start()
                return c

            lax.fori_loop(0, PEER_SLOTS, first, 0, unroll=8)

    gates_t = gate_ref[...].T
    zeros_v = jnp.zeros((SC_ROW_TILES - U_TILE_ROWS, LANES), F32)
    for t in range(TC_PEER_TOKENS):
        slot = t % TC_PEER_BUFFERS
        nxt = t + ahead
        nxt_slot = nxt % TC_PEER_BUFFERS
        wait_rows(slot)
        h_lo, h_hi = unpack(hp_ref[t])
        h_lo = jnp.concatenate([h_lo, zeros_v], axis=0)
        h_hi = jnp.concatenate([h_hi, zeros_v], axis=0)

        def u_body(e, c, slot=slot, nxt=nxt, nxt_slot=nxt_slot, h_lo=h_lo, h_hi=h_hi):
            if nxt >= TC_PEER_TOKENS:
                @pl.when(step + 1 < pl.num_programs(0))
                def _():
                    row_copy(idx_next_ref, nxt - TC_PEER_TOKENS, e, nxt_slot).start()
            else:
                row_copy(idx_ref, nxt, e, nxt_slot).start()
            lo, hi = unpack(rows_buf[slot, e])
            a_buf[pl.ds(e, 1), :] = jnp.sum(lo * h_lo + hi * h_hi, axis=0, keepdims=True)
            return c

        lax.fori_loop(0, PEER_SLOTS, u_body, 0, unroll=8)
        a = jnp.sum(a_buf[...], axis=1, keepdims=True)
        c_buf[...] = jnp.broadcast_to(gates_t[:, t:t + 1] * _gelu(a), (PEER_SLOTS, LANES))

        def v_body(e, acc, slot=slot):
            lo, hi = unpack(rows_buf[slot, e])
            c = c_buf[pl.ds(e, 1), :]
            return acc[0] + c * lo, acc[1] + c * hi

        zero = jnp.zeros((SC_ROW_TILES, LANES), F32)
        acc_lo, acc_hi = lax.fori_loop(0, PEER_SLOTS, v_body, (zero, zero), unroll=8)
        o_ref[t, 0:U_TILE_ROWS, :] = acc_lo[U_TILE_ROWS:, :]
        o_ref[t, U_TILE_ROWS:, :] = acc_hi[U_TILE_ROWS:, :]


def tc_peer_experts(table, idx, gates, hp):
    t = hp.shape[0]
    n_steps = t // TC_PEER_TOKENS
    ids = TC_PEER_TOKENS * PEER_SLOTS
    smem = functools.partial(pl.BlockSpec, memory_space=pltpu.SMEM)
    out = pl.pallas_call(
        _tc_peer_kernel,
        grid=(n_steps,),
        in_specs=[
            smem((ids,), lambda i: (i,)),
            smem((ids,), lambda i: (jnp.minimum(i + 1, n_steps - 1),)),
            pl.BlockSpec((TC_PEER_TOKENS, PEER_SLOTS), lambda i: (i, 0)),
            pl.BlockSpec((TC_PEER_TOKENS, U_TILE_ROWS, LANES), lambda i: (i, 0, 0)),
            pl.BlockSpec(memory_space=pl.ANY),
        ],
        out_specs=pl.BlockSpec((TC_PEER_TOKENS, SC_ROW_TILES, LANES), lambda i: (i, 0, 0)),
        out_shape=jax.ShapeDtypeStruct((t, SC_ROW_TILES, LANES), F32),
        scratch_shapes=[
            pltpu.VMEM((TC_PEER_BUFFERS, PEER_SLOTS, SC_ROW_TILES, LANES), jnp.int32),
            pltpu.VMEM((PEER_SLOTS, LANES), F32),
            pltpu.VMEM((PEER_SLOTS, LANES), F32),
            pltpu.SemaphoreType.DMA((TC_PEER_BUFFERS,)),
        ],
        compiler_params=pltpu.CompilerParams(dimension_semantics=("arbitrary",), vmem_limit_bytes=VMEM_LIMIT,
                                             disable_bounds_checks=True),
        name="tc_peer_experts",
    )(idx, idx, gates, hp.reshape(t, U_TILE_ROWS, LANES), table)
    return out.reshape(t, D_MODEL)


TC_PEER_SHARE = 8


def peer_block(x, norm_g, wq, subkeys, table, prev_sc=None):
    t = x.shape[0]
    hp, idx3, gate3 = peer_query(x, norm_g, wq, subkeys)
    idx = jnp.transpose(idx3, (0, 2, 1)).reshape(t * PEER_SLOTS)
    gates = jnp.transpose(gate3, (0, 2, 1)).reshape(t * PEER_SLOTS)
    info = plsc.get_sparse_core_info()
    sc_round = info.num_cores * info.num_subcores * SC_MAX_TOKEN_BLOCK
    t_tc = (t // TC_PEER_SHARE) // sc_round * sc_round
    t_sc = t - t_tc
    idx_sc = idx[:t_sc * PEER_SLOTS]
    if prev_sc is not None:
        idx_sc, _ = lax.optimization_barrier((idx_sc, prev_sc))
    on_sc = sc_peer_experts(table, idx_sc, gates[:t_sc * PEER_SLOTS], hp[:t_sc])
    if t_tc == 0:
        return on_sc, (hp,), on_sc
    on_tc = tc_peer_experts(table, idx[t_sc * PEER_SLOTS:], gates[t_sc * PEER_SLOTS:].reshape(t_tc, PEER_SLOTS),
                            hp[t_sc:])
    return jnp.concatenate([on_sc, on_tc], axis=0), (hp, on_tc), on_sc


def _ple_kernel(x_ref, ffn_ref, p_ref, g_ref, wg_ref, wp_ref, gf_ref, o_ref, *, final):
    x = x_ref[...] + ffn_ref[...]
    gate = _sigmoid(jnp.dot(_rms(x, g_ref[...]).astype(BF16), wg_ref[...], preferred_element_type=F32))
    emb = jnp.dot(p_ref[...].astype(BF16), wp_ref[...], preferred_element_type=F32)
    y = x + gate * emb
    o_ref[...] = _rms(y, gf_ref[...]) if final else y


def ple_block(x, ffn, p, norm_g, w_gate, w_proj, norm_final, final):
    t = x.shape[0]
    tm = min(t, 512)
    row = lambda i: (i, 0)
    full = lambda i: (0, 0)
    return pl.pallas_call(
        functools.partial(_ple_kernel, final=final),
        grid=(t // tm,),
        in_specs=[
            pl.BlockSpec((tm, D_MODEL), row),
            pl.BlockSpec((tm, D_MODEL), row),
            pl.BlockSpec((tm, PLE_DIM), row),
            pl.BlockSpec((1, D_MODEL), full),
            pl.BlockSpec((D_MODEL, D_MODEL), full),
            pl.BlockSpec((PLE_DIM, D_MODEL), full),
            pl.BlockSpec((1, D_MODEL), full),
        ],
        out_specs=pl.BlockSpec((tm, D_MODEL), row),
        out_shape=jax.ShapeDtypeStruct((t, D_MODEL), F32),
        compiler_params=_cparams("parallel"),
        name="ple_block",
    )(x, ffn, p, norm_g.reshape(1, D_MODEL), w_gate, w_proj, norm_final.reshape(1, D_MODEL))


def _prompt_groups(batch):
    sizes = []
    while sum(sizes) < batch:
        nxt = 1 if len(sizes) < 2 else -(-sizes[-1] * 7 // 5)
        sizes.append(min(nxt, batch - sum(sizes)))
    return sizes
def _trunk_layer(x, ple, lw, batch, seq, pool_prefix, pool_start, cache, final, after=None, prev_sc=None):
    if after is not None:
        x, _ = lax.optimization_barrier((x, after))
    uv = norm_matmul(x, lw["norm_mix"], lw["w_uv"], "gelu", F32, 2 * BW)
    mid = norm_matmul(x, lw["norm_mix"], lw["w_mid"], "none", F32, 4 * BW)
    gates = norm_matmul(x, lw["norm_mix"], lw["w_gates"], "sigmoid", BF16, D_MODEL)
    ya, vn = gmlp_mix(uv, lw["gmlp_ln_g"], lw["gmlp_ln_b"], lw["gmlp_ws"], lw["gmlp_bs"], min(seq, GMLP_CHUNK))
    yb, pool_state = pool_mix(mid, pool_prefix, pool_start, lw["pool_w"], lw["pool_scale"], batch, seq)
    if cache is None:
        yc = attn_prompt(mid, lw["rel_bias"], batch, seq)
    else:
        yc = attn_sample(mid, cache[0], cache[1], lw["rel_bias"], batch, seq)
    x = mixer_out(x, ya, yb, yc, gates, lw["w_branch"], lw["w_out"])
    ffn, stage, sc_out = peer_block(x, lw["norm_ffn"], lw["peer_wq"], lw["peer_subkeys"], lw["peer_table"], prev_sc)
    x = ple_block(x, ffn, ple, lw["norm_ple"], lw["ple_gate"], lw["ple_proj"], lw["norm_final"], final)
    return x, mid, pool_state, vn, stage, sc_out


def kernel(x_prompt, x_sample, cache_attn_k, cache_attn_v, state_pool, p_prompt, p_sample, norm_mix, w_in, gmlp_ln_g, gmlp_ln_b, gmlp_ws, gmlp_bs, pool_w, pool_scale, attn_rel_bias, w_branch, w_out, norm_ffn, peer_wq, peer_subkeys, peer_u, peer_v, norm_ple, ple_gate, ple_proj, norm_final):
    bp, lp, _ = x_prompt.shape
    bs, ls, _ = x_sample.shape
    assert lp % BAND_PAST == 0 and lp % GMLP_CHUNK == 0 and ls <= CHUNK
    n_keep = min(BAND_PAST, lp)
    sizes = _prompt_groups(bp)
    starts = [sum(sizes[:g]) for g in range(len(sizes))]
    xg = [x_prompt[a:a + n].reshape(n * lp, D_MODEL) for a, n in zip(starts, sizes)]
    xs = x_sample.reshape(bs * ls, D_MODEL)
    outs = {k: [] for k in ("pk", "pv", "pps", "sk", "sv", "sps", "sgv")}
    after = prev_sc = None
    for i in range(DEPTH):
        w_in_b = w_in[i].astype(BF16)
        lw = dict(
            norm_mix=norm_mix[i],
            w_uv=w_in_b[:, :2 * BW],
            w_mid=w_in_b[:, 2 * BW:6 * BW],
            w_gates=w_in_b[:, 6 * BW:],
            gmlp_ln_g=gmlp_ln_g[i], gmlp_ln_b=gmlp_ln_b[i], gmlp_ws=gmlp_ws[i], gmlp_bs=gmlp_bs[i],
            pool_w=pool_w[i], pool_scale=pool_scale[i], rel_bias=attn_rel_bias[i],
            w_branch=w_branch[i].astype(BF16), w_out=w_out[i].astype(BF16),
            norm_ffn=norm_ffn[i], peer_wq=peer_wq[i].astype(BF16),
            peer_subkeys=peer_subkeys[i].reshape(2 * PEER_HEADS, PEER_NKEYS, PEER_HALF).astype(BF16),
            peer_table=pack_expert_tables(peer_u[i], peer_v[i]),
            norm_ple=norm_ple[i], ple_gate=ple_gate[i].astype(BF16), ple_proj=ple_proj[i].astype(BF16),
            norm_final=norm_final,
        )
        final = i == DEPTH - 1
        pk, pv, pps = [], [], []
        for g, (a, bg) in enumerate(zip(starts, sizes)):
            ple_g = p_prompt[i, a:a + bg].reshape(bg * lp, PLE_DIM)
            zero_prefix = jnp.zeros((bg, POOL_STATE, BW), F32)
            xg[g], mid_p, ps_p, _, after, prev_sc = _trunk_layer(xg[g], ple_g, lw, bg, lp, zero_prefix, 0, None, final,
                                                                 after, prev_sc)
            mid_p = mid_p.reshape(bg, lp, 4 * BW)
            pk.append(mid_p[:, lp - n_keep:, 2 * BW:3 * BW].reshape(bg, n_keep, HEADS, HEAD_DIM))
            pv.append(mid_p[:, lp - n_keep:, 3 * BW:].reshape(bg, n_keep, HEADS, HEAD_DIM))
            pps.append(ps_p)
        outs["pk"].append(jnp.concatenate(pk, axis=0))
        outs["pv"].append(jnp.concatenate(pv, axis=0))
        outs["pps"].append(jnp.concatenate(pps, axis=0))
        xs, mid_s, ps_s, vn_s, _, _ = _trunk_layer(xs, p_sample[i].reshape(bs * ls, PLE_DIM), lw, bs, ls, state_pool[i],
                                                PAST_LEN, (cache_attn_k[i], cache_attn_v[i]), final)
        mid_s = mid_s.reshape(bs, ls, 4 * BW)
        outs["sk"].append(mid_s[:, :, 2 * BW:3 * BW].reshape(bs, ls, HEADS, HEAD_DIM))
        outs["sv"].append(mid_s[:, :, 3 * BW:].reshape(bs, ls, HEADS, HEAD_DIM))
        outs["sps"].append(ps_s)
        outs["sgv"].append(vn_s.reshape(bs, ls, BW))
    st = lambda k: jnp.stack(outs[k])
    y_prompt = jnp.concatenate(xg, axis=0).reshape(bp, lp, D_MODEL)
    return (y_prompt, xs.reshape(bs, ls, D_MODEL), st("pk"), st("pv"), st("pps"),
            st("sk"), st("sv"), st("sps"), st("sgv"))
```

```python
import functools
import math

import jax
import jax.numpy as jnp
import numpy as np
from jax import lax
from jax.experimental import pallas as pl
from jax.experimental.pallas import tpu as pltpu
from jax.experimental.pallas import tpu_sc as plsc

F32 = jnp.float32
BF16 = jnp.bfloat16

D_MODEL = 1024
DEPTH = 2
CHUNK = 64
EPS = 1e-6
BW = D_MODEL // 2
GMLP_CHUNK = 128
GROUPS = 4
GDIM = BW // GROUPS
POOL_WINDOWS = (2, 4, 8, 16)
POOL_STATE = 15
POOL_PAD = 16
HEADS = 8
HEAD_DIM = BW // HEADS
BAND_CHUNKS = 8
BAND_PAST = BAND_CHUNKS * CHUNK
REL_CLIP = 128
PAST_LEN = 4096
PEER_HEADS = 8
PEER_NKEYS = 128
PEER_HALF = 128
PEER_TOPK = 16
PEER_SLOTS = PEER_HEADS * PEER_TOPK
PLE_DIM = 256

LANES = 128
VMEM_LIMIT = 56 * 1024 * 1024
NEG = -1e30


def _cparams(*sem):
    return pltpu.CompilerParams(dimension_semantics=sem, vmem_limit_bytes=VMEM_LIMIT)


def _rms(x, g):
    ms = jnp.mean(x * x, axis=-1, keepdims=True)
    return x * lax.rsqrt(ms + EPS) * g


def _gelu(x):
    c = math.sqrt(2.0 / math.pi)
    return 0.5 * x * (1.0 + jnp.tanh(c * (x + 0.044715 * (x * x * x))))


def _sigmoid(x):
    return 1.0 / (1.0 + jnp.exp(-x))


_ACTS = {"gelu": _gelu, "sigmoid": _sigmoid, "none": lambda z: z}


def _norm_matmul_kernel(x_ref, g_ref, w_ref, o_ref, h_ref, *, act):
    @pl.when(pl.program_id(1) == 0)
    def _():
        h_ref[...] = _rms(x_ref[...], g_ref[...]).astype(BF16)

    z = jnp.dot(h_ref[...], w_ref[...], preferred_element_type=F32)
    o_ref[...] = _ACTS[act](z).astype(o_ref.dtype)


def norm_matmul(x, g, w, act, out_dtype, tn):
    t, d = x.shape
    n = w.shape[1]
    tm = min(t, 512)
    return pl.pallas_call(
        functools.partial(_norm_matmul_kernel, act=act),
        grid=(t // tm, n // tn),
        in_specs=[
            pl.BlockSpec((tm, d), lambda i, j: (i, 0)),
            pl.BlockSpec((1, d), lambda i, j: (0, 0)),
            pl.BlockSpec((d, tn), lambda i, j: (0, j)),
        ],
        out_specs=pl.BlockSpec((tm, tn), lambda i, j: (i, j)),
        out_shape=jax.ShapeDtypeStruct((t, n), out_dtype),
        scratch_shapes=[pltpu.VMEM((tm, d), BF16)],
        compiler_params=_cparams("parallel", "arbitrary"),
        name="norm_matmul_" + act,
    )(x, g.reshape(1, d), w)


def _gmlp_kernel(uv_ref, lng_ref, lnb_ref, ws_ref, bst_ref, y_ref, vn_ref, *, lc):
    u = uv_ref[:, :BW]
    v = uv_ref[:, BW:]
    mu = jnp.mean(v, axis=-1, keepdims=True)
    vc = v - mu
    var = jnp.mean(vc * vc, axis=-1, keepdims=True)
    vn = vc * lax.rsqrt(var + EPS) * lng_ref[...] + lnb_ref[...]
    vn_ref[...] = vn
    row = lax.broadcasted_iota(jnp.int32, (lc, lc), 0) // CHUNK
    col = lax.broadcasted_iota(jnp.int32, (lc, lc), 1) // CHUNK
    causal = col <= row
    vnb = vn.astype(BF16)
    for g in range(GROUPS):
        w = jnp.where(causal, ws_ref[g], 0.0).astype(BF16)
        s = jnp.dot(w, vnb[:, g * GDIM:(g + 1) * GDIM], preferred_element_type=F32)
        s = s + bst_ref[:, g:g + 1]
        y_ref[:, g * GDIM:(g + 1) * GDIM] = (u[:, g * GDIM:(g + 1) * GDIM] * s).astype(y_ref.dtype)


def gmlp_mix(uv, ln_g, ln_b, ws, bs, lc):
    t = uv.shape[0]
    return pl.pallas_call(
        functools.partial(_gmlp_kernel, lc=lc),
        grid=(t // lc,),
        in_specs=[
            pl.BlockSpec((lc, 2 * BW), lambda i: (i, 0)),
            pl.BlockSpec((1, BW), lambda i: (0, 0)),
            pl.BlockSpec((1, BW), lambda i: (0, 0)),
            pl.BlockSpec((GROUPS, lc, lc), lambda i: (0, 0, 0)),
            pl.BlockSpec((lc, GROUPS), lambda i: (0, 0)),
        ],
        out_specs=[
            pl.BlockSpec((lc, BW), lambda i: (i, 0)),
            pl.BlockSpec((lc, BW), lambda i: (i, 0)),
        ],
        out_shape=[
            jax.ShapeDtypeStruct((t, BW), BF16),
            jax.ShapeDtypeStruct((t, BW), F32),
        ],
        compiler_params=_cparams("parallel"),
        name="gmlp_mix",
    )(uv, ln_g.reshape(1, BW), ln_b.reshape(1, BW), ws[:, :lc, :lc], bs[:, :lc].T)


def _pool_kernel(u_ref, pre_ref, w_ref, sc_ref, y_ref, st_ref, pad_ref, *, seq, start_pos):
    pad_ref[0:POOL_PAD, :] = pre_ref[...]
    pad_ref[POOL_PAD:, :] = u_ref[...]
    pos = lax.broadcasted_iota(jnp.int32, (seq, 1), 0) + start_pos
    for g, win in enumerate(POOL_WINDOWS):
        cols = slice(g * GDIM, (g + 1) * GDIM)
        tok = pad_ref[POOL_PAD:, cols]
        acc = tok
        for k in range(1, win):
            acc = acc + pad_ref[POOL_PAD - k:POOL_PAD - k + seq, cols]
        cnt = jnp.minimum(pos + 1, win).astype(F32)
        d = acc / cnt - tok
        y = jnp.dot(d.astype(BF16), w_ref[g], preferred_element_type=F32)
        y_ref[:, cols] = (y * sc_ref[:, cols]).astype(y_ref.dtype)
    st_ref[...] = pad_ref[seq + 1:seq + POOL_PAD, :]


def pool_mix(mid, prefix, start_pos, pool_w, pool_scale, batch, seq):
    mid3 = mid.reshape(batch, seq, 4 * BW)
    pre = jnp.concatenate([jnp.zeros((batch, 1, BW), F32), prefix], axis=1)
    y, st = pl.pallas_call(
        functools.partial(_pool_kernel, seq=seq, start_pos=start_pos),
        grid=(batch,),
        in_specs=[
            pl.BlockSpec((None, seq, BW), lambda b: (b, 0, 0)),
            pl.BlockSpec((None, POOL_PAD, BW), lambda b: (b, 0, 0)),
            pl.BlockSpec((GROUPS, GDIM, GDIM), lambda b: (0, 0, 0)),
            pl.BlockSpec((1, BW), lambda b: (0, 0)),
        ],
        out_specs=[
            pl.BlockSpec((None, seq, BW), lambda b: (b, 0, 0)),
            pl.BlockSpec((None, POOL_STATE, BW), lambda b: (b, 0, 0)),
        ],
        out_shape=[
            jax.ShapeDtypeStruct((batch, seq, BW), BF16),
            jax.ShapeDtypeStruct((batch, POOL_STATE, BW), F32),
        ],
        scratch_shapes=[pltpu.VMEM((seq + POOL_PAD, BW), F32)],
        compiler_params=_cparams("parallel"),
        name="pool_mix",
    )(mid3, pre, pool_w.astype(BF16), pool_scale.reshape(1, BW))
    return y.reshape(batch * seq, BW), st


def _attn_chunks(q_ref, kcat_ref, vcat_ref, bias_ref, o_ref, *, n_chunks, cq, band, first_block):
    scale = HEAD_DIM ** -0.5
    for h in range(HEADS):
        cols = slice(h * HEAD_DIM, (h + 1) * HEAD_DIM)
        bias = bias_ref[h]
        for i in range(n_chunks):
            q = (q_ref[i * cq:(i + 1) * cq, cols] * scale).astype(BF16)
            k = kcat_ref[i * cq:i * cq + band, cols]
            v = vcat_ref[i * cq:i * cq + band, cols]
            s = lax.dot_general(q, k, (((1,), (1,)), ((), ())), preferred_element_type=F32) + bias
            if first_block is not None:
                key = lax.broadcasted_iota(jnp.int32, (1, band), 1)
                s = jnp.where(key >= first_block * (BAND_PAST - i * cq), s, NEG)
            m = jnp.max(s, axis=-1, keepdims=True)
            p = jnp.exp(s - m)
            l = jnp.sum(p, axis=-1, keepdims=True)
            o = jnp.dot(p.astype(BF16), v, preferred_element_type=F32) / l
            o_ref[i * cq:(i + 1) * cq, cols] = o.astype(o_ref.dtype)


def _attn_prompt_kernel(q_ref, kp_ref, ko_ref, vp_ref, vo_ref, bias_ref, o_ref, kcat_ref, vcat_ref):
    kcat_ref[0:BAND_PAST, :] = kp_ref[...].astype(BF16)
    kcat_ref[BAND_PAST:, :] = ko_ref[...].astype(BF16)
    vcat_ref[0:BAND_PAST, :] = vp_ref[...].astype(BF16)
    vcat_ref[BAND_PAST:, :] = vo_ref[...].astype(BF16)
    _attn_chunks(q_ref, kcat_ref, vcat_ref, bias_ref, o_ref, n_chunks=BAND_CHUNKS, cq=CHUNK,
                 band=BAND_PAST + CHUNK, first_block=(pl.program_id(1) == 0).astype(jnp.int32))


def _rel_bias_tile(rel_bias, qpos, kpos):
    rel = np.clip(qpos[:, None] - kpos[None, :], -REL_CLIP, REL_CLIP) + REL_CLIP
    return rel_bias[:, rel]


def attn_prompt(mid, rel_bias, batch, seq):
    mid3 = mid.reshape(batch, seq, 4 * BW)
    blk = BAND_PAST
    bias = _rel_bias_tile(rel_bias, np.arange(CHUNK), np.arange(BAND_PAST + CHUNK) - BAND_PAST)
    prev = lambda b, j: jnp.maximum(j - 1, 0)
    y = pl.pallas_call(
        _attn_prompt_kernel,
        grid=(batch, seq // blk),
        in_specs=[
            pl.BlockSpec((None, blk, BW), lambda b, j: (b, j, 1)),
            pl.BlockSpec((None, blk, BW), lambda b, j: (b, prev(b, j), 2)),
            pl.BlockSpec((None, blk, BW), lambda b, j: (b, j, 2)),
            pl.BlockSpec((None, blk, BW), lambda b, j: (b, prev(b, j), 3)),
            pl.BlockSpec((None, blk, BW), lambda b, j: (b, j, 3)),
            pl.BlockSpec((HEADS, CHUNK, BAND_PAST + CHUNK), lambda b, j: (0, 0, 0)),
        ],
        out_specs=pl.BlockSpec((None, blk, BW), lambda b, j: (b, j, 0)),
        out_shape=jax.ShapeDtypeStruct((batch, seq, BW), BF16),
        scratch_shapes=[pltpu.VMEM((2 * blk, BW), BF16), pltpu.VMEM((2 * blk, BW), BF16)],
        compiler_params=_cparams("parallel", "parallel"),
        name="attn_prompt",
    )(mid3, mid3, mid3, mid3, mid3, bias)
    return y.reshape(batch * seq, BW)


def _attn_sample_kernel(q_ref, kc_ref, kn_ref, vc_ref, vn_ref, bias_ref, o_ref, kcat_ref, vcat_ref, *, n_cache):
    kcat_ref[0:n_cache, :] = kc_ref[...].astype(BF16)
    kcat_ref[n_cache:, :] = kn_ref[...].astype(BF16)
    vcat_ref[0:n_cache, :] = vc_ref[...].astype(BF16)
    vcat_ref[n_cache:, :] = vn_ref[...].astype(BF16)
    seq = q_ref.shape[0]
    _attn_chunks(q_ref, kcat_ref, vcat_ref, bias_ref, o_ref, n_chunks=1, cq=seq, band=n_cache + seq,
                 first_block=None)


def attn_sample(mid, cache_k, cache_v, rel_bias, batch, seq):
    n_cache = cache_k.shape[1]
    assert PAST_LEN >= n_cache
    mid3 = mid.reshape(batch, seq, 4 * BW)
    ck = cache_k.reshape(batch, n_cache, BW)
    cv = cache_v.reshape(batch, n_cache, BW)
    bias = _rel_bias_tile(rel_bias, PAST_LEN + np.arange(seq), PAST_LEN - n_cache + np.arange(n_cache + seq))
    y = pl.pallas_call(
        functools.partial(_attn_sample_kernel, n_cache=n_cache),
        grid=(batch,),
        in_specs=[
            pl.BlockSpec((None, seq, BW), lambda b: (b, 0, 1)),
            pl.BlockSpec((None, n_cache, BW), lambda b: (b, 0, 0)),
            pl.BlockSpec((None, seq, BW), lambda b: (b, 0, 2)),
            pl.BlockSpec((None, n_cache, BW), lambda b: (b, 0, 0)),
            pl.BlockSpec((None, seq, BW), lambda b: (b, 0, 3)),
            pl.BlockSpec((HEADS, seq, n_cache + seq), lambda b: (0, 0, 0)),
        ],
        out_specs=pl.BlockSpec((None, seq, BW), lambda b: (b, 0, 0)),
        out_shape=jax.ShapeDtypeStruct((batch, seq, BW), BF16),
        scratch_shapes=[pltpu.VMEM((n_cache + seq, BW), BF16), pltpu.VMEM((n_cache + seq, BW), BF16)],
        compiler_params=_cparams("parallel"),
        name="attn_sample",
    )(mid3, ck, mid3, cv, mid3, bias)
    return y.reshape(batch * seq, BW)


def _mixer_out_kernel(x_ref, ya_ref, yb_ref, yc_ref, gate_ref, wb_ref, wo_ref, o_ref):
    acc = None
    for n, y_ref in enumerate((ya_ref, yb_ref, yc_ref)):
        proj = jnp.dot(y_ref[...], wb_ref[n], preferred_element_type=F32)
        term = gate_ref[:, n * D_MODEL:(n + 1) * D_MODEL].astype(F32) * proj
        acc = term if acc is None else acc + term
    o_ref[...] = x_ref[...] + jnp.dot(acc.astype(BF16), wo_ref[...], preferred_element_type=F32)


def mixer_out(x, ya, yb, yc, gates, w_branch, w_out):
    t = x.shape[0]
    tm = min(t, 512)
    row = lambda i: (i, 0)
    return pl.pallas_call(
        _mixer_out_kernel,
        grid=(t // tm,),
        in_specs=[
            pl.BlockSpec((tm, D_MODEL), row),
            pl.BlockSpec((tm, BW), row),
            pl.BlockSpec((tm, BW), row),
            pl.BlockSpec((tm, BW), row),
            pl.BlockSpec((tm, 3 * D_MODEL), row),
            pl.BlockSpec((3, BW, D_MODEL), lambda i: (0, 0, 0)),
            pl.BlockSpec((D_MODEL, D_MODEL), lambda i: (0, 0)),
        ],
        out_specs=pl.BlockSpec((tm, D_MODEL), row),
        out_shape=jax.ShapeDtypeStruct((t, D_MODEL), F32),
        compiler_params=_cparams("parallel"),
        name="mixer_out",
    )(x, ya, yb, yc, gates, w_branch, w_out)


def _extract_top(s, payload, k):
    r = float(s.shape[0])
    rows = lax.broadcasted_iota(jnp.int32, s.shape, 0).astype(F32)
    vals, pays = [], []
    for _ in range(k):
        m = jnp.max(s, axis=0, keepdims=True)
        idx = jnp.min(jnp.where(s == m, rows, r), axis=0, keepdims=True)
        sel = rows == idx
        vals.append(m)
        pays.append(idx if payload is None else jnp.max(jnp.where(sel, payload, -1.0), axis=0, keepdims=True))
        s = jnp.where(sel, -jnp.inf, s)
    return jnp.concatenate(vals, axis=0), jnp.concatenate(pays, axis=0)


def _pair_candidates(sv, si):
    k = PEER_TOPK
    sub = 8
    assert k == 2 * sub
    b_row = lax.broadcasted_iota(jnp.int32, (sub, LANES), 0)
    vals = [sv[0][0:1] + sv[1], sv[0][1:2] + sv[1][0:sub]]
    ids = [si[0][0:1] * PEER_NKEYS + si[1], si[0][1:2] * PEER_NKEYS + si[1][0:sub]]
    for a in range(2, sub):
        keep = b_row < k // (a + 1)
        vals.append(jnp.where(keep, sv[0][a:a + 1] + sv[1][0:sub], -jnp.inf))
        ids.append(si[0][a:a + 1] * PEER_NKEYS + si[1][0:sub])
    vals.append(sv[0][sub:k] + sv[1][0:1])
    ids.append(si[0][sub:k] * PEER_NKEYS + si[1][0:1])
    return jnp.concatenate(vals, axis=0), jnp.concatenate(ids, axis=0)


def _peer_query_kernel(x_ref, g_ref, wq_ref, sk_ref, hp_ref, idx_ref, gate_ref, q_ref, *, tm):
    hb = _rms(x_ref[...], g_ref[...]).astype(BF16)
    q_ref[...] = jnp.dot(hb, wq_ref[...], preferred_element_type=F32).astype(BF16)
    bits = lax.bitcast_convert_type(hb.astype(F32), jnp.int32)
    half = D_MODEL // 2
    hp_ref[...] = bits[:, half:] | lax.shift_right_logical(bits[:, :half], 16)

    def sub_block(sb, carry):
        tok = pl.ds(pl.multiple_of(sb * LANES, LANES), LANES)
        for hd in range(PEER_HEADS):
            sv, si = [], []
            for p in range(2):
                hp = hd * 2 + p
                q = q_ref[tok, hp * PEER_HALF:(hp + 1) * PEER_HALF]
                s = lax.dot_general(sk_ref[hp], q, (((1,), (1,)), ((), ())), preferred_element_type=F32)
                v, i = _extract_top(s, None, PEER_TOPK)
                sv.append(v)
                si.append(i)
            cand, eid = _pair_candidates(sv, si)
            tv, te = _extract_top(cand, eid, PEER_TOPK)
            e = jnp.exp(tv - tv[0:1])
            gate = e / jnp.sum(e, axis=0, keepdims=True)
            idx_ref[sb, hd * PEER_TOPK:(hd + 1) * PEER_TOPK, :] = te.astype(jnp.int32)
            gate_ref[sb, hd * PEER_TOPK:(hd + 1) * PEER_TOPK, :] = gate
        return carry

    lax.fori_loop(0, tm // LANES, sub_block, 0)


def peer_query(x, norm_g, wq, subkeys):
    t = x.shape[0]
    tm = min(t, 512)
    nq = wq.shape[1]
    nsb = tm // LANES
    return pl.pallas_call(
        functools.partial(_peer_query_kernel, tm=tm),
        grid=(t // tm,),
        in_specs=[
            pl.BlockSpec((tm, D_MODEL), lambda i: (i, 0)),
            pl.BlockSpec((1, D_MODEL), lambda i: (0, 0)),
            pl.BlockSpec((D_MODEL, nq), lambda i: (0, 0)),
            pl.BlockSpec((2 * PEER_HEADS, PEER_NKEYS, PEER_HALF), lambda i: (0, 0, 0)),
        ],
        out_specs=[
            pl.BlockSpec((tm, D_MODEL // 2), lambda i: (i, 0)),
            pl.BlockSpec((nsb, PEER_SLOTS, LANES), lambda i: (i, 0, 0)),
            pl.BlockSpec((nsb, PEER_SLOTS, LANES), lambda i: (i, 0, 0)),
        ],
        out_shape=[
            jax.ShapeDtypeStruct((t, D_MODEL // 2), jnp.int32),
            jax.ShapeDtypeStruct((t // LANES, PEER_SLOTS, LANES), jnp.int32),
            jax.ShapeDtypeStruct((t // LANES, PEER_SLOTS, LANES), F32),
        ],
        scratch_shapes=[pltpu.VMEM((tm, nq), BF16)],
        compiler_params=_cparams("parallel"),
        name="peer_query",
    )(x, norm_g.reshape(1, D_MODEL), wq, subkeys)


HALF_D = D_MODEL // 2
SC_LANES = 16
SC_UNIT_ROWS = 32
SC_UNITS_PER_TOKEN = PEER_SLOTS // SC_UNIT_ROWS
SC_MAX_TOKEN_BLOCK = 32
SC_ROW_GROUP = 8
SC_HALF_VECS = HALF_D // SC_LANES
SC_OUT_VECS = 8
SC_BF16_TERMS = 4
SC_ROW_TILES = D_MODEL // LANES


def pack_expert_tables(peer_u, peer_v):
    def pack(x):
        b = lax.bitcast_convert_type(x.astype(BF16), jnp.uint16).astype(jnp.uint32)
        return (b[:, HALF_D:] << 16) | b[:, :HALF_D]

    rows = lax.bitcast_convert_type(jnp.concatenate([pack(peer_u), pack(peer_v)], axis=1), jnp.int32)
    return rows.reshape(rows.shape[0], SC_ROW_TILES, LANES)


def sc_peer_experts(table, idx, gates, hp):
    t = hp.shape[0]
    info = plsc.get_sparse_core_info()
    n_workers = info.num_cores * info.num_subcores
    tpw = t // n_workers
    token_block = min(tpw, SC_MAX_TOKEN_BLOCK)
    assert t % (n_workers * token_block) == 0
    units = token_block * SC_UNITS_PER_TOKEN
    mesh = plsc.VectorSubcoreMesh(core_axis_name="core", subcore_axis_name="subcore")
    hi_mask = jnp.int32(-65536)
    gelu_c = math.sqrt(2.0 / math.pi)

    @functools.partial(
        pl.kernel,
        out_type=jax.ShapeDtypeStruct((t, D_MODEL), F32),
        mesh=mesh,
        scratch_types=[
            pltpu.VMEM((token_block * PEER_SLOTS,), jnp.int32),
            pltpu.VMEM((token_block * PEER_SLOTS,), F32),
            pltpu.VMEM((token_block, HALF_D), jnp.int32),
            pltpu.VMEM((token_block, D_MODEL), F32),
            pltpu.VMEM((2, SC_UNIT_ROWS, SC_ROW_TILES, LANES), jnp.int32),
            pltpu.VMEM((SC_UNIT_ROWS, SC_LANES), F32),
            pltpu.VMEM((SC_UNIT_ROWS,), jnp.int32),
            pltpu.SemaphoreType.DMA((2,)),
        ],
        compiler_params=pltpu.CompilerParams(needs_layout_passes=False),
        name="peer_sc_experts",
    )
    def kern(table_hbm, idx_hbm, gate_hbm, h_hbm, out_hbm, idx_v, gate_v, h_v, out_v, rows_v, part_v, coef_v, sem):
        wid = lax.axis_index("subcore") * info.num_cores + lax.axis_index("core")
        lane = lax.iota(jnp.int32, SC_LANES)
        zero = jnp.zeros((SC_LANES,), F32)

        def gather(unit, b):
            rows = idx_v.at[pl.ds(unit * SC_UNIT_ROWS, SC_UNIT_ROWS)]
            return pltpu.make_async_copy(table_hbm.at[rows], rows_v.at[b], sem.at[b])

        def row_vec(b, r, vec):
            per_tile_row = LANES // SC_LANES
            lane0 = pl.multiple_of((vec % per_tile_row) * SC_LANES, SC_LANES)
            return rows_v[b, r, vec // per_tile_row, pl.ds(lane0, SC_LANES)]

        def unpack(w):
            return lax.bitcast_convert_type(w << 16, F32), lax.bitcast_convert_type(w & hi_mask, F32)

        def as_pairs(w):
            return plsc.bitcast(w, BF16)

        def compute(unit, b):
            tl = unit // SC_UNITS_PER_TOKEN
            q = unit % SC_UNITS_PER_TOKEN

            def row_group(rg, carry):
                def kstep(k, accs):
                    hs = []
                    for j in range(SC_BF16_TERMS):
                        off = pl.multiple_of((k * SC_BF16_TERMS + j) * SC_LANES, SC_LANES)
                        hs.append(as_pairs(h_v[tl, pl.ds(off, SC_LANES)]))
                    new = []
                    for r in range(SC_ROW_GROUP):
                        p = None
                        for j in range(SC_BF16_TERMS):
                            term = as_pairs(row_vec(b, rg * SC_ROW_GROUP + r, k * SC_BF16_TERMS + j)) * hs[j]
                            p = term if p is None else p + term
                        lo, hi = unpack(plsc.bitcast(p, jnp.int32))
                        new.append(accs[r] + lo + hi)
                    return tuple(new)

                accs = lax.fori_loop(0, SC_HALF_VECS // SC_BF16_TERMS, kstep, (zero,) * SC_ROW_GROUP)
                for r in range(SC_ROW_GROUP):
                    part_v[rg * SC_ROW_GROUP + r, :] = accs[r]
                return carry

            lax.fori_loop(0, SC_UNIT_ROWS // SC_ROW_GROUP, row_group, 0)

            for i in range(SC_UNIT_ROWS // SC_LANES):
                rows = lane + i * SC_LANES
                a = zero
                for l in range(SC_LANES):
                    a = a + plsc.load_gather(part_v, [rows, jnp.full((SC_LANES,), l, jnp.int32)])
                z = gelu_c * (a + 0.044715 * (a * a * a))
                act = a / (1.0 + jnp.exp(-2.0 * z))
                slot = pl.multiple_of(tl * PEER_SLOTS + q * SC_UNIT_ROWS + i * SC_LANES, SC_LANES)
                bits = lax.bitcast_convert_type(gate_v[pl.ds(slot, SC_LANES)] * act, jnp.int32)
                top = (bits + 0x7FFF + ((bits >> 16) & 1)) & hi_mask
                coef_v[pl.ds(i * SC_LANES, SC_LANES)] = top | lax.shift_right_logical(top, 16)

            def out_pass(dq, carry):
                def row_quad(rq, accs):
                    cs = []
                    for j in range(SC_BF16_TERMS):
                        row = jnp.full((SC_LANES,), rq * SC_BF16_TERMS + j, jnp.int32)
                        cs.append(as_pairs(plsc.load_gather(coef_v, [row])))
                    new = []
                    for k in range(SC_OUT_VECS):
                        p = None
                        for j in range(SC_BF16_TERMS):
                            w = row_vec(b, rq * SC_BF16_TERMS + j, SC_HALF_VECS + dq * SC_OUT_VECS + k)
                            term = as_pairs(w) * cs[j]
                            p = term if p is None else p + term
                        lo, hi = unpack(plsc.bitcast(p, jnp.int32))
                        new.append(accs[2 * k] + lo)
                        new.append(accs[2 * k + 1] + hi)
                    return tuple(new)

                accs = lax.fori_loop(0, SC_UNIT_ROWS // SC_BF16_TERMS, row_quad, (zero,) * (2 * SC_OUT_VECS))
                for k in range(SC_OUT_VECS):
                    off = pl.multiple_of((dq * SC_OUT_VECS + k) * SC_LANES, SC_LANES)
                    plsc.addupdate(out_v.at[tl, pl.ds(off, SC_LANES)], accs[2 * k])
                    plsc.addupdate(out_v.at[tl, pl.ds(HALF_D + off, SC_LANES)], accs[2 * k + 1])
                return carry

            lax.fori_loop(0, SC_HALF_VECS // SC_OUT_VECS, out_pass, 0)

        @pl.loop(0, tpw // token_block)
        def _(blk):
            tok0 = wid * tpw + blk * token_block
            slots = pl.ds(tok0 * PEER_SLOTS, token_block * PEER_SLOTS)
            pltpu.sync_copy(idx_hbm.at[slots], idx_v)
            pltpu.sync_copy(gate_hbm.at[slots], gate_v)
            pltpu.sync_copy(h_hbm.at[pl.ds(tok0, token_block)], h_v)

            @pl.loop(0, token_block)
            def _(tl):
                @pl.loop(0, D_MODEL // SC_LANES)
                def _(k):
                    out_v[tl, pl.ds(pl.multiple_of(k * SC_LANES, SC_LANES), SC_LANES)] = zero

            gather(0, 0).start()

            @pl.loop(0, units, step=2)
            def _(u0):
                for b in range(2):
                    unit = u0 + b

                    @pl.when(unit + 1 < units)
                    def _():
                        gather(unit + 1, 1 - b).start()

                    gather(unit, b).wait()
                    compute(unit, b)

            pltpu.sync_copy(out_v, out_hbm.at[pl.ds(tok0, token_block)])

    return kern(table, idx, gates, hp)


TC_PEER_TOKENS = 8
TC_PEER_BUFFERS = 4
U_TILE_ROWS = SC_ROW_TILES // 2


def _tc_peer_kernel(idx_ref, idx_next_ref, gate_ref, hp_ref, table_hbm, o_ref, rows_buf, a_buf, c_buf, sem):
    step = pl.program_id(0)
    hi_mask = jnp.int32(-65536)

    def unpack(w):
        return lax.bitcast_convert_type(w << 16, F32), lax.bitcast_convert_type(w & hi_mask, F32)

    def row_copy(ids_ref, tok, e, slot):
        return pltpu.make_async_copy(table_hbm.at[ids_ref[tok * PEER_SLOTS + e]], rows_buf.at[slot, e], sem.at[slot])

    def wait_rows(slot):
        pltpu.make_async_copy(table_hbm.at[pl.ds(0, PEER_SLOTS)], rows_buf.at[slot], sem.at[slot]).wait()

    ahead = TC_PEER_BUFFERS - 1

    @pl.when(step == 0)
    def _():
        for tok in range(ahead):
            def first(e, c, tok=tok):
                row_copy(idx_ref, tok, e, tok).start()
                return c

            lax.fori_loop(0, PEER_SLOTS, first, 0, unroll=8)

    gates_t = gate_ref[...].T
    zeros_v = jnp.zeros((SC_ROW_TILES - U_TILE_ROWS, LANES), F32)
    for t in range(TC_PEER_TOKENS):
        slot = t % TC_PEER_BUFFERS
        nxt = t + ahead
        nxt_slot = nxt % TC_PEER_BUFFERS
        wait_rows(slot)
        h_lo, h_hi = unpack(hp_ref[t])
        h_lo = jnp.concatenate([h_lo, zeros_v], axis=0)
        h_hi = jnp.concatenate([h_hi, zeros_v], axis=0)

        def u_body(e, c, slot=slot, nxt=nxt, nxt_slot=nxt_slot, h_lo=h_lo, h_hi=h_hi):
            if nxt >= TC_PEER_TOKENS:
                @pl.when(step + 1 < pl.num_programs(0))
                def _():
                    row_copy(idx_next_ref, nxt - TC_PEER_TOKENS, e, nxt_slot).start()
            else:
                row_copy(idx_ref, nxt, e, nxt_slot).start()
            lo, hi = unpack(rows_buf[slot, e])
            a_buf[pl.ds(e, 1), :] = jnp.sum(lo * h_lo + hi * h_hi, axis=0, keepdims=True)
            return c

        lax.fori_loop(0, PEER_SLOTS, u_body, 0, unroll=8)
        a = jnp.sum(a_buf[...], axis=1, keepdims=True)
        c_buf[...] = jnp.broadcast_to(gates_t[:, t:t + 1] * _gelu(a), (PEER_SLOTS, LANES))

        def v_body(e, acc, slot=slot):
            lo, hi = unpack(rows_buf[slot, e])
            c = c_buf[pl.ds(e, 1), :]
            return acc[0] + c * lo, acc[1] + c * hi

        zero = jnp.zeros((SC_ROW_TILES, LANES), F32)
        acc_lo, acc_hi = lax.fori_loop(0, PEER_SLOTS, v_body, (zero, zero), unroll=8)
        o_ref[t, 0:U_TILE_ROWS, :] = acc_lo[U_TILE_ROWS:, :]
        o_ref[t, U_TILE_ROWS:, :] = acc_hi[U_TILE_ROWS:, :]


def tc_peer_experts(table, idx, gates, hp):
    t = hp.shape[0]
    n_steps = t // TC_PEER_TOKENS
    ids = TC_PEER_TOKENS * PEER_SLOTS
    smem = functools.partial(pl.BlockSpec, memory_space=pltpu.SMEM)
    out = pl.pallas_call(
        _tc_peer_kernel,
        grid=(n_steps,),
        in_specs=[
            smem((ids,), lambda i: (i,)),
            smem((ids,), lambda i: (jnp.minimum(i + 1, n_steps - 1),)),
            pl.BlockSpec((TC_PEER_TOKENS, PEER_SLOTS), lambda i: (i, 0)),
            pl.BlockSpec((TC_PEER_TOKENS, U_TILE_ROWS, LANES), lambda i: (i, 0, 0)),
            pl.BlockSpec(memory_space=pl.ANY),
        ],
        out_specs=pl.BlockSpec((TC_PEER_TOKENS, SC_ROW_TILES, LANES), lambda i: (i, 0, 0)),
        out_shape=jax.ShapeDtypeStruct((t, SC_ROW_TILES, LANES), F32),
        scratch_shapes=[
            pltpu.VMEM((TC_PEER_BUFFERS, PEER_SLOTS, SC_ROW_TILES, LANES), jnp.int32),
            pltpu.VMEM((PEER_SLOTS, LANES), F32),
            pltpu.VMEM((PEER_SLOTS, LANES), F32),
            pltpu.SemaphoreType.DMA((TC_PEER_BUFFERS,)),
        ],
        compiler_params=pltpu.CompilerParams(dimension_semantics=("arbitrary",), vmem_limit_bytes=VMEM_LIMIT,
                                             disable_bounds_checks=True),
        name="tc_peer_experts",
    )(idx, idx, gates, hp.reshape(t, U_TILE_ROWS, LANES), table)
    return out.reshape(t, D_MODEL)


TC_PEER_SHARE = 8


def peer_block(x, norm_g, wq, subkeys, table, prev_sc=None):
    t = x.shape[0]
    hp, idx3, gate3 = peer_query(x, norm_g, wq, subkeys)
    idx = jnp.transpose(idx3, (0, 2, 1)).reshape(t * PEER_SLOTS)
    gates = jnp.transpose(gate3, (0, 2, 1)).reshape(t * PEER_SLOTS)
    info = plsc.get_sparse_core_info()
    sc_round = info.num_cores * info.num_subcores * SC_MAX_TOKEN_BLOCK
    t_tc = (t // TC_PEER_SHARE) // sc_round * sc_round
    t_sc = t - t_tc
    idx_sc = idx[:t_sc * PEER_SLOTS]
    if prev_sc is not None:
        idx_sc, _ = lax.optimization_barrier((idx_sc, prev_sc))
    on_sc = sc_peer_experts(table, idx_sc, gates[:t_sc * PEER_SLOTS], hp[:t_sc])
    if t_tc == 0:
        return on_sc, (hp,), on_sc
    on_tc = tc_peer_experts(table, idx[t_sc * PEER_SLOTS:], gates[t_sc * PEER_SLOTS:].reshape(t_tc, PEER_SLOTS),
                            hp[t_sc:])
    return jnp.concatenate([on_sc, on_tc], axis=0), (hp, on_tc), on_sc


def _ple_kernel(x_ref, ffn_ref, p_ref, g_ref, wg_ref, wp_ref, gf_ref, o_ref, *, final):
    x = x_ref[...] + ffn_ref[...]
    gate = _sigmoid(jnp.dot(_rms(x, g_ref[...]).astype(BF16), wg_ref[...], preferred_element_type=F32))
    emb = jnp.dot(p_ref[...].astype(BF16), wp_ref[...], preferred_element_type=F32)
    y = x + gate * emb
    o_ref[...] = _rms(y, gf_ref[...]) if final else y


def ple_block(x, ffn, p, norm_g, w_gate, w_proj, norm_final, final):
    t = x.shape[0]
    tm = min(t, 512)
    row = lambda i: (i, 0)
    full = lambda i: (0, 0)
    return pl.pallas_call(
        functools.partial(_ple_kernel, final=final),
        grid=(t // tm,),
        in_specs=[
            pl.BlockSpec((tm, D_MODEL), row),
            pl.BlockSpec((tm, D_MODEL), row),
            pl.BlockSpec((tm, PLE_DIM), row),
            pl.BlockSpec((1, D_MODEL), full),
            pl.BlockSpec((D_MODEL, D_MODEL), full),
            pl.BlockSpec((PLE_DIM, D_MODEL), full),
            pl.BlockSpec((1, D_MODEL), full),
        ],
        out_specs=pl.BlockSpec((tm, D_MODEL), row),
        out_shape=jax.ShapeDtypeStruct((t, D_MODEL), F32),
        compiler_params=_cparams("parallel"),
        name="ple_block",
    )(x, ffn, p, norm_g.reshape(1, D_MODEL), w_gate, w_proj, norm_final.reshape(1, D_MODEL))


def _prompt_groups(batch):
    sizes = []
    while sum(sizes) < batch:
        nxt = 1 if len(sizes) < 2 else -(-sizes[-1] * 7 // 5)
        sizes.append(min(nxt, batch - sum(sizes)))
    return sizes
def _trunk_layer(x, ple, lw, batch, seq, pool_prefix, pool_start, cache, final, after=None, prev_sc=None):
    if after is not None:
        x, _ = lax.optimization_barrier((x, after))
    uv = norm_matmul(x, lw["norm_mix"], lw["w_uv"], "gelu", F32, 2 * BW)
    mid = norm_matmul(x, lw["norm_mix"], lw["w_mid"], "none", F32, 4 * BW)
    gates = norm_matmul(x, lw["norm_mix"], lw["w_gates"], "sigmoid", BF16, D_MODEL)
    ya, vn = gmlp_mix(uv, lw["gmlp_ln_g"], lw["gmlp_ln_b"], lw["gmlp_ws"], lw["gmlp_bs"], min(seq, GMLP_CHUNK))
    yb, pool_state = pool_mix(mid, pool_prefix, pool_start, lw["pool_w"], lw["pool_scale"], batch, seq)
    if cache is None:
        yc = attn_prompt(mid, lw["rel_bias"], batch, seq)
    else:
        yc = attn_sample(mid, cache[0], cache[1], lw["rel_bias"], batch, seq)
    x = mixer_out(x, ya, yb, yc, gates, lw["w_branch"], lw["w_out"])
    ffn, stage, sc_out = peer_block(x, lw["norm_ffn"], lw["peer_wq"], lw["peer_subkeys"], lw["peer_table"], prev_sc)
    x = ple_block(x, ffn, ple, lw["norm_ple"], lw["ple_gate"], lw["ple_proj"], lw["norm_final"], final)
    return x, mid, pool_state, vn, stage, sc_out


def kernel(x_prompt, x_sample, cache_attn_k, cache_attn_v, state_pool, p_prompt, p_sample, norm_mix, w_in, gmlp_ln_g, gmlp_ln_b, gmlp_ws, gmlp_bs, pool_w, pool_scale, attn_rel_bias, w_branch, w_out, norm_ffn, peer_wq, peer_subkeys, peer_u, peer_v, norm_ple, ple_gate, ple_proj, norm_final):
    bp, lp, _ = x_prompt.shape
    bs, ls, _ = x_sample.shape
    assert lp % BAND_PAST == 0 and lp % GMLP_CHUNK == 0 and ls <= CHUNK
    n_keep = min(BAND_PAST, lp)
    sizes = _prompt_groups(bp)
    starts = [sum(sizes[:g]) for g in range(len(sizes))]
    xg = [x_prompt[a:a + n].reshape(n * lp, D_MODEL) for a, n in zip(starts, sizes)]
    xs = x_sample.reshape(bs * ls, D_MODEL)
    outs = {k: [] for k in ("pk", "pv", "pps", "sk", "sv", "sps", "sgv")}
    after = prev_sc = None
    for i in range(DEPTH):
        w_in_b = w_in[i].astype(BF16)
        lw = dict(
            norm_mix=norm_mix[i],
            w_uv=w_in_b[:, :2 * BW],
            w_mid=w_in_b[:, 2 * BW:6 * BW],
            w_gates=w_in_b[:, 6 * BW:],
            gmlp_ln_g=gmlp_ln_g[i], gmlp_ln_b=gmlp_ln_b[i], gmlp_ws=gmlp_ws[i], gmlp_bs=gmlp_bs[i],
            pool_w=pool_w[i], pool_scale=pool_scale[i], rel_bias=attn_rel_bias[i],
            w_branch=w_branch[i].astype(BF16), w_out=w_out[i].astype(BF16),
            norm_ffn=norm_ffn[i], peer_wq=peer_wq[i].astype(BF16),
            peer_subkeys=peer_subkeys[i].reshape(2 * PEER_HEADS, PEER_NKEYS, PEER_HALF).astype(BF16),
            peer_table=pack_expert_tables(peer_u[i], peer_v[i]),
            norm_ple=norm_ple[i], ple_gate=ple_gate[i].astype(BF16), ple_proj=ple_proj[i].astype(BF16),
            norm_final=norm_final,
        )
        final = i == DEPTH - 1
        pk, pv, pps = [], [], []
        for g, (a, bg) in enumerate(zip(starts, sizes)):
            ple_g = p_prompt[i, a:a + bg].reshape(bg * lp, PLE_DIM)
            zero_prefix = jnp.zeros((bg, POOL_STATE, BW), F32)
            xg[g], mid_p, ps_p, _, after, prev_sc = _trunk_layer(xg[g], ple_g, lw, bg, lp, zero_prefix, 0, None, final,
                                                                 after, prev_sc)
            mid_p = mid_p.reshape(bg, lp, 4 * BW)
            pk.append(mid_p[:, lp - n_keep:, 2 * BW:3 * BW].reshape(bg, n_keep, HEADS, HEAD_DIM))
            pv.append(mid_p[:, lp - n_keep:, 3 * BW:].reshape(bg, n_keep, HEADS, HEAD_DIM))
            pps.append(ps_p)
        outs["pk"].append(jnp.concatenate(pk, axis=0))
        outs["pv"].append(jnp.concatenate(pv, axis=0))
        outs["pps"].append(jnp.concatenate(pps, axis=0))
        xs, mid_s, ps_s, vn_s, _, _ = _trunk_layer(xs, p_sample[i].reshape(bs * ls, PLE_DIM), lw, bs, ls, state_pool[i],
                                                PAST_LEN, (cache_attn_k[i], cache_attn_v[i]), final)
        mid_s = mid_s.reshape(bs, ls, 4 * BW)
        outs["sk"].append(mid_s[:, :, 2 * BW:3 * BW].reshape(bs, ls, HEADS, HEAD_DIM))
        outs["sv"].append(mid_s[:, :, 3 * BW:].reshape(bs, ls, HEADS, HEAD_DIM))
        outs["sps"].append(ps_s)
        outs["sgv"].append(vn_s.reshape(bs, ls, BW))
    st = lambda k: jnp.stack(outs[k])
    y_prompt = jnp.concatenate(xg, axis=0).reshape(bp, lp, D_MODEL)
    return (y_prompt, xs.reshape(bs, ls, D_MODEL), st("pk"), st("pv"), st("pps"),
            st("sk"), st("sv"), st("sps"), st("sgv"))
```

```python
import functools
import math

import jax
import jax.numpy as jnp
import numpy as np
from jax import lax
from jax.experimental import pallas as pl
from jax.experimental.pallas import tpu as pltpu
from jax.experimental.pallas import tpu_sc as plsc

F32 = jnp.float32
BF16 = jnp.bfloat16

D_MODEL = 1024
DEPTH = 2
CHUNK = 64
EPS = 1e-6
BW = D_MODEL // 2
GMLP_CHUNK = 128
GROUPS = 4
GDIM = BW // GROUPS
POOL_WINDOWS = (2, 4, 8, 16)
POOL_STATE = 15
POOL_PAD = 16
HEADS = 8
HEAD_DIM = BW // HEADS
BAND_CHUNKS = 8
BAND_PAST = BAND_CHUNKS * CHUNK
REL_CLIP = 128
PAST_LEN = 4096
PEER_HEADS = 8
PEER_NKEYS = 128
PEER_HALF = 128
PEER_TOPK = 16
PEER_SLOTS = PEER_HEADS * PEER_TOPK
PLE_DIM = 256

LANES = 128
VMEM_LIMIT = 56 * 1024 * 1024
NEG = -1e30


def _cparams(*sem):
    return pltpu.CompilerParams(dimension_semantics=sem, vmem_limit_bytes=VMEM_LIMIT)


def _rms(x, g):
    ms = jnp.mean(x * x, axis=-1, keepdims=True)
    return x * lax.rsqrt(ms + EPS) * g


def _gelu(x):
    c = math.sqrt(2.0 / math.pi)
    return 0.5 * x * (1.0 + jnp.tanh(c * (x + 0.044715 * (x * x * x))))


def _sigmoid(x):
    return 1.0 / (1.0 + jnp.exp(-x))


_ACTS = {"gelu": _gelu, "sigmoid": _sigmoid, "none": lambda z: z}


def _norm_matmul_kernel(x_ref, g_ref, w_ref, o_ref, h_ref, *, act):
    @pl.when(pl.program_id(1) == 0)
    def _():
        h_ref[...] = _rms(x_ref[...], g_ref[...]).astype(BF16)

    z = jnp.dot(h_ref[...], w_ref[...], preferred_element_type=F32)
    o_ref[...] = _ACTS[act](z).astype(o_ref.dtype)


def norm_matmul(x, g, w, act, out_dtype, tn):
    t, d = x.shape
    n = w.shape[1]
    tm = min(t, 512)
    return pl.pallas_call(
        functools.partial(_norm_matmul_kernel, act=act),
        grid=(t // tm, n // tn),
        in_specs=[
            pl.BlockSpec((tm, d), lambda i, j: (i, 0)),
            pl.BlockSpec((1, d), lambda i, j: (0, 0)),
            pl.BlockSpec((d, tn), lambda i, j: (0, j)),
        ],
        out_specs=pl.BlockSpec((tm, tn), lambda i, j: (i, j)),
        out_shape=jax.ShapeDtypeStruct((t, n), out_dtype),
        scratch_shapes=[pltpu.VMEM((tm, d), BF16)],
        compiler_params=_cparams("parallel", "arbitrary"),
        name="norm_matmul_" + act,
    )(x, g.reshape(1, d), w)


def _gmlp_kernel(uv_ref, lng_ref, lnb_ref, ws_ref, bst_ref, y_ref, vn_ref, *, lc):
    u = uv_ref[:, :BW]
    v = uv_ref[:, BW:]
    mu = jnp.mean(v, axis=-1, keepdims=True)
    vc = v - mu
    var = jnp.mean(vc * vc, axis=-1, keepdims=True)
    vn = vc * lax.rsqrt(var + EPS) * lng_ref[...] + lnb_ref[...]
    vn_ref[...] = vn
    row = lax.broadcasted_iota(jnp.int32, (lc, lc), 0) // CHUNK
    col = lax.broadcasted_iota(jnp.int32, (lc, lc), 1) // CHUNK
    causal = col <= row
    vnb = vn.astype(BF16)
    for g in range(GROUPS):
        w = jnp.where(causal, ws_ref[g], 0.0).astype(BF16)
        s = jnp.dot(w, vnb[:, g * GDIM:(g + 1) * GDIM], preferred_element_type=F32)
        s = s + bst_ref[:, g:g + 1]
        y_ref[:, g * GDIM:(g + 1) * GDIM] = (u[:, g * GDIM:(g + 1) * GDIM] * s).astype(y_ref.dtype)


def gmlp_mix(uv, ln_g, ln_b, ws, bs, lc):
    t = uv.shape[0]
    return pl.pallas_call(
        functools.partial(_gmlp_kernel, lc=lc),
        grid=(t // lc,),
        in_specs=[
            pl.BlockSpec((lc, 2 * BW), lambda i: (i, 0)),
            pl.BlockSpec((1, BW), lambda i: (0, 0)),
            pl.BlockSpec((1, BW), lambda i: (0, 0)),
            pl.BlockSpec((GROUPS, lc, lc), lambda i: (0, 0, 0)),
            pl.BlockSpec((lc, GROUPS), lambda i: (0, 0)),
        ],
        out_specs=[
            pl.BlockSpec((lc, BW), lambda i: (i, 0)),
            pl.BlockSpec((lc, BW), lambda i: (i, 0)),
        ],
        out_shape=[
            jax.ShapeDtypeStruct((t, BW), BF16),
            jax.ShapeDtypeStruct((t, BW), F32),
        ],
        compiler_params=_cparams("parallel"),
        name="gmlp_mix",
    )(uv, ln_g.reshape(1, BW), ln_b.reshape(1, BW), ws[:, :lc, :lc], bs[:, :lc].T)


def _pool_kernel(u_ref, pre_ref, w_ref, sc_ref, y_ref, st_ref, pad_ref, *, seq, start_pos):
    pad_ref[0:POOL_PAD, :] = pre_ref[...]
    pad_ref[POOL_PAD:, :] = u_ref[...]
    pos = lax.broadcasted_iota(jnp.int32, (seq, 1), 0) + start_pos
    for g, win in enumerate(POOL_WINDOWS):
        cols = slice(g * GDIM, (g + 1) * GDIM)
        tok = pad_ref[POOL_PAD:, cols]
        acc = tok
        for k in range(1, win):
            acc = acc + pad_ref[POOL_PAD - k:POOL_PAD - k + seq, cols]
        cnt = jnp.minimum(pos + 1, win).astype(F32)
        d = acc / cnt - tok
        y = jnp.dot(d.astype(BF16), w_ref[g], preferred_element_type=F32)
        y_ref[:, cols] = (y * sc_ref[:, cols]).astype(y_ref.dtype)
    st_ref[...] = pad_ref[seq + 1:seq + POOL_PAD, :]


def pool_mix(mid, prefix, start_pos, pool_w, pool_scale, batch, seq):
    mid3 = mid.reshape(batch, seq, 4 * BW)
    pre = jnp.concatenate([jnp.zeros((batch, 1, BW), F32), prefix], axis=1)
    y, st = pl.pallas_call(
        functools.partial(_pool_kernel, seq=seq, start_pos=start_pos),
        grid=(batch,),
        in_specs=[
            pl.BlockSpec((None, seq, BW), lambda b: (b, 0, 0)),
            pl.BlockSpec((None, POOL_PAD, BW), lambda b: (b, 0, 0)),
            pl.BlockSpec((GROUPS, GDIM, GDIM), lambda b: (0, 0, 0)),
            pl.BlockSpec((1, BW), lambda b: (0, 0)),
        ],
        out_specs=[
            pl.BlockSpec((None, seq, BW), lambda b: (b, 0, 0)),
            pl.BlockSpec((None, POOL_STATE, BW), lambda b: (b, 0, 0)),
        ],
        out_shape=[
            jax.ShapeDtypeStruct((batch, seq, BW), BF16),
            jax.ShapeDtypeStruct((batch, POOL_STATE, BW), F32),
        ],
        scratch_shapes=[pltpu.VMEM((seq + POOL_PAD, BW), F32)],
        compiler_params=_cparams("parallel"),
        name="pool_mix",
    )(mid3, pre, pool_w.astype(BF16), pool_scale.reshape(1, BW))
    return y.reshape(batch * seq, BW), st


def _attn_chunks(q_ref, kcat_ref, vcat_ref, bias_ref, o_ref, *, n_chunks, cq, band, first_block):
    scale = HEAD_DIM ** -0.5
    for h in range(HEADS):
        cols = slice(h * HEAD_DIM, (h + 1) * HEAD_DIM)
        bias = bias_ref[h]
        for i in range(n_chunks):
            q = (q_ref[i * cq:(i + 1) * cq, cols] * scale).astype(BF16)
            k = kcat_ref[i * cq:i * cq + band, cols]
            v = vcat_ref[i * cq:i * cq + band, cols]
            s = lax.dot_general(q, k, (((1,), (1,)), ((), ())), preferred_element_type=F32) + bias
            if first_block is not None:
                key = lax.broadcasted_iota(jnp.int32, (1, band), 1)
                s = jnp.where(key >= first_block * (BAND_PAST - i * cq), s, NEG)
            m = jnp.max(s, axis=-1, keepdims=True)
            p = jnp.exp(s - m)
            l = jnp.sum(p, axis=-1, keepdims=True)
            o = jnp.dot(p.astype(BF16), v, preferred_element_type=F32) / l
            o_ref[i * cq:(i + 1) * cq, cols] = o.astype(o_ref.dtype)


def _attn_prompt_kernel(q_ref, kp_ref, ko_ref, vp_ref, vo_ref, bias_ref, o_ref, kcat_ref, vcat_ref):
    kcat_ref[0:BAND_PAST, :] = kp_ref[...].astype(BF16)
    kcat_ref[BAND_PAST:, :] = ko_ref[...].astype(BF16)
    vcat_ref[0:BAND_PAST, :] = vp_ref[...].astype(BF16)
    vcat_ref[BAND_PAST:, :] = vo_ref[...].astype(BF16)
    _attn_chunks(q_ref, kcat_ref, vcat_ref, bias_ref, o_ref, n_chunks=BAND_CHUNKS, cq=CHUNK,
                 band=BAND_PAST + CHUNK, first_block=(pl.program_id(1) == 0).astype(jnp.int32))


def _rel_bias_tile(rel_bias, qpos, kpos):
    rel = np.clip(qpos[:, None] - kpos[None, :], -REL_CLIP, REL_CLIP) + REL_CLIP
    return rel_bias[:, rel]


def attn_prompt(mid, rel_bias, batch, seq):
    mid3 = mid.reshape(batch, seq, 4 * BW)
    blk = BAND_PAST
    bias = _rel_bias_tile(rel_bias, np.arange(CHUNK), np.arange(BAND_PAST + CHUNK) - BAND_PAST)
    prev = lambda b, j: jnp.maximum(j - 1, 0)
    y = pl.pallas_call(
        _attn_prompt_kernel,
        grid=(batch, seq // blk),
        in_specs=[
            pl.BlockSpec((None, blk, BW), lambda b, j: (b, j, 1)),
            pl.BlockSpec((None, blk, BW), lambda b, j: (b, prev(b, j), 2)),
            pl.BlockSpec((None, blk, BW), lambda b, j: (b, j, 2)),
            pl.BlockSpec((None, blk, BW), lambda b, j: (b, prev(b, j), 3)),
            pl.BlockSpec((None, blk, BW), lambda b, j: (b, j, 3)),
            pl.BlockSpec((HEADS, CHUNK, BAND_PAST + CHUNK), lambda b, j: (0, 0, 0)),
        ],
        out_specs=pl.BlockSpec((None, blk, BW), lambda b, j: (b, j, 0)),
        out_shape=jax.ShapeDtypeStruct((batch, seq, BW), BF16),
        scratch_shapes=[pltpu.VMEM((2 * blk, BW), BF16), pltpu.VMEM((2 * blk, BW), BF16)],
        compiler_params=_cparams("parallel", "parallel"),
        name="attn_prompt",
    )(mid3, mid3, mid3, mid3, mid3, bias)
    return y.reshape(batch * seq, BW)


def _attn_sample_kernel(q_ref, kc_ref, kn_ref, vc_ref, vn_ref, bias_ref, o_ref, kcat_ref, vcat_ref, *, n_cache):
    kcat_ref[0:n_cache, :] = kc_ref[...].astype(BF16)
    kcat_ref[n_cache:, :] = kn_ref[...].astype(BF16)
    vcat_ref[0:n_cache, :] = vc_ref[...].astype(BF16)
    vcat_ref[n_cache:, :] = vn_ref[...].astype(BF16)
    seq = q_ref.shape[0]
    _attn_chunks(q_ref, kcat_ref, vcat_ref, bias_ref, o_ref, n_chunks=1, cq=seq, band=n_cache + seq,
                 first_block=None)


def attn_sample(mid, cache_k, cache_v, rel_bias, batch, seq):
    n_cache = cache_k.shape[1]
    assert PAST_LEN >= n_cache
    mid3 = mid.reshape(batch, seq, 4 * BW)
    ck = cache_k.reshape(batch, n_cache, BW)
    cv = cache_v.reshape(batch, n_cache, BW)
    bias = _rel_bias_tile(rel_bias, PAST_LEN + np.arange(seq), PAST_LEN - n_cache + np.arange(n_cache + seq))
    y = pl.pallas_call(
        functools.partial(_attn_sample_kernel, n_cache=n_cache),
        grid=(batch,),
        in_specs=[
            pl.BlockSpec((None, seq, BW), lambda b: (b, 0, 1)),
            pl.BlockSpec((None, n_cache, BW), lambda b: (b, 0, 0)),
            pl.BlockSpec((None, seq, BW), lambda b: (b, 0, 2)),
            pl.BlockSpec((None, n_cache, BW), lambda b: (b, 0, 0)),
            pl.BlockSpec((None, seq, BW), lambda b: (b, 0, 3)),
            pl.BlockSpec((HEADS, seq, n_cache + seq), lambda b: (0, 0, 0)),
        ],
        out_specs=pl.BlockSpec((None, seq, BW), lambda b: (b, 0, 0)),
        out_shape=jax.ShapeDtypeStruct((batch, seq, BW), BF16),
        scratch_shapes=[pltpu.VMEM((n_cache + seq, BW), BF16), pltpu.VMEM((n_cache + seq, BW), BF16)],
        compiler_params=_cparams("parallel"),
        name="attn_sample",
    )(mid3, ck, mid3, cv, mid3, bias)
    return y.reshape(batch * seq, BW)


def _mixer_out_kernel(x_ref, ya_ref, yb_ref, yc_ref, gate_ref, wb_ref, wo_ref, o_ref):
    acc = None
    for n, y_ref in enumerate((ya_ref, yb_ref, yc_ref)):
        proj = jnp.dot(y_ref[...], wb_ref[n], preferred_element_type=F32)
        term = gate_ref[:, n * D_MODEL:(n + 1) * D_MODEL].astype(F32) * proj
        acc = term if acc is None else acc + term
    o_ref[...] = x_ref[...] + jnp.dot(acc.astype(BF16), wo_ref[...], preferred_element_type=F32)


def mixer_out(x, ya, yb, yc, gates, w_branch, w_out):
    t = x.shape[0]
    tm = min(t, 512)
    row = lambda i: (i, 0)
    return pl.pallas_call(
        _mixer_out_kernel,
        grid=(t // tm,),
        in_specs=[
            pl.BlockSpec((tm, D_MODEL), row),
            pl.BlockSpec((tm, BW), row),
            pl.BlockSpec((tm, BW), row),
            pl.BlockSpec((tm, BW), row),
            pl.BlockSpec((tm, 3 * D_MODEL), row),
            pl.BlockSpec((3, BW, D_MODEL), lambda i: (0, 0, 0)),
            pl.BlockSpec((D_MODEL, D_MODEL), lambda i: (0, 0)),
        ],
        out_specs=pl.BlockSpec((tm, D_MODEL), row),
        out_shape=jax.ShapeDtypeStruct((t, D_MODEL), F32),
        compiler_params=_cparams("parallel"),
        name="mixer_out",
    )(x, ya, yb, yc, gates, w_branch, w_out)


def _extract_top(s, payload, k):
    r = float(s.shape[0])
    rows = lax.broadcasted_iota(jnp.int32, s.shape, 0).astype(F32)
    vals, pays = [], []
    for _ in range(k):
        m = jnp.max(s, axis=0, keepdims=True)
        idx = jnp.min(jnp.where(s == m, rows, r), axis=0, keepdims=True)
        sel = rows == idx
        vals.append(m)
        pays.append(idx if payload is None else jnp.max(jnp.where(sel, payload, -1.0), axis=0, keepdims=True))
        s = jnp.where(sel, -jnp.inf, s)
    return jnp.concatenate(vals, axis=0), jnp.concatenate(pays, axis=0)


def _pair_candidates(sv, si):
    k = PEER_TOPK
    sub = 8
    assert k == 2 * sub
    b_row = lax.broadcasted_iota(jnp.int32, (sub, LANES), 0)
    vals = [sv[0][0:1] + sv[1], sv[0][1:2] + sv[1][0:sub]]
    ids = [si[0][0:1] * PEER_NKEYS + si[1], si[0][1:2] * PEER_NKEYS + si[1][0:sub]]
    for a in range(2, sub):
        keep = b_row < k // (a + 1)
        vals.append(jnp.where(keep, sv[0][a:a + 1] + sv[1][0:sub], -jnp.inf))
        ids.append(si[0][a:a + 1] * PEER_NKEYS + si[1][0:sub])
    vals.append(sv[0][sub:k] + sv[1][0:1])
    ids.append(si[0][sub:k] * PEER_NKEYS + si[1][0:1])
    return jnp.concatenate(vals, axis=0), jnp.concatenate(ids, axis=0)


def _peer_query_kernel(x_ref, g_ref, wq_ref, sk_ref, hp_ref, idx_ref, gate_ref, q_ref, *, tm):
    hb = _rms(x_ref[...], g_ref[...]).astype(BF16)
    q_ref[...] = jnp.dot(hb, wq_ref[...], preferred_element_type=F32).astype(BF16)
    bits = lax.bitcast_convert_type(hb.astype(F32), jnp.int32)
    half = D_MODEL // 2
    hp_ref[...] = bits[:, half:] | lax.shift_right_logical(bits[:, :half], 16)

    def sub_block(sb, carry):
        tok = pl.ds(pl.multiple_of(sb * LANES, LANES), LANES)
        for hd in range(PEER_HEADS):
            sv, si = [], []
            for p in range(2):
                hp = hd * 2 + p
                q = q_ref[tok, hp * PEER_HALF:(hp + 1) * PEER_HALF]
                s = lax.dot_general(sk_ref[hp], q, (((1,), (1,)), ((), ())), preferred_element_type=F32)
                v, i = _extract_top(s, None, PEER_TOPK)
                sv.append(v)
                si.append(i)
            cand, eid = _pair_candidates(sv, si)
            tv, te = _extract_top(cand, eid, PEER_TOPK)
            e = jnp.exp(tv - tv[0:1])
            gate = e / jnp.sum(e, axis=0, keepdims=True)
            idx_ref[sb, hd * PEER_TOPK:(hd + 1) * PEER_TOPK, :] = te.astype(jnp.int32)
            gate_ref[sb, hd * PEER_TOPK:(hd + 1) * PEER_TOPK, :] = gate
        return carry

    lax.fori_loop(0, tm // LANES, sub_block, 0)


def peer_query(x, norm_g, wq, subkeys):
    t = x.shape[0]
    tm = min(t, 512)
    nq = wq.shape[1]
    nsb = tm // LANES
    return pl.pallas_call(
        functools.partial(_peer_query_kernel, tm=tm),
        grid=(t // tm,),
        in_specs=[
            pl.BlockSpec((tm, D_MODEL), lambda i: (i, 0)),
            pl.BlockSpec((1, D_MODEL), lambda i: (0, 0)),
            pl.BlockSpec((D_MODEL, nq), lambda i: (0, 0)),
            pl.BlockSpec((2 * PEER_HEADS, PEER_NKEYS, PEER_HALF), lambda i: (0, 0, 0)),
        ],
        out_specs=[
            pl.BlockSpec((tm, D_MODEL // 2), lambda i: (i, 0)),
            pl.BlockSpec((nsb, PEER_SLOTS, LANES), lambda i: (i, 0, 0)),
            pl.BlockSpec((nsb, PEER_SLOTS, LANES), lambda i: (i, 0, 0)),
        ],
        out_shape=[
            jax.ShapeDtypeStruct((t, D_MODEL // 2), jnp.int32),
            jax.ShapeDtypeStruct((t // LANES, PEER_SLOTS, LANES), jnp.int32),
            jax.ShapeDtypeStruct((t // LANES, PEER_SLOTS, LANES), F32),
        ],
        scratch_shapes=[pltpu.VMEM((tm, nq), BF16)],
        compiler_params=_cparams("parallel"),
        name="peer_query",
    )(x, norm_g.reshape(1, D_MODEL), wq, subkeys)


HALF_D = D_MODEL // 2
SC_LANES = 16
SC_UNIT_ROWS = 32
SC_UNITS_PER_TOKEN = PEER_SLOTS // SC_UNIT_ROWS
SC_MAX_TOKEN_BLOCK = 32
SC_ROW_GROUP = 8
SC_HALF_VECS = HALF_D // SC_LANES
SC_OUT_VECS = 8
SC_BF16_TERMS = 4
SC_ROW_TILES = D_MODEL // LANES


def pack_expert_tables(peer_u, peer_v):
    def pack(x):
        b = lax.bitcast_convert_type(x.astype(BF16), jnp.uint16).astype(jnp.uint32)
        return (b[:, HALF_D:] << 16) | b[:, :HALF_D]

    rows = lax.bitcast_convert_type(jnp.concatenate([pack(peer_u), pack(peer_v)], axis=1), jnp.int32)
    return rows.reshape(rows.shape[0], SC_ROW_TILES, LANES)


def sc_peer_experts(table, idx, gates, hp):
    t = hp.shape[0]
    info = plsc.get_sparse_core_info()
    n_workers = info.num_cores * info.num_subcores
    tpw = t // n_workers
    token_block = min(tpw, SC_MAX_TOKEN_BLOCK)
    assert t % (n_workers * token_block) == 0
    units = token_block * SC_UNITS_PER_TOKEN
    mesh = plsc.VectorSubcoreMesh(core_axis_name="core", subcore_axis_name="subcore")
    hi_mask = jnp.int32(-65536)
    gelu_c = math.sqrt(2.0 / math.pi)

    @functools.partial(
        pl.kernel,
        out_type=jax.ShapeDtypeStruct((t, D_MODEL), F32),
        mesh=mesh,
        scratch_types=[
            pltpu.VMEM((token_block * PEER_SLOTS,), jnp.int32),
            pltpu.VMEM((token_block * PEER_SLOTS,), F32),
            pltpu.VMEM((token_block, HALF_D), jnp.int32),
            pltpu.VMEM((token_block, D_MODEL), F32),
            pltpu.VMEM((2, SC_UNIT_ROWS, SC_ROW_TILES, LANES), jnp.int32),
            pltpu.VMEM((SC_UNIT_ROWS, SC_LANES), F32),
            pltpu.VMEM((SC_UNIT_ROWS,), jnp.int32),
            pltpu.SemaphoreType.DMA((2,)),
        ],
        compiler_params=pltpu.CompilerParams(needs_layout_passes=False),
        name="peer_sc_experts",
    )
    def kern(table_hbm, idx_hbm, gate_hbm, h_hbm, out_hbm, idx_v, gate_v, h_v, out_v, rows_v, part_v, coef_v, sem):
        wid = lax.axis_index("subcore") * info.num_cores + lax.axis_index("core")
        lane = lax.iota(jnp.int32, SC_LANES)
        zero = jnp.zeros((SC_LANES,), F32)

        def gather(unit, b):
            rows = idx_v.at[pl.ds(unit * SC_UNIT_ROWS, SC_UNIT_ROWS)]
            return pltpu.make_async_copy(table_hbm.at[rows], rows_v.at[b], sem.at[b])

        def row_vec(b, r, vec):
            per_tile_row = LANES // SC_LANES
            lane0 = pl.multiple_of((vec % per_tile_row) * SC_LANES, SC_LANES)
            return rows_v[b, r, vec // per_tile_row, pl.ds(lane0, SC_LANES)]

        def unpack(w):
            return lax.bitcast_convert_type(w << 16, F32), lax.bitcast_convert_type(w & hi_mask, F32)

        def as_pairs(w):
            return plsc.bitcast(w, BF16)

        def compute(unit, b):
            tl = unit // SC_UNITS_PER_TOKEN
            q = unit % SC_UNITS_PER_TOKEN

            def row_group(rg, carry):
                def kstep(k, accs):
                    hs = []
                    for j in range(SC_BF16_TERMS):
                        off = pl.multiple_of((k * SC_BF16_TERMS + j) * SC_LANES, SC_LANES)
                        hs.append(as_pairs(h_v[tl, pl.ds(off, SC_LANES)]))
                    new = []
                    for r in range(SC_ROW_GROUP):
                        p = None
                        for j in range(SC_BF16_TERMS):
                            term = as_pairs(row_vec(b, rg * SC_ROW_GROUP + r, k * SC_BF16_TERMS + j)) * hs[j]
                            p = term if p is None else p + term
                        lo, hi = unpack(plsc.bitcast(p, jnp.int32))
                        new.append(accs[r] + lo + hi)
                    return tuple(new)

                accs = lax.fori_loop(0, SC_HALF_VECS // SC_BF16_TERMS, kstep, (zero,) * SC_ROW_GROUP)
                for r in range(SC_ROW_GROUP):
                    part_v[rg * SC_ROW_GROUP + r, :] = accs[r]
                return carry

            lax.fori_loop(0, SC_UNIT_ROWS // SC_ROW_GROUP, row_group, 0)

            for i in range(SC_UNIT_ROWS // SC_LANES):
                rows = lane + i * SC_LANES
                a = zero
                for l in range(SC_LANES):
                    a = a + plsc.load_gather(part_v, [rows, jnp.full((SC_LANES,), l, jnp.int32)])
                z = gelu_c * (a + 0.044715 * (a * a * a))
                act = a / (1.0 + jnp.exp(-2.0 * z))
                slot = pl.multiple_of(tl * PEER_SLOTS + q * SC_UNIT_ROWS + i * SC_LANES, SC_LANES)
                bits = lax.bitcast_convert_type(gate_v[pl.ds(slot, SC_LANES)] * act, jnp.int32)
                top = (bits + 0x7FFF + ((bits >> 16) & 1)) & hi_mask
                coef_v[pl.ds(i * SC_LANES, SC_LANES)] = top | lax.shift_right_logical(top, 16)

            def out_pass(dq, carry):
                def row_quad(rq, accs):
                    cs = []
                    for j in range(SC_BF16_TERMS):
                        row = jnp.full((SC_LANES,), rq * SC_BF16_TERMS + j, jnp.int32)
                        cs.append(as_pairs(plsc.load_gather(coef_v, [row])))
                    new = []
                    for k in range(SC_OUT_VECS):
                        p = None
                        for j in range(SC_BF16_TERMS):
                            w = row_vec(b, rq * SC_BF16_TERMS + j, SC_HALF_VECS + dq * SC_OUT_VECS + k)
                            term = as_pairs(w) * cs[j]
                            p = term if p is None else p + term
                        lo, hi = unpack(plsc.bitcast(p, jnp.int32))
                        new.append(accs[2 * k] + lo)
                        new.append(accs[2 * k + 1] + hi)
                    return tuple(new)

                accs = lax.fori_loop(0, SC_UNIT_ROWS // SC_BF16_TERMS, row_quad, (zero,) * (2 * SC_OUT_VECS))
                for k in range(SC_OUT_VECS):
                    off = pl.multiple_of((dq * SC_OUT_VECS + k) * SC_LANES, SC_LANES)
                    plsc.addupdate(out_v.at[tl, pl.ds(off, SC_LANES)], accs[2 * k])
                    plsc.addupdate(out_v.at[tl, pl.ds(HALF_D + off, SC_LANES)], accs[2 * k + 1])
                return carry

            lax.fori_loop(0, SC_HALF_VECS // SC_OUT_VECS, out_pass, 0)

        @pl.loop(0, tpw // token_block)
        def _(blk):
            tok0 = wid * tpw + blk * token_block
            slots = pl.ds(tok0 * PEER_SLOTS, token_block * PEER_SLOTS)
            pltpu.sync_copy(idx_hbm.at[slots], idx_v)
            pltpu.sync_copy(gate_hbm.at[slots], gate_v)
            pltpu.sync_copy(h_hbm.at[pl.ds(tok0, token_block)], h_v)

            @pl.loop(0, token_block)
            def _(tl):
                @pl.loop(0, D_MODEL // SC_LANES)
                def _(k):
                    out_v[tl, pl.ds(pl.multiple_of(k * SC_LANES, SC_LANES), SC_LANES)] = zero

            gather(0, 0).start()

            @pl.loop(0, units, step=2)
            def _(u0):
                for b in range(2):
                    unit = u0 + b

                    @pl.when(unit + 1 < units)
                    def _():
                        gather(unit + 1, 1 - b).start()

                    gather(unit, b).wait()
                    compute(unit, b)

            pltpu.sync_copy(out_v, out_hbm.at[pl.ds(tok0, token_block)])

    return kern(table, idx, gates, hp)


TC_PEER_TOKENS = 8
TC_PEER_BUFFERS = 4
U_TILE_ROWS = SC_ROW_TILES // 2


def _tc_peer_kernel(idx_ref, idx_next_ref, gate_ref, hp_ref, table_hbm, o_ref, rows_buf, a_buf, c_buf, sem):
    step = pl.program_id(0)
    hi_mask = jnp.int32(-65536)

    def unpack(w):
        return lax.bitcast_convert_type(w << 16, F32), lax.bitcast_convert_type(w & hi_mask, F32)

    def row_copy(ids_ref, tok, e, slot):
        return pltpu.make_async_copy(table_hbm.at[ids_ref[tok * PEER_SLOTS + e]], rows_buf.at[slot, e], sem.at[slot])

    def wait_rows(slot):
        pltpu.make_async_copy(table_hbm.at[pl.ds(0, PEER_SLOTS)], rows_buf.at[slot], sem.at[slot]).wait()

    ahead = TC_PEER_BUFFERS - 1

    @pl.when(step == 0)
    def _():
        for tok in range(ahead):
            def first(e, c, tok=tok):
                row_copy(idx_ref, tok, e, tok).start()
                return c

            lax.fori_loop(0, PEER_SLOTS, first, 0, unroll=8)

    gates_t = gate_ref[...].T
    zeros_v = jnp.zeros((SC_ROW_TILES - U_TILE_ROWS, LANES), F32)
    for t in range(TC_PEER_TOKENS):
        slot = t % TC_PEER_BUFFERS
        nxt = t + ahead
        nxt_slot = nxt % TC_PEER_BUFFERS
        wait_rows(slot)
        h_lo, h_hi = unpack(hp_ref[t])
        h_lo = jnp.concatenate([h_lo, zeros_v], axis=0)
        h_hi = jnp.concatenate([h_hi, zeros_v], axis=0)

        def u_body(e, c, slot=slot, nxt=nxt, nxt_slot=nxt_slot, h_lo=h_lo, h_hi=h_hi):
            if nxt >= TC_PEER_TOKENS:
                @pl.when(step + 1 < pl.num_programs(0))
                def _():
                    row_copy(idx_next_ref, nxt - TC_PEER_TOKENS, e, nxt_slot).start()
            else:
                row_copy(idx_ref, nxt, e, nxt_slot).start()
            lo, hi = unpack(rows_buf[slot, e])
            a_buf[pl.ds(e, 1), :] = jnp.sum(lo * h_lo + hi * h_hi, axis=0, keepdims=True)
            return c

        lax.fori_loop(0, PEER_SLOTS, u_body, 0, unroll=8)
        a = jnp.sum(a_buf[...], axis=1, keepdims=True)
        c_buf[...] = jnp.broadcast_to(gates_t[:, t:t + 1] * _gelu(a), (PEER_SLOTS, LANES))

        def v_body(e, acc, slot=slot):
            lo, hi = unpack(rows_buf[slot, e])
            c = c_buf[pl.ds(e, 1), :]
            return acc[0] + c * lo, acc[1] + c * hi

        zero = jnp.zeros((SC_ROW_TILES, LANES), F32)
        acc_lo, acc_hi = lax.fori_loop(0, PEER_SLOTS, v_body, (zero, zero), unroll=8)
        o_ref[t, 0:U_TILE_ROWS, :] = acc_lo[U_TILE_ROWS:, :]
        o_ref[t, U_TILE_ROWS:, :] = acc_hi[U_TILE_ROWS:, :]


def tc_peer_experts(table, idx, gates, hp):
    t = hp.shape[0]
    n_steps = t // TC_PEER_TOKENS
    ids = TC_PEER_TOKENS * PEER_SLOTS
    smem = functools.partial(pl.BlockSpec, memory_space=pltpu.SMEM)
    out = pl.pallas_call(
        _tc_peer_kernel,
        grid=(n_steps,),
        in_specs=[
            smem((ids,), lambda i: (i,)),
            smem((ids,), lambda i: (jnp.minimum(i + 1, n_steps - 1),)),
            pl.BlockSpec((TC_PEER_TOKENS, PEER_SLOTS), lambda i: (i, 0)),
            pl.BlockSpec((TC_PEER_TOKENS, U_TILE_ROWS, LANES), lambda i: (i, 0, 0)),
            pl.BlockSpec(memory_space=pl.ANY),
        ],
        out_specs=pl.BlockSpec((TC_PEER_TOKENS, SC_ROW_TILES, LANES), lambda i: (i, 0, 0)),
        out_shape=jax.ShapeDtypeStruct((t, SC_ROW_TILES, LANES), F32),
        scratch_shapes=[
            pltpu.VMEM((TC_PEER_BUFFERS, PEER_SLOTS, SC_ROW_TILES, LANES), jnp.int32),
            pltpu.VMEM((PEER_SLOTS, LANES), F32),
            pltpu.VMEM((PEER_SLOTS, LANES), F32),
            pltpu.SemaphoreType.DMA((TC_PEER_BUFFERS,)),
        ],
        compiler_params=pltpu.CompilerParams(dimension_semantics=("arbitrary",), vmem_limit_bytes=VMEM_LIMIT,
                                             disable_bounds_checks=True),
        name="tc_peer_experts",
    )(idx, idx, gates, hp.reshape(t, U_TILE_ROWS, LANES), table)
    return out.reshape(t, D_MODEL)


TC_PEER_SHARE = 6


def peer_block(x, norm_g, wq, subkeys, table, prev_sc=None):
    t = x.shape[0]
    hp, idx3, gate3 = peer_query(x, norm_g, wq, subkeys)
    idx = jnp.transpose(idx3, (0, 2, 1)).reshape(t * PEER_SLOTS)
    gates = jnp.transpose(gate3, (0, 2, 1)).reshape(t * PEER_SLOTS)
    info = plsc.get_sparse_core_info()
    sc_round = info.num_cores * info.num_subcores * SC_MAX_TOKEN_BLOCK
    t_tc = (t // TC_PEER_SHARE) // sc_round * sc_round
    t_sc = t - t_tc
    idx_sc = idx[:t_sc * PEER_SLOTS]
    if prev_sc is not None:
        idx_sc, _ = lax.optimization_barrier((idx_sc, prev_sc))
    on_sc = sc_peer_experts(table, idx_sc, gates[:t_sc * PEER_SLOTS], hp[:t_sc])
    if t_tc == 0:
        return on_sc, (hp,), on_sc
    on_tc = tc_peer_experts(table, idx[t_sc * PEER_SLOTS:], gates[t_sc * PEER_SLOTS:].reshape(t_tc, PEER_SLOTS),
                            hp[t_sc:])
    return jnp.concatenate([on_sc, on_tc], axis=0), (hp, on_tc), on_sc


def _ple_kernel(x_ref, ffn_ref, p_ref, g_ref, wg_ref, wp_ref, gf_ref, o_ref, *, final):
    x = x_ref[...] + ffn_ref[...]
    gate = _sigmoid(jnp.dot(_rms(x, g_ref[...]).astype(BF16), wg_ref[...], preferred_element_type=F32))
    emb = jnp.dot(p_ref[...].astype(BF16), wp_ref[...], preferred_element_type=F32)
    y = x + gate * emb
    o_ref[...] = _rms(y, gf_ref[...]) if final else y


def ple_block(x, ffn, p, norm_g, w_gate, w_proj, norm_final, final):
    t = x.shape[0]
    tm = min(t, 512)
    row = lambda i: (i, 0)
    full = lambda i: (0, 0)
    return pl.pallas_call(
        functools.partial(_ple_kernel, final=final),
        grid=(t // tm,),
        in_specs=[
            pl.BlockSpec((tm, D_MODEL), row),
            pl.BlockSpec((tm, D_MODEL), row),
            pl.BlockSpec((tm, PLE_DIM), row),
            pl.BlockSpec((1, D_MODEL), full),
            pl.BlockSpec((D_MODEL, D_MODEL), full),
            pl.BlockSpec((PLE_DIM, D_MODEL), full),
            pl.BlockSpec((1, D_MODEL), full),
        ],
        out_specs=pl.BlockSpec((tm, D_MODEL), row),
        out_shape=jax.ShapeDtypeStruct((t, D_MODEL), F32),
        compiler_params=_cparams("parallel"),
        name="ple_block",
    )(x, ffn, p, norm_g.reshape(1, D_MODEL), w_gate, w_proj, norm_final.reshape(1, D_MODEL))


def _prompt_groups(batch):
    sizes = []
    while sum(sizes) < batch:
        nxt = 1 if len(sizes) < 2 else -(-sizes[-1] * 7 // 5)
        sizes.append(min(nxt, batch - sum(sizes)))
    return sizes
def _trunk_layer(x, ple, lw, batch, seq, pool_prefix, pool_start, cache, final, after=None, prev_sc=None):
    if after is not None:
        x, _ = lax.optimization_barrier((x, after))
    uv = norm_matmul(x, lw["norm_mix"], lw["w_uv"], "gelu", F32, 2 * BW)
    mid = norm_matmul(x, lw["norm_mix"], lw["w_mid"], "none", F32, 4 * BW)
    gates = norm_matmul(x, lw["norm_mix"], lw["w_gates"], "sigmoid", BF16, D_MODEL)
    ya, vn = gmlp_mix(uv, lw["gmlp_ln_g"], lw["gmlp_ln_b"], lw["gmlp_ws"], lw["gmlp_bs"], min(seq, GMLP_CHUNK))
    yb, pool_state = pool_mix(mid, pool_prefix, pool_start, lw["pool_w"], lw["pool_scale"], batch, seq)
    if cache is None:
        yc = attn_prompt(mid, lw["rel_bias"], batch, seq)
    else:
        yc = attn_sample(mid, cache[0], cache[1], lw["rel_bias"], batch, seq)
    x = mixer_out(x, ya, yb, yc, gates, lw["w_branch"], lw["w_out"])
    ffn, stage, sc_out = peer_block(x, lw["norm_ffn"], lw["peer_wq"], lw["peer_subkeys"], lw["peer_table"], prev_sc)
    x = ple_block(x, ffn, ple, lw["norm_ple"], lw["ple_gate"], lw["ple_proj"], lw["norm_final"], final)
    return x, mid, pool_state, vn, stage, sc_out


def kernel(x_prompt, x_sample, cache_attn_k, cache_attn_v, state_pool, p_prompt, p_sample, norm_mix, w_in, gmlp_ln_g, gmlp_ln_b, gmlp_ws, gmlp_bs, pool_w, pool_scale, attn_rel_bias, w_branch, w_out, norm_ffn, peer_wq, peer_subkeys, peer_u, peer_v, norm_ple, ple_gate, ple_proj, norm_final):
    bp, lp, _ = x_prompt.shape
    bs, ls, _ = x_sample.shape
    assert lp % BAND_PAST == 0 and lp % GMLP_CHUNK == 0 and ls <= CHUNK
    n_keep = min(BAND_PAST, lp)
    sizes = _prompt_groups(bp)
    starts = [sum(sizes[:g]) for g in range(len(sizes))]
    xg = [x_prompt[a:a + n].reshape(n * lp, D_MODEL) for a, n in zip(starts, sizes)]
    xs = x_sample.reshape(bs * ls, D_MODEL)
    outs = {k: [] for k in ("pk", "pv", "pps", "sk", "sv", "sps", "sgv")}
    after = prev_sc = None
    for i in range(DEPTH):
        w_in_b = w_in[i].astype(BF16)
        lw = dict(
            norm_mix=norm_mix[i],
            w_uv=w_in_b[:, :2 * BW],
            w_mid=w_in_b[:, 2 * BW:6 * BW],
            w_gates=w_in_b[:, 6 * BW:],
            gmlp_ln_g=gmlp_ln_g[i], gmlp_ln_b=gmlp_ln_b[i], gmlp_ws=gmlp_ws[i], gmlp_bs=gmlp_bs[i],
            pool_w=pool_w[i], pool_scale=pool_scale[i], rel_bias=attn_rel_bias[i],
            w_branch=w_branch[i].astype(BF16), w_out=w_out[i].astype(BF16),
            norm_ffn=norm_ffn[i], peer_wq=peer_wq[i].astype(BF16),
            peer_subkeys=peer_subkeys[i].reshape(2 * PEER_HEADS, PEER_NKEYS, PEER_HALF).astype(BF16),
            peer_table=pack_expert_tables(peer_u[i], peer_v[i]),
            norm_ple=norm_ple[i], ple_gate=ple_gate[i].astype(BF16), ple_proj=ple_proj[i].astype(BF16),
            norm_final=norm_final,
        )
        final = i == DEPTH - 1
        pk, pv, pps = [], [], []
        for g, (a, bg) in enumerate(zip(starts, sizes)):
            ple_g = p_prompt[i, a:a + bg].reshape(bg * lp, PLE_DIM)
            zero_prefix = jnp.zeros((bg, POOL_STATE, BW), F32)
            xg[g], mid_p, ps_p, _, after, prev_sc = _trunk_layer(xg[g], ple_g, lw, bg, lp, zero_prefix, 0, None, final,
                                                                 after, prev_sc)
            mid_p = mid_p.reshape(bg, lp, 4 * BW)
            pk.append(mid_p[:, lp - n_keep:, 2 * BW:3 * BW].reshape(bg, n_keep, HEADS, HEAD_DIM))
            pv.append(mid_p[:, lp - n_keep:, 3 * BW:].reshape(bg, n_keep, HEADS, HEAD_DIM))
            pps.append(ps_p)
        outs["pk"].append(jnp.concatenate(pk, axis=0))
        outs["pv"].append(jnp.concatenate(pv, axis=0))
        outs["pps"].append(jnp.concatenate(pps, axis=0))
        xs, mid_s, ps_s, vn_s, _, _ = _trunk_layer(xs, p_sample[i].reshape(bs * ls, PLE_DIM), lw, bs, ls, state_pool[i],
                                                PAST_LEN, (cache_attn_k[i], cache_attn_v[i]), final)
        mid_s = mid_s.reshape(bs, ls, 4 * BW)
        outs["sk"].append(mid_s[:, :, 2 * BW:3 * BW].reshape(bs, ls, HEADS, HEAD_DIM))
        outs["sv"].append(mid_s[:, :, 3 * BW:].reshape(bs, ls, HEADS, HEAD_DIM))
        outs["sps"].append(ps_s)
        outs["sgv"].append(vn_s.reshape(bs, ls, BW))
    st = lambda k: jnp.stack(outs[k])
    y_prompt = jnp.concatenate(xg, axis=0).reshape(bp, lp, D_MODEL)
    return (y_prompt, xs.reshape(bs, ls, D_MODEL), st("pk"), st("pv"), st("pps"),
            st("sk"), st("sv"), st("sps"), st("sgv"))
```

```python
import functools
import math

import jax
import jax.numpy as jnp
import numpy as np
from jax import lax
from jax.experimental import pallas as pl
from jax.experimental.pallas import tpu as pltpu
from jax.experimental.pallas import tpu_sc as plsc

F32 = jnp.float32
BF16 = jnp.bfloat16

D_MODEL = 1024
DEPTH = 2
CHUNK = 64
EPS = 1e-6
BW = D_MODEL // 2
GMLP_CHUNK = 128
GROUPS = 4
GDIM = BW // GROUPS
POOL_WINDOWS = (2, 4, 8, 16)
POOL_STATE = 15
POOL_PAD = 16
HEADS = 8
HEAD_DIM = BW // HEADS
BAND_CHUNKS = 8
BAND_PAST = BAND_CHUNKS * CHUNK
REL_CLIP = 128
PAST_LEN = 4096
PEER_HEADS = 8
PEER_NKEYS = 128
PEER_HALF = 128
PEER_TOPK = 16
PEER_SLOTS = PEER_HEADS * PEER_TOPK
PLE_DIM = 256

LANES = 128
VMEM_LIMIT = 56 * 1024 * 1024
NEG = -1e30


def _cparams(*sem):
    return pltpu.CompilerParams(dimension_semantics=sem, vmem_limit_bytes=VMEM_LIMIT)


def _rms(x, g):
    ms = jnp.mean(x * x, axis=-1, keepdims=True)
    return x * lax.rsqrt(ms + EPS) * g


def _gelu(x):
    c = math.sqrt(2.0 / math.pi)
    return 0.5 * x * (1.0 + jnp.tanh(c * (x + 0.044715 * (x * x * x))))


def _sigmoid(x):
    return 1.0 / (1.0 + jnp.exp(-x))


_ACTS = {"gelu": _gelu, "sigmoid": _sigmoid, "none": lambda z: z}


def _norm_matmul_kernel(x_ref, g_ref, w_ref, o_ref, h_ref, *, act):
    @pl.when(pl.program_id(1) == 0)
    def _():
        h_ref[...] = _rms(x_ref[...], g_ref[...]).astype(BF16)

    z = jnp.dot(h_ref[...], w_ref[...], preferred_element_type=F32)
    o_ref[...] = _ACTS[act](z).astype(o_ref.dtype)


def norm_matmul(x, g, w, act, out_dtype, tn):
    t, d = x.shape
    n = w.shape[1]
    tm = min(t, 512)
    return pl.pallas_call(
        functools.partial(_norm_matmul_kernel, act=act),
        grid=(t // tm, n // tn),
        in_specs=[
            pl.BlockSpec((tm, d), lambda i, j: (i, 0)),
            pl.BlockSpec((1, d), lambda i, j: (0, 0)),
            pl.BlockSpec((d, tn), lambda i, j: (0, j)),
        ],
        out_specs=pl.BlockSpec((tm, tn), lambda i, j: (i, j)),
        out_shape=jax.ShapeDtypeStruct((t, n), out_dtype),
        scratch_shapes=[pltpu.VMEM((tm, d), BF16)],
        compiler_params=_cparams("parallel", "arbitrary"),
        name="norm_matmul_" + act,
    )(x, g.reshape(1, d), w)


def _gmlp_kernel(uv_ref, lng_ref, lnb_ref, ws_ref, bst_ref, y_ref, vn_ref, *, lc):
    u = uv_ref[:, :BW]
    v = uv_ref[:, BW:]
    mu = jnp.mean(v, axis=-1, keepdims=True)
    vc = v - mu
    var = jnp.mean(vc * vc, axis=-1, keepdims=True)
    vn = vc * lax.rsqrt(var + EPS) * lng_ref[...] + lnb_ref[...]
    vn_ref[...] = vn
    row = lax.broadcasted_iota(jnp.int32, (lc, lc), 0) // CHUNK
    col = lax.broadcasted_iota(jnp.int32, (lc, lc), 1) // CHUNK
    causal = col <= row
    vnb = vn.astype(BF16)
    for g in range(GROUPS):
        w = jnp.where(causal, ws_ref[g], 0.0).astype(BF16)
        s = jnp.dot(w, vnb[:, g * GDIM:(g + 1) * GDIM], preferred_element_type=F32)
        s = s + bst_ref[:, g:g + 1]
        y_ref[:, g * GDIM:(g + 1) * GDIM] = (u[:, g * GDIM:(g + 1) * GDIM] * s).astype(y_ref.dtype)


def gmlp_mix(uv, ln_g, ln_b, ws, bs, lc):
    t = uv.shape[0]
    return pl.pallas_call(
        functools.partial(_gmlp_kernel, lc=lc),
        grid=(t // lc,),
        in_specs=[
            pl.BlockSpec((lc, 2 * BW), lambda i: (i, 0)),
            pl.BlockSpec((1, BW), lambda i: (0, 0)),
            pl.BlockSpec((1, BW), lambda i: (0, 0)),
            pl.BlockSpec((GROUPS, lc, lc), lambda i: (0, 0, 0)),
            pl.BlockSpec((lc, GROUPS), lambda i: (0, 0)),
        ],
        out_specs=[
            pl.BlockSpec((lc, BW), lambda i: (i, 0)),
            pl.BlockSpec((lc, BW), lambda i: (i, 0)),
        ],
        out_shape=[
            jax.ShapeDtypeStruct((t, BW), BF16),
            jax.ShapeDtypeStruct((t, BW), F32),
        ],
        compiler_params=_cparams("parallel"),
        name="gmlp_mix",
    )(uv, ln_g.reshape(1, BW), ln_b.reshape(1, BW), ws[:, :lc, :lc], bs[:, :lc].T)


def _pool_kernel(u_ref, pre_ref, w_ref, sc_ref, y_ref, st_ref, pad_ref, *, seq, start_pos):
    pad_ref[0:POOL_PAD, :] = pre_ref[...]
    pad_ref[POOL_PAD:, :] = u_ref[...]
    pos = lax.broadcasted_iota(jnp.int32, (seq, 1), 0) + start_pos
    for g, win in enumerate(POOL_WINDOWS):
        cols = slice(g * GDIM, (g + 1) * GDIM)
        tok = pad_ref[POOL_PAD:, cols]
        acc = tok
        for k in range(1, win):
            acc = acc + pad_ref[POOL_PAD - k:POOL_PAD - k + seq, cols]
        cnt = jnp.minimum(pos + 1, win).astype(F32)
        d = acc / cnt - tok
        y = jnp.dot(d.astype(BF16), w_ref[g], preferred_element_type=F32)
        y_ref[:, cols] = (y * sc_ref[:, cols]).astype(y_ref.dtype)
    st_ref[...] = pad_ref[seq + 1:seq + POOL_PAD, :]


def pool_mix(mid, prefix, start_pos, pool_w, pool_scale, batch, seq):
    mid3 = mid.reshape(batch, seq, 4 * BW)
    pre = jnp.concatenate([jnp.zeros((batch, 1, BW), F32), prefix], axis=1)
    y, st = pl.pallas_call(
        functools.partial(_pool_kernel, seq=seq, start_pos=start_pos),
        grid=(batch,),
        in_specs=[
            pl.BlockSpec((None, seq, BW), lambda b: (b, 0, 0)),
            pl.BlockSpec((None, POOL_PAD, BW), lambda b: (b, 0, 0)),
            pl.BlockSpec((GROUPS, GDIM, GDIM), lambda b: (0, 0, 0)),
            pl.BlockSpec((1, BW), lambda b: (0, 0)),
        ],
        out_specs=[
            pl.BlockSpec((None, seq, BW), lambda b: (b, 0, 0)),
            pl.BlockSpec((None, POOL_STATE, BW), lambda b: (b, 0, 0)),
        ],
        out_shape=[
            jax.ShapeDtypeStruct((batch, seq, BW), BF16),
            jax.ShapeDtypeStruct((batch, POOL_STATE, BW), F32),
        ],
        scratch_shapes=[pltpu.VMEM((seq + POOL_PAD, BW), F32)],
        compiler_params=_cparams("parallel"),
        name="pool_mix",
    )(mid3, pre, pool_w.astype(BF16), pool_scale.reshape(1, BW))
    return y.reshape(batch * seq, BW), st


def _attn_chunks(q_ref, kcat_ref, vcat_ref, bias_ref, o_ref, *, n_chunks, cq, band, first_block):
    scale = HEAD_DIM ** -0.5
    heads_per_tile = LANES // HEAD_DIM
    lane_head = lax.broadcasted_iota(jnp.int32, (1, LANES), 1) // HEAD_DIM
    for tile in range(HEADS // heads_per_tile):
        cols = slice(tile * LANES, (tile + 1) * LANES)
        for i in range(n_chunks):
            rows = slice(i * cq, (i + 1) * cq)
            q = (q_ref[rows, cols] * scale).astype(BF16)
            k = kcat_ref[i * cq:i * cq + band, cols]
            v = vcat_ref[i * cq:i * cq + band, cols]
            out = None
            for j in range(heads_per_tile):
                own = lane_head == j
                s = lax.dot_general(jnp.where(own, q, jnp.zeros_like(q)), k, (((1,), (1,)), ((), ())),
                                    preferred_element_type=F32) + bias_ref[tile * heads_per_tile + j]
                if first_block is not None:
                    key = lax.broadcasted_iota(jnp.int32, (1, band), 1)
                    s = jnp.where(key >= first_block * (BAND_PAST - i * cq), s, NEG)
                m = jnp.max(s, axis=-1, keepdims=True)
                p = jnp.exp(s - m)
                l = jnp.sum(p, axis=-1, keepdims=True)
                o = jnp.dot(p.astype(BF16), v, preferred_element_type=F32) / l
                out = o if out is None else jnp.where(own, o, out)
            o_ref[rows, cols] = out.astype(o_ref.dtype)


def _attn_prompt_kernel(q_ref, kp_ref, ko_ref, vp_ref, vo_ref, bias_ref, o_ref, kcat_ref, vcat_ref):
    kcat_ref[0:BAND_PAST, :] = kp_ref[...].astype(BF16)
    kcat_ref[BAND_PAST:, :] = ko_ref[...].astype(BF16)
    vcat_ref[0:BAND_PAST, :] = vp_ref[...].astype(BF16)
    vcat_ref[BAND_PAST:, :] = vo_ref[...].astype(BF16)
    _attn_chunks(q_ref, kcat_ref, vcat_ref, bias_ref, o_ref, n_chunks=BAND_CHUNKS, cq=CHUNK,
                 band=BAND_PAST + CHUNK, first_block=(pl.program_id(1) == 0).astype(jnp.int32))


def _rel_bias_tile(rel_bias, qpos, kpos):
    rel = np.clip(qpos[:, None] - kpos[None, :], -REL_CLIP, REL_CLIP) + REL_CLIP
    return rel_bias[:, rel]


def attn_prompt(mid, rel_bias, batch, seq):
    mid3 = mid.reshape(batch, seq, 4 * BW)
    blk = BAND_PAST
    bias = _rel_bias_tile(rel_bias, np.arange(CHUNK), np.arange(BAND_PAST + CHUNK) - BAND_PAST)
    prev = lambda b, j: jnp.maximum(j - 1, 0)
    y = pl.pallas_call(
        _attn_prompt_kernel,
        grid=(batch, seq // blk),
        in_specs=[
            pl.BlockSpec((None, blk, BW), lambda b, j: (b, j, 1)),
            pl.BlockSpec((None, blk, BW), lambda b, j: (b, prev(b, j), 2)),
            pl.BlockSpec((None, blk, BW), lambda b, j: (b, j, 2)),
            pl.BlockSpec((None, blk, BW), lambda b, j: (b, prev(b, j), 3)),
            pl.BlockSpec((None, blk, BW), lambda b, j: (b, j, 3)),
            pl.BlockSpec((HEADS, CHUNK, BAND_PAST + CHUNK), lambda b, j: (0, 0, 0)),
        ],
        out_specs=pl.BlockSpec((None, blk, BW), lambda b, j: (b, j, 0)),
        out_shape=jax.ShapeDtypeStruct((batch, seq, BW), BF16),
        scratch_shapes=[pltpu.VMEM((2 * blk, BW), BF16), pltpu.VMEM((2 * blk, BW), BF16)],
        compiler_params=_cparams("parallel", "parallel"),
        name="attn_prompt",
    )(mid3, mid3, mid3, mid3, mid3, bias)
    return y.reshape(batch * seq, BW)


def _attn_sample_kernel(q_ref, kc_ref, kn_ref, vc_ref, vn_ref, bias_ref, o_ref, kcat_ref, vcat_ref, *, n_cache):
    kcat_ref[0:n_cache, :] = kc_ref[...].astype(BF16)
    kcat_ref[n_cache:, :] = kn_ref[...].astype(BF16)
    vcat_ref[0:n_cache, :] = vc_ref[...].astype(BF16)
    vcat_ref[n_cache:, :] = vn_ref[...].astype(BF16)
    seq = q_ref.shape[0]
    _attn_chunks(q_ref, kcat_ref, vcat_ref, bias_ref, o_ref, n_chunks=1, cq=seq, band=n_cache + seq,
                 first_block=None)


def attn_sample(mid, cache_k, cache_v, rel_bias, batch, seq):
    n_cache = cache_k.shape[1]
    assert PAST_LEN >= n_cache
    mid3 = mid.reshape(batch, seq, 4 * BW)
    ck = cache_k.reshape(batch, n_cache, BW)
    cv = cache_v.reshape(batch, n_cache, BW)
    bias = _rel_bias_tile(rel_bias, PAST_LEN + np.arange(seq), PAST_LEN - n_cache + np.arange(n_cache + seq))
    y = pl.pallas_call(
        functools.partial(_attn_sample_kernel, n_cache=n_cache),
        grid=(batch,),
        in_specs=[
            pl.BlockSpec((None, seq, BW), lambda b: (b, 0, 1)),
            pl.BlockSpec((None, n_cache, BW), lambda b: (b, 0, 0)),
            pl.BlockSpec((None, seq, BW), lambda b: (b, 0, 2)),
            pl.BlockSpec((None, n_cache, BW), lambda b: (b, 0, 0)),
            pl.BlockSpec((None, seq, BW), lambda b: (b, 0, 3)),
            pl.BlockSpec((HEADS, seq, n_cache + seq), lambda b: (0, 0, 0)),
        ],
        out_specs=pl.BlockSpec((None, seq, BW), lambda b: (b, 0, 0)),
        out_shape=jax.ShapeDtypeStruct((batch, seq, BW), BF16),
        scratch_shapes=[pltpu.VMEM((n_cache + seq, BW), BF16), pltpu.VMEM((n_cache + seq, BW), BF16)],
        compiler_params=_cparams("parallel"),
        name="attn_sample",
    )(mid3, ck, mid3, cv, mid3, bias)
    return y.reshape(batch * seq, BW)


def _mixer_out_kernel(x_ref, ya_ref, yb_ref, yc_ref, gate_ref, wb_ref, wo_ref, o_ref):
    acc = None
    for n, y_ref in enumerate((ya_ref, yb_ref, yc_ref)):
        proj = jnp.dot(y_ref[...], wb_ref[n], preferred_element_type=F32)
        term = gate_ref[:, n * D_MODEL:(n + 1) * D_MODEL].astype(F32) * proj
        acc = term if acc is None else acc + term
    o_ref[...] = x_ref[...] + jnp.dot(acc.astype(BF16), wo_ref[...], preferred_element_type=F32)


def mixer_out(x, ya, yb, yc, gates, w_branch, w_out):
    t = x.shape[0]
    tm = min(t, 512)
    row = lambda i: (i, 0)
    return pl.pallas_call(
        _mixer_out_kernel,
        grid=(t // tm,),
        in_specs=[
            pl.BlockSpec((tm, D_MODEL), row),
            pl.BlockSpec((tm, BW), row),
            pl.BlockSpec((tm, BW), row),
            pl.BlockSpec((tm, BW), row),
            pl.BlockSpec((tm, 3 * D_MODEL), row),
            pl.BlockSpec((3, BW, D_MODEL), lambda i: (0, 0, 0)),
            pl.BlockSpec((D_MODEL, D_MODEL), lambda i: (0, 0)),
        ],
        out_specs=pl.BlockSpec((tm, D_MODEL), row),
        out_shape=jax.ShapeDtypeStruct((t, D_MODEL), F32),
        compiler_params=_cparams("parallel"),
        name="mixer_out",
    )(x, ya, yb, yc, gates, w_branch, w_out)


def _extract_top(s, payload, k):
    r = float(s.shape[0])
    rows = lax.broadcasted_iota(jnp.int32, s.shape, 0).astype(F32)
    vals, pays = [], []
    for _ in range(k):
        m = jnp.max(s, axis=0, keepdims=True)
        idx = jnp.min(jnp.where(s == m, rows, r), axis=0, keepdims=True)
        sel = rows == idx
        vals.append(m)
        pays.append(idx if payload is None else jnp.max(jnp.where(sel, payload, -1.0), axis=0, keepdims=True))
        s = jnp.where(sel, -jnp.inf, s)
    return jnp.concatenate(vals, axis=0), jnp.concatenate(pays, axis=0)


def _pair_candidates(sv, si):
    k = PEER_TOPK
    sub = 8
    assert k == 2 * sub
    b_row = lax.broadcasted_iota(jnp.int32, (sub, LANES), 0)
    vals = [sv[0][0:1] + sv[1], sv[0][1:2] + sv[1][0:sub]]
    ids = [si[0][0:1] * PEER_NKEYS + si[1], si[0][1:2] * PEER_NKEYS + si[1][0:sub]]
    for a in range(2, sub):
        keep = b_row < k // (a + 1)
        vals.append(jnp.where(keep, sv[0][a:a + 1] + sv[1][0:sub], -jnp.inf))
        ids.append(si[0][a:a + 1] * PEER_NKEYS + si[1][0:sub])
    vals.append(sv[0][sub:k] + sv[1][0:1])
    ids.append(si[0][sub:k] * PEER_NKEYS + si[1][0:1])
    return jnp.concatenate(vals, axis=0), jnp.concatenate(ids, axis=0)


def _peer_query_kernel(x_ref, g_ref, wq_ref, sk_ref, hp_ref, idx_ref, gate_ref, q_ref, *, tm):
    hb = _rms(x_ref[...], g_ref[...]).astype(BF16)
    q_ref[...] = jnp.dot(hb, wq_ref[...], preferred_element_type=F32).astype(BF16)
    bits = lax.bitcast_convert_type(hb.astype(F32), jnp.int32)
    half = D_MODEL // 2
    hp_ref[...] = bits[:, half:] | lax.shift_right_logical(bits[:, :half], 16)

    def sub_block(sb, carry):
        tok = pl.ds(pl.multiple_of(sb * LANES, LANES), LANES)
        for hd in range(PEER_HEADS):
            sv, si = [], []
            for p in range(2):
                hp = hd * 2 + p
                q = q_ref[tok, hp * PEER_HALF:(hp + 1) * PEER_HALF]
                s = lax.dot_general(sk_ref[hp], q, (((1,), (1,)), ((), ())), preferred_element_type=F32)
                v, i = _extract_top(s, None, PEER_TOPK)
                sv.append(v)
                si.append(i)
            cand, eid = _pair_candidates(sv, si)
            tv, te = _extract_top(cand, eid, PEER_TOPK)
            e = jnp.exp(tv - tv[0:1])
            gate = e / jnp.sum(e, axis=0, keepdims=True)
            idx_ref[sb, hd * PEER_TOPK:(hd + 1) * PEER_TOPK, :] = te.astype(jnp.int32)
            gate_ref[sb, hd * PEER_TOPK:(hd + 1) * PEER_TOPK, :] = gate
        return carry

    lax.fori_loop(0, tm // LANES, sub_block, 0)


def peer_query(x, norm_g, wq, subkeys):
    t = x.shape[0]
    tm = min(t, 512)
    nq = wq.shape[1]
    nsb = tm // LANES
    return pl.pallas_call(
        functools.partial(_peer_query_kernel, tm=tm),
        grid=(t // tm,),
        in_specs=[
            pl.BlockSpec((tm, D_MODEL), lambda i: (i, 0)),
            pl.BlockSpec((1, D_MODEL), lambda i: (0, 0)),
            pl.BlockSpec((D_MODEL, nq), lambda i: (0, 0)),
            pl.BlockSpec((2 * PEER_HEADS, PEER_NKEYS, PEER_HALF), lambda i: (0, 0, 0)),
        ],
        out_specs=[
            pl.BlockSpec((tm, D_MODEL // 2), lambda i: (i, 0)),
            pl.BlockSpec((nsb, PEER_SLOTS, LANES), lambda i: (i, 0, 0)),
            pl.BlockSpec((nsb, PEER_SLOTS, LANES), lambda i: (i, 0, 0)),
        ],
        out_shape=[
            jax.ShapeDtypeStruct((t, D_MODEL // 2), jnp.int32),
            jax.ShapeDtypeStruct((t // LANES, PEER_SLOTS, LANES), jnp.int32),
            jax.ShapeDtypeStruct((t // LANES, PEER_SLOTS, LANES), F32),
        ],
        scratch_shapes=[pltpu.VMEM((tm, nq), BF16)],
        compiler_params=_cparams("parallel"),
        name="peer_query",
    )(x, norm_g.reshape(1, D_MODEL), wq, subkeys)


HALF_D = D_MODEL // 2
SC_LANES = 16
SC_UNIT_ROWS = 32
SC_UNITS_PER_TOKEN = PEER_SLOTS // SC_UNIT_ROWS
SC_MAX_TOKEN_BLOCK = 32
SC_ROW_GROUP = 8
SC_HALF_VECS = HALF_D // SC_LANES
SC_OUT_VECS = 8
SC_BF16_TERMS = 4
SC_ROW_TILES = D_MODEL // LANES


def pack_expert_tables(peer_u, peer_v):
    def pack(x):
        b = lax.bitcast_convert_type(x.astype(BF16), jnp.uint16).astype(jnp.uint32)
        return (b[:, HALF_D:] << 16) | b[:, :HALF_D]

    rows = lax.bitcast_convert_type(jnp.concatenate([pack(peer_u), pack(peer_v)], axis=1), jnp.int32)
    return rows.reshape(rows.shape[0], SC_ROW_TILES, LANES)


def sc_peer_experts(table, idx, gates, hp):
    t = hp.shape[0]
    info = plsc.get_sparse_core_info()
    n_workers = info.num_cores * info.num_subcores
    tpw = t // n_workers
    token_block = min(tpw, SC_MAX_TOKEN_BLOCK)
    assert t % (n_workers * token_block) == 0
    units = token_block * SC_UNITS_PER_TOKEN
    mesh = plsc.VectorSubcoreMesh(core_axis_name="core", subcore_axis_name="subcore")
    hi_mask = jnp.int32(-65536)
    gelu_c = math.sqrt(2.0 / math.pi)

    @functools.partial(
        pl.kernel,
        out_type=jax.ShapeDtypeStruct((t, D_MODEL), F32),
        mesh=mesh,
        scratch_types=[
            pltpu.VMEM((token_block * PEER_SLOTS,), jnp.int32),
            pltpu.VMEM((token_block * PEER_SLOTS,), F32),
            pltpu.VMEM((token_block, HALF_D), jnp.int32),
            pltpu.VMEM((token_block, D_MODEL), F32),
            pltpu.VMEM((2, SC_UNIT_ROWS, SC_ROW_TILES, LANES), jnp.int32),
            pltpu.VMEM((SC_UNIT_ROWS, SC_LANES), F32),
            pltpu.VMEM((SC_UNIT_ROWS,), jnp.int32),
            pltpu.SemaphoreType.DMA((2,)),
        ],
        compiler_params=pltpu.CompilerParams(needs_layout_passes=False),
        name="peer_sc_experts",
    )
    def kern(table_hbm, idx_hbm, gate_hbm, h_hbm, out_hbm, idx_v, gate_v, h_v, out_v, rows_v, part_v, coef_v, sem):
        wid = lax.axis_index("subcore") * info.num_cores + lax.axis_index("core")
        lane = lax.iota(jnp.int32, SC_LANES)
        zero = jnp.zeros((SC_LANES,), F32)

        def gather(unit, b):
            rows = idx_v.at[pl.ds(unit * SC_UNIT_ROWS, SC_UNIT_ROWS)]
            return pltpu.make_async_copy(table_hbm.at[rows], rows_v.at[b], sem.at[b])

        def row_vec(b, r, vec):
            per_tile_row = LANES // SC_LANES
            lane0 = pl.multiple_of((vec % per_tile_row) * SC_LANES, SC_LANES)
            return rows_v[b, r, vec // per_tile_row, pl.ds(lane0, SC_LANES)]

        def unpack(w):
            return lax.bitcast_convert_type(w << 16, F32), lax.bitcast_convert_type(w & hi_mask, F32)

        def as_pairs(w):
            return plsc.bitcast(w, BF16)

        def compute(unit, b):
            tl = unit // SC_UNITS_PER_TOKEN
            q = unit % SC_UNITS_PER_TOKEN

            def row_group(rg, carry):
                def kstep(k, accs):
                    hs = []
                    for j in range(SC_BF16_TERMS):
                        off = pl.multiple_of((k * SC_BF16_TERMS + j) * SC_LANES, SC_LANES)
                        hs.append(as_pairs(h_v[tl, pl.ds(off, SC_LANES)]))
                    new = []
                    for r in range(SC_ROW_GROUP):
                        p = None
                        for j in range(SC_BF16_TERMS):
                            term = as_pairs(row_vec(b, rg * SC_ROW_GROUP + r, k * SC_BF16_TERMS + j)) * hs[j]
                            p = term if p is None else p + term
                        lo, hi = unpack(plsc.bitcast(p, jnp.int32))
                        new.append(accs[r] + lo + hi)
                    return tuple(new)

                accs = lax.fori_loop(0, SC_HALF_VECS // SC_BF16_TERMS, kstep, (zero,) * SC_ROW_GROUP)
                for r in range(SC_ROW_GROUP):
                    part_v[rg * SC_ROW_GROUP + r, :] = accs[r]
                return carry

            lax.fori_loop(0, SC_UNIT_ROWS // SC_ROW_GROUP, row_group, 0)

            for i in range(SC_UNIT_ROWS // SC_LANES):
                rows = lane + i * SC_LANES
                a = zero
                for l in range(SC_LANES):
                    a = a + plsc.load_gather(part_v, [rows, jnp.full((SC_LANES,), l, jnp.int32)])
                z = gelu_c * (a + 0.044715 * (a * a * a))
                act = a / (1.0 + jnp.exp(-2.0 * z))
                slot = pl.multiple_of(tl * PEER_SLOTS + q * SC_UNIT_ROWS + i * SC_LANES, SC_LANES)
                bits = lax.bitcast_convert_type(gate_v[pl.ds(slot, SC_LANES)] * act, jnp.int32)
                top = (bits + 0x7FFF + ((bits >> 16) & 1)) & hi_mask
                coef_v[pl.ds(i * SC_LANES, SC_LANES)] = top | lax.shift_right_logical(top, 16)

            def out_pass(dq, carry):
                def row_quad(rq, accs):
                    cs = []
                    for j in range(SC_BF16_TERMS):
                        row = jnp.full((SC_LANES,), rq * SC_BF16_TERMS + j, jnp.int32)
                        cs.append(as_pairs(plsc.load_gather(coef_v, [row])))
                    new = []
                    for k in range(SC_OUT_VECS):
                        p = None
                        for j in range(SC_BF16_TERMS):
                            w = row_vec(b, rq * SC_BF16_TERMS + j, SC_HALF_VECS + dq * SC_OUT_VECS + k)
                            term = as_pairs(w) * cs[j]
                            p = term if p is None else p + term
                        lo, hi = unpack(plsc.bitcast(p, jnp.int32))
                        new.append(accs[2 * k] + lo)
                        new.append(accs[2 * k + 1] + hi)
                    return tuple(new)

                accs = lax.fori_loop(0, SC_UNIT_ROWS // SC_BF16_TERMS, row_quad, (zero,) * (2 * SC_OUT_VECS))
                for k in range(SC_OUT_VECS):
                    off = pl.multiple_of((dq * SC_OUT_VECS + k) * SC_LANES, SC_LANES)
                    plsc.addupdate(out_v.at[tl, pl.ds(off, SC_LANES)], accs[2 * k])
                    plsc.addupdate(out_v.at[tl, pl.ds(HALF_D + off, SC_LANES)], accs[2 * k + 1])
                return carry

            lax.fori_loop(0, SC_HALF_VECS // SC_OUT_VECS, out_pass, 0)

        @pl.loop(0, tpw // token_block)
        def _(blk):
            tok0 = wid * tpw + blk * token_block
            slots = pl.ds(tok0 * PEER_SLOTS, token_block * PEER_SLOTS)
            pltpu.sync_copy(idx_hbm.at[slots], idx_v)
            pltpu.sync_copy(gate_hbm.at[slots], gate_v)
            pltpu.sync_copy(h_hbm.at[pl.ds(tok0, token_block)], h_v)

            @pl.loop(0, token_block)
            def _(tl):
                @pl.loop(0, D_MODEL // SC_LANES)
                def _(k):
                    out_v[tl, pl.ds(pl.multiple_of(k * SC_LANES, SC_LANES), SC_LANES)] = zero

            gather(0, 0).start()

            @pl.loop(0, units, step=2)
            def _(u0):
                for b in range(2):
                    unit = u0 + b

                    @pl.when(unit + 1 < units)
                    def _():
                        gather(unit + 1, 1 - b).start()

                    gather(unit, b).wait()
                    compute(unit, b)

            pltpu.sync_copy(out_v, out_hbm.at[pl.ds(tok0, token_block)])

    return kern(table, idx, gates, hp)


TC_PEER_TOKENS = 8
TC_PEER_BUFFERS = 4
U_TILE_ROWS = SC_ROW_TILES // 2


def _tc_peer_kernel(idx_ref, idx_next_ref, gate_ref, hp_ref, table_hbm, o_ref, rows_buf, a_buf, c_buf, sem):
    step = pl.program_id(0)
    hi_mask = jnp.int32(-65536)

    def unpack(w):
        return lax.bitcast_convert_type(w << 16, F32), lax.bitcast_convert_type(w & hi_mask, F32)

    def row_copy(ids_ref, tok, e, slot):
        return pltpu.make_async_copy(table_hbm.at[ids_ref[tok * PEER_SLOTS + e]], rows_buf.at[slot, e], sem.at[slot])

    def wait_rows(slot):
        pltpu.make_async_copy(table_hbm.at[pl.ds(0, PEER_SLOTS)], rows_buf.at[slot], sem.at[slot]).wait()

    ahead = TC_PEER_BUFFERS - 1

    @pl.when(step == 0)
    def _():
        for tok in range(ahead):
            def first(e, c, tok=tok):
                row_copy(idx_ref, tok, e, tok).start()
                return c

            lax.fori_loop(0, PEER_SLOTS, first, 0, unroll=8)

    gates_t = gate_ref[...].T
    zeros_v = jnp.zeros((SC_ROW_TILES - U_TILE_ROWS, LANES), F32)
    for t in range(TC_PEER_TOKENS):
        slot = t % TC_PEER_BUFFERS
        nxt = t + ahead
        nxt_slot = nxt % TC_PEER_BUFFERS
        wait_rows(slot)
        h_lo, h_hi = unpack(hp_ref[t])
        h_lo = jnp.concatenate([h_lo, zeros_v], axis=0)
        h_hi = jnp.concatenate([h_hi, zeros_v], axis=0)

        def u_body(e, c, slot=slot, nxt=nxt, nxt_slot=nxt_slot, h_lo=h_lo, h_hi=h_hi):
            if nxt >= TC_PEER_TOKENS:
                @pl.when(step + 1 < pl.num_programs(0))
                def _():
                    row_copy(idx_next_ref, nxt - TC_PEER_TOKENS, e, nxt_slot).start()
            else:
                row_copy(idx_ref, nxt, e, nxt_slot).start()
            lo, hi = unpack(rows_buf[slot, e])
            a_buf[pl.ds(e, 1), :] = jnp.sum(lo * h_lo + hi * h_hi, axis=0, keepdims=True)
            return c

        lax.fori_loop(0, PEER_SLOTS, u_body, 0, unroll=8)
        a = jnp.sum(a_buf[...], axis=1, keepdims=True)
        c_buf[...] = jnp.broadcast_to(gates_t[:, t:t + 1] * _gelu(a), (PEER_SLOTS, LANES))

        def v_body(e, acc, slot=slot):
            lo, hi = unpack(rows_buf[slot, e])
            c = c_buf[pl.ds(e, 1), :]
            return acc[0] + c * lo, acc[1] + c * hi

        zero = jnp.zeros((SC_ROW_TILES, LANES), F32)
        acc_lo, acc_hi = lax.fori_loop(0, PEER_SLOTS, v_body, (zero, zero), unroll=8)
        o_ref[t, 0:U_TILE_ROWS, :] = acc_lo[U_TILE_ROWS:, :]
        o_ref[t, U_TILE_ROWS:, :] = acc_hi[U_TILE_ROWS:, :]


def tc_peer_experts(table, idx, gates, hp):
    t = hp.shape[0]
    assert t % TC_PEER_TOKENS == 0 and TC_PEER_TOKENS % TC_PEER_BUFFERS == 0
    n_steps = t // TC_PEER_TOKENS
    ids = TC_PEER_TOKENS * PEER_SLOTS
    smem = functools.partial(pl.BlockSpec, memory_space=pltpu.SMEM)
    out = pl.pallas_call(
        _tc_peer_kernel,
        grid=(n_steps,),
        in_specs=[
            smem((ids,), lambda i: (i,)),
            smem((ids,), lambda i: (jnp.minimum(i + 1, n_steps - 1),)),
            pl.BlockSpec((TC_PEER_TOKENS, PEER_SLOTS), lambda i: (i, 0)),
            pl.BlockSpec((TC_PEER_TOKENS, U_TILE_ROWS, LANES), lambda i: (i, 0, 0)),
            pl.BlockSpec(memory_space=pl.ANY),
        ],
        out_specs=pl.BlockSpec((TC_PEER_TOKENS, SC_ROW_TILES, LANES), lambda i: (i, 0, 0)),
        out_shape=jax.ShapeDtypeStruct((t, SC_ROW_TILES, LANES), F32),
        scratch_shapes=[
            pltpu.VMEM((TC_PEER_BUFFERS, PEER_SLOTS, SC_ROW_TILES, LANES), jnp.int32),
            pltpu.VMEM((PEER_SLOTS, LANES), F32),
            pltpu.VMEM((PEER_SLOTS, LANES), F32),
            pltpu.SemaphoreType.DMA((TC_PEER_BUFFERS,)),
        ],
        compiler_params=pltpu.CompilerParams(dimension_semantics=("arbitrary",), vmem_limit_bytes=VMEM_LIMIT,
                                             disable_bounds_checks=True),
        name="tc_peer_experts",
    )(idx, idx, gates, hp.reshape(t, U_TILE_ROWS, LANES), table)
    return out.reshape(t, D_MODEL)


TC_PEER_SHARE = 8


def peer_block(x, norm_g, wq, subkeys, table, prev_sc=None):
    t = x.shape[0]
    hp, idx3, gate3 = peer_query(x, norm_g, wq, subkeys)
    idx = jnp.transpose(idx3, (0, 2, 1)).reshape(t * PEER_SLOTS)
    gates = jnp.transpose(gate3, (0, 2, 1)).reshape(t * PEER_SLOTS)
    info = plsc.get_sparse_core_info()
    sc_round = info.num_cores * info.num_subcores * SC_MAX_TOKEN_BLOCK
    t_tc = (t // TC_PEER_SHARE) // sc_round * sc_round
    t_sc = t - t_tc
    idx_sc = idx[:t_sc * PEER_SLOTS]
    if prev_sc is not None:
        idx_sc, _ = lax.optimization_barrier((idx_sc, prev_sc))
    on_sc = sc_peer_experts(table, idx_sc, gates[:t_sc * PEER_SLOTS], hp[:t_sc])
    if t_tc == 0:
        return on_sc, (hp,), on_sc
    on_tc = tc_peer_experts(table, idx[t_sc * PEER_SLOTS:], gates[t_sc * PEER_SLOTS:].reshape(t_tc, PEER_SLOTS),
                            hp[t_sc:])
    return jnp.concatenate([on_sc, on_tc], axis=0), (hp, on_tc), on_sc


def _ple_kernel(x_ref, ffn_ref, p_ref, g_ref, wg_ref, wp_ref, gf_ref, o_ref, *, final):
    x = x_ref[...] + ffn_ref[...]
    gate = _sigmoid(jnp.dot(_rms(x, g_ref[...]).astype(BF16), wg_ref[...], preferred_element_type=F32))
    emb = jnp.dot(p_ref[...].astype(BF16), wp_ref[...], preferred_element_type=F32)
    y = x + gate * emb
    o_ref[...] = _rms(y, gf_ref[...]) if final else y


def ple_block(x, ffn, p, norm_g, w_gate, w_proj, norm_final, final):
    t = x.shape[0]
    tm = min(t, 512)
    row = lambda i: (i, 0)
    full = lambda i: (0, 0)
    return pl.pallas_call(
        functools.partial(_ple_kernel, final=final),
        grid=(t // tm,),
        in_specs=[
            pl.BlockSpec((tm, D_MODEL), row),
            pl.BlockSpec((tm, D_MODEL), row),
            pl.BlockSpec((tm, PLE_DIM), row),
            pl.BlockSpec((1, D_MODEL), full),
            pl.BlockSpec((D_MODEL, D_MODEL), full),
            pl.BlockSpec((PLE_DIM, D_MODEL), full),
            pl.BlockSpec((1, D_MODEL), full),
        ],
        out_specs=pl.BlockSpec((tm, D_MODEL), row),
        out_shape=jax.ShapeDtypeStruct((t, D_MODEL), F32),
        compiler_params=_cparams("parallel"),
        name="ple_block",
    )(x, ffn, p, norm_g.reshape(1, D_MODEL), w_gate, w_proj, norm_final.reshape(1, D_MODEL))


def _prompt_groups(batch):
    sizes = []
    while sum(sizes) < batch:
        nxt = 1 if len(sizes) < 2 else -(-sizes[-1] * 7 // 5)
        sizes.append(min(nxt, batch - sum(sizes)))
    return sizes
def _trunk_layer(x, ple, lw, batch, seq, pool_prefix, pool_start, cache, final, after=None, prev_sc=None):
    if after is not None:
        x, _ = lax.optimization_barrier((x, after))
    uv = norm_matmul(x, lw["norm_mix"], lw["w_uv"], "gelu", F32, 2 * BW)
    mid = norm_matmul(x, lw["norm_mix"], lw["w_mid"], "none", F32, 4 * BW)
    gates = norm_matmul(x, lw["norm_mix"], lw["w_gates"], "sigmoid", BF16, D_MODEL)
    ya, vn = gmlp_mix(uv, lw["gmlp_ln_g"], lw["gmlp_ln_b"], lw["gmlp_ws"], lw["gmlp_bs"], min(seq, GMLP_CHUNK))
    yb, pool_state = pool_mix(mid, pool_prefix, pool_start, lw["pool_w"], lw["pool_scale"], batch, seq)
    if cache is None:
        yc = attn_prompt(mid, lw["rel_bias"], batch, seq)
    else:
        yc = attn_sample(mid, cache[0], cache[1], lw["rel_bias"], batch, seq)
    x = mixer_out(x, ya, yb, yc, gates, lw["w_branch"], lw["w_out"])
    ffn, stage, sc_out = peer_block(x, lw["norm_ffn"], lw["peer_wq"], lw["peer_subkeys"], lw["peer_table"], prev_sc)
    x = ple_block(x, ffn, ple, lw["norm_ple"], lw["ple_gate"], lw["ple_proj"], lw["norm_final"], final)
    return x, mid, pool_state, vn, stage, sc_out


def kernel(x_prompt, x_sample, cache_attn_k, cache_attn_v, state_pool, p_prompt, p_sample, norm_mix, w_in, gmlp_ln_g, gmlp_ln_b, gmlp_ws, gmlp_bs, pool_w, pool_scale, attn_rel_bias, w_branch, w_out, norm_ffn, peer_wq, peer_subkeys, peer_u, peer_v, norm_ple, ple_gate, ple_proj, norm_final):
    bp, lp, _ = x_prompt.shape
    bs, ls, _ = x_sample.shape
    assert lp % BAND_PAST == 0 and lp % GMLP_CHUNK == 0 and ls <= CHUNK
    n_keep = min(BAND_PAST, lp)
    sizes = _prompt_groups(bp)
    starts = [sum(sizes[:g]) for g in range(len(sizes))]
    xg = [x_prompt[a:a + n].reshape(n * lp, D_MODEL) for a, n in zip(starts, sizes)]
    xs = x_sample.reshape(bs * ls, D_MODEL)
    outs = {k: [] for k in ("pk", "pv", "pps", "sk", "sv", "sps", "sgv")}
    after = prev_sc = None
    stages = []
    for i in range(DEPTH):
        tab_u, tab_v = peer_u[i], peer_v[i]
        if stages:
            tab_u, tab_v, _ = lax.optimization_barrier((tab_u, tab_v, stages[len(stages) // 2]))
        stages = []
        w_in_b = w_in[i].astype(BF16)
        lw = dict(
            norm_mix=norm_mix[i],
            w_uv=w_in_b[:, :2 * BW],
            w_mid=w_in_b[:, 2 * BW:6 * BW],
            w_gates=w_in_b[:, 6 * BW:],
            gmlp_ln_g=gmlp_ln_g[i], gmlp_ln_b=gmlp_ln_b[i], gmlp_ws=gmlp_ws[i], gmlp_bs=gmlp_bs[i],
            pool_w=pool_w[i], pool_scale=pool_scale[i], rel_bias=attn_rel_bias[i],
            w_branch=w_branch[i].astype(BF16), w_out=w_out[i].astype(BF16),
            norm_ffn=norm_ffn[i], peer_wq=peer_wq[i].astype(BF16),
            peer_subkeys=peer_subkeys[i].reshape(2 * PEER_HEADS, PEER_NKEYS, PEER_HALF).astype(BF16),
            peer_table=pack_expert_tables(tab_u, tab_v),
            norm_ple=norm_ple[i], ple_gate=ple_gate[i].astype(BF16), ple_proj=ple_proj[i].astype(BF16),
            norm_final=norm_final,
        )
        final = i == DEPTH - 1
        pk, pv, pps = [], [], []
        for g, (a, bg) in enumerate(zip(starts, sizes)):
            ple_g = p_prompt[i, a:a + bg].reshape(bg * lp, PLE_DIM)
            zero_prefix = jnp.zeros((bg, POOL_STATE, BW), F32)
            xg[g], mid_p, ps_p, _, after, prev_sc = _trunk_layer(xg[g], ple_g, lw, bg, lp, zero_prefix, 0, None, final,
                                                                 after, prev_sc)
            stages.append(after)
            mid_p = mid_p.reshape(bg, lp, 4 * BW)
            pk.append(mid_p[:, lp - n_keep:, 2 * BW:3 * BW].reshape(bg, n_keep, HEADS, HEAD_DIM))
            pv.append(mid_p[:, lp - n_keep:, 3 * BW:].reshape(bg, n_keep, HEADS, HEAD_DIM))
            pps.append(ps_p)
        outs["pk"].append(jnp.concatenate(pk, axis=0))
        outs["pv"].append(jnp.concatenate(pv, axis=0))
        outs["pps"].append(jnp.concatenate(pps, axis=0))
        xs, mid_s, ps_s, vn_s, _, _ = _trunk_layer(xs, p_sample[i].reshape(bs * ls, PLE_DIM), lw, bs, ls, state_pool[i],
                                                PAST_LEN, (cache_attn_k[i], cache_attn_v[i]), final)
        mid_s = mid_s.reshape(bs, ls, 4 * BW)
        outs["sk"].append(mid_s[:, :, 2 * BW:3 * BW].reshape(bs, ls, HEADS, HEAD_DIM))
        outs["sv"].append(mid_s[:, :, 3 * BW:].reshape(bs, ls, HEADS, HEAD_DIM))
        outs["sps"].append(ps_s)
        outs["sgv"].append(vn_s.reshape(bs, ls, BW))
    st = lambda k: jnp.stack(outs[k])
    y_prompt = jnp.concatenate(xg, axis=0).reshape(bp, lp, D_MODEL)
    return (y_prompt, xs.reshape(bs, ls, D_MODEL), st("pk"), st("pv"), st("pps"),
            st("sk"), st("sv"), st("sps"), st("sgv"))
```

```python
import functools
import math

import jax
import jax.numpy as jnp
import numpy as np
from jax import lax
from jax.experimental import pallas as pl
from jax.experimental.pallas import tpu as pltpu
from jax.experimental.pallas import tpu_sc as plsc

F32 = jnp.float32
BF16 = jnp.bfloat16

D_MODEL = 1024
DEPTH = 2
CHUNK = 64
EPS = 1e-6
BW = D_MODEL // 2
GMLP_CHUNK = 128
GROUPS = 4
GDIM = BW // GROUPS
POOL_WINDOWS = (2, 4, 8, 16)
POOL_STATE = 15
POOL_PAD = 16
HEADS = 8
HEAD_DIM = BW // HEADS
BAND_CHUNKS = 8
BAND_PAST = BAND_CHUNKS * CHUNK
REL_CLIP = 128
PAST_LEN = 4096
PEER_HEADS = 8
PEER_NKEYS = 128
PEER_HALF = 128
PEER_TOPK = 16
PEER_SLOTS = PEER_HEADS * PEER_TOPK
PLE_DIM = 256

LANES = 128
VMEM_LIMIT = 56 * 1024 * 1024
NEG = -1e30


def _cparams(*sem):
    return pltpu.CompilerParams(dimension_semantics=sem, vmem_limit_bytes=VMEM_LIMIT)


def _rms(x, g):
    ms = jnp.mean(x * x, axis=-1, keepdims=True)
    return x * lax.rsqrt(ms + EPS) * g


def _gelu(x):
    c = math.sqrt(2.0 / math.pi)
    return 0.5 * x * (1.0 + jnp.tanh(c * (x + 0.044715 * (x * x * x))))


def _sigmoid(x):
    return 1.0 / (1.0 + jnp.exp(-x))


_ACTS = {"gelu": _gelu, "sigmoid": _sigmoid, "none": lambda z: z}


def _norm_matmul_kernel(x_ref, g_ref, w_ref, o_ref, h_ref, *, act):
    @pl.when(pl.program_id(1) == 0)
    def _():
        h_ref[...] = _rms(x_ref[...], g_ref[...]).astype(BF16)

    z = jnp.dot(h_ref[...], w_ref[...], preferred_element_type=F32)
    o_ref[...] = _ACTS[act](z).astype(o_ref.dtype)


def norm_matmul(x, g, w, act, out_dtype, tn):
    t, d = x.shape
    n = w.shape[1]
    tm = min(t, 512)
    return pl.pallas_call(
        functools.partial(_norm_matmul_kernel, act=act),
        grid=(t // tm, n // tn),
        in_specs=[
            pl.BlockSpec((tm, d), lambda i, j: (i, 0)),
            pl.BlockSpec((1, d), lambda i, j: (0, 0)),
            pl.BlockSpec((d, tn), lambda i, j: (0, j)),
        ],
        out_specs=pl.BlockSpec((tm, tn), lambda i, j: (i, j)),
        out_shape=jax.ShapeDtypeStruct((t, n), out_dtype),
        scratch_shapes=[pltpu.VMEM((tm, d), BF16)],
        compiler_params=_cparams("parallel", "arbitrary"),
        name="norm_matmul_" + act,
    )(x, g.reshape(1, d), w)


def _gmlp_kernel(uv_ref, lng_ref, lnb_ref, ws_ref, bst_ref, y_ref, vn_ref, *, lc):
    u = uv_ref[:, :BW]
    v = uv_ref[:, BW:]
    mu = jnp.mean(v, axis=-1, keepdims=True)
    vc = v - mu
    var = jnp.mean(vc * vc, axis=-1, keepdims=True)
    vn = vc * lax.rsqrt(var + EPS) * lng_ref[...] + lnb_ref[...]
    vn_ref[...] = vn
    row = lax.broadcasted_iota(jnp.int32, (lc, lc), 0) // CHUNK
    col = lax.broadcasted_iota(jnp.int32, (lc, lc), 1) // CHUNK
    causal = col <= row
    vnb = vn.astype(BF16)
    for g in range(GROUPS):
        w = jnp.where(causal, ws_ref[g], 0.0).astype(BF16)
        s = jnp.dot(w, vnb[:, g * GDIM:(g + 1) * GDIM], preferred_element_type=F32)
        s = s + bst_ref[:, g:g + 1]
        y_ref[:, g * GDIM:(g + 1) * GDIM] = (u[:, g * GDIM:(g + 1) * GDIM] * s).astype(y_ref.dtype)


def gmlp_mix(uv, ln_g, ln_b, ws, bs, lc):
    t = uv.shape[0]
    return pl.pallas_call(
        functools.partial(_gmlp_kernel, lc=lc),
        grid=(t // lc,),
        in_specs=[
            pl.BlockSpec((lc, 2 * BW), lambda i: (i, 0)),
            pl.BlockSpec((1, BW), lambda i: (0, 0)),
            pl.BlockSpec((1, BW), lambda i: (0, 0)),
            pl.BlockSpec((GROUPS, lc, lc), lambda i: (0, 0, 0)),
            pl.BlockSpec((lc, GROUPS), lambda i: (0, 0)),
        ],
        out_specs=[
            pl.BlockSpec((lc, BW), lambda i: (i, 0)),
            pl.BlockSpec((lc, BW), lambda i: (i, 0)),
        ],
        out_shape=[
            jax.ShapeDtypeStruct((t, BW), BF16),
            jax.ShapeDtypeStruct((t, BW), F32),
        ],
        compiler_params=_cparams("parallel"),
        name="gmlp_mix",
    )(uv, ln_g.reshape(1, BW), ln_b.reshape(1, BW), ws[:, :lc, :lc], bs[:, :lc].T)


def _pool_kernel(u_ref, pre_ref, w_ref, sc_ref, y_ref, st_ref, pad_ref, *, seq, start_pos):
    pad_ref[0:POOL_PAD, :] = pre_ref[...]
    pad_ref[POOL_PAD:, :] = u_ref[...]
    pos = lax.broadcasted_iota(jnp.int32, (seq, 1), 0) + start_pos
    for g, win in enumerate(POOL_WINDOWS):
        cols = slice(g * GDIM, (g + 1) * GDIM)
        tok = pad_ref[POOL_PAD:, cols]
        acc = tok
        for k in range(1, win):
            acc = acc + pad_ref[POOL_PAD - k:POOL_PAD - k + seq, cols]
        cnt = jnp.minimum(pos + 1, win).astype(F32)
        d = acc / cnt - tok
        y = jnp.dot(d.astype(BF16), w_ref[g], preferred_element_type=F32)
        y_ref[:, cols] = (y * sc_ref[:, cols]).astype(y_ref.dtype)
    st_ref[...] = pad_ref[seq + 1:seq + POOL_PAD, :]


def pool_mix(mid, prefix, start_pos, pool_w, pool_scale, batch, seq):
    mid3 = mid.reshape(batch, seq, 4 * BW)
    pre = jnp.concatenate([jnp.zeros((batch, 1, BW), F32), prefix], axis=1)
    y, st = pl.pallas_call(
        functools.partial(_pool_kernel, seq=seq, start_pos=start_pos),
        grid=(batch,),
        in_specs=[
            pl.BlockSpec((None, seq, BW), lambda b: (b, 0, 0)),
            pl.BlockSpec((None, POOL_PAD, BW), lambda b: (b, 0, 0)),
            pl.BlockSpec((GROUPS, GDIM, GDIM), lambda b: (0, 0, 0)),
            pl.BlockSpec((1, BW), lambda b: (0, 0)),
        ],
        out_specs=[
            pl.BlockSpec((None, seq, BW), lambda b: (b, 0, 0)),
            pl.BlockSpec((None, POOL_STATE, BW), lambda b: (b, 0, 0)),
        ],
        out_shape=[
            jax.ShapeDtypeStruct((batch, seq, BW), BF16),
            jax.ShapeDtypeStruct((batch, POOL_STATE, BW), F32),
        ],
        scratch_shapes=[pltpu.VMEM((seq + POOL_PAD, BW), F32)],
        compiler_params=_cparams("parallel"),
        name="pool_mix",
    )(mid3, pre, pool_w.astype(BF16), pool_scale.reshape(1, BW))
    return y.reshape(batch * seq, BW), st


def _attn_chunks(q_ref, kcat_ref, vcat_ref, bias_ref, o_ref, *, n_chunks, cq, band, first_block):
    scale = HEAD_DIM ** -0.5
    heads_per_tile = LANES // HEAD_DIM
    lane_head = lax.broadcasted_iota(jnp.int32, (1, LANES), 1) // HEAD_DIM
    for tile in range(HEADS // heads_per_tile):
        cols = slice(tile * LANES, (tile + 1) * LANES)
        for i in range(n_chunks):
            rows = slice(i * cq, (i + 1) * cq)
            q = (q_ref[rows, cols] * scale).astype(BF16)
            k = kcat_ref[i * cq:i * cq + band, cols]
            v = vcat_ref[i * cq:i * cq + band, cols]
            out = None
            for j in range(heads_per_tile):
                own = lane_head == j
                s = lax.dot_general(jnp.where(own, q, jnp.zeros_like(q)), k, (((1,), (1,)), ((), ())),
                                    preferred_element_type=F32) + bias_ref[tile * heads_per_tile + j]
                if first_block is not None:
                    key = lax.broadcasted_iota(jnp.int32, (1, band), 1)
                    s = jnp.where(key >= first_block * (BAND_PAST - i * cq), s, NEG)
                m = jnp.max(s, axis=-1, keepdims=True)
                p = jnp.exp(s - m)
                l = jnp.sum(p, axis=-1, keepdims=True)
                o = jnp.dot(p.astype(BF16), v, preferred_element_type=F32) / l
                out = o if out is None else jnp.where(own, o, out)
            o_ref[rows, cols] = out.astype(o_ref.dtype)


def _attn_prompt_kernel(q_ref, kp_ref, ko_ref, vp_ref, vo_ref, bias_ref, o_ref, kcat_ref, vcat_ref):
    kcat_ref[0:BAND_PAST, :] = kp_ref[...].astype(BF16)
    kcat_ref[BAND_PAST:, :] = ko_ref[...].astype(BF16)
    vcat_ref[0:BAND_PAST, :] = vp_ref[...].astype(BF16)
    vcat_ref[BAND_PAST:, :] = vo_ref[...].astype(BF16)
    _attn_chunks(q_ref, kcat_ref, vcat_ref, bias_ref, o_ref, n_chunks=BAND_CHUNKS, cq=CHUNK,
                 band=BAND_PAST + CHUNK, first_block=(pl.program_id(1) == 0).astype(jnp.int32))


def _rel_bias_tile(rel_bias, qpos, kpos):
    rel = np.clip(qpos[:, None] - kpos[None, :], -REL_CLIP, REL_CLIP) + REL_CLIP
    return rel_bias[:, rel]


def attn_prompt(mid, rel_bias, batch, seq):
    mid3 = mid.reshape(batch, seq, 4 * BW)
    blk = BAND_PAST
    bias = _rel_bias_tile(rel_bias, np.arange(CHUNK), np.arange(BAND_PAST + CHUNK) - BAND_PAST)
    prev = lambda b, j: jnp.maximum(j - 1, 0)
    y = pl.pallas_call(
        _attn_prompt_kernel,
        grid=(batch, seq // blk),
        in_specs=[
            pl.BlockSpec((None, blk, BW), lambda b, j: (b, j, 1)),
            pl.BlockSpec((None, blk, BW), lambda b, j: (b, prev(b, j), 2)),
            pl.BlockSpec((None, blk, BW), lambda b, j: (b, j, 2)),
            pl.BlockSpec((None, blk, BW), lambda b, j: (b, prev(b, j), 3)),
            pl.BlockSpec((None, blk, BW), lambda b, j: (b, j, 3)),
            pl.BlockSpec((HEADS, CHUNK, BAND_PAST + CHUNK), lambda b, j: (0, 0, 0)),
        ],
        out_specs=pl.BlockSpec((None, blk, BW), lambda b, j: (b, j, 0)),
        out_shape=jax.ShapeDtypeStruct((batch, seq, BW), BF16),
        scratch_shapes=[pltpu.VMEM((2 * blk, BW), BF16), pltpu.VMEM((2 * blk, BW), BF16)],
        compiler_params=_cparams("parallel", "parallel"),
        name="attn_prompt",
    )(mid3, mid3, mid3, mid3, mid3, bias)
    return y.reshape(batch * seq, BW)


def _attn_sample_kernel(q_ref, kc_ref, kn_ref, vc_ref, vn_ref, bias_ref, o_ref, kcat_ref, vcat_ref, *, n_cache):
    kcat_ref[0:n_cache, :] = kc_ref[...].astype(BF16)
    kcat_ref[n_cache:, :] = kn_ref[...].astype(BF16)
    vcat_ref[0:n_cache, :] = vc_ref[...].astype(BF16)
    vcat_ref[n_cache:, :] = vn_ref[...].astype(BF16)
    seq = q_ref.shape[0]
    _attn_chunks(q_ref, kcat_ref, vcat_ref, bias_ref, o_ref, n_chunks=1, cq=seq, band=n_cache + seq,
                 first_block=None)


def attn_sample(mid, cache_k, cache_v, rel_bias, batch, seq):
    n_cache = cache_k.shape[1]
    assert PAST_LEN >= n_cache
    mid3 = mid.reshape(batch, seq, 4 * BW)
    ck = cache_k.reshape(batch, n_cache, BW)
    cv = cache_v.reshape(batch, n_cache, BW)
    bias = _rel_bias_tile(rel_bias, PAST_LEN + np.arange(seq), PAST_LEN - n_cache + np.arange(n_cache + seq))
    y = pl.pallas_call(
        functools.partial(_attn_sample_kernel, n_cache=n_cache),
        grid=(batch,),
        in_specs=[
            pl.BlockSpec((None, seq, BW), lambda b: (b, 0, 1)),
            pl.BlockSpec((None, n_cache, BW), lambda b: (b, 0, 0)),
            pl.BlockSpec((None, seq, BW), lambda b: (b, 0, 2)),
            pl.BlockSpec((None, n_cache, BW), lambda b: (b, 0, 0)),
            pl.BlockSpec((None, seq, BW), lambda b: (b, 0, 3)),
            pl.BlockSpec((HEADS, seq, n_cache + seq), lambda b: (0, 0, 0)),
        ],
        out_specs=pl.BlockSpec((None, seq, BW), lambda b: (b, 0, 0)),
        out_shape=jax.ShapeDtypeStruct((batch, seq, BW), BF16),
        scratch_shapes=[pltpu.VMEM((n_cache + seq, BW), BF16), pltpu.VMEM((n_cache + seq, BW), BF16)],
        compiler_params=_cparams("parallel"),
        name="attn_sample",
    )(mid3, ck, mid3, cv, mid3, bias)
    return y.reshape(batch * seq, BW)


def _mixer_out_kernel(x_ref, ya_ref, yb_ref, yc_ref, gate_ref, wb_ref, wo_ref, o_ref):
    acc = None
    for n, y_ref in enumerate((ya_ref, yb_ref, yc_ref)):
        proj = jnp.dot(y_ref[...], wb_ref[n], preferred_element_type=F32)
        term = gate_ref[:, n * D_MODEL:(n + 1) * D_MODEL].astype(F32) * proj
        acc = term if acc is None else acc + term
    o_ref[...] = x_ref[...] + jnp.dot(acc.astype(BF16), wo_ref[...], preferred_element_type=F32)


def mixer_out(x, ya, yb, yc, gates, w_branch, w_out):
    t = x.shape[0]
    tm = min(t, 512)
    row = lambda i: (i, 0)
    return pl.pallas_call(
        _mixer_out_kernel,
        grid=(t // tm,),
        in_specs=[
            pl.BlockSpec((tm, D_MODEL), row),
            pl.BlockSpec((tm, BW), row),
            pl.BlockSpec((tm, BW), row),
            pl.BlockSpec((tm, BW), row),
            pl.BlockSpec((tm, 3 * D_MODEL), row),
            pl.BlockSpec((3, BW, D_MODEL), lambda i: (0, 0, 0)),
            pl.BlockSpec((D_MODEL, D_MODEL), lambda i: (0, 0)),
        ],
        out_specs=pl.BlockSpec((tm, D_MODEL), row),
        out_shape=jax.ShapeDtypeStruct((t, D_MODEL), F32),
        compiler_params=_cparams("parallel"),
        name="mixer_out",
    )(x, ya, yb, yc, gates, w_branch, w_out)


def _extract_top(s, payload, k):
    r = float(s.shape[0])
    rows = lax.broadcasted_iota(jnp.int32, s.shape, 0).astype(F32)
    vals, pays = [], []
    for _ in range(k):
        m = jnp.max(s, axis=0, keepdims=True)
        idx = jnp.min(jnp.where(s == m, rows, r), axis=0, keepdims=True)
        sel = rows == idx
        vals.append(m)
        pays.append(idx if payload is None else jnp.max(jnp.where(sel, payload, -1.0), axis=0, keepdims=True))
        s = jnp.where(sel, -jnp.inf, s)
    return jnp.concatenate(vals, axis=0), jnp.concatenate(pays, axis=0)


def _pair_candidates(sv, si):
    k = PEER_TOPK
    sub = 8
    assert k == 2 * sub
    b_row = lax.broadcasted_iota(jnp.int32, (sub, LANES), 0)
    vals = [sv[0][0:1] + sv[1], sv[0][1:2] + sv[1][0:sub]]
    ids = [si[0][0:1] * PEER_NKEYS + si[1], si[0][1:2] * PEER_NKEYS + si[1][0:sub]]
    for a in range(2, sub):
        keep = b_row < k // (a + 1)
        vals.append(jnp.where(keep, sv[0][a:a + 1] + sv[1][0:sub], -jnp.inf))
        ids.append(si[0][a:a + 1] * PEER_NKEYS + si[1][0:sub])
    vals.append(sv[0][sub:k] + sv[1][0:1])
    ids.append(si[0][sub:k] * PEER_NKEYS + si[1][0:1])
    return jnp.concatenate(vals, axis=0), jnp.concatenate(ids, axis=0)


def _peer_query_kernel(x_ref, g_ref, wq_ref, sk_ref, hp_ref, idx_ref, gate_ref, q_ref, *, tm):
    hb = _rms(x_ref[...], g_ref[...]).astype(BF16)
    q_ref[...] = jnp.dot(hb, wq_ref[...], preferred_element_type=F32).astype(BF16)
    bits = lax.bitcast_convert_type(hb.astype(F32), jnp.int32)
    half = D_MODEL // 2
    hp_ref[...] = bits[:, half:] | lax.shift_right_logical(bits[:, :half], 16)

    def sub_block(sb, carry):
        tok = pl.ds(pl.multiple_of(sb * LANES, LANES), LANES)
        for hd in range(PEER_HEADS):
            sv, si = [], []
            for p in range(2):
                hp = hd * 2 + p
                q = q_ref[tok, hp * PEER_HALF:(hp + 1) * PEER_HALF]
                s = lax.dot_general(sk_ref[hp], q, (((1,), (1,)), ((), ())), preferred_element_type=F32)
                v, i = _extract_top(s, None, PEER_TOPK)
                sv.append(v)
                si.append(i)
            cand, eid = _pair_candidates(sv, si)
            tv, te = _extract_top(cand, eid, PEER_TOPK)
            e = jnp.exp(tv - tv[0:1])
            gate = e / jnp.sum(e, axis=0, keepdims=True)
            idx_ref[sb, hd * PEER_TOPK:(hd + 1) * PEER_TOPK, :] = te.astype(jnp.int32)
            gate_ref[sb, hd * PEER_TOPK:(hd + 1) * PEER_TOPK, :] = gate
        return carry

    lax.fori_loop(0, tm // LANES, sub_block, 0)


def peer_query(x, norm_g, wq, subkeys):
    t = x.shape[0]
    tm = min(t, 512)
    nq = wq.shape[1]
    nsb = tm // LANES
    return pl.pallas_call(
        functools.partial(_peer_query_kernel, tm=tm),
        grid=(t // tm,),
        in_specs=[
            pl.BlockSpec((tm, D_MODEL), lambda i: (i, 0)),
            pl.BlockSpec((1, D_MODEL), lambda i: (0, 0)),
            pl.BlockSpec((D_MODEL, nq), lambda i: (0, 0)),
            pl.BlockSpec((2 * PEER_HEADS, PEER_NKEYS, PEER_HALF), lambda i: (0, 0, 0)),
        ],
        out_specs=[
            pl.BlockSpec((tm, D_MODEL // 2), lambda i: (i, 0)),
            pl.BlockSpec((nsb, PEER_SLOTS, LANES), lambda i: (i, 0, 0)),
            pl.BlockSpec((nsb, PEER_SLOTS, LANES), lambda i: (i, 0, 0)),
        ],
        out_shape=[
            jax.ShapeDtypeStruct((t, D_MODEL // 2), jnp.int32),
            jax.ShapeDtypeStruct((t // LANES, PEER_SLOTS, LANES), jnp.int32),
            jax.ShapeDtypeStruct((t // LANES, PEER_SLOTS, LANES), F32),
        ],
        scratch_shapes=[pltpu.VMEM((tm, nq), BF16)],
        compiler_params=_cparams("parallel"),
        name="peer_query",
    )(x, norm_g.reshape(1, D_MODEL), wq, subkeys)


HALF_D = D_MODEL // 2
SC_LANES = 16
SC_UNIT_ROWS = 32
SC_UNITS_PER_TOKEN = PEER_SLOTS // SC_UNIT_ROWS
SC_MAX_TOKEN_BLOCK = 8
SC_BUFFERS = 3
SC_ROW_GROUP = 8
SC_HALF_VECS = HALF_D // SC_LANES
SC_OUT_VECS = 8
SC_BF16_TERMS = 4
SC_ROW_TILES = D_MODEL // LANES


def pack_expert_tables(peer_u, peer_v):
    def pack(x):
        b = lax.bitcast_convert_type(x.astype(BF16), jnp.uint16).astype(jnp.uint32)
        return (b[:, HALF_D:] << 16) | b[:, :HALF_D]

    rows = lax.bitcast_convert_type(jnp.concatenate([pack(peer_u), pack(peer_v)], axis=1), jnp.int32)
    return rows.reshape(rows.shape[0], SC_ROW_TILES, LANES)


def sc_peer_experts(table, idx, gates, hp):
    t = hp.shape[0]
    info = plsc.get_sparse_core_info()
    n_workers = info.num_cores * info.num_subcores
    tpw = t // n_workers
    token_block = min(tpw, SC_MAX_TOKEN_BLOCK)
    assert t % (n_workers * token_block) == 0
    units = token_block * SC_UNITS_PER_TOKEN
    mesh = plsc.VectorSubcoreMesh(core_axis_name="core", subcore_axis_name="subcore")
    hi_mask = jnp.int32(-65536)
    gelu_c = math.sqrt(2.0 / math.pi)

    @functools.partial(
        pl.kernel,
        out_type=jax.ShapeDtypeStruct((t, D_MODEL), F32),
        mesh=mesh,
        scratch_types=[
            pltpu.VMEM((token_block * PEER_SLOTS,), jnp.int32),
            pltpu.VMEM((token_block * PEER_SLOTS,), F32),
            pltpu.VMEM((token_block, HALF_D), jnp.int32),
            pltpu.VMEM((token_block, D_MODEL), F32),
            pltpu.VMEM((SC_BUFFERS, SC_UNIT_ROWS, SC_ROW_TILES, LANES), jnp.int32),
            pltpu.VMEM((SC_UNIT_ROWS, SC_LANES), F32),
            pltpu.VMEM((SC_UNIT_ROWS,), jnp.int32),
            pltpu.SemaphoreType.DMA((SC_BUFFERS,)),
        ],
        compiler_params=pltpu.CompilerParams(needs_layout_passes=False),
        name="peer_sc_experts",
    )
    def kern(table_hbm, idx_hbm, gate_hbm, h_hbm, out_hbm, idx_v, gate_v, h_v, out_v, rows_v, part_v, coef_v, sem):
        wid = lax.axis_index("subcore") * info.num_cores + lax.axis_index("core")
        lane = lax.iota(jnp.int32, SC_LANES)
        zero = jnp.zeros((SC_LANES,), F32)

        def gather(unit, b):
            rows = idx_v.at[pl.ds(unit * SC_UNIT_ROWS, SC_UNIT_ROWS)]
            return pltpu.make_async_copy(table_hbm.at[rows], rows_v.at[b], sem.at[b])

        def row_vec(b, r, vec):
            per_tile_row = LANES // SC_LANES
            lane0 = pl.multiple_of((vec % per_tile_row) * SC_LANES, SC_LANES)
            return rows_v[b, r, vec // per_tile_row, pl.ds(lane0, SC_LANES)]

        def unpack(w):
            return lax.bitcast_convert_type(w << 16, F32), lax.bitcast_convert_type(w & hi_mask, F32)

        def as_pairs(w):
            return plsc.bitcast(w, BF16)

        def compute(unit, b):
            tl = unit // SC_UNITS_PER_TOKEN
            q = unit % SC_UNITS_PER_TOKEN

            def row_group(rg, carry):
                def kstep(k, accs):
                    hs = []
                    for j in range(SC_BF16_TERMS):
                        off = pl.multiple_of((k * SC_BF16_TERMS + j) * SC_LANES, SC_LANES)
                        hs.append(as_pairs(h_v[tl, pl.ds(off, SC_LANES)]))
                    new = []
                    for r in range(SC_ROW_GROUP):
                        p = None
                        for j in range(SC_BF16_TERMS):
                            term = as_pairs(row_vec(b, rg * SC_ROW_GROUP + r, k * SC_BF16_TERMS + j)) * hs[j]
                            p = term if p is None else p + term
                        lo, hi = unpack(plsc.bitcast(p, jnp.int32))
                        new.append(accs[r] + lo + hi)
                    return tuple(new)

                accs = lax.fori_loop(0, SC_HALF_VECS // SC_BF16_TERMS, kstep, (zero,) * SC_ROW_GROUP)
                for r in range(SC_ROW_GROUP):
                    part_v[rg * SC_ROW_GROUP + r, :] = accs[r]
                return carry

            lax.fori_loop(0, SC_UNIT_ROWS // SC_ROW_GROUP, row_group, 0)

            for i in range(SC_UNIT_ROWS // SC_LANES):
                rows = lane + i * SC_LANES
                a = zero
                for l in range(SC_LANES):
                    a = a + plsc.load_gather(part_v, [rows, jnp.full((SC_LANES,), l, jnp.int32)])
                z = gelu_c * (a + 0.044715 * (a * a * a))
                act = a / (1.0 + jnp.exp(-2.0 * z))
                slot = pl.multiple_of(tl * PEER_SLOTS + q * SC_UNIT_ROWS + i * SC_LANES, SC_LANES)
                bits = lax.bitcast_convert_type(gate_v[pl.ds(slot, SC_LANES)] * act, jnp.int32)
                top = (bits + 0x7FFF + ((bits >> 16) & 1)) & hi_mask
                coef_v[pl.ds(i * SC_LANES, SC_LANES)] = top | lax.shift_right_logical(top, 16)

            def out_pass(dq, carry):
                def row_quad(rq, accs):
                    cs = []
                    for j in range(SC_BF16_TERMS):
                        row = jnp.full((SC_LANES,), rq * SC_BF16_TERMS + j, jnp.int32)
                        cs.append(as_pairs(plsc.load_gather(coef_v, [row])))
                    new = []
                    for k in range(SC_OUT_VECS):
                        p = None
                        for j in range(SC_BF16_TERMS):
                            w = row_vec(b, rq * SC_BF16_TERMS + j, SC_HALF_VECS + dq * SC_OUT_VECS + k)
                            term = as_pairs(w) * cs[j]
                            p = term if p is None else p + term
                        lo, hi = unpack(plsc.bitcast(p, jnp.int32))
                        new.append(accs[2 * k] + lo)
                        new.append(accs[2 * k + 1] + hi)
                    return tuple(new)

                accs = lax.fori_loop(0, SC_UNIT_ROWS // SC_BF16_TERMS, row_quad, (zero,) * (2 * SC_OUT_VECS))
                for k in range(SC_OUT_VECS):
                    off = pl.multiple_of((dq * SC_OUT_VECS + k) * SC_LANES, SC_LANES)
                    plsc.addupdate(out_v.at[tl, pl.ds(off, SC_LANES)], accs[2 * k])
                    plsc.addupdate(out_v.at[tl, pl.ds(HALF_D + off, SC_LANES)], accs[2 * k + 1])
                return carry

            lax.fori_loop(0, SC_HALF_VECS // SC_OUT_VECS, out_pass, 0)

        @pl.loop(0, tpw // token_block)
        def _(blk):
            tok0 = wid * tpw + blk * token_block
            slots = pl.ds(tok0 * PEER_SLOTS, token_block * PEER_SLOTS)
            pltpu.sync_copy(idx_hbm.at[slots], idx_v)
            pltpu.sync_copy(gate_hbm.at[slots], gate_v)
            pltpu.sync_copy(h_hbm.at[pl.ds(tok0, token_block)], h_v)

            @pl.loop(0, token_block)
            def _(tl):
                @pl.loop(0, D_MODEL // SC_LANES)
                def _(k):
                    out_v[tl, pl.ds(pl.multiple_of(k * SC_LANES, SC_LANES), SC_LANES)] = zero

            for u in range(SC_BUFFERS - 1):
                gather(u, u).start()

            @pl.loop(0, units)
            def _(unit):
                ahead = unit + SC_BUFFERS - 1

                @pl.when(ahead < units)
                def _():
                    gather(ahead, ahead % SC_BUFFERS).start()

                b = unit % SC_BUFFERS
                gather(unit, b).wait()
                compute(unit, b)

            pltpu.sync_copy(out_v, out_hbm.at[pl.ds(tok0, token_block)])

    return kern(table, idx, gates, hp)


TC_PEER_TOKENS = 8
TC_PEER_BUFFERS = 4
U_TILE_ROWS = SC_ROW_TILES // 2


def _tc_peer_kernel(idx_ref, idx_next_ref, gate_ref, hp_ref, table_hbm, o_ref, rows_buf, a_buf, c_buf, sem):
    step = pl.program_id(0)
    hi_mask = jnp.int32(-65536)

    def unpack(w):
        return lax.bitcast_convert_type(w << 16, F32), lax.bitcast_convert_type(w & hi_mask, F32)

    def row_copy(ids_ref, tok, e, slot):
        return pltpu.make_async_copy(table_hbm.at[ids_ref[tok * PEER_SLOTS + e]], rows_buf.at[slot, e], sem.at[slot])

    def wait_rows(slot):
        pltpu.make_async_copy(table_hbm.at[pl.ds(0, PEER_SLOTS)], rows_buf.at[slot], sem.at[slot]).wait()

    ahead = TC_PEER_BUFFERS - 1

    @pl.when(step == 0)
    def _():
        for tok in range(ahead):
            def first(e, c, tok=tok):
                row_copy(idx_ref, tok, e, tok).start()
                return c

            lax.fori_loop(0, PEER_SLOTS, first, 0, unroll=8)

    gates_t = gate_ref[...].T
    zeros_v = jnp.zeros((SC_ROW_TILES - U_TILE_ROWS, LANES), F32)
    for t in range(TC_PEER_TOKENS):
        slot = t % TC_PEER_BUFFERS
        nxt = t + ahead
        nxt_slot = nxt % TC_PEER_BUFFERS
        wait_rows(slot)
        h_lo, h_hi = unpack(hp_ref[t])
        h_lo = jnp.concatenate([h_lo, zeros_v], axis=0)
        h_hi = jnp.concatenate([h_hi, zeros_v], axis=0)

        def u_body(e, c, slot=slot, nxt=nxt, nxt_slot=nxt_slot, h_lo=h_lo, h_hi=h_hi):
            if nxt >= TC_PEER_TOKENS:
                @pl.when(step + 1 < pl.num_programs(0))
                def _():
                    row_copy(idx_next_ref, nxt - TC_PEER_TOKENS, e, nxt_slot).start()
            else:
                row_copy(idx_ref, nxt, e, nxt_slot).start()
            lo, hi = unpack(rows_buf[slot, e])
            a_buf[pl.ds(e, 1), :] = jnp.sum(lo * h_lo + hi * h_hi, axis=0, keepdims=True)
            return c

        lax.fori_loop(0, PEER_SLOTS, u_body, 0, unroll=8)
        a = jnp.sum(a_buf[...], axis=1, keepdims=True)
        c_buf[...] = jnp.broadcast_to(gates_t[:, t:t + 1] * _gelu(a), (PEER_SLOTS, LANES))

        def v_body(e, acc, slot=slot):
            lo, hi = unpack(rows_buf[slot, e])
            c = c_buf[pl.ds(e, 1), :]
            return acc[0] + c * lo, acc[1] + c * hi

        zero = jnp.zeros((SC_ROW_TILES, LANES), F32)
        acc_lo, acc_hi = lax.fori_loop(0, PEER_SLOTS, v_body, (zero, zero), unroll=8)
        o_ref[t, 0:U_TILE_ROWS, :] = acc_lo[U_TILE_ROWS:, :]
        o_ref[t, U_TILE_ROWS:, :] = acc_hi[U_TILE_ROWS:, :]


def tc_peer_experts(table, idx, gates, hp):
    t = hp.shape[0]
    assert t % TC_PEER_TOKENS == 0 and TC_PEER_TOKENS % TC_PEER_BUFFERS == 0
    n_steps = t // TC_PEER_TOKENS
    ids = TC_PEER_TOKENS * PEER_SLOTS
    smem = functools.partial(pl.BlockSpec, memory_space=pltpu.SMEM)
    out = pl.pallas_call(
        _tc_peer_kernel,
        grid=(n_steps,),
        in_specs=[
            smem((ids,), lambda i: (i,)),
            smem((ids,), lambda i: (jnp.minimum(i + 1, n_steps - 1),)),
            pl.BlockSpec((TC_PEER_TOKENS, PEER_SLOTS), lambda i: (i, 0)),
            pl.BlockSpec((TC_PEER_TOKENS, U_TILE_ROWS, LANES), lambda i: (i, 0, 0)),
            pl.BlockSpec(memory_space=pl.ANY),
        ],
        out_specs=pl.BlockSpec((TC_PEER_TOKENS, SC_ROW_TILES, LANES), lambda i: (i, 0, 0)),
        out_shape=jax.ShapeDtypeStruct((t, SC_ROW_TILES, LANES), F32),
        scratch_shapes=[
            pltpu.VMEM((TC_PEER_BUFFERS, PEER_SLOTS, SC_ROW_TILES, LANES), jnp.int32),
            pltpu.VMEM((PEER_SLOTS, LANES), F32),
            pltpu.VMEM((PEER_SLOTS, LANES), F32),
            pltpu.SemaphoreType.DMA((TC_PEER_BUFFERS,)),
        ],
        compiler_params=pltpu.CompilerParams(dimension_semantics=("arbitrary",), vmem_limit_bytes=VMEM_LIMIT,
                                             disable_bounds_checks=True),
        name="tc_peer_experts",
    )(idx, idx, gates, hp.reshape(t, U_TILE_ROWS, LANES), table)
    return out.reshape(t, D_MODEL)


TC_PEER_SHARE = 8
TC_PEER_QUANTUM = 1024


def peer_block(x, norm_g, wq, subkeys, table, prev_sc=None):
    t = x.shape[0]
    hp, idx3, gate3 = peer_query(x, norm_g, wq, subkeys)
    idx = jnp.transpose(idx3, (0, 2, 1)).reshape(t * PEER_SLOTS)
    gates = jnp.transpose(gate3, (0, 2, 1)).reshape(t * PEER_SLOTS)
    t_tc = (t // TC_PEER_SHARE) // TC_PEER_QUANTUM * TC_PEER_QUANTUM
    t_sc = t - t_tc
    idx_sc = idx[:t_sc * PEER_SLOTS]
    if prev_sc is not None:
        idx_sc, _ = lax.optimization_barrier((idx_sc, prev_sc))
    on_sc = sc_peer_experts(table, idx_sc, gates[:t_sc * PEER_SLOTS], hp[:t_sc])
    if t_tc == 0:
        return on_sc, (hp,), on_sc
    on_tc = tc_peer_experts(table, idx[t_sc * PEER_SLOTS:], gates[t_sc * PEER_SLOTS:].reshape(t_tc, PEER_SLOTS),
                            hp[t_sc:])
    return jnp.concatenate([on_sc, on_tc], axis=0), (hp, on_tc), on_sc


def _ple_kernel(x_ref, ffn_ref, p_ref, g_ref, wg_ref, wp_ref, gf_ref, o_ref, *, final):
    x = x_ref[...] + ffn_ref[...]
    gate = _sigmoid(jnp.dot(_rms(x, g_ref[...]).astype(BF16), wg_ref[...], preferred_element_type=F32))
    emb = jnp.dot(p_ref[...].astype(BF16), wp_ref[...], preferred_element_type=F32)
    y = x + gate * emb
    o_ref[...] = _rms(y, gf_ref[...]) if final else y


def ple_block(x, ffn, p, norm_g, w_gate, w_proj, norm_final, final):
    t = x.shape[0]
    tm = min(t, 512)
    row = lambda i: (i, 0)
    full = lambda i: (0, 0)
    return pl.pallas_call(
        functools.partial(_ple_kernel, final=final),
        grid=(t // tm,),
        in_specs=[
            pl.BlockSpec((tm, D_MODEL), row),
            pl.BlockSpec((tm, D_MODEL), row),
            pl.BlockSpec((tm, PLE_DIM), row),
            pl.BlockSpec((1, D_MODEL), full),
            pl.BlockSpec((D_MODEL, D_MODEL), full),
            pl.BlockSpec((PLE_DIM, D_MODEL), full),
            pl.BlockSpec((1, D_MODEL), full),
        ],
        out_specs=pl.BlockSpec((tm, D_MODEL), row),
        out_shape=jax.ShapeDtypeStruct((t, D_MODEL), F32),
        compiler_params=_cparams("parallel"),
        name="ple_block",
    )(x, ffn, p, norm_g.reshape(1, D_MODEL), w_gate, w_proj, norm_final.reshape(1, D_MODEL))


def _prompt_groups(batch):
    sizes = []
    while sum(sizes) < batch:
        nxt = 1 if len(sizes) < 2 else -(-sizes[-1] * 7 // 5)
        sizes.append(min(nxt, batch - sum(sizes)))
    return sizes
def _trunk_layer(x, ple, lw, batch, seq, pool_prefix, pool_start, cache, final, after=None, prev_sc=None):
    if after is not None:
        x, _ = lax.optimization_barrier((x, after))
    uv = norm_matmul(x, lw["norm_mix"], lw["w_uv"], "gelu", F32, 2 * BW)
    mid = norm_matmul(x, lw["norm_mix"], lw["w_mid"], "none", F32, 4 * BW)
    gates = norm_matmul(x, lw["norm_mix"], lw["w_gates"], "sigmoid", BF16, D_MODEL)
    ya, vn = gmlp_mix(uv, lw["gmlp_ln_g"], lw["gmlp_ln_b"], lw["gmlp_ws"], lw["gmlp_bs"], min(seq, GMLP_CHUNK))
    yb, pool_state = pool_mix(mid, pool_prefix, pool_start, lw["pool_w"], lw["pool_scale"], batch, seq)
    if cache is None:
        yc = attn_prompt(mid, lw["rel_bias"], batch, seq)
    else:
        yc = attn_sample(mid, cache[0], cache[1], lw["rel_bias"], batch, seq)
    x = mixer_out(x, ya, yb, yc, gates, lw["w_branch"], lw["w_out"])
    ffn, stage, sc_out = peer_block(x, lw["norm_ffn"], lw["peer_wq"], lw["peer_subkeys"], lw["peer_table"], prev_sc)
    x = ple_block(x, ffn, ple, lw["norm_ple"], lw["ple_gate"], lw["ple_proj"], lw["norm_final"], final)
    return x, mid, pool_state, vn, stage, sc_out


def kernel(x_prompt, x_sample, cache_attn_k, cache_attn_v, state_pool, p_prompt, p_sample, norm_mix, w_in, gmlp_ln_g, gmlp_ln_b, gmlp_ws, gmlp_bs, pool_w, pool_scale, attn_rel_bias, w_branch, w_out, norm_ffn, peer_wq, peer_subkeys, peer_u, peer_v, norm_ple, ple_gate, ple_proj, norm_final):
    bp, lp, _ = x_prompt.shape
    bs, ls, _ = x_sample.shape
    assert lp % BAND_PAST == 0 and lp % GMLP_CHUNK == 0 and ls <= CHUNK
    n_keep = min(BAND_PAST, lp)
    sizes = _prompt_groups(bp)
    starts = [sum(sizes[:g]) for g in range(len(sizes))]
    xg = [x_prompt[a:a + n].reshape(n * lp, D_MODEL) for a, n in zip(starts, sizes)]
    xs = x_sample.reshape(bs * ls, D_MODEL)
    outs = {k: [] for k in ("pk", "pv", "pps", "sk", "sv", "sps", "sgv")}
    after = prev_sc = None
    stages = []
    for i in range(DEPTH):
        tab_u, tab_v = peer_u[i], peer_v[i]
        if stages:
            tab_u, tab_v, _ = lax.optimization_barrier((tab_u, tab_v, stages[len(stages) // 2]))
        stages = []
        w_in_b = w_in[i].astype(BF16)
        lw = dict(
            norm_mix=norm_mix[i],
            w_uv=w_in_b[:, :2 * BW],
            w_mid=w_in_b[:, 2 * BW:6 * BW],
            w_gates=w_in_b[:, 6 * BW:],
            gmlp_ln_g=gmlp_ln_g[i], gmlp_ln_b=gmlp_ln_b[i], gmlp_ws=gmlp_ws[i], gmlp_bs=gmlp_bs[i],
            pool_w=pool_w[i], pool_scale=pool_scale[i], rel_bias=attn_rel_bias[i],
            w_branch=w_branch[i].astype(BF16), w_out=w_out[i].astype(BF16),
            norm_ffn=norm_ffn[i], peer_wq=peer_wq[i].astype(BF16),
            peer_subkeys=peer_subkeys[i].reshape(2 * PEER_HEADS, PEER_NKEYS, PEER_HALF).astype(BF16),
            peer_table=pack_expert_tables(tab_u, tab_v),
            norm_ple=norm_ple[i], ple_gate=ple_gate[i].astype(BF16), ple_proj=ple_proj[i].astype(BF16),
            norm_final=norm_final,
        )
        final = i == DEPTH - 1
        pk, pv, pps = [], [], []
        for g, (a, bg) in enumerate(zip(starts, sizes)):
            ple_g = p_prompt[i, a:a + bg].reshape(bg * lp, PLE_DIM)
            zero_prefix = jnp.zeros((bg, POOL_STATE, BW), F32)
            xg[g], mid_p, ps_p, _, after, prev_sc = _trunk_layer(xg[g], ple_g, lw, bg, lp, zero_prefix, 0, None, final,
                                                                 after, prev_sc)
            stages.append(after)
            mid_p = mid_p.reshape(bg, lp, 4 * BW)
            pk.append(mid_p[:, lp - n_keep:, 2 * BW:3 * BW].reshape(bg, n_keep, HEADS, HEAD_DIM))
            pv.append(mid_p[:, lp - n_keep:, 3 * BW:].reshape(bg, n_keep, HEADS, HEAD_DIM))
            pps.append(ps_p)
        outs["pk"].append(jnp.concatenate(pk, axis=0))
        outs["pv"].append(jnp.concatenate(pv, axis=0))
        outs["pps"].append(jnp.concatenate(pps, axis=0))
        xs, mid_s, ps_s, vn_s, _, _ = _trunk_layer(xs, p_sample[i].reshape(bs * ls, PLE_DIM), lw, bs, ls, state_pool[i],
                                                PAST_LEN, (cache_attn_k[i], cache_attn_v[i]), final)
        mid_s = mid_s.reshape(bs, ls, 4 * BW)
        outs["sk"].append(mid_s[:, :, 2 * BW:3 * BW].reshape(bs, ls, HEADS, HEAD_DIM))
        outs["sv"].append(mid_s[:, :, 3 * BW:].reshape(bs, ls, HEADS, HEAD_DIM))
        outs["sps"].append(ps_s)
        outs["sgv"].append(vn_s.reshape(bs, ls, BW))
    st = lambda k: jnp.stack(outs[k])
    y_prompt = jnp.concatenate(xg, axis=0).reshape(bp, lp, D_MODEL)
    return (y_prompt, xs.reshape(bs, ls, D_MODEL), st("pk"), st("pv"), st("pps"),
            st("sk"), st("sv"), st("sps"), st("sgv"))
```

```python
import functools
import math

import jax
import jax.numpy as jnp
import numpy as np
from jax import lax
from jax.experimental import pallas as pl
from jax.experimental.pallas import tpu as pltpu
from jax.experimental.pallas import tpu_sc as plsc

F32 = jnp.float32
BF16 = jnp.bfloat16

D_MODEL = 1024
DEPTH = 2
CHUNK = 64
EPS = 1e-6
BW = D_MODEL // 2
GMLP_CHUNK = 128
GROUPS = 4
GDIM = BW // GROUPS
POOL_WINDOWS = (2, 4, 8, 16)
POOL_STATE = 15
POOL_PAD = 16
HEADS = 8
HEAD_DIM = BW // HEADS
BAND_CHUNKS = 8
BAND_PAST = BAND_CHUNKS * CHUNK
REL_CLIP = 128
PAST_LEN = 4096
PEER_HEADS = 8
PEER_NKEYS = 128
PEER_HALF = 128
PEER_TOPK = 16
PEER_SLOTS = PEER_HEADS * PEER_TOPK
PLE_DIM = 256

LANES = 128
VMEM_LIMIT = 56 * 1024 * 1024
NEG = -1e30


def _cparams(*sem):
    return pltpu.CompilerParams(dimension_semantics=sem, vmem_limit_bytes=VMEM_LIMIT)


def _rms(x, g):
    ms = jnp.mean(x * x, axis=-1, keepdims=True)
    return x * lax.rsqrt(ms + EPS) * g


def _gelu(x):
    c = math.sqrt(2.0 / math.pi)
    return 0.5 * x * (1.0 + jnp.tanh(c * (x + 0.044715 * (x * x * x))))


def _sigmoid(x):
    return 1.0 / (1.0 + jnp.exp(-x))


_ACTS = {"gelu": _gelu, "sigmoid": _sigmoid, "none": lambda z: z}


def _norm_matmul_kernel(x_ref, g_ref, w_ref, o_ref, h_ref, *, act):
    @pl.when(pl.program_id(1) == 0)
    def _():
        h_ref[...] = _rms(x_ref[...], g_ref[...]).astype(BF16)

    z = jnp.dot(h_ref[...], w_ref[...], preferred_element_type=F32)
    o_ref[...] = _ACTS[act](z).astype(o_ref.dtype)


def norm_matmul(x, g, w, act, out_dtype, tn):
    t, d = x.shape
    n = w.shape[1]
    tm = min(t, 512)
    return pl.pallas_call(
        functools.partial(_norm_matmul_kernel, act=act),
        grid=(t // tm, n // tn),
        in_specs=[
            pl.BlockSpec((tm, d), lambda i, j: (i, 0)),
            pl.BlockSpec((1, d), lambda i, j: (0, 0)),
            pl.BlockSpec((d, tn), lambda i, j: (0, j)),
        ],
        out_specs=pl.BlockSpec((tm, tn), lambda i, j: (i, j)),
        out_shape=jax.ShapeDtypeStruct((t, n), out_dtype),
        scratch_shapes=[pltpu.VMEM((tm, d), BF16)],
        compiler_params=_cparams("parallel", "arbitrary"),
        name="norm_matmul_" + act,
    )(x, g.reshape(1, d), w)


def _gmlp_kernel(uv_ref, lng_ref, lnb_ref, ws_ref, bst_ref, y_ref, vn_ref, *, lc):
    u = uv_ref[:, :BW]
    v = uv_ref[:, BW:]
    mu = jnp.mean(v, axis=-1, keepdims=True)
    vc = v - mu
    var = jnp.mean(vc * vc, axis=-1, keepdims=True)
    vn = vc * lax.rsqrt(var + EPS) * lng_ref[...] + lnb_ref[...]
    vn_ref[...] = vn
    row = lax.broadcasted_iota(jnp.int32, (lc, lc), 0) // CHUNK
    col = lax.broadcasted_iota(jnp.int32, (lc, lc), 1) // CHUNK
    causal = col <= row
    vnb = vn.astype(BF16)
    for g in range(GROUPS):
        w = jnp.where(causal, ws_ref[g], 0.0).astype(BF16)
        s = jnp.dot(w, vnb[:, g * GDIM:(g + 1) * GDIM], preferred_element_type=F32)
        s = s + bst_ref[:, g:g + 1]
        y_ref[:, g * GDIM:(g + 1) * GDIM] = (u[:, g * GDIM:(g + 1) * GDIM] * s).astype(y_ref.dtype)


def gmlp_mix(uv, ln_g, ln_b, ws, bs, lc):
    t = uv.shape[0]
    return pl.pallas_call(
        functools.partial(_gmlp_kernel, lc=lc),
        grid=(t // lc,),
        in_specs=[
            pl.BlockSpec((lc, 2 * BW), lambda i: (i, 0)),
            pl.BlockSpec((1, BW), lambda i: (0, 0)),
            pl.BlockSpec((1, BW), lambda i: (0, 0)),
            pl.BlockSpec((GROUPS, lc, lc), lambda i: (0, 0, 0)),
            pl.BlockSpec((lc, GROUPS), lambda i: (0, 0)),
        ],
        out_specs=[
            pl.BlockSpec((lc, BW), lambda i: (i, 0)),
            pl.BlockSpec((lc, BW), lambda i: (i, 0)),
        ],
        out_shape=[
            jax.ShapeDtypeStruct((t, BW), BF16),
            jax.ShapeDtypeStruct((t, BW), F32),
        ],
        compiler_params=_cparams("parallel"),
        name="gmlp_mix",
    )(uv, ln_g.reshape(1, BW), ln_b.reshape(1, BW), ws[:, :lc, :lc], bs[:, :lc].T)


def _pool_kernel(u_ref, pre_ref, w_ref, sc_ref, y_ref, st_ref, pad_ref, *, seq, start_pos):
    pad_ref[0:POOL_PAD, :] = pre_ref[...]
    pad_ref[POOL_PAD:, :] = u_ref[...]
    pos = lax.broadcasted_iota(jnp.int32, (seq, 1), 0) + start_pos
    for g, win in enumerate(POOL_WINDOWS):
        cols = slice(g * GDIM, (g + 1) * GDIM)
        tok = pad_ref[POOL_PAD:, cols]
        acc = tok
        for k in range(1, win):
            acc = acc + pad_ref[POOL_PAD - k:POOL_PAD - k + seq, cols]
        cnt = jnp.minimum(pos + 1, win).astype(F32)
        d = acc / cnt - tok
        y = jnp.dot(d.astype(BF16), w_ref[g], preferred_element_type=F32)
        y_ref[:, cols] = (y * sc_ref[:, cols]).astype(y_ref.dtype)
    st_ref[...] = pad_ref[seq + 1:seq + POOL_PAD, :]


def pool_mix(mid, prefix, start_pos, pool_w, pool_scale, batch, seq):
    mid3 = mid.reshape(batch, seq, 4 * BW)
    pre = jnp.concatenate([jnp.zeros((batch, 1, BW), F32), prefix], axis=1)
    y, st = pl.pallas_call(
        functools.partial(_pool_kernel, seq=seq, start_pos=start_pos),
        grid=(batch,),
        in_specs=[
            pl.BlockSpec((None, seq, BW), lambda b: (b, 0, 0)),
            pl.BlockSpec((None, POOL_PAD, BW), lambda b: (b, 0, 0)),
            pl.BlockSpec((GROUPS, GDIM, GDIM), lambda b: (0, 0, 0)),
            pl.BlockSpec((1, BW), lambda b: (0, 0)),
        ],
        out_specs=[
            pl.BlockSpec((None, seq, BW), lambda b: (b, 0, 0)),
            pl.BlockSpec((None, POOL_STATE, BW), lambda b: (b, 0, 0)),
        ],
        out_shape=[
            jax.ShapeDtypeStruct((batch, seq, BW), BF16),
            jax.ShapeDtypeStruct((batch, POOL_STATE, BW), F32),
        ],
        scratch_shapes=[pltpu.VMEM((seq + POOL_PAD, BW), F32)],
        compiler_params=_cparams("parallel"),
        name="pool_mix",
    )(mid3, pre, pool_w.astype(BF16), pool_scale.reshape(1, BW))
    return y.reshape(batch * seq, BW), st


def _attn_chunks(q_ref, kcat_ref, vcat_ref, bias_ref, o_ref, *, n_chunks, cq, band, first_block):
    scale = HEAD_DIM ** -0.5
    heads_per_tile = LANES // HEAD_DIM
    lane_head = lax.broadcasted_iota(jnp.int32, (1, LANES), 1) // HEAD_DIM
    for tile in range(HEADS // heads_per_tile):
        cols = slice(tile * LANES, (tile + 1) * LANES)
        for i in range(n_chunks):
            rows = slice(i * cq, (i + 1) * cq)
            q = (q_ref[rows, cols] * scale).astype(BF16)
            k = kcat_ref[i * cq:i * cq + band, cols]
            v = vcat_ref[i * cq:i * cq + band, cols]
            out = None
            for j in range(heads_per_tile):
                own = lane_head == j
                s = lax.dot_general(jnp.where(own, q, jnp.zeros_like(q)), k, (((1,), (1,)), ((), ())),
                                    preferred_element_type=F32) + bias_ref[tile * heads_per_tile + j]
                if first_block is not None:
                    key = lax.broadcasted_iota(jnp.int32, (1, band), 1)
                    s = jnp.where(key >= first_block * (BAND_PAST - i * cq), s, NEG)
                m = jnp.max(s, axis=-1, keepdims=True)
                p = jnp.exp(s - m)
                l = jnp.sum(p, axis=-1, keepdims=True)
                o = jnp.dot(p.astype(BF16), v, preferred_element_type=F32) / l
                out = o if out is None else jnp.where(own, o, out)
            o_ref[rows, cols] = out.astype(o_ref.dtype)


def _attn_prompt_kernel(q_ref, kp_ref, ko_ref, vp_ref, vo_ref, bias_ref, o_ref, kcat_ref, vcat_ref):
    kcat_ref[0:BAND_PAST, :] = kp_ref[...].astype(BF16)
    kcat_ref[BAND_PAST:, :] = ko_ref[...].astype(BF16)
    vcat_ref[0:BAND_PAST, :] = vp_ref[...].astype(BF16)
    vcat_ref[BAND_PAST:, :] = vo_ref[...].astype(BF16)
    _attn_chunks(q_ref, kcat_ref, vcat_ref, bias_ref, o_ref, n_chunks=BAND_CHUNKS, cq=CHUNK,
                 band=BAND_PAST + CHUNK, first_block=(pl.program_id(1) == 0).astype(jnp.int32))


def _rel_bias_tile(rel_bias, qpos, kpos):
    rel = np.clip(qpos[:, None] - kpos[None, :], -REL_CLIP, REL_CLIP) + REL_CLIP
    return rel_bias[:, rel]


def attn_prompt(mid, rel_bias, batch, seq):
    mid3 = mid.reshape(batch, seq, 4 * BW)
    blk = BAND_PAST
    bias = _rel_bias_tile(rel_bias, np.arange(CHUNK), np.arange(BAND_PAST + CHUNK) - BAND_PAST)
    prev = lambda b, j: jnp.maximum(j - 1, 0)
    y = pl.pallas_call(
        _attn_prompt_kernel,
        grid=(batch, seq // blk),
        in_specs=[
            pl.BlockSpec((None, blk, BW), lambda b, j: (b, j, 1)),
            pl.BlockSpec((None, blk, BW), lambda b, j: (b, prev(b, j), 2)),
            pl.BlockSpec((None, blk, BW), lambda b, j: (b, j, 2)),
            pl.BlockSpec((None, blk, BW), lambda b, j: (b, prev(b, j), 3)),
            pl.BlockSpec((None, blk, BW), lambda b, j: (b, j, 3)),
            pl.BlockSpec((HEADS, CHUNK, BAND_PAST + CHUNK), lambda b, j: (0, 0, 0)),
        ],
        out_specs=pl.BlockSpec((None, blk, BW), lambda b, j: (b, j, 0)),
        out_shape=jax.ShapeDtypeStruct((batch, seq, BW), BF16),
        scratch_shapes=[pltpu.VMEM((2 * blk, BW), BF16), pltpu.VMEM((2 * blk, BW), BF16)],
        compiler_params=_cparams("parallel", "parallel"),
        name="attn_prompt",
    )(mid3, mid3, mid3, mid3, mid3, bias)
    return y.reshape(batch * seq, BW)


def _attn_sample_kernel(q_ref, kc_ref, kn_ref, vc_ref, vn_ref, bias_ref, o_ref, kcat_ref, vcat_ref, *, n_cache):
    kcat_ref[0:n_cache, :] = kc_ref[...].astype(BF16)
    kcat_ref[n_cache:, :] = kn_ref[...].astype(BF16)
    vcat_ref[0:n_cache, :] = vc_ref[...].astype(BF16)
    vcat_ref[n_cache:, :] = vn_ref[...].astype(BF16)
    seq = q_ref.shape[0]
    _attn_chunks(q_ref, kcat_ref, vcat_ref, bias_ref, o_ref, n_chunks=1, cq=seq, band=n_cache + seq,
                 first_block=None)


def attn_sample(mid, cache_k, cache_v, rel_bias, batch, seq):
    n_cache = cache_k.shape[1]
    assert PAST_LEN >= n_cache
    mid3 = mid.reshape(batch, seq, 4 * BW)
    ck = cache_k.reshape(batch, n_cache, BW)
    cv = cache_v.reshape(batch, n_cache, BW)
    bias = _rel_bias_tile(rel_bias, PAST_LEN + np.arange(seq), PAST_LEN - n_cache + np.arange(n_cache + seq))
    y = pl.pallas_call(
        functools.partial(_attn_sample_kernel, n_cache=n_cache),
        grid=(batch,),
        in_specs=[
            pl.BlockSpec((None, seq, BW), lambda b: (b, 0, 1)),
            pl.BlockSpec((None, n_cache, BW), lambda b: (b, 0, 0)),
            pl.BlockSpec((None, seq, BW), lambda b: (b, 0, 2)),
            pl.BlockSpec((None, n_cache, BW), lambda b: (b, 0, 0)),
            pl.BlockSpec((None, seq, BW), lambda b: (b, 0, 3)),
            pl.BlockSpec((HEADS, seq, n_cache + seq), lambda b: (0, 0, 0)),
        ],
        out_specs=pl.BlockSpec((None, seq, BW), lambda b: (b, 0, 0)),
        out_shape=jax.ShapeDtypeStruct((batch, seq, BW), BF16),
        scratch_shapes=[pltpu.VMEM((n_cache + seq, BW), BF16), pltpu.VMEM((n_cache + seq, BW), BF16)],
        compiler_params=_cparams("parallel"),
        name="attn_sample",
    )(mid3, ck, mid3, cv, mid3, bias)
    return y.reshape(batch * seq, BW)


def _mixer_out_kernel(x_ref, ya_ref, yb_ref, yc_ref, gate_ref, wb_ref, wo_ref, o_ref):
    acc = None
    for n, y_ref in enumerate((ya_ref, yb_ref, yc_ref)):
        proj = jnp.dot(y_ref[...], wb_ref[n], preferred_element_type=F32)
        term = gate_ref[:, n * D_MODEL:(n + 1) * D_MODEL].astype(F32) * proj
        acc = term if acc is None else acc + term
    o_ref[...] = x_ref[...] + jnp.dot(acc.astype(BF16), wo_ref[...], preferred_element_type=F32)


def mixer_out(x, ya, yb, yc, gates, w_branch, w_out):
    t = x.shape[0]
    tm = min(t, 512)
    row = lambda i: (i, 0)
    return pl.pallas_call(
        _mixer_out_kernel,
        grid=(t // tm,),
        in_specs=[
            pl.BlockSpec((tm, D_MODEL), row),
            pl.BlockSpec((tm, BW), row),
            pl.BlockSpec((tm, BW), row),
            pl.BlockSpec((tm, BW), row),
            pl.BlockSpec((tm, 3 * D_MODEL), row),
            pl.BlockSpec((3, BW, D_MODEL), lambda i: (0, 0, 0)),
            pl.BlockSpec((D_MODEL, D_MODEL), lambda i: (0, 0)),
        ],
        out_specs=pl.BlockSpec((tm, D_MODEL), row),
        out_shape=jax.ShapeDtypeStruct((t, D_MODEL), F32),
        compiler_params=_cparams("parallel"),
        name="mixer_out",
    )(x, ya, yb, yc, gates, w_branch, w_out)


def _extract_top(s, payload, k):
    r = float(s.shape[0])
    rows = lax.broadcasted_iota(jnp.int32, s.shape, 0).astype(F32)
    vals, pays = [], []
    for _ in range(k):
        m = jnp.max(s, axis=0, keepdims=True)
        idx = jnp.min(jnp.where(s == m, rows, r), axis=0, keepdims=True)
        sel = rows == idx
        vals.append(m)
        pays.append(idx if payload is None else jnp.max(jnp.where(sel, payload, -1.0), axis=0, keepdims=True))
        s = jnp.where(sel, -jnp.inf, s)
    return jnp.concatenate(vals, axis=0), jnp.concatenate(pays, axis=0)


def _pair_candidates(sv, si):
    k = PEER_TOPK
    sub = 8
    assert k == 2 * sub
    b_row = lax.broadcasted_iota(jnp.int32, (sub, LANES), 0)
    vals = [sv[0][0:1] + sv[1], sv[0][1:2] + sv[1][0:sub]]
    ids = [si[0][0:1] * PEER_NKEYS + si[1], si[0][1:2] * PEER_NKEYS + si[1][0:sub]]
    for a in range(2, sub):
        keep = b_row < k // (a + 1)
        vals.append(jnp.where(keep, sv[0][a:a + 1] + sv[1][0:sub], -jnp.inf))
        ids.append(si[0][a:a + 1] * PEER_NKEYS + si[1][0:sub])
    vals.append(sv[0][sub:k] + sv[1][0:1])
    ids.append(si[0][sub:k] * PEER_NKEYS + si[1][0:1])
    return jnp.concatenate(vals, axis=0), jnp.concatenate(ids, axis=0)


def _peer_query_kernel(x_ref, g_ref, wq_ref, sk_ref, hp_ref, idx_ref, gate_ref, q_ref, *, tm):
    hb = _rms(x_ref[...], g_ref[...]).astype(BF16)
    q_ref[...] = jnp.dot(hb, wq_ref[...], preferred_element_type=F32).astype(BF16)
    bits = lax.bitcast_convert_type(hb.astype(F32), jnp.int32)
    half = D_MODEL // 2
    hp_ref[...] = bits[:, half:] | lax.shift_right_logical(bits[:, :half], 16)

    def sub_block(sb, carry):
        tok = pl.ds(pl.multiple_of(sb * LANES, LANES), LANES)
        for hd in range(PEER_HEADS):
            sv, si = [], []
            for p in range(2):
                hp = hd * 2 + p
                q = q_ref[tok, hp * PEER_HALF:(hp + 1) * PEER_HALF]
                s = lax.dot_general(sk_ref[hp], q, (((1,), (1,)), ((), ())), preferred_element_type=F32)
                v, i = _extract_top(s, None, PEER_TOPK)
                sv.append(v)
                si.append(i)
            cand, eid = _pair_candidates(sv, si)
            tv, te = _extract_top(cand, eid, PEER_TOPK)
            e = jnp.exp(tv - tv[0:1])
            gate = e / jnp.sum(e, axis=0, keepdims=True)
            idx_ref[sb, hd * PEER_TOPK:(hd + 1) * PEER_TOPK, :] = te.astype(jnp.int32)
            gate_ref[sb, hd * PEER_TOPK:(hd + 1) * PEER_TOPK, :] = gate
        return carry

    lax.fori_loop(0, tm // LANES, sub_block, 0)


def peer_query(x, norm_g, wq, subkeys):
    t = x.shape[0]
    tm = min(t, 512)
    nq = wq.shape[1]
    nsb = tm // LANES
    return pl.pallas_call(
        functools.partial(_peer_query_kernel, tm=tm),
        grid=(t // tm,),
        in_specs=[
            pl.BlockSpec((tm, D_MODEL), lambda i: (i, 0)),
            pl.BlockSpec((1, D_MODEL), lambda i: (0, 0)),
            pl.BlockSpec((D_MODEL, nq), lambda i: (0, 0)),
            pl.BlockSpec((2 * PEER_HEADS, PEER_NKEYS, PEER_HALF), lambda i: (0, 0, 0)),
        ],
        out_specs=[
            pl.BlockSpec((tm, D_MODEL // 2), lambda i: (i, 0)),
            pl.BlockSpec((nsb, PEER_SLOTS, LANES), lambda i: (i, 0, 0)),
            pl.BlockSpec((nsb, PEER_SLOTS, LANES), lambda i: (i, 0, 0)),
        ],
        out_shape=[
            jax.ShapeDtypeStruct((t, D_MODEL // 2), jnp.int32),
            jax.ShapeDtypeStruct((t // LANES, PEER_SLOTS, LANES), jnp.int32),
            jax.ShapeDtypeStruct((t // LANES, PEER_SLOTS, LANES), F32),
        ],
        scratch_shapes=[pltpu.VMEM((tm, nq), BF16)],
        compiler_params=_cparams("parallel"),
        name="peer_query",
    )(x, norm_g.reshape(1, D_MODEL), wq, subkeys)


HALF_D = D_MODEL // 2
SC_LANES = 16
SC_UNIT_ROWS = 32
SC_UNITS_PER_TOKEN = PEER_SLOTS // SC_UNIT_ROWS
SC_MAX_TOKEN_BLOCK = 8
SC_BUFFERS = 3
SC_ROW_GROUP = 8
SC_HALF_VECS = HALF_D // SC_LANES
SC_OUT_VECS = 8
SC_BF16_TERMS = 4
SC_ROW_TILES = D_MODEL // LANES


def pack_expert_tables(peer_u, peer_v):
    def pack(x):
        b = lax.bitcast_convert_type(x.astype(BF16), jnp.uint16).astype(jnp.uint32)
        return (b[:, HALF_D:] << 16) | b[:, :HALF_D]

    rows = lax.bitcast_convert_type(jnp.concatenate([pack(peer_u), pack(peer_v)], axis=1), jnp.int32)
    return rows.reshape(rows.shape[0], SC_ROW_TILES, LANES)


def sc_peer_experts(table, idx, gates, hp):
    t = hp.shape[0]
    info = plsc.get_sparse_core_info()
    n_workers = info.num_cores * info.num_subcores
    tpw = t // n_workers
    token_block = min(tpw, SC_MAX_TOKEN_BLOCK)
    assert t % (n_workers * token_block) == 0
    units = token_block * SC_UNITS_PER_TOKEN
    mesh = plsc.VectorSubcoreMesh(core_axis_name="core", subcore_axis_name="subcore")
    hi_mask = jnp.int32(-65536)
    gelu_c = math.sqrt(2.0 / math.pi)

    @functools.partial(
        pl.kernel,
        out_type=jax.ShapeDtypeStruct((t, D_MODEL), F32),
        mesh=mesh,
        scratch_types=[
            pltpu.VMEM((token_block * PEER_SLOTS,), jnp.int32),
            pltpu.VMEM((token_block * PEER_SLOTS,), F32),
            pltpu.VMEM((token_block, HALF_D), jnp.int32),
            pltpu.VMEM((token_block, D_MODEL), F32),
            pltpu.VMEM((SC_BUFFERS, SC_UNIT_ROWS, SC_ROW_TILES, LANES), jnp.int32),
            pltpu.VMEM((SC_UNIT_ROWS, SC_LANES), F32),
            pltpu.VMEM((SC_UNIT_ROWS,), jnp.int32),
            pltpu.SemaphoreType.DMA((SC_BUFFERS,)),
        ],
        compiler_params=pltpu.CompilerParams(needs_layout_passes=False),
        name="peer_sc_experts",
    )
    def kern(table_hbm, idx_hbm, gate_hbm, h_hbm, out_hbm, idx_v, gate_v, h_v, out_v, rows_v, part_v, coef_v, sem):
        wid = lax.axis_index("subcore") * info.num_cores + lax.axis_index("core")
        lane = lax.iota(jnp.int32, SC_LANES)
        zero = jnp.zeros((SC_LANES,), F32)

        def gather(unit, b):
            rows = idx_v.at[pl.ds(unit * SC_UNIT_ROWS, SC_UNIT_ROWS)]
            return pltpu.make_async_copy(table_hbm.at[rows], rows_v.at[b], sem.at[b])

        def row_vec(b, r, vec):
            per_tile_row = LANES // SC_LANES
            lane0 = pl.multiple_of((vec % per_tile_row) * SC_LANES, SC_LANES)
            return rows_v[b, r, vec // per_tile_row, pl.ds(lane0, SC_LANES)]

        def unpack(w):
            return lax.bitcast_convert_type(w << 16, F32), lax.bitcast_convert_type(w & hi_mask, F32)

        def as_pairs(w):
            return plsc.bitcast(w, BF16)

        def compute(unit, b):
            tl = unit // SC_UNITS_PER_TOKEN
            q = unit % SC_UNITS_PER_TOKEN

            def row_group(rg, carry):
                def kstep(k, accs):
                    hs = []
                    for j in range(SC_BF16_TERMS):
                        off = pl.multiple_of((k * SC_BF16_TERMS + j) * SC_LANES, SC_LANES)
                        hs.append(as_pairs(h_v[tl, pl.ds(off, SC_LANES)]))
                    new = []
                    for r in range(SC_ROW_GROUP):
                        p = None
                        for j in range(SC_BF16_TERMS):
                            term = as_pairs(row_vec(b, rg * SC_ROW_GROUP + r, k * SC_BF16_TERMS + j)) * hs[j]
                            p = term if p is None else p + term
                        lo, hi = unpack(plsc.bitcast(p, jnp.int32))
                        new.append(accs[r] + lo + hi)
                    return tuple(new)

                accs = lax.fori_loop(0, SC_HALF_VECS // SC_BF16_TERMS, kstep, (zero,) * SC_ROW_GROUP)
                for r in range(SC_ROW_GROUP):
                    part_v[rg * SC_ROW_GROUP + r, :] = accs[r]
                return carry

            lax.fori_loop(0, SC_UNIT_ROWS // SC_ROW_GROUP, row_group, 0)

            for i in range(SC_UNIT_ROWS // SC_LANES):
                rows = lane + i * SC_LANES
                a = zero
                for l in range(SC_LANES):
                    a = a + plsc.load_gather(part_v, [rows, jnp.full((SC_LANES,), l, jnp.int32)])
                z = gelu_c * (a + 0.044715 * (a * a * a))
                act = a / (1.0 + jnp.exp(-2.0 * z))
                slot = pl.multiple_of(tl * PEER_SLOTS + q * SC_UNIT_ROWS + i * SC_LANES, SC_LANES)
                bits = lax.bitcast_convert_type(gate_v[pl.ds(slot, SC_LANES)] * act, jnp.int32)
                top = (bits + 0x7FFF + ((bits >> 16) & 1)) & hi_mask
                coef_v[pl.ds(i * SC_LANES, SC_LANES)] = top | lax.shift_right_logical(top, 16)

            def out_pass(dq, carry):
                def row_quad(rq, accs):
                    per_vec = SC_LANES // SC_BF16_TERMS
                    cvec = coef_v[pl.ds(pl.multiple_of((rq // per_vec) * SC_LANES, SC_LANES), SC_LANES)]
                    cs = []
                    for j in range(SC_BF16_TERMS):
                        src = jnp.full((SC_LANES,), (rq % per_vec) * SC_BF16_TERMS + j, jnp.int32)
                        cs.append(as_pairs(cvec.at[src].get(mode="promise_in_bounds")))
                    new = []
                    for k in range(SC_OUT_VECS):
                        p = None
                        for j in range(SC_BF16_TERMS):
                            w = row_vec(b, rq * SC_BF16_TERMS + j, SC_HALF_VECS + dq * SC_OUT_VECS + k)
                            term = as_pairs(w) * cs[j]
                            p = term if p is None else p + term
                        lo, hi = unpack(plsc.bitcast(p, jnp.int32))
                        new.append(accs[2 * k] + lo)
                        new.append(accs[2 * k + 1] + hi)
                    return tuple(new)

                accs = lax.fori_loop(0, SC_UNIT_ROWS // SC_BF16_TERMS, row_quad, (zero,) * (2 * SC_OUT_VECS))
                for k in range(SC_OUT_VECS):
                    off = pl.multiple_of((dq * SC_OUT_VECS + k) * SC_LANES, SC_LANES)
                    plsc.addupdate(out_v.at[tl, pl.ds(off, SC_LANES)], accs[2 * k])
                    plsc.addupdate(out_v.at[tl, pl.ds(HALF_D + off, SC_LANES)], accs[2 * k + 1])
                return carry

            lax.fori_loop(0, SC_HALF_VECS // SC_OUT_VECS, out_pass, 0)

        @pl.loop(0, tpw // token_block)
        def _(blk):
            tok0 = wid * tpw + blk * token_block
            slots = pl.ds(tok0 * PEER_SLOTS, token_block * PEER_SLOTS)
            pltpu.sync_copy(idx_hbm.at[slots], idx_v)
            pltpu.sync_copy(gate_hbm.at[slots], gate_v)
            pltpu.sync_copy(h_hbm.at[pl.ds(tok0, token_block)], h_v)

            @pl.loop(0, token_block)
            def _(tl):
                @pl.loop(0, D_MODEL // SC_LANES)
                def _(k):
                    out_v[tl, pl.ds(pl.multiple_of(k * SC_LANES, SC_LANES), SC_LANES)] = zero

            for u in range(SC_BUFFERS - 1):
                gather(u, u).start()

            @pl.loop(0, units)
            def _(unit):
                ahead = unit + SC_BUFFERS - 1

                @pl.when(ahead < units)
                def _():
                    gather(ahead, ahead % SC_BUFFERS).start()

                b = unit % SC_BUFFERS
                gather(unit, b).wait()
                compute(unit, b)

            pltpu.sync_copy(out_v, out_hbm.at[pl.ds(tok0, token_block)])

    return kern(table, idx, gates, hp)


TC_PEER_TOKENS = 8
TC_PEER_BUFFERS = 4
U_TILE_ROWS = SC_ROW_TILES // 2


def _tc_peer_kernel(idx_ref, idx_next_ref, gate_ref, hp_ref, table_hbm, o_ref, rows_buf, a_buf, c_buf, sem):
    step = pl.program_id(0)
    hi_mask = jnp.int32(-65536)

    def unpack(w):
        return lax.bitcast_convert_type(w << 16, F32), lax.bitcast_convert_type(w & hi_mask, F32)

    def row_copy(ids_ref, tok, e, slot):
        return pltpu.make_async_copy(table_hbm.at[ids_ref[tok * PEER_SLOTS + e]], rows_buf.at[slot, e], sem.at[slot])

    def wait_rows(slot):
        pltpu.make_async_copy(table_hbm.at[pl.ds(0, PEER_SLOTS)], rows_buf.at[slot], sem.at[slot]).wait()

    ahead = TC_PEER_BUFFERS - 1

    @pl.when(step == 0)
    def _():
        for tok in range(ahead):
            def first(e, c, tok=tok):
                row_copy(idx_ref, tok, e, tok).start()
                return c

            lax.fori_loop(0, PEER_SLOTS, first, 0, unroll=8)

    gates_t = gate_ref[...].T
    zeros_v = jnp.zeros((SC_ROW_TILES - U_TILE_ROWS, LANES), F32)
    for t in range(TC_PEER_TOKENS):
        slot = t % TC_PEER_BUFFERS
        nxt = t + ahead
        nxt_slot = nxt % TC_PEER_BUFFERS
        wait_rows(slot)
        h_lo, h_hi = unpack(hp_ref[t])
        h_lo = jnp.concatenate([h_lo, zeros_v], axis=0)
        h_hi = jnp.concatenate([h_hi, zeros_v], axis=0)

        def u_body(e, c, slot=slot, nxt=nxt, nxt_slot=nxt_slot, h_lo=h_lo, h_hi=h_hi):
            if nxt >= TC_PEER_TOKENS:
                @pl.when(step + 1 < pl.num_programs(0))
                def _():
                    row_copy(idx_next_ref, nxt - TC_PEER_TOKENS, e, nxt_slot).start()
            else:
                row_copy(idx_ref, nxt, e, nxt_slot).start()
            lo, hi = unpack(rows_buf[slot, e])
            a_buf[pl.ds(e, 1), :] = jnp.sum(lo * h_lo + hi * h_hi, axis=0, keepdims=True)
            return c

        lax.fori_loop(0, PEER_SLOTS, u_body, 0, unroll=8)
        a = jnp.sum(a_buf[...], axis=1, keepdims=True)
        c_buf[...] = jnp.broadcast_to(gates_t[:, t:t + 1] * _gelu(a), (PEER_SLOTS, LANES))

        def v_body(e, acc, slot=slot):
            lo, hi = unpack(rows_buf[slot, e])
            c = c_buf[pl.ds(e, 1), :]
            return acc[0] + c * lo, acc[1] + c * hi

        zero = jnp.zeros((SC_ROW_TILES, LANES), F32)
        acc_lo, acc_hi = lax.fori_loop(0, PEER_SLOTS, v_body, (zero, zero), unroll=8)
        o_ref[t, 0:U_TILE_ROWS, :] = acc_lo[U_TILE_ROWS:, :]
        o_ref[t, U_TILE_ROWS:, :] = acc_hi[U_TILE_ROWS:, :]


def tc_peer_experts(table, idx, gates, hp):
    t = hp.shape[0]
    assert t % TC_PEER_TOKENS == 0 and TC_PEER_TOKENS % TC_PEER_BUFFERS == 0
    n_steps = t // TC_PEER_TOKENS
    ids = TC_PEER_TOKENS * PEER_SLOTS
    smem = functools.partial(pl.BlockSpec, memory_space=pltpu.SMEM)
    out = pl.pallas_call(
        _tc_peer_kernel,
        grid=(n_steps,),
        in_specs=[
            smem((ids,), lambda i: (i,)),
            smem((ids,), lambda i: (jnp.minimum(i + 1, n_steps - 1),)),
            pl.BlockSpec((TC_PEER_TOKENS, PEER_SLOTS), lambda i: (i, 0)),
            pl.BlockSpec((TC_PEER_TOKENS, U_TILE_ROWS, LANES), lambda i: (i, 0, 0)),
            pl.BlockSpec(memory_space=pl.ANY),
        ],
        out_specs=pl.BlockSpec((TC_PEER_TOKENS, SC_ROW_TILES, LANES), lambda i: (i, 0, 0)),
        out_shape=jax.ShapeDtypeStruct((t, SC_ROW_TILES, LANES), F32),
        scratch_shapes=[
            pltpu.VMEM((TC_PEER_BUFFERS, PEER_SLOTS, SC_ROW_TILES, LANES), jnp.int32),
            pltpu.VMEM((PEER_SLOTS, LANES), F32),
            pltpu.VMEM((PEER_SLOTS, LANES), F32),
            pltpu.SemaphoreType.DMA((TC_PEER_BUFFERS,)),
        ],
        compiler_params=pltpu.CompilerParams(dimension_semantics=("arbitrary",), vmem_limit_bytes=VMEM_LIMIT,
                                             disable_bounds_checks=True),
        name="tc_peer_experts",
    )(idx, idx, gates, hp.reshape(t, U_TILE_ROWS, LANES), table)
    return out.reshape(t, D_MODEL)


TC_PEER_SHARE = 8
TC_PEER_QUANTUM = 1024


def peer_block(x, norm_g, wq, subkeys, table, prev_sc=None):
    t = x.shape[0]
    hp, idx3, gate3 = peer_query(x, norm_g, wq, subkeys)
    idx = jnp.transpose(idx3, (0, 2, 1)).reshape(t * PEER_SLOTS)
    gates = jnp.transpose(gate3, (0, 2, 1)).reshape(t * PEER_SLOTS)
    t_tc = (t // TC_PEER_SHARE) // TC_PEER_QUANTUM * TC_PEER_QUANTUM
    t_sc = t - t_tc
    idx_sc = idx[:t_sc * PEER_SLOTS]
    if prev_sc is not None:
        idx_sc, _ = lax.optimization_barrier((idx_sc, prev_sc))
    on_sc = sc_peer_experts(table, idx_sc, gates[:t_sc * PEER_SLOTS], hp[:t_sc])
    if t_tc == 0:
        return on_sc, (hp,), on_sc
    on_tc = tc_peer_experts(table, idx[t_sc * PEER_SLOTS:], gates[t_sc * PEER_SLOTS:].reshape(t_tc, PEER_SLOTS),
                            hp[t_sc:])
    return jnp.concatenate([on_sc, on_tc], axis=0), (hp, on_tc), on_sc


def _ple_kernel(x_ref, ffn_ref, p_ref, g_ref, wg_ref, wp_ref, gf_ref, o_ref, *, final):
    x = x_ref[...] + ffn_ref[...]
    gate = _sigmoid(jnp.dot(_rms(x, g_ref[...]).astype(BF16), wg_ref[...], preferred_element_type=F32))
    emb = jnp.dot(p_ref[...].astype(BF16), wp_ref[...], preferred_element_type=F32)
    y = x + gate * emb
    o_ref[...] = _rms(y, gf_ref[...]) if final else y


def ple_block(x, ffn, p, norm_g, w_gate, w_proj, norm_final, final):
    t = x.shape[0]
    tm = min(t, 512)
    row = lambda i: (i, 0)
    full = lambda i: (0, 0)
    return pl.pallas_call(
        functools.partial(_ple_kernel, final=final),
        grid=(t // tm,),
        in_specs=[
            pl.BlockSpec((tm, D_MODEL), row),
            pl.BlockSpec((tm, D_MODEL), row),
            pl.BlockSpec((tm, PLE_DIM), row),
            pl.BlockSpec((1, D_MODEL), full),
            pl.BlockSpec((D_MODEL, D_MODEL), full),
            pl.BlockSpec((PLE_DIM, D_MODEL), full),
            pl.BlockSpec((1, D_MODEL), full),
        ],
        out_specs=pl.BlockSpec((tm, D_MODEL), row),
        out_shape=jax.ShapeDtypeStruct((t, D_MODEL), F32),
        compiler_params=_cparams("parallel"),
        name="ple_block",
    )(x, ffn, p, norm_g.reshape(1, D_MODEL), w_gate, w_proj, norm_final.reshape(1, D_MODEL))


def _prompt_groups(batch):
    sizes = []
    while sum(sizes) < batch:
        nxt = 1 if len(sizes) < 2 else -(-sizes[-1] * 7 // 5)
        sizes.append(min(nxt, batch - sum(sizes)))
    return sizes
def _trunk_layer(x, ple, lw, batch, seq, pool_prefix, pool_start, cache, final, after=None, prev_sc=None):
    if after is not None:
        x, _ = lax.optimization_barrier((x, after))
    uv = norm_matmul(x, lw["norm_mix"], lw["w_uv"], "gelu", F32, 2 * BW)
    mid = norm_matmul(x, lw["norm_mix"], lw["w_mid"], "none", F32, 4 * BW)
    gates = norm_matmul(x, lw["norm_mix"], lw["w_gates"], "sigmoid", BF16, D_MODEL)
    ya, vn = gmlp_mix(uv, lw["gmlp_ln_g"], lw["gmlp_ln_b"], lw["gmlp_ws"], lw["gmlp_bs"], min(seq, GMLP_CHUNK))
    yb, pool_state = pool_mix(mid, pool_prefix, pool_start, lw["pool_w"], lw["pool_scale"], batch, seq)
    if cache is None:
        yc = attn_prompt(mid, lw["rel_bias"], batch, seq)
    else:
        yc = attn_sample(mid, cache[0], cache[1], lw["rel_bias"], batch, seq)
    x = mixer_out(x, ya, yb, yc, gates, lw["w_branch"], lw["w_out"])
    ffn, stage, sc_out = peer_block(x, lw["norm_ffn"], lw["peer_wq"], lw["peer_subkeys"], lw["peer_table"], prev_sc)
    x = ple_block(x, ffn, ple, lw["norm_ple"], lw["ple_gate"], lw["ple_proj"], lw["norm_final"], final)
    return x, mid, pool_state, vn, stage, sc_out


def kernel(x_prompt, x_sample, cache_attn_k, cache_attn_v, state_pool, p_prompt, p_sample, norm_mix, w_in, gmlp_ln_g, gmlp_ln_b, gmlp_ws, gmlp_bs, pool_w, pool_scale, attn_rel_bias, w_branch, w_out, norm_ffn, peer_wq, peer_subkeys, peer_u, peer_v, norm_ple, ple_gate, ple_proj, norm_final):
    bp, lp, _ = x_prompt.shape
    bs, ls, _ = x_sample.shape
    assert lp % BAND_PAST == 0 and lp % GMLP_CHUNK == 0 and ls <= CHUNK
    n_keep = min(BAND_PAST, lp)
    sizes = _prompt_groups(bp)
    starts = [sum(sizes[:g]) for g in range(len(sizes))]
    xg = [x_prompt[a:a + n].reshape(n * lp, D_MODEL) for a, n in zip(starts, sizes)]
    xs = x_sample.reshape(bs * ls, D_MODEL)
    outs = {k: [] for k in ("pk", "pv", "pps", "sk", "sv", "sps", "sgv")}
    after = prev_sc = None
    stages = []
    for i in range(DEPTH):
        tab_u, tab_v = peer_u[i], peer_v[i]
        if stages:
            tab_u, tab_v, _ = lax.optimization_barrier((tab_u, tab_v, stages[len(stages) // 2]))
        stages = []
        w_in_b = w_in[i].astype(BF16)
        lw = dict(
            norm_mix=norm_mix[i],
            w_uv=w_in_b[:, :2 * BW],
            w_mid=w_in_b[:, 2 * BW:6 * BW],
            w_gates=w_in_b[:, 6 * BW:],
            gmlp_ln_g=gmlp_ln_g[i], gmlp_ln_b=gmlp_ln_b[i], gmlp_ws=gmlp_ws[i], gmlp_bs=gmlp_bs[i],
            pool_w=pool_w[i], pool_scale=pool_scale[i], rel_bias=attn_rel_bias[i],
            w_branch=w_branch[i].astype(BF16), w_out=w_out[i].astype(BF16),
            norm_ffn=norm_ffn[i], peer_wq=peer_wq[i].astype(BF16),
            peer_subkeys=peer_subkeys[i].reshape(2 * PEER_HEADS, PEER_NKEYS, PEER_HALF).astype(BF16),
            peer_table=pack_expert_tables(tab_u, tab_v),
            norm_ple=norm_ple[i], ple_gate=ple_gate[i].astype(BF16), ple_proj=ple_proj[i].astype(BF16),
            norm_final=norm_final,
        )
        final = i == DEPTH - 1
        pk, pv, pps = [], [], []
        for g, (a, bg) in enumerate(zip(starts, sizes)):
            ple_g = p_prompt[i, a:a + bg].reshape(bg * lp, PLE_DIM)
            zero_prefix = jnp.zeros((bg, POOL_STATE, BW), F32)
            xg[g], mid_p, ps_p, _, after, prev_sc = _trunk_layer(xg[g], ple_g, lw, bg, lp, zero_prefix, 0, None, final,
                                                                 after, prev_sc)
            stages.append(after)
            mid_p = mid_p.reshape(bg, lp, 4 * BW)
            pk.append(mid_p[:, lp - n_keep:, 2 * BW:3 * BW].reshape(bg, n_keep, HEADS, HEAD_DIM))
            pv.append(mid_p[:, lp - n_keep:, 3 * BW:].reshape(bg, n_keep, HEADS, HEAD_DIM))
            pps.append(ps_p)
        outs["pk"].append(jnp.concatenate(pk, axis=0))
        outs["pv"].append(jnp.concatenate(pv, axis=0))
        outs["pps"].append(jnp.concatenate(pps, axis=0))
        xs, mid_s, ps_s, vn_s, _, _ = _trunk_layer(xs, p_sample[i].reshape(bs * ls, PLE_DIM), lw, bs, ls, state_pool[i],
                                                PAST_LEN, (cache_attn_k[i], cache_attn_v[i]), final)
        mid_s = mid_s.reshape(bs, ls, 4 * BW)
        outs["sk"].append(mid_s[:, :, 2 * BW:3 * BW].reshape(bs, ls, HEADS, HEAD_DIM))
        outs["sv"].append(mid_s[:, :, 3 * BW:].reshape(bs, ls, HEADS, HEAD_DIM))
        outs["sps"].append(ps_s)
        outs["sgv"].append(vn_s.reshape(bs, ls, BW))
    st = lambda k: jnp.stack(outs[k])
    y_prompt = jnp.concatenate(xg, axis=0).reshape(bp, lp, D_MODEL)
    return (y_prompt, xs.reshape(bs, ls, D_MODEL), st("pk"), st("pv"), st("pps"),
            st("sk"), st("sv"), st("sps"), st("sgv"))
```

```python
import functools
import math

import jax
import jax.numpy as jnp
import numpy as np
from jax import lax
from jax.experimental import pallas as pl
from jax.experimental.pallas import tpu as pltpu
from jax.experimental.pallas import tpu_sc as plsc

F32 = jnp.float32
BF16 = jnp.bfloat16

D_MODEL = 1024
DEPTH = 2
CHUNK = 64
EPS = 1e-6
BW = D_MODEL // 2
GMLP_CHUNK = 128
GROUPS = 4
GDIM = BW // GROUPS
POOL_WINDOWS = (2, 4, 8, 16)
POOL_STATE = 15
POOL_PAD = 16
HEADS = 8
HEAD_DIM = BW // HEADS
BAND_CHUNKS = 8
BAND_PAST = BAND_CHUNKS * CHUNK
REL_CLIP = 128
PAST_LEN = 4096
PEER_HEADS = 8
PEER_NKEYS = 128
PEER_HALF = 128
PEER_TOPK = 16
PEER_SLOTS = PEER_HEADS * PEER_TOPK
PLE_DIM = 256

LANES = 128
VMEM_LIMIT = 56 * 1024 * 1024
NEG = -1e30


def _cparams(*sem):
    return pltpu.CompilerParams(dimension_semantics=sem, vmem_limit_bytes=VMEM_LIMIT)


def _rms(x, g):
    ms = jnp.mean(x * x, axis=-1, keepdims=True)
    return x * lax.rsqrt(ms + EPS) * g


def _gelu(x):
    c = math.sqrt(2.0 / math.pi)
    return 0.5 * x * (1.0 + jnp.tanh(c * (x + 0.044715 * (x * x * x))))


def _sigmoid(x):
    return 1.0 / (1.0 + jnp.exp(-x))


_ACTS = {"gelu": _gelu, "sigmoid": _sigmoid, "none": lambda z: z}


def _norm_matmul_kernel(x_ref, g_ref, w_ref, o_ref, h_ref, *, act):
    @pl.when(pl.program_id(1) == 0)
    def _():
        h_ref[...] = _rms(x_ref[...], g_ref[...]).astype(BF16)

    z = jnp.dot(h_ref[...], w_ref[...], preferred_element_type=F32)
    o_ref[...] = _ACTS[act](z).astype(o_ref.dtype)


def norm_matmul(x, g, w, act, out_dtype, tn):
    t, d = x.shape
    n = w.shape[1]
    tm = min(t, 512)
    return pl.pallas_call(
        functools.partial(_norm_matmul_kernel, act=act),
        grid=(t // tm, n // tn),
        in_specs=[
            pl.BlockSpec((tm, d), lambda i, j: (i, 0)),
            pl.BlockSpec((1, d), lambda i, j: (0, 0)),
            pl.BlockSpec((d, tn), lambda i, j: (0, j)),
        ],
        out_specs=pl.BlockSpec((tm, tn), lambda i, j: (i, j)),
        out_shape=jax.ShapeDtypeStruct((t, n), out_dtype),
        scratch_shapes=[pltpu.VMEM((tm, d), BF16)],
        compiler_params=_cparams("parallel", "arbitrary"),
        name="norm_matmul_" + act,
    )(x, g.reshape(1, d), w)


def _gmlp_kernel(uv_ref, lng_ref, lnb_ref, ws_ref, bst_ref, y_ref, vn_ref, *, lc):
    u = uv_ref[:, :BW]
    v = uv_ref[:, BW:]
    mu = jnp.mean(v, axis=-1, keepdims=True)
    vc = v - mu
    var = jnp.mean(vc * vc, axis=-1, keepdims=True)
    vn = vc * lax.rsqrt(var + EPS) * lng_ref[...] + lnb_ref[...]
    vn_ref[...] = vn
    row = lax.broadcasted_iota(jnp.int32, (lc, lc), 0) // CHUNK
    col = lax.broadcasted_iota(jnp.int32, (lc, lc), 1) // CHUNK
    causal = col <= row
    vnb = vn.astype(BF16)
    for g in range(GROUPS):
        w = jnp.where(causal, ws_ref[g], 0.0).astype(BF16)
        s = jnp.dot(w, vnb[:, g * GDIM:(g + 1) * GDIM], preferred_element_type=F32)
        s = s + bst_ref[:, g:g + 1]
        y_ref[:, g * GDIM:(g + 1) * GDIM] = (u[:, g * GDIM:(g + 1) * GDIM] * s).astype(y_ref.dtype)


def gmlp_mix(uv, ln_g, ln_b, ws, bs, lc):
    t = uv.shape[0]
    return pl.pallas_call(
        functools.partial(_gmlp_kernel, lc=lc),
        grid=(t // lc,),
        in_specs=[
            pl.BlockSpec((lc, 2 * BW), lambda i: (i, 0)),
            pl.BlockSpec((1, BW), lambda i: (0, 0)),
            pl.BlockSpec((1, BW), lambda i: (0, 0)),
            pl.BlockSpec((GROUPS, lc, lc), lambda i: (0, 0, 0)),
            pl.BlockSpec((lc, GROUPS), lambda i: (0, 0)),
        ],
        out_specs=[
            pl.BlockSpec((lc, BW), lambda i: (i, 0)),
            pl.BlockSpec((lc, BW), lambda i: (i, 0)),
        ],
        out_shape=[
            jax.ShapeDtypeStruct((t, BW), BF16),
            jax.ShapeDtypeStruct((t, BW), F32),
        ],
        compiler_params=_cparams("parallel"),
        name="gmlp_mix",
    )(uv, ln_g.reshape(1, BW), ln_b.reshape(1, BW), ws[:, :lc, :lc], bs[:, :lc].T)


def _pool_kernel(u_ref, pre_ref, w_ref, sc_ref, y_ref, st_ref, pad_ref, *, seq, start_pos):
    pad_ref[0:POOL_PAD, :] = pre_ref[...]
    pad_ref[POOL_PAD:, :] = u_ref[...]
    pos = lax.broadcasted_iota(jnp.int32, (seq, 1), 0) + start_pos
    for g, win in enumerate(POOL_WINDOWS):
        cols = slice(g * GDIM, (g + 1) * GDIM)
        tok = pad_ref[POOL_PAD:, cols]
        acc = tok
        for k in range(1, win):
            acc = acc + pad_ref[POOL_PAD - k:POOL_PAD - k + seq, cols]
        cnt = jnp.minimum(pos + 1, win).astype(F32)
        d = acc / cnt - tok
        y = jnp.dot(d.astype(BF16), w_ref[g], preferred_element_type=F32)
        y_ref[:, cols] = (y * sc_ref[:, cols]).astype(y_ref.dtype)
    st_ref[...] = pad_ref[seq + 1:seq + POOL_PAD, :]


def pool_mix(mid, prefix, start_pos, pool_w, pool_scale, batch, seq):
    mid3 = mid.reshape(batch, seq, 4 * BW)
    pre = jnp.concatenate([jnp.zeros((batch, 1, BW), F32), prefix], axis=1)
    y, st = pl.pallas_call(
        functools.partial(_pool_kernel, seq=seq, start_pos=start_pos),
        grid=(batch,),
        in_specs=[
            pl.BlockSpec((None, seq, BW), lambda b: (b, 0, 0)),
            pl.BlockSpec((None, POOL_PAD, BW), lambda b: (b, 0, 0)),
            pl.BlockSpec((GROUPS, GDIM, GDIM), lambda b: (0, 0, 0)),
            pl.BlockSpec((1, BW), lambda b: (0, 0)),
        ],
        out_specs=[
            pl.BlockSpec((None, seq, BW), lambda b: (b, 0, 0)),
            pl.BlockSpec((None, POOL_STATE, BW), lambda b: (b, 0, 0)),
        ],
        out_shape=[
            jax.ShapeDtypeStruct((batch, seq, BW), BF16),
            jax.ShapeDtypeStruct((batch, POOL_STATE, BW), F32),
        ],
        scratch_shapes=[pltpu.VMEM((seq + POOL_PAD, BW), F32)],
        compiler_params=_cparams("parallel"),
        name="pool_mix",
    )(mid3, pre, pool_w.astype(BF16), pool_scale.reshape(1, BW))
    return y.reshape(batch * seq, BW), st


def _attn_chunks(q_ref, kcat_ref, vcat_ref, bias_ref, o_ref, *, n_chunks, cq, band, first_block):
    scale = HEAD_DIM ** -0.5
    heads_per_tile = LANES // HEAD_DIM
    lane_head = lax.broadcasted_iota(jnp.int32, (1, LANES), 1) // HEAD_DIM
    for tile in range(HEADS // heads_per_tile):
        cols = slice(tile * LANES, (tile + 1) * LANES)
        for i in range(n_chunks):
            rows = slice(i * cq, (i + 1) * cq)
            q = (q_ref[rows, cols] * scale).astype(BF16)
            k = kcat_ref[i * cq:i * cq + band, cols]
            v = vcat_ref[i * cq:i * cq + band, cols]
            out = None
            for j in range(heads_per_tile):
                own = lane_head == j
                s = lax.dot_general(jnp.where(own, q, jnp.zeros_like(q)), k, (((1,), (1,)), ((), ())),
                                    preferred_element_type=F32) + bias_ref[tile * heads_per_tile + j]
                if first_block is not None:
                    key = lax.broadcasted_iota(jnp.int32, (1, band), 1)
                    s = jnp.where(key >= first_block * (BAND_PAST - i * cq), s, NEG)
                m = jnp.max(s, axis=-1, keepdims=True)
                p = jnp.exp(s - m)
                l = jnp.sum(p, axis=-1, keepdims=True)
                o = jnp.dot(p.astype(BF16), v, preferred_element_type=F32) / l
                out = o if out is None else jnp.where(own, o, out)
            o_ref[rows, cols] = out.astype(o_ref.dtype)


def _attn_prompt_kernel(q_ref, kp_ref, ko_ref, vp_ref, vo_ref, bias_ref, o_ref, kcat_ref, vcat_ref):
    kcat_ref[0:BAND_PAST, :] = kp_ref[...].astype(BF16)
    kcat_ref[BAND_PAST:, :] = ko_ref[...].astype(BF16)
    vcat_ref[0:BAND_PAST, :] = vp_ref[...].astype(BF16)
    vcat_ref[BAND_PAST:, :] = vo_ref[...].astype(BF16)
    _attn_chunks(q_ref, kcat_ref, vcat_ref, bias_ref, o_ref, n_chunks=BAND_CHUNKS, cq=CHUNK,
                 band=BAND_PAST + CHUNK, first_block=(pl.program_id(1) == 0).astype(jnp.int32))


def _rel_bias_tile(rel_bias, qpos, kpos):
    rel = np.clip(qpos[:, None] - kpos[None, :], -REL_CLIP, REL_CLIP) + REL_CLIP
    return rel_bias[:, rel]


def attn_prompt(mid, rel_bias, batch, seq):
    mid3 = mid.reshape(batch, seq, 4 * BW)
    blk = BAND_PAST
    bias = _rel_bias_tile(rel_bias, np.arange(CHUNK), np.arange(BAND_PAST + CHUNK) - BAND_PAST)
    prev = lambda b, j: jnp.maximum(j - 1, 0)
    y = pl.pallas_call(
        _attn_prompt_kernel,
        grid=(batch, seq // blk),
        in_specs=[
            pl.BlockSpec((None, blk, BW), lambda b, j: (b, j, 1)),
            pl.BlockSpec((None, blk, BW), lambda b, j: (b, prev(b, j), 2)),
            pl.BlockSpec((None, blk, BW), lambda b, j: (b, j, 2)),
            pl.BlockSpec((None, blk, BW), lambda b, j: (b, prev(b, j), 3)),
            pl.BlockSpec((None, blk, BW), lambda b, j: (b, j, 3)),
            pl.BlockSpec((HEADS, CHUNK, BAND_PAST + CHUNK), lambda b, j: (0, 0, 0)),
        ],
        out_specs=pl.BlockSpec((None, blk, BW), lambda b, j: (b, j, 0)),
        out_shape=jax.ShapeDtypeStruct((batch, seq, BW), BF16),
        scratch_shapes=[pltpu.VMEM((2 * blk, BW), BF16), pltpu.VMEM((2 * blk, BW), BF16)],
        compiler_params=_cparams("parallel", "parallel"),
        name="attn_prompt",
    )(mid3, mid3, mid3, mid3, mid3, bias)
    return y.reshape(batch * seq, BW)


def _attn_sample_kernel(q_ref, kc_ref, kn_ref, vc_ref, vn_ref, bias_ref, o_ref, kcat_ref, vcat_ref, *, n_cache):
    kcat_ref[0:n_cache, :] = kc_ref[...].astype(BF16)
    kcat_ref[n_cache:, :] = kn_ref[...].astype(BF16)
    vcat_ref[0:n_cache, :] = vc_ref[...].astype(BF16)
    vcat_ref[n_cache:, :] = vn_ref[...].astype(BF16)
    seq = q_ref.shape[0]
    _attn_chunks(q_ref, kcat_ref, vcat_ref, bias_ref, o_ref, n_chunks=1, cq=seq, band=n_cache + seq,
                 first_block=None)


def attn_sample(mid, cache_k, cache_v, rel_bias, batch, seq):
    n_cache = cache_k.shape[1]
    assert PAST_LEN >= n_cache
    mid3 = mid.reshape(batch, seq, 4 * BW)
    ck = cache_k.reshape(batch, n_cache, BW)
    cv = cache_v.reshape(batch, n_cache, BW)
    bias = _rel_bias_tile(rel_bias, PAST_LEN + np.arange(seq), PAST_LEN - n_cache + np.arange(n_cache + seq))
    y = pl.pallas_call(
        functools.partial(_attn_sample_kernel, n_cache=n_cache),
        grid=(batch,),
        in_specs=[
            pl.BlockSpec((None, seq, BW), lambda b: (b, 0, 1)),
            pl.BlockSpec((None, n_cache, BW), lambda b: (b, 0, 0)),
            pl.BlockSpec((None, seq, BW), lambda b: (b, 0, 2)),
            pl.BlockSpec((None, n_cache, BW), lambda b: (b, 0, 0)),
            pl.BlockSpec((None, seq, BW), lambda b: (b, 0, 3)),
            pl.BlockSpec((HEADS, seq, n_cache + seq), lambda b: (0, 0, 0)),
        ],
        out_specs=pl.BlockSpec((None, seq, BW), lambda b: (b, 0, 0)),
        out_shape=jax.ShapeDtypeStruct((batch, seq, BW), BF16),
        scratch_shapes=[pltpu.VMEM((n_cache + seq, BW), BF16), pltpu.VMEM((n_cache + seq, BW), BF16)],
        compiler_params=_cparams("parallel"),
        name="attn_sample",
    )(mid3, ck, mid3, cv, mid3, bias)
    return y.reshape(batch * seq, BW)


def _mixer_out_kernel(x_ref, ya_ref, yb_ref, yc_ref, gate_ref, wb_ref, wo_ref, o_ref):
    acc = None
    for n, y_ref in enumerate((ya_ref, yb_ref, yc_ref)):
        proj = jnp.dot(y_ref[...], wb_ref[n], preferred_element_type=F32)
        term = gate_ref[:, n * D_MODEL:(n + 1) * D_MODEL].astype(F32) * proj
        acc = term if acc is None else acc + term
    o_ref[...] = x_ref[...] + jnp.dot(acc.astype(BF16), wo_ref[...], preferred_element_type=F32)


def mixer_out(x, ya, yb, yc, gates, w_branch, w_out):
    t = x.shape[0]
    tm = min(t, 512)
    row = lambda i: (i, 0)
    return pl.pallas_call(
        _mixer_out_kernel,
        grid=(t // tm,),
        in_specs=[
            pl.BlockSpec((tm, D_MODEL), row),
            pl.BlockSpec((tm, BW), row),
            pl.BlockSpec((tm, BW), row),
            pl.BlockSpec((tm, BW), row),
            pl.BlockSpec((tm, 3 * D_MODEL), row),
            pl.BlockSpec((3, BW, D_MODEL), lambda i: (0, 0, 0)),
            pl.BlockSpec((D_MODEL, D_MODEL), lambda i: (0, 0)),
        ],
        out_specs=pl.BlockSpec((tm, D_MODEL), row),
        out_shape=jax.ShapeDtypeStruct((t, D_MODEL), F32),
        compiler_params=_cparams("parallel"),
        name="mixer_out",
    )(x, ya, yb, yc, gates, w_branch, w_out)


def _extract_top(s, payload, k):
    r = float(s.shape[0])
    rows = lax.broadcasted_iota(jnp.int32, s.shape, 0).astype(F32)
    vals, pays = [], []
    for _ in range(k):
        m = jnp.max(s, axis=0, keepdims=True)
        idx = jnp.min(jnp.where(s == m, rows, r), axis=0, keepdims=True)
        sel = rows == idx
        vals.append(m)
        pays.append(idx if payload is None else jnp.max(jnp.where(sel, payload, -1.0), axis=0, keepdims=True))
        s = jnp.where(sel, -jnp.inf, s)
    return jnp.concatenate(vals, axis=0), jnp.concatenate(pays, axis=0)


def _pair_candidates(sv, si):
    k = PEER_TOPK
    sub = 8
    assert k == 2 * sub
    b_row = lax.broadcasted_iota(jnp.int32, (sub, LANES), 0)
    vals = [sv[0][0:1] + sv[1], sv[0][1:2] + sv[1][0:sub]]
    ids = [si[0][0:1] * PEER_NKEYS + si[1], si[0][1:2] * PEER_NKEYS + si[1][0:sub]]
    for a in range(2, sub):
        keep = b_row < k // (a + 1)
        vals.append(jnp.where(keep, sv[0][a:a + 1] + sv[1][0:sub], -jnp.inf))
        ids.append(si[0][a:a + 1] * PEER_NKEYS + si[1][0:sub])
    vals.append(sv[0][sub:k] + sv[1][0:1])
    ids.append(si[0][sub:k] * PEER_NKEYS + si[1][0:1])
    return jnp.concatenate(vals, axis=0), jnp.concatenate(ids, axis=0)


def _peer_query_kernel(x_ref, g_ref, wq_ref, sk_ref, hp_ref, idx_ref, gate_ref, q_ref, *, tm):
    hb = _rms(x_ref[...], g_ref[...]).astype(BF16)
    q_ref[...] = jnp.dot(hb, wq_ref[...], preferred_element_type=F32).astype(BF16)
    bits = lax.bitcast_convert_type(hb.astype(F32), jnp.int32)
    half = D_MODEL // 2
    hp_ref[...] = bits[:, half:] | lax.shift_right_logical(bits[:, :half], 16)

    def sub_block(sb, carry):
        tok = pl.ds(pl.multiple_of(sb * LANES, LANES), LANES)
        for hd in range(PEER_HEADS):
            sv, si = [], []
            for p in range(2):
                hp = hd * 2 + p
                q = q_ref[tok, hp * PEER_HALF:(hp + 1) * PEER_HALF]
                s = lax.dot_general(sk_ref[hp], q, (((1,), (1,)), ((), ())), preferred_element_type=F32)
                v, i = _extract_top(s, None, PEER_TOPK)
                sv.append(v)
                si.append(i)
            cand, eid = _pair_candidates(sv, si)
            tv, te = _extract_top(cand, eid, PEER_TOPK)
            e = jnp.exp(tv - tv[0:1])
            gate = e / jnp.sum(e, axis=0, keepdims=True)
            idx_ref[sb, hd * PEER_TOPK:(hd + 1) * PEER_TOPK, :] = te.astype(jnp.int32)
            gate_ref[sb, hd * PEER_TOPK:(hd + 1) * PEER_TOPK, :] = gate
        return carry

    lax.fori_loop(0, tm // LANES, sub_block, 0)


def peer_query(x, norm_g, wq, subkeys):
    t = x.shape[0]
    tm = min(t, 512)
    nq = wq.shape[1]
    nsb = tm // LANES
    return pl.pallas_call(
        functools.partial(_peer_query_kernel, tm=tm),
        grid=(t // tm,),
        in_specs=[
            pl.BlockSpec((tm, D_MODEL), lambda i: (i, 0)),
            pl.BlockSpec((1, D_MODEL), lambda i: (0, 0)),
            pl.BlockSpec((D_MODEL, nq), lambda i: (0, 0)),
            pl.BlockSpec((2 * PEER_HEADS, PEER_NKEYS, PEER_HALF), lambda i: (0, 0, 0)),
        ],
        out_specs=[
            pl.BlockSpec((tm, D_MODEL // 2), lambda i: (i, 0)),
            pl.BlockSpec((nsb, PEER_SLOTS, LANES), lambda i: (i, 0, 0)),
            pl.BlockSpec((nsb, PEER_SLOTS, LANES), lambda i: (i, 0, 0)),
        ],
        out_shape=[
            jax.ShapeDtypeStruct((t, D_MODEL // 2), jnp.int32),
            jax.ShapeDtypeStruct((t // LANES, PEER_SLOTS, LANES), jnp.int32),
            jax.ShapeDtypeStruct((t // LANES, PEER_SLOTS, LANES), F32),
        ],
        scratch_shapes=[pltpu.VMEM((tm, nq), BF16)],
        compiler_params=_cparams("parallel"),
        name="peer_query",
    )(x, norm_g.reshape(1, D_MODEL), wq, subkeys)


HALF_D = D_MODEL // 2
SC_LANES = 16
SC_UNIT_ROWS = 32
SC_UNITS_PER_TOKEN = PEER_SLOTS // SC_UNIT_ROWS
SC_MAX_TOKEN_BLOCK = 8
SC_BUFFERS = 3
SC_ROW_GROUP = 8
SC_HALF_VECS = HALF_D // SC_LANES
SC_OUT_VECS = 8
SC_BF16_TERMS = 4
SC_ROW_TILES = D_MODEL // LANES


def pack_expert_tables(peer_u, peer_v):
    def pack(x):
        b = lax.bitcast_convert_type(x.astype(BF16), jnp.uint16).astype(jnp.uint32)
        return (b[:, HALF_D:] << 16) | b[:, :HALF_D]

    rows = lax.bitcast_convert_type(jnp.concatenate([pack(peer_u), pack(peer_v)], axis=1), jnp.int32)
    return rows.reshape(rows.shape[0], SC_ROW_TILES, LANES)


def sc_peer_experts(table, idx, gates, hp):
    t = hp.shape[0]
    info = plsc.get_sparse_core_info()
    n_workers = info.num_cores * info.num_subcores
    tpw = t // n_workers
    token_block = min(tpw, SC_MAX_TOKEN_BLOCK)
    assert t % (n_workers * token_block) == 0
    units = token_block * SC_UNITS_PER_TOKEN
    mesh = plsc.VectorSubcoreMesh(core_axis_name="core", subcore_axis_name="subcore")
    hi_mask = jnp.int32(-65536)
    gelu_c = math.sqrt(2.0 / math.pi)

    @functools.partial(
        pl.kernel,
        out_type=jax.ShapeDtypeStruct((t, D_MODEL), F32),
        mesh=mesh,
        scratch_types=[
            pltpu.VMEM((token_block * PEER_SLOTS,), jnp.int32),
            pltpu.VMEM((token_block * PEER_SLOTS,), F32),
            pltpu.VMEM((token_block, HALF_D), jnp.int32),
            pltpu.VMEM((token_block, D_MODEL), F32),
            pltpu.VMEM((SC_BUFFERS, SC_UNIT_ROWS, SC_ROW_TILES, LANES), jnp.int32),
            pltpu.VMEM((SC_UNIT_ROWS, SC_LANES), F32),
            pltpu.VMEM((SC_UNIT_ROWS,), jnp.int32),
            pltpu.SemaphoreType.DMA((SC_BUFFERS,)),
        ],
        compiler_params=pltpu.CompilerParams(needs_layout_passes=False),
        name="peer_sc_experts",
    )
    def kern(table_hbm, idx_hbm, gate_hbm, h_hbm, out_hbm, idx_v, gate_v, h_v, out_v, rows_v, part_v, coef_v, sem):
        wid = lax.axis_index("subcore") * info.num_cores + lax.axis_index("core")
        lane = lax.iota(jnp.int32, SC_LANES)
        zero = jnp.zeros((SC_LANES,), F32)

        def gather(unit, b):
            rows = idx_v.at[pl.ds(unit * SC_UNIT_ROWS, SC_UNIT_ROWS)]
            return pltpu.make_async_copy(table_hbm.at[rows], rows_v.at[b], sem.at[b])

        def row_vec(b, r, vec):
            per_tile_row = LANES // SC_LANES
            lane0 = pl.multiple_of((vec % per_tile_row) * SC_LANES, SC_LANES)
            return rows_v[b, r, vec // per_tile_row, pl.ds(lane0, SC_LANES)]

        def unpack(w):
            return lax.bitcast_convert_type(w << 16, F32), lax.bitcast_convert_type(w & hi_mask, F32)

        def as_pairs(w):
            return plsc.bitcast(w, BF16)

        def compute(unit, b):
            tl = unit // SC_UNITS_PER_TOKEN
            q = unit % SC_UNITS_PER_TOKEN

            def row_group(rg, carry):
                def kstep(k, accs):
                    hs = []
                    for j in range(SC_BF16_TERMS):
                        off = pl.multiple_of((k * SC_BF16_TERMS + j) * SC_LANES, SC_LANES)
                        hs.append(as_pairs(h_v[tl, pl.ds(off, SC_LANES)]))
                    new = []
                    for r in range(SC_ROW_GROUP):
                        p = None
                        for j in range(SC_BF16_TERMS):
                            term = as_pairs(row_vec(b, rg * SC_ROW_GROUP + r, k * SC_BF16_TERMS + j)) * hs[j]
                            p = term if p is None else p + term
                        lo, hi = unpack(plsc.bitcast(p, jnp.int32))
                        new.append(accs[r] + lo + hi)
                    return tuple(new)

                accs = lax.fori_loop(0, SC_HALF_VECS // SC_BF16_TERMS, kstep, (zero,) * SC_ROW_GROUP)
                for r in range(SC_ROW_GROUP):
                    part_v[rg * SC_ROW_GROUP + r, :] = accs[r]
                return carry

            lax.fori_loop(0, SC_UNIT_ROWS // SC_ROW_GROUP, row_group, 0)

            dots = []
            for i in range(SC_UNIT_ROWS // SC_LANES):
                rows = lane + i * SC_LANES
                terms = [plsc.load_gather(part_v, [rows, jnp.full((SC_LANES,), l, jnp.int32)])
                         for l in range(SC_LANES)]
                while len(terms) > 1:
                    terms = [terms[j] + terms[j + 1] for j in range(0, len(terms), 2)]
                dots.append(terms[0])
            for i, a in enumerate(dots):
                z = gelu_c * (a + 0.044715 * (a * a * a))
                act = a / (1.0 + jnp.exp(-2.0 * z))
                slot = pl.multiple_of(tl * PEER_SLOTS + q * SC_UNIT_ROWS + i * SC_LANES, SC_LANES)
                bits = lax.bitcast_convert_type(gate_v[pl.ds(slot, SC_LANES)] * act, jnp.int32)
                top = (bits + 0x7FFF + ((bits >> 16) & 1)) & hi_mask
                coef_v[pl.ds(i * SC_LANES, SC_LANES)] = top | lax.shift_right_logical(top, 16)

            def out_pass(dq, carry):
                def row_quad(rq, accs):
                    per_vec = SC_LANES // SC_BF16_TERMS
                    cvec = coef_v[pl.ds(pl.multiple_of((rq // per_vec) * SC_LANES, SC_LANES), SC_LANES)]
                    cs = []
                    for j in range(SC_BF16_TERMS):
                        src = jnp.full((SC_LANES,), (rq % per_vec) * SC_BF16_TERMS + j, jnp.int32)
                        cs.append(as_pairs(cvec.at[src].get(mode="promise_in_bounds")))
                    new = []
                    for k in range(SC_OUT_VECS):
                        p = None
                        for j in range(SC_BF16_TERMS):
                            w = row_vec(b, rq * SC_BF16_TERMS + j, SC_HALF_VECS + dq * SC_OUT_VECS + k)
                            term = as_pairs(w) * cs[j]
                            p = term if p is None else p + term
                        lo, hi = unpack(plsc.bitcast(p, jnp.int32))
                        new.append(accs[2 * k] + lo)
                        new.append(accs[2 * k + 1] + hi)
                    return tuple(new)

                accs = lax.fori_loop(0, SC_UNIT_ROWS // SC_BF16_TERMS, row_quad, (zero,) * (2 * SC_OUT_VECS))
                for k in range(SC_OUT_VECS):
                    off = pl.multiple_of((dq * SC_OUT_VECS + k) * SC_LANES, SC_LANES)
                    plsc.addupdate(out_v.at[tl, pl.ds(off, SC_LANES)], accs[2 * k])
                    plsc.addupdate(out_v.at[tl, pl.ds(HALF_D + off, SC_LANES)], accs[2 * k + 1])
                return carry

            lax.fori_loop(0, SC_HALF_VECS // SC_OUT_VECS, out_pass, 0)

        @pl.loop(0, tpw // token_block)
        def _(blk):
            tok0 = wid * tpw + blk * token_block
            slots = pl.ds(tok0 * PEER_SLOTS, token_block * PEER_SLOTS)
            pltpu.sync_copy(idx_hbm.at[slots], idx_v)
            pltpu.sync_copy(gate_hbm.at[slots], gate_v)
            pltpu.sync_copy(h_hbm.at[pl.ds(tok0, token_block)], h_v)

            @pl.loop(0, token_block)
            def _(tl):
                @pl.loop(0, D_MODEL // SC_LANES)
                def _(k):
                    out_v[tl, pl.ds(pl.multiple_of(k * SC_LANES, SC_LANES), SC_LANES)] = zero

            for u in range(SC_BUFFERS - 1):
                gather(u, u).start()

            @pl.loop(0, units)
            def _(unit):
                ahead = unit + SC_BUFFERS - 1

                @pl.when(ahead < units)
                def _():
                    gather(ahead, ahead % SC_BUFFERS).start()

                b = unit % SC_BUFFERS
                gather(unit, b).wait()
                compute(unit, b)

            pltpu.sync_copy(out_v, out_hbm.at[pl.ds(tok0, token_block)])

    return kern(table, idx, gates, hp)


TC_PEER_TOKENS = 8
TC_PEER_BUFFERS = 4
U_TILE_ROWS = SC_ROW_TILES // 2


def _tc_peer_kernel(idx_ref, idx_next_ref, gate_ref, hp_ref, table_hbm, o_ref, rows_buf, a_buf, c_buf, sem):
    step = pl.program_id(0)
    hi_mask = jnp.int32(-65536)

    def unpack(w):
        return lax.bitcast_convert_type(w << 16, F32), lax.bitcast_convert_type(w & hi_mask, F32)

    def row_copy(ids_ref, tok, e, slot):
        return pltpu.make_async_copy(table_hbm.at[ids_ref[tok * PEER_SLOTS + e]], rows_buf.at[slot, e], sem.at[slot])

    def wait_rows(slot):
        pltpu.make_async_copy(table_hbm.at[pl.ds(0, PEER_SLOTS)], rows_buf.at[slot], sem.at[slot]).wait()

    ahead = TC_PEER_BUFFERS - 1

    @pl.when(step == 0)
    def _():
        for tok in range(ahead):
            def first(e, c, tok=tok):
                row_copy(idx_ref, tok, e, tok).start()
                return c

            lax.fori_loop(0, PEER_SLOTS, first, 0, unroll=8)

    gates_t = gate_ref[...].T
    zeros_v = jnp.zeros((SC_ROW_TILES - U_TILE_ROWS, LANES), F32)
    for t in range(TC_PEER_TOKENS):
        slot = t % TC_PEER_BUFFERS
        nxt = t + ahead
        nxt_slot = nxt % TC_PEER_BUFFERS
        wait_rows(slot)
        h_lo, h_hi = unpack(hp_ref[t])
        h_lo = jnp.concatenate([h_lo, zeros_v], axis=0)
        h_hi = jnp.concatenate([h_hi, zeros_v], axis=0)

        def u_body(e, c, slot=slot, nxt=nxt, nxt_slot=nxt_slot, h_lo=h_lo, h_hi=h_hi):
            if nxt >= TC_PEER_TOKENS:
                @pl.when(step + 1 < pl.num_programs(0))
                def _():
                    row_copy(idx_next_ref, nxt - TC_PEER_TOKENS, e, nxt_slot).start()
            else:
                row_copy(idx_ref, nxt, e, nxt_slot).start()
            lo, hi = unpack(rows_buf[slot, e])
            a_buf[pl.ds(e, 1), :] = jnp.sum(lo * h_lo + hi * h_hi, axis=0, keepdims=True)
            return c

        lax.fori_loop(0, PEER_SLOTS, u_body, 0, unroll=8)
        a = jnp.sum(a_buf[...], axis=1, keepdims=True)
        c_buf[...] = jnp.broadcast_to(gates_t[:, t:t + 1] * _gelu(a), (PEER_SLOTS, LANES))

        def v_body(e, acc, slot=slot):
            lo, hi = unpack(rows_buf[slot, e])
            c = c_buf[pl.ds(e, 1), :]
            return acc[0] + c * lo, acc[1] + c * hi

        zero = jnp.zeros((SC_ROW_TILES, LANES), F32)
        acc_lo, acc_hi = lax.fori_loop(0, PEER_SLOTS, v_body, (zero, zero), unroll=8)
        o_ref[t, 0:U_TILE_ROWS, :] = acc_lo[U_TILE_ROWS:, :]
        o_ref[t, U_TILE_ROWS:, :] = acc_hi[U_TILE_ROWS:, :]


def tc_peer_experts(table, idx, gates, hp):
    t = hp.shape[0]
    assert t % TC_PEER_TOKENS == 0 and TC_PEER_TOKENS % TC_PEER_BUFFERS == 0
    n_steps = t // TC_PEER_TOKENS
    ids = TC_PEER_TOKENS * PEER_SLOTS
    smem = functools.partial(pl.BlockSpec, memory_space=pltpu.SMEM)
    out = pl.pallas_call(
        _tc_peer_kernel,
        grid=(n_steps,),
        in_specs=[
            smem((ids,), lambda i: (i,)),
            smem((ids,), lambda i: (jnp.minimum(i + 1, n_steps - 1),)),
            pl.BlockSpec((TC_PEER_TOKENS, PEER_SLOTS), lambda i: (i, 0)),
            pl.BlockSpec((TC_PEER_TOKENS, U_TILE_ROWS, LANES), lambda i: (i, 0, 0)),
            pl.BlockSpec(memory_space=pl.ANY),
        ],
        out_specs=pl.BlockSpec((TC_PEER_TOKENS, SC_ROW_TILES, LANES), lambda i: (i, 0, 0)),
        out_shape=jax.ShapeDtypeStruct((t, SC_ROW_TILES, LANES), F32),
        scratch_shapes=[
            pltpu.VMEM((TC_PEER_BUFFERS, PEER_SLOTS, SC_ROW_TILES, LANES), jnp.int32),
            pltpu.VMEM((PEER_SLOTS, LANES), F32),
            pltpu.VMEM((PEER_SLOTS, LANES), F32),
            pltpu.SemaphoreType.DMA((TC_PEER_BUFFERS,)),
        ],
        compiler_params=pltpu.CompilerParams(dimension_semantics=("arbitrary",), vmem_limit_bytes=VMEM_LIMIT,
                                             disable_bounds_checks=True),
        name="tc_peer_experts",
    )(idx, idx, gates, hp.reshape(t, U_TILE_ROWS, LANES), table)
    return out.reshape(t, D_MODEL)


TC_PEER_SHARE = 8
TC_PEER_QUANTUM = 1024


def peer_block(x, norm_g, wq, subkeys, table, prev_sc=None):
    t = x.shape[0]
    hp, idx3, gate3 = peer_query(x, norm_g, wq, subkeys)
    idx = jnp.transpose(idx3, (0, 2, 1)).reshape(t * PEER_SLOTS)
    gates = jnp.transpose(gate3, (0, 2, 1)).reshape(t * PEER_SLOTS)
    t_tc = (t // TC_PEER_SHARE) // TC_PEER_QUANTUM * TC_PEER_QUANTUM
    t_sc = t - t_tc
    idx_sc = idx[:t_sc * PEER_SLOTS]
    if prev_sc is not None:
        idx_sc, _ = lax.optimization_barrier((idx_sc, prev_sc))
    on_sc = sc_peer_experts(table, idx_sc, gates[:t_sc * PEER_SLOTS], hp[:t_sc])
    if t_tc == 0:
        return on_sc, (hp,), on_sc
    on_tc = tc_peer_experts(table, idx[t_sc * PEER_SLOTS:], gates[t_sc * PEER_SLOTS:].reshape(t_tc, PEER_SLOTS),
                            hp[t_sc:])
    return jnp.concatenate([on_sc, on_tc], axis=0), (hp, on_tc), on_sc


def _ple_kernel(x_ref, ffn_ref, p_ref, g_ref, wg_ref, wp_ref, gf_ref, o_ref, *, final):
    x = x_ref[...] + ffn_ref[...]
    gate = _sigmoid(jnp.dot(_rms(x, g_ref[...]).astype(BF16), wg_ref[...], preferred_element_type=F32))
    emb = jnp.dot(p_ref[...].astype(BF16), wp_ref[...], preferred_element_type=F32)
    y = x + gate * emb
    o_ref[...] = _rms(y, gf_ref[...]) if final else y


def ple_block(x, ffn, p, norm_g, w_gate, w_proj, norm_final, final):
    t = x.shape[0]
    tm = min(t, 512)
    row = lambda i: (i, 0)
    full = lambda i: (0, 0)
    return pl.pallas_call(
        functools.partial(_ple_kernel, final=final),
        grid=(t // tm,),
        in_specs=[
            pl.BlockSpec((tm, D_MODEL), row),
            pl.BlockSpec((tm, D_MODEL), row),
            pl.BlockSpec((tm, PLE_DIM), row),
            pl.BlockSpec((1, D_MODEL), full),
            pl.BlockSpec((D_MODEL, D_MODEL), full),
            pl.BlockSpec((PLE_DIM, D_MODEL), full),
            pl.BlockSpec((1, D_MODEL), full),
        ],
        out_specs=pl.BlockSpec((tm, D_MODEL), row),
        out_shape=jax.ShapeDtypeStruct((t, D_MODEL), F32),
        compiler_params=_cparams("parallel"),
        name="ple_block",
    )(x, ffn, p, norm_g.reshape(1, D_MODEL), w_gate, w_proj, norm_final.reshape(1, D_MODEL))


def _prompt_groups(batch):
    sizes = []
    while sum(sizes) < batch:
        nxt = 1 if len(sizes) < 2 else -(-sizes[-1] * 7 // 5)
        sizes.append(min(nxt, batch - sum(sizes)))
    return sizes
def _trunk_layer(x, ple, lw, batch, seq, pool_prefix, pool_start, cache, final, after=None, prev_sc=None):
    if after is not None:
        x, _ = lax.optimization_barrier((x, after))
    uv = norm_matmul(x, lw["norm_mix"], lw["w_uv"], "gelu", F32, 2 * BW)
    mid = norm_matmul(x, lw["norm_mix"], lw["w_mid"], "none", F32, 4 * BW)
    gates = norm_matmul(x, lw["norm_mix"], lw["w_gates"], "sigmoid", BF16, D_MODEL)
    ya, vn = gmlp_mix(uv, lw["gmlp_ln_g"], lw["gmlp_ln_b"], lw["gmlp_ws"], lw["gmlp_bs"], min(seq, GMLP_CHUNK))
    yb, pool_state = pool_mix(mid, pool_prefix, pool_start, lw["pool_w"], lw["pool_scale"], batch, seq)
    if cache is None:
        yc = attn_prompt(mid, lw["rel_bias"], batch, seq)
    else:
        yc = attn_sample(mid, cache[0], cache[1], lw["rel_bias"], batch, seq)
    x = mixer_out(x, ya, yb, yc, gates, lw["w_branch"], lw["w_out"])
    ffn, stage, sc_out = peer_block(x, lw["norm_ffn"], lw["peer_wq"], lw["peer_subkeys"], lw["peer_table"], prev_sc)
    x = ple_block(x, ffn, ple, lw["norm_ple"], lw["ple_gate"], lw["ple_proj"], lw["norm_final"], final)
    return x, mid, pool_state, vn, stage, sc_out


def kernel(x_prompt, x_sample, cache_attn_k, cache_attn_v, state_pool, p_prompt, p_sample, norm_mix, w_in, gmlp_ln_g, gmlp_ln_b, gmlp_ws, gmlp_bs, pool_w, pool_scale, attn_rel_bias, w_branch, w_out, norm_ffn, peer_wq, peer_subkeys, peer_u, peer_v, norm_ple, ple_gate, ple_proj, norm_final):
    bp, lp, _ = x_prompt.shape
    bs, ls, _ = x_sample.shape
    assert lp % BAND_PAST == 0 and lp % GMLP_CHUNK == 0 and ls <= CHUNK
    n_keep = min(BAND_PAST, lp)
    sizes = _prompt_groups(bp)
    starts = [sum(sizes[:g]) for g in range(len(sizes))]
    xg = [x_prompt[a:a + n].reshape(n * lp, D_MODEL) for a, n in zip(starts, sizes)]
    xs = x_sample.reshape(bs * ls, D_MODEL)
    outs = {k: [] for k in ("pk", "pv", "pps", "sk", "sv", "sps", "sgv")}
    after = prev_sc = None
    stages = []
    for i in range(DEPTH):
        tab_u, tab_v = peer_u[i], peer_v[i]
        if stages:
            tab_u, tab_v, _ = lax.optimization_barrier((tab_u, tab_v, stages[len(stages) // 2]))
        stages = []
        w_in_b = w_in[i].astype(BF16)
        lw = dict(
            norm_mix=norm_mix[i],
            w_uv=w_in_b[:, :2 * BW],
            w_mid=w_in_b[:, 2 * BW:6 * BW],
            w_gates=w_in_b[:, 6 * BW:],
            gmlp_ln_g=gmlp_ln_g[i], gmlp_ln_b=gmlp_ln_b[i], gmlp_ws=gmlp_ws[i], gmlp_bs=gmlp_bs[i],
            pool_w=pool_w[i], pool_scale=pool_scale[i], rel_bias=attn_rel_bias[i],
            w_branch=w_branch[i].astype(BF16), w_out=w_out[i].astype(BF16),
            norm_ffn=norm_ffn[i], peer_wq=peer_wq[i].astype(BF16),
            peer_subkeys=peer_subkeys[i].reshape(2 * PEER_HEADS, PEER_NKEYS, PEER_HALF).astype(BF16),
            peer_table=pack_expert_tables(tab_u, tab_v),
            norm_ple=norm_ple[i], ple_gate=ple_gate[i].astype(BF16), ple_proj=ple_proj[i].astype(BF16),
            norm_final=norm_final,
        )
        final = i == DEPTH - 1
        pk, pv, pps = [], [], []
        for g, (a, bg) in enumerate(zip(starts, sizes)):
            ple_g = p_prompt[i, a:a + bg].reshape(bg * lp, PLE_DIM)
            zero_prefix = jnp.zeros((bg, POOL_STATE, BW), F32)
            xg[g], mid_p, ps_p, _, after, prev_sc = _trunk_layer(xg[g], ple_g, lw, bg, lp, zero_prefix, 0, None, final,
                                                                 after, prev_sc)
            stages.append(after)
            mid_p = mid_p.reshape(bg, lp, 4 * BW)
            pk.append(mid_p[:, lp - n_keep:, 2 * BW:3 * BW].reshape(bg, n_keep, HEADS, HEAD_DIM))
            pv.append(mid_p[:, lp - n_keep:, 3 * BW:].reshape(bg, n_keep, HEADS, HEAD_DIM))
            pps.append(ps_p)
        outs["pk"].append(jnp.concatenate(pk, axis=0))
        outs["pv"].append(jnp.concatenate(pv, axis=0))
        outs["pps"].append(jnp.concatenate(pps, axis=0))
        xs, mid_s, ps_s, vn_s, _, _ = _trunk_layer(xs, p_sample[i].reshape(bs * ls, PLE_DIM), lw, bs, ls, state_pool[i],
                                                PAST_LEN, (cache_attn_k[i], cache_attn_v[i]), final)
        mid_s = mid_s.reshape(bs, ls, 4 * BW)
        outs["sk"].append(mid_s[:, :, 2 * BW:3 * BW].reshape(bs, ls, HEADS, HEAD_DIM))
        outs["sv"].append(mid_s[:, :, 3 * BW:].reshape(bs, ls, HEADS, HEAD_DIM))
        outs["sps"].append(ps_s)
        outs["sgv"].append(vn_s.reshape(bs, ls, BW))
    st = lambda k: jnp.stack(outs[k])
    y_prompt = jnp.concatenate(xg, axis=0).reshape(bp, lp, D_MODEL)
    return (y_prompt, xs.reshape(bs, ls, D_MODEL), st("pk"), st("pv"), st("pps"),
            st("sk"), st("sv"), st("sps"), st("sgv"))
```

```python
import functools
import math

import jax
import jax.numpy as jnp
import numpy as np
from jax import lax
from jax.experimental import pallas as pl
from jax.experimental.pallas import tpu as pltpu
from jax.experimental.pallas import tpu_sc as plsc

F32 = jnp.float32
BF16 = jnp.bfloat16

D_MODEL = 1024
DEPTH = 2
CHUNK = 64
EPS = 1e-6
BW = D_MODEL // 2
GMLP_CHUNK = 128
GROUPS = 4
GDIM = BW // GROUPS
POOL_WINDOWS = (2, 4, 8, 16)
POOL_STATE = 15
POOL_PAD = 16
HEADS = 8
HEAD_DIM = BW // HEADS
BAND_CHUNKS = 8
BAND_PAST = BAND_CHUNKS * CHUNK
REL_CLIP = 128
PAST_LEN = 4096
PEER_HEADS = 8
PEER_NKEYS = 128
PEER_HALF = 128
PEER_TOPK = 16
PEER_SLOTS = PEER_HEADS * PEER_TOPK
PLE_DIM = 256

LANES = 128
VMEM_LIMIT = 56 * 1024 * 1024
NEG = -1e30


def _cparams(*sem):
    return pltpu.CompilerParams(dimension_semantics=sem, vmem_limit_bytes=VMEM_LIMIT)


def _rms(x, g):
    ms = jnp.mean(x * x, axis=-1, keepdims=True)
    return x * lax.rsqrt(ms + EPS) * g


def _gelu(x):
    c = math.sqrt(2.0 / math.pi)
    return 0.5 * x * (1.0 + jnp.tanh(c * (x + 0.044715 * (x * x * x))))


def _sigmoid(x):
    return 1.0 / (1.0 + jnp.exp(-x))


_ACTS = {"gelu": _gelu, "sigmoid": _sigmoid, "none": lambda z: z}


def _norm_matmul_kernel(x_ref, g_ref, w_ref, o_ref, h_ref, *, act):
    @pl.when(pl.program_id(1) == 0)
    def _():
        h_ref[...] = _rms(x_ref[...], g_ref[...]).astype(BF16)

    z = jnp.dot(h_ref[...], w_ref[...], preferred_element_type=F32)
    o_ref[...] = _ACTS[act](z).astype(o_ref.dtype)


def norm_matmul(x, g, w, act, out_dtype, tn):
    t, d = x.shape
    n = w.shape[1]
    tm = min(t, 1024)
    return pl.pallas_call(
        functools.partial(_norm_matmul_kernel, act=act),
        grid=(t // tm, n // tn),
        in_specs=[
            pl.BlockSpec((tm, d), lambda i, j: (i, 0)),
            pl.BlockSpec((1, d), lambda i, j: (0, 0)),
            pl.BlockSpec((d, tn), lambda i, j: (0, j)),
        ],
        out_specs=pl.BlockSpec((tm, tn), lambda i, j: (i, j)),
        out_shape=jax.ShapeDtypeStruct((t, n), out_dtype),
        scratch_shapes=[pltpu.VMEM((tm, d), BF16)],
        compiler_params=_cparams("parallel", "arbitrary"),
        name="norm_matmul_" + act,
    )(x, g.reshape(1, d), w)


def _gmlp_kernel(uv_ref, lng_ref, lnb_ref, ws_ref, bst_ref, y_ref, vn_ref, *, lc):
    u = uv_ref[:, :BW]
    v = uv_ref[:, BW:]
    mu = jnp.mean(v, axis=-1, keepdims=True)
    vc = v - mu
    var = jnp.mean(vc * vc, axis=-1, keepdims=True)
    vn = vc * lax.rsqrt(var + EPS) * lng_ref[...] + lnb_ref[...]
    vn_ref[...] = vn
    row = lax.broadcasted_iota(jnp.int32, (lc, lc), 0) // CHUNK
    col = lax.broadcasted_iota(jnp.int32, (lc, lc), 1) // CHUNK
    causal = col <= row
    vnb = vn.astype(BF16)
    for g in range(GROUPS):
        w = jnp.where(causal, ws_ref[g], 0.0).astype(BF16)
        s = jnp.dot(w, vnb[:, g * GDIM:(g + 1) * GDIM], preferred_element_type=F32)
        s = s + bst_ref[:, g:g + 1]
        y_ref[:, g * GDIM:(g + 1) * GDIM] = (u[:, g * GDIM:(g + 1) * GDIM] * s).astype(y_ref.dtype)


def gmlp_mix(uv, ln_g, ln_b, ws, bs, lc):
    t = uv.shape[0]
    return pl.pallas_call(
        functools.partial(_gmlp_kernel, lc=lc),
        grid=(t // lc,),
        in_specs=[
            pl.BlockSpec((lc, 2 * BW), lambda i: (i, 0)),
            pl.BlockSpec((1, BW), lambda i: (0, 0)),
            pl.BlockSpec((1, BW), lambda i: (0, 0)),
            pl.BlockSpec((GROUPS, lc, lc), lambda i: (0, 0, 0)),
            pl.BlockSpec((lc, GROUPS), lambda i: (0, 0)),
        ],
        out_specs=[
            pl.BlockSpec((lc, BW), lambda i: (i, 0)),
            pl.BlockSpec((lc, BW), lambda i: (i, 0)),
        ],
        out_shape=[
            jax.ShapeDtypeStruct((t, BW), BF16),
            jax.ShapeDtypeStruct((t, BW), F32),
        ],
        compiler_params=_cparams("parallel"),
        name="gmlp_mix",
    )(uv, ln_g.reshape(1, BW), ln_b.reshape(1, BW), ws[:, :lc, :lc], bs[:, :lc].T)


def _pool_kernel(u_ref, pre_ref, w_ref, sc_ref, y_ref, st_ref, pad_ref, *, seq, start_pos):
    pad_ref[0:POOL_PAD, :] = pre_ref[...]
    pad_ref[POOL_PAD:, :] = u_ref[...]
    pos = lax.broadcasted_iota(jnp.int32, (seq, 1), 0) + start_pos
    for g, win in enumerate(POOL_WINDOWS):
        cols = slice(g * GDIM, (g + 1) * GDIM)
        tok = pad_ref[POOL_PAD:, cols]
        acc = tok
        for k in range(1, win):
            acc = acc + pad_ref[POOL_PAD - k:POOL_PAD - k + seq, cols]
        cnt = jnp.minimum(pos + 1, win).astype(F32)
        d = acc / cnt - tok
        y = jnp.dot(d.astype(BF16), w_ref[g], preferred_element_type=F32)
        y_ref[:, cols] = (y * sc_ref[:, cols]).astype(y_ref.dtype)
    st_ref[...] = pad_ref[seq + 1:seq + POOL_PAD, :]


def pool_mix(mid, prefix, start_pos, pool_w, pool_scale, batch, seq):
    mid3 = mid.reshape(batch, seq, 4 * BW)
    pre = jnp.concatenate([jnp.zeros((batch, 1, BW), F32), prefix], axis=1)
    y, st = pl.pallas_call(
        functools.partial(_pool_kernel, seq=seq, start_pos=start_pos),
        grid=(batch,),
        in_specs=[
            pl.BlockSpec((None, seq, BW), lambda b: (b, 0, 0)),
            pl.BlockSpec((None, POOL_PAD, BW), lambda b: (b, 0, 0)),
            pl.BlockSpec((GROUPS, GDIM, GDIM), lambda b: (0, 0, 0)),
            pl.BlockSpec((1, BW), lambda b: (0, 0)),
        ],
        out_specs=[
            pl.BlockSpec((None, seq, BW), lambda b: (b, 0, 0)),
            pl.BlockSpec((None, POOL_STATE, BW), lambda b: (b, 0, 0)),
        ],
        out_shape=[
            jax.ShapeDtypeStruct((batch, seq, BW), BF16),
            jax.ShapeDtypeStruct((batch, POOL_STATE, BW), F32),
        ],
        scratch_shapes=[pltpu.VMEM((seq + POOL_PAD, BW), F32)],
        compiler_params=_cparams("parallel"),
        name="pool_mix",
    )(mid3, pre, pool_w.astype(BF16), pool_scale.reshape(1, BW))
    return y.reshape(batch * seq, BW), st


def _attn_chunks(q_ref, kcat_ref, vcat_ref, bias_ref, o_ref, *, n_chunks, cq, band, first_block):
    scale = HEAD_DIM ** -0.5
    heads_per_tile = LANES // HEAD_DIM
    lane_head = lax.broadcasted_iota(jnp.int32, (1, LANES), 1) // HEAD_DIM
    for tile in range(HEADS // heads_per_tile):
        cols = slice(tile * LANES, (tile + 1) * LANES)
        for i in range(n_chunks):
            rows = slice(i * cq, (i + 1) * cq)
            q = (q_ref[rows, cols] * scale).astype(BF16)
            k = kcat_ref[i * cq:i * cq + band, cols]
            v = vcat_ref[i * cq:i * cq + band, cols]
            out = None
            for j in range(heads_per_tile):
                own = lane_head == j
                s = lax.dot_general(jnp.where(own, q, jnp.zeros_like(q)), k, (((1,), (1,)), ((), ())),
                                    preferred_element_type=F32) + bias_ref[tile * heads_per_tile + j]
                if first_block is not None:
                    key = lax.broadcasted_iota(jnp.int32, (1, band), 1)
                    s = jnp.where(key >= first_block * (BAND_PAST - i * cq), s, NEG)
                m = jnp.max(s, axis=-1, keepdims=True)
                p = jnp.exp(s - m)
                l = jnp.sum(p, axis=-1, keepdims=True)
                o = jnp.dot(p.astype(BF16), v, preferred_element_type=F32) / l
                out = o if out is None else jnp.where(own, o, out)
            o_ref[rows, cols] = out.astype(o_ref.dtype)


def _attn_prompt_kernel(q_ref, kp_ref, ko_ref, vp_ref, vo_ref, bias_ref, o_ref, kcat_ref, vcat_ref):
    kcat_ref[0:BAND_PAST, :] = kp_ref[...].astype(BF16)
    kcat_ref[BAND_PAST:, :] = ko_ref[...].astype(BF16)
    vcat_ref[0:BAND_PAST, :] = vp_ref[...].astype(BF16)
    vcat_ref[BAND_PAST:, :] = vo_ref[...].astype(BF16)
    _attn_chunks(q_ref, kcat_ref, vcat_ref, bias_ref, o_ref, n_chunks=BAND_CHUNKS, cq=CHUNK,
                 band=BAND_PAST + CHUNK, first_block=(pl.program_id(1) == 0).astype(jnp.int32))


def _rel_bias_tile(rel_bias, qpos, kpos):
    rel = np.clip(qpos[:, None] - kpos[None, :], -REL_CLIP, REL_CLIP) + REL_CLIP
    return rel_bias[:, rel]


def attn_prompt(mid, rel_bias, batch, seq):
    mid3 = mid.reshape(batch, seq, 4 * BW)
    blk = BAND_PAST
    bias = _rel_bias_tile(rel_bias, np.arange(CHUNK), np.arange(BAND_PAST + CHUNK) - BAND_PAST)
    prev = lambda b, j: jnp.maximum(j - 1, 0)
    y = pl.pallas_call(
        _attn_prompt_kernel,
        grid=(batch, seq // blk),
        in_specs=[
            pl.BlockSpec((None, blk, BW), lambda b, j: (b, j, 1)),
            pl.BlockSpec((None, blk, BW), lambda b, j: (b, prev(b, j), 2)),
            pl.BlockSpec((None, blk, BW), lambda b, j: (b, j, 2)),
            pl.BlockSpec((None, blk, BW), lambda b, j: (b, prev(b, j), 3)),
            pl.BlockSpec((None, blk, BW), lambda b, j: (b, j, 3)),
            pl.BlockSpec((HEADS, CHUNK, BAND_PAST + CHUNK), lambda b, j: (0, 0, 0)),
        ],
        out_specs=pl.BlockSpec((None, blk, BW), lambda b, j: (b, j, 0)),
        out_shape=jax.ShapeDtypeStruct((batch, seq, BW), BF16),
        scratch_shapes=[pltpu.VMEM((2 * blk, BW), BF16), pltpu.VMEM((2 * blk, BW), BF16)],
        compiler_params=_cparams("parallel", "parallel"),
        name="attn_prompt",
    )(mid3, mid3, mid3, mid3, mid3, bias)
    return y.reshape(batch * seq, BW)


def _attn_sample_kernel(q_ref, kc_ref, kn_ref, vc_ref, vn_ref, bias_ref, o_ref, kcat_ref, vcat_ref, *, n_cache):
    kcat_ref[0:n_cache, :] = kc_ref[...].astype(BF16)
    kcat_ref[n_cache:, :] = kn_ref[...].astype(BF16)
    vcat_ref[0:n_cache, :] = vc_ref[...].astype(BF16)
    vcat_ref[n_cache:, :] = vn_ref[...].astype(BF16)
    seq = q_ref.shape[0]
    _attn_chunks(q_ref, kcat_ref, vcat_ref, bias_ref, o_ref, n_chunks=1, cq=seq, band=n_cache + seq,
                 first_block=None)


def attn_sample(mid, cache_k, cache_v, rel_bias, batch, seq):
    n_cache = cache_k.shape[1]
    assert PAST_LEN >= n_cache
    mid3 = mid.reshape(batch, seq, 4 * BW)
    ck = cache_k.reshape(batch, n_cache, BW)
    cv = cache_v.reshape(batch, n_cache, BW)
    bias = _rel_bias_tile(rel_bias, PAST_LEN + np.arange(seq), PAST_LEN - n_cache + np.arange(n_cache + seq))
    y = pl.pallas_call(
        functools.partial(_attn_sample_kernel, n_cache=n_cache),
        grid=(batch,),
        in_specs=[
            pl.BlockSpec((None, seq, BW), lambda b: (b, 0, 1)),
            pl.BlockSpec((None, n_cache, BW), lambda b: (b, 0, 0)),
            pl.BlockSpec((None, seq, BW), lambda b: (b, 0, 2)),
            pl.BlockSpec((None, n_cache, BW), lambda b: (b, 0, 0)),
            pl.BlockSpec((None, seq, BW), lambda b: (b, 0, 3)),
            pl.BlockSpec((HEADS, seq, n_cache + seq), lambda b: (0, 0, 0)),
        ],
        out_specs=pl.BlockSpec((None, seq, BW), lambda b: (b, 0, 0)),
        out_shape=jax.ShapeDtypeStruct((batch, seq, BW), BF16),
        scratch_shapes=[pltpu.VMEM((n_cache + seq, BW), BF16), pltpu.VMEM((n_cache + seq, BW), BF16)],
        compiler_params=_cparams("parallel"),
        name="attn_sample",
    )(mid3, ck, mid3, cv, mid3, bias)
    return y.reshape(batch * seq, BW)


def _mixer_out_kernel(x_ref, ya_ref, yb_ref, yc_ref, gate_ref, wb_ref, wo_ref, o_ref):
    acc = None
    for n, y_ref in enumerate((ya_ref, yb_ref, yc_ref)):
        proj = jnp.dot(y_ref[...], wb_ref[n], preferred_element_type=F32)
        term = gate_ref[:, n * D_MODEL:(n + 1) * D_MODEL].astype(F32) * proj
        acc = term if acc is None else acc + term
    o_ref[...] = x_ref[...] + jnp.dot(acc.astype(BF16), wo_ref[...], preferred_element_type=F32)


def mixer_out(x, ya, yb, yc, gates, w_branch, w_out):
    t = x.shape[0]
    tm = min(t, 512)
    row = lambda i: (i, 0)
    return pl.pallas_call(
        _mixer_out_kernel,
        grid=(t // tm,),
        in_specs=[
            pl.BlockSpec((tm, D_MODEL), row),
            pl.BlockSpec((tm, BW), row),
            pl.BlockSpec((tm, BW), row),
            pl.BlockSpec((tm, BW), row),
            pl.BlockSpec((tm, 3 * D_MODEL), row),
            pl.BlockSpec((3, BW, D_MODEL), lambda i: (0, 0, 0)),
            pl.BlockSpec((D_MODEL, D_MODEL), lambda i: (0, 0)),
        ],
        out_specs=pl.BlockSpec((tm, D_MODEL), row),
        out_shape=jax.ShapeDtypeStruct((t, D_MODEL), F32),
        compiler_params=_cparams("parallel"),
        name="mixer_out",
    )(x, ya, yb, yc, gates, w_branch, w_out)


def _extract_top(s, payload, k):
    r = float(s.shape[0])
    rows = lax.broadcasted_iota(jnp.int32, s.shape, 0).astype(F32)
    vals, pays = [], []
    for _ in range(k):
        m = jnp.max(s, axis=0, keepdims=True)
        idx = jnp.min(jnp.where(s == m, rows, r), axis=0, keepdims=True)
        sel = rows == idx
        vals.append(m)
        pays.append(idx if payload is None else jnp.max(jnp.where(sel, payload, -1.0), axis=0, keepdims=True))
        s = jnp.where(sel, -jnp.inf, s)
    return jnp.concatenate(vals, axis=0), jnp.concatenate(pays, axis=0)


def _pair_candidates(sv, si):
    k = PEER_TOPK
    sub = 8
    assert k == 2 * sub
    b_row = lax.broadcasted_iota(jnp.int32, (sub, LANES), 0)
    vals = [sv[0][0:1] + sv[1], sv[0][1:2] + sv[1][0:sub]]
    ids = [si[0][0:1] * PEER_NKEYS + si[1], si[0][1:2] * PEER_NKEYS + si[1][0:sub]]
    for a in range(2, sub):
        keep = b_row < k // (a + 1)
        vals.append(jnp.where(keep, sv[0][a:a + 1] + sv[1][0:sub], -jnp.inf))
        ids.append(si[0][a:a + 1] * PEER_NKEYS + si[1][0:sub])
    vals.append(sv[0][sub:k] + sv[1][0:1])
    ids.append(si[0][sub:k] * PEER_NKEYS + si[1][0:1])
    return jnp.concatenate(vals, axis=0), jnp.concatenate(ids, axis=0)


def _peer_query_kernel(x_ref, g_ref, wq_ref, sk_ref, hp_ref, idx_ref, gate_ref, q_ref, *, tm):
    hb = _rms(x_ref[...], g_ref[...]).astype(BF16)
    q_ref[...] = jnp.dot(hb, wq_ref[...], preferred_element_type=F32).astype(BF16)
    bits = lax.bitcast_convert_type(hb.astype(F32), jnp.int32)
    half = D_MODEL // 2
    hp_ref[...] = bits[:, half:] | lax.shift_right_logical(bits[:, :half], 16)

    def sub_block(sb, carry):
        tok = pl.ds(pl.multiple_of(sb * LANES, LANES), LANES)
        for hd in range(PEER_HEADS):
            sv, si = [], []
            for p in range(2):
                hp = hd * 2 + p
                q = q_ref[tok, hp * PEER_HALF:(hp + 1) * PEER_HALF]
                s = lax.dot_general(sk_ref[hp], q, (((1,), (1,)), ((), ())), preferred_element_type=F32)
                v, i = _extract_top(s, None, PEER_TOPK)
                sv.append(v)
                si.append(i)
            cand, eid = _pair_candidates(sv, si)
            tv, te = _extract_top(cand, eid, PEER_TOPK)
            e = jnp.exp(tv - tv[0:1])
            gate = e / jnp.sum(e, axis=0, keepdims=True)
            idx_ref[sb, hd * PEER_TOPK:(hd + 1) * PEER_TOPK, :] = te.astype(jnp.int32)
            gate_ref[sb, hd * PEER_TOPK:(hd + 1) * PEER_TOPK, :] = gate
        return carry

    lax.fori_loop(0, tm // LANES, sub_block, 0)


def peer_query(x, norm_g, wq, subkeys):
    t = x.shape[0]
    tm = min(t, 512)
    nq = wq.shape[1]
    nsb = tm // LANES
    return pl.pallas_call(
        functools.partial(_peer_query_kernel, tm=tm),
        grid=(t // tm,),
        in_specs=[
            pl.BlockSpec((tm, D_MODEL), lambda i: (i, 0)),
            pl.BlockSpec((1, D_MODEL), lambda i: (0, 0)),
            pl.BlockSpec((D_MODEL, nq), lambda i: (0, 0)),
            pl.BlockSpec((2 * PEER_HEADS, PEER_NKEYS, PEER_HALF), lambda i: (0, 0, 0)),
        ],
        out_specs=[
            pl.BlockSpec((tm, D_MODEL // 2), lambda i: (i, 0)),
            pl.BlockSpec((nsb, PEER_SLOTS, LANES), lambda i: (i, 0, 0)),
            pl.BlockSpec((nsb, PEER_SLOTS, LANES), lambda i: (i, 0, 0)),
        ],
        out_shape=[
            jax.ShapeDtypeStruct((t, D_MODEL // 2), jnp.int32),
            jax.ShapeDtypeStruct((t // LANES, PEER_SLOTS, LANES), jnp.int32),
            jax.ShapeDtypeStruct((t // LANES, PEER_SLOTS, LANES), F32),
        ],
        scratch_shapes=[pltpu.VMEM((tm, nq), BF16)],
        compiler_params=_cparams("parallel"),
        name="peer_query",
    )(x, norm_g.reshape(1, D_MODEL), wq, subkeys)


HALF_D = D_MODEL // 2
SC_LANES = 16
SC_UNIT_ROWS = 32
SC_UNITS_PER_TOKEN = PEER_SLOTS // SC_UNIT_ROWS
SC_MAX_TOKEN_BLOCK = 8
SC_BUFFERS = 3
SC_ROW_GROUP = 8
SC_HALF_VECS = HALF_D // SC_LANES
SC_OUT_VECS = 8
SC_BF16_TERMS = 4
SC_ROW_TILES = D_MODEL // LANES


def pack_expert_tables(peer_u, peer_v):
    def pack(x):
        b = lax.bitcast_convert_type(x.astype(BF16), jnp.uint16).astype(jnp.uint32)
        return (b[:, HALF_D:] << 16) | b[:, :HALF_D]

    rows = lax.bitcast_convert_type(jnp.concatenate([pack(peer_u), pack(peer_v)], axis=1), jnp.int32)
    return rows.reshape(rows.shape[0], SC_ROW_TILES, LANES)


def sc_peer_experts(table, idx, gates, hp):
    t = hp.shape[0]
    info = plsc.get_sparse_core_info()
    n_workers = info.num_cores * info.num_subcores
    tpw = t // n_workers
    token_block = min(tpw, SC_MAX_TOKEN_BLOCK)
    assert t % (n_workers * token_block) == 0
    units = token_block * SC_UNITS_PER_TOKEN
    mesh = plsc.VectorSubcoreMesh(core_axis_name="core", subcore_axis_name="subcore")
    hi_mask = jnp.int32(-65536)
    gelu_c = math.sqrt(2.0 / math.pi)

    @functools.partial(
        pl.kernel,
        out_type=jax.ShapeDtypeStruct((t, D_MODEL), F32),
        mesh=mesh,
        scratch_types=[
            pltpu.VMEM((token_block * PEER_SLOTS,), jnp.int32),
            pltpu.VMEM((token_block * PEER_SLOTS,), F32),
            pltpu.VMEM((token_block, HALF_D), jnp.int32),
            pltpu.VMEM((token_block, D_MODEL), F32),
            pltpu.VMEM((SC_BUFFERS, SC_UNIT_ROWS, SC_ROW_TILES, LANES), jnp.int32),
            pltpu.VMEM((SC_UNIT_ROWS, SC_LANES), F32),
            pltpu.VMEM((SC_UNIT_ROWS,), jnp.int32),
            pltpu.SemaphoreType.DMA((SC_BUFFERS,)),
        ],
        compiler_params=pltpu.CompilerParams(needs_layout_passes=False),
        name="peer_sc_experts",
    )
    def kern(table_hbm, idx_hbm, gate_hbm, h_hbm, out_hbm, idx_v, gate_v, h_v, out_v, rows_v, part_v, coef_v, sem):
        wid = lax.axis_index("subcore") * info.num_cores + lax.axis_index("core")
        lane = lax.iota(jnp.int32, SC_LANES)
        zero = jnp.zeros((SC_LANES,), F32)

        def gather(unit, b):
            rows = idx_v.at[pl.ds(unit * SC_UNIT_ROWS, SC_UNIT_ROWS)]
            return pltpu.make_async_copy(table_hbm.at[rows], rows_v.at[b], sem.at[b])

        def row_vec(b, r, vec):
            per_tile_row = LANES // SC_LANES
            lane0 = pl.multiple_of((vec % per_tile_row) * SC_LANES, SC_LANES)
            return rows_v[b, r, vec // per_tile_row, pl.ds(lane0, SC_LANES)]

        def unpack(w):
            return lax.bitcast_convert_type(w << 16, F32), lax.bitcast_convert_type(w & hi_mask, F32)

        def as_pairs(w):
            return plsc.bitcast(w, BF16)

        def compute(unit, b):
            tl = unit // SC_UNITS_PER_TOKEN
            q = unit % SC_UNITS_PER_TOKEN

            def row_group(rg, carry):
                def kstep(k, accs):
                    hs = []
                    for j in range(SC_BF16_TERMS):
                        off = pl.multiple_of((k * SC_BF16_TERMS + j) * SC_LANES, SC_LANES)
                        hs.append(as_pairs(h_v[tl, pl.ds(off, SC_LANES)]))
                    new = []
                    for r in range(SC_ROW_GROUP):
                        p = None
                        for j in range(SC_BF16_TERMS):
                            term = as_pairs(row_vec(b, rg * SC_ROW_GROUP + r, k * SC_BF16_TERMS + j)) * hs[j]
                            p = term if p is None else p + term
                        lo, hi = unpack(plsc.bitcast(p, jnp.int32))
                        new.append(accs[r] + lo + hi)
                    return tuple(new)

                accs = lax.fori_loop(0, SC_HALF_VECS // SC_BF16_TERMS, kstep, (zero,) * SC_ROW_GROUP)
                for r in range(SC_ROW_GROUP):
                    part_v[rg * SC_ROW_GROUP + r, :] = accs[r]
                return carry

            lax.fori_loop(0, SC_UNIT_ROWS // SC_ROW_GROUP, row_group, 0)

            dots = []
            for i in range(SC_UNIT_ROWS // SC_LANES):
                rows = lane + i * SC_LANES
                terms = [plsc.load_gather(part_v, [rows, jnp.full((SC_LANES,), l, jnp.int32)])
                         for l in range(SC_LANES)]
                while len(terms) > 1:
                    terms = [terms[j] + terms[j + 1] for j in range(0, len(terms), 2)]
                dots.append(terms[0])
            for i, a in enumerate(dots):
                z = gelu_c * (a + 0.044715 * (a * a * a))
                act = a / (1.0 + jnp.exp(-2.0 * z))
                slot = pl.multiple_of(tl * PEER_SLOTS + q * SC_UNIT_ROWS + i * SC_LANES, SC_LANES)
                bits = lax.bitcast_convert_type(gate_v[pl.ds(slot, SC_LANES)] * act, jnp.int32)
                top = (bits + 0x7FFF + ((bits >> 16) & 1)) & hi_mask
                coef_v[pl.ds(i * SC_LANES, SC_LANES)] = top | lax.shift_right_logical(top, 16)

            def out_pass(dq, carry):
                def row_quad(rq, accs):
                    per_vec = SC_LANES // SC_BF16_TERMS
                    cvec = coef_v[pl.ds(pl.multiple_of((rq // per_vec) * SC_LANES, SC_LANES), SC_LANES)]
                    cs = []
                    for j in range(SC_BF16_TERMS):
                        src = jnp.full((SC_LANES,), (rq % per_vec) * SC_BF16_TERMS + j, jnp.int32)
                        cs.append(as_pairs(cvec.at[src].get(mode="promise_in_bounds")))
                    new = []
                    for k in range(SC_OUT_VECS):
                        p = None
                        for j in range(SC_BF16_TERMS):
                            w = row_vec(b, rq * SC_BF16_TERMS + j, SC_HALF_VECS + dq * SC_OUT_VECS + k)
                            term = as_pairs(w) * cs[j]
                            p = term if p is None else p + term
                        lo, hi = unpack(plsc.bitcast(p, jnp.int32))
                        new.append(accs[2 * k] + lo)
                        new.append(accs[2 * k + 1] + hi)
                    return tuple(new)

                accs = lax.fori_loop(0, SC_UNIT_ROWS // SC_BF16_TERMS, row_quad, (zero,) * (2 * SC_OUT_VECS))
                for k in range(SC_OUT_VECS):
                    off = pl.multiple_of((dq * SC_OUT_VECS + k) * SC_LANES, SC_LANES)
                    plsc.addupdate(out_v.at[tl, pl.ds(off, SC_LANES)], accs[2 * k])
                    plsc.addupdate(out_v.at[tl, pl.ds(HALF_D + off, SC_LANES)], accs[2 * k + 1])
                return carry

            lax.fori_loop(0, SC_HALF_VECS // SC_OUT_VECS, out_pass, 0)

        @pl.loop(0, tpw // token_block)
        def _(blk):
            tok0 = wid * tpw + blk * token_block
            slots = pl.ds(tok0 * PEER_SLOTS, token_block * PEER_SLOTS)
            pltpu.sync_copy(idx_hbm.at[slots], idx_v)
            pltpu.sync_copy(gate_hbm.at[slots], gate_v)
            pltpu.sync_copy(h_hbm.at[pl.ds(tok0, token_block)], h_v)

            @pl.loop(0, token_block)
            def _(tl):
                @pl.loop(0, D_MODEL // SC_LANES)
                def _(k):
                    out_v[tl, pl.ds(pl.multiple_of(k * SC_LANES, SC_LANES), SC_LANES)] = zero

            for u in range(SC_BUFFERS - 1):
                gather(u, u).start()

            @pl.loop(0, units)
            def _(unit):
                ahead = unit + SC_BUFFERS - 1

                @pl.when(ahead < units)
                def _():
                    gather(ahead, ahead % SC_BUFFERS).start()

                b = unit % SC_BUFFERS
                gather(unit, b).wait()
                compute(unit, b)

            pltpu.sync_copy(out_v, out_hbm.at[pl.ds(tok0, token_block)])

    return kern(table, idx, gates, hp)


TC_PEER_TOKENS = 8
TC_PEER_BUFFERS = 4
U_TILE_ROWS = SC_ROW_TILES // 2


def _tc_peer_kernel(idx_ref, idx_next_ref, gate_ref, hp_ref, table_hbm, o_ref, rows_buf, a_buf, c_buf, sem):
    step = pl.program_id(0)
    hi_mask = jnp.int32(-65536)

    def unpack(w):
        return lax.bitcast_convert_type(w << 16, F32), lax.bitcast_convert_type(w & hi_mask, F32)

    def row_copy(ids_ref, tok, e, slot):
        return pltpu.make_async_copy(table_hbm.at[ids_ref[tok * PEER_SLOTS + e]], rows_buf.at[slot, e], sem.at[slot])

    def wait_rows(slot):
        pltpu.make_async_copy(table_hbm.at[pl.ds(0, PEER_SLOTS)], rows_buf.at[slot], sem.at[slot]).wait()

    ahead = TC_PEER_BUFFERS - 1

    @pl.when(step == 0)
    def _():
        for tok in range(ahead):
            def first(e, c, tok=tok):
                row_copy(idx_ref, tok, e, tok).start()
                return c

            lax.fori_loop(0, PEER_SLOTS, first, 0, unroll=8)

    gates_t = gate_ref[...].T
    zeros_v = jnp.zeros((SC_ROW_TILES - U_TILE_ROWS, LANES), F32)
    for t in range(TC_PEER_TOKENS):
        slot = t % TC_PEER_BUFFERS
        nxt = t + ahead
        nxt_slot = nxt % TC_PEER_BUFFERS
        wait_rows(slot)
        h_lo, h_hi = unpack(hp_ref[t])
        h_lo = jnp.concatenate([h_lo, zeros_v], axis=0)
        h_hi = jnp.concatenate([h_hi, zeros_v], axis=0)

        def u_body(e, c, slot=slot, nxt=nxt, nxt_slot=nxt_slot, h_lo=h_lo, h_hi=h_hi):
            if nxt >= TC_PEER_TOKENS:
                @pl.when(step + 1 < pl.num_programs(0))
                def _():
                    row_copy(idx_next_ref, nxt - TC_PEER_TOKENS, e, nxt_slot).start()
            else:
                row_copy(idx_ref, nxt, e, nxt_slot).start()
            lo, hi = unpack(rows_buf[slot, e])
            a_buf[pl.ds(e, 1), :] = jnp.sum(lo * h_lo + hi * h_hi, axis=0, keepdims=True)
            return c

        lax.fori_loop(0, PEER_SLOTS, u_body, 0, unroll=8)
        a = jnp.sum(a_buf[...], axis=1, keepdims=True)
        c_buf[...] = jnp.broadcast_to(gates_t[:, t:t + 1] * _gelu(a), (PEER_SLOTS, LANES))

        def v_body(e, acc, slot=slot):
            lo, hi = unpack(rows_buf[slot, e])
            c = c_buf[pl.ds(e, 1), :]
            return acc[0] + c * lo, acc[1] + c * hi

        zero = jnp.zeros((SC_ROW_TILES, LANES), F32)
        acc_lo, acc_hi = lax.fori_loop(0, PEER_SLOTS, v_body, (zero, zero), unroll=8)
        o_ref[t, 0:U_TILE_ROWS, :] = acc_lo[U_TILE_ROWS:, :]
        o_ref[t, U_TILE_ROWS:, :] = acc_hi[U_TILE_ROWS:, :]


def tc_peer_experts(table, idx, gates, hp):
    t = hp.shape[0]
    assert t % TC_PEER_TOKENS == 0 and TC_PEER_TOKENS % TC_PEER_BUFFERS == 0
    n_steps = t // TC_PEER_TOKENS
    ids = TC_PEER_TOKENS * PEER_SLOTS
    smem = functools.partial(pl.BlockSpec, memory_space=pltpu.SMEM)
    out = pl.pallas_call(
        _tc_peer_kernel,
        grid=(n_steps,),
        in_specs=[
            smem((ids,), lambda i: (i,)),
            smem((ids,), lambda i: (jnp.minimum(i + 1, n_steps - 1),)),
            pl.BlockSpec((TC_PEER_TOKENS, PEER_SLOTS), lambda i: (i, 0)),
            pl.BlockSpec((TC_PEER_TOKENS, U_TILE_ROWS, LANES), lambda i: (i, 0, 0)),
            pl.BlockSpec(memory_space=pl.ANY),
        ],
        out_specs=pl.BlockSpec((TC_PEER_TOKENS, SC_ROW_TILES, LANES), lambda i: (i, 0, 0)),
        out_shape=jax.ShapeDtypeStruct((t, SC_ROW_TILES, LANES), F32),
        scratch_shapes=[
            pltpu.VMEM((TC_PEER_BUFFERS, PEER_SLOTS, SC_ROW_TILES, LANES), jnp.int32),
            pltpu.VMEM((PEER_SLOTS, LANES), F32),
            pltpu.VMEM((PEER_SLOTS, LANES), F32),
            pltpu.SemaphoreType.DMA((TC_PEER_BUFFERS,)),
        ],
        compiler_params=pltpu.CompilerParams(dimension_semantics=("arbitrary",), vmem_limit_bytes=VMEM_LIMIT,
                                             disable_bounds_checks=True),
        name="tc_peer_experts",
    )(idx, idx, gates, hp.reshape(t, U_TILE_ROWS, LANES), table)
    return out.reshape(t, D_MODEL)


TC_PEER_SHARE = 8
TC_PEER_QUANTUM = 1024


def peer_block(x, norm_g, wq, subkeys, table, prev_sc=None):
    t = x.shape[0]
    hp, idx3, gate3 = peer_query(x, norm_g, wq, subkeys)
    idx = jnp.transpose(idx3, (0, 2, 1)).reshape(t * PEER_SLOTS)
    gates = jnp.transpose(gate3, (0, 2, 1)).reshape(t * PEER_SLOTS)
    t_tc = (t // TC_PEER_SHARE) // TC_PEER_QUANTUM * TC_PEER_QUANTUM
    t_sc = t - t_tc
    idx_sc = idx[:t_sc * PEER_SLOTS]
    if prev_sc is not None:
        idx_sc, _ = lax.optimization_barrier((idx_sc, prev_sc))
    on_sc = sc_peer_experts(table, idx_sc, gates[:t_sc * PEER_SLOTS], hp[:t_sc])
    if t_tc == 0:
        return on_sc, (hp,), on_sc
    on_tc = tc_peer_experts(table, idx[t_sc * PEER_SLOTS:], gates[t_sc * PEER_SLOTS:].reshape(t_tc, PEER_SLOTS),
                            hp[t_sc:])
    return jnp.concatenate([on_sc, on_tc], axis=0), (hp, on_tc), on_sc


def _ple_kernel(x_ref, ffn_ref, p_ref, g_ref, wg_ref, wp_ref, gf_ref, o_ref, *, final):
    x = x_ref[...] + ffn_ref[...]
    gate = _sigmoid(jnp.dot(_rms(x, g_ref[...]).astype(BF16), wg_ref[...], preferred_element_type=F32))
    emb = jnp.dot(p_ref[...].astype(BF16), wp_ref[...], preferred_element_type=F32)
    y = x + gate * emb
    o_ref[...] = _rms(y, gf_ref[...]) if final else y


def ple_block(x, ffn, p, norm_g, w_gate, w_proj, norm_final, final):
    t = x.shape[0]
    tm = min(t, 1024)
    row = lambda i: (i, 0)
    full = lambda i: (0, 0)
    return pl.pallas_call(
        functools.partial(_ple_kernel, final=final),
        grid=(t // tm,),
        in_specs=[
            pl.BlockSpec((tm, D_MODEL), row),
            pl.BlockSpec((tm, D_MODEL), row),
            pl.BlockSpec((tm, PLE_DIM), row),
            pl.BlockSpec((1, D_MODEL), full),
            pl.BlockSpec((D_MODEL, D_MODEL), full),
            pl.BlockSpec((PLE_DIM, D_MODEL), full),
            pl.BlockSpec((1, D_MODEL), full),
        ],
        out_specs=pl.BlockSpec((tm, D_MODEL), row),
        out_shape=jax.ShapeDtypeStruct((t, D_MODEL), F32),
        compiler_params=_cparams("parallel"),
        name="ple_block",
    )(x, ffn, p, norm_g.reshape(1, D_MODEL), w_gate, w_proj, norm_final.reshape(1, D_MODEL))


def _prompt_groups(batch):
    sizes = []
    while sum(sizes) < batch:
        nxt = 1 if len(sizes) < 2 else -(-sizes[-1] * 7 // 5)
        sizes.append(min(nxt, batch - sum(sizes)))
    return sizes
def _trunk_layer(x, ple, lw, batch, seq, pool_prefix, pool_start, cache, final, after=None, prev_sc=None):
    if after is not None:
        x, _ = lax.optimization_barrier((x, after))
    uv = norm_matmul(x, lw["norm_mix"], lw["w_uv"], "gelu", F32, 2 * BW)
    mid = norm_matmul(x, lw["norm_mix"], lw["w_mid"], "none", F32, 4 * BW)
    gates = norm_matmul(x, lw["norm_mix"], lw["w_gates"], "sigmoid", BF16, D_MODEL)
    ya, vn = gmlp_mix(uv, lw["gmlp_ln_g"], lw["gmlp_ln_b"], lw["gmlp_ws"], lw["gmlp_bs"], min(seq, GMLP_CHUNK))
    yb, pool_state = pool_mix(mid, pool_prefix, pool_start, lw["pool_w"], lw["pool_scale"], batch, seq)
    if cache is None:
        yc = attn_prompt(mid, lw["rel_bias"], batch, seq)
    else:
        yc = attn_sample(mid, cache[0], cache[1], lw["rel_bias"], batch, seq)
    x = mixer_out(x, ya, yb, yc, gates, lw["w_branch"], lw["w_out"])
    ffn, stage, sc_out = peer_block(x, lw["norm_ffn"], lw["peer_wq"], lw["peer_subkeys"], lw["peer_table"], prev_sc)
    x = ple_block(x, ffn, ple, lw["norm_ple"], lw["ple_gate"], lw["ple_proj"], lw["norm_final"], final)
    return x, mid, pool_state, vn, stage, sc_out


def kernel(x_prompt, x_sample, cache_attn_k, cache_attn_v, state_pool, p_prompt, p_sample, norm_mix, w_in, gmlp_ln_g, gmlp_ln_b, gmlp_ws, gmlp_bs, pool_w, pool_scale, attn_rel_bias, w_branch, w_out, norm_ffn, peer_wq, peer_subkeys, peer_u, peer_v, norm_ple, ple_gate, ple_proj, norm_final):
    bp, lp, _ = x_prompt.shape
    bs, ls, _ = x_sample.shape
    assert lp % BAND_PAST == 0 and lp % GMLP_CHUNK == 0 and ls <= CHUNK
    n_keep = min(BAND_PAST, lp)
    sizes = _prompt_groups(bp)
    starts = [sum(sizes[:g]) for g in range(len(sizes))]
    xg = [x_prompt[a:a + n].reshape(n * lp, D_MODEL) for a, n in zip(starts, sizes)]
    xs = x_sample.reshape(bs * ls, D_MODEL)
    outs = {k: [] for k in ("pk", "pv", "pps", "sk", "sv", "sps", "sgv")}
    after = prev_sc = None
    stages = []
    for i in range(DEPTH):
        tab_u, tab_v = peer_u[i], peer_v[i]
        if stages:
            tab_u, tab_v, _ = lax.optimization_barrier((tab_u, tab_v, stages[len(stages) // 2]))
        stages = []
        w_in_b = w_in[i].astype(BF16)
        lw = dict(
            norm_mix=norm_mix[i],
            w_uv=w_in_b[:, :2 * BW],
            w_mid=w_in_b[:, 2 * BW:6 * BW],
            w_gates=w_in_b[:, 6 * BW:],
            gmlp_ln_g=gmlp_ln_g[i], gmlp_ln_b=gmlp_ln_b[i], gmlp_ws=gmlp_ws[i], gmlp_bs=gmlp_bs[i],
            pool_w=pool_w[i], pool_scale=pool_scale[i], rel_bias=attn_rel_bias[i],
            w_branch=w_branch[i].astype(BF16), w_out=w_out[i].astype(BF16),
            norm_ffn=norm_ffn[i], peer_wq=peer_wq[i].astype(BF16),
            peer_subkeys=peer_subkeys[i].reshape(2 * PEER_HEADS, PEER_NKEYS, PEER_HALF).astype(BF16),
            peer_table=pack_expert_tables(tab_u, tab_v),
            norm_ple=norm_ple[i], ple_gate=ple_gate[i].astype(BF16), ple_proj=ple_proj[i].astype(BF16),
            norm_final=norm_final,
        )
        final = i == DEPTH - 1
        pk, pv, pps = [], [], []
        for g, (a, bg) in enumerate(zip(starts, sizes)):
            ple_g = p_prompt[i, a:a + bg].reshape(bg * lp, PLE_DIM)
            zero_prefix = jnp.zeros((bg, POOL_STATE, BW), F32)
            xg[g], mid_p, ps_p, _, after, prev_sc = _trunk_layer(xg[g], ple_g, lw, bg, lp, zero_prefix, 0, None, final,
                                                                 after, prev_sc)
            stages.append(after)
            mid_p = mid_p.reshape(bg, lp, 4 * BW)
            pk.append(mid_p[:, lp - n_keep:, 2 * BW:3 * BW].reshape(bg, n_keep, HEADS, HEAD_DIM))
            pv.append(mid_p[:, lp - n_keep:, 3 * BW:].reshape(bg, n_keep, HEADS, HEAD_DIM))
            pps.append(ps_p)
        outs["pk"].append(jnp.concatenate(pk, axis=0))
        outs["pv"].append(jnp.concatenate(pv, axis=0))
        outs["pps"].append(jnp.concatenate(pps, axis=0))
        xs, mid_s, ps_s, vn_s, _, _ = _trunk_layer(xs, p_sample[i].reshape(bs * ls, PLE_DIM), lw, bs, ls, state_pool[i],
                                                PAST_LEN, (cache_attn_k[i], cache_attn_v[i]), final)
        mid_s = mid_s.reshape(bs, ls, 4 * BW)
        outs["sk"].append(mid_s[:, :, 2 * BW:3 * BW].reshape(bs, ls, HEADS, HEAD_DIM))
        outs["sv"].append(mid_s[:, :, 3 * BW:].reshape(bs, ls, HEADS, HEAD_DIM))
        outs["sps"].append(ps_s)
        outs["sgv"].append(vn_s.reshape(bs, ls, BW))
    st = lambda k: jnp.stack(outs[k])
    y_prompt = jnp.concatenate(xg, axis=0).reshape(bp, lp, D_MODEL)
    return (y_prompt, xs.reshape(bs, ls, D_MODEL), st("pk"), st("pv"), st("pps"),
            st("sk"), st("sv"), st("sps"), st("sgv"))
```

```python
import functools
import math

import jax
import jax.numpy as jnp
import numpy as np
from jax import lax
from jax.experimental import pallas as pl
from jax.experimental.pallas import tpu as pltpu
from jax.experimental.pallas import tpu_sc as plsc

F32 = jnp.float32
BF16 = jnp.bfloat16

D_MODEL = 1024
DEPTH = 2
CHUNK = 64
EPS = 1e-6
BW = D_MODEL // 2
GMLP_CHUNK = 128
GROUPS = 4
GDIM = BW // GROUPS
POOL_WINDOWS = (2, 4, 8, 16)
POOL_STATE = 15
POOL_PAD = 16
HEADS = 8
HEAD_DIM = BW // HEADS
BAND_CHUNKS = 8
BAND_PAST = BAND_CHUNKS * CHUNK
REL_CLIP = 128
PAST_LEN = 4096
PEER_HEADS = 8
PEER_NKEYS = 128
PEER_HALF = 128
PEER_TOPK = 16
PEER_SLOTS = PEER_HEADS * PEER_TOPK
PLE_DIM = 256

LANES = 128
VMEM_LIMIT = 56 * 1024 * 1024
NEG = -1e30


def _cparams(*sem):
    return pltpu.CompilerParams(dimension_semantics=sem, vmem_limit_bytes=VMEM_LIMIT)


def _rms(x, g):
    ms = jnp.mean(x * x, axis=-1, keepdims=True)
    return x * lax.rsqrt(ms + EPS) * g


def _gelu(x):
    c = math.sqrt(2.0 / math.pi)
    return 0.5 * x * (1.0 + jnp.tanh(c * (x + 0.044715 * (x * x * x))))


def _sigmoid(x):
    return 1.0 / (1.0 + jnp.exp(-x))


_ACTS = {"gelu": _gelu, "sigmoid": _sigmoid, "none": lambda z: z}


def _norm_matmul_kernel(x_ref, g_ref, w_ref, o_ref, h_ref, *, act):
    @pl.when(pl.program_id(1) == 0)
    def _():
        h_ref[...] = _rms(x_ref[...], g_ref[...]).astype(BF16)

    z = jnp.dot(h_ref[...], w_ref[...], preferred_element_type=F32)
    o_ref[...] = _ACTS[act](z).astype(o_ref.dtype)


def norm_matmul(x, g, w, act, out_dtype, tn):
    t, d = x.shape
    n = w.shape[1]
    tm = min(t, 1024)
    return pl.pallas_call(
        functools.partial(_norm_matmul_kernel, act=act),
        grid=(t // tm, n // tn),
        in_specs=[
            pl.BlockSpec((tm, d), lambda i, j: (i, 0)),
            pl.BlockSpec((1, d), lambda i, j: (0, 0)),
            pl.BlockSpec((d, tn), lambda i, j: (0, j)),
        ],
        out_specs=pl.BlockSpec((tm, tn), lambda i, j: (i, j)),
        out_shape=jax.ShapeDtypeStruct((t, n), out_dtype),
        scratch_shapes=[pltpu.VMEM((tm, d), BF16)],
        compiler_params=_cparams("parallel", "arbitrary"),
        name="norm_matmul_" + act,
    )(x, g.reshape(1, d), w)


def _gmlp_kernel(uv_ref, lng_ref, lnb_ref, ws_ref, bst_ref, y_ref, vn_ref, *, lc):
    u = uv_ref[:, :BW]
    v = uv_ref[:, BW:]
    mu = jnp.mean(v, axis=-1, keepdims=True)
    vc = v - mu
    var = jnp.mean(vc * vc, axis=-1, keepdims=True)
    vn = vc * lax.rsqrt(var + EPS) * lng_ref[...] + lnb_ref[...]
    vn_ref[...] = vn
    row = lax.broadcasted_iota(jnp.int32, (lc, lc), 0) // CHUNK
    col = lax.broadcasted_iota(jnp.int32, (lc, lc), 1) // CHUNK
    causal = col <= row
    vnb = vn.astype(BF16)
    for g in range(GROUPS):
        w = jnp.where(causal, ws_ref[g], 0.0).astype(BF16)
        s = jnp.dot(w, vnb[:, g * GDIM:(g + 1) * GDIM], preferred_element_type=F32)
        s = s + bst_ref[:, g:g + 1]
        y_ref[:, g * GDIM:(g + 1) * GDIM] = (u[:, g * GDIM:(g + 1) * GDIM] * s).astype(y_ref.dtype)


def gmlp_mix(uv, ln_g, ln_b, ws, bs, lc):
    t = uv.shape[0]
    return pl.pallas_call(
        functools.partial(_gmlp_kernel, lc=lc),
        grid=(t // lc,),
        in_specs=[
            pl.BlockSpec((lc, 2 * BW), lambda i: (i, 0)),
            pl.BlockSpec((1, BW), lambda i: (0, 0)),
            pl.BlockSpec((1, BW), lambda i: (0, 0)),
            pl.BlockSpec((GROUPS, lc, lc), lambda i: (0, 0, 0)),
            pl.BlockSpec((lc, GROUPS), lambda i: (0, 0)),
        ],
        out_specs=[
            pl.BlockSpec((lc, BW), lambda i: (i, 0)),
            pl.BlockSpec((lc, BW), lambda i: (i, 0)),
        ],
        out_shape=[
            jax.ShapeDtypeStruct((t, BW), BF16),
            jax.ShapeDtypeStruct((t, BW), F32),
        ],
        compiler_params=_cparams("parallel"),
        name="gmlp_mix",
    )(uv, ln_g.reshape(1, BW), ln_b.reshape(1, BW), ws[:, :lc, :lc], bs[:, :lc].T)


def _pool_kernel(u_ref, pre_ref, w_ref, sc_ref, y_ref, st_ref, pad_ref, *, seq, start_pos):
    pad_ref[0:POOL_PAD, :] = pre_ref[...]
    pad_ref[POOL_PAD:, :] = u_ref[...]
    pos = lax.broadcasted_iota(jnp.int32, (seq, 1), 0) + start_pos
    for g, win in enumerate(POOL_WINDOWS):
        cols = slice(g * GDIM, (g + 1) * GDIM)
        tok = pad_ref[POOL_PAD:, cols]
        acc = tok
        for k in range(1, win):
            acc = acc + pad_ref[POOL_PAD - k:POOL_PAD - k + seq, cols]
        cnt = jnp.minimum(pos + 1, win).astype(F32)
        d = acc / cnt - tok
        y = jnp.dot(d.astype(BF16), w_ref[g], preferred_element_type=F32)
        y_ref[:, cols] = (y * sc_ref[:, cols]).astype(y_ref.dtype)
    st_ref[...] = pad_ref[seq + 1:seq + POOL_PAD, :]


def pool_mix(mid, prefix, start_pos, pool_w, pool_scale, batch, seq):
    mid3 = mid.reshape(batch, seq, 4 * BW)
    pre = jnp.concatenate([jnp.zeros((batch, 1, BW), F32), prefix], axis=1)
    y, st = pl.pallas_call(
        functools.partial(_pool_kernel, seq=seq, start_pos=start_pos),
        grid=(batch,),
        in_specs=[
            pl.BlockSpec((None, seq, BW), lambda b: (b, 0, 0)),
            pl.BlockSpec((None, POOL_PAD, BW), lambda b: (b, 0, 0)),
            pl.BlockSpec((GROUPS, GDIM, GDIM), lambda b: (0, 0, 0)),
            pl.BlockSpec((1, BW), lambda b: (0, 0)),
        ],
        out_specs=[
            pl.BlockSpec((None, seq, BW), lambda b: (b, 0, 0)),
            pl.BlockSpec((None, POOL_STATE, BW), lambda b: (b, 0, 0)),
        ],
        out_shape=[
            jax.ShapeDtypeStruct((batch, seq, BW), BF16),
            jax.ShapeDtypeStruct((batch, POOL_STATE, BW), F32),
        ],
        scratch_shapes=[pltpu.VMEM((seq + POOL_PAD, BW), F32)],
        compiler_params=_cparams("parallel"),
        name="pool_mix",
    )(mid3, pre, pool_w.astype(BF16), pool_scale.reshape(1, BW))
    return y.reshape(batch * seq, BW), st


def _attn_chunks(q_ref, kcat_ref, vcat_ref, bias_ref, o_ref, *, n_chunks, cq, band, first_block):
    scale = HEAD_DIM ** -0.5
    heads_per_tile = LANES // HEAD_DIM
    lane_head = lax.broadcasted_iota(jnp.int32, (1, LANES), 1) // HEAD_DIM
    for tile in range(HEADS // heads_per_tile):
        cols = slice(tile * LANES, (tile + 1) * LANES)
        for i in range(n_chunks):
            rows = slice(i * cq, (i + 1) * cq)
            q = (q_ref[rows, cols] * scale).astype(BF16)
            k = kcat_ref[i * cq:i * cq + band, cols]
            v = vcat_ref[i * cq:i * cq + band, cols]
            out = None
            for j in range(heads_per_tile):
                own = lane_head == j
                s = lax.dot_general(jnp.where(own, q, jnp.zeros_like(q)), k, (((1,), (1,)), ((), ())),
                                    preferred_element_type=F32) + bias_ref[tile * heads_per_tile + j]
                if first_block is not None:
                    key = lax.broadcasted_iota(jnp.int32, (1, band), 1)
                    s = jnp.where(key >= first_block * (BAND_PAST - i * cq), s, NEG)
                m = jnp.max(s, axis=-1, keepdims=True)
                p = jnp.exp(s - m)
                l = jnp.sum(p, axis=-1, keepdims=True)
                o = jnp.dot(p.astype(BF16), v, preferred_element_type=F32) / l
                out = o if out is None else jnp.where(own, o, out)
            o_ref[rows, cols] = out.astype(o_ref.dtype)


def _attn_prompt_kernel(q_ref, kp_ref, ko_ref, vp_ref, vo_ref, bias_ref, o_ref, kcat_ref, vcat_ref):
    kcat_ref[0:BAND_PAST, :] = kp_ref[...].astype(BF16)
    kcat_ref[BAND_PAST:, :] = ko_ref[...].astype(BF16)
    vcat_ref[0:BAND_PAST, :] = vp_ref[...].astype(BF16)
    vcat_ref[BAND_PAST:, :] = vo_ref[...].astype(BF16)
    _attn_chunks(q_ref, kcat_ref, vcat_ref, bias_ref, o_ref, n_chunks=BAND_CHUNKS, cq=CHUNK,
                 band=BAND_PAST + CHUNK, first_block=(pl.program_id(1) == 0).astype(jnp.int32))


def _rel_bias_tile(rel_bias, qpos, kpos):
    lq, lk = len(qpos), len(kpos)
    assert (np.diff(qpos) == 1).all() and (np.diff(kpos) == 1).all()
    lr = lq + lk - 1
    dist = (qpos[0] - kpos[0]) - (np.arange(lr) - (lq - 1))
    diag = rel_bias[:, np.clip(dist, -REL_CLIP, REL_CLIP) + REL_CLIP]
    skewed = jnp.tile(diag, (1, lq))[:, lq - 1:lq - 1 + lq * (lr - 1)].reshape(-1, lq, lr - 1)
    return skewed[:, :, :lk]


def attn_prompt(mid, rel_bias, batch, seq):
    mid3 = mid.reshape(batch, seq, 4 * BW)
    blk = BAND_PAST
    bias = _rel_bias_tile(rel_bias, np.arange(CHUNK), np.arange(BAND_PAST + CHUNK) - BAND_PAST)
    prev = lambda b, j: jnp.maximum(j - 1, 0)
    y = pl.pallas_call(
        _attn_prompt_kernel,
        grid=(batch, seq // blk),
        in_specs=[
            pl.BlockSpec((None, blk, BW), lambda b, j: (b, j, 1)),
            pl.BlockSpec((None, blk, BW), lambda b, j: (b, prev(b, j), 2)),
            pl.BlockSpec((None, blk, BW), lambda b, j: (b, j, 2)),
            pl.BlockSpec((None, blk, BW), lambda b, j: (b, prev(b, j), 3)),
            pl.BlockSpec((None, blk, BW), lambda b, j: (b, j, 3)),
            pl.BlockSpec((HEADS, CHUNK, BAND_PAST + CHUNK), lambda b, j: (0, 0, 0)),
        ],
        out_specs=pl.BlockSpec((None, blk, BW), lambda b, j: (b, j, 0)),
        out_shape=jax.ShapeDtypeStruct((batch, seq, BW), BF16),
        scratch_shapes=[pltpu.VMEM((2 * blk, BW), BF16), pltpu.VMEM((2 * blk, BW), BF16)],
        compiler_params=_cparams("parallel", "parallel"),
        name="attn_prompt",
    )(mid3, mid3, mid3, mid3, mid3, bias)
    return y.reshape(batch * seq, BW)


def _attn_sample_kernel(q_ref, kc_ref, kn_ref, vc_ref, vn_ref, bias_ref, o_ref, kcat_ref, vcat_ref, *, n_cache):
    kcat_ref[0:n_cache, :] = kc_ref[...].astype(BF16)
    kcat_ref[n_cache:, :] = kn_ref[...].astype(BF16)
    vcat_ref[0:n_cache, :] = vc_ref[...].astype(BF16)
    vcat_ref[n_cache:, :] = vn_ref[...].astype(BF16)
    seq = q_ref.shape[0]
    _attn_chunks(q_ref, kcat_ref, vcat_ref, bias_ref, o_ref, n_chunks=1, cq=seq, band=n_cache + seq,
                 first_block=None)


def attn_sample(mid, cache_k, cache_v, rel_bias, batch, seq):
    n_cache = cache_k.shape[1]
    assert PAST_LEN >= n_cache
    mid3 = mid.reshape(batch, seq, 4 * BW)
    ck = cache_k.reshape(batch, n_cache, BW)
    cv = cache_v.reshape(batch, n_cache, BW)
    bias = _rel_bias_tile(rel_bias, PAST_LEN + np.arange(seq), PAST_LEN - n_cache + np.arange(n_cache + seq))
    y = pl.pallas_call(
        functools.partial(_attn_sample_kernel, n_cache=n_cache),
        grid=(batch,),
        in_specs=[
            pl.BlockSpec((None, seq, BW), lambda b: (b, 0, 1)),
            pl.BlockSpec((None, n_cache, BW), lambda b: (b, 0, 0)),
            pl.BlockSpec((None, seq, BW), lambda b: (b, 0, 2)),
            pl.BlockSpec((None, n_cache, BW), lambda b: (b, 0, 0)),
            pl.BlockSpec((None, seq, BW), lambda b: (b, 0, 3)),
            pl.BlockSpec((HEADS, seq, n_cache + seq), lambda b: (0, 0, 0)),
        ],
        out_specs=pl.BlockSpec((None, seq, BW), lambda b: (b, 0, 0)),
        out_shape=jax.ShapeDtypeStruct((batch, seq, BW), BF16),
        scratch_shapes=[pltpu.VMEM((n_cache + seq, BW), BF16), pltpu.VMEM((n_cache + seq, BW), BF16)],
        compiler_params=_cparams("parallel"),
        name="attn_sample",
    )(mid3, ck, mid3, cv, mid3, bias)
    return y.reshape(batch * seq, BW)


def _mixer_out_kernel(x_ref, ya_ref, yb_ref, yc_ref, gate_ref, wb_ref, wo_ref, o_ref):
    acc = None
    for n, y_ref in enumerate((ya_ref, yb_ref, yc_ref)):
        proj = jnp.dot(y_ref[...], wb_ref[n], preferred_element_type=F32)
        term = gate_ref[:, n * D_MODEL:(n + 1) * D_MODEL].astype(F32) * proj
        acc = term if acc is None else acc + term
    o_ref[...] = x_ref[...] + jnp.dot(acc.astype(BF16), wo_ref[...], preferred_element_type=F32)


def mixer_out(x, ya, yb, yc, gates, w_branch, w_out):
    t = x.shape[0]
    tm = min(t, 512)
    row = lambda i: (i, 0)
    return pl.pallas_call(
        _mixer_out_kernel,
        grid=(t // tm,),
        in_specs=[
            pl.BlockSpec((tm, D_MODEL), row),
            pl.BlockSpec((tm, BW), row),
            pl.BlockSpec((tm, BW), row),
            pl.BlockSpec((tm, BW), row),
            pl.BlockSpec((tm, 3 * D_MODEL), row),
            pl.BlockSpec((3, BW, D_MODEL), lambda i: (0, 0, 0)),
            pl.BlockSpec((D_MODEL, D_MODEL), lambda i: (0, 0)),
        ],
        out_specs=pl.BlockSpec((tm, D_MODEL), row),
        out_shape=jax.ShapeDtypeStruct((t, D_MODEL), F32),
        compiler_params=_cparams("parallel"),
        name="mixer_out",
    )(x, ya, yb, yc, gates, w_branch, w_out)


def _extract_top(s, payload, k):
    r = float(s.shape[0])
    rows = lax.broadcasted_iota(jnp.int32, s.shape, 0).astype(F32)
    vals, pays = [], []
    for _ in range(k):
        m = jnp.max(s, axis=0, keepdims=True)
        idx = jnp.min(jnp.where(s == m, rows, r), axis=0, keepdims=True)
        sel = rows == idx
        vals.append(m)
        pays.append(idx if payload is None else jnp.max(jnp.where(sel, payload, -1.0), axis=0, keepdims=True))
        s = jnp.where(sel, -jnp.inf, s)
    return jnp.concatenate(vals, axis=0), jnp.concatenate(pays, axis=0)


def _pair_candidates(sv, si):
    k = PEER_TOPK
    sub = 8
    assert k == 2 * sub
    b_row = lax.broadcasted_iota(jnp.int32, (sub, LANES), 0)
    vals = [sv[0][0:1] + sv[1], sv[0][1:2] + sv[1][0:sub]]
    ids = [si[0][0:1] * PEER_NKEYS + si[1], si[0][1:2] * PEER_NKEYS + si[1][0:sub]]
    for a in range(2, sub):
        keep = b_row < k // (a + 1)
        vals.append(jnp.where(keep, sv[0][a:a + 1] + sv[1][0:sub], -jnp.inf))
        ids.append(si[0][a:a + 1] * PEER_NKEYS + si[1][0:sub])
    vals.append(sv[0][sub:k] + sv[1][0:1])
    ids.append(si[0][sub:k] * PEER_NKEYS + si[1][0:1])
    return jnp.concatenate(vals, axis=0), jnp.concatenate(ids, axis=0)


def _peer_query_kernel(x_ref, g_ref, wq_ref, sk_ref, hp_ref, idx_ref, gate_ref, q_ref, *, tm):
    hb = _rms(x_ref[...], g_ref[...]).astype(BF16)
    q_ref[...] = jnp.dot(hb, wq_ref[...], preferred_element_type=F32).astype(BF16)
    bits = lax.bitcast_convert_type(hb.astype(F32), jnp.int32)
    half = D_MODEL // 2
    hp_ref[...] = bits[:, half:] | lax.shift_right_logical(bits[:, :half], 16)

    def sub_block(sb, carry):
        tok = pl.ds(pl.multiple_of(sb * LANES, LANES), LANES)
        for hd in range(PEER_HEADS):
            sv, si = [], []
            for p in range(2):
                hp = hd * 2 + p
                q = q_ref[tok, hp * PEER_HALF:(hp + 1) * PEER_HALF]
                s = lax.dot_general(sk_ref[hp], q, (((1,), (1,)), ((), ())), preferred_element_type=F32)
                v, i = _extract_top(s, None, PEER_TOPK)
                sv.append(v)
                si.append(i)
            cand, eid = _pair_candidates(sv, si)
            tv, te = _extract_top(cand, eid, PEER_TOPK)
            e = jnp.exp(tv - tv[0:1])
            gate = e / jnp.sum(e, axis=0, keepdims=True)
            idx_ref[sb, hd * PEER_TOPK:(hd + 1) * PEER_TOPK, :] = te.astype(jnp.int32)
            gate_ref[sb, hd * PEER_TOPK:(hd + 1) * PEER_TOPK, :] = gate
        return carry

    lax.fori_loop(0, tm // LANES, sub_block, 0)


def peer_query(x, norm_g, wq, subkeys):
    t = x.shape[0]
    tm = min(t, 512)
    nq = wq.shape[1]
    nsb = tm // LANES
    return pl.pallas_call(
        functools.partial(_peer_query_kernel, tm=tm),
        grid=(t // tm,),
        in_specs=[
            pl.BlockSpec((tm, D_MODEL), lambda i: (i, 0)),
            pl.BlockSpec((1, D_MODEL), lambda i: (0, 0)),
            pl.BlockSpec((D_MODEL, nq), lambda i: (0, 0)),
            pl.BlockSpec((2 * PEER_HEADS, PEER_NKEYS, PEER_HALF), lambda i: (0, 0, 0)),
        ],
        out_specs=[
            pl.BlockSpec((tm, D_MODEL // 2), lambda i: (i, 0)),
            pl.BlockSpec((nsb, PEER_SLOTS, LANES), lambda i: (i, 0, 0)),
            pl.BlockSpec((nsb, PEER_SLOTS, LANES), lambda i: (i, 0, 0)),
        ],
        out_shape=[
            jax.ShapeDtypeStruct((t, D_MODEL // 2), jnp.int32),
            jax.ShapeDtypeStruct((t // LANES, PEER_SLOTS, LANES), jnp.int32),
            jax.ShapeDtypeStruct((t // LANES, PEER_SLOTS, LANES), F32),
        ],
        scratch_shapes=[pltpu.VMEM((tm, nq), BF16)],
        compiler_params=_cparams("parallel"),
        name="peer_query",
    )(x, norm_g.reshape(1, D_MODEL), wq, subkeys)


HALF_D = D_MODEL // 2
SC_LANES = 16
SC_UNIT_ROWS = 32
SC_UNITS_PER_TOKEN = PEER_SLOTS // SC_UNIT_ROWS
SC_MAX_TOKEN_BLOCK = 8
SC_BUFFERS = 3
SC_ROW_GROUP = 8
SC_HALF_VECS = HALF_D // SC_LANES
SC_OUT_VECS = 8
SC_BF16_TERMS = 4
SC_ROW_TILES = D_MODEL // LANES


def pack_expert_tables(peer_u, peer_v):
    def pack(x):
        b = lax.bitcast_convert_type(x.astype(BF16), jnp.uint16).astype(jnp.uint32)
        return (b[:, HALF_D:] << 16) | b[:, :HALF_D]

    rows = lax.bitcast_convert_type(jnp.concatenate([pack(peer_u), pack(peer_v)], axis=1), jnp.int32)
    return rows.reshape(rows.shape[0], SC_ROW_TILES, LANES)


def sc_peer_experts(table, idx, gates, hp):
    t = hp.shape[0]
    info = plsc.get_sparse_core_info()
    n_workers = info.num_cores * info.num_subcores
    tpw = t // n_workers
    token_block = min(tpw, SC_MAX_TOKEN_BLOCK)
    assert t % (n_workers * token_block) == 0
    units = token_block * SC_UNITS_PER_TOKEN
    mesh = plsc.VectorSubcoreMesh(core_axis_name="core", subcore_axis_name="subcore")
    hi_mask = jnp.int32(-65536)
    gelu_c = math.sqrt(2.0 / math.pi)

    @functools.partial(
        pl.kernel,
        out_type=jax.ShapeDtypeStruct((t, D_MODEL), F32),
        mesh=mesh,
        scratch_types=[
            pltpu.VMEM((token_block * PEER_SLOTS,), jnp.int32),
            pltpu.VMEM((token_block * PEER_SLOTS,), F32),
            pltpu.VMEM((token_block, HALF_D), jnp.int32),
            pltpu.VMEM((token_block, D_MODEL), F32),
            pltpu.VMEM((SC_BUFFERS, SC_UNIT_ROWS, SC_ROW_TILES, LANES), jnp.int32),
            pltpu.VMEM((SC_UNIT_ROWS, SC_LANES), F32),
            pltpu.VMEM((SC_UNIT_ROWS,), jnp.int32),
            pltpu.SemaphoreType.DMA((SC_BUFFERS,)),
        ],
        compiler_params=pltpu.CompilerParams(needs_layout_passes=False),
        name="peer_sc_experts",
    )
    def kern(table_hbm, idx_hbm, gate_hbm, h_hbm, out_hbm, idx_v, gate_v, h_v, out_v, rows_v, part_v, coef_v, sem):
        wid = lax.axis_index("subcore") * info.num_cores + lax.axis_index("core")
        lane = lax.iota(jnp.int32, SC_LANES)
        zero = jnp.zeros((SC_LANES,), F32)

        def gather(unit, b):
            rows = idx_v.at[pl.ds(unit * SC_UNIT_ROWS, SC_UNIT_ROWS)]
            return pltpu.make_async_copy(table_hbm.at[rows], rows_v.at[b], sem.at[b])

        def row_vec(b, r, vec):
            per_tile_row = LANES // SC_LANES
            lane0 = pl.multiple_of((vec % per_tile_row) * SC_LANES, SC_LANES)
            return rows_v[b, r, vec // per_tile_row, pl.ds(lane0, SC_LANES)]

        def unpack(w):
            return lax.bitcast_convert_type(w << 16, F32), lax.bitcast_convert_type(w & hi_mask, F32)

        def as_pairs(w):
            return plsc.bitcast(w, BF16)

        def compute(unit, b):
            tl = unit // SC_UNITS_PER_TOKEN
            q = unit % SC_UNITS_PER_TOKEN

            def row_group(rg, carry):
                def kstep(k, accs):
                    hs = []
                    for j in range(SC_BF16_TERMS):
                        off = pl.multiple_of((k * SC_BF16_TERMS + j) * SC_LANES, SC_LANES)
                        hs.append(as_pairs(h_v[tl, pl.ds(off, SC_LANES)]))
                    new = []
                    for r in range(SC_ROW_GROUP):
                        p = None
                        for j in range(SC_BF16_TERMS):
                            term = as_pairs(row_vec(b, rg * SC_ROW_GROUP + r, k * SC_BF16_TERMS + j)) * hs[j]
                            p = term if p is None else p + term
                        lo, hi = unpack(plsc.bitcast(p, jnp.int32))
                        new.append(accs[r] + lo + hi)
                    return tuple(new)

                accs = lax.fori_loop(0, SC_HALF_VECS // SC_BF16_TERMS, kstep, (zero,) * SC_ROW_GROUP)
                for r in range(SC_ROW_GROUP):
                    part_v[rg * SC_ROW_GROUP + r, :] = accs[r]
                return carry

            lax.fori_loop(0, SC_UNIT_ROWS // SC_ROW_GROUP, row_group, 0)

            dots = []
            for i in range(SC_UNIT_ROWS // SC_LANES):
                rows = lane + i * SC_LANES
                terms = [plsc.load_gather(part_v, [rows, jnp.full((SC_LANES,), l, jnp.int32)])
                         for l in range(SC_LANES)]
                while len(terms) > 1:
                    terms = [terms[j] + terms[j + 1] for j in range(0, len(terms), 2)]
                dots.append(terms[0])
            for i, a in enumerate(dots):
                z = gelu_c * (a + 0.044715 * (a * a * a))
                act = a / (1.0 + jnp.exp(-2.0 * z))
                slot = pl.multiple_of(tl * PEER_SLOTS + q * SC_UNIT_ROWS + i * SC_LANES, SC_LANES)
                bits = lax.bitcast_convert_type(gate_v[pl.ds(slot, SC_LANES)] * act, jnp.int32)
                top = (bits + 0x7FFF + ((bits >> 16) & 1)) & hi_mask
                coef_v[pl.ds(i * SC_LANES, SC_LANES)] = top | lax.shift_right_logical(top, 16)

            def out_pass(dq, carry):
                def row_quad(rq, accs):
                    per_vec = SC_LANES // SC_BF16_TERMS
                    cvec = coef_v[pl.ds(pl.multiple_of((rq // per_vec) * SC_LANES, SC_LANES), SC_LANES)]
                    cs = []
                    for j in range(SC_BF16_TERMS):
                        src = jnp.full((SC_LANES,), (rq % per_vec) * SC_BF16_TERMS + j, jnp.int32)
                        cs.append(as_pairs(cvec.at[src].get(mode="promise_in_bounds")))
                    new = []
                    for k in range(SC_OUT_VECS):
                        p = None
                        for j in range(SC_BF16_TERMS):
                            w = row_vec(b, rq * SC_BF16_TERMS + j, SC_HALF_VECS + dq * SC_OUT_VECS + k)
                            term = as_pairs(w) * cs[j]
                            p = term if p is None else p + term
                        lo, hi = unpack(plsc.bitcast(p, jnp.int32))
                        new.append(accs[2 * k] + lo)
                        new.append(accs[2 * k + 1] + hi)
                    return tuple(new)

                accs = lax.fori_loop(0, SC_UNIT_ROWS // SC_BF16_TERMS, row_quad, (zero,) * (2 * SC_OUT_VECS))
                for k in range(SC_OUT_VECS):
                    off = pl.multiple_of((dq * SC_OUT_VECS + k) * SC_LANES, SC_LANES)
                    plsc.addupdate(out_v.at[tl, pl.ds(off, SC_LANES)], accs[2 * k])
                    plsc.addupdate(out_v.at[tl, pl.ds(HALF_D + off, SC_LANES)], accs[2 * k + 1])
                return carry

            lax.fori_loop(0, SC_HALF_VECS // SC_OUT_VECS, out_pass, 0)

        @pl.loop(0, tpw // token_block)
        def _(blk):
            tok0 = wid * tpw + blk * token_block
            slots = pl.ds(tok0 * PEER_SLOTS, token_block * PEER_SLOTS)
            pltpu.sync_copy(idx_hbm.at[slots], idx_v)
            pltpu.sync_copy(gate_hbm.at[slots], gate_v)
            pltpu.sync_copy(h_hbm.at[pl.ds(tok0, token_block)], h_v)

            @pl.loop(0, token_block)
            def _(tl):
                @pl.loop(0, D_MODEL // SC_LANES)
                def _(k):
                    out_v[tl, pl.ds(pl.multiple_of(k * SC_LANES, SC_LANES), SC_LANES)] = zero

            for u in range(SC_BUFFERS - 1):
                gather(u, u).start()

            @pl.loop(0, units)
            def _(unit):
                ahead = unit + SC_BUFFERS - 1

                @pl.when(ahead < units)
                def _():
                    gather(ahead, ahead % SC_BUFFERS).start()

                b = unit % SC_BUFFERS
                gather(unit, b).wait()
                compute(unit, b)

            pltpu.sync_copy(out_v, out_hbm.at[pl.ds(tok0, token_block)])

    return kern(table, idx, gates, hp)


TC_PEER_TOKENS = 8
TC_PEER_BUFFERS = 4
U_TILE_ROWS = SC_ROW_TILES // 2


def _tc_peer_kernel(idx_ref, idx_next_ref, gate_ref, hp_ref, table_hbm, o_ref, rows_buf, a_buf, c_buf, sem):
    step = pl.program_id(0)
    hi_mask = jnp.int32(-65536)

    def unpack(w):
        return lax.bitcast_convert_type(w << 16, F32), lax.bitcast_convert_type(w & hi_mask, F32)

    def row_copy(ids_ref, tok, e, slot):
        return pltpu.make_async_copy(table_hbm.at[ids_ref[tok * PEER_SLOTS + e]], rows_buf.at[slot, e], sem.at[slot])

    def wait_rows(slot):
        pltpu.make_async_copy(table_hbm.at[pl.ds(0, PEER_SLOTS)], rows_buf.at[slot], sem.at[slot]).wait()

    ahead = TC_PEER_BUFFERS - 1

    @pl.when(step == 0)
    def _():
        for tok in range(ahead):
            def first(e, c, tok=tok):
                row_copy(idx_ref, tok, e, tok).start()
                return c

            lax.fori_loop(0, PEER_SLOTS, first, 0, unroll=8)

    gates_t = gate_ref[...].T
    zeros_v = jnp.zeros((SC_ROW_TILES - U_TILE_ROWS, LANES), F32)
    for t in range(TC_PEER_TOKENS):
        slot = t % TC_PEER_BUFFERS
        nxt = t + ahead
        nxt_slot = nxt % TC_PEER_BUFFERS
        wait_rows(slot)
        h_lo, h_hi = unpack(hp_ref[t])
        h_lo = jnp.concatenate([h_lo, zeros_v], axis=0)
        h_hi = jnp.concatenate([h_hi, zeros_v], axis=0)

        def u_body(e, c, slot=slot, nxt=nxt, nxt_slot=nxt_slot, h_lo=h_lo, h_hi=h_hi):
            if nxt >= TC_PEER_TOKENS:
                @pl.when(step + 1 < pl.num_programs(0))
                def _():
                    row_copy(idx_next_ref, nxt - TC_PEER_TOKENS, e, nxt_slot).start()
            else:
                row_copy(idx_ref, nxt, e, nxt_slot).start()
            lo, hi = unpack(rows_buf[slot, e])
            a_buf[pl.ds(e, 1), :] = jnp.sum(lo * h_lo + hi * h_hi, axis=0, keepdims=True)
            return c

        lax.fori_loop(0, PEER_SLOTS, u_body, 0, unroll=8)
        a = jnp.sum(a_buf[...], axis=1, keepdims=True)
        c_buf[...] = jnp.broadcast_to(gates_t[:, t:t + 1] * _gelu(a), (PEER_SLOTS, LANES))

        def v_body(e, acc, slot=slot):
            lo, hi = unpack(rows_buf[slot, e])
            c = c_buf[pl.ds(e, 1), :]
            return acc[0] + c * lo, acc[1] + c * hi

        zero = jnp.zeros((SC_ROW_TILES, LANES), F32)
        acc_lo, acc_hi = lax.fori_loop(0, PEER_SLOTS, v_body, (zero, zero), unroll=8)
        o_ref[t, 0:U_TILE_ROWS, :] = acc_lo[U_TILE_ROWS:, :]
        o_ref[t, U_TILE_ROWS:, :] = acc_hi[U_TILE_ROWS:, :]


def tc_peer_experts(table, idx, gates, hp):
    t = hp.shape[0]
    assert t % TC_PEER_TOKENS == 0 and TC_PEER_TOKENS % TC_PEER_BUFFERS == 0
    n_steps = t // TC_PEER_TOKENS
    ids = TC_PEER_TOKENS * PEER_SLOTS
    smem = functools.partial(pl.BlockSpec, memory_space=pltpu.SMEM)
    out = pl.pallas_call(
        _tc_peer_kernel,
        grid=(n_steps,),
        in_specs=[
            smem((ids,), lambda i: (i,)),
            smem((ids,), lambda i: (jnp.minimum(i + 1, n_steps - 1),)),
            pl.BlockSpec((TC_PEER_TOKENS, PEER_SLOTS), lambda i: (i, 0)),
            pl.BlockSpec((TC_PEER_TOKENS, U_TILE_ROWS, LANES), lambda i: (i, 0, 0)),
            pl.BlockSpec(memory_space=pl.ANY),
        ],
        out_specs=pl.BlockSpec((TC_PEER_TOKENS, SC_ROW_TILES, LANES), lambda i: (i, 0, 0)),
        out_shape=jax.ShapeDtypeStruct((t, SC_ROW_TILES, LANES), F32),
        scratch_shapes=[
            pltpu.VMEM((TC_PEER_BUFFERS, PEER_SLOTS, SC_ROW_TILES, LANES), jnp.int32),
            pltpu.VMEM((PEER_SLOTS, LANES), F32),
            pltpu.VMEM((PEER_SLOTS, LANES), F32),
            pltpu.SemaphoreType.DMA((TC_PEER_BUFFERS,)),
        ],
        compiler_params=pltpu.CompilerParams(dimension_semantics=("arbitrary",), vmem_limit_bytes=VMEM_LIMIT,
                                             disable_bounds_checks=True),
        name="tc_peer_experts",
    )(idx, idx, gates, hp.reshape(t, U_TILE_ROWS, LANES), table)
    return out.reshape(t, D_MODEL)


TC_PEER_SHARE = 8
TC_PEER_QUANTUM = 1024


def peer_block(x, norm_g, wq, subkeys, table, prev_sc=None):
    t = x.shape[0]
    hp, idx3, gate3 = peer_query(x, norm_g, wq, subkeys)
    idx = jnp.transpose(idx3, (0, 2, 1)).reshape(t * PEER_SLOTS)
    gates = jnp.transpose(gate3, (0, 2, 1)).reshape(t * PEER_SLOTS)
    t_tc = (t // TC_PEER_SHARE) // TC_PEER_QUANTUM * TC_PEER_QUANTUM
    t_sc = t - t_tc
    idx_sc = idx[:t_sc * PEER_SLOTS]
    if prev_sc is not None:
        idx_sc, _ = lax.optimization_barrier((idx_sc, prev_sc))
    on_sc = sc_peer_experts(table, idx_sc, gates[:t_sc * PEER_SLOTS], hp[:t_sc])
    if t_tc == 0:
        return on_sc, (hp,), on_sc
    on_tc = tc_peer_experts(table, idx[t_sc * PEER_SLOTS:], gates[t_sc * PEER_SLOTS:].reshape(t_tc, PEER_SLOTS),
                            hp[t_sc:])
    return jnp.concatenate([on_sc, on_tc], axis=0), (hp, on_tc), on_sc


def _ple_kernel(x_ref, ffn_ref, p_ref, g_ref, wg_ref, wp_ref, gf_ref, o_ref, *, final):
    x = x_ref[...] + ffn_ref[...]
    gate = _sigmoid(jnp.dot(_rms(x, g_ref[...]).astype(BF16), wg_ref[...], preferred_element_type=F32))
    emb = jnp.dot(p_ref[...].astype(BF16), wp_ref[...], preferred_element_type=F32)
    y = x + gate * emb
    o_ref[...] = _rms(y, gf_ref[...]) if final else y


def ple_block(x, ffn, p, norm_g, w_gate, w_proj, norm_final, final):
    t = x.shape[0]
    tm = min(t, 1024)
    row = lambda i: (i, 0)
    full = lambda i: (0, 0)
    return pl.pallas_call(
        functools.partial(_ple_kernel, final=final),
        grid=(t // tm,),
        in_specs=[
            pl.BlockSpec((tm, D_MODEL), row),
            pl.BlockSpec((tm, D_MODEL), row),
            pl.BlockSpec((tm, PLE_DIM), row),
            pl.BlockSpec((1, D_MODEL), full),
            pl.BlockSpec((D_MODEL, D_MODEL), full),
            pl.BlockSpec((PLE_DIM, D_MODEL), full),
            pl.BlockSpec((1, D_MODEL), full),
        ],
        out_specs=pl.BlockSpec((tm, D_MODEL), row),
        out_shape=jax.ShapeDtypeStruct((t, D_MODEL), F32),
        compiler_params=_cparams("parallel"),
        name="ple_block",
    )(x, ffn, p, norm_g.reshape(1, D_MODEL), w_gate, w_proj, norm_final.reshape(1, D_MODEL))


def _prompt_groups(batch):
    sizes = []
    while sum(sizes) < batch:
        nxt = 1 if len(sizes) < 2 else -(-sizes[-1] * 7 // 5)
        sizes.append(min(nxt, batch - sum(sizes)))
    return sizes
def _trunk_layer(x, ple, lw, batch, seq, pool_prefix, pool_start, cache, final, after=None, prev_sc=None):
    if after is not None:
        x, _ = lax.optimization_barrier((x, after))
    uv = norm_matmul(x, lw["norm_mix"], lw["w_uv"], "gelu", F32, 2 * BW)
    mid = norm_matmul(x, lw["norm_mix"], lw["w_mid"], "none", F32, 4 * BW)
    gates = norm_matmul(x, lw["norm_mix"], lw["w_gates"], "sigmoid", BF16, D_MODEL)
    ya, vn = gmlp_mix(uv, lw["gmlp_ln_g"], lw["gmlp_ln_b"], lw["gmlp_ws"], lw["gmlp_bs"], min(seq, GMLP_CHUNK))
    yb, pool_state = pool_mix(mid, pool_prefix, pool_start, lw["pool_w"], lw["pool_scale"], batch, seq)
    if cache is None:
        yc = attn_prompt(mid, lw["rel_bias"], batch, seq)
    else:
        yc = attn_sample(mid, cache[0], cache[1], lw["rel_bias"], batch, seq)
    x = mixer_out(x, ya, yb, yc, gates, lw["w_branch"], lw["w_out"])
    ffn, stage, sc_out = peer_block(x, lw["norm_ffn"], lw["peer_wq"], lw["peer_subkeys"], lw["peer_table"], prev_sc)
    x = ple_block(x, ffn, ple, lw["norm_ple"], lw["ple_gate"], lw["ple_proj"], lw["norm_final"], final)
    return x, mid, pool_state, vn, stage, sc_out


def kernel(x_prompt, x_sample, cache_attn_k, cache_attn_v, state_pool, p_prompt, p_sample, norm_mix, w_in, gmlp_ln_g, gmlp_ln_b, gmlp_ws, gmlp_bs, pool_w, pool_scale, attn_rel_bias, w_branch, w_out, norm_ffn, peer_wq, peer_subkeys, peer_u, peer_v, norm_ple, ple_gate, ple_proj, norm_final):
    bp, lp, _ = x_prompt.shape
    bs, ls, _ = x_sample.shape
    assert lp % BAND_PAST == 0 and lp % GMLP_CHUNK == 0 and ls <= CHUNK
    n_keep = min(BAND_PAST, lp)
    sizes = _prompt_groups(bp)
    starts = [sum(sizes[:g]) for g in range(len(sizes))]
    xg = [x_prompt[a:a + n].reshape(n * lp, D_MODEL) for a, n in zip(starts, sizes)]
    xs = x_sample.reshape(bs * ls, D_MODEL)
    outs = {k: [] for k in ("pk", "pv", "pps", "sk", "sv", "sps", "sgv")}
    after = prev_sc = None
    stages = []
    for i in range(DEPTH):
        tab_u, tab_v = peer_u[i], peer_v[i]
        if stages:
            tab_u, tab_v, _ = lax.optimization_barrier((tab_u, tab_v, stages[len(stages) // 2]))
        stages = []
        w_in_b = w_in[i].astype(BF16)
        lw = dict(
            norm_mix=norm_mix[i],
            w_uv=w_in_b[:, :2 * BW],
            w_mid=w_in_b[:, 2 * BW:6 * BW],
            w_gates=w_in_b[:, 6 * BW:],
            gmlp_ln_g=gmlp_ln_g[i], gmlp_ln_b=gmlp_ln_b[i], gmlp_ws=gmlp_ws[i], gmlp_bs=gmlp_bs[i],
            pool_w=pool_w[i], pool_scale=pool_scale[i], rel_bias=attn_rel_bias[i],
            w_branch=w_branch[i].astype(BF16), w_out=w_out[i].astype(BF16),
            norm_ffn=norm_ffn[i], peer_wq=peer_wq[i].astype(BF16),
            peer_subkeys=peer_subkeys[i].reshape(2 * PEER_HEADS, PEER_NKEYS, PEER_HALF).astype(BF16),
            peer_table=pack_expert_tables(tab_u, tab_v),
            norm_ple=norm_ple[i], ple_gate=ple_gate[i].astype(BF16), ple_proj=ple_proj[i].astype(BF16),
            norm_final=norm_final,
        )
        final = i == DEPTH - 1
        pk, pv, pps = [], [], []
        for g, (a, bg) in enumerate(zip(starts, sizes)):
            ple_g = p_prompt[i, a:a + bg].reshape(bg * lp, PLE_DIM)
            zero_prefix = jnp.zeros((bg, POOL_STATE, BW), F32)
            xg[g], mid_p, ps_p, _, after, prev_sc = _trunk_layer(xg[g], ple_g, lw, bg, lp, zero_prefix, 0, None, final,
                                                                 after, prev_sc)
            stages.append(after)
            mid_p = mid_p.reshape(bg, lp, 4 * BW)
            pk.append(mid_p[:, lp - n_keep:, 2 * BW:3 * BW].reshape(bg, n_keep, HEADS, HEAD_DIM))
            pv.append(mid_p[:, lp - n_keep:, 3 * BW:].reshape(bg, n_keep, HEADS, HEAD_DIM))
            pps.append(ps_p)
        outs["pk"].append(jnp.concatenate(pk, axis=0))
        outs["pv"].append(jnp.concatenate(pv, axis=0))
        outs["pps"].append(jnp.concatenate(pps, axis=0))
        xs, mid_s, ps_s, vn_s, _, _ = _trunk_layer(xs, p_sample[i].reshape(bs * ls, PLE_DIM), lw, bs, ls, state_pool[i],
                                                PAST_LEN, (cache_attn_k[i], cache_attn_v[i]), final)
        mid_s = mid_s.reshape(bs, ls, 4 * BW)
        outs["sk"].append(mid_s[:, :, 2 * BW:3 * BW].reshape(bs, ls, HEADS, HEAD_DIM))
        outs["sv"].append(mid_s[:, :, 3 * BW:].reshape(bs, ls, HEADS, HEAD_DIM))
        outs["sps"].append(ps_s)
        outs["sgv"].append(vn_s.reshape(bs, ls, BW))
    st = lambda k: jnp.stack(outs[k])
    y_prompt = jnp.concatenate(xg, axis=0).reshape(bp, lp, D_MODEL)
    return (y_prompt, xs.reshape(bs, ls, D_MODEL), st("pk"), st("pv"), st("pps"),
            st("sk"), st("sv"), st("sps"), st("sgv"))
```

```python
import functools
import math

import jax
import jax.numpy as jnp
import numpy as np
from jax import lax
from jax.experimental import pallas as pl
from jax.experimental.pallas import tpu as pltpu
from jax.experimental.pallas import tpu_sc as plsc

F32 = jnp.float32
BF16 = jnp.bfloat16

D_MODEL = 1024
DEPTH = 2
CHUNK = 64
EPS = 1e-6
BW = D_MODEL // 2
GMLP_CHUNK = 128
GROUPS = 4
GDIM = BW // GROUPS
POOL_WINDOWS = (2, 4, 8, 16)
POOL_STATE = 15
POOL_PAD = 16
HEADS = 8
HEAD_DIM = BW // HEADS
BAND_CHUNKS = 8
BAND_PAST = BAND_CHUNKS * CHUNK
REL_CLIP = 128
PAST_LEN = 4096
PEER_HEADS = 8
PEER_NKEYS = 128
PEER_HALF = 128
PEER_TOPK = 16
PEER_SLOTS = PEER_HEADS * PEER_TOPK
PLE_DIM = 256

LANES = 128
VMEM_LIMIT = 56 * 1024 * 1024
NEG = -1e30


def _cparams(*sem):
    return pltpu.CompilerParams(dimension_semantics=sem, vmem_limit_bytes=VMEM_LIMIT)


def _rms(x, g):
    ms = jnp.mean(x * x, axis=-1, keepdims=True)
    return x * lax.rsqrt(ms + EPS) * g


def _gelu(x):
    c = math.sqrt(2.0 / math.pi)
    return 0.5 * x * (1.0 + jnp.tanh(c * (x + 0.044715 * (x * x * x))))


def _sigmoid(x):
    return 1.0 / (1.0 + jnp.exp(-x))


_ACTS = {"gelu": _gelu, "sigmoid": _sigmoid, "none": lambda z: z}


def _norm_matmul_kernel(x_ref, g_ref, w_ref, o_ref, h_ref, *, act):
    @pl.when(pl.program_id(1) == 0)
    def _():
        h_ref[...] = _rms(x_ref[...], g_ref[...]).astype(BF16)

    z = jnp.dot(h_ref[...], w_ref[...], preferred_element_type=F32)
    o_ref[...] = _ACTS[act](z).astype(o_ref.dtype)


def norm_matmul(x, g, w, act, out_dtype, tn):
    t, d = x.shape
    n = w.shape[1]
    tm = min(t, 1024)
    return pl.pallas_call(
        functools.partial(_norm_matmul_kernel, act=act),
        grid=(t // tm, n // tn),
        in_specs=[
            pl.BlockSpec((tm, d), lambda i, j: (i, 0)),
            pl.BlockSpec((1, d), lambda i, j: (0, 0)),
            pl.BlockSpec((d, tn), lambda i, j: (0, j)),
        ],
        out_specs=pl.BlockSpec((tm, tn), lambda i, j: (i, j)),
        out_shape=jax.ShapeDtypeStruct((t, n), out_dtype),
        scratch_shapes=[pltpu.VMEM((tm, d), BF16)],
        compiler_params=_cparams("parallel", "arbitrary"),
        name="norm_matmul_" + act,
    )(x, g.reshape(1, d), w)


def _gmlp_kernel(uv_ref, lng_ref, lnb_ref, ws_ref, bst_ref, y_ref, vn_ref, *, lc):
    u = uv_ref[:, :BW]
    v = uv_ref[:, BW:]
    mu = jnp.mean(v, axis=-1, keepdims=True)
    vc = v - mu
    var = jnp.mean(vc * vc, axis=-1, keepdims=True)
    vn = vc * lax.rsqrt(var + EPS) * lng_ref[...] + lnb_ref[...]
    vn_ref[...] = vn
    row = lax.broadcasted_iota(jnp.int32, (lc, lc), 0) // CHUNK
    col = lax.broadcasted_iota(jnp.int32, (lc, lc), 1) // CHUNK
    causal = col <= row
    vnb = vn.astype(BF16)
    for g in range(GROUPS):
        w = jnp.where(causal, ws_ref[g], 0.0).astype(BF16)
        s = jnp.dot(w, vnb[:, g * GDIM:(g + 1) * GDIM], preferred_element_type=F32)
        s = s + bst_ref[:, g:g + 1]
        y_ref[:, g * GDIM:(g + 1) * GDIM] = (u[:, g * GDIM:(g + 1) * GDIM] * s).astype(y_ref.dtype)


def gmlp_mix(uv, ln_g, ln_b, ws, bs, lc):
    t = uv.shape[0]
    return pl.pallas_call(
        functools.partial(_gmlp_kernel, lc=lc),
        grid=(t // lc,),
        in_specs=[
            pl.BlockSpec((lc, 2 * BW), lambda i: (i, 0)),
            pl.BlockSpec((1, BW), lambda i: (0, 0)),
            pl.BlockSpec((1, BW), lambda i: (0, 0)),
            pl.BlockSpec((GROUPS, lc, lc), lambda i: (0, 0, 0)),
            pl.BlockSpec((lc, GROUPS), lambda i: (0, 0)),
        ],
        out_specs=[
            pl.BlockSpec((lc, BW), lambda i: (i, 0)),
            pl.BlockSpec((lc, BW), lambda i: (i, 0)),
        ],
        out_shape=[
            jax.ShapeDtypeStruct((t, BW), BF16),
            jax.ShapeDtypeStruct((t, BW), F32),
        ],
        compiler_params=_cparams("parallel"),
        name="gmlp_mix",
    )(uv, ln_g.reshape(1, BW), ln_b.reshape(1, BW), ws[:, :lc, :lc], bs[:, :lc].T)


def _pool_kernel(u_ref, pre_ref, w_ref, sc_ref, y_ref, st_ref, pad_ref, *, seq, start_pos):
    pad_ref[0:POOL_PAD, :] = pre_ref[...]
    pad_ref[POOL_PAD:, :] = u_ref[...]
    pos = lax.broadcasted_iota(jnp.int32, (seq, 1), 0) + start_pos
    for g, win in enumerate(POOL_WINDOWS):
        cols = slice(g * GDIM, (g + 1) * GDIM)
        tok = pad_ref[POOL_PAD:, cols]
        acc = tok
        for k in range(1, win):
            acc = acc + pad_ref[POOL_PAD - k:POOL_PAD - k + seq, cols]
        cnt = jnp.minimum(pos + 1, win).astype(F32)
        d = acc / cnt - tok
        y = jnp.dot(d.astype(BF16), w_ref[g], preferred_element_type=F32)
        y_ref[:, cols] = (y * sc_ref[:, cols]).astype(y_ref.dtype)
    st_ref[...] = pad_ref[seq + 1:seq + POOL_PAD, :]


def pool_mix(mid, prefix, start_pos, pool_w, pool_scale, batch, seq):
    mid3 = mid.reshape(batch, seq, 4 * BW)
    pre = jnp.concatenate([jnp.zeros((batch, 1, BW), F32), prefix], axis=1)
    y, st = pl.pallas_call(
        functools.partial(_pool_kernel, seq=seq, start_pos=start_pos),
        grid=(batch,),
        in_specs=[
            pl.BlockSpec((None, seq, BW), lambda b: (b, 0, 0)),
            pl.BlockSpec((None, POOL_PAD, BW), lambda b: (b, 0, 0)),
            pl.BlockSpec((GROUPS, GDIM, GDIM), lambda b: (0, 0, 0)),
            pl.BlockSpec((1, BW), lambda b: (0, 0)),
        ],
        out_specs=[
            pl.BlockSpec((None, seq, BW), lambda b: (b, 0, 0)),
            pl.BlockSpec((None, POOL_STATE, BW), lambda b: (b, 0, 0)),
        ],
        out_shape=[
            jax.ShapeDtypeStruct((batch, seq, BW), BF16),
            jax.ShapeDtypeStruct((batch, POOL_STATE, BW), F32),
        ],
        scratch_shapes=[pltpu.VMEM((seq + POOL_PAD, BW), F32)],
        compiler_params=_cparams("parallel"),
        name="pool_mix",
    )(mid3, pre, pool_w.astype(BF16), pool_scale.reshape(1, BW))
    return y.reshape(batch * seq, BW), st


def _attn_chunks(q_ref, kcat_ref, vcat_ref, bias_ref, o_ref, *, n_chunks, cq, band, first_block):
    scale = HEAD_DIM ** -0.5
    heads_per_tile = LANES // HEAD_DIM
    lane_head = lax.broadcasted_iota(jnp.int32, (1, LANES), 1) // HEAD_DIM
    for tile in range(HEADS // heads_per_tile):
        cols = slice(tile * LANES, (tile + 1) * LANES)
        for i in range(n_chunks):
            rows = slice(i * cq, (i + 1) * cq)
            q = (q_ref[rows, cols] * scale).astype(BF16)
            k = kcat_ref[i * cq:i * cq + band, cols]
            v = vcat_ref[i * cq:i * cq + band, cols]
            out = None
            for j in range(heads_per_tile):
                own = lane_head == j
                s = lax.dot_general(jnp.where(own, q, jnp.zeros_like(q)), k, (((1,), (1,)), ((), ())),
                                    preferred_element_type=F32) + bias_ref[tile * heads_per_tile + j]
                if first_block is not None:
                    key = lax.broadcasted_iota(jnp.int32, (1, band), 1)
                    s = jnp.where(key >= first_block * (BAND_PAST - i * cq), s, NEG)
                m = jnp.max(s, axis=-1, keepdims=True)
                p = jnp.exp(s - m)
                l = jnp.sum(p, axis=-1, keepdims=True)
                o = jnp.dot(p.astype(BF16), v, preferred_element_type=F32) / l
                out = o if out is None else jnp.where(own, o, out)
            o_ref[rows, cols] = out.astype(o_ref.dtype)


def _attn_prompt_kernel(q_ref, kp_ref, ko_ref, vp_ref, vo_ref, bias_ref, o_ref, kcat_ref, vcat_ref):
    kcat_ref[0:BAND_PAST, :] = kp_ref[...].astype(BF16)
    kcat_ref[BAND_PAST:, :] = ko_ref[...].astype(BF16)
    vcat_ref[0:BAND_PAST, :] = vp_ref[...].astype(BF16)
    vcat_ref[BAND_PAST:, :] = vo_ref[...].astype(BF16)
    _attn_chunks(q_ref, kcat_ref, vcat_ref, bias_ref, o_ref, n_chunks=BAND_CHUNKS, cq=CHUNK,
                 band=BAND_PAST + CHUNK, first_block=(pl.program_id(1) == 0).astype(jnp.int32))


def _rel_bias_tile(rel_bias, qpos, kpos):
    lq, lk = len(qpos), len(kpos)
    assert (np.diff(qpos) == 1).all() and (np.diff(kpos) == 1).all()
    lr = lq + lk - 1
    dist = (qpos[0] - kpos[0]) - (np.arange(lr) - (lq - 1))
    diag = rel_bias[:, np.clip(dist, -REL_CLIP, REL_CLIP) + REL_CLIP]
    skewed = jnp.tile(diag, (1, lq))[:, lq - 1:lq - 1 + lq * (lr - 1)].reshape(-1, lq, lr - 1)
    return skewed[:, :, :lk]


def attn_prompt(mid, rel_bias, batch, seq):
    mid3 = mid.reshape(batch, seq, 4 * BW)
    blk = BAND_PAST
    bias = _rel_bias_tile(rel_bias, np.arange(CHUNK), np.arange(BAND_PAST + CHUNK) - BAND_PAST)
    prev = lambda b, j: jnp.maximum(j - 1, 0)
    y = pl.pallas_call(
        _attn_prompt_kernel,
        grid=(batch, seq // blk),
        in_specs=[
            pl.BlockSpec((None, blk, BW), lambda b, j: (b, j, 1)),
            pl.BlockSpec((None, blk, BW), lambda b, j: (b, prev(b, j), 2)),
            pl.BlockSpec((None, blk, BW), lambda b, j: (b, j, 2)),
            pl.BlockSpec((None, blk, BW), lambda b, j: (b, prev(b, j), 3)),
            pl.BlockSpec((None, blk, BW), lambda b, j: (b, j, 3)),
            pl.BlockSpec((HEADS, CHUNK, BAND_PAST + CHUNK), lambda b, j: (0, 0, 0)),
        ],
        out_specs=pl.BlockSpec((None, blk, BW), lambda b, j: (b, j, 0)),
        out_shape=jax.ShapeDtypeStruct((batch, seq, BW), BF16),
        scratch_shapes=[pltpu.VMEM((2 * blk, BW), BF16), pltpu.VMEM((2 * blk, BW), BF16)],
        compiler_params=_cparams("parallel", "parallel"),
        name="attn_prompt",
    )(mid3, mid3, mid3, mid3, mid3, bias)
    return y.reshape(batch * seq, BW)


def _attn_sample_kernel(q_ref, kc_ref, kn_ref, vc_ref, vn_ref, bias_ref, o_ref, kcat_ref, vcat_ref, *, n_cache):
    kcat_ref[0:n_cache, :] = kc_ref[...].astype(BF16)
    kcat_ref[n_cache:, :] = kn_ref[...].astype(BF16)
    vcat_ref[0:n_cache, :] = vc_ref[...].astype(BF16)
    vcat_ref[n_cache:, :] = vn_ref[...].astype(BF16)
    seq = q_ref.shape[0]
    _attn_chunks(q_ref, kcat_ref, vcat_ref, bias_ref, o_ref, n_chunks=1, cq=seq, band=n_cache + seq,
                 first_block=None)


def attn_sample(mid, cache_k, cache_v, rel_bias, batch, seq):
    n_cache = cache_k.shape[1]
    assert PAST_LEN >= n_cache
    mid3 = mid.reshape(batch, seq, 4 * BW)
    ck = cache_k.reshape(batch, n_cache, BW)
    cv = cache_v.reshape(batch, n_cache, BW)
    bias = _rel_bias_tile(rel_bias, PAST_LEN + np.arange(seq), PAST_LEN - n_cache + np.arange(n_cache + seq))
    y = pl.pallas_call(
        functools.partial(_attn_sample_kernel, n_cache=n_cache),
        grid=(batch,),
        in_specs=[
            pl.BlockSpec((None, seq, BW), lambda b: (b, 0, 1)),
            pl.BlockSpec((None, n_cache, BW), lambda b: (b, 0, 0)),
            pl.BlockSpec((None, seq, BW), lambda b: (b, 0, 2)),
            pl.BlockSpec((None, n_cache, BW), lambda b: (b, 0, 0)),
            pl.BlockSpec((None, seq, BW), lambda b: (b, 0, 3)),
            pl.BlockSpec((HEADS, seq, n_cache + seq), lambda b: (0, 0, 0)),
        ],
        out_specs=pl.BlockSpec((None, seq, BW), lambda b: (b, 0, 0)),
        out_shape=jax.ShapeDtypeStruct((batch, seq, BW), BF16),
        scratch_shapes=[pltpu.VMEM((n_cache + seq, BW), BF16), pltpu.VMEM((n_cache + seq, BW), BF16)],
        compiler_params=_cparams("parallel"),
        name="attn_sample",
    )(mid3, ck, mid3, cv, mid3, bias)
    return y.reshape(batch * seq, BW)


def _mixer_out_kernel(x_ref, ya_ref, yb_ref, yc_ref, gate_ref, wb_ref, wo_ref, o_ref):
    acc = None
    for n, y_ref in enumerate((ya_ref, yb_ref, yc_ref)):
        proj = jnp.dot(y_ref[...], wb_ref[n], preferred_element_type=F32)
        term = gate_ref[:, n * D_MODEL:(n + 1) * D_MODEL].astype(F32) * proj
        acc = term if acc is None else acc + term
    o_ref[...] = x_ref[...] + jnp.dot(acc.astype(BF16), wo_ref[...], preferred_element_type=F32)


def mixer_out(x, ya, yb, yc, gates, w_branch, w_out):
    t = x.shape[0]
    tm = min(t, 512)
    row = lambda i: (i, 0)
    return pl.pallas_call(
        _mixer_out_kernel,
        grid=(t // tm,),
        in_specs=[
            pl.BlockSpec((tm, D_MODEL), row),
            pl.BlockSpec((tm, BW), row),
            pl.BlockSpec((tm, BW), row),
            pl.BlockSpec((tm, BW), row),
            pl.BlockSpec((tm, 3 * D_MODEL), row),
            pl.BlockSpec((3, BW, D_MODEL), lambda i: (0, 0, 0)),
            pl.BlockSpec((D_MODEL, D_MODEL), lambda i: (0, 0)),
        ],
        out_specs=pl.BlockSpec((tm, D_MODEL), row),
        out_shape=jax.ShapeDtypeStruct((t, D_MODEL), F32),
        compiler_params=_cparams("parallel"),
        name="mixer_out",
    )(x, ya, yb, yc, gates, w_branch, w_out)


def _extract_top(s, payload, k):
    r = float(s.shape[0])
    rows = lax.broadcasted_iota(jnp.int32, s.shape, 0).astype(F32)
    vals, pays = [], []
    for _ in range(k):
        m = jnp.max(s, axis=0, keepdims=True)
        idx = jnp.min(jnp.where(s == m, rows, r), axis=0, keepdims=True)
        sel = rows == idx
        vals.append(m)
        pays.append(idx if payload is None else jnp.max(jnp.where(sel, payload, -1.0), axis=0, keepdims=True))
        s = jnp.where(sel, -jnp.inf, s)
    return jnp.concatenate(vals, axis=0), jnp.concatenate(pays, axis=0)


def _pair_candidates(sv, si):
    k = PEER_TOPK
    sub = 8
    assert k == 2 * sub
    b_row = lax.broadcasted_iota(jnp.int32, (sub, LANES), 0)
    vals = [sv[0][0:1] + sv[1], sv[0][1:2] + sv[1][0:sub]]
    ids = [si[0][0:1] * PEER_NKEYS + si[1], si[0][1:2] * PEER_NKEYS + si[1][0:sub]]
    for a in range(2, sub):
        keep = b_row < k // (a + 1)
        vals.append(jnp.where(keep, sv[0][a:a + 1] + sv[1][0:sub], -jnp.inf))
        ids.append(si[0][a:a + 1] * PEER_NKEYS + si[1][0:sub])
    vals.append(sv[0][sub:k] + sv[1][0:1])
    ids.append(si[0][sub:k] * PEER_NKEYS + si[1][0:1])
    return jnp.concatenate(vals, axis=0), jnp.concatenate(ids, axis=0)


def _peer_query_kernel(x_ref, g_ref, wq_ref, sk_ref, hp_ref, idx_ref, gate_ref, q_ref, *, tm):
    hb = _rms(x_ref[...], g_ref[...]).astype(BF16)
    q_ref[...] = jnp.dot(hb, wq_ref[...], preferred_element_type=F32).astype(BF16)
    bits = lax.bitcast_convert_type(hb.astype(F32), jnp.int32)
    half = D_MODEL // 2
    hp_ref[...] = bits[:, half:] | lax.shift_right_logical(bits[:, :half], 16)

    def sub_block(sb, carry):
        tok = pl.ds(pl.multiple_of(sb * LANES, LANES), LANES)
        for hd in range(PEER_HEADS):
            sv, si = [], []
            for p in range(2):
                hp = hd * 2 + p
                q = q_ref[tok, hp * PEER_HALF:(hp + 1) * PEER_HALF]
                s = lax.dot_general(sk_ref[hp], q, (((1,), (1,)), ((), ())), preferred_element_type=F32)
                v, i = _extract_top(s, None, PEER_TOPK)
                sv.append(v)
                si.append(i)
            cand, eid = _pair_candidates(sv, si)
            tv, te = _extract_top(cand, eid, PEER_TOPK)
            e = jnp.exp(tv - tv[0:1])
            gate = e / jnp.sum(e, axis=0, keepdims=True)
            idx_ref[sb, hd * PEER_TOPK:(hd + 1) * PEER_TOPK, :] = te.astype(jnp.int32)
            gate_ref[sb, hd * PEER_TOPK:(hd + 1) * PEER_TOPK, :] = gate
        return carry

    lax.fori_loop(0, tm // LANES, sub_block, 0)


def peer_query(x, norm_g, wq, subkeys):
    t = x.shape[0]
    tm = min(t, 512)
    nq = wq.shape[1]
    nsb = tm // LANES
    return pl.pallas_call(
        functools.partial(_peer_query_kernel, tm=tm),
        grid=(t // tm,),
        in_specs=[
            pl.BlockSpec((tm, D_MODEL), lambda i: (i, 0)),
            pl.BlockSpec((1, D_MODEL), lambda i: (0, 0)),
            pl.BlockSpec((D_MODEL, nq), lambda i: (0, 0)),
            pl.BlockSpec((2 * PEER_HEADS, PEER_NKEYS, PEER_HALF), lambda i: (0, 0, 0)),
        ],
        out_specs=[
            pl.BlockSpec((tm, D_MODEL // 2), lambda i: (i, 0)),
            pl.BlockSpec((nsb, PEER_SLOTS, LANES), lambda i: (i, 0, 0)),
            pl.BlockSpec((nsb, PEER_SLOTS, LANES), lambda i: (i, 0, 0)),
        ],
        out_shape=[
            jax.ShapeDtypeStruct((t, D_MODEL // 2), jnp.int32),
            jax.ShapeDtypeStruct((t // LANES, PEER_SLOTS, LANES), jnp.int32),
            jax.ShapeDtypeStruct((t // LANES, PEER_SLOTS, LANES), F32),
        ],
        scratch_shapes=[pltpu.VMEM((tm, nq), BF16)],
        compiler_params=_cparams("parallel"),
        name="peer_query",
    )(x, norm_g.reshape(1, D_MODEL), wq, subkeys)


HALF_D = D_MODEL // 2
SC_LANES = 16
SC_UNIT_ROWS = 32
SC_UNITS_PER_TOKEN = PEER_SLOTS // SC_UNIT_ROWS
SC_MAX_TOKEN_BLOCK = 8
SC_BUFFERS = 3
SC_ROW_GROUP = 8
SC_HALF_VECS = HALF_D // SC_LANES
SC_OUT_VECS = 8
SC_BF16_TERMS = 4
SC_ROW_TILES = D_MODEL // LANES


PACK_ROWS = 256


def _pack_kernel(u_ref, v_ref, o_ref):
    def words(x):
        bits = lax.bitcast_convert_type(x.astype(BF16).astype(F32), jnp.int32)
        return bits[:, HALF_D:] | lax.shift_right_logical(bits[:, :HALF_D], 16)

    half_tiles = SC_ROW_TILES // 2
    for base, w in ((0, words(u_ref[...])), (half_tiles, words(v_ref[...]))):
        for j in range(half_tiles):
            o_ref[:, base + j, :] = w[:, j * LANES:(j + 1) * LANES]


def pack_expert_tables(peer_u, peer_v):
    e = peer_u.shape[0]
    return pl.pallas_call(
        _pack_kernel,
        grid=(e // PACK_ROWS,),
        in_specs=[pl.BlockSpec((PACK_ROWS, D_MODEL), lambda i: (i, 0))] * 2,
        out_specs=pl.BlockSpec((PACK_ROWS, SC_ROW_TILES, LANES), lambda i: (i, 0, 0)),
        out_shape=jax.ShapeDtypeStruct((e, SC_ROW_TILES, LANES), jnp.int32),
        compiler_params=_cparams("parallel"),
        name="pack_expert_tables",
    )(peer_u, peer_v)


def sc_peer_experts(table, idx, gates, hp):
    t = hp.shape[0]
    info = plsc.get_sparse_core_info()
    n_workers = info.num_cores * info.num_subcores
    tpw = t // n_workers
    token_block = min(tpw, SC_MAX_TOKEN_BLOCK)
    assert t % (n_workers * token_block) == 0
    units = token_block * SC_UNITS_PER_TOKEN
    mesh = plsc.VectorSubcoreMesh(core_axis_name="core", subcore_axis_name="subcore")
    hi_mask = jnp.int32(-65536)
    gelu_c = math.sqrt(2.0 / math.pi)

    @functools.partial(
        pl.kernel,
        out_type=jax.ShapeDtypeStruct((t, D_MODEL), F32),
        mesh=mesh,
        scratch_types=[
            pltpu.VMEM((token_block * PEER_SLOTS,), jnp.int32),
            pltpu.VMEM((token_block * PEER_SLOTS,), F32),
            pltpu.VMEM((token_block, HALF_D), jnp.int32),
            pltpu.VMEM((token_block, D_MODEL), F32),
            pltpu.VMEM((SC_BUFFERS, SC_UNIT_ROWS, SC_ROW_TILES, LANES), jnp.int32),
            pltpu.VMEM((SC_UNIT_ROWS, SC_LANES), F32),
            pltpu.VMEM((SC_UNIT_ROWS,), jnp.int32),
            pltpu.SemaphoreType.DMA((SC_BUFFERS,)),
        ],
        compiler_params=pltpu.CompilerParams(needs_layout_passes=False),
        name="peer_sc_experts",
    )
    def kern(table_hbm, idx_hbm, gate_hbm, h_hbm, out_hbm, idx_v, gate_v, h_v, out_v, rows_v, part_v, coef_v, sem):
        wid = lax.axis_index("subcore") * info.num_cores + lax.axis_index("core")
        lane = lax.iota(jnp.int32, SC_LANES)
        zero = jnp.zeros((SC_LANES,), F32)

        def gather(unit, b):
            rows = idx_v.at[pl.ds(unit * SC_UNIT_ROWS, SC_UNIT_ROWS)]
            return pltpu.make_async_copy(table_hbm.at[rows], rows_v.at[b], sem.at[b])

        def row_vec(b, r, vec):
            per_tile_row = LANES // SC_LANES
            lane0 = pl.multiple_of((vec % per_tile_row) * SC_LANES, SC_LANES)
            return rows_v[b, r, vec // per_tile_row, pl.ds(lane0, SC_LANES)]

        def unpack(w):
            return lax.bitcast_convert_type(w << 16, F32), lax.bitcast_convert_type(w & hi_mask, F32)

        def as_pairs(w):
            return plsc.bitcast(w, BF16)

        def compute(unit, b):
            tl = unit // SC_UNITS_PER_TOKEN
            q = unit % SC_UNITS_PER_TOKEN

            def row_group(rg, carry):
                def kstep(k, accs):
                    hs = []
                    for j in range(SC_BF16_TERMS):
                        off = pl.multiple_of((k * SC_BF16_TERMS + j) * SC_LANES, SC_LANES)
                        hs.append(as_pairs(h_v[tl, pl.ds(off, SC_LANES)]))
                    new = []
                    for r in range(SC_ROW_GROUP):
                        p = None
                        for j in range(SC_BF16_TERMS):
                            term = as_pairs(row_vec(b, rg * SC_ROW_GROUP + r, k * SC_BF16_TERMS + j)) * hs[j]
                            p = term if p is None else p + term
                        lo, hi = unpack(plsc.bitcast(p, jnp.int32))
                        new.append(accs[r] + lo + hi)
                    return tuple(new)

                accs = lax.fori_loop(0, SC_HALF_VECS // SC_BF16_TERMS, kstep, (zero,) * SC_ROW_GROUP)
                for r in range(SC_ROW_GROUP):
                    part_v[rg * SC_ROW_GROUP + r, :] = accs[r]
                return carry

            lax.fori_loop(0, SC_UNIT_ROWS // SC_ROW_GROUP, row_group, 0)

            dots = []
            for i in range(SC_UNIT_ROWS // SC_LANES):
                rows = lane + i * SC_LANES
                terms = [plsc.load_gather(part_v, [rows, jnp.full((SC_LANES,), l, jnp.int32)])
                         for l in range(SC_LANES)]
                while len(terms) > 1:
                    terms = [terms[j] + terms[j + 1] for j in range(0, len(terms), 2)]
                dots.append(terms[0])
            for i, a in enumerate(dots):
                z = gelu_c * (a + 0.044715 * (a * a * a))
                act = a / (1.0 + jnp.exp(-2.0 * z))
                slot = pl.multiple_of(tl * PEER_SLOTS + q * SC_UNIT_ROWS + i * SC_LANES, SC_LANES)
                bits = lax.bitcast_convert_type(gate_v[pl.ds(slot, SC_LANES)] * act, jnp.int32)
                top = (bits + 0x7FFF + ((bits >> 16) & 1)) & hi_mask
                coef_v[pl.ds(i * SC_LANES, SC_LANES)] = top | lax.shift_right_logical(top, 16)

            def out_pass(dq, carry):
                def row_quad(rq, accs):
                    per_vec = SC_LANES // SC_BF16_TERMS
                    cvec = coef_v[pl.ds(pl.multiple_of((rq // per_vec) * SC_LANES, SC_LANES), SC_LANES)]
                    cs = []
                    for j in range(SC_BF16_TERMS):
                        src = jnp.full((SC_LANES,), (rq % per_vec) * SC_BF16_TERMS + j, jnp.int32)
                        cs.append(as_pairs(cvec.at[src].get(mode="promise_in_bounds")))
                    new = []
                    for k in range(SC_OUT_VECS):
                        p = None
                        for j in range(SC_BF16_TERMS):
                            w = row_vec(b, rq * SC_BF16_TERMS + j, SC_HALF_VECS + dq * SC_OUT_VECS + k)
                            term = as_pairs(w) * cs[j]
                            p = term if p is None else p + term
                        lo, hi = unpack(plsc.bitcast(p, jnp.int32))
                        new.append(accs[2 * k] + lo)
                        new.append(accs[2 * k + 1] + hi)
                    return tuple(new)

                accs = lax.fori_loop(0, SC_UNIT_ROWS // SC_BF16_TERMS, row_quad, (zero,) * (2 * SC_OUT_VECS))
                for k in range(SC_OUT_VECS):
                    off = pl.multiple_of((dq * SC_OUT_VECS + k) * SC_LANES, SC_LANES)
                    plsc.addupdate(out_v.at[tl, pl.ds(off, SC_LANES)], accs[2 * k])
                    plsc.addupdate(out_v.at[tl, pl.ds(HALF_D + off, SC_LANES)], accs[2 * k + 1])
                return carry

            lax.fori_loop(0, SC_HALF_VECS // SC_OUT_VECS, out_pass, 0)

        @pl.loop(0, tpw // token_block)
        def _(blk):
            tok0 = wid * tpw + blk * token_block
            slots = pl.ds(tok0 * PEER_SLOTS, token_block * PEER_SLOTS)
            pltpu.sync_copy(idx_hbm.at[slots], idx_v)
            pltpu.sync_copy(gate_hbm.at[slots], gate_v)
            pltpu.sync_copy(h_hbm.at[pl.ds(tok0, token_block)], h_v)

            @pl.loop(0, token_block)
            def _(tl):
                @pl.loop(0, D_MODEL // SC_LANES)
                def _(k):
                    out_v[tl, pl.ds(pl.multiple_of(k * SC_LANES, SC_LANES), SC_LANES)] = zero

            for u in range(SC_BUFFERS - 1):
                gather(u, u).start()

            @pl.loop(0, units)
            def _(unit):
                ahead = unit + SC_BUFFERS - 1

                @pl.when(ahead < units)
                def _():
                    gather(ahead, ahead % SC_BUFFERS).start()

                b = unit % SC_BUFFERS
                gather(unit, b).wait()
                compute(unit, b)

            pltpu.sync_copy(out_v, out_hbm.at[pl.ds(tok0, token_block)])

    return kern(table, idx, gates, hp)


TC_PEER_TOKENS = 8
TC_PEER_BUFFERS = 4
U_TILE_ROWS = SC_ROW_TILES // 2


def _tc_peer_kernel(idx_ref, idx_next_ref, gate_ref, hp_ref, table_hbm, o_ref, rows_buf, a_buf, c_buf, sem):
    step = pl.program_id(0)
    hi_mask = jnp.int32(-65536)

    def unpack(w):
        return lax.bitcast_convert_type(w << 16, F32), lax.bitcast_convert_type(w & hi_mask, F32)

    def row_copy(ids_ref, tok, e, slot):
        return pltpu.make_async_copy(table_hbm.at[ids_ref[tok * PEER_SLOTS + e]], rows_buf.at[slot, e], sem.at[slot])

    def wait_rows(slot):
        pltpu.make_async_copy(table_hbm.at[pl.ds(0, PEER_SLOTS)], rows_buf.at[slot], sem.at[slot]).wait()

    ahead = TC_PEER_BUFFERS - 1

    @pl.when(step == 0)
    def _():
        for tok in range(ahead):
            def first(e, c, tok=tok):
                row_copy(idx_ref, tok, e, tok).start()
                return c

            lax.fori_loop(0, PEER_SLOTS, first, 0, unroll=8)

    gates_t = gate_ref[...].T
    zeros_v = jnp.zeros((SC_ROW_TILES - U_TILE_ROWS, LANES), F32)
    for t in range(TC_PEER_TOKENS):
        slot = t % TC_PEER_BUFFERS
        nxt = t + ahead
        nxt_slot = nxt % TC_PEER_BUFFERS
        wait_rows(slot)
        h_lo, h_hi = unpack(hp_ref[t])
        h_lo = jnp.concatenate([h_lo, zeros_v], axis=0)
        h_hi = jnp.concatenate([h_hi, zeros_v], axis=0)

        def u_body(e, c, slot=slot, nxt=nxt, nxt_slot=nxt_slot, h_lo=h_lo, h_hi=h_hi):
            if nxt >= TC_PEER_TOKENS:
                @pl.when(step + 1 < pl.num_programs(0))
                def _():
                    row_copy(idx_next_ref, nxt - TC_PEER_TOKENS, e, nxt_slot).start()
            else:
                row_copy(idx_ref, nxt, e, nxt_slot).start()
            lo, hi = unpack(rows_buf[slot, e])
            a_buf[pl.ds(e, 1), :] = jnp.sum(lo * h_lo + hi * h_hi, axis=0, keepdims=True)
            return c

        lax.fori_loop(0, PEER_SLOTS, u_body, 0, unroll=8)
        a = jnp.sum(a_buf[...], axis=1, keepdims=True)
        c_buf[...] = jnp.broadcast_to(gates_t[:, t:t + 1] * _gelu(a), (PEER_SLOTS, LANES))

        def v_body(e, acc, slot=slot):
            lo, hi = unpack(rows_buf[slot, e])
            c = c_buf[pl.ds(e, 1), :]
            return acc[0] + c * lo, acc[1] + c * hi

        zero = jnp.zeros((SC_ROW_TILES, LANES), F32)
        acc_lo, acc_hi = lax.fori_loop(0, PEER_SLOTS, v_body, (zero, zero), unroll=8)
        o_ref[t, 0:U_TILE_ROWS, :] = acc_lo[U_TILE_ROWS:, :]
        o_ref[t, U_TILE_ROWS:, :] = acc_hi[U_TILE_ROWS:, :]


def tc_peer_experts(table, idx, gates, hp):
    t = hp.shape[0]
    assert t % TC_PEER_TOKENS == 0 and TC_PEER_TOKENS % TC_PEER_BUFFERS == 0
    n_steps = t // TC_PEER_TOKENS
    ids = TC_PEER_TOKENS * PEER_SLOTS
    smem = functools.partial(pl.BlockSpec, memory_space=pltpu.SMEM)
    out = pl.pallas_call(
        _tc_peer_kernel,
        grid=(n_steps,),
        in_specs=[
            smem((ids,), lambda i: (i,)),
            smem((ids,), lambda i: (jnp.minimum(i + 1, n_steps - 1),)),
            pl.BlockSpec((TC_PEER_TOKENS, PEER_SLOTS), lambda i: (i, 0)),
            pl.BlockSpec((TC_PEER_TOKENS, U_TILE_ROWS, LANES), lambda i: (i, 0, 0)),
            pl.BlockSpec(memory_space=pl.ANY),
        ],
        out_specs=pl.BlockSpec((TC_PEER_TOKENS, SC_ROW_TILES, LANES), lambda i: (i, 0, 0)),
        out_shape=jax.ShapeDtypeStruct((t, SC_ROW_TILES, LANES), F32),
        scratch_shapes=[
            pltpu.VMEM((TC_PEER_BUFFERS, PEER_SLOTS, SC_ROW_TILES, LANES), jnp.int32),
            pltpu.VMEM((PEER_SLOTS, LANES), F32),
            pltpu.VMEM((PEER_SLOTS, LANES), F32),
            pltpu.SemaphoreType.DMA((TC_PEER_BUFFERS,)),
        ],
        compiler_params=pltpu.CompilerParams(dimension_semantics=("arbitrary",), vmem_limit_bytes=VMEM_LIMIT,
                                             disable_bounds_checks=True),
        name="tc_peer_experts",
    )(idx, idx, gates, hp.reshape(t, U_TILE_ROWS, LANES), table)
    return out.reshape(t, D_MODEL)


TC_PEER_SHARE = 8
TC_PEER_QUANTUM = 1024


def peer_block(x, norm_g, wq, subkeys, table, prev_sc=None):
    t = x.shape[0]
    hp, idx3, gate3 = peer_query(x, norm_g, wq, subkeys)
    idx = jnp.transpose(idx3, (0, 2, 1)).reshape(t * PEER_SLOTS)
    gates = jnp.transpose(gate3, (0, 2, 1)).reshape(t * PEER_SLOTS)
    t_tc = (t // TC_PEER_SHARE) // TC_PEER_QUANTUM * TC_PEER_QUANTUM
    t_sc = t - t_tc
    idx_sc = idx[:t_sc * PEER_SLOTS]
    if prev_sc is not None:
        idx_sc, _ = lax.optimization_barrier((idx_sc, prev_sc))
    on_sc = sc_peer_experts(table, idx_sc, gates[:t_sc * PEER_SLOTS], hp[:t_sc])
    if t_tc == 0:
        return on_sc, (hp,), on_sc
    on_tc = tc_peer_experts(table, idx[t_sc * PEER_SLOTS:], gates[t_sc * PEER_SLOTS:].reshape(t_tc, PEER_SLOTS),
                            hp[t_sc:])
    return jnp.concatenate([on_sc, on_tc], axis=0), (hp, on_tc), on_sc


def _ple_kernel(x_ref, ffn_ref, p_ref, g_ref, wg_ref, wp_ref, gf_ref, o_ref, *, final):
    x = x_ref[...] + ffn_ref[...]
    gate = _sigmoid(jnp.dot(_rms(x, g_ref[...]).astype(BF16), wg_ref[...], preferred_element_type=F32))
    emb = jnp.dot(p_ref[...].astype(BF16), wp_ref[...], preferred_element_type=F32)
    y = x + gate * emb
    o_ref[...] = _rms(y, gf_ref[...]) if final else y


def ple_block(x, ffn, p, norm_g, w_gate, w_proj, norm_final, final):
    t = x.shape[0]
    tm = min(t, 1024)
    row = lambda i: (i, 0)
    full = lambda i: (0, 0)
    return pl.pallas_call(
        functools.partial(_ple_kernel, final=final),
        grid=(t // tm,),
        in_specs=[
            pl.BlockSpec((tm, D_MODEL), row),
            pl.BlockSpec((tm, D_MODEL), row),
            pl.BlockSpec((tm, PLE_DIM), row),
            pl.BlockSpec((1, D_MODEL), full),
            pl.BlockSpec((D_MODEL, D_MODEL), full),
            pl.BlockSpec((PLE_DIM, D_MODEL), full),
            pl.BlockSpec((1, D_MODEL), full),
        ],
        out_specs=pl.BlockSpec((tm, D_MODEL), row),
        out_shape=jax.ShapeDtypeStruct((t, D_MODEL), F32),
        compiler_params=_cparams("parallel"),
        name="ple_block",
    )(x, ffn, p, norm_g.reshape(1, D_MODEL), w_gate, w_proj, norm_final.reshape(1, D_MODEL))


def _prompt_groups(batch):
    sizes = []
    while sum(sizes) < batch:
        nxt = 1 if len(sizes) < 2 else -(-sizes[-1] * 7 // 5)
        sizes.append(min(nxt, batch - sum(sizes)))
    return sizes
def _trunk_layer(x, ple, lw, batch, seq, pool_prefix, pool_start, cache, final, after=None, prev_sc=None):
    if after is not None:
        x, _ = lax.optimization_barrier((x, after))
    uv = norm_matmul(x, lw["norm_mix"], lw["w_uv"], "gelu", F32, 2 * BW)
    mid = norm_matmul(x, lw["norm_mix"], lw["w_mid"], "none", F32, 4 * BW)
    gates = norm_matmul(x, lw["norm_mix"], lw["w_gates"], "sigmoid", BF16, D_MODEL)
    ya, vn = gmlp_mix(uv, lw["gmlp_ln_g"], lw["gmlp_ln_b"], lw["gmlp_ws"], lw["gmlp_bs"], min(seq, GMLP_CHUNK))
    yb, pool_state = pool_mix(mid, pool_prefix, pool_start, lw["pool_w"], lw["pool_scale"], batch, seq)
    if cache is None:
        yc = attn_prompt(mid, lw["rel_bias"], batch, seq)
    else:
        yc = attn_sample(mid, cache[0], cache[1], lw["rel_bias"], batch, seq)
    x = mixer_out(x, ya, yb, yc, gates, lw["w_branch"], lw["w_out"])
    ffn, stage, sc_out = peer_block(x, lw["norm_ffn"], lw["peer_wq"], lw["peer_subkeys"], lw["peer_table"], prev_sc)
    x = ple_block(x, ffn, ple, lw["norm_ple"], lw["ple_gate"], lw["ple_proj"], lw["norm_final"], final)
    return x, mid, pool_state, vn, stage, sc_out


def kernel(x_prompt, x_sample, cache_attn_k, cache_attn_v, state_pool, p_prompt, p_sample, norm_mix, w_in, gmlp_ln_g, gmlp_ln_b, gmlp_ws, gmlp_bs, pool_w, pool_scale, attn_rel_bias, w_branch, w_out, norm_ffn, peer_wq, peer_subkeys, peer_u, peer_v, norm_ple, ple_gate, ple_proj, norm_final):
    bp, lp, _ = x_prompt.shape
    bs, ls, _ = x_sample.shape
    assert lp % BAND_PAST == 0 and lp % GMLP_CHUNK == 0 and ls <= CHUNK
    n_keep = min(BAND_PAST, lp)
    sizes = _prompt_groups(bp)
    starts = [sum(sizes[:g]) for g in range(len(sizes))]
    xg = [x_prompt[a:a + n].reshape(n * lp, D_MODEL) for a, n in zip(starts, sizes)]
    xs = x_sample.reshape(bs * ls, D_MODEL)
    outs = {k: [] for k in ("pk", "pv", "pps", "sk", "sv", "sps", "sgv")}
    after = prev_sc = None
    stages = []
    for i in range(DEPTH):
        tab_u, tab_v = peer_u[i], peer_v[i]
        if stages:
            tab_u, tab_v, _ = lax.optimization_barrier((tab_u, tab_v, stages[len(stages) // 2]))
        stages = []
        w_in_b = w_in[i].astype(BF16)
        lw = dict(
            norm_mix=norm_mix[i],
            w_uv=w_in_b[:, :2 * BW],
            w_mid=w_in_b[:, 2 * BW:6 * BW],
            w_gates=w_in_b[:, 6 * BW:],
            gmlp_ln_g=gmlp_ln_g[i], gmlp_ln_b=gmlp_ln_b[i], gmlp_ws=gmlp_ws[i], gmlp_bs=gmlp_bs[i],
            pool_w=pool_w[i], pool_scale=pool_scale[i], rel_bias=attn_rel_bias[i],
            w_branch=w_branch[i].astype(BF16), w_out=w_out[i].astype(BF16),
            norm_ffn=norm_ffn[i], peer_wq=peer_wq[i].astype(BF16),
            peer_subkeys=peer_subkeys[i].reshape(2 * PEER_HEADS, PEER_NKEYS, PEER_HALF).astype(BF16),
            peer_table=pack_expert_tables(tab_u, tab_v),
            norm_ple=norm_ple[i], ple_gate=ple_gate[i].astype(BF16), ple_proj=ple_proj[i].astype(BF16),
            norm_final=norm_final,
        )
        final = i == DEPTH - 1
        pk, pv, pps = [], [], []
        for g, (a, bg) in enumerate(zip(starts, sizes)):
            ple_g = p_prompt[i, a:a + bg].reshape(bg * lp, PLE_DIM)
            zero_prefix = jnp.zeros((bg, POOL_STATE, BW), F32)
            xg[g], mid_p, ps_p, _, after, prev_sc = _trunk_layer(xg[g], ple_g, lw, bg, lp, zero_prefix, 0, None, final,
                                                                 after, prev_sc)
            stages.append(after)
            mid_p = mid_p.reshape(bg, lp, 4 * BW)
            pk.append(mid_p[:, lp - n_keep:, 2 * BW:3 * BW].reshape(bg, n_keep, HEADS, HEAD_DIM))
            pv.append(mid_p[:, lp - n_keep:, 3 * BW:].reshape(bg, n_keep, HEADS, HEAD_DIM))
            pps.append(ps_p)
        outs["pk"].append(jnp.concatenate(pk, axis=0))
        outs["pv"].append(jnp.concatenate(pv, axis=0))
        outs["pps"].append(jnp.concatenate(pps, axis=0))
        xs, mid_s, ps_s, vn_s, _, _ = _trunk_layer(xs, p_sample[i].reshape(bs * ls, PLE_DIM), lw, bs, ls, state_pool[i],
                                                PAST_LEN, (cache_attn_k[i], cache_attn_v[i]), final)
        mid_s = mid_s.reshape(bs, ls, 4 * BW)
        outs["sk"].append(mid_s[:, :, 2 * BW:3 * BW].reshape(bs, ls, HEADS, HEAD_DIM))
        outs["sv"].append(mid_s[:, :, 3 * BW:].reshape(bs, ls, HEADS, HEAD_DIM))
        outs["sps"].append(ps_s)
        outs["sgv"].append(vn_s.reshape(bs, ls, BW))
    st = lambda k: jnp.stack(outs[k])
    y_prompt = jnp.concatenate(xg, axis=0).reshape(bp, lp, D_MODEL)
    return (y_prompt, xs.reshape(bs, ls, D_MODEL), st("pk"), st("pv"), st("pps"),
            st("sk"), st("sv"), st("sps"), st("sgv"))
```

```python
import functools
import math

import jax
import jax.numpy as jnp
import numpy as np
from jax import lax
from jax.experimental import pallas as pl
from jax.experimental.pallas import tpu as pltpu
from jax.experimental.pallas import tpu_sc as plsc

F32 = jnp.float32
BF16 = jnp.bfloat16

D_MODEL = 1024
DEPTH = 2
CHUNK = 64
EPS = 1e-6
BW = D_MODEL // 2
GMLP_CHUNK = 128
GROUPS = 4
GDIM = BW // GROUPS
POOL_WINDOWS = (2, 4, 8, 16)
POOL_STATE = 15
POOL_PAD = 16
HEADS = 8
HEAD_DIM = BW // HEADS
BAND_CHUNKS = 8
BAND_PAST = BAND_CHUNKS * CHUNK
REL_CLIP = 128
PAST_LEN = 4096
PEER_HEADS = 8
PEER_NKEYS = 128
PEER_HALF = 128
PEER_TOPK = 16
PEER_SLOTS = PEER_HEADS * PEER_TOPK
PLE_DIM = 256

LANES = 128
VMEM_LIMIT = 56 * 1024 * 1024
NEG = -1e30


def _cparams(*sem):
    return pltpu.CompilerParams(dimension_semantics=sem, vmem_limit_bytes=VMEM_LIMIT)


def _rms(x, g):
    ms = jnp.mean(x * x, axis=-1, keepdims=True)
    return x * lax.rsqrt(ms + EPS) * g


def _gelu(x):
    c = math.sqrt(2.0 / math.pi)
    return 0.5 * x * (1.0 + jnp.tanh(c * (x + 0.044715 * (x * x * x))))


def _sigmoid(x):
    return 1.0 / (1.0 + jnp.exp(-x))


_ACTS = {"gelu": _gelu, "sigmoid": _sigmoid, "none": lambda z: z}


def _norm_matmul_kernel(x_ref, g_ref, w_ref, o_ref, h_ref, *, act):
    @pl.when(pl.program_id(1) == 0)
    def _():
        h_ref[...] = _rms(x_ref[...], g_ref[...]).astype(BF16)

    z = jnp.dot(h_ref[...], w_ref[...], preferred_element_type=F32)
    o_ref[...] = _ACTS[act](z).astype(o_ref.dtype)


def norm_matmul(x, g, w, act, out_dtype, tn):
    t, d = x.shape
    n = w.shape[1]
    tm = min(t, 1024)
    return pl.pallas_call(
        functools.partial(_norm_matmul_kernel, act=act),
        grid=(t // tm, n // tn),
        in_specs=[
            pl.BlockSpec((tm, d), lambda i, j: (i, 0)),
            pl.BlockSpec((1, d), lambda i, j: (0, 0)),
            pl.BlockSpec((d, tn), lambda i, j: (0, j)),
        ],
        out_specs=pl.BlockSpec((tm, tn), lambda i, j: (i, j)),
        out_shape=jax.ShapeDtypeStruct((t, n), out_dtype),
        scratch_shapes=[pltpu.VMEM((tm, d), BF16)],
        compiler_params=_cparams("parallel", "arbitrary"),
        name="norm_matmul_" + act,
    )(x, g.reshape(1, d), w)


def _gmlp_kernel(uv_ref, lng_ref, lnb_ref, ws_ref, bst_ref, y_ref, vn_ref, *, lc):
    u = uv_ref[:, :BW]
    v = uv_ref[:, BW:]
    mu = jnp.mean(v, axis=-1, keepdims=True)
    vc = v - mu
    var = jnp.mean(vc * vc, axis=-1, keepdims=True)
    vn = vc * lax.rsqrt(var + EPS) * lng_ref[...] + lnb_ref[...]
    vn_ref[...] = vn
    row = lax.broadcasted_iota(jnp.int32, (lc, lc), 0) // CHUNK
    col = lax.broadcasted_iota(jnp.int32, (lc, lc), 1) // CHUNK
    causal = col <= row
    vnb = vn.astype(BF16)
    for g in range(GROUPS):
        w = jnp.where(causal, ws_ref[g], 0.0).astype(BF16)
        s = jnp.dot(w, vnb[:, g * GDIM:(g + 1) * GDIM], preferred_element_type=F32)
        s = s + bst_ref[:, g:g + 1]
        y_ref[:, g * GDIM:(g + 1) * GDIM] = (u[:, g * GDIM:(g + 1) * GDIM] * s).astype(y_ref.dtype)


def gmlp_mix(uv, ln_g, ln_b, ws, bs, lc):
    t = uv.shape[0]
    return pl.pallas_call(
        functools.partial(_gmlp_kernel, lc=lc),
        grid=(t // lc,),
        in_specs=[
            pl.BlockSpec((lc, 2 * BW), lambda i: (i, 0)),
            pl.BlockSpec((1, BW), lambda i: (0, 0)),
            pl.BlockSpec((1, BW), lambda i: (0, 0)),
            pl.BlockSpec((GROUPS, lc, lc), lambda i: (0, 0, 0)),
            pl.BlockSpec((lc, GROUPS), lambda i: (0, 0)),
        ],
        out_specs=[
            pl.BlockSpec((lc, BW), lambda i: (i, 0)),
            pl.BlockSpec((lc, BW), lambda i: (i, 0)),
        ],
        out_shape=[
            jax.ShapeDtypeStruct((t, BW), BF16),
            jax.ShapeDtypeStruct((t, BW), F32),
        ],
        compiler_params=_cparams("parallel"),
        name="gmlp_mix",
    )(uv, ln_g.reshape(1, BW), ln_b.reshape(1, BW), ws[:, :lc, :lc], bs[:, :lc].T)


def _pool_kernel(u_ref, pre_ref, w_ref, sc_ref, y_ref, st_ref, pad_ref, *, seq, start_pos):
    pad_ref[0:POOL_PAD, :] = pre_ref[...]
    pad_ref[POOL_PAD:, :] = u_ref[...]
    pos = lax.broadcasted_iota(jnp.int32, (seq, 1), 0) + start_pos
    for g, win in enumerate(POOL_WINDOWS):
        cols = slice(g * GDIM, (g + 1) * GDIM)
        tok = pad_ref[POOL_PAD:, cols]
        acc = tok
        for k in range(1, win):
            acc = acc + pad_ref[POOL_PAD - k:POOL_PAD - k + seq, cols]
        cnt = jnp.minimum(pos + 1, win).astype(F32)
        d = acc / cnt - tok
        y = jnp.dot(d.astype(BF16), w_ref[g], preferred_element_type=F32)
        y_ref[:, cols] = (y * sc_ref[:, cols]).astype(y_ref.dtype)
    st_ref[...] = pad_ref[seq + 1:seq + POOL_PAD, :]


def pool_mix(mid, prefix, start_pos, pool_w, pool_scale, batch, seq):
    mid3 = mid.reshape(batch, seq, 4 * BW)
    pre = jnp.concatenate([jnp.zeros((batch, 1, BW), F32), prefix], axis=1)
    y, st = pl.pallas_call(
        functools.partial(_pool_kernel, seq=seq, start_pos=start_pos),
        grid=(batch,),
        in_specs=[
            pl.BlockSpec((None, seq, BW), lambda b: (b, 0, 0)),
            pl.BlockSpec((None, POOL_PAD, BW), lambda b: (b, 0, 0)),
            pl.BlockSpec((GROUPS, GDIM, GDIM), lambda b: (0, 0, 0)),
            pl.BlockSpec((1, BW), lambda b: (0, 0)),
        ],
        out_specs=[
            pl.BlockSpec((None, seq, BW), lambda b: (b, 0, 0)),
            pl.BlockSpec((None, POOL_STATE, BW), lambda b: (b, 0, 0)),
        ],
        out_shape=[
            jax.ShapeDtypeStruct((batch, seq, BW), BF16),
            jax.ShapeDtypeStruct((batch, POOL_STATE, BW), F32),
        ],
        scratch_shapes=[pltpu.VMEM((seq + POOL_PAD, BW), F32)],
        compiler_params=_cparams("parallel"),
        name="pool_mix",
    )(mid3, pre, pool_w.astype(BF16), pool_scale.reshape(1, BW))
    return y.reshape(batch * seq, BW), st


def _attn_chunks(q_ref, kcat_ref, vcat_ref, bias_ref, o_ref, *, n_chunks, cq, band, first_block):
    scale = HEAD_DIM ** -0.5
    heads_per_tile = LANES // HEAD_DIM
    lane_head = lax.broadcasted_iota(jnp.int32, (1, LANES), 1) // HEAD_DIM
    for tile in range(HEADS // heads_per_tile):
        cols = slice(tile * LANES, (tile + 1) * LANES)
        for i in range(n_chunks):
            rows = slice(i * cq, (i + 1) * cq)
            q = (q_ref[rows, cols] * scale).astype(BF16)
            k = kcat_ref[i * cq:i * cq + band, cols]
            v = vcat_ref[i * cq:i * cq + band, cols]
            out = None
            for j in range(heads_per_tile):
                own = lane_head == j
                s = lax.dot_general(jnp.where(own, q, jnp.zeros_like(q)), k, (((1,), (1,)), ((), ())),
                                    preferred_element_type=F32) + bias_ref[tile * heads_per_tile + j]
                if first_block is not None:
                    key = lax.broadcasted_iota(jnp.int32, (1, band), 1)
                    s = jnp.where(key >= first_block * (BAND_PAST - i * cq), s, NEG)
                m = jnp.max(s, axis=-1, keepdims=True)
                p = jnp.exp(s - m)
                l = jnp.sum(p, axis=-1, keepdims=True)
                o = jnp.dot(p.astype(BF16), v, preferred_element_type=F32) / l
                out = o if out is None else jnp.where(own, o, out)
            o_ref[rows, cols] = out.astype(o_ref.dtype)


def _attn_prompt_kernel(q_ref, kp_ref, ko_ref, vp_ref, vo_ref, bias_ref, o_ref, kcat_ref, vcat_ref):
    kcat_ref[0:BAND_PAST, :] = kp_ref[...].astype(BF16)
    kcat_ref[BAND_PAST:, :] = ko_ref[...].astype(BF16)
    vcat_ref[0:BAND_PAST, :] = vp_ref[...].astype(BF16)
    vcat_ref[BAND_PAST:, :] = vo_ref[...].astype(BF16)
    _attn_chunks(q_ref, kcat_ref, vcat_ref, bias_ref, o_ref, n_chunks=BAND_CHUNKS, cq=CHUNK,
                 band=BAND_PAST + CHUNK, first_block=(pl.program_id(1) == 0).astype(jnp.int32))


def _rel_bias_tile(rel_bias, qpos, kpos):
    lq, lk = len(qpos), len(kpos)
    assert (np.diff(qpos) == 1).all() and (np.diff(kpos) == 1).all()
    lr = lq + lk - 1
    dist = (qpos[0] - kpos[0]) - (np.arange(lr) - (lq - 1))
    diag = rel_bias[:, np.clip(dist, -REL_CLIP, REL_CLIP) + REL_CLIP]
    skewed = jnp.tile(diag, (1, lq))[:, lq - 1:lq - 1 + lq * (lr - 1)].reshape(-1, lq, lr - 1)
    return skewed[:, :, :lk]


def attn_prompt(mid, rel_bias, batch, seq):
    mid3 = mid.reshape(batch, seq, 4 * BW)
    blk = BAND_PAST
    bias = _rel_bias_tile(rel_bias, np.arange(CHUNK), np.arange(BAND_PAST + CHUNK) - BAND_PAST)
    prev = lambda b, j: jnp.maximum(j - 1, 0)
    y = pl.pallas_call(
        _attn_prompt_kernel,
        grid=(batch, seq // blk),
        in_specs=[
            pl.BlockSpec((None, blk, BW), lambda b, j: (b, j, 1)),
            pl.BlockSpec((None, blk, BW), lambda b, j: (b, prev(b, j), 2)),
            pl.BlockSpec((None, blk, BW), lambda b, j: (b, j, 2)),
            pl.BlockSpec((None, blk, BW), lambda b, j: (b, prev(b, j), 3)),
            pl.BlockSpec((None, blk, BW), lambda b, j: (b, j, 3)),
            pl.BlockSpec((HEADS, CHUNK, BAND_PAST + CHUNK), lambda b, j: (0, 0, 0)),
        ],
        out_specs=pl.BlockSpec((None, blk, BW), lambda b, j: (b, j, 0)),
        out_shape=jax.ShapeDtypeStruct((batch, seq, BW), BF16),
        scratch_shapes=[pltpu.VMEM((2 * blk, BW), BF16), pltpu.VMEM((2 * blk, BW), BF16)],
        compiler_params=_cparams("parallel", "parallel"),
        name="attn_prompt",
    )(mid3, mid3, mid3, mid3, mid3, bias)
    return y.reshape(batch * seq, BW)


def _attn_sample_kernel(q_ref, kc_ref, kn_ref, vc_ref, vn_ref, bias_ref, o_ref, kcat_ref, vcat_ref, *, n_cache):
    kcat_ref[0:n_cache, :] = kc_ref[...].astype(BF16)
    kcat_ref[n_cache:, :] = kn_ref[...].astype(BF16)
    vcat_ref[0:n_cache, :] = vc_ref[...].astype(BF16)
    vcat_ref[n_cache:, :] = vn_ref[...].astype(BF16)
    seq = q_ref.shape[0]
    _attn_chunks(q_ref, kcat_ref, vcat_ref, bias_ref, o_ref, n_chunks=1, cq=seq, band=n_cache + seq,
                 first_block=None)


def attn_sample(mid, cache_k, cache_v, rel_bias, batch, seq):
    n_cache = cache_k.shape[1]
    assert PAST_LEN >= n_cache
    mid3 = mid.reshape(batch, seq, 4 * BW)
    ck = cache_k.reshape(batch, n_cache, BW)
    cv = cache_v.reshape(batch, n_cache, BW)
    bias = _rel_bias_tile(rel_bias, PAST_LEN + np.arange(seq), PAST_LEN - n_cache + np.arange(n_cache + seq))
    y = pl.pallas_call(
        functools.partial(_attn_sample_kernel, n_cache=n_cache),
        grid=(batch,),
        in_specs=[
            pl.BlockSpec((None, seq, BW), lambda b: (b, 0, 1)),
            pl.BlockSpec((None, n_cache, BW), lambda b: (b, 0, 0)),
            pl.BlockSpec((None, seq, BW), lambda b: (b, 0, 2)),
            pl.BlockSpec((None, n_cache, BW), lambda b: (b, 0, 0)),
            pl.BlockSpec((None, seq, BW), lambda b: (b, 0, 3)),
            pl.BlockSpec((HEADS, seq, n_cache + seq), lambda b: (0, 0, 0)),
        ],
        out_specs=pl.BlockSpec((None, seq, BW), lambda b: (b, 0, 0)),
        out_shape=jax.ShapeDtypeStruct((batch, seq, BW), BF16),
        scratch_shapes=[pltpu.VMEM((n_cache + seq, BW), BF16), pltpu.VMEM((n_cache + seq, BW), BF16)],
        compiler_params=_cparams("parallel"),
        name="attn_sample",
    )(mid3, ck, mid3, cv, mid3, bias)
    return y.reshape(batch * seq, BW)


def _mixer_out_kernel(x_ref, ya_ref, yb_ref, yc_ref, gate_ref, wb_ref, wo_ref, o_ref):
    acc = None
    for n, y_ref in enumerate((ya_ref, yb_ref, yc_ref)):
        proj = jnp.dot(y_ref[...], wb_ref[n], preferred_element_type=F32)
        term = gate_ref[:, n * D_MODEL:(n + 1) * D_MODEL].astype(F32) * proj
        acc = term if acc is None else acc + term
    o_ref[...] = x_ref[...] + jnp.dot(acc.astype(BF16), wo_ref[...], preferred_element_type=F32)


def mixer_out(x, ya, yb, yc, gates, w_branch, w_out):
    t = x.shape[0]
    tm = min(t, 512)
    row = lambda i: (i, 0)
    return pl.pallas_call(
        _mixer_out_kernel,
        grid=(t // tm,),
        in_specs=[
            pl.BlockSpec((tm, D_MODEL), row),
            pl.BlockSpec((tm, BW), row),
            pl.BlockSpec((tm, BW), row),
            pl.BlockSpec((tm, BW), row),
            pl.BlockSpec((tm, 3 * D_MODEL), row),
            pl.BlockSpec((3, BW, D_MODEL), lambda i: (0, 0, 0)),
            pl.BlockSpec((D_MODEL, D_MODEL), lambda i: (0, 0)),
        ],
        out_specs=pl.BlockSpec((tm, D_MODEL), row),
        out_shape=jax.ShapeDtypeStruct((t, D_MODEL), F32),
        compiler_params=_cparams("parallel"),
        name="mixer_out",
    )(x, ya, yb, yc, gates, w_branch, w_out)


def _extract_top(s, payload, k):
    r = float(s.shape[0])
    rows = lax.broadcasted_iota(jnp.int32, s.shape, 0).astype(F32)
    vals, pays = [], []
    for _ in range(k):
        m = jnp.max(s, axis=0, keepdims=True)
        idx = jnp.min(jnp.where(s == m, rows, r), axis=0, keepdims=True)
        sel = rows == idx
        vals.append(m)
        pays.append(idx if payload is None else jnp.max(jnp.where(sel, payload, -1.0), axis=0, keepdims=True))
        s = jnp.where(sel, -jnp.inf, s)
    return jnp.concatenate(vals, axis=0), jnp.concatenate(pays, axis=0)


def _pair_candidates(sv, si):
    k = PEER_TOPK
    sub = 8
    assert k == 2 * sub
    b_row = lax.broadcasted_iota(jnp.int32, (sub, LANES), 0)
    vals = [sv[0][0:1] + sv[1], sv[0][1:2] + sv[1][0:sub]]
    ids = [si[0][0:1] * PEER_NKEYS + si[1], si[0][1:2] * PEER_NKEYS + si[1][0:sub]]
    for a in range(2, sub):
        keep = b_row < k // (a + 1)
        vals.append(jnp.where(keep, sv[0][a:a + 1] + sv[1][0:sub], -jnp.inf))
        ids.append(si[0][a:a + 1] * PEER_NKEYS + si[1][0:sub])
    vals.append(sv[0][sub:k] + sv[1][0:1])
    ids.append(si[0][sub:k] * PEER_NKEYS + si[1][0:1])
    return jnp.concatenate(vals, axis=0), jnp.concatenate(ids, axis=0)


def _peer_query_kernel(x_ref, g_ref, wq_ref, sk_ref, hp_ref, idx_ref, gate_ref, q_ref, *, tm):
    hb = _rms(x_ref[...], g_ref[...]).astype(BF16)
    q_ref[...] = jnp.dot(hb, wq_ref[...], preferred_element_type=F32).astype(BF16)
    bits = lax.bitcast_convert_type(hb.astype(F32), jnp.int32)
    half = D_MODEL // 2
    hp_ref[...] = bits[:, half:] | lax.shift_right_logical(bits[:, :half], 16)

    def sub_block(sb, carry):
        tok = pl.ds(pl.multiple_of(sb * LANES, LANES), LANES)
        for hd in range(PEER_HEADS):
            sv, si = [], []
            for p in range(2):
                hp = hd * 2 + p
                q = q_ref[tok, hp * PEER_HALF:(hp + 1) * PEER_HALF]
                s = lax.dot_general(sk_ref[hp], q, (((1,), (1,)), ((), ())), preferred_element_type=F32)
                v, i = _extract_top(s, None, PEER_TOPK)
                sv.append(v)
                si.append(i)
            cand, eid = _pair_candidates(sv, si)
            tv, te = _extract_top(cand, eid, PEER_TOPK)
            e = jnp.exp(tv - tv[0:1])
            gate = e / jnp.sum(e, axis=0, keepdims=True)
            idx_ref[sb, hd * PEER_TOPK:(hd + 1) * PEER_TOPK, :] = te.astype(jnp.int32)
            gate_ref[sb, hd * PEER_TOPK:(hd + 1) * PEER_TOPK, :] = gate
        return carry

    lax.fori_loop(0, tm // LANES, sub_block, 0)


def peer_query(x, norm_g, wq, subkeys):
    t = x.shape[0]
    tm = min(t, 512)
    nq = wq.shape[1]
    nsb = tm // LANES
    return pl.pallas_call(
        functools.partial(_peer_query_kernel, tm=tm),
        grid=(t // tm,),
        in_specs=[
            pl.BlockSpec((tm, D_MODEL), lambda i: (i, 0)),
            pl.BlockSpec((1, D_MODEL), lambda i: (0, 0)),
            pl.BlockSpec((D_MODEL, nq), lambda i: (0, 0)),
            pl.BlockSpec((2 * PEER_HEADS, PEER_NKEYS, PEER_HALF), lambda i: (0, 0, 0)),
        ],
        out_specs=[
            pl.BlockSpec((tm, D_MODEL // 2), lambda i: (i, 0)),
            pl.BlockSpec((nsb, PEER_SLOTS, LANES), lambda i: (i, 0, 0)),
            pl.BlockSpec((nsb, PEER_SLOTS, LANES), lambda i: (i, 0, 0)),
        ],
        out_shape=[
            jax.ShapeDtypeStruct((t, D_MODEL // 2), jnp.int32),
            jax.ShapeDtypeStruct((t // LANES, PEER_SLOTS, LANES), jnp.int32),
            jax.ShapeDtypeStruct((t // LANES, PEER_SLOTS, LANES), F32),
        ],
        scratch_shapes=[pltpu.VMEM((tm, nq), BF16)],
        compiler_params=_cparams("parallel"),
        name="peer_query",
    )(x, norm_g.reshape(1, D_MODEL), wq, subkeys)


HALF_D = D_MODEL // 2
SC_LANES = 16
SC_UNIT_ROWS = 32
SC_UNITS_PER_TOKEN = PEER_SLOTS // SC_UNIT_ROWS
SC_MAX_TOKEN_BLOCK = 8
SC_BUFFERS = 3
SC_ROW_GROUP = 8
SC_HALF_VECS = HALF_D // SC_LANES
SC_OUT_VECS = 8
SC_BF16_TERMS = 4
SC_ROW_TILES = D_MODEL // LANES


PACK_ROWS = 256


def _pack_kernel(u_ref, v_ref, o_ref):
    def words(x):
        bits = lax.bitcast_convert_type(x.astype(BF16).astype(F32), jnp.int32)
        return bits[:, HALF_D:] | lax.shift_right_logical(bits[:, :HALF_D], 16)

    half_tiles = SC_ROW_TILES // 2
    for base, w in ((0, words(u_ref[...])), (half_tiles, words(v_ref[...]))):
        for j in range(half_tiles):
            o_ref[:, base + j, :] = w[:, j * LANES:(j + 1) * LANES]


def pack_expert_tables(peer_u, peer_v):
    e = peer_u.shape[0]
    return pl.pallas_call(
        _pack_kernel,
        grid=(e // PACK_ROWS,),
        in_specs=[pl.BlockSpec((PACK_ROWS, D_MODEL), lambda i: (i, 0))] * 2,
        out_specs=pl.BlockSpec((PACK_ROWS, SC_ROW_TILES, LANES), lambda i: (i, 0, 0)),
        out_shape=jax.ShapeDtypeStruct((e, SC_ROW_TILES, LANES), jnp.int32),
        compiler_params=_cparams("parallel"),
        name="pack_expert_tables",
    )(peer_u, peer_v)


def sc_peer_experts(table, idx, gates, hp):
    t = hp.shape[0]
    info = plsc.get_sparse_core_info()
    n_workers = info.num_cores * info.num_subcores
    tpw = t // n_workers
    token_block = min(tpw, SC_MAX_TOKEN_BLOCK)
    assert t % (n_workers * token_block) == 0
    units = token_block * SC_UNITS_PER_TOKEN
    mesh = plsc.VectorSubcoreMesh(core_axis_name="core", subcore_axis_name="subcore")
    hi_mask = jnp.int32(-65536)
    gelu_c = math.sqrt(2.0 / math.pi)

    @functools.partial(
        pl.kernel,
        out_type=jax.ShapeDtypeStruct((t, D_MODEL), F32),
        mesh=mesh,
        scratch_types=[
            pltpu.VMEM((2, token_block * PEER_SLOTS), jnp.int32),
            pltpu.VMEM((2, token_block * PEER_SLOTS), F32),
            pltpu.VMEM((2, token_block, HALF_D), jnp.int32),
            pltpu.VMEM((token_block, D_MODEL), F32),
            pltpu.VMEM((SC_BUFFERS, SC_UNIT_ROWS, SC_ROW_TILES, LANES), jnp.int32),
            pltpu.VMEM((SC_UNIT_ROWS, SC_LANES), F32),
            pltpu.VMEM((SC_UNIT_ROWS,), jnp.int32),
            pltpu.SemaphoreType.DMA((SC_BUFFERS,)),
            pltpu.SemaphoreType.DMA((2,)),
        ],
        compiler_params=pltpu.CompilerParams(needs_layout_passes=False),
        name="peer_sc_experts",
    )
    def kern(table_hbm, idx_hbm, gate_hbm, h_hbm, out_hbm, idx_v, gate_v, h_v, out_v, rows_v, part_v, coef_v, sem,
             in_sem):
        wid = lax.axis_index("subcore") * info.num_cores + lax.axis_index("core")
        lane = lax.iota(jnp.int32, SC_LANES)
        zero = jnp.zeros((SC_LANES,), F32)

        def gather(unit, b, s):
            rows = idx_v.at[s, pl.ds(unit * SC_UNIT_ROWS, SC_UNIT_ROWS)]
            return pltpu.make_async_copy(table_hbm.at[rows], rows_v.at[b], sem.at[b])

        def row_vec(b, r, vec):
            per_tile_row = LANES // SC_LANES
            lane0 = pl.multiple_of((vec % per_tile_row) * SC_LANES, SC_LANES)
            return rows_v[b, r, vec // per_tile_row, pl.ds(lane0, SC_LANES)]

        def unpack(w):
            return lax.bitcast_convert_type(w << 16, F32), lax.bitcast_convert_type(w & hi_mask, F32)

        def as_pairs(w):
            return plsc.bitcast(w, BF16)

        def compute(unit, b, s):
            tl = unit // SC_UNITS_PER_TOKEN
            q = unit % SC_UNITS_PER_TOKEN

            def row_group(rg, carry):
                def kstep(k, accs):
                    hs = []
                    for j in range(SC_BF16_TERMS):
                        off = pl.multiple_of((k * SC_BF16_TERMS + j) * SC_LANES, SC_LANES)
                        hs.append(as_pairs(h_v[s, tl, pl.ds(off, SC_LANES)]))
                    new = []
                    for r in range(SC_ROW_GROUP):
                        p = None
                        for j in range(SC_BF16_TERMS):
                            term = as_pairs(row_vec(b, rg * SC_ROW_GROUP + r, k * SC_BF16_TERMS + j)) * hs[j]
                            p = term if p is None else p + term
                        lo, hi = unpack(plsc.bitcast(p, jnp.int32))
                        new.append(accs[r] + lo + hi)
                    return tuple(new)

                accs = lax.fori_loop(0, SC_HALF_VECS // SC_BF16_TERMS, kstep, (zero,) * SC_ROW_GROUP)
                for r in range(SC_ROW_GROUP):
                    part_v[rg * SC_ROW_GROUP + r, :] = accs[r]
                return carry

            lax.fori_loop(0, SC_UNIT_ROWS // SC_ROW_GROUP, row_group, 0)

            dots = []
            for i in range(SC_UNIT_ROWS // SC_LANES):
                rows = lane + i * SC_LANES
                terms = [plsc.load_gather(part_v, [rows, jnp.full((SC_LANES,), l, jnp.int32)])
                         for l in range(SC_LANES)]
                while len(terms) > 1:
                    terms = [terms[j] + terms[j + 1] for j in range(0, len(terms), 2)]
                dots.append(terms[0])
            for i, a in enumerate(dots):
                z = gelu_c * (a + 0.044715 * (a * a * a))
                act = a / (1.0 + jnp.exp(-2.0 * z))
                slot = pl.multiple_of(tl * PEER_SLOTS + q * SC_UNIT_ROWS + i * SC_LANES, SC_LANES)
                bits = lax.bitcast_convert_type(gate_v[s, pl.ds(slot, SC_LANES)] * act, jnp.int32)
                top = (bits + 0x7FFF + ((bits >> 16) & 1)) & hi_mask
                coef_v[pl.ds(i * SC_LANES, SC_LANES)] = top | lax.shift_right_logical(top, 16)

            def out_pass(dq, carry):
                def row_quad(rq, accs):
                    per_vec = SC_LANES // SC_BF16_TERMS
                    cvec = coef_v[pl.ds(pl.multiple_of((rq // per_vec) * SC_LANES, SC_LANES), SC_LANES)]
                    cs = []
                    for j in range(SC_BF16_TERMS):
                        src = jnp.full((SC_LANES,), (rq % per_vec) * SC_BF16_TERMS + j, jnp.int32)
                        cs.append(as_pairs(cvec.at[src].get(mode="promise_in_bounds")))
                    new = []
                    for k in range(SC_OUT_VECS):
                        p = None
                        for j in range(SC_BF16_TERMS):
                            w = row_vec(b, rq * SC_BF16_TERMS + j, SC_HALF_VECS + dq * SC_OUT_VECS + k)
                            term = as_pairs(w) * cs[j]
                            p = term if p is None else p + term
                        lo, hi = unpack(plsc.bitcast(p, jnp.int32))
                        new.append(accs[2 * k] + lo)
                        new.append(accs[2 * k + 1] + hi)
                    return tuple(new)

                accs = lax.fori_loop(0, SC_UNIT_ROWS // SC_BF16_TERMS, row_quad, (zero,) * (2 * SC_OUT_VECS))
                for k in range(SC_OUT_VECS):
                    off = pl.multiple_of((dq * SC_OUT_VECS + k) * SC_LANES, SC_LANES)
                    plsc.addupdate(out_v.at[tl, pl.ds(off, SC_LANES)], accs[2 * k])
                    plsc.addupdate(out_v.at[tl, pl.ds(HALF_D + off, SC_LANES)], accs[2 * k + 1])
                return carry

            lax.fori_loop(0, SC_HALF_VECS // SC_OUT_VECS, out_pass, 0)

        n_blocks = tpw // token_block

        def block_inputs(blk, s):
            tok0 = wid * tpw + blk * token_block
            slots = pl.ds(tok0 * PEER_SLOTS, token_block * PEER_SLOTS)
            return (pltpu.make_async_copy(idx_hbm.at[slots], idx_v.at[s], in_sem.at[s]),
                    pltpu.make_async_copy(gate_hbm.at[slots], gate_v.at[s], in_sem.at[s]),
                    pltpu.make_async_copy(h_hbm.at[pl.ds(tok0, token_block)], h_v.at[s], in_sem.at[s]))

        for c in block_inputs(0, 0):
            c.start()

        @pl.loop(0, n_blocks)
        def _(blk):
            s = blk % 2
            for c in block_inputs(blk, s):
                c.wait()

            @pl.when(blk + 1 < n_blocks)
            def _():
                for c in block_inputs(blk + 1, 1 - s):
                    c.start()

            @pl.loop(0, token_block)
            def _(tl):
                @pl.loop(0, D_MODEL // SC_LANES)
                def _(k):
                    out_v[tl, pl.ds(pl.multiple_of(k * SC_LANES, SC_LANES), SC_LANES)] = zero

            for u in range(SC_BUFFERS - 1):
                gather(u, u, s).start()

            @pl.loop(0, units)
            def _(unit):
                ahead = unit + SC_BUFFERS - 1

                @pl.when(ahead < units)
                def _():
                    gather(ahead, ahead % SC_BUFFERS, s).start()

                b = unit % SC_BUFFERS
                gather(unit, b, s).wait()
                compute(unit, b, s)

            tok0 = wid * tpw + blk * token_block
            pltpu.sync_copy(out_v, out_hbm.at[pl.ds(tok0, token_block)])

    return kern(table, idx, gates, hp)


TC_PEER_TOKENS = 8
TC_PEER_BUFFERS = 4
U_TILE_ROWS = SC_ROW_TILES // 2


def _tc_peer_kernel(idx_ref, idx_next_ref, gate_ref, hp_ref, table_hbm, o_ref, rows_buf, a_buf, c_buf, sem):
    step = pl.program_id(0)
    hi_mask = jnp.int32(-65536)

    def unpack(w):
        return lax.bitcast_convert_type(w << 16, F32), lax.bitcast_convert_type(w & hi_mask, F32)

    def row_copy(ids_ref, tok, e, slot):
        return pltpu.make_async_copy(table_hbm.at[ids_ref[tok * PEER_SLOTS + e]], rows_buf.at[slot, e], sem.at[slot])

    def wait_rows(slot):
        pltpu.make_async_copy(table_hbm.at[pl.ds(0, PEER_SLOTS)], rows_buf.at[slot], sem.at[slot]).wait()

    ahead = TC_PEER_BUFFERS - 1

    @pl.when(step == 0)
    def _():
        for tok in range(ahead):
            def first(e, c, tok=tok):
                row_copy(idx_ref, tok, e, tok).start()
                return c

            lax.fori_loop(0, PEER_SLOTS, first, 0, unroll=8)

    gates_t = gate_ref[...].T
    zeros_v = jnp.zeros((SC_ROW_TILES - U_TILE_ROWS, LANES), F32)
    for t in range(TC_PEER_TOKENS):
        slot = t % TC_PEER_BUFFERS
        nxt = t + ahead
        nxt_slot = nxt % TC_PEER_BUFFERS
        wait_rows(slot)
        h_lo, h_hi = unpack(hp_ref[t])
        h_lo = jnp.concatenate([h_lo, zeros_v], axis=0)
        h_hi = jnp.concatenate([h_hi, zeros_v], axis=0)

        def u_body(e, c, slot=slot, nxt=nxt, nxt_slot=nxt_slot, h_lo=h_lo, h_hi=h_hi):
            if nxt >= TC_PEER_TOKENS:
                @pl.when(step + 1 < pl.num_programs(0))
                def _():
                    row_copy(idx_next_ref, nxt - TC_PEER_TOKENS, e, nxt_slot).start()
            else:
                row_copy(idx_ref, nxt, e, nxt_slot).start()
            lo, hi = unpack(rows_buf[slot, e])
            a_buf[pl.ds(e, 1), :] = jnp.sum(lo * h_lo + hi * h_hi, axis=0, keepdims=True)
            return c

        lax.fori_loop(0, PEER_SLOTS, u_body, 0, unroll=8)
        a = jnp.sum(a_buf[...], axis=1, keepdims=True)
        c_buf[...] = jnp.broadcast_to(gates_t[:, t:t + 1] * _gelu(a), (PEER_SLOTS, LANES))

        def v_body(e, acc, slot=slot):
            lo, hi = unpack(rows_buf[slot, e])
            c = c_buf[pl.ds(e, 1), :]
            return acc[0] + c * lo, acc[1] + c * hi

        zero = jnp.zeros((SC_ROW_TILES, LANES), F32)
        acc_lo, acc_hi = lax.fori_loop(0, PEER_SLOTS, v_body, (zero, zero), unroll=8)
        o_ref[t, 0:U_TILE_ROWS, :] = acc_lo[U_TILE_ROWS:, :]
        o_ref[t, U_TILE_ROWS:, :] = acc_hi[U_TILE_ROWS:, :]


def tc_peer_experts(table, idx, gates, hp):
    t = hp.shape[0]
    assert t % TC_PEER_TOKENS == 0 and TC_PEER_TOKENS % TC_PEER_BUFFERS == 0
    n_steps = t // TC_PEER_TOKENS
    ids = TC_PEER_TOKENS * PEER_SLOTS
    smem = functools.partial(pl.BlockSpec, memory_space=pltpu.SMEM)
    out = pl.pallas_call(
        _tc_peer_kernel,
        grid=(n_steps,),
        in_specs=[
            smem((ids,), lambda i: (i,)),
            smem((ids,), lambda i: (jnp.minimum(i + 1, n_steps - 1),)),
            pl.BlockSpec((TC_PEER_TOKENS, PEER_SLOTS), lambda i: (i, 0)),
            pl.BlockSpec((TC_PEER_TOKENS, U_TILE_ROWS, LANES), lambda i: (i, 0, 0)),
            pl.BlockSpec(memory_space=pl.ANY),
        ],
        out_specs=pl.BlockSpec((TC_PEER_TOKENS, SC_ROW_TILES, LANES), lambda i: (i, 0, 0)),
        out_shape=jax.ShapeDtypeStruct((t, SC_ROW_TILES, LANES), F32),
        scratch_shapes=[
            pltpu.VMEM((TC_PEER_BUFFERS, PEER_SLOTS, SC_ROW_TILES, LANES), jnp.int32),
            pltpu.VMEM((PEER_SLOTS, LANES), F32),
            pltpu.VMEM((PEER_SLOTS, LANES), F32),
            pltpu.SemaphoreType.DMA((TC_PEER_BUFFERS,)),
        ],
        compiler_params=pltpu.CompilerParams(dimension_semantics=("arbitrary",), vmem_limit_bytes=VMEM_LIMIT,
                                             disable_bounds_checks=True),
        name="tc_peer_experts",
    )(idx, idx, gates, hp.reshape(t, U_TILE_ROWS, LANES), table)
    return out.reshape(t, D_MODEL)


TC_PEER_SHARE = 8
TC_PEER_QUANTUM = 1024


def peer_block(x, norm_g, wq, subkeys, table, prev_sc=None):
    t = x.shape[0]
    hp, idx3, gate3 = peer_query(x, norm_g, wq, subkeys)
    idx = jnp.transpose(idx3, (0, 2, 1)).reshape(t * PEER_SLOTS)
    gates = jnp.transpose(gate3, (0, 2, 1)).reshape(t * PEER_SLOTS)
    t_tc = (t // TC_PEER_SHARE) // TC_PEER_QUANTUM * TC_PEER_QUANTUM
    t_sc = t - t_tc
    idx_sc = idx[:t_sc * PEER_SLOTS]
    if prev_sc is not None:
        idx_sc, _ = lax.optimization_barrier((idx_sc, prev_sc))
    on_sc = sc_peer_experts(table, idx_sc, gates[:t_sc * PEER_SLOTS], hp[:t_sc])
    if t_tc == 0:
        return on_sc, (hp,), on_sc
    on_tc = tc_peer_experts(table, idx[t_sc * PEER_SLOTS:], gates[t_sc * PEER_SLOTS:].reshape(t_tc, PEER_SLOTS),
                            hp[t_sc:])
    return jnp.concatenate([on_sc, on_tc], axis=0), (hp, on_tc), on_sc


def _ple_kernel(x_ref, ffn_ref, p_ref, g_ref, wg_ref, wp_ref, gf_ref, o_ref, *, final):
    x = x_ref[...] + ffn_ref[...]
    gate = _sigmoid(jnp.dot(_rms(x, g_ref[...]).astype(BF16), wg_ref[...], preferred_element_type=F32))
    emb = jnp.dot(p_ref[...].astype(BF16), wp_ref[...], preferred_element_type=F32)
    y = x + gate * emb
    o_ref[...] = _rms(y, gf_ref[...]) if final else y


def ple_block(x, ffn, p, norm_g, w_gate, w_proj, norm_final, final):
    t = x.shape[0]
    tm = min(t, 1024)
    row = lambda i: (i, 0)
    full = lambda i: (0, 0)
    return pl.pallas_call(
        functools.partial(_ple_kernel, final=final),
        grid=(t // tm,),
        in_specs=[
            pl.BlockSpec((tm, D_MODEL), row),
            pl.BlockSpec((tm, D_MODEL), row),
            pl.BlockSpec((tm, PLE_DIM), row),
            pl.BlockSpec((1, D_MODEL), full),
            pl.BlockSpec((D_MODEL, D_MODEL), full),
            pl.BlockSpec((PLE_DIM, D_MODEL), full),
            pl.BlockSpec((1, D_MODEL), full),
        ],
        out_specs=pl.BlockSpec((tm, D_MODEL), row),
        out_shape=jax.ShapeDtypeStruct((t, D_MODEL), F32),
        compiler_params=_cparams("parallel"),
        name="ple_block",
    )(x, ffn, p, norm_g.reshape(1, D_MODEL), w_gate, w_proj, norm_final.reshape(1, D_MODEL))


def _prompt_groups(batch):
    sizes = []
    while sum(sizes) < batch:
        nxt = 1 if len(sizes) < 2 else -(-sizes[-1] * 7 // 5)
        sizes.append(min(nxt, batch - sum(sizes)))
    return sizes
def _trunk_layer(x, ple, lw, batch, seq, pool_prefix, pool_start, cache, final, after=None, prev_sc=None):
    if after is not None:
        x, _ = lax.optimization_barrier((x, after))
    uv = norm_matmul(x, lw["norm_mix"], lw["w_uv"], "gelu", F32, 2 * BW)
    mid = norm_matmul(x, lw["norm_mix"], lw["w_mid"], "none", F32, 4 * BW)
    gates = norm_matmul(x, lw["norm_mix"], lw["w_gates"], "sigmoid", BF16, D_MODEL)
    ya, vn = gmlp_mix(uv, lw["gmlp_ln_g"], lw["gmlp_ln_b"], lw["gmlp_ws"], lw["gmlp_bs"], min(seq, GMLP_CHUNK))
    yb, pool_state = pool_mix(mid, pool_prefix, pool_start, lw["pool_w"], lw["pool_scale"], batch, seq)
    if cache is None:
        yc = attn_prompt(mid, lw["rel_bias"], batch, seq)
    else:
        yc = attn_sample(mid, cache[0], cache[1], lw["rel_bias"], batch, seq)
    x = mixer_out(x, ya, yb, yc, gates, lw["w_branch"], lw["w_out"])
    ffn, stage, sc_out = peer_block(x, lw["norm_ffn"], lw["peer_wq"], lw["peer_subkeys"], lw["peer_table"], prev_sc)
    x = ple_block(x, ffn, ple, lw["norm_ple"], lw["ple_gate"], lw["ple_proj"], lw["norm_final"], final)
    return x, mid, pool_state, vn, stage, sc_out


def kernel(x_prompt, x_sample, cache_attn_k, cache_attn_v, state_pool, p_prompt, p_sample, norm_mix, w_in, gmlp_ln_g, gmlp_ln_b, gmlp_ws, gmlp_bs, pool_w, pool_scale, attn_rel_bias, w_branch, w_out, norm_ffn, peer_wq, peer_subkeys, peer_u, peer_v, norm_ple, ple_gate, ple_proj, norm_final):
    bp, lp, _ = x_prompt.shape
    bs, ls, _ = x_sample.shape
    assert lp % BAND_PAST == 0 and lp % GMLP_CHUNK == 0 and ls <= CHUNK
    n_keep = min(BAND_PAST, lp)
    sizes = _prompt_groups(bp)
    starts = [sum(sizes[:g]) for g in range(len(sizes))]
    xg = [x_prompt[a:a + n].reshape(n * lp, D_MODEL) for a, n in zip(starts, sizes)]
    xs = x_sample.reshape(bs * ls, D_MODEL)
    outs = {k: [] for k in ("pk", "pv", "pps", "sk", "sv", "sps", "sgv")}
    after = prev_sc = None
    stages = []
    for i in range(DEPTH):
        tab_u, tab_v = peer_u[i], peer_v[i]
        if stages:
            tab_u, tab_v, _ = lax.optimization_barrier((tab_u, tab_v, stages[len(stages) // 2]))
        stages = []
        w_in_b = w_in[i].astype(BF16)
        lw = dict(
            norm_mix=norm_mix[i],
            w_uv=w_in_b[:, :2 * BW],
            w_mid=w_in_b[:, 2 * BW:6 * BW],
            w_gates=w_in_b[:, 6 * BW:],
            gmlp_ln_g=gmlp_ln_g[i], gmlp_ln_b=gmlp_ln_b[i], gmlp_ws=gmlp_ws[i], gmlp_bs=gmlp_bs[i],
            pool_w=pool_w[i], pool_scale=pool_scale[i], rel_bias=attn_rel_bias[i],
            w_branch=w_branch[i].astype(BF16), w_out=w_out[i].astype(BF16),
            norm_ffn=norm_ffn[i], peer_wq=peer_wq[i].astype(BF16),
            peer_subkeys=peer_subkeys[i].reshape(2 * PEER_HEADS, PEER_NKEYS, PEER_HALF).astype(BF16),
            peer_table=pack_expert_tables(tab_u, tab_v),
            norm_ple=norm_ple[i], ple_gate=ple_gate[i].astype(BF16), ple_proj=ple_proj[i].astype(BF16),
            norm_final=norm_final,
        )
        final = i == DEPTH - 1
        pk, pv, pps = [], [], []
        for g, (a, bg) in enumerate(zip(starts, sizes)):
            ple_g = p_prompt[i, a:a + bg].reshape(bg * lp, PLE_DIM)
            zero_prefix = jnp.zeros((bg, POOL_STATE, BW), F32)
            xg[g], mid_p, ps_p, _, after, prev_sc = _trunk_layer(xg[g], ple_g, lw, bg, lp, zero_prefix, 0, None, final,
                                                                 after, prev_sc)
            stages.append(after)
            mid_p = mid_p.reshape(bg, lp, 4 * BW)
            pk.append(mid_p[:, lp - n_keep:, 2 * BW:3 * BW].reshape(bg, n_keep, HEADS, HEAD_DIM))
            pv.append(mid_p[:, lp - n_keep:, 3 * BW:].reshape(bg, n_keep, HEADS, HEAD_DIM))
            pps.append(ps_p)
        outs["pk"].append(jnp.concatenate(pk, axis=0))
        outs["pv"].append(jnp.concatenate(pv, axis=0))
        outs["pps"].append(jnp.concatenate(pps, axis=0))
        xs, mid_s, ps_s, vn_s, _, _ = _trunk_layer(xs, p_sample[i].reshape(bs * ls, PLE_DIM), lw, bs, ls, state_pool[i],
                                                PAST_LEN, (cache_attn_k[i], cache_attn_v[i]), final)
        mid_s = mid_s.reshape(bs, ls, 4 * BW)
        outs["sk"].append(mid_s[:, :, 2 * BW:3 * BW].reshape(bs, ls, HEADS, HEAD_DIM))
        outs["sv"].append(mid_s[:, :, 3 * BW:].reshape(bs, ls, HEADS, HEAD_DIM))
        outs["sps"].append(ps_s)
        outs["sgv"].append(vn_s.reshape(bs, ls, BW))
    st = lambda k: jnp.stack(outs[k])
    y_prompt = jnp.concatenate(xg, axis=0).reshape(bp, lp, D_MODEL)
    return (y_prompt, xs.reshape(bs, ls, D_MODEL), st("pk"), st("pv"), st("pps"),
            st("sk"), st("sv"), st("sps"), st("sgv"))
```

```python
import functools
import math

import jax
import jax.numpy as jnp
import numpy as np
from jax import lax
from jax.experimental import pallas as pl
from jax.experimental.pallas import tpu as pltpu
from jax.experimental.pallas import tpu_sc as plsc

F32 = jnp.float32
BF16 = jnp.bfloat16

D_MODEL = 1024
DEPTH = 2
CHUNK = 64
EPS = 1e-6
BW = D_MODEL // 2
GMLP_CHUNK = 128
GROUPS = 4
GDIM = BW // GROUPS
POOL_WINDOWS = (2, 4, 8, 16)
POOL_STATE = 15
POOL_PAD = 16
HEADS = 8
HEAD_DIM = BW // HEADS
BAND_CHUNKS = 8
BAND_PAST = BAND_CHUNKS * CHUNK
REL_CLIP = 128
PAST_LEN = 4096
PEER_HEADS = 8
PEER_NKEYS = 128
PEER_HALF = 128
PEER_TOPK = 16
PEER_SLOTS = PEER_HEADS * PEER_TOPK
PLE_DIM = 256

LANES = 128
VMEM_LIMIT = 56 * 1024 * 1024
NEG = -1e30


def _cparams(*sem):
    return pltpu.CompilerParams(dimension_semantics=sem, vmem_limit_bytes=VMEM_LIMIT)


def _rms(x, g):
    ms = jnp.mean(x * x, axis=-1, keepdims=True)
    return x * lax.rsqrt(ms + EPS) * g


def _gelu(x):
    c = math.sqrt(2.0 / math.pi)
    return 0.5 * x * (1.0 + jnp.tanh(c * (x + 0.044715 * (x * x * x))))


def _sigmoid(x):
    return 1.0 / (1.0 + jnp.exp(-x))


_ACTS = {"gelu": _gelu, "sigmoid": _sigmoid, "none": lambda z: z}


def _norm_matmul_kernel(x_ref, g_ref, w_ref, o_ref, h_ref, *, act):
    @pl.when(pl.program_id(1) == 0)
    def _():
        h_ref[...] = _rms(x_ref[...], g_ref[...]).astype(BF16)

    z = jnp.dot(h_ref[...], w_ref[...], preferred_element_type=F32)
    o_ref[...] = _ACTS[act](z).astype(o_ref.dtype)


def norm_matmul(x, g, w, act, out_dtype, tn):
    t, d = x.shape
    n = w.shape[1]
    tm = min(t, 1024)
    return pl.pallas_call(
        functools.partial(_norm_matmul_kernel, act=act),
        grid=(t // tm, n // tn),
        in_specs=[
            pl.BlockSpec((tm, d), lambda i, j: (i, 0)),
            pl.BlockSpec((1, d), lambda i, j: (0, 0)),
            pl.BlockSpec((d, tn), lambda i, j: (0, j)),
        ],
        out_specs=pl.BlockSpec((tm, tn), lambda i, j: (i, j)),
        out_shape=jax.ShapeDtypeStruct((t, n), out_dtype),
        scratch_shapes=[pltpu.VMEM((tm, d), BF16)],
        compiler_params=_cparams("parallel", "arbitrary"),
        name="norm_matmul_" + act,
    )(x, g.reshape(1, d), w)


def _gmlp_kernel(uv_ref, lng_ref, lnb_ref, ws_ref, bst_ref, y_ref, vn_ref, *, lc):
    u = uv_ref[:, :BW]
    v = uv_ref[:, BW:]
    mu = jnp.mean(v, axis=-1, keepdims=True)
    vc = v - mu
    var = jnp.mean(vc * vc, axis=-1, keepdims=True)
    vn = vc * lax.rsqrt(var + EPS) * lng_ref[...] + lnb_ref[...]
    vn_ref[...] = vn
    row = lax.broadcasted_iota(jnp.int32, (lc, lc), 0) // CHUNK
    col = lax.broadcasted_iota(jnp.int32, (lc, lc), 1) // CHUNK
    causal = col <= row
    vnb = vn.astype(BF16)
    for g in range(GROUPS):
        w = jnp.where(causal, ws_ref[g], 0.0).astype(BF16)
        s = jnp.dot(w, vnb[:, g * GDIM:(g + 1) * GDIM], preferred_element_type=F32)
        s = s + bst_ref[:, g:g + 1]
        y_ref[:, g * GDIM:(g + 1) * GDIM] = (u[:, g * GDIM:(g + 1) * GDIM] * s).astype(y_ref.dtype)


def gmlp_mix(uv, ln_g, ln_b, ws, bs, lc):
    t = uv.shape[0]
    return pl.pallas_call(
        functools.partial(_gmlp_kernel, lc=lc),
        grid=(t // lc,),
        in_specs=[
            pl.BlockSpec((lc, 2 * BW), lambda i: (i, 0)),
            pl.BlockSpec((1, BW), lambda i: (0, 0)),
            pl.BlockSpec((1, BW), lambda i: (0, 0)),
            pl.BlockSpec((GROUPS, lc, lc), lambda i: (0, 0, 0)),
            pl.BlockSpec((lc, GROUPS), lambda i: (0, 0)),
        ],
        out_specs=[
            pl.BlockSpec((lc, BW), lambda i: (i, 0)),
            pl.BlockSpec((lc, BW), lambda i: (i, 0)),
        ],
        out_shape=[
            jax.ShapeDtypeStruct((t, BW), BF16),
            jax.ShapeDtypeStruct((t, BW), F32),
        ],
        compiler_params=_cparams("parallel"),
        name="gmlp_mix",
    )(uv, ln_g.reshape(1, BW), ln_b.reshape(1, BW), ws[:, :lc, :lc], bs[:, :lc].T)


def _pool_kernel(u_ref, pre_ref, w_ref, sc_ref, y_ref, st_ref, pad_ref, *, seq, start_pos):
    pad_ref[0:POOL_PAD, :] = pre_ref[...]
    pad_ref[POOL_PAD:, :] = u_ref[...]
    pos = lax.broadcasted_iota(jnp.int32, (seq, 1), 0) + start_pos
    for g, win in enumerate(POOL_WINDOWS):
        cols = slice(g * GDIM, (g + 1) * GDIM)
        tok = pad_ref[POOL_PAD:, cols]
        acc = tok
        for k in range(1, win):
            acc = acc + pad_ref[POOL_PAD - k:POOL_PAD - k + seq, cols]
        cnt = jnp.minimum(pos + 1, win).astype(F32)
        d = acc / cnt - tok
        y = jnp.dot(d.astype(BF16), w_ref[g], preferred_element_type=F32)
        y_ref[:, cols] = (y * sc_ref[:, cols]).astype(y_ref.dtype)
    st_ref[...] = pad_ref[seq + 1:seq + POOL_PAD, :]


def pool_mix(mid, prefix, start_pos, pool_w, pool_scale, batch, seq):
    mid3 = mid.reshape(batch, seq, 4 * BW)
    pre = jnp.concatenate([jnp.zeros((batch, 1, BW), F32), prefix], axis=1)
    y, st = pl.pallas_call(
        functools.partial(_pool_kernel, seq=seq, start_pos=start_pos),
        grid=(batch,),
        in_specs=[
            pl.BlockSpec((None, seq, BW), lambda b: (b, 0, 0)),
            pl.BlockSpec((None, POOL_PAD, BW), lambda b: (b, 0, 0)),
            pl.BlockSpec((GROUPS, GDIM, GDIM), lambda b: (0, 0, 0)),
            pl.BlockSpec((1, BW), lambda b: (0, 0)),
        ],
        out_specs=[
            pl.BlockSpec((None, seq, BW), lambda b: (b, 0, 0)),
            pl.BlockSpec((None, POOL_STATE, BW), lambda b: (b, 0, 0)),
        ],
        out_shape=[
            jax.ShapeDtypeStruct((batch, seq, BW), BF16),
            jax.ShapeDtypeStruct((batch, POOL_STATE, BW), F32),
        ],
        scratch_shapes=[pltpu.VMEM((seq + POOL_PAD, BW), F32)],
        compiler_params=_cparams("parallel"),
        name="pool_mix",
    )(mid3, pre, pool_w.astype(BF16), pool_scale.reshape(1, BW))
    return y.reshape(batch * seq, BW), st


def _attn_chunks(q_ref, kcat_ref, vcat_ref, bias_ref, o_ref, *, n_chunks, cq, band, first_block):
    scale = HEAD_DIM ** -0.5
    heads_per_tile = LANES // HEAD_DIM
    lane_head = lax.broadcasted_iota(jnp.int32, (1, LANES), 1) // HEAD_DIM
    for tile in range(HEADS // heads_per_tile):
        cols = slice(tile * LANES, (tile + 1) * LANES)
        for i in range(n_chunks):
            rows = slice(i * cq, (i + 1) * cq)
            q = (q_ref[rows, cols] * scale).astype(BF16)
            k = kcat_ref[i * cq:i * cq + band, cols]
            v = vcat_ref[i * cq:i * cq + band, cols]
            out = None
            for j in range(heads_per_tile):
                own = lane_head == j
                s = lax.dot_general(jnp.where(own, q, jnp.zeros_like(q)), k, (((1,), (1,)), ((), ())),
                                    preferred_element_type=F32) + bias_ref[tile * heads_per_tile + j]
                if first_block is not None:
                    key = lax.broadcasted_iota(jnp.int32, (1, band), 1)
                    s = jnp.where(key >= first_block * (BAND_PAST - i * cq), s, NEG)
                m = jnp.max(s, axis=-1, keepdims=True)
                p = jnp.exp(s - m)
                l = jnp.sum(p, axis=-1, keepdims=True)
                o = jnp.dot(p.astype(BF16), v, preferred_element_type=F32) / l
                out = o if out is None else jnp.where(own, o, out)
            o_ref[rows, cols] = out.astype(o_ref.dtype)


def _attn_prompt_kernel(q_ref, kp_ref, ko_ref, vp_ref, vo_ref, bias_ref, o_ref, kcat_ref, vcat_ref):
    kcat_ref[0:BAND_PAST, :] = kp_ref[...].astype(BF16)
    kcat_ref[BAND_PAST:, :] = ko_ref[...].astype(BF16)
    vcat_ref[0:BAND_PAST, :] = vp_ref[...].astype(BF16)
    vcat_ref[BAND_PAST:, :] = vo_ref[...].astype(BF16)
    _attn_chunks(q_ref, kcat_ref, vcat_ref, bias_ref, o_ref, n_chunks=BAND_CHUNKS, cq=CHUNK,
                 band=BAND_PAST + CHUNK, first_block=(pl.program_id(1) == 0).astype(jnp.int32))


def _rel_bias_tile(rel_bias, qpos, kpos):
    lq, lk = len(qpos), len(kpos)
    assert (np.diff(qpos) == 1).all() and (np.diff(kpos) == 1).all()
    lr = lq + lk - 1
    dist = (qpos[0] - kpos[0]) - (np.arange(lr) - (lq - 1))
    diag = rel_bias[:, np.clip(dist, -REL_CLIP, REL_CLIP) + REL_CLIP]
    skewed = jnp.tile(diag, (1, lq))[:, lq - 1:lq - 1 + lq * (lr - 1)].reshape(-1, lq, lr - 1)
    return skewed[:, :, :lk]


def attn_prompt(mid, rel_bias, batch, seq):
    mid3 = mid.reshape(batch, seq, 4 * BW)
    blk = BAND_PAST
    bias = _rel_bias_tile(rel_bias, np.arange(CHUNK), np.arange(BAND_PAST + CHUNK) - BAND_PAST)
    prev = lambda b, j: jnp.maximum(j - 1, 0)
    y = pl.pallas_call(
        _attn_prompt_kernel,
        grid=(batch, seq // blk),
        in_specs=[
            pl.BlockSpec((None, blk, BW), lambda b, j: (b, j, 1)),
            pl.BlockSpec((None, blk, BW), lambda b, j: (b, prev(b, j), 2)),
            pl.BlockSpec((None, blk, BW), lambda b, j: (b, j, 2)),
            pl.BlockSpec((None, blk, BW), lambda b, j: (b, prev(b, j), 3)),
            pl.BlockSpec((None, blk, BW), lambda b, j: (b, j, 3)),
            pl.BlockSpec((HEADS, CHUNK, BAND_PAST + CHUNK), lambda b, j: (0, 0, 0)),
        ],
        out_specs=pl.BlockSpec((None, blk, BW), lambda b, j: (b, j, 0)),
        out_shape=jax.ShapeDtypeStruct((batch, seq, BW), BF16),
        scratch_shapes=[pltpu.VMEM((2 * blk, BW), BF16), pltpu.VMEM((2 * blk, BW), BF16)],
        compiler_params=_cparams("parallel", "parallel"),
        name="attn_prompt",
    )(mid3, mid3, mid3, mid3, mid3, bias)
    return y.reshape(batch * seq, BW)


def _attn_sample_kernel(q_ref, kc_ref, kn_ref, vc_ref, vn_ref, bias_ref, o_ref, kcat_ref, vcat_ref, *, n_cache):
    kcat_ref[0:n_cache, :] = kc_ref[...].astype(BF16)
    kcat_ref[n_cache:, :] = kn_ref[...].astype(BF16)
    vcat_ref[0:n_cache, :] = vc_ref[...].astype(BF16)
    vcat_ref[n_cache:, :] = vn_ref[...].astype(BF16)
    seq = q_ref.shape[0]
    _attn_chunks(q_ref, kcat_ref, vcat_ref, bias_ref, o_ref, n_chunks=1, cq=seq, band=n_cache + seq,
                 first_block=None)


def attn_sample(mid, cache_k, cache_v, rel_bias, batch, seq):
    n_cache = cache_k.shape[1]
    assert PAST_LEN >= n_cache
    mid3 = mid.reshape(batch, seq, 4 * BW)
    ck = cache_k.reshape(batch, n_cache, BW)
    cv = cache_v.reshape(batch, n_cache, BW)
    bias = _rel_bias_tile(rel_bias, PAST_LEN + np.arange(seq), PAST_LEN - n_cache + np.arange(n_cache + seq))
    y = pl.pallas_call(
        functools.partial(_attn_sample_kernel, n_cache=n_cache),
        grid=(batch,),
        in_specs=[
            pl.BlockSpec((None, seq, BW), lambda b: (b, 0, 1)),
            pl.BlockSpec((None, n_cache, BW), lambda b: (b, 0, 0)),
            pl.BlockSpec((None, seq, BW), lambda b: (b, 0, 2)),
            pl.BlockSpec((None, n_cache, BW), lambda b: (b, 0, 0)),
            pl.BlockSpec((None, seq, BW), lambda b: (b, 0, 3)),
            pl.BlockSpec((HEADS, seq, n_cache + seq), lambda b: (0, 0, 0)),
        ],
        out_specs=pl.BlockSpec((None, seq, BW), lambda b: (b, 0, 0)),
        out_shape=jax.ShapeDtypeStruct((batch, seq, BW), BF16),
        scratch_shapes=[pltpu.VMEM((n_cache + seq, BW), BF16), pltpu.VMEM((n_cache + seq, BW), BF16)],
        compiler_params=_cparams("parallel"),
        name="attn_sample",
    )(mid3, ck, mid3, cv, mid3, bias)
    return y.reshape(batch * seq, BW)


def _mixer_out_kernel(x_ref, ya_ref, yb_ref, yc_ref, gate_ref, wb_ref, wo_ref, o_ref):
    acc = None
    for n, y_ref in enumerate((ya_ref, yb_ref, yc_ref)):
        proj = jnp.dot(y_ref[...], wb_ref[n], preferred_element_type=F32)
        term = gate_ref[:, n * D_MODEL:(n + 1) * D_MODEL].astype(F32) * proj
        acc = term if acc is None else acc + term
    o_ref[...] = x_ref[...] + jnp.dot(acc.astype(BF16), wo_ref[...], preferred_element_type=F32)


def mixer_out(x, ya, yb, yc, gates, w_branch, w_out):
    t = x.shape[0]
    tm = min(t, 512)
    row = lambda i: (i, 0)
    return pl.pallas_call(
        _mixer_out_kernel,
        grid=(t // tm,),
        in_specs=[
            pl.BlockSpec((tm, D_MODEL), row),
            pl.BlockSpec((tm, BW), row),
            pl.BlockSpec((tm, BW), row),
            pl.BlockSpec((tm, BW), row),
            pl.BlockSpec((tm, 3 * D_MODEL), row),
            pl.BlockSpec((3, BW, D_MODEL), lambda i: (0, 0, 0)),
            pl.BlockSpec((D_MODEL, D_MODEL), lambda i: (0, 0)),
        ],
        out_specs=pl.BlockSpec((tm, D_MODEL), row),
        out_shape=jax.ShapeDtypeStruct((t, D_MODEL), F32),
        compiler_params=_cparams("parallel"),
        name="mixer_out",
    )(x, ya, yb, yc, gates, w_branch, w_out)


def _extract_top(s, payload, k):
    r = float(s.shape[0])
    rows = lax.broadcasted_iota(jnp.int32, s.shape, 0).astype(F32)
    vals, pays = [], []
    for _ in range(k):
        m = jnp.max(s, axis=0, keepdims=True)
        idx = jnp.min(jnp.where(s == m, rows, r), axis=0, keepdims=True)
        sel = rows == idx
        vals.append(m)
        pays.append(idx if payload is None else jnp.max(jnp.where(sel, payload, -1.0), axis=0, keepdims=True))
        s = jnp.where(sel, -jnp.inf, s)
    return jnp.concatenate(vals, axis=0), jnp.concatenate(pays, axis=0)


def _pair_candidates(sv, si):
    k = PEER_TOPK
    sub = 8
    assert k == 2 * sub
    b_row = lax.broadcasted_iota(jnp.int32, (sub, LANES), 0)
    vals = [sv[0][0:1] + sv[1], sv[0][1:2] + sv[1][0:sub]]
    ids = [si[0][0:1] * PEER_NKEYS + si[1], si[0][1:2] * PEER_NKEYS + si[1][0:sub]]
    for a in range(2, sub):
        keep = b_row < k // (a + 1)
        vals.append(jnp.where(keep, sv[0][a:a + 1] + sv[1][0:sub], -jnp.inf))
        ids.append(si[0][a:a + 1] * PEER_NKEYS + si[1][0:sub])
    vals.append(sv[0][sub:k] + sv[1][0:1])
    ids.append(si[0][sub:k] * PEER_NKEYS + si[1][0:1])
    return jnp.concatenate(vals, axis=0), jnp.concatenate(ids, axis=0)


def _peer_query_kernel(x_ref, g_ref, wq_ref, sk_ref, hp_ref, idx_ref, gate_ref, q_ref, *, tm):
    hb = _rms(x_ref[...], g_ref[...]).astype(BF16)
    q_ref[...] = jnp.dot(hb, wq_ref[...], preferred_element_type=F32).astype(BF16)
    bits = lax.bitcast_convert_type(hb.astype(F32), jnp.int32)
    half = D_MODEL // 2
    hp_ref[...] = bits[:, half:] | lax.shift_right_logical(bits[:, :half], 16)

    def sub_block(sb, carry):
        tok = pl.ds(pl.multiple_of(sb * LANES, LANES), LANES)
        for hd in range(PEER_HEADS):
            sv, si = [], []
            for p in range(2):
                hp = hd * 2 + p
                q = q_ref[tok, hp * PEER_HALF:(hp + 1) * PEER_HALF]
                s = lax.dot_general(sk_ref[hp], q, (((1,), (1,)), ((), ())), preferred_element_type=F32)
                v, i = _extract_top(s, None, PEER_TOPK)
                sv.append(v)
                si.append(i)
            cand, eid = _pair_candidates(sv, si)
            tv, te = _extract_top(cand, eid, PEER_TOPK)
            e = jnp.exp(tv - tv[0:1])
            gate = e / jnp.sum(e, axis=0, keepdims=True)
            idx_ref[sb, hd * PEER_TOPK:(hd + 1) * PEER_TOPK, :] = te.astype(jnp.int32)
            gate_ref[sb, hd * PEER_TOPK:(hd + 1) * PEER_TOPK, :] = gate
        return carry

    lax.fori_loop(0, tm // LANES, sub_block, 0)


def peer_query(x, norm_g, wq, subkeys):
    t = x.shape[0]
    tm = min(t, 512)
    nq = wq.shape[1]
    nsb = tm // LANES
    return pl.pallas_call(
        functools.partial(_peer_query_kernel, tm=tm),
        grid=(t // tm,),
        in_specs=[
            pl.BlockSpec((tm, D_MODEL), lambda i: (i, 0)),
            pl.BlockSpec((1, D_MODEL), lambda i: (0, 0)),
            pl.BlockSpec((D_MODEL, nq), lambda i: (0, 0)),
            pl.BlockSpec((2 * PEER_HEADS, PEER_NKEYS, PEER_HALF), lambda i: (0, 0, 0)),
        ],
        out_specs=[
            pl.BlockSpec((tm, D_MODEL // 2), lambda i: (i, 0)),
            pl.BlockSpec((nsb, PEER_SLOTS, LANES), lambda i: (i, 0, 0)),
            pl.BlockSpec((nsb, PEER_SLOTS, LANES), lambda i: (i, 0, 0)),
        ],
        out_shape=[
            jax.ShapeDtypeStruct((t, D_MODEL // 2), jnp.int32),
            jax.ShapeDtypeStruct((t // LANES, PEER_SLOTS, LANES), jnp.int32),
            jax.ShapeDtypeStruct((t // LANES, PEER_SLOTS, LANES), F32),
        ],
        scratch_shapes=[pltpu.VMEM((tm, nq), BF16)],
        compiler_params=_cparams("parallel"),
        name="peer_query",
    )(x, norm_g.reshape(1, D_MODEL), wq, subkeys)


HALF_D = D_MODEL // 2
SC_LANES = 16
SC_UNIT_ROWS = 32
SC_UNITS_PER_TOKEN = PEER_SLOTS // SC_UNIT_ROWS
SC_MAX_TOKEN_BLOCK = 8
SC_BUFFERS = 3
SC_ROW_GROUP = 8
SC_HALF_VECS = HALF_D // SC_LANES
SC_OUT_VECS = 8
SC_BF16_TERMS = 4
SC_ROW_TILES = D_MODEL // LANES


PACK_ROWS = 256


def _pack_kernel(u_ref, v_ref, o_ref):
    def words(x):
        bits = lax.bitcast_convert_type(x.astype(BF16).astype(F32), jnp.int32)
        return bits[:, HALF_D:] | lax.shift_right_logical(bits[:, :HALF_D], 16)

    half_tiles = SC_ROW_TILES // 2
    for base, w in ((0, words(u_ref[...])), (half_tiles, words(v_ref[...]))):
        for j in range(half_tiles):
            o_ref[:, base + j, :] = w[:, j * LANES:(j + 1) * LANES]


def pack_expert_tables(peer_u, peer_v):
    e = peer_u.shape[0]
    return pl.pallas_call(
        _pack_kernel,
        grid=(e // PACK_ROWS,),
        in_specs=[pl.BlockSpec((PACK_ROWS, D_MODEL), lambda i: (i, 0))] * 2,
        out_specs=pl.BlockSpec((PACK_ROWS, SC_ROW_TILES, LANES), lambda i: (i, 0, 0)),
        out_shape=jax.ShapeDtypeStruct((e, SC_ROW_TILES, LANES), jnp.int32),
        compiler_params=_cparams("parallel"),
        name="pack_expert_tables",
    )(peer_u, peer_v)


def sc_peer_experts(table, idx, gates, hp):
    t = hp.shape[0]
    info = plsc.get_sparse_core_info()
    n_workers = info.num_cores * info.num_subcores
    tpw = t // n_workers
    token_block = min(tpw, SC_MAX_TOKEN_BLOCK)
    assert t % (n_workers * token_block) == 0
    units = token_block * SC_UNITS_PER_TOKEN
    mesh = plsc.VectorSubcoreMesh(core_axis_name="core", subcore_axis_name="subcore")
    hi_mask = jnp.int32(-65536)
    gelu_c = math.sqrt(2.0 / math.pi)

    @functools.partial(
        pl.kernel,
        out_type=jax.ShapeDtypeStruct((t, D_MODEL), F32),
        mesh=mesh,
        scratch_types=[
            pltpu.VMEM((2, token_block * PEER_SLOTS), jnp.int32),
            pltpu.VMEM((2, token_block * PEER_SLOTS), F32),
            pltpu.VMEM((2, token_block, HALF_D), jnp.int32),
            pltpu.VMEM((token_block, D_MODEL), F32),
            pltpu.VMEM((SC_BUFFERS, SC_UNIT_ROWS, SC_ROW_TILES, LANES), jnp.int32),
            pltpu.VMEM((SC_UNIT_ROWS, SC_LANES), F32),
            pltpu.VMEM((SC_UNIT_ROWS,), jnp.int32),
            pltpu.SemaphoreType.DMA((SC_BUFFERS,)),
            pltpu.SemaphoreType.DMA((2,)),
        ],
        compiler_params=pltpu.CompilerParams(needs_layout_passes=False),
        name="peer_sc_experts",
    )
    def kern(table_hbm, idx_hbm, gate_hbm, h_hbm, out_hbm, idx_v, gate_v, h_v, out_v, rows_v, part_v, coef_v, sem,
             in_sem):
        wid = lax.axis_index("subcore") * info.num_cores + lax.axis_index("core")
        lane = lax.iota(jnp.int32, SC_LANES)
        zero = jnp.zeros((SC_LANES,), F32)

        def gather(unit, b, s):
            rows = idx_v.at[s, pl.ds(unit * SC_UNIT_ROWS, SC_UNIT_ROWS)]
            return pltpu.make_async_copy(table_hbm.at[rows], rows_v.at[b], sem.at[b])

        def row_vec(b, r, vec):
            per_tile_row = LANES // SC_LANES
            lane0 = pl.multiple_of((vec % per_tile_row) * SC_LANES, SC_LANES)
            return rows_v[b, r, vec // per_tile_row, pl.ds(lane0, SC_LANES)]

        def unpack(w):
            return lax.bitcast_convert_type(w << 16, F32), lax.bitcast_convert_type(w & hi_mask, F32)

        def as_pairs(w):
            return plsc.bitcast(w, BF16)

        def compute(unit, b, s):
            tl = unit // SC_UNITS_PER_TOKEN
            q = unit % SC_UNITS_PER_TOKEN

            def row_group(rg, carry):
                def kstep(k, accs):
                    hs = []
                    for j in range(SC_BF16_TERMS):
                        off = pl.multiple_of((k * SC_BF16_TERMS + j) * SC_LANES, SC_LANES)
                        hs.append(as_pairs(h_v[s, tl, pl.ds(off, SC_LANES)]))
                    new = []
                    for r in range(SC_ROW_GROUP):
                        p = None
                        for j in range(SC_BF16_TERMS):
                            term = as_pairs(row_vec(b, rg * SC_ROW_GROUP + r, k * SC_BF16_TERMS + j)) * hs[j]
                            p = term if p is None else p + term
                        lo, hi = unpack(plsc.bitcast(p, jnp.int32))
                        new.append(accs[r] + lo + hi)
                    return tuple(new)

                accs = lax.fori_loop(0, SC_HALF_VECS // SC_BF16_TERMS, kstep, (zero,) * SC_ROW_GROUP)
                for r in range(SC_ROW_GROUP):
                    part_v[rg * SC_ROW_GROUP + r, :] = accs[r]
                return carry

            lax.fori_loop(0, SC_UNIT_ROWS // SC_ROW_GROUP, row_group, 0)

            dots = []
            for i in range(SC_UNIT_ROWS // SC_LANES):
                rows = lane + i * SC_LANES
                terms = [plsc.load_gather(part_v, [rows, jnp.full((SC_LANES,), l, jnp.int32)])
                         for l in range(SC_LANES)]
                while len(terms) > 1:
                    terms = [terms[j] + terms[j + 1] for j in range(0, len(terms), 2)]
                dots.append(terms[0])
            for i, a in enumerate(dots):
                z = gelu_c * (a + 0.044715 * (a * a * a))
                act = a / (1.0 + jnp.exp(-2.0 * z))
                slot = pl.multiple_of(tl * PEER_SLOTS + q * SC_UNIT_ROWS + i * SC_LANES, SC_LANES)
                bits = lax.bitcast_convert_type(gate_v[s, pl.ds(slot, SC_LANES)] * act, jnp.int32)
                top = (bits + 0x7FFF + ((bits >> 16) & 1)) & hi_mask
                coef_v[pl.ds(i * SC_LANES, SC_LANES)] = top | lax.shift_right_logical(top, 16)

            def out_pass(dq, carry):
                def row_quad(rq, accs):
                    per_vec = SC_LANES // SC_BF16_TERMS
                    cvec = coef_v[pl.ds(pl.multiple_of((rq // per_vec) * SC_LANES, SC_LANES), SC_LANES)]
                    cs = []
                    for j in range(SC_BF16_TERMS):
                        src = jnp.full((SC_LANES,), (rq % per_vec) * SC_BF16_TERMS + j, jnp.int32)
                        cs.append(as_pairs(cvec.at[src].get(mode="promise_in_bounds")))
                    new = []
                    for k in range(SC_OUT_VECS):
                        p = None
                        for j in range(SC_BF16_TERMS):
                            w = row_vec(b, rq * SC_BF16_TERMS + j, SC_HALF_VECS + dq * SC_OUT_VECS + k)
                            term = as_pairs(w) * cs[j]
                            p = term if p is None else p + term
                        lo, hi = unpack(plsc.bitcast(p, jnp.int32))
                        new.append(accs[2 * k] + lo)
                        new.append(accs[2 * k + 1] + hi)
                    return tuple(new)

                accs = lax.fori_loop(0, SC_UNIT_ROWS // SC_BF16_TERMS, row_quad, (zero,) * (2 * SC_OUT_VECS))
                for k in range(SC_OUT_VECS):
                    off = pl.multiple_of((dq * SC_OUT_VECS + k) * SC_LANES, SC_LANES)
                    plsc.addupdate(out_v.at[tl, pl.ds(off, SC_LANES)], accs[2 * k])
                    plsc.addupdate(out_v.at[tl, pl.ds(HALF_D + off, SC_LANES)], accs[2 * k + 1])
                return carry

            lax.fori_loop(0, SC_HALF_VECS // SC_OUT_VECS, out_pass, 0)

        n_blocks = tpw // token_block

        def block_inputs(blk, s):
            tok0 = wid * tpw + blk * token_block
            slots = pl.ds(tok0 * PEER_SLOTS, token_block * PEER_SLOTS)
            return (pltpu.make_async_copy(idx_hbm.at[slots], idx_v.at[s], in_sem.at[s]),
                    pltpu.make_async_copy(gate_hbm.at[slots], gate_v.at[s], in_sem.at[s]),
                    pltpu.make_async_copy(h_hbm.at[pl.ds(tok0, token_block)], h_v.at[s], in_sem.at[s]))

        def first_gathers(s):
            for u in range(SC_BUFFERS - 1):
                gather(u, u, s).start()

        for c in block_inputs(0, 0):
            c.start()
        for c in block_inputs(0, 0):
            c.wait()
        first_gathers(0)

        @pl.loop(0, n_blocks)
        def _(blk):
            s = blk % 2
            more = blk + 1 < n_blocks

            @pl.when(more)
            def _():
                for c in block_inputs(blk + 1, 1 - s):
                    c.start()

            @pl.loop(0, token_block)
            def _(tl):
                @pl.loop(0, D_MODEL // SC_LANES)
                def _(k):
                    out_v[tl, pl.ds(pl.multiple_of(k * SC_LANES, SC_LANES), SC_LANES)] = zero

            @pl.loop(0, units)
            def _(unit):
                ahead = unit + SC_BUFFERS - 1

                @pl.when(ahead < units)
                def _():
                    gather(ahead, ahead % SC_BUFFERS, s).start()

                b = unit % SC_BUFFERS
                gather(unit, b, s).wait()
                compute(unit, b, s)

            @pl.when(more)
            def _():
                for c in block_inputs(blk + 1, 1 - s):
                    c.wait()
                first_gathers(1 - s)

            tok0 = wid * tpw + blk * token_block
            pltpu.sync_copy(out_v, out_hbm.at[pl.ds(tok0, token_block)])

    return kern(table, idx, gates, hp)


TC_PEER_TOKENS = 8
TC_PEER_BUFFERS = 4
U_TILE_ROWS = SC_ROW_TILES // 2


def _tc_peer_kernel(idx_ref, idx_next_ref, gate_ref, hp_ref, table_hbm, o_ref, rows_buf, a_buf, c_buf, sem):
    step = pl.program_id(0)
    hi_mask = jnp.int32(-65536)

    def unpack(w):
        return lax.bitcast_convert_type(w << 16, F32), lax.bitcast_convert_type(w & hi_mask, F32)

    def row_copy(ids_ref, tok, e, slot):
        return pltpu.make_async_copy(table_hbm.at[ids_ref[tok * PEER_SLOTS + e]], rows_buf.at[slot, e], sem.at[slot])

    def wait_rows(slot):
        pltpu.make_async_copy(table_hbm.at[pl.ds(0, PEER_SLOTS)], rows_buf.at[slot], sem.at[slot]).wait()

    ahead = TC_PEER_BUFFERS - 1

    @pl.when(step == 0)
    def _():
        for tok in range(ahead):
            def first(e, c, tok=tok):
                row_copy(idx_ref, tok, e, tok).start()
                return c

            lax.fori_loop(0, PEER_SLOTS, first, 0, unroll=8)

    gates_t = gate_ref[...].T
    zeros_v = jnp.zeros((SC_ROW_TILES - U_TILE_ROWS, LANES), F32)
    for t in range(TC_PEER_TOKENS):
        slot = t % TC_PEER_BUFFERS
        nxt = t + ahead
        nxt_slot = nxt % TC_PEER_BUFFERS
        wait_rows(slot)
        h_lo, h_hi = unpack(hp_ref[t])
        h_lo = jnp.concatenate([h_lo, zeros_v], axis=0)
        h_hi = jnp.concatenate([h_hi, zeros_v], axis=0)

        def u_body(e, c, slot=slot, nxt=nxt, nxt_slot=nxt_slot, h_lo=h_lo, h_hi=h_hi):
            if nxt >= TC_PEER_TOKENS:
                @pl.when(step + 1 < pl.num_programs(0))
                def _():
                    row_copy(idx_next_ref, nxt - TC_PEER_TOKENS, e, nxt_slot).start()
            else:
                row_copy(idx_ref, nxt, e, nxt_slot).start()
            lo, hi = unpack(rows_buf[slot, e])
            a_buf[pl.ds(e, 1), :] = jnp.sum(lo * h_lo + hi * h_hi, axis=0, keepdims=True)
            return c

        lax.fori_loop(0, PEER_SLOTS, u_body, 0, unroll=8)
        a = jnp.sum(a_buf[...], axis=1, keepdims=True)
        c_buf[...] = jnp.broadcast_to(gates_t[:, t:t + 1] * _gelu(a), (PEER_SLOTS, LANES))

        def v_body(e, acc, slot=slot):
            lo, hi = unpack(rows_buf[slot, e])
            c = c_buf[pl.ds(e, 1), :]
            return acc[0] + c * lo, acc[1] + c * hi

        zero = jnp.zeros((SC_ROW_TILES, LANES), F32)
        acc_lo, acc_hi = lax.fori_loop(0, PEER_SLOTS, v_body, (zero, zero), unroll=8)
        o_ref[t, 0:U_TILE_ROWS, :] = acc_lo[U_TILE_ROWS:, :]
        o_ref[t, U_TILE_ROWS:, :] = acc_hi[U_TILE_ROWS:, :]


def tc_peer_experts(table, idx, gates, hp):
    t = hp.shape[0]
    assert t % TC_PEER_TOKENS == 0 and TC_PEER_TOKENS % TC_PEER_BUFFERS == 0
    n_steps = t // TC_PEER_TOKENS
    ids = TC_PEER_TOKENS * PEER_SLOTS
    smem = functools.partial(pl.BlockSpec, memory_space=pltpu.SMEM)
    out = pl.pallas_call(
        _tc_peer_kernel,
        grid=(n_steps,),
        in_specs=[
            smem((ids,), lambda i: (i,)),
            smem((ids,), lambda i: (jnp.minimum(i + 1, n_steps - 1),)),
            pl.BlockSpec((TC_PEER_TOKENS, PEER_SLOTS), lambda i: (i, 0)),
            pl.BlockSpec((TC_PEER_TOKENS, U_TILE_ROWS, LANES), lambda i: (i, 0, 0)),
            pl.BlockSpec(memory_space=pl.ANY),
        ],
        out_specs=pl.BlockSpec((TC_PEER_TOKENS, SC_ROW_TILES, LANES), lambda i: (i, 0, 0)),
        out_shape=jax.ShapeDtypeStruct((t, SC_ROW_TILES, LANES), F32),
        scratch_shapes=[
            pltpu.VMEM((TC_PEER_BUFFERS, PEER_SLOTS, SC_ROW_TILES, LANES), jnp.int32),
            pltpu.VMEM((PEER_SLOTS, LANES), F32),
            pltpu.VMEM((PEER_SLOTS, LANES), F32),
            pltpu.SemaphoreType.DMA((TC_PEER_BUFFERS,)),
        ],
        compiler_params=pltpu.CompilerParams(dimension_semantics=("arbitrary",), vmem_limit_bytes=VMEM_LIMIT,
                                             disable_bounds_checks=True),
        name="tc_peer_experts",
    )(idx, idx, gates, hp.reshape(t, U_TILE_ROWS, LANES), table)
    return out.reshape(t, D_MODEL)


TC_PEER_SHARE = 8
TC_PEER_QUANTUM = 1024


def peer_block(x, norm_g, wq, subkeys, table, prev_sc=None):
    t = x.shape[0]
    hp, idx3, gate3 = peer_query(x, norm_g, wq, subkeys)
    idx = jnp.transpose(idx3, (0, 2, 1)).reshape(t * PEER_SLOTS)
    gates = jnp.transpose(gate3, (0, 2, 1)).reshape(t * PEER_SLOTS)
    t_tc = (t // TC_PEER_SHARE) // TC_PEER_QUANTUM * TC_PEER_QUANTUM
    t_sc = t - t_tc
    idx_sc = idx[:t_sc * PEER_SLOTS]
    if prev_sc is not None:
        idx_sc, _ = lax.optimization_barrier((idx_sc, prev_sc))
    on_sc = sc_peer_experts(table, idx_sc, gates[:t_sc * PEER_SLOTS], hp[:t_sc])
    if t_tc == 0:
        return on_sc, (hp,), on_sc
    on_tc = tc_peer_experts(table, idx[t_sc * PEER_SLOTS:], gates[t_sc * PEER_SLOTS:].reshape(t_tc, PEER_SLOTS),
                            hp[t_sc:])
    return jnp.concatenate([on_sc, on_tc], axis=0), (hp, on_tc), on_sc


def _ple_kernel(x_ref, ffn_ref, p_ref, g_ref, wg_ref, wp_ref, gf_ref, o_ref, *, final):
    x = x_ref[...] + ffn_ref[...]
    gate = _sigmoid(jnp.dot(_rms(x, g_ref[...]).astype(BF16), wg_ref[...], preferred_element_type=F32))
    emb = jnp.dot(p_ref[...].astype(BF16), wp_ref[...], preferred_element_type=F32)
    y = x + gate * emb
    o_ref[...] = _rms(y, gf_ref[...]) if final else y


def ple_block(x, ffn, p, norm_g, w_gate, w_proj, norm_final, final):
    t = x.shape[0]
    tm = min(t, 1024)
    row = lambda i: (i, 0)
    full = lambda i: (0, 0)
    return pl.pallas_call(
        functools.partial(_ple_kernel, final=final),
        grid=(t // tm,),
        in_specs=[
            pl.BlockSpec((tm, D_MODEL), row),
            pl.BlockSpec((tm, D_MODEL), row),
            pl.BlockSpec((tm, PLE_DIM), row),
            pl.BlockSpec((1, D_MODEL), full),
            pl.BlockSpec((D_MODEL, D_MODEL), full),
            pl.BlockSpec((PLE_DIM, D_MODEL), full),
            pl.BlockSpec((1, D_MODEL), full),
        ],
        out_specs=pl.BlockSpec((tm, D_MODEL), row),
        out_shape=jax.ShapeDtypeStruct((t, D_MODEL), F32),
        compiler_params=_cparams("parallel"),
        name="ple_block",
    )(x, ffn, p, norm_g.reshape(1, D_MODEL), w_gate, w_proj, norm_final.reshape(1, D_MODEL))


def _prompt_groups(batch):
    sizes = []
    while sum(sizes) < batch:
        nxt = 1 if len(sizes) < 2 else -(-sizes[-1] * 7 // 5)
        sizes.append(min(nxt, batch - sum(sizes)))
    return sizes
def _trunk_layer(x, ple, lw, batch, seq, pool_prefix, pool_start, cache, final, after=None, prev_sc=None):
    if after is not None:
        x, _ = lax.optimization_barrier((x, after))
    uv = norm_matmul(x, lw["norm_mix"], lw["w_uv"], "gelu", F32, 2 * BW)
    mid = norm_matmul(x, lw["norm_mix"], lw["w_mid"], "none", F32, 4 * BW)
    gates = norm_matmul(x, lw["norm_mix"], lw["w_gates"], "sigmoid", BF16, D_MODEL)
    ya, vn = gmlp_mix(uv, lw["gmlp_ln_g"], lw["gmlp_ln_b"], lw["gmlp_ws"], lw["gmlp_bs"], min(seq, GMLP_CHUNK))
    yb, pool_state = pool_mix(mid, pool_prefix, pool_start, lw["pool_w"], lw["pool_scale"], batch, seq)
    if cache is None:
        yc = attn_prompt(mid, lw["rel_bias"], batch, seq)
    else:
        yc = attn_sample(mid, cache[0], cache[1], lw["rel_bias"], batch, seq)
    x = mixer_out(x, ya, yb, yc, gates, lw["w_branch"], lw["w_out"])
    ffn, stage, sc_out = peer_block(x, lw["norm_ffn"], lw["peer_wq"], lw["peer_subkeys"], lw["peer_table"], prev_sc)
    x = ple_block(x, ffn, ple, lw["norm_ple"], lw["ple_gate"], lw["ple_proj"], lw["norm_final"], final)
    return x, mid, pool_state, vn, stage, sc_out


def kernel(x_prompt, x_sample, cache_attn_k, cache_attn_v, state_pool, p_prompt, p_sample, norm_mix, w_in, gmlp_ln_g, gmlp_ln_b, gmlp_ws, gmlp_bs, pool_w, pool_scale, attn_rel_bias, w_branch, w_out, norm_ffn, peer_wq, peer_subkeys, peer_u, peer_v, norm_ple, ple_gate, ple_proj, norm_final):
    bp, lp, _ = x_prompt.shape
    bs, ls, _ = x_sample.shape
    assert lp % BAND_PAST == 0 and lp % GMLP_CHUNK == 0 and ls <= CHUNK
    n_keep = min(BAND_PAST, lp)
    sizes = _prompt_groups(bp)
    starts = [sum(sizes[:g]) for g in range(len(sizes))]
    xg = [x_prompt[a:a + n].reshape(n * lp, D_MODEL) for a, n in zip(starts, sizes)]
    xs = x_sample.reshape(bs * ls, D_MODEL)
    outs = {k: [] for k in ("pk", "pv", "pps", "sk", "sv", "sps", "sgv")}
    after = prev_sc = None
    stages = []
    for i in range(DEPTH):
        tab_u, tab_v = peer_u[i], peer_v[i]
        if stages:
            tab_u, tab_v, _ = lax.optimization_barrier((tab_u, tab_v, stages[len(stages) // 2]))
        stages = []
        w_in_b = w_in[i].astype(BF16)
        lw = dict(
            norm_mix=norm_mix[i],
            w_uv=w_in_b[:, :2 * BW],
            w_mid=w_in_b[:, 2 * BW:6 * BW],
            w_gates=w_in_b[:, 6 * BW:],
            gmlp_ln_g=gmlp_ln_g[i], gmlp_ln_b=gmlp_ln_b[i], gmlp_ws=gmlp_ws[i], gmlp_bs=gmlp_bs[i],
            pool_w=pool_w[i], pool_scale=pool_scale[i], rel_bias=attn_rel_bias[i],
            w_branch=w_branch[i].astype(BF16), w_out=w_out[i].astype(BF16),
            norm_ffn=norm_ffn[i], peer_wq=peer_wq[i].astype(BF16),
            peer_subkeys=peer_subkeys[i].reshape(2 * PEER_HEADS, PEER_NKEYS, PEER_HALF).astype(BF16),
            peer_table=pack_expert_tables(tab_u, tab_v),
            norm_ple=norm_ple[i], ple_gate=ple_gate[i].astype(BF16), ple_proj=ple_proj[i].astype(BF16),
            norm_final=norm_final,
        )
        final = i == DEPTH - 1
        pk, pv, pps = [], [], []
        for g, (a, bg) in enumerate(zip(starts, sizes)):
            ple_g = p_prompt[i, a:a + bg].reshape(bg * lp, PLE_DIM)
            zero_prefix = jnp.zeros((bg, POOL_STATE, BW), F32)
            xg[g], mid_p, ps_p, _, after, prev_sc = _trunk_layer(xg[g], ple_g, lw, bg, lp, zero_prefix, 0, None, final,
                                                                 after, prev_sc)
            stages.append(after)
            mid_p = mid_p.reshape(bg, lp, 4 * BW)
            pk.append(mid_p[:, lp - n_keep:, 2 * BW:3 * BW].reshape(bg, n_keep, HEADS, HEAD_DIM))
            pv.append(mid_p[:, lp - n_keep:, 3 * BW:].reshape(bg, n_keep, HEADS, HEAD_DIM))
            pps.append(ps_p)
        outs["pk"].append(jnp.concatenate(pk, axis=0))
        outs["pv"].append(jnp.concatenate(pv, axis=0))
        outs["pps"].append(jnp.concatenate(pps, axis=0))
        xs, mid_s, ps_s, vn_s, _, _ = _trunk_layer(xs, p_sample[i].reshape(bs * ls, PLE_DIM), lw, bs, ls, state_pool[i],
                                                PAST_LEN, (cache_attn_k[i], cache_attn_v[i]), final)
        mid_s = mid_s.reshape(bs, ls, 4 * BW)
        outs["sk"].append(mid_s[:, :, 2 * BW:3 * BW].reshape(bs, ls, HEADS, HEAD_DIM))
        outs["sv"].append(mid_s[:, :, 3 * BW:].reshape(bs, ls, HEADS, HEAD_DIM))
        outs["sps"].append(ps_s)
        outs["sgv"].append(vn_s.reshape(bs, ls, BW))
    st = lambda k: jnp.stack(outs[k])
    y_prompt = jnp.concatenate(xg, axis=0).reshape(bp, lp, D_MODEL)
    return (y_prompt, xs.reshape(bs, ls, D_MODEL), st("pk"), st("pv"), st("pps"),
            st("sk"), st("sv"), st("sps"), st("sgv"))
```

```python
import functools
import math

import jax
import jax.numpy as jnp
import numpy as np
from jax import lax
from jax.experimental import pallas as pl
from jax.experimental.pallas import tpu as pltpu
from jax.experimental.pallas import tpu_sc as plsc

F32 = jnp.float32
BF16 = jnp.bfloat16

D_MODEL = 1024
DEPTH = 2
CHUNK = 64
EPS = 1e-6
BW = D_MODEL // 2
GMLP_CHUNK = 128
GROUPS = 4
GDIM = BW // GROUPS
POOL_WINDOWS = (2, 4, 8, 16)
POOL_STATE = 15
POOL_PAD = 16
HEADS = 8
HEAD_DIM = BW // HEADS
BAND_CHUNKS = 8
BAND_PAST = BAND_CHUNKS * CHUNK
REL_CLIP = 128
PAST_LEN = 4096
PEER_HEADS = 8
PEER_NKEYS = 128
PEER_HALF = 128
PEER_TOPK = 16
PEER_SLOTS = PEER_HEADS * PEER_TOPK
PLE_DIM = 256

LANES = 128
VMEM_LIMIT = 56 * 1024 * 1024
NEG = -1e30


def _cparams(*sem):
    return pltpu.CompilerParams(dimension_semantics=sem, vmem_limit_bytes=VMEM_LIMIT)


def _rms(x, g):
    ms = jnp.mean(x * x, axis=-1, keepdims=True)
    return x * lax.rsqrt(ms + EPS) * g


def _gelu(x):
    c = math.sqrt(2.0 / math.pi)
    return 0.5 * x * (1.0 + jnp.tanh(c * (x + 0.044715 * (x * x * x))))


def _sigmoid(x):
    return 1.0 / (1.0 + jnp.exp(-x))


_ACTS = {"gelu": _gelu, "sigmoid": _sigmoid, "none": lambda z: z}


def _norm_matmul_kernel(x_ref, g_ref, w_ref, o_ref, h_ref, *, act):
    @pl.when(pl.program_id(1) == 0)
    def _():
        h_ref[...] = _rms(x_ref[...], g_ref[...]).astype(BF16)

    z = jnp.dot(h_ref[...], w_ref[...], preferred_element_type=F32)
    o_ref[...] = _ACTS[act](z).astype(o_ref.dtype)


def norm_matmul(x, g, w, act, out_dtype, tn):
    t, d = x.shape
    n = w.shape[1]
    tm = min(t, 1024)
    return pl.pallas_call(
        functools.partial(_norm_matmul_kernel, act=act),
        grid=(t // tm, n // tn),
        in_specs=[
            pl.BlockSpec((tm, d), lambda i, j: (i, 0)),
            pl.BlockSpec((1, d), lambda i, j: (0, 0)),
            pl.BlockSpec((d, tn), lambda i, j: (0, j)),
        ],
        out_specs=pl.BlockSpec((tm, tn), lambda i, j: (i, j)),
        out_shape=jax.ShapeDtypeStruct((t, n), out_dtype),
        scratch_shapes=[pltpu.VMEM((tm, d), BF16)],
        compiler_params=_cparams("parallel", "arbitrary"),
        name="norm_matmul_" + act,
    )(x, g.reshape(1, d), w)


def _gmlp_kernel(uv_ref, lng_ref, lnb_ref, ws_ref, bst_ref, y_ref, vn_ref, *, lc):
    u = uv_ref[:, :BW]
    v = uv_ref[:, BW:]
    mu = jnp.mean(v, axis=-1, keepdims=True)
    vc = v - mu
    var = jnp.mean(vc * vc, axis=-1, keepdims=True)
    vn = vc * lax.rsqrt(var + EPS) * lng_ref[...] + lnb_ref[...]
    vn_ref[...] = vn
    row = lax.broadcasted_iota(jnp.int32, (lc, lc), 0) // CHUNK
    col = lax.broadcasted_iota(jnp.int32, (lc, lc), 1) // CHUNK
    causal = col <= row
    vnb = vn.astype(BF16)
    for g in range(GROUPS):
        w = jnp.where(causal, ws_ref[g], 0.0).astype(BF16)
        s = jnp.dot(w, vnb[:, g * GDIM:(g + 1) * GDIM], preferred_element_type=F32)
        s = s + bst_ref[:, g:g + 1]
        y_ref[:, g * GDIM:(g + 1) * GDIM] = (u[:, g * GDIM:(g + 1) * GDIM] * s).astype(y_ref.dtype)


def gmlp_mix(uv, ln_g, ln_b, ws, bs, lc):
    t = uv.shape[0]
    return pl.pallas_call(
        functools.partial(_gmlp_kernel, lc=lc),
        grid=(t // lc,),
        in_specs=[
            pl.BlockSpec((lc, 2 * BW), lambda i: (i, 0)),
            pl.BlockSpec((1, BW), lambda i: (0, 0)),
            pl.BlockSpec((1, BW), lambda i: (0, 0)),
            pl.BlockSpec((GROUPS, lc, lc), lambda i: (0, 0, 0)),
            pl.BlockSpec((lc, GROUPS), lambda i: (0, 0)),
        ],
        out_specs=[
            pl.BlockSpec((lc, BW), lambda i: (i, 0)),
            pl.BlockSpec((lc, BW), lambda i: (i, 0)),
        ],
        out_shape=[
            jax.ShapeDtypeStruct((t, BW), BF16),
            jax.ShapeDtypeStruct((t, BW), F32),
        ],
        compiler_params=_cparams("parallel"),
        name="gmlp_mix",
    )(uv, ln_g.reshape(1, BW), ln_b.reshape(1, BW), ws[:, :lc, :lc], bs[:, :lc].T)


def _pool_kernel(u_ref, pre_ref, w_ref, sc_ref, y_ref, st_ref, pad_ref, *, seq, start_pos):
    pad_ref[0:POOL_PAD, :] = pre_ref[...]
    pad_ref[POOL_PAD:, :] = u_ref[...]
    pos = lax.broadcasted_iota(jnp.int32, (seq, 1), 0) + start_pos
    for g, win in enumerate(POOL_WINDOWS):
        cols = slice(g * GDIM, (g + 1) * GDIM)
        tok = pad_ref[POOL_PAD:, cols]
        acc = tok
        for k in range(1, win):
            acc = acc + pad_ref[POOL_PAD - k:POOL_PAD - k + seq, cols]
        cnt = jnp.minimum(pos + 1, win).astype(F32)
        d = acc / cnt - tok
        y = jnp.dot(d.astype(BF16), w_ref[g], preferred_element_type=F32)
        y_ref[:, cols] = (y * sc_ref[:, cols]).astype(y_ref.dtype)
    st_ref[...] = pad_ref[seq + 1:seq + POOL_PAD, :]


def pool_mix(mid, prefix, start_pos, pool_w, pool_scale, batch, seq):
    mid3 = mid.reshape(batch, seq, 4 * BW)
    pre = jnp.concatenate([jnp.zeros((batch, 1, BW), F32), prefix], axis=1)
    y, st = pl.pallas_call(
        functools.partial(_pool_kernel, seq=seq, start_pos=start_pos),
        grid=(batch,),
        in_specs=[
            pl.BlockSpec((None, seq, BW), lambda b: (b, 0, 0)),
            pl.BlockSpec((None, POOL_PAD, BW), lambda b: (b, 0, 0)),
            pl.BlockSpec((GROUPS, GDIM, GDIM), lambda b: (0, 0, 0)),
            pl.BlockSpec((1, BW), lambda b: (0, 0)),
        ],
        out_specs=[
            pl.BlockSpec((None, seq, BW), lambda b: (b, 0, 0)),
            pl.BlockSpec((None, POOL_STATE, BW), lambda b: (b, 0, 0)),
        ],
        out_shape=[
            jax.ShapeDtypeStruct((batch, seq, BW), BF16),
            jax.ShapeDtypeStruct((batch, POOL_STATE, BW), F32),
        ],
        scratch_shapes=[pltpu.VMEM((seq + POOL_PAD, BW), F32)],
        compiler_params=_cparams("parallel"),
        name="pool_mix",
    )(mid3, pre, pool_w.astype(BF16), pool_scale.reshape(1, BW))
    return y.reshape(batch * seq, BW), st


def _attn_chunks(q_ref, kcat_ref, vcat_ref, bias_ref, o_ref, *, n_chunks, cq, band, first_block):
    scale = HEAD_DIM ** -0.5
    heads_per_tile = LANES // HEAD_DIM
    lane_head = lax.broadcasted_iota(jnp.int32, (1, LANES), 1) // HEAD_DIM
    for tile in range(HEADS // heads_per_tile):
        cols = slice(tile * LANES, (tile + 1) * LANES)
        for i in range(n_chunks):
            rows = slice(i * cq, (i + 1) * cq)
            q = (q_ref[rows, cols] * scale).astype(BF16)
            k = kcat_ref[i * cq:i * cq + band, cols]
            v = vcat_ref[i * cq:i * cq + band, cols]
            out = None
            for j in range(heads_per_tile):
                own = lane_head == j
                s = lax.dot_general(jnp.where(own, q, jnp.zeros_like(q)), k, (((1,), (1,)), ((), ())),
                                    preferred_element_type=F32) + bias_ref[tile * heads_per_tile + j]
                if first_block is not None:
                    key = lax.broadcasted_iota(jnp.int32, (1, band), 1)
                    s = jnp.where(key >= first_block * (BAND_PAST - i * cq), s, NEG)
                m = jnp.max(s, axis=-1, keepdims=True)
                p = jnp.exp(s - m)
                l = jnp.sum(p, axis=-1, keepdims=True)
                o = jnp.dot(p.astype(BF16), v, preferred_element_type=F32) / l
                out = o if out is None else jnp.where(own, o, out)
            o_ref[rows, cols] = out.astype(o_ref.dtype)


def _attn_prompt_kernel(q_ref, kp_ref, ko_ref, vp_ref, vo_ref, bias_ref, o_ref, kcat_ref, vcat_ref):
    kcat_ref[0:BAND_PAST, :] = kp_ref[...].astype(BF16)
    kcat_ref[BAND_PAST:, :] = ko_ref[...].astype(BF16)
    vcat_ref[0:BAND_PAST, :] = vp_ref[...].astype(BF16)
    vcat_ref[BAND_PAST:, :] = vo_ref[...].astype(BF16)
    _attn_chunks(q_ref, kcat_ref, vcat_ref, bias_ref, o_ref, n_chunks=BAND_CHUNKS, cq=CHUNK,
                 band=BAND_PAST + CHUNK, first_block=(pl.program_id(1) == 0).astype(jnp.int32))


def _rel_bias_tile(rel_bias, qpos, kpos):
    lq, lk = len(qpos), len(kpos)
    assert (np.diff(qpos) == 1).all() and (np.diff(kpos) == 1).all()
    lr = lq + lk - 1
    dist = (qpos[0] - kpos[0]) - (np.arange(lr) - (lq - 1))
    diag = rel_bias[:, np.clip(dist, -REL_CLIP, REL_CLIP) + REL_CLIP]
    skewed = jnp.tile(diag, (1, lq))[:, lq - 1:lq - 1 + lq * (lr - 1)].reshape(-1, lq, lr - 1)
    return skewed[:, :, :lk]


def attn_prompt(mid, rel_bias, batch, seq):
    mid3 = mid.reshape(batch, seq, 4 * BW)
    blk = BAND_PAST
    bias = _rel_bias_tile(rel_bias, np.arange(CHUNK), np.arange(BAND_PAST + CHUNK) - BAND_PAST)
    prev = lambda b, j: jnp.maximum(j - 1, 0)
    y = pl.pallas_call(
        _attn_prompt_kernel,
        grid=(batch, seq // blk),
        in_specs=[
            pl.BlockSpec((None, blk, BW), lambda b, j: (b, j, 1)),
            pl.BlockSpec((None, blk, BW), lambda b, j: (b, prev(b, j), 2)),
            pl.BlockSpec((None, blk, BW), lambda b, j: (b, j, 2)),
            pl.BlockSpec((None, blk, BW), lambda b, j: (b, prev(b, j), 3)),
            pl.BlockSpec((None, blk, BW), lambda b, j: (b, j, 3)),
            pl.BlockSpec((HEADS, CHUNK, BAND_PAST + CHUNK), lambda b, j: (0, 0, 0)),
        ],
        out_specs=pl.BlockSpec((None, blk, BW), lambda b, j: (b, j, 0)),
        out_shape=jax.ShapeDtypeStruct((batch, seq, BW), BF16),
        scratch_shapes=[pltpu.VMEM((2 * blk, BW), BF16), pltpu.VMEM((2 * blk, BW), BF16)],
        compiler_params=_cparams("parallel", "parallel"),
        name="attn_prompt",
    )(mid3, mid3, mid3, mid3, mid3, bias)
    return y.reshape(batch * seq, BW)


def _attn_sample_kernel(q_ref, kc_ref, kn_ref, vc_ref, vn_ref, bias_ref, o_ref, kcat_ref, vcat_ref, *, n_cache):
    kcat_ref[0:n_cache, :] = kc_ref[...].astype(BF16)
    kcat_ref[n_cache:, :] = kn_ref[...].astype(BF16)
    vcat_ref[0:n_cache, :] = vc_ref[...].astype(BF16)
    vcat_ref[n_cache:, :] = vn_ref[...].astype(BF16)
    seq = q_ref.shape[0]
    _attn_chunks(q_ref, kcat_ref, vcat_ref, bias_ref, o_ref, n_chunks=1, cq=seq, band=n_cache + seq,
                 first_block=None)


def attn_sample(mid, cache_k, cache_v, rel_bias, batch, seq):
    n_cache = cache_k.shape[1]
    assert PAST_LEN >= n_cache
    mid3 = mid.reshape(batch, seq, 4 * BW)
    ck = cache_k.reshape(batch, n_cache, BW)
    cv = cache_v.reshape(batch, n_cache, BW)
    bias = _rel_bias_tile(rel_bias, PAST_LEN + np.arange(seq), PAST_LEN - n_cache + np.arange(n_cache + seq))
    y = pl.pallas_call(
        functools.partial(_attn_sample_kernel, n_cache=n_cache),
        grid=(batch,),
        in_specs=[
            pl.BlockSpec((None, seq, BW), lambda b: (b, 0, 1)),
            pl.BlockSpec((None, n_cache, BW), lambda b: (b, 0, 0)),
            pl.BlockSpec((None, seq, BW), lambda b: (b, 0, 2)),
            pl.BlockSpec((None, n_cache, BW), lambda b: (b, 0, 0)),
            pl.BlockSpec((None, seq, BW), lambda b: (b, 0, 3)),
            pl.BlockSpec((HEADS, seq, n_cache + seq), lambda b: (0, 0, 0)),
        ],
        out_specs=pl.BlockSpec((None, seq, BW), lambda b: (b, 0, 0)),
        out_shape=jax.ShapeDtypeStruct((batch, seq, BW), BF16),
        scratch_shapes=[pltpu.VMEM((n_cache + seq, BW), BF16), pltpu.VMEM((n_cache + seq, BW), BF16)],
        compiler_params=_cparams("parallel"),
        name="attn_sample",
    )(mid3, ck, mid3, cv, mid3, bias)
    return y.reshape(batch * seq, BW)


def _mixer_out_kernel(x_ref, ya_ref, yb_ref, yc_ref, gate_ref, wb_ref, wo_ref, o_ref):
    acc = None
    for n, y_ref in enumerate((ya_ref, yb_ref, yc_ref)):
        proj = jnp.dot(y_ref[...], wb_ref[n], preferred_element_type=F32)
        term = gate_ref[:, n * D_MODEL:(n + 1) * D_MODEL].astype(F32) * proj
        acc = term if acc is None else acc + term
    o_ref[...] = x_ref[...] + jnp.dot(acc.astype(BF16), wo_ref[...], preferred_element_type=F32)


def mixer_out(x, ya, yb, yc, gates, w_branch, w_out):
    t = x.shape[0]
    tm = min(t, 512)
    row = lambda i: (i, 0)
    return pl.pallas_call(
        _mixer_out_kernel,
        grid=(t // tm,),
        in_specs=[
            pl.BlockSpec((tm, D_MODEL), row),
            pl.BlockSpec((tm, BW), row),
            pl.BlockSpec((tm, BW), row),
            pl.BlockSpec((tm, BW), row),
            pl.BlockSpec((tm, 3 * D_MODEL), row),
            pl.BlockSpec((3, BW, D_MODEL), lambda i: (0, 0, 0)),
            pl.BlockSpec((D_MODEL, D_MODEL), lambda i: (0, 0)),
        ],
        out_specs=pl.BlockSpec((tm, D_MODEL), row),
        out_shape=jax.ShapeDtypeStruct((t, D_MODEL), F32),
        compiler_params=_cparams("parallel"),
        name="mixer_out",
    )(x, ya, yb, yc, gates, w_branch, w_out)


def _extract_top(s, payload, k):
    r = float(s.shape[0])
    rows = lax.broadcasted_iota(jnp.int32, s.shape, 0).astype(F32)
    vals, pays = [], []
    for _ in range(k):
        m = jnp.max(s, axis=0, keepdims=True)
        idx = jnp.min(jnp.where(s == m, rows, r), axis=0, keepdims=True)
        sel = rows == idx
        vals.append(m)
        pays.append(idx if payload is None else jnp.max(jnp.where(sel, payload, -1.0), axis=0, keepdims=True))
        s = jnp.where(sel, -jnp.inf, s)
    return jnp.concatenate(vals, axis=0), jnp.concatenate(pays, axis=0)


def _pair_candidates(sv, si):
    k = PEER_TOPK
    sub = 8
    assert k == 2 * sub
    b_row = lax.broadcasted_iota(jnp.int32, (sub, LANES), 0)
    vals = [sv[0][0:1] + sv[1], sv[0][1:2] + sv[1][0:sub]]
    ids = [si[0][0:1] * PEER_NKEYS + si[1], si[0][1:2] * PEER_NKEYS + si[1][0:sub]]
    for a in range(2, sub):
        keep = b_row < k // (a + 1)
        vals.append(jnp.where(keep, sv[0][a:a + 1] + sv[1][0:sub], -jnp.inf))
        ids.append(si[0][a:a + 1] * PEER_NKEYS + si[1][0:sub])
    vals.append(sv[0][sub:k] + sv[1][0:1])
    ids.append(si[0][sub:k] * PEER_NKEYS + si[1][0:1])
    return jnp.concatenate(vals, axis=0), jnp.concatenate(ids, axis=0)


def _peer_query_kernel(x_ref, g_ref, wq_ref, sk_ref, hp_ref, idx_ref, gate_ref, q_ref, *, tm):
    hb = _rms(x_ref[...], g_ref[...]).astype(BF16)
    q_ref[...] = jnp.dot(hb, wq_ref[...], preferred_element_type=F32).astype(BF16)
    bits = lax.bitcast_convert_type(hb.astype(F32), jnp.int32)
    half = D_MODEL // 2
    hp_ref[...] = bits[:, half:] | lax.shift_right_logical(bits[:, :half], 16)

    def sub_block(sb, carry):
        tok = pl.ds(pl.multiple_of(sb * LANES, LANES), LANES)
        for hd in range(PEER_HEADS):
            sv, si = [], []
            for p in range(2):
                hp = hd * 2 + p
                q = q_ref[tok, hp * PEER_HALF:(hp + 1) * PEER_HALF]
                s = lax.dot_general(sk_ref[hp], q, (((1,), (1,)), ((), ())), preferred_element_type=F32)
                v, i = _extract_top(s, None, PEER_TOPK)
                sv.append(v)
                si.append(i)
            cand, eid = _pair_candidates(sv, si)
            tv, te = _extract_top(cand, eid, PEER_TOPK)
            e = jnp.exp(tv - tv[0:1])
            gate = e / jnp.sum(e, axis=0, keepdims=True)
            idx_ref[sb, hd * PEER_TOPK:(hd + 1) * PEER_TOPK, :] = te.astype(jnp.int32)
            gate_ref[sb, hd * PEER_TOPK:(hd + 1) * PEER_TOPK, :] = gate
        return carry

    lax.fori_loop(0, tm // LANES, sub_block, 0)


def peer_query(x, norm_g, wq, subkeys):
    t = x.shape[0]
    tm = min(t, 512)
    nq = wq.shape[1]
    nsb = tm // LANES
    return pl.pallas_call(
        functools.partial(_peer_query_kernel, tm=tm),
        grid=(t // tm,),
        in_specs=[
            pl.BlockSpec((tm, D_MODEL), lambda i: (i, 0)),
            pl.BlockSpec((1, D_MODEL), lambda i: (0, 0)),
            pl.BlockSpec((D_MODEL, nq), lambda i: (0, 0)),
            pl.BlockSpec((2 * PEER_HEADS, PEER_NKEYS, PEER_HALF), lambda i: (0, 0, 0)),
        ],
        out_specs=[
            pl.BlockSpec((tm, D_MODEL // 2), lambda i: (i, 0)),
            pl.BlockSpec((nsb, PEER_SLOTS, LANES), lambda i: (i, 0, 0)),
            pl.BlockSpec((nsb, PEER_SLOTS, LANES), lambda i: (i, 0, 0)),
        ],
        out_shape=[
            jax.ShapeDtypeStruct((t, D_MODEL // 2), jnp.int32),
            jax.ShapeDtypeStruct((t // LANES, PEER_SLOTS, LANES), jnp.int32),
            jax.ShapeDtypeStruct((t // LANES, PEER_SLOTS, LANES), F32),
        ],
        scratch_shapes=[pltpu.VMEM((tm, nq), BF16)],
        compiler_params=_cparams("parallel"),
        name="peer_query",
    )(x, norm_g.reshape(1, D_MODEL), wq, subkeys)


HALF_D = D_MODEL // 2
SC_LANES = 16
SC_UNIT_ROWS = 32
SC_UNITS_PER_TOKEN = PEER_SLOTS // SC_UNIT_ROWS
SC_MAX_TOKEN_BLOCK = 8
SC_BUFFERS = 3
SC_ROW_GROUP = 8
SC_HALF_VECS = HALF_D // SC_LANES
SC_OUT_VECS = 8
SC_BF16_TERMS = 4
SC_ROW_TILES = D_MODEL // LANES


PACK_ROWS = 256


def _pack_kernel(u_ref, v_ref, o_ref):
    def words(x):
        bits = lax.bitcast_convert_type(x.astype(BF16).astype(F32), jnp.int32)
        return bits[:, HALF_D:] | lax.shift_right_logical(bits[:, :HALF_D], 16)

    half_tiles = SC_ROW_TILES // 2
    for base, w in ((0, words(u_ref[...])), (half_tiles, words(v_ref[...]))):
        for j in range(half_tiles):
            o_ref[:, base + j, :] = w[:, j * LANES:(j + 1) * LANES]


def pack_expert_tables(peer_u, peer_v):
    e = peer_u.shape[0]
    return pl.pallas_call(
        _pack_kernel,
        grid=(e // PACK_ROWS,),
        in_specs=[pl.BlockSpec((PACK_ROWS, D_MODEL), lambda i: (i, 0))] * 2,
        out_specs=pl.BlockSpec((PACK_ROWS, SC_ROW_TILES, LANES), lambda i: (i, 0, 0)),
        out_shape=jax.ShapeDtypeStruct((e, SC_ROW_TILES, LANES), jnp.int32),
        compiler_params=_cparams("parallel"),
        name="pack_expert_tables",
    )(peer_u, peer_v)


def sc_peer_experts(table, idx, gates, hp):
    t = hp.shape[0]
    info = plsc.get_sparse_core_info()
    n_workers = info.num_cores * info.num_subcores
    tpw = t // n_workers
    token_block = min(tpw, SC_MAX_TOKEN_BLOCK)
    assert t % (n_workers * token_block) == 0
    units = token_block * SC_UNITS_PER_TOKEN
    mesh = plsc.VectorSubcoreMesh(core_axis_name="core", subcore_axis_name="subcore")
    hi_mask = jnp.int32(-65536)
    gelu_c = math.sqrt(2.0 / math.pi)

    @functools.partial(
        pl.kernel,
        out_type=jax.ShapeDtypeStruct((t, D_MODEL), F32),
        mesh=mesh,
        scratch_types=[
            pltpu.VMEM((2, token_block * PEER_SLOTS), jnp.int32),
            pltpu.VMEM((2, token_block * PEER_SLOTS), F32),
            pltpu.VMEM((2, token_block, HALF_D), jnp.int32),
            pltpu.VMEM((token_block, D_MODEL), F32),
            pltpu.VMEM((SC_BUFFERS, SC_UNIT_ROWS, SC_ROW_TILES, LANES), jnp.int32),
            pltpu.VMEM((SC_UNIT_ROWS, SC_LANES), F32),
            pltpu.VMEM((SC_UNIT_ROWS,), jnp.int32),
            pltpu.SemaphoreType.DMA((SC_BUFFERS,)),
            pltpu.SemaphoreType.DMA((2,)),
        ],
        compiler_params=pltpu.CompilerParams(needs_layout_passes=False),
        name="peer_sc_experts",
    )
    def kern(table_hbm, idx_hbm, gate_hbm, h_hbm, out_hbm, idx_v, gate_v, h_v, out_v, rows_v, part_v, coef_v, sem,
             in_sem):
        wid = lax.axis_index("subcore") * info.num_cores + lax.axis_index("core")
        lane = lax.iota(jnp.int32, SC_LANES)
        zero = jnp.zeros((SC_LANES,), F32)

        def gather(unit, b, s):
            rows = idx_v.at[s, pl.ds(unit * SC_UNIT_ROWS, SC_UNIT_ROWS)]
            return pltpu.make_async_copy(table_hbm.at[rows], rows_v.at[b], sem.at[b])

        def row_vec(b, r, vec):
            per_tile_row = LANES // SC_LANES
            lane0 = pl.multiple_of((vec % per_tile_row) * SC_LANES, SC_LANES)
            return rows_v[b, r, vec // per_tile_row, pl.ds(lane0, SC_LANES)]

        def unpack(w):
            return lax.bitcast_convert_type(w << 16, F32), lax.bitcast_convert_type(w & hi_mask, F32)

        def as_pairs(w):
            return plsc.bitcast(w, BF16)

        def compute(unit, b, s):
            tl = unit // SC_UNITS_PER_TOKEN
            q = unit % SC_UNITS_PER_TOKEN

            def row_group(rg, carry):
                def kstep(k, accs):
                    hs = []
                    for j in range(SC_BF16_TERMS):
                        off = pl.multiple_of((k * SC_BF16_TERMS + j) * SC_LANES, SC_LANES)
                        hs.append(as_pairs(h_v[s, tl, pl.ds(off, SC_LANES)]))
                    new = []
                    for r in range(SC_ROW_GROUP):
                        p = None
                        for j in range(SC_BF16_TERMS):
                            term = as_pairs(row_vec(b, rg * SC_ROW_GROUP + r, k * SC_BF16_TERMS + j)) * hs[j]
                            p = term if p is None else p + term
                        lo, hi = unpack(plsc.bitcast(p, jnp.int32))
                        new.append(accs[r] + lo + hi)
                    return tuple(new)

                accs = lax.fori_loop(0, SC_HALF_VECS // SC_BF16_TERMS, kstep, (zero,) * SC_ROW_GROUP)
                for r in range(SC_ROW_GROUP):
                    part_v[rg * SC_ROW_GROUP + r, :] = accs[r]
                return carry

            lax.fori_loop(0, SC_UNIT_ROWS // SC_ROW_GROUP, row_group, 0)

            dots = []
            for i in range(SC_UNIT_ROWS // SC_LANES):
                rows = lane + i * SC_LANES
                terms = [plsc.load_gather(part_v, [rows, jnp.full((SC_LANES,), l, jnp.int32)])
                         for l in range(SC_LANES)]
                while len(terms) > 1:
                    terms = [terms[j] + terms[j + 1] for j in range(0, len(terms), 2)]
                dots.append(terms[0])
            for i, a in enumerate(dots):
                z = gelu_c * (a + 0.044715 * (a * a * a))
                act = a / (1.0 + jnp.exp(-2.0 * z))
                slot = pl.multiple_of(tl * PEER_SLOTS + q * SC_UNIT_ROWS + i * SC_LANES, SC_LANES)
                bits = lax.bitcast_convert_type(gate_v[s, pl.ds(slot, SC_LANES)] * act, jnp.int32)
                top = (bits + 0x7FFF + ((bits >> 16) & 1)) & hi_mask
                coef_v[pl.ds(i * SC_LANES, SC_LANES)] = top | lax.shift_right_logical(top, 16)

            def out_pass(dq, carry):
                def row_quad(rq, accs):
                    per_vec = SC_LANES // SC_BF16_TERMS
                    cvec = coef_v[pl.ds(pl.multiple_of((rq // per_vec) * SC_LANES, SC_LANES), SC_LANES)]
                    cs = []
                    for j in range(SC_BF16_TERMS):
                        src = jnp.full((SC_LANES,), (rq % per_vec) * SC_BF16_TERMS + j, jnp.int32)
                        cs.append(as_pairs(cvec.at[src].get(mode="promise_in_bounds")))
                    new = []
                    for k in range(SC_OUT_VECS):
                        p = None
                        for j in range(SC_BF16_TERMS):
                            w = row_vec(b, rq * SC_BF16_TERMS + j, SC_HALF_VECS + dq * SC_OUT_VECS + k)
                            term = as_pairs(w) * cs[j]
                            p = term if p is None else p + term
                        lo, hi = unpack(plsc.bitcast(p, jnp.int32))
                        new.append(accs[2 * k] + lo)
                        new.append(accs[2 * k + 1] + hi)
                    return tuple(new)

                accs = lax.fori_loop(0, SC_UNIT_ROWS // SC_BF16_TERMS, row_quad, (zero,) * (2 * SC_OUT_VECS))
                for k in range(SC_OUT_VECS):
                    off = pl.multiple_of((dq * SC_OUT_VECS + k) * SC_LANES, SC_LANES)
                    plsc.addupdate(out_v.at[tl, pl.ds(off, SC_LANES)], accs[2 * k])
                    plsc.addupdate(out_v.at[tl, pl.ds(HALF_D + off, SC_LANES)], accs[2 * k + 1])
                return carry

            lax.fori_loop(0, SC_HALF_VECS // SC_OUT_VECS, out_pass, 0)

        n_blocks = tpw // token_block

        def block_inputs(blk, s):
            tok0 = wid * tpw + blk * token_block
            slots = pl.ds(tok0 * PEER_SLOTS, token_block * PEER_SLOTS)
            return (pltpu.make_async_copy(idx_hbm.at[slots], idx_v.at[s], in_sem.at[s]),
                    pltpu.make_async_copy(gate_hbm.at[slots], gate_v.at[s], in_sem.at[s]),
                    pltpu.make_async_copy(h_hbm.at[pl.ds(tok0, token_block)], h_v.at[s], in_sem.at[s]))

        def first_gathers(s):
            for u in range(SC_BUFFERS - 1):
                gather(u, u, s).start()

        for c in block_inputs(0, 0):
            c.start()
        for c in block_inputs(0, 0):
            c.wait()
        first_gathers(0)

        @pl.loop(0, n_blocks)
        def _(blk):
            s = blk % 2
            more = blk + 1 < n_blocks

            @pl.when(more)
            def _():
                for c in block_inputs(blk + 1, 1 - s):
                    c.start()

            @pl.loop(0, token_block)
            def _(tl):
                @pl.loop(0, D_MODEL // SC_LANES)
                def _(k):
                    out_v[tl, pl.ds(pl.multiple_of(k * SC_LANES, SC_LANES), SC_LANES)] = zero

            @pl.loop(0, units)
            def _(unit):
                ahead = unit + SC_BUFFERS - 1

                @pl.when(ahead < units)
                def _():
                    gather(ahead, ahead % SC_BUFFERS, s).start()

                b = unit % SC_BUFFERS
                gather(unit, b, s).wait()
                compute(unit, b, s)

            @pl.when(more)
            def _():
                for c in block_inputs(blk + 1, 1 - s):
                    c.wait()
                first_gathers(1 - s)

            tok0 = wid * tpw + blk * token_block
            pltpu.sync_copy(out_v, out_hbm.at[pl.ds(tok0, token_block)])

    return kern(table, idx, gates, hp)


TC_PEER_TOKENS = 8
TC_PEER_BUFFERS = 4
U_TILE_ROWS = SC_ROW_TILES // 2


def _tc_peer_kernel(idx_ref, idx_next_ref, gate_ref, hp_ref, table_hbm, o_ref, rows_buf, a_buf, c_buf, sem):
    step = pl.program_id(0)
    hi_mask = jnp.int32(-65536)

    def unpack(w):
        return lax.bitcast_convert_type(w << 16, F32), lax.bitcast_convert_type(w & hi_mask, F32)

    def row_copy(ids_ref, tok, e, slot):
        return pltpu.make_async_copy(table_hbm.at[ids_ref[tok * PEER_SLOTS + e]], rows_buf.at[slot, e], sem.at[slot])

    def wait_rows(slot):
        pltpu.make_async_copy(table_hbm.at[pl.ds(0, PEER_SLOTS)], rows_buf.at[slot], sem.at[slot]).wait()

    ahead = TC_PEER_BUFFERS - 1

    @pl.when(step == 0)
    def _():
        for tok in range(ahead):
            def first(e, c, tok=tok):
                row_copy(idx_ref, tok, e, tok).start()
                return c

            lax.fori_loop(0, PEER_SLOTS, first, 0, unroll=8)

    gates_t = gate_ref[...].T
    zeros_v = jnp.zeros((SC_ROW_TILES - U_TILE_ROWS, LANES), F32)
    for t in range(TC_PEER_TOKENS):
        slot = t % TC_PEER_BUFFERS
        nxt = t + ahead
        nxt_slot = nxt % TC_PEER_BUFFERS
        wait_rows(slot)
        h_lo, h_hi = unpack(hp_ref[t])
        h_lo = jnp.concatenate([h_lo, zeros_v], axis=0)
        h_hi = jnp.concatenate([h_hi, zeros_v], axis=0)

        def u_body(e, c, slot=slot, nxt=nxt, nxt_slot=nxt_slot, h_lo=h_lo, h_hi=h_hi):
            if nxt >= TC_PEER_TOKENS:
                @pl.when(step + 1 < pl.num_programs(0))
                def _():
                    row_copy(idx_next_ref, nxt - TC_PEER_TOKENS, e, nxt_slot).start()
            else:
                row_copy(idx_ref, nxt, e, nxt_slot).start()
            lo, hi = unpack(rows_buf[slot, e])
            a_buf[pl.ds(e, 1), :] = jnp.sum(lo * h_lo + hi * h_hi, axis=0, keepdims=True)
            return c

        lax.fori_loop(0, PEER_SLOTS, u_body, 0, unroll=8)
        a = jnp.sum(a_buf[...], axis=1, keepdims=True)
        c_buf[...] = jnp.broadcast_to(gates_t[:, t:t + 1] * _gelu(a), (PEER_SLOTS, LANES))

        def v_body(e, acc, slot=slot):
            lo, hi = unpack(rows_buf[slot, e])
            c = c_buf[pl.ds(e, 1), :]
            return acc[0] + c * lo, acc[1] + c * hi

        zero = jnp.zeros((SC_ROW_TILES, LANES), F32)
        acc_lo, acc_hi = lax.fori_loop(0, PEER_SLOTS, v_body, (zero, zero), unroll=8)
        o_ref[t, 0:U_TILE_ROWS, :] = acc_lo[U_TILE_ROWS:, :]
        o_ref[t, U_TILE_ROWS:, :] = acc_hi[U_TILE_ROWS:, :]


def tc_peer_experts(table, idx, gates, hp):
    t = hp.shape[0]
    assert t % TC_PEER_TOKENS == 0 and TC_PEER_TOKENS % TC_PEER_BUFFERS == 0
    n_steps = t // TC_PEER_TOKENS
    ids = TC_PEER_TOKENS * PEER_SLOTS
    smem = functools.partial(pl.BlockSpec, memory_space=pltpu.SMEM)
    out = pl.pallas_call(
        _tc_peer_kernel,
        grid=(n_steps,),
        in_specs=[
            smem((ids,), lambda i: (i,)),
            smem((ids,), lambda i: (jnp.minimum(i + 1, n_steps - 1),)),
            pl.BlockSpec((TC_PEER_TOKENS, PEER_SLOTS), lambda i: (i, 0)),
            pl.BlockSpec((TC_PEER_TOKENS, U_TILE_ROWS, LANES), lambda i: (i, 0, 0)),
            pl.BlockSpec(memory_space=pl.ANY),
        ],
        out_specs=pl.BlockSpec((TC_PEER_TOKENS, SC_ROW_TILES, LANES), lambda i: (i, 0, 0)),
        out_shape=jax.ShapeDtypeStruct((t, SC_ROW_TILES, LANES), F32),
        scratch_shapes=[
            pltpu.VMEM((TC_PEER_BUFFERS, PEER_SLOTS, SC_ROW_TILES, LANES), jnp.int32),
            pltpu.VMEM((PEER_SLOTS, LANES), F32),
            pltpu.VMEM((PEER_SLOTS, LANES), F32),
            pltpu.SemaphoreType.DMA((TC_PEER_BUFFERS,)),
        ],
        compiler_params=pltpu.CompilerParams(dimension_semantics=("arbitrary",), vmem_limit_bytes=VMEM_LIMIT,
                                             disable_bounds_checks=True),
        name="tc_peer_experts",
    )(idx, idx, gates, hp.reshape(t, U_TILE_ROWS, LANES), table)
    return out.reshape(t, D_MODEL)


TC_PEER_SHARE = 16
TC_PEER_QUANTUM = 256


def peer_block(x, norm_g, wq, subkeys, table, prev_sc=None):
    t = x.shape[0]
    hp, idx3, gate3 = peer_query(x, norm_g, wq, subkeys)
    idx = jnp.transpose(idx3, (0, 2, 1)).reshape(t * PEER_SLOTS)
    gates = jnp.transpose(gate3, (0, 2, 1)).reshape(t * PEER_SLOTS)
    t_tc = (t // TC_PEER_SHARE) // TC_PEER_QUANTUM * TC_PEER_QUANTUM
    t_sc = t - t_tc
    idx_sc = idx[:t_sc * PEER_SLOTS]
    if prev_sc is not None:
        idx_sc, _ = lax.optimization_barrier((idx_sc, prev_sc))
    on_sc = sc_peer_experts(table, idx_sc, gates[:t_sc * PEER_SLOTS], hp[:t_sc])
    if t_tc == 0:
        return on_sc, (hp,), on_sc
    on_tc = tc_peer_experts(table, idx[t_sc * PEER_SLOTS:], gates[t_sc * PEER_SLOTS:].reshape(t_tc, PEER_SLOTS),
                            hp[t_sc:])
    return jnp.concatenate([on_sc, on_tc], axis=0), (hp, on_tc), on_sc


def _ple_kernel(x_ref, ffn_ref, p_ref, g_ref, wg_ref, wp_ref, gf_ref, o_ref, *, final):
    x = x_ref[...] + ffn_ref[...]
    gate = _sigmoid(jnp.dot(_rms(x, g_ref[...]).astype(BF16), wg_ref[...], preferred_element_type=F32))
    emb = jnp.dot(p_ref[...].astype(BF16), wp_ref[...], preferred_element_type=F32)
    y = x + gate * emb
    o_ref[...] = _rms(y, gf_ref[...]) if final else y


def ple_block(x, ffn, p, norm_g, w_gate, w_proj, norm_final, final):
    t = x.shape[0]
    tm = min(t, 1024)
    row = lambda i: (i, 0)
    full = lambda i: (0, 0)
    return pl.pallas_call(
        functools.partial(_ple_kernel, final=final),
        grid=(t // tm,),
        in_specs=[
            pl.BlockSpec((tm, D_MODEL), row),
            pl.BlockSpec((tm, D_MODEL), row),
            pl.BlockSpec((tm, PLE_DIM), row),
            pl.BlockSpec((1, D_MODEL), full),
            pl.BlockSpec((D_MODEL, D_MODEL), full),
            pl.BlockSpec((PLE_DIM, D_MODEL), full),
            pl.BlockSpec((1, D_MODEL), full),
        ],
        out_specs=pl.BlockSpec((tm, D_MODEL), row),
        out_shape=jax.ShapeDtypeStruct((t, D_MODEL), F32),
        compiler_params=_cparams("parallel"),
        name="ple_block",
    )(x, ffn, p, norm_g.reshape(1, D_MODEL), w_gate, w_proj, norm_final.reshape(1, D_MODEL))


def _prompt_groups(batch):
    sizes = []
    while sum(sizes) < batch:
        nxt = 1 if len(sizes) < 2 else -(-sizes[-1] * 7 // 5)
        sizes.append(min(nxt, batch - sum(sizes)))
    return sizes
def _trunk_layer(x, ple, lw, batch, seq, pool_prefix, pool_start, cache, final, after=None, prev_sc=None):
    if after is not None:
        x, _ = lax.optimization_barrier((x, after))
    uv = norm_matmul(x, lw["norm_mix"], lw["w_uv"], "gelu", F32, 2 * BW)
    mid = norm_matmul(x, lw["norm_mix"], lw["w_mid"], "none", F32, 4 * BW)
    gates = norm_matmul(x, lw["norm_mix"], lw["w_gates"], "sigmoid", BF16, D_MODEL)
    ya, vn = gmlp_mix(uv, lw["gmlp_ln_g"], lw["gmlp_ln_b"], lw["gmlp_ws"], lw["gmlp_bs"], min(seq, GMLP_CHUNK))
    yb, pool_state = pool_mix(mid, pool_prefix, pool_start, lw["pool_w"], lw["pool_scale"], batch, seq)
    if cache is None:
        yc = attn_prompt(mid, lw["rel_bias"], batch, seq)
    else:
        yc = attn_sample(mid, cache[0], cache[1], lw["rel_bias"], batch, seq)
    x = mixer_out(x, ya, yb, yc, gates, lw["w_branch"], lw["w_out"])
    ffn, stage, sc_out = peer_block(x, lw["norm_ffn"], lw["peer_wq"], lw["peer_subkeys"], lw["peer_table"], prev_sc)
    x = ple_block(x, ffn, ple, lw["norm_ple"], lw["ple_gate"], lw["ple_proj"], lw["norm_final"], final)
    return x, mid, pool_state, vn, stage, sc_out


def kernel(x_prompt, x_sample, cache_attn_k, cache_attn_v, state_pool, p_prompt, p_sample, norm_mix, w_in, gmlp_ln_g, gmlp_ln_b, gmlp_ws, gmlp_bs, pool_w, pool_scale, attn_rel_bias, w_branch, w_out, norm_ffn, peer_wq, peer_subkeys, peer_u, peer_v, norm_ple, ple_gate, ple_proj, norm_final):
    bp, lp, _ = x_prompt.shape
    bs, ls, _ = x_sample.shape
    assert lp % BAND_PAST == 0 and lp % GMLP_CHUNK == 0 and ls <= CHUNK
    n_keep = min(BAND_PAST, lp)
    sizes = _prompt_groups(bp)
    starts = [sum(sizes[:g]) for g in range(len(sizes))]
    xg = [x_prompt[a:a + n].reshape(n * lp, D_MODEL) for a, n in zip(starts, sizes)]
    xs = x_sample.reshape(bs * ls, D_MODEL)
    outs = {k: [] for k in ("pk", "pv", "pps", "sk", "sv", "sps", "sgv")}
    after = prev_sc = None
    stages = []
    for i in range(DEPTH):
        tab_u, tab_v = peer_u[i], peer_v[i]
        if stages:
            tab_u, tab_v, _ = lax.optimization_barrier((tab_u, tab_v, stages[len(stages) // 2]))
        stages = []
        w_in_b = w_in[i].astype(BF16)
        lw = dict(
            norm_mix=norm_mix[i],
            w_uv=w_in_b[:, :2 * BW],
            w_mid=w_in_b[:, 2 * BW:6 * BW],
            w_gates=w_in_b[:, 6 * BW:],
            gmlp_ln_g=gmlp_ln_g[i], gmlp_ln_b=gmlp_ln_b[i], gmlp_ws=gmlp_ws[i], gmlp_bs=gmlp_bs[i],
            pool_w=pool_w[i], pool_scale=pool_scale[i], rel_bias=attn_rel_bias[i],
            w_branch=w_branch[i].astype(BF16), w_out=w_out[i].astype(BF16),
            norm_ffn=norm_ffn[i], peer_wq=peer_wq[i].astype(BF16),
            peer_subkeys=peer_subkeys[i].reshape(2 * PEER_HEADS, PEER_NKEYS, PEER_HALF).astype(BF16),
            peer_table=pack_expert_tables(tab_u, tab_v),
            norm_ple=norm_ple[i], ple_gate=ple_gate[i].astype(BF16), ple_proj=ple_proj[i].astype(BF16),
            norm_final=norm_final,
        )
        final = i == DEPTH - 1
        pk, pv, pps = [], [], []
        for g, (a, bg) in enumerate(zip(starts, sizes)):
            ple_g = p_prompt[i, a:a + bg].reshape(bg * lp, PLE_DIM)
            zero_prefix = jnp.zeros((bg, POOL_STATE, BW), F32)
            xg[g], mid_p, ps_p, _, after, prev_sc = _trunk_layer(xg[g], ple_g, lw, bg, lp, zero_prefix, 0, None, final,
                                                                 after, prev_sc)
            stages.append(after)
            mid_p = mid_p.reshape(bg, lp, 4 * BW)
            pk.append(mid_p[:, lp - n_keep:, 2 * BW:3 * BW].reshape(bg, n_keep, HEADS, HEAD_DIM))
            pv.append(mid_p[:, lp - n_keep:, 3 * BW:].reshape(bg, n_keep, HEADS, HEAD_DIM))
            pps.append(ps_p)
        outs["pk"].append(jnp.concatenate(pk, axis=0))
        outs["pv"].append(jnp.concatenate(pv, axis=0))
        outs["pps"].append(jnp.concatenate(pps, axis=0))
        xs, mid_s, ps_s, vn_s, _, _ = _trunk_layer(xs, p_sample[i].reshape(bs * ls, PLE_DIM), lw, bs, ls, state_pool[i],
                                                PAST_LEN, (cache_attn_k[i], cache_attn_v[i]), final)
        mid_s = mid_s.reshape(bs, ls, 4 * BW)
        outs["sk"].append(mid_s[:, :, 2 * BW:3 * BW].reshape(bs, ls, HEADS, HEAD_DIM))
        outs["sv"].append(mid_s[:, :, 3 * BW:].reshape(bs, ls, HEADS, HEAD_DIM))
        outs["sps"].append(ps_s)
        outs["sgv"].append(vn_s.reshape(bs, ls, BW))
    st = lambda k: jnp.stack(outs[k])
    y_prompt = jnp.concatenate(xg, axis=0).reshape(bp, lp, D_MODEL)
    return (y_prompt, xs.reshape(bs, ls, D_MODEL), st("pk"), st("pv"), st("pps"),
            st("sk"), st("sv"), st("sps"), st("sgv"))
```

```python
import functools
import math

import jax
import jax.numpy as jnp
import numpy as np
from jax import lax
from jax.experimental import pallas as pl
from jax.experimental.pallas import tpu as pltpu
from jax.experimental.pallas import tpu_sc as plsc

F32 = jnp.float32
BF16 = jnp.bfloat16

D_MODEL = 1024
DEPTH = 2
CHUNK = 64
EPS = 1e-6
BW = D_MODEL // 2
GMLP_CHUNK = 128
GROUPS = 4
GDIM = BW // GROUPS
POOL_WINDOWS = (2, 4, 8, 16)
POOL_STATE = 15
POOL_PAD = 16
HEADS = 8
HEAD_DIM = BW // HEADS
BAND_CHUNKS = 8
BAND_PAST = BAND_CHUNKS * CHUNK
REL_CLIP = 128
PAST_LEN = 4096
PEER_HEADS = 8
PEER_NKEYS = 128
PEER_HALF = 128
PEER_TOPK = 16
PEER_SLOTS = PEER_HEADS * PEER_TOPK
PLE_DIM = 256

LANES = 128
VMEM_LIMIT = 56 * 1024 * 1024
NEG = -1e30


def _cparams(*sem):
    return pltpu.CompilerParams(dimension_semantics=sem, vmem_limit_bytes=VMEM_LIMIT)


def _rms(x, g):
    ms = jnp.mean(x * x, axis=-1, keepdims=True)
    return x * lax.rsqrt(ms + EPS) * g


def _gelu(x):
    c = math.sqrt(2.0 / math.pi)
    return 0.5 * x * (1.0 + jnp.tanh(c * (x + 0.044715 * (x * x * x))))


def _sigmoid(x):
    return 1.0 / (1.0 + jnp.exp(-x))


IN_GELU_COLS = 2 * BW
IN_PLAIN_COLS = 4 * BW
IN_GATE_COLS = 3 * D_MODEL


def _in_proj_kernel(x_ref, g_ref, w_ref, uv_ref, mid_ref, gate_ref, h_ref):
    j = pl.program_id(1)

    @pl.when(j == 0)
    def _():
        h_ref[...] = _rms(x_ref[...], g_ref[...]).astype(BF16)

    z = jnp.dot(h_ref[...], w_ref[...], preferred_element_type=F32)
    n_gelu = IN_GELU_COLS // D_MODEL
    n_plain = IN_PLAIN_COLS // D_MODEL

    @pl.when(j < n_gelu)
    def _():
        uv_ref[...] = _gelu(z)

    @pl.when(jnp.logical_and(j >= n_gelu, j < n_gelu + n_plain))
    def _():
        mid_ref[...] = z

    @pl.when(j >= n_gelu + n_plain)
    def _():
        gate_ref[...] = _sigmoid(z).astype(gate_ref.dtype)


def in_proj(x, g, w):
    t, d = x.shape
    tm = min(t, 1024)
    tn = D_MODEL
    n_gelu, n_plain, n_gate = IN_GELU_COLS // tn, IN_PLAIN_COLS // tn, IN_GATE_COLS // tn
    assert w.shape[1] == IN_GELU_COLS + IN_PLAIN_COLS + IN_GATE_COLS
    return pl.pallas_call(
        _in_proj_kernel,
        grid=(t // tm, n_gelu + n_plain + n_gate),
        in_specs=[
            pl.BlockSpec((tm, d), lambda i, j: (i, 0)),
            pl.BlockSpec((1, d), lambda i, j: (0, 0)),
            pl.BlockSpec((d, tn), lambda i, j: (0, j)),
        ],
        out_specs=[
            pl.BlockSpec((tm, tn), lambda i, j: (i, jnp.clip(j, 0, n_gelu - 1))),
            pl.BlockSpec((tm, tn), lambda i, j: (i, jnp.clip(j - n_gelu, 0, n_plain - 1))),
            pl.BlockSpec((tm, tn), lambda i, j: (i, jnp.clip(j - n_gelu - n_plain, 0, n_gate - 1))),
        ],
        out_shape=[
            jax.ShapeDtypeStruct((t, IN_GELU_COLS), F32),
            jax.ShapeDtypeStruct((t, IN_PLAIN_COLS), F32),
            jax.ShapeDtypeStruct((t, IN_GATE_COLS), BF16),
        ],
        scratch_shapes=[pltpu.VMEM((tm, d), BF16)],
        compiler_params=_cparams("parallel", "arbitrary"),
        name="in_proj",
    )(x, g.reshape(1, d), w)


def _gmlp_kernel(uv_ref, lng_ref, lnb_ref, ws_ref, bst_ref, y_ref, vn_ref, *, lc):
    u = uv_ref[:, :BW]
    v = uv_ref[:, BW:]
    mu = jnp.mean(v, axis=-1, keepdims=True)
    vc = v - mu
    var = jnp.mean(vc * vc, axis=-1, keepdims=True)
    vn = vc * lax.rsqrt(var + EPS) * lng_ref[...] + lnb_ref[...]
    vn_ref[...] = vn
    row = lax.broadcasted_iota(jnp.int32, (lc, lc), 0) // CHUNK
    col = lax.broadcasted_iota(jnp.int32, (lc, lc), 1) // CHUNK
    causal = col <= row
    vnb = vn.astype(BF16)
    for g in range(GROUPS):
        w = jnp.where(causal, ws_ref[g], 0.0).astype(BF16)
        s = jnp.dot(w, vnb[:, g * GDIM:(g + 1) * GDIM], preferred_element_type=F32)
        s = s + bst_ref[:, g:g + 1]
        y_ref[:, g * GDIM:(g + 1) * GDIM] = (u[:, g * GDIM:(g + 1) * GDIM] * s).astype(y_ref.dtype)


def gmlp_mix(uv, ln_g, ln_b, ws, bs, lc):
    t = uv.shape[0]
    return pl.pallas_call(
        functools.partial(_gmlp_kernel, lc=lc),
        grid=(t // lc,),
        in_specs=[
            pl.BlockSpec((lc, 2 * BW), lambda i: (i, 0)),
            pl.BlockSpec((1, BW), lambda i: (0, 0)),
            pl.BlockSpec((1, BW), lambda i: (0, 0)),
            pl.BlockSpec((GROUPS, lc, lc), lambda i: (0, 0, 0)),
            pl.BlockSpec((lc, GROUPS), lambda i: (0, 0)),
        ],
        out_specs=[
            pl.BlockSpec((lc, BW), lambda i: (i, 0)),
            pl.BlockSpec((lc, BW), lambda i: (i, 0)),
        ],
        out_shape=[
            jax.ShapeDtypeStruct((t, BW), BF16),
            jax.ShapeDtypeStruct((t, BW), F32),
        ],
        compiler_params=_cparams("parallel"),
        name="gmlp_mix",
    )(uv, ln_g.reshape(1, BW), ln_b.reshape(1, BW), ws[:, :lc, :lc], bs[:, :lc].T)


def _pool_kernel(u_ref, pre_ref, w_ref, sc_ref, y_ref, st_ref, pad_ref, *, seq, start_pos):
    pad_ref[0:POOL_PAD, :] = pre_ref[...]
    pad_ref[POOL_PAD:, :] = u_ref[...]
    pos = lax.broadcasted_iota(jnp.int32, (seq, 1), 0) + start_pos
    for g, win in enumerate(POOL_WINDOWS):
        cols = slice(g * GDIM, (g + 1) * GDIM)
        tok = pad_ref[POOL_PAD:, cols]
        acc = tok
        for k in range(1, win):
            acc = acc + pad_ref[POOL_PAD - k:POOL_PAD - k + seq, cols]
        cnt = jnp.minimum(pos + 1, win).astype(F32)
        d = acc / cnt - tok
        y = jnp.dot(d.astype(BF16), w_ref[g], preferred_element_type=F32)
        y_ref[:, cols] = (y * sc_ref[:, cols]).astype(y_ref.dtype)
    st_ref[...] = pad_ref[seq + 1:seq + POOL_PAD, :]


def pool_mix(mid, prefix, start_pos, pool_w, pool_scale, batch, seq):
    mid3 = mid.reshape(batch, seq, 4 * BW)
    pre = jnp.concatenate([jnp.zeros((batch, 1, BW), F32), prefix], axis=1)
    y, st = pl.pallas_call(
        functools.partial(_pool_kernel, seq=seq, start_pos=start_pos),
        grid=(batch,),
        in_specs=[
            pl.BlockSpec((None, seq, BW), lambda b: (b, 0, 0)),
            pl.BlockSpec((None, POOL_PAD, BW), lambda b: (b, 0, 0)),
            pl.BlockSpec((GROUPS, GDIM, GDIM), lambda b: (0, 0, 0)),
            pl.BlockSpec((1, BW), lambda b: (0, 0)),
        ],
        out_specs=[
            pl.BlockSpec((None, seq, BW), lambda b: (b, 0, 0)),
            pl.BlockSpec((None, POOL_STATE, BW), lambda b: (b, 0, 0)),
        ],
        out_shape=[
            jax.ShapeDtypeStruct((batch, seq, BW), BF16),
            jax.ShapeDtypeStruct((batch, POOL_STATE, BW), F32),
        ],
        scratch_shapes=[pltpu.VMEM((seq + POOL_PAD, BW), F32)],
        compiler_params=_cparams("parallel"),
        name="pool_mix",
    )(mid3, pre, pool_w.astype(BF16), pool_scale.reshape(1, BW))
    return y.reshape(batch * seq, BW), st


def _attn_chunks(q_ref, kcat_ref, vcat_ref, bias_ref, o_ref, *, n_chunks, cq, band, first_block):
    scale = HEAD_DIM ** -0.5
    heads_per_tile = LANES // HEAD_DIM
    lane_head = lax.broadcasted_iota(jnp.int32, (1, LANES), 1) // HEAD_DIM
    for tile in range(HEADS // heads_per_tile):
        cols = slice(tile * LANES, (tile + 1) * LANES)
        for i in range(n_chunks):
            rows = slice(i * cq, (i + 1) * cq)
            q = (q_ref[rows, cols] * scale).astype(BF16)
            k = kcat_ref[i * cq:i * cq + band, cols]
            v = vcat_ref[i * cq:i * cq + band, cols]
            out = None
            for j in range(heads_per_tile):
                own = lane_head == j
                s = lax.dot_general(jnp.where(own, q, jnp.zeros_like(q)), k, (((1,), (1,)), ((), ())),
                                    preferred_element_type=F32) + bias_ref[tile * heads_per_tile + j]
                if first_block is not None:
                    key = lax.broadcasted_iota(jnp.int32, (1, band), 1)
                    s = jnp.where(key >= first_block * (BAND_PAST - i * cq), s, NEG)
                m = jnp.max(s, axis=-1, keepdims=True)
                p = jnp.exp(s - m)
                l = jnp.sum(p, axis=-1, keepdims=True)
                o = jnp.dot(p.astype(BF16), v, preferred_element_type=F32) / l
                out = o if out is None else jnp.where(own, o, out)
            o_ref[rows, cols] = out.astype(o_ref.dtype)


def _attn_prompt_kernel(q_ref, kp_ref, ko_ref, vp_ref, vo_ref, bias_ref, o_ref, kcat_ref, vcat_ref):
    kcat_ref[0:BAND_PAST, :] = kp_ref[...].astype(BF16)
    kcat_ref[BAND_PAST:, :] = ko_ref[...].astype(BF16)
    vcat_ref[0:BAND_PAST, :] = vp_ref[...].astype(BF16)
    vcat_ref[BAND_PAST:, :] = vo_ref[...].astype(BF16)
    _attn_chunks(q_ref, kcat_ref, vcat_ref, bias_ref, o_ref, n_chunks=BAND_CHUNKS, cq=CHUNK,
                 band=BAND_PAST + CHUNK, first_block=(pl.program_id(1) == 0).astype(jnp.int32))


def _rel_bias_tile(rel_bias, qpos, kpos):
    lq, lk = len(qpos), len(kpos)
    assert (np.diff(qpos) == 1).all() and (np.diff(kpos) == 1).all()
    lr = lq + lk - 1
    dist = (qpos[0] - kpos[0]) - (np.arange(lr) - (lq - 1))
    diag = rel_bias[:, np.clip(dist, -REL_CLIP, REL_CLIP) + REL_CLIP]
    skewed = jnp.tile(diag, (1, lq))[:, lq - 1:lq - 1 + lq * (lr - 1)].reshape(-1, lq, lr - 1)
    return skewed[:, :, :lk]


def attn_prompt(mid, rel_bias, batch, seq):
    mid3 = mid.reshape(batch, seq, 4 * BW)
    blk = BAND_PAST
    bias = _rel_bias_tile(rel_bias, np.arange(CHUNK), np.arange(BAND_PAST + CHUNK) - BAND_PAST)
    prev = lambda b, j: jnp.maximum(j - 1, 0)
    y = pl.pallas_call(
        _attn_prompt_kernel,
        grid=(batch, seq // blk),
        in_specs=[
            pl.BlockSpec((None, blk, BW), lambda b, j: (b, j, 1)),
            pl.BlockSpec((None, blk, BW), lambda b, j: (b, prev(b, j), 2)),
            pl.BlockSpec((None, blk, BW), lambda b, j: (b, j, 2)),
            pl.BlockSpec((None, blk, BW), lambda b, j: (b, prev(b, j), 3)),
            pl.BlockSpec((None, blk, BW), lambda b, j: (b, j, 3)),
            pl.BlockSpec((HEADS, CHUNK, BAND_PAST + CHUNK), lambda b, j: (0, 0, 0)),
        ],
        out_specs=pl.BlockSpec((None, blk, BW), lambda b, j: (b, j, 0)),
        out_shape=jax.ShapeDtypeStruct((batch, seq, BW), BF16),
        scratch_shapes=[pltpu.VMEM((2 * blk, BW), BF16), pltpu.VMEM((2 * blk, BW), BF16)],
        compiler_params=_cparams("parallel", "parallel"),
        name="attn_prompt",
    )(mid3, mid3, mid3, mid3, mid3, bias)
    return y.reshape(batch * seq, BW)


def _attn_sample_kernel(q_ref, kc_ref, kn_ref, vc_ref, vn_ref, bias_ref, o_ref, kcat_ref, vcat_ref, *, n_cache):
    kcat_ref[0:n_cache, :] = kc_ref[...].astype(BF16)
    kcat_ref[n_cache:, :] = kn_ref[...].astype(BF16)
    vcat_ref[0:n_cache, :] = vc_ref[...].astype(BF16)
    vcat_ref[n_cache:, :] = vn_ref[...].astype(BF16)
    seq = q_ref.shape[0]
    _attn_chunks(q_ref, kcat_ref, vcat_ref, bias_ref, o_ref, n_chunks=1, cq=seq, band=n_cache + seq,
                 first_block=None)


def attn_sample(mid, cache_k, cache_v, rel_bias, batch, seq):
    n_cache = cache_k.shape[1]
    assert PAST_LEN >= n_cache
    mid3 = mid.reshape(batch, seq, 4 * BW)
    ck = cache_k.reshape(batch, n_cache, BW)
    cv = cache_v.reshape(batch, n_cache, BW)
    bias = _rel_bias_tile(rel_bias, PAST_LEN + np.arange(seq), PAST_LEN - n_cache + np.arange(n_cache + seq))
    y = pl.pallas_call(
        functools.partial(_attn_sample_kernel, n_cache=n_cache),
        grid=(batch,),
        in_specs=[
            pl.BlockSpec((None, seq, BW), lambda b: (b, 0, 1)),
            pl.BlockSpec((None, n_cache, BW), lambda b: (b, 0, 0)),
            pl.BlockSpec((None, seq, BW), lambda b: (b, 0, 2)),
            pl.BlockSpec((None, n_cache, BW), lambda b: (b, 0, 0)),
            pl.BlockSpec((None, seq, BW), lambda b: (b, 0, 3)),
            pl.BlockSpec((HEADS, seq, n_cache + seq), lambda b: (0, 0, 0)),
        ],
        out_specs=pl.BlockSpec((None, seq, BW), lambda b: (b, 0, 0)),
        out_shape=jax.ShapeDtypeStruct((batch, seq, BW), BF16),
        scratch_shapes=[pltpu.VMEM((n_cache + seq, BW), BF16), pltpu.VMEM((n_cache + seq, BW), BF16)],
        compiler_params=_cparams("parallel"),
        name="attn_sample",
    )(mid3, ck, mid3, cv, mid3, bias)
    return y.reshape(batch * seq, BW)


def _mixer_out_kernel(x_ref, ya_ref, yb_ref, yc_ref, gate_ref, wb_ref, wo_ref, o_ref):
    acc = None
    for n, y_ref in enumerate((ya_ref, yb_ref, yc_ref)):
        proj = jnp.dot(y_ref[...], wb_ref[n], preferred_element_type=F32)
        term = gate_ref[:, n * D_MODEL:(n + 1) * D_MODEL].astype(F32) * proj
        acc = term if acc is None else acc + term
    o_ref[...] = x_ref[...] + jnp.dot(acc.astype(BF16), wo_ref[...], preferred_element_type=F32)


def mixer_out(x, ya, yb, yc, gates, w_branch, w_out):
    t = x.shape[0]
    tm = min(t, 512)
    row = lambda i: (i, 0)
    return pl.pallas_call(
        _mixer_out_kernel,
        grid=(t // tm,),
        in_specs=[
            pl.BlockSpec((tm, D_MODEL), row),
            pl.BlockSpec((tm, BW), row),
            pl.BlockSpec((tm, BW), row),
            pl.BlockSpec((tm, BW), row),
            pl.BlockSpec((tm, 3 * D_MODEL), row),
            pl.BlockSpec((3, BW, D_MODEL), lambda i: (0, 0, 0)),
            pl.BlockSpec((D_MODEL, D_MODEL), lambda i: (0, 0)),
        ],
        out_specs=pl.BlockSpec((tm, D_MODEL), row),
        out_shape=jax.ShapeDtypeStruct((t, D_MODEL), F32),
        compiler_params=_cparams("parallel"),
        name="mixer_out",
    )(x, ya, yb, yc, gates, w_branch, w_out)


def _extract_top(s, payload, k):
    r = float(s.shape[0])
    rows = lax.broadcasted_iota(jnp.int32, s.shape, 0).astype(F32)
    vals, pays = [], []
    for _ in range(k):
        m = jnp.max(s, axis=0, keepdims=True)
        idx = jnp.min(jnp.where(s == m, rows, r), axis=0, keepdims=True)
        sel = rows == idx
        vals.append(m)
        pays.append(idx if payload is None else jnp.max(jnp.where(sel, payload, -1.0), axis=0, keepdims=True))
        s = jnp.where(sel, -jnp.inf, s)
    return jnp.concatenate(vals, axis=0), jnp.concatenate(pays, axis=0)


def _pair_candidates(sv, si):
    k = PEER_TOPK
    sub = 8
    assert k == 2 * sub
    b_row = lax.broadcasted_iota(jnp.int32, (sub, LANES), 0)
    vals = [sv[0][0:1] + sv[1], sv[0][1:2] + sv[1][0:sub]]
    ids = [si[0][0:1] * PEER_NKEYS + si[1], si[0][1:2] * PEER_NKEYS + si[1][0:sub]]
    for a in range(2, sub):
        keep = b_row < k // (a + 1)
        vals.append(jnp.where(keep, sv[0][a:a + 1] + sv[1][0:sub], -jnp.inf))
        ids.append(si[0][a:a + 1] * PEER_NKEYS + si[1][0:sub])
    vals.append(sv[0][sub:k] + sv[1][0:1])
    ids.append(si[0][sub:k] * PEER_NKEYS + si[1][0:1])
    return jnp.concatenate(vals, axis=0), jnp.concatenate(ids, axis=0)


def _peer_query_kernel(x_ref, g_ref, wq_ref, sk_ref, hp_ref, idx_ref, gate_ref, q_ref, *, tm):
    hb = _rms(x_ref[...], g_ref[...]).astype(BF16)
    q_ref[...] = jnp.dot(hb, wq_ref[...], preferred_element_type=F32).astype(BF16)
    bits = lax.bitcast_convert_type(hb.astype(F32), jnp.int32)
    half = D_MODEL // 2
    hp_ref[...] = bits[:, half:] | lax.shift_right_logical(bits[:, :half], 16)

    def sub_block(sb, carry):
        tok = pl.ds(pl.multiple_of(sb * LANES, LANES), LANES)
        for hd in range(PEER_HEADS):
            sv, si = [], []
            for p in range(2):
                hp = hd * 2 + p
                q = q_ref[tok, hp * PEER_HALF:(hp + 1) * PEER_HALF]
                s = lax.dot_general(sk_ref[hp], q, (((1,), (1,)), ((), ())), preferred_element_type=F32)
                v, i = _extract_top(s, None, PEER_TOPK)
                sv.append(v)
                si.append(i)
            cand, eid = _pair_candidates(sv, si)
            tv, te = _extract_top(cand, eid, PEER_TOPK)
            e = jnp.exp(tv - tv[0:1])
            gate = e / jnp.sum(e, axis=0, keepdims=True)
            idx_ref[sb, hd * PEER_TOPK:(hd + 1) * PEER_TOPK, :] = te.astype(jnp.int32)
            gate_ref[sb, hd * PEER_TOPK:(hd + 1) * PEER_TOPK, :] = gate
        return carry

    lax.fori_loop(0, tm // LANES, sub_block, 0)


def peer_query(x, norm_g, wq, subkeys):
    t = x.shape[0]
    tm = min(t, 512)
    nq = wq.shape[1]
    nsb = tm // LANES
    return pl.pallas_call(
        functools.partial(_peer_query_kernel, tm=tm),
        grid=(t // tm,),
        in_specs=[
            pl.BlockSpec((tm, D_MODEL), lambda i: (i, 0)),
            pl.BlockSpec((1, D_MODEL), lambda i: (0, 0)),
            pl.BlockSpec((D_MODEL, nq), lambda i: (0, 0)),
            pl.BlockSpec((2 * PEER_HEADS, PEER_NKEYS, PEER_HALF), lambda i: (0, 0, 0)),
        ],
        out_specs=[
            pl.BlockSpec((tm, D_MODEL // 2), lambda i: (i, 0)),
            pl.BlockSpec((nsb, PEER_SLOTS, LANES), lambda i: (i, 0, 0)),
            pl.BlockSpec((nsb, PEER_SLOTS, LANES), lambda i: (i, 0, 0)),
        ],
        out_shape=[
            jax.ShapeDtypeStruct((t, D_MODEL // 2), jnp.int32),
            jax.ShapeDtypeStruct((t // LANES, PEER_SLOTS, LANES), jnp.int32),
            jax.ShapeDtypeStruct((t // LANES, PEER_SLOTS, LANES), F32),
        ],
        scratch_shapes=[pltpu.VMEM((tm, nq), BF16)],
        compiler_params=_cparams("parallel"),
        name="peer_query",
    )(x, norm_g.reshape(1, D_MODEL), wq, subkeys)


HALF_D = D_MODEL // 2
SC_LANES = 16
SC_UNIT_ROWS = 32
SC_UNITS_PER_TOKEN = PEER_SLOTS // SC_UNIT_ROWS
SC_MAX_TOKEN_BLOCK = 8
SC_BUFFERS = 3
SC_ROW_GROUP = 8
SC_HALF_VECS = HALF_D // SC_LANES
SC_OUT_VECS = 8
SC_BF16_TERMS = 4
SC_ROW_TILES = D_MODEL // LANES


PACK_ROWS = 256


def _pack_kernel(u_ref, v_ref, o_ref):
    def words(x):
        bits = lax.bitcast_convert_type(x.astype(BF16).astype(F32), jnp.int32)
        return bits[:, HALF_D:] | lax.shift_right_logical(bits[:, :HALF_D], 16)

    half_tiles = SC_ROW_TILES // 2
    for base, w in ((0, words(u_ref[...])), (half_tiles, words(v_ref[...]))):
        for j in range(half_tiles):
            o_ref[:, base + j, :] = w[:, j * LANES:(j + 1) * LANES]


def pack_expert_tables(peer_u, peer_v):
    e = peer_u.shape[0]
    return pl.pallas_call(
        _pack_kernel,
        grid=(e // PACK_ROWS,),
        in_specs=[pl.BlockSpec((PACK_ROWS, D_MODEL), lambda i: (i, 0))] * 2,
        out_specs=pl.BlockSpec((PACK_ROWS, SC_ROW_TILES, LANES), lambda i: (i, 0, 0)),
        out_shape=jax.ShapeDtypeStruct((e, SC_ROW_TILES, LANES), jnp.int32),
        compiler_params=_cparams("parallel"),
        name="pack_expert_tables",
    )(peer_u, peer_v)


def sc_peer_experts(table, idx, gates, hp):
    t = hp.shape[0]
    info = plsc.get_sparse_core_info()
    n_workers = info.num_cores * info.num_subcores
    tpw = t // n_workers
    token_block = min(tpw, SC_MAX_TOKEN_BLOCK)
    assert t % (n_workers * token_block) == 0
    units = token_block * SC_UNITS_PER_TOKEN
    mesh = plsc.VectorSubcoreMesh(core_axis_name="core", subcore_axis_name="subcore")
    hi_mask = jnp.int32(-65536)
    gelu_c = math.sqrt(2.0 / math.pi)

    @functools.partial(
        pl.kernel,
        out_type=jax.ShapeDtypeStruct((t, D_MODEL), F32),
        mesh=mesh,
        scratch_types=[
            pltpu.VMEM((2, token_block * PEER_SLOTS), jnp.int32),
            pltpu.VMEM((2, token_block * PEER_SLOTS), F32),
            pltpu.VMEM((2, token_block, HALF_D), jnp.int32),
            pltpu.VMEM((token_block, D_MODEL), F32),
            pltpu.VMEM((SC_BUFFERS, SC_UNIT_ROWS, SC_ROW_TILES, LANES), jnp.int32),
            pltpu.VMEM((SC_UNIT_ROWS, SC_LANES), F32),
            pltpu.VMEM((SC_UNIT_ROWS,), jnp.int32),
            pltpu.SemaphoreType.DMA((SC_BUFFERS,)),
            pltpu.SemaphoreType.DMA((2,)),
        ],
        compiler_params=pltpu.CompilerParams(needs_layout_passes=False),
        name="peer_sc_experts",
    )
    def kern(table_hbm, idx_hbm, gate_hbm, h_hbm, out_hbm, idx_v, gate_v, h_v, out_v, rows_v, part_v, coef_v, sem,
             in_sem):
        wid = lax.axis_index("subcore") * info.num_cores + lax.axis_index("core")
        lane = lax.iota(jnp.int32, SC_LANES)
        zero = jnp.zeros((SC_LANES,), F32)

        def gather(unit, b, s):
            rows = idx_v.at[s, pl.ds(unit * SC_UNIT_ROWS, SC_UNIT_ROWS)]
            return pltpu.make_async_copy(table_hbm.at[rows], rows_v.at[b], sem.at[b])

        def row_vec(b, r, vec):
            per_tile_row = LANES // SC_LANES
            lane0 = pl.multiple_of((vec % per_tile_row) * SC_LANES, SC_LANES)
            return rows_v[b, r, vec // per_tile_row, pl.ds(lane0, SC_LANES)]

        def unpack(w):
            return lax.bitcast_convert_type(w << 16, F32), lax.bitcast_convert_type(w & hi_mask, F32)

        def as_pairs(w):
            return plsc.bitcast(w, BF16)

        def compute(unit, b, s):
            tl = unit // SC_UNITS_PER_TOKEN
            q = unit % SC_UNITS_PER_TOKEN

            def row_group(rg, carry):
                def kstep(k, accs):
                    hs = []
                    for j in range(SC_BF16_TERMS):
                        off = pl.multiple_of((k * SC_BF16_TERMS + j) * SC_LANES, SC_LANES)
                        hs.append(as_pairs(h_v[s, tl, pl.ds(off, SC_LANES)]))
                    new = []
                    for r in range(SC_ROW_GROUP):
                        p = None
                        for j in range(SC_BF16_TERMS):
                            term = as_pairs(row_vec(b, rg * SC_ROW_GROUP + r, k * SC_BF16_TERMS + j)) * hs[j]
                            p = term if p is None else p + term
                        lo, hi = unpack(plsc.bitcast(p, jnp.int32))
                        new.append(accs[r] + lo + hi)
                    return tuple(new)

                accs = lax.fori_loop(0, SC_HALF_VECS // SC_BF16_TERMS, kstep, (zero,) * SC_ROW_GROUP)
                for r in range(SC_ROW_GROUP):
                    part_v[rg * SC_ROW_GROUP + r, :] = accs[r]
                return carry

            lax.fori_loop(0, SC_UNIT_ROWS // SC_ROW_GROUP, row_group, 0)

            dots = []
            for i in range(SC_UNIT_ROWS // SC_LANES):
                rows = lane + i * SC_LANES
                terms = [plsc.load_gather(part_v, [rows, jnp.full((SC_LANES,), l, jnp.int32)])
                         for l in range(SC_LANES)]
                while len(terms) > 1:
                    terms = [terms[j] + terms[j + 1] for j in range(0, len(terms), 2)]
                dots.append(terms[0])
            for i, a in enumerate(dots):
                z = gelu_c * (a + 0.044715 * (a * a * a))
                act = a / (1.0 + jnp.exp(-2.0 * z))
                slot = pl.multiple_of(tl * PEER_SLOTS + q * SC_UNIT_ROWS + i * SC_LANES, SC_LANES)
                bits = lax.bitcast_convert_type(gate_v[s, pl.ds(slot, SC_LANES)] * act, jnp.int32)
                top = (bits + 0x7FFF + ((bits >> 16) & 1)) & hi_mask
                coef_v[pl.ds(i * SC_LANES, SC_LANES)] = top | lax.shift_right_logical(top, 16)

            def out_pass(dq, carry):
                def row_quad(rq, accs):
                    per_vec = SC_LANES // SC_BF16_TERMS
                    cvec = coef_v[pl.ds(pl.multiple_of((rq // per_vec) * SC_LANES, SC_LANES), SC_LANES)]
                    cs = []
                    for j in range(SC_BF16_TERMS):
                        src = jnp.full((SC_LANES,), (rq % per_vec) * SC_BF16_TERMS + j, jnp.int32)
                        cs.append(as_pairs(cvec.at[src].get(mode="promise_in_bounds")))
                    new = []
                    for k in range(SC_OUT_VECS):
                        p = None
                        for j in range(SC_BF16_TERMS):
                            w = row_vec(b, rq * SC_BF16_TERMS + j, SC_HALF_VECS + dq * SC_OUT_VECS + k)
                            term = as_pairs(w) * cs[j]
                            p = term if p is None else p + term
                        lo, hi = unpack(plsc.bitcast(p, jnp.int32))
                        new.append(accs[2 * k] + lo)
                        new.append(accs[2 * k + 1] + hi)
                    return tuple(new)

                accs = lax.fori_loop(0, SC_UNIT_ROWS // SC_BF16_TERMS, row_quad, (zero,) * (2 * SC_OUT_VECS))
                for k in range(SC_OUT_VECS):
                    off = pl.multiple_of((dq * SC_OUT_VECS + k) * SC_LANES, SC_LANES)
                    plsc.addupdate(out_v.at[tl, pl.ds(off, SC_LANES)], accs[2 * k])
                    plsc.addupdate(out_v.at[tl, pl.ds(HALF_D + off, SC_LANES)], accs[2 * k + 1])
                return carry

            lax.fori_loop(0, SC_HALF_VECS // SC_OUT_VECS, out_pass, 0)

        n_blocks = tpw // token_block

        def block_inputs(blk, s):
            tok0 = wid * tpw + blk * token_block
            slots = pl.ds(tok0 * PEER_SLOTS, token_block * PEER_SLOTS)
            return (pltpu.make_async_copy(idx_hbm.at[slots], idx_v.at[s], in_sem.at[s]),
                    pltpu.make_async_copy(gate_hbm.at[slots], gate_v.at[s], in_sem.at[s]),
                    pltpu.make_async_copy(h_hbm.at[pl.ds(tok0, token_block)], h_v.at[s], in_sem.at[s]))

        def first_gathers(s):
            for u in range(SC_BUFFERS - 1):
                gather(u, u, s).start()

        for c in block_inputs(0, 0):
            c.start()
        for c in block_inputs(0, 0):
            c.wait()
        first_gathers(0)

        @pl.loop(0, n_blocks)
        def _(blk):
            s = blk % 2
            more = blk + 1 < n_blocks

            @pl.when(more)
            def _():
                for c in block_inputs(blk + 1, 1 - s):
                    c.start()

            @pl.loop(0, token_block)
            def _(tl):
                @pl.loop(0, D_MODEL // SC_LANES)
                def _(k):
                    out_v[tl, pl.ds(pl.multiple_of(k * SC_LANES, SC_LANES), SC_LANES)] = zero

            @pl.loop(0, units)
            def _(unit):
                ahead = unit + SC_BUFFERS - 1

                @pl.when(ahead < units)
                def _():
                    gather(ahead, ahead % SC_BUFFERS, s).start()

                b = unit % SC_BUFFERS
                gather(unit, b, s).wait()
                compute(unit, b, s)

            @pl.when(more)
            def _():
                for c in block_inputs(blk + 1, 1 - s):
                    c.wait()
                first_gathers(1 - s)

            tok0 = wid * tpw + blk * token_block
            pltpu.sync_copy(out_v, out_hbm.at[pl.ds(tok0, token_block)])

    return kern(table, idx, gates, hp)


TC_PEER_TOKENS = 8
TC_PEER_BUFFERS = 4
U_TILE_ROWS = SC_ROW_TILES // 2


def _tc_peer_kernel(idx_ref, idx_next_ref, gate_ref, hp_ref, table_hbm, o_ref, rows_buf, a_buf, c_buf, sem):
    step = pl.program_id(0)
    hi_mask = jnp.int32(-65536)

    def unpack(w):
        return lax.bitcast_convert_type(w << 16, F32), lax.bitcast_convert_type(w & hi_mask, F32)

    def row_copy(ids_ref, tok, e, slot):
        return pltpu.make_async_copy(table_hbm.at[ids_ref[tok * PEER_SLOTS + e]], rows_buf.at[slot, e], sem.at[slot])

    def wait_rows(slot):
        pltpu.make_async_copy(table_hbm.at[pl.ds(0, PEER_SLOTS)], rows_buf.at[slot], sem.at[slot]).wait()

    ahead = TC_PEER_BUFFERS - 1

    @pl.when(step == 0)
    def _():
        for tok in range(ahead):
            def first(e, c, tok=tok):
                row_copy(idx_ref, tok, e, tok).start()
                return c

            lax.fori_loop(0, PEER_SLOTS, first, 0, unroll=8)

    gates_t = gate_ref[...].T
    zeros_v = jnp.zeros((SC_ROW_TILES - U_TILE_ROWS, LANES), F32)
    for t in range(TC_PEER_TOKENS):
        slot = t % TC_PEER_BUFFERS
        nxt = t + ahead
        nxt_slot = nxt % TC_PEER_BUFFERS
        wait_rows(slot)
        h_lo, h_hi = unpack(hp_ref[t])
        h_lo = jnp.concatenate([h_lo, zeros_v], axis=0)
        h_hi = jnp.concatenate([h_hi, zeros_v], axis=0)

        def u_body(e, c, slot=slot, nxt=nxt, nxt_slot=nxt_slot, h_lo=h_lo, h_hi=h_hi):
            if nxt >= TC_PEER_TOKENS:
                @pl.when(step + 1 < pl.num_programs(0))
                def _():
                    row_copy(idx_next_ref, nxt - TC_PEER_TOKENS, e, nxt_slot).start()
            else:
                row_copy(idx_ref, nxt, e, nxt_slot).start()
            lo, hi = unpack(rows_buf[slot, e])
            a_buf[pl.ds(e, 1), :] = jnp.sum(lo * h_lo + hi * h_hi, axis=0, keepdims=True)
            return c

        lax.fori_loop(0, PEER_SLOTS, u_body, 0, unroll=8)
        a = jnp.sum(a_buf[...], axis=1, keepdims=True)
        c_buf[...] = jnp.broadcast_to(gates_t[:, t:t + 1] * _gelu(a), (PEER_SLOTS, LANES))

        def v_body(e, acc, slot=slot):
            lo, hi = unpack(rows_buf[slot, e])
            c = c_buf[pl.ds(e, 1), :]
            return acc[0] + c * lo, acc[1] + c * hi

        zero = jnp.zeros((SC_ROW_TILES, LANES), F32)
        acc_lo, acc_hi = lax.fori_loop(0, PEER_SLOTS, v_body, (zero, zero), unroll=8)
        o_ref[t, 0:U_TILE_ROWS, :] = acc_lo[U_TILE_ROWS:, :]
        o_ref[t, U_TILE_ROWS:, :] = acc_hi[U_TILE_ROWS:, :]


def tc_peer_experts(table, idx, gates, hp):
    t = hp.shape[0]
    assert t % TC_PEER_TOKENS == 0 and TC_PEER_TOKENS % TC_PEER_BUFFERS == 0
    n_steps = t // TC_PEER_TOKENS
    ids = TC_PEER_TOKENS * PEER_SLOTS
    smem = functools.partial(pl.BlockSpec, memory_space=pltpu.SMEM)
    out = pl.pallas_call(
        _tc_peer_kernel,
        grid=(n_steps,),
        in_specs=[
            smem((ids,), lambda i: (i,)),
            smem((ids,), lambda i: (jnp.minimum(i + 1, n_steps - 1),)),
            pl.BlockSpec((TC_PEER_TOKENS, PEER_SLOTS), lambda i: (i, 0)),
            pl.BlockSpec((TC_PEER_TOKENS, U_TILE_ROWS, LANES), lambda i: (i, 0, 0)),
            pl.BlockSpec(memory_space=pl.ANY),
        ],
        out_specs=pl.BlockSpec((TC_PEER_TOKENS, SC_ROW_TILES, LANES), lambda i: (i, 0, 0)),
        out_shape=jax.ShapeDtypeStruct((t, SC_ROW_TILES, LANES), F32),
        scratch_shapes=[
            pltpu.VMEM((TC_PEER_BUFFERS, PEER_SLOTS, SC_ROW_TILES, LANES), jnp.int32),
            pltpu.VMEM((PEER_SLOTS, LANES), F32),
            pltpu.VMEM((PEER_SLOTS, LANES), F32),
            pltpu.SemaphoreType.DMA((TC_PEER_BUFFERS,)),
        ],
        compiler_params=pltpu.CompilerParams(dimension_semantics=("arbitrary",), vmem_limit_bytes=VMEM_LIMIT,
                                             disable_bounds_checks=True),
        name="tc_peer_experts",
    )(idx, idx, gates, hp.reshape(t, U_TILE_ROWS, LANES), table)
    return out.reshape(t, D_MODEL)


TC_PEER_SHARE = 8
TC_PEER_QUANTUM = 1024


def peer_block(x, norm_g, wq, subkeys, table, prev_sc=None):
    t = x.shape[0]
    hp, idx3, gate3 = peer_query(x, norm_g, wq, subkeys)
    idx = jnp.transpose(idx3, (0, 2, 1)).reshape(t * PEER_SLOTS)
    gates = jnp.transpose(gate3, (0, 2, 1)).reshape(t * PEER_SLOTS)
    t_tc = (t // TC_PEER_SHARE) // TC_PEER_QUANTUM * TC_PEER_QUANTUM
    t_sc = t - t_tc
    idx_sc = idx[:t_sc * PEER_SLOTS]
    if prev_sc is not None:
        idx_sc, _ = lax.optimization_barrier((idx_sc, prev_sc))
    on_sc = sc_peer_experts(table, idx_sc, gates[:t_sc * PEER_SLOTS], hp[:t_sc])
    if t_tc == 0:
        return on_sc, (hp,), on_sc
    on_tc = tc_peer_experts(table, idx[t_sc * PEER_SLOTS:], gates[t_sc * PEER_SLOTS:].reshape(t_tc, PEER_SLOTS),
                            hp[t_sc:])
    return jnp.concatenate([on_sc, on_tc], axis=0), (hp, on_tc), on_sc


def _ple_kernel(x_ref, ffn_ref, p_ref, g_ref, wg_ref, wp_ref, gf_ref, o_ref, *, final):
    x = x_ref[...] + ffn_ref[...]
    gate = _sigmoid(jnp.dot(_rms(x, g_ref[...]).astype(BF16), wg_ref[...], preferred_element_type=F32))
    emb = jnp.dot(p_ref[...].astype(BF16), wp_ref[...], preferred_element_type=F32)
    y = x + gate * emb
    o_ref[...] = _rms(y, gf_ref[...]) if final else y


def ple_block(x, ffn, p, norm_g, w_gate, w_proj, norm_final, final):
    t = x.shape[0]
    tm = min(t, 1024)
    row = lambda i: (i, 0)
    full = lambda i: (0, 0)
    return pl.pallas_call(
        functools.partial(_ple_kernel, final=final),
        grid=(t // tm,),
        in_specs=[
            pl.BlockSpec((tm, D_MODEL), row),
            pl.BlockSpec((tm, D_MODEL), row),
            pl.BlockSpec((tm, PLE_DIM), row),
            pl.BlockSpec((1, D_MODEL), full),
            pl.BlockSpec((D_MODEL, D_MODEL), full),
            pl.BlockSpec((PLE_DIM, D_MODEL), full),
            pl.BlockSpec((1, D_MODEL), full),
        ],
        out_specs=pl.BlockSpec((tm, D_MODEL), row),
        out_shape=jax.ShapeDtypeStruct((t, D_MODEL), F32),
        compiler_params=_cparams("parallel"),
        name="ple_block",
    )(x, ffn, p, norm_g.reshape(1, D_MODEL), w_gate, w_proj, norm_final.reshape(1, D_MODEL))


def _prompt_groups(batch):
    sizes = []
    while sum(sizes) < batch:
        nxt = 1 if len(sizes) < 2 else -(-sizes[-1] * 7 // 5)
        sizes.append(min(nxt, batch - sum(sizes)))
    return sizes
def _trunk_layer(x, ple, lw, batch, seq, pool_prefix, pool_start, cache, final, after=None, prev_sc=None):
    if after is not None:
        x, _ = lax.optimization_barrier((x, after))
    uv, mid, gates = in_proj(x, lw["norm_mix"], lw["w_in"])
    ya, vn = gmlp_mix(uv, lw["gmlp_ln_g"], lw["gmlp_ln_b"], lw["gmlp_ws"], lw["gmlp_bs"], min(seq, GMLP_CHUNK))
    yb, pool_state = pool_mix(mid, pool_prefix, pool_start, lw["pool_w"], lw["pool_scale"], batch, seq)
    if cache is None:
        yc = attn_prompt(mid, lw["rel_bias"], batch, seq)
    else:
        yc = attn_sample(mid, cache[0], cache[1], lw["rel_bias"], batch, seq)
    x = mixer_out(x, ya, yb, yc, gates, lw["w_branch"], lw["w_out"])
    ffn, stage, sc_out = peer_block(x, lw["norm_ffn"], lw["peer_wq"], lw["peer_subkeys"], lw["peer_table"], prev_sc)
    x = ple_block(x, ffn, ple, lw["norm_ple"], lw["ple_gate"], lw["ple_proj"], lw["norm_final"], final)
    return x, mid, pool_state, vn, stage, sc_out


def kernel(x_prompt, x_sample, cache_attn_k, cache_attn_v, state_pool, p_prompt, p_sample, norm_mix, w_in, gmlp_ln_g, gmlp_ln_b, gmlp_ws, gmlp_bs, pool_w, pool_scale, attn_rel_bias, w_branch, w_out, norm_ffn, peer_wq, peer_subkeys, peer_u, peer_v, norm_ple, ple_gate, ple_proj, norm_final):
    bp, lp, _ = x_prompt.shape
    bs, ls, _ = x_sample.shape
    assert lp % BAND_PAST == 0 and lp % GMLP_CHUNK == 0 and ls <= CHUNK
    n_keep = min(BAND_PAST, lp)
    sizes = _prompt_groups(bp)
    starts = [sum(sizes[:g]) for g in range(len(sizes))]
    xg = [x_prompt[a:a + n].reshape(n * lp, D_MODEL) for a, n in zip(starts, sizes)]
    xs = x_sample.reshape(bs * ls, D_MODEL)
    outs = {k: [] for k in ("pk", "pv", "pps", "sk", "sv", "sps", "sgv")}
    after = prev_sc = None
    stages = []
    for i in range(DEPTH):
        tab_u, tab_v = peer_u[i], peer_v[i]
        if stages:
            tab_u, tab_v, _ = lax.optimization_barrier((tab_u, tab_v, stages[len(stages) // 2]))
        stages = []
        lw = dict(
            norm_mix=norm_mix[i],
            w_in=w_in[i].astype(BF16),
            gmlp_ln_g=gmlp_ln_g[i], gmlp_ln_b=gmlp_ln_b[i], gmlp_ws=gmlp_ws[i], gmlp_bs=gmlp_bs[i],
            pool_w=pool_w[i], pool_scale=pool_scale[i], rel_bias=attn_rel_bias[i],
            w_branch=w_branch[i].astype(BF16), w_out=w_out[i].astype(BF16),
            norm_ffn=norm_ffn[i], peer_wq=peer_wq[i].astype(BF16),
            peer_subkeys=peer_subkeys[i].reshape(2 * PEER_HEADS, PEER_NKEYS, PEER_HALF).astype(BF16),
            peer_table=pack_expert_tables(tab_u, tab_v),
            norm_ple=norm_ple[i], ple_gate=ple_gate[i].astype(BF16), ple_proj=ple_proj[i].astype(BF16),
            norm_final=norm_final,
        )
        final = i == DEPTH - 1
        pk, pv, pps = [], [], []
        for g, (a, bg) in enumerate(zip(starts, sizes)):
            ple_g = p_prompt[i, a:a + bg].reshape(bg * lp, PLE_DIM)
            zero_prefix = jnp.zeros((bg, POOL_STATE, BW), F32)
            xg[g], mid_p, ps_p, _, after, prev_sc = _trunk_layer(xg[g], ple_g, lw, bg, lp, zero_prefix, 0, None, final,
                                                                 after, prev_sc)
            stages.append(after)
            mid_p = mid_p.reshape(bg, lp, 4 * BW)
            pk.append(mid_p[:, lp - n_keep:, 2 * BW:3 * BW].reshape(bg, n_keep, HEADS, HEAD_DIM))
            pv.append(mid_p[:, lp - n_keep:, 3 * BW:].reshape(bg, n_keep, HEADS, HEAD_DIM))
            pps.append(ps_p)
        outs["pk"].append(jnp.concatenate(pk, axis=0))
        outs["pv"].append(jnp.concatenate(pv, axis=0))
        outs["pps"].append(jnp.concatenate(pps, axis=0))
        xs, mid_s, ps_s, vn_s, _, _ = _trunk_layer(xs, p_sample[i].reshape(bs * ls, PLE_DIM), lw, bs, ls, state_pool[i],
                                                PAST_LEN, (cache_attn_k[i], cache_attn_v[i]), final)
        mid_s = mid_s.reshape(bs, ls, 4 * BW)
        outs["sk"].append(mid_s[:, :, 2 * BW:3 * BW].reshape(bs, ls, HEADS, HEAD_DIM))
        outs["sv"].append(mid_s[:, :, 3 * BW:].reshape(bs, ls, HEADS, HEAD_DIM))
        outs["sps"].append(ps_s)
        outs["sgv"].append(vn_s.reshape(bs, ls, BW))
    st = lambda k: jnp.stack(outs[k])
    y_prompt = jnp.concatenate(xg, axis=0).reshape(bp, lp, D_MODEL)
    return (y_prompt, xs.reshape(bs, ls, D_MODEL), st("pk"), st("pv"), st("pps"),
            st("sk"), st("sv"), st("sps"), st("sgv"))
```

```python
import functools
import math

import jax
import jax.numpy as jnp
import numpy as np
from jax import lax
from jax.experimental import pallas as pl
from jax.experimental.pallas import tpu as pltpu
from jax.experimental.pallas import tpu_sc as plsc

F32 = jnp.float32
BF16 = jnp.bfloat16

D_MODEL = 1024
DEPTH = 2
CHUNK = 64
EPS = 1e-6
BW = D_MODEL // 2
GMLP_CHUNK = 128
GROUPS = 4
GDIM = BW // GROUPS
POOL_WINDOWS = (2, 4, 8, 16)
POOL_STATE = 15
POOL_PAD = 16
HEADS = 8
HEAD_DIM = BW // HEADS
BAND_CHUNKS = 8
BAND_PAST = BAND_CHUNKS * CHUNK
REL_CLIP = 128
PAST_LEN = 4096
PEER_HEADS = 8
PEER_NKEYS = 128
PEER_HALF = 128
PEER_TOPK = 16
PEER_SLOTS = PEER_HEADS * PEER_TOPK
PLE_DIM = 256

LANES = 128
VMEM_LIMIT = 56 * 1024 * 1024
NEG = -1e30


def _cparams(*sem):
    return pltpu.CompilerParams(dimension_semantics=sem, vmem_limit_bytes=VMEM_LIMIT)


def _rms(x, g):
    ms = jnp.mean(x * x, axis=-1, keepdims=True)
    return x * lax.rsqrt(ms + EPS) * g


def _gelu(x):
    c = math.sqrt(2.0 / math.pi)
    return 0.5 * x * (1.0 + jnp.tanh(c * (x + 0.044715 * (x * x * x))))


def _sigmoid(x):
    return 1.0 / (1.0 + jnp.exp(-x))


IN_GELU_COLS = 2 * BW
IN_PLAIN_COLS = 4 * BW
IN_GATE_COLS = 3 * D_MODEL


def _in_proj_kernel(x_ref, g_ref, w_ref, uv_ref, mid_ref, gate_ref, h_ref):
    j = pl.program_id(1)

    @pl.when(j == 0)
    def _():
        h_ref[...] = _rms(x_ref[...], g_ref[...]).astype(BF16)

    z = jnp.dot(h_ref[...], w_ref[...], preferred_element_type=F32)
    n_gelu = IN_GELU_COLS // D_MODEL
    n_plain = IN_PLAIN_COLS // D_MODEL

    @pl.when(j < n_gelu)
    def _():
        uv_ref[...] = _gelu(z)

    @pl.when(jnp.logical_and(j >= n_gelu, j < n_gelu + n_plain))
    def _():
        mid_ref[...] = z

    @pl.when(j >= n_gelu + n_plain)
    def _():
        gate_ref[...] = _sigmoid(z).astype(gate_ref.dtype)


def in_proj(x, g, w, row0=0, rows=None):
    d = x.shape[1]
    t = x.shape[0] if rows is None else rows
    tm = min(t, 1024)
    assert row0 % tm == 0
    first = row0 // tm
    tn = D_MODEL
    n_gelu, n_plain, n_gate = IN_GELU_COLS // tn, IN_PLAIN_COLS // tn, IN_GATE_COLS // tn
    assert w.shape[1] == IN_GELU_COLS + IN_PLAIN_COLS + IN_GATE_COLS
    return pl.pallas_call(
        _in_proj_kernel,
        grid=(t // tm, n_gelu + n_plain + n_gate),
        in_specs=[
            pl.BlockSpec((tm, d), lambda i, j: (first + i, 0)),
            pl.BlockSpec((1, d), lambda i, j: (0, 0)),
            pl.BlockSpec((d, tn), lambda i, j: (0, j)),
        ],
        out_specs=[
            pl.BlockSpec((tm, tn), lambda i, j: (i, jnp.clip(j, 0, n_gelu - 1))),
            pl.BlockSpec((tm, tn), lambda i, j: (i, jnp.clip(j - n_gelu, 0, n_plain - 1))),
            pl.BlockSpec((tm, tn), lambda i, j: (i, jnp.clip(j - n_gelu - n_plain, 0, n_gate - 1))),
        ],
        out_shape=[
            jax.ShapeDtypeStruct((t, IN_GELU_COLS), F32),
            jax.ShapeDtypeStruct((t, IN_PLAIN_COLS), F32),
            jax.ShapeDtypeStruct((t, IN_GATE_COLS), BF16),
        ],
        scratch_shapes=[pltpu.VMEM((tm, d), BF16)],
        compiler_params=_cparams("parallel", "arbitrary"),
        name="in_proj",
    )(x, g.reshape(1, d), w)


def _gmlp_kernel(uv_ref, lng_ref, lnb_ref, ws_ref, bst_ref, y_ref, vn_ref, *, lc):
    u = uv_ref[:, :BW]
    v = uv_ref[:, BW:]
    mu = jnp.mean(v, axis=-1, keepdims=True)
    vc = v - mu
    var = jnp.mean(vc * vc, axis=-1, keepdims=True)
    vn = vc * lax.rsqrt(var + EPS) * lng_ref[...] + lnb_ref[...]
    vn_ref[...] = vn
    row = lax.broadcasted_iota(jnp.int32, (lc, lc), 0) // CHUNK
    col = lax.broadcasted_iota(jnp.int32, (lc, lc), 1) // CHUNK
    causal = col <= row
    vnb = vn.astype(BF16)
    for g in range(GROUPS):
        w = jnp.where(causal, ws_ref[g], 0.0).astype(BF16)
        s = jnp.dot(w, vnb[:, g * GDIM:(g + 1) * GDIM], preferred_element_type=F32)
        s = s + bst_ref[:, g:g + 1]
        y_ref[:, g * GDIM:(g + 1) * GDIM] = (u[:, g * GDIM:(g + 1) * GDIM] * s).astype(y_ref.dtype)


def gmlp_mix(uv, ln_g, ln_b, ws, bs, lc):
    t = uv.shape[0]
    return pl.pallas_call(
        functools.partial(_gmlp_kernel, lc=lc),
        grid=(t // lc,),
        in_specs=[
            pl.BlockSpec((lc, 2 * BW), lambda i: (i, 0)),
            pl.BlockSpec((1, BW), lambda i: (0, 0)),
            pl.BlockSpec((1, BW), lambda i: (0, 0)),
            pl.BlockSpec((GROUPS, lc, lc), lambda i: (0, 0, 0)),
            pl.BlockSpec((lc, GROUPS), lambda i: (0, 0)),
        ],
        out_specs=[
            pl.BlockSpec((lc, BW), lambda i: (i, 0)),
            pl.BlockSpec((lc, BW), lambda i: (i, 0)),
        ],
        out_shape=[
            jax.ShapeDtypeStruct((t, BW), BF16),
            jax.ShapeDtypeStruct((t, BW), F32),
        ],
        compiler_params=_cparams("parallel"),
        name="gmlp_mix",
    )(uv, ln_g.reshape(1, BW), ln_b.reshape(1, BW), ws[:, :lc, :lc], bs[:, :lc].T)


def _pool_kernel(u_ref, pre_ref, w_ref, sc_ref, y_ref, st_ref, pad_ref, *, seq, start_pos):
    pad_ref[0:POOL_PAD, :] = pre_ref[...]
    pad_ref[POOL_PAD:, :] = u_ref[...]
    pos = lax.broadcasted_iota(jnp.int32, (seq, 1), 0) + start_pos
    for g, win in enumerate(POOL_WINDOWS):
        cols = slice(g * GDIM, (g + 1) * GDIM)
        tok = pad_ref[POOL_PAD:, cols]
        acc = tok
        for k in range(1, win):
            acc = acc + pad_ref[POOL_PAD - k:POOL_PAD - k + seq, cols]
        cnt = jnp.minimum(pos + 1, win).astype(F32)
        d = acc / cnt - tok
        y = jnp.dot(d.astype(BF16), w_ref[g], preferred_element_type=F32)
        y_ref[:, cols] = (y * sc_ref[:, cols]).astype(y_ref.dtype)
    st_ref[...] = pad_ref[seq + 1:seq + POOL_PAD, :]


def pool_mix(mid, prefix, start_pos, pool_w, pool_scale, batch, seq):
    mid3 = mid.reshape(batch, seq, 4 * BW)
    pre = jnp.concatenate([jnp.zeros((batch, 1, BW), F32), prefix], axis=1)
    y, st = pl.pallas_call(
        functools.partial(_pool_kernel, seq=seq, start_pos=start_pos),
        grid=(batch,),
        in_specs=[
            pl.BlockSpec((None, seq, BW), lambda b: (b, 0, 0)),
            pl.BlockSpec((None, POOL_PAD, BW), lambda b: (b, 0, 0)),
            pl.BlockSpec((GROUPS, GDIM, GDIM), lambda b: (0, 0, 0)),
            pl.BlockSpec((1, BW), lambda b: (0, 0)),
        ],
        out_specs=[
            pl.BlockSpec((None, seq, BW), lambda b: (b, 0, 0)),
            pl.BlockSpec((None, POOL_STATE, BW), lambda b: (b, 0, 0)),
        ],
        out_shape=[
            jax.ShapeDtypeStruct((batch, seq, BW), BF16),
            jax.ShapeDtypeStruct((batch, POOL_STATE, BW), F32),
        ],
        scratch_shapes=[pltpu.VMEM((seq + POOL_PAD, BW), F32)],
        compiler_params=_cparams("parallel"),
        name="pool_mix",
    )(mid3, pre, pool_w.astype(BF16), pool_scale.reshape(1, BW))
    return y.reshape(batch * seq, BW), st


def _attn_chunks(q_ref, kcat_ref, vcat_ref, bias_ref, o_ref, *, n_chunks, cq, band, first_block):
    scale = HEAD_DIM ** -0.5
    heads_per_tile = LANES // HEAD_DIM
    lane_head = lax.broadcasted_iota(jnp.int32, (1, LANES), 1) // HEAD_DIM
    for tile in range(HEADS // heads_per_tile):
        cols = slice(tile * LANES, (tile + 1) * LANES)
        for i in range(n_chunks):
            rows = slice(i * cq, (i + 1) * cq)
            q = (q_ref[rows, cols] * scale).astype(BF16)
            k = kcat_ref[i * cq:i * cq + band, cols]
            v = vcat_ref[i * cq:i * cq + band, cols]
            out = None
            for j in range(heads_per_tile):
                own = lane_head == j
                s = lax.dot_general(jnp.where(own, q, jnp.zeros_like(q)), k, (((1,), (1,)), ((), ())),
                                    preferred_element_type=F32) + bias_ref[tile * heads_per_tile + j]
                if first_block is not None:
                    key = lax.broadcasted_iota(jnp.int32, (1, band), 1)
                    s = jnp.where(key >= first_block * (BAND_PAST - i * cq), s, NEG)
                m = jnp.max(s, axis=-1, keepdims=True)
                p = jnp.exp(s - m)
                l = jnp.sum(p, axis=-1, keepdims=True)
                o = jnp.dot(p.astype(BF16), v, preferred_element_type=F32) / l
                out = o if out is None else jnp.where(own, o, out)
            o_ref[rows, cols] = out.astype(o_ref.dtype)


def _attn_prompt_kernel(q_ref, kp_ref, ko_ref, vp_ref, vo_ref, bias_ref, o_ref, kcat_ref, vcat_ref):
    kcat_ref[0:BAND_PAST, :] = kp_ref[...].astype(BF16)
    kcat_ref[BAND_PAST:, :] = ko_ref[...].astype(BF16)
    vcat_ref[0:BAND_PAST, :] = vp_ref[...].astype(BF16)
    vcat_ref[BAND_PAST:, :] = vo_ref[...].astype(BF16)
    _attn_chunks(q_ref, kcat_ref, vcat_ref, bias_ref, o_ref, n_chunks=BAND_CHUNKS, cq=CHUNK,
                 band=BAND_PAST + CHUNK, first_block=(pl.program_id(1) == 0).astype(jnp.int32))


def _rel_bias_tile(rel_bias, qpos, kpos):
    lq, lk = len(qpos), len(kpos)
    assert (np.diff(qpos) == 1).all() and (np.diff(kpos) == 1).all()
    lr = lq + lk - 1
    dist = (qpos[0] - kpos[0]) - (np.arange(lr) - (lq - 1))
    diag = rel_bias[:, np.clip(dist, -REL_CLIP, REL_CLIP) + REL_CLIP]
    skewed = jnp.tile(diag, (1, lq))[:, lq - 1:lq - 1 + lq * (lr - 1)].reshape(-1, lq, lr - 1)
    return skewed[:, :, :lk]


def attn_prompt(mid, rel_bias, batch, seq):
    mid3 = mid.reshape(batch, seq, 4 * BW)
    blk = BAND_PAST
    bias = _rel_bias_tile(rel_bias, np.arange(CHUNK), np.arange(BAND_PAST + CHUNK) - BAND_PAST)
    prev = lambda b, j: jnp.maximum(j - 1, 0)
    y = pl.pallas_call(
        _attn_prompt_kernel,
        grid=(batch, seq // blk),
        in_specs=[
            pl.BlockSpec((None, blk, BW), lambda b, j: (b, j, 1)),
            pl.BlockSpec((None, blk, BW), lambda b, j: (b, prev(b, j), 2)),
            pl.BlockSpec((None, blk, BW), lambda b, j: (b, j, 2)),
            pl.BlockSpec((None, blk, BW), lambda b, j: (b, prev(b, j), 3)),
            pl.BlockSpec((None, blk, BW), lambda b, j: (b, j, 3)),
            pl.BlockSpec((HEADS, CHUNK, BAND_PAST + CHUNK), lambda b, j: (0, 0, 0)),
        ],
        out_specs=pl.BlockSpec((None, blk, BW), lambda b, j: (b, j, 0)),
        out_shape=jax.ShapeDtypeStruct((batch, seq, BW), BF16),
        scratch_shapes=[pltpu.VMEM((2 * blk, BW), BF16), pltpu.VMEM((2 * blk, BW), BF16)],
        compiler_params=_cparams("parallel", "parallel"),
        name="attn_prompt",
    )(mid3, mid3, mid3, mid3, mid3, bias)
    return y.reshape(batch * seq, BW)


def _attn_sample_kernel(q_ref, kc_ref, kn_ref, vc_ref, vn_ref, bias_ref, o_ref, kcat_ref, vcat_ref, *, n_cache):
    kcat_ref[0:n_cache, :] = kc_ref[...].astype(BF16)
    kcat_ref[n_cache:, :] = kn_ref[...].astype(BF16)
    vcat_ref[0:n_cache, :] = vc_ref[...].astype(BF16)
    vcat_ref[n_cache:, :] = vn_ref[...].astype(BF16)
    seq = q_ref.shape[0]
    _attn_chunks(q_ref, kcat_ref, vcat_ref, bias_ref, o_ref, n_chunks=1, cq=seq, band=n_cache + seq,
                 first_block=None)


def attn_sample(mid, cache_k, cache_v, rel_bias, batch, seq):
    n_cache = cache_k.shape[1]
    assert PAST_LEN >= n_cache
    mid3 = mid.reshape(batch, seq, 4 * BW)
    ck = cache_k.reshape(batch, n_cache, BW)
    cv = cache_v.reshape(batch, n_cache, BW)
    bias = _rel_bias_tile(rel_bias, PAST_LEN + np.arange(seq), PAST_LEN - n_cache + np.arange(n_cache + seq))
    y = pl.pallas_call(
        functools.partial(_attn_sample_kernel, n_cache=n_cache),
        grid=(batch,),
        in_specs=[
            pl.BlockSpec((None, seq, BW), lambda b: (b, 0, 1)),
            pl.BlockSpec((None, n_cache, BW), lambda b: (b, 0, 0)),
            pl.BlockSpec((None, seq, BW), lambda b: (b, 0, 2)),
            pl.BlockSpec((None, n_cache, BW), lambda b: (b, 0, 0)),
            pl.BlockSpec((None, seq, BW), lambda b: (b, 0, 3)),
            pl.BlockSpec((HEADS, seq, n_cache + seq), lambda b: (0, 0, 0)),
        ],
        out_specs=pl.BlockSpec((None, seq, BW), lambda b: (b, 0, 0)),
        out_shape=jax.ShapeDtypeStruct((batch, seq, BW), BF16),
        scratch_shapes=[pltpu.VMEM((n_cache + seq, BW), BF16), pltpu.VMEM((n_cache + seq, BW), BF16)],
        compiler_params=_cparams("parallel"),
        name="attn_sample",
    )(mid3, ck, mid3, cv, mid3, bias)
    return y.reshape(batch * seq, BW)


def _mixer_out_kernel(x_ref, ya_ref, yb_ref, yc_ref, gate_ref, wb_ref, wo_ref, o_ref):
    acc = None
    for n, y_ref in enumerate((ya_ref, yb_ref, yc_ref)):
        proj = jnp.dot(y_ref[...], wb_ref[n], preferred_element_type=F32)
        term = gate_ref[:, n * D_MODEL:(n + 1) * D_MODEL].astype(F32) * proj
        acc = term if acc is None else acc + term
    o_ref[...] = x_ref[...] + jnp.dot(acc.astype(BF16), wo_ref[...], preferred_element_type=F32)


def mixer_out(x, ya, yb, yc, gates, w_branch, w_out, row0=0):
    t = ya.shape[0]
    tm = min(t, 512)
    assert row0 % tm == 0
    first = row0 // tm
    row = lambda i: (i, 0)
    return pl.pallas_call(
        _mixer_out_kernel,
        grid=(t // tm,),
        in_specs=[
            pl.BlockSpec((tm, D_MODEL), lambda i: (first + i, 0)),
            pl.BlockSpec((tm, BW), row),
            pl.BlockSpec((tm, BW), row),
            pl.BlockSpec((tm, BW), row),
            pl.BlockSpec((tm, 3 * D_MODEL), row),
            pl.BlockSpec((3, BW, D_MODEL), lambda i: (0, 0, 0)),
            pl.BlockSpec((D_MODEL, D_MODEL), lambda i: (0, 0)),
        ],
        out_specs=pl.BlockSpec((tm, D_MODEL), row),
        out_shape=jax.ShapeDtypeStruct((t, D_MODEL), F32),
        compiler_params=_cparams("parallel"),
        name="mixer_out",
    )(x, ya, yb, yc, gates, w_branch, w_out)


def _extract_top(s, payload, k):
    r = float(s.shape[0])
    rows = lax.broadcasted_iota(jnp.int32, s.shape, 0).astype(F32)
    vals, pays = [], []
    for _ in range(k):
        m = jnp.max(s, axis=0, keepdims=True)
        idx = jnp.min(jnp.where(s == m, rows, r), axis=0, keepdims=True)
        sel = rows == idx
        vals.append(m)
        pays.append(idx if payload is None else jnp.max(jnp.where(sel, payload, -1.0), axis=0, keepdims=True))
        s = jnp.where(sel, -jnp.inf, s)
    return jnp.concatenate(vals, axis=0), jnp.concatenate(pays, axis=0)


def _pair_candidates(sv, si):
    k = PEER_TOPK
    sub = 8
    assert k == 2 * sub
    b_row = lax.broadcasted_iota(jnp.int32, (sub, LANES), 0)
    vals = [sv[0][0:1] + sv[1], sv[0][1:2] + sv[1][0:sub]]
    ids = [si[0][0:1] * PEER_NKEYS + si[1], si[0][1:2] * PEER_NKEYS + si[1][0:sub]]
    for a in range(2, sub):
        keep = b_row < k // (a + 1)
        vals.append(jnp.where(keep, sv[0][a:a + 1] + sv[1][0:sub], -jnp.inf))
        ids.append(si[0][a:a + 1] * PEER_NKEYS + si[1][0:sub])
    vals.append(sv[0][sub:k] + sv[1][0:1])
    ids.append(si[0][sub:k] * PEER_NKEYS + si[1][0:1])
    return jnp.concatenate(vals, axis=0), jnp.concatenate(ids, axis=0)


def _peer_query_kernel(x_ref, g_ref, wq_ref, sk_ref, hp_ref, idx_ref, gate_ref, q_ref, *, tm):
    hb = _rms(x_ref[...], g_ref[...]).astype(BF16)
    q_ref[...] = jnp.dot(hb, wq_ref[...], preferred_element_type=F32).astype(BF16)
    bits = lax.bitcast_convert_type(hb.astype(F32), jnp.int32)
    half = D_MODEL // 2
    hp_ref[...] = bits[:, half:] | lax.shift_right_logical(bits[:, :half], 16)

    def sub_block(sb, carry):
        tok = pl.ds(pl.multiple_of(sb * LANES, LANES), LANES)
        for hd in range(PEER_HEADS):
            sv, si = [], []
            for p in range(2):
                hp = hd * 2 + p
                q = q_ref[tok, hp * PEER_HALF:(hp + 1) * PEER_HALF]
                s = lax.dot_general(sk_ref[hp], q, (((1,), (1,)), ((), ())), preferred_element_type=F32)
                v, i = _extract_top(s, None, PEER_TOPK)
                sv.append(v)
                si.append(i)
            cand, eid = _pair_candidates(sv, si)
            tv, te = _extract_top(cand, eid, PEER_TOPK)
            e = jnp.exp(tv - tv[0:1])
            gate = e / jnp.sum(e, axis=0, keepdims=True)
            idx_ref[sb, hd * PEER_TOPK:(hd + 1) * PEER_TOPK, :] = te.astype(jnp.int32)
            gate_ref[sb, hd * PEER_TOPK:(hd + 1) * PEER_TOPK, :] = gate
        return carry

    lax.fori_loop(0, tm // LANES, sub_block, 0)


def peer_query(x, norm_g, wq, subkeys):
    t = x.shape[0]
    tm = min(t, 512)
    nq = wq.shape[1]
    nsb = tm // LANES
    return pl.pallas_call(
        functools.partial(_peer_query_kernel, tm=tm),
        grid=(t // tm,),
        in_specs=[
            pl.BlockSpec((tm, D_MODEL), lambda i: (i, 0)),
            pl.BlockSpec((1, D_MODEL), lambda i: (0, 0)),
            pl.BlockSpec((D_MODEL, nq), lambda i: (0, 0)),
            pl.BlockSpec((2 * PEER_HEADS, PEER_NKEYS, PEER_HALF), lambda i: (0, 0, 0)),
        ],
        out_specs=[
            pl.BlockSpec((tm, D_MODEL // 2), lambda i: (i, 0)),
            pl.BlockSpec((nsb, PEER_SLOTS, LANES), lambda i: (i, 0, 0)),
            pl.BlockSpec((nsb, PEER_SLOTS, LANES), lambda i: (i, 0, 0)),
        ],
        out_shape=[
            jax.ShapeDtypeStruct((t, D_MODEL // 2), jnp.int32),
            jax.ShapeDtypeStruct((t // LANES, PEER_SLOTS, LANES), jnp.int32),
            jax.ShapeDtypeStruct((t // LANES, PEER_SLOTS, LANES), F32),
        ],
        scratch_shapes=[pltpu.VMEM((tm, nq), BF16)],
        compiler_params=_cparams("parallel"),
        name="peer_query",
    )(x, norm_g.reshape(1, D_MODEL), wq, subkeys)


HALF_D = D_MODEL // 2
SC_LANES = 16
SC_UNIT_ROWS = 32
SC_UNITS_PER_TOKEN = PEER_SLOTS // SC_UNIT_ROWS
SC_MAX_TOKEN_BLOCK = 8
SC_BUFFERS = 3
SC_ROW_GROUP = 8
SC_HALF_VECS = HALF_D // SC_LANES
SC_OUT_VECS = 8
SC_BF16_TERMS = 4
SC_ROW_TILES = D_MODEL // LANES


PACK_ROWS = 256


def _pack_kernel(u_ref, v_ref, o_ref):
    def words(x):
        bits = lax.bitcast_convert_type(x.astype(BF16).astype(F32), jnp.int32)
        return bits[:, HALF_D:] | lax.shift_right_logical(bits[:, :HALF_D], 16)

    half_tiles = SC_ROW_TILES // 2
    for base, w in ((0, words(u_ref[...])), (half_tiles, words(v_ref[...]))):
        for j in range(half_tiles):
            o_ref[:, base + j, :] = w[:, j * LANES:(j + 1) * LANES]


def pack_expert_tables(peer_u, peer_v):
    e = peer_u.shape[0]
    return pl.pallas_call(
        _pack_kernel,
        grid=(e // PACK_ROWS,),
        in_specs=[pl.BlockSpec((PACK_ROWS, D_MODEL), lambda i: (i, 0))] * 2,
        out_specs=pl.BlockSpec((PACK_ROWS, SC_ROW_TILES, LANES), lambda i: (i, 0, 0)),
        out_shape=jax.ShapeDtypeStruct((e, SC_ROW_TILES, LANES), jnp.int32),
        compiler_params=_cparams("parallel"),
        name="pack_expert_tables",
    )(peer_u, peer_v)


def sc_peer_experts(table, idx, gates, hp):
    t = hp.shape[0]
    info = plsc.get_sparse_core_info()
    n_workers = info.num_cores * info.num_subcores
    tpw = t // n_workers
    token_block = min(tpw, SC_MAX_TOKEN_BLOCK)
    assert t % (n_workers * token_block) == 0
    units = token_block * SC_UNITS_PER_TOKEN
    mesh = plsc.VectorSubcoreMesh(core_axis_name="core", subcore_axis_name="subcore")
    hi_mask = jnp.int32(-65536)
    gelu_c = math.sqrt(2.0 / math.pi)

    @functools.partial(
        pl.kernel,
        out_type=jax.ShapeDtypeStruct((t, D_MODEL), F32),
        mesh=mesh,
        scratch_types=[
            pltpu.VMEM((2, token_block * PEER_SLOTS), jnp.int32),
            pltpu.VMEM((2, token_block * PEER_SLOTS), F32),
            pltpu.VMEM((2, token_block, HALF_D), jnp.int32),
            pltpu.VMEM((token_block, D_MODEL), F32),
            pltpu.VMEM((SC_BUFFERS, SC_UNIT_ROWS, SC_ROW_TILES, LANES), jnp.int32),
            pltpu.VMEM((SC_UNIT_ROWS, SC_LANES), F32),
            pltpu.VMEM((SC_UNIT_ROWS,), jnp.int32),
            pltpu.SemaphoreType.DMA((SC_BUFFERS,)),
            pltpu.SemaphoreType.DMA((2,)),
        ],
        compiler_params=pltpu.CompilerParams(needs_layout_passes=False),
        name="peer_sc_experts",
    )
    def kern(table_hbm, idx_hbm, gate_hbm, h_hbm, out_hbm, idx_v, gate_v, h_v, out_v, rows_v, part_v, coef_v, sem,
             in_sem):
        wid = lax.axis_index("subcore") * info.num_cores + lax.axis_index("core")
        lane = lax.iota(jnp.int32, SC_LANES)
        zero = jnp.zeros((SC_LANES,), F32)

        def gather(unit, b, s):
            rows = idx_v.at[s, pl.ds(unit * SC_UNIT_ROWS, SC_UNIT_ROWS)]
            return pltpu.make_async_copy(table_hbm.at[rows], rows_v.at[b], sem.at[b])

        def row_vec(b, r, vec):
            per_tile_row = LANES // SC_LANES
            lane0 = pl.multiple_of((vec % per_tile_row) * SC_LANES, SC_LANES)
            return rows_v[b, r, vec // per_tile_row, pl.ds(lane0, SC_LANES)]

        def unpack(w):
            return lax.bitcast_convert_type(w << 16, F32), lax.bitcast_convert_type(w & hi_mask, F32)

        def as_pairs(w):
            return plsc.bitcast(w, BF16)

        def compute(unit, b, s):
            tl = unit // SC_UNITS_PER_TOKEN
            q = unit % SC_UNITS_PER_TOKEN

            def row_group(rg, carry):
                def kstep(k, accs):
                    hs = []
                    for j in range(SC_BF16_TERMS):
                        off = pl.multiple_of((k * SC_BF16_TERMS + j) * SC_LANES, SC_LANES)
                        hs.append(as_pairs(h_v[s, tl, pl.ds(off, SC_LANES)]))
                    new = []
                    for r in range(SC_ROW_GROUP):
                        p = None
                        for j in range(SC_BF16_TERMS):
                            term = as_pairs(row_vec(b, rg * SC_ROW_GROUP + r, k * SC_BF16_TERMS + j)) * hs[j]
                            p = term if p is None else p + term
                        lo, hi = unpack(plsc.bitcast(p, jnp.int32))
                        new.append(accs[r] + lo + hi)
                    return tuple(new)

                accs = lax.fori_loop(0, SC_HALF_VECS // SC_BF16_TERMS, kstep, (zero,) * SC_ROW_GROUP)
                for r in range(SC_ROW_GROUP):
                    part_v[rg * SC_ROW_GROUP + r, :] = accs[r]
                return carry

            lax.fori_loop(0, SC_UNIT_ROWS // SC_ROW_GROUP, row_group, 0)

            dots = []
            for i in range(SC_UNIT_ROWS // SC_LANES):
                rows = lane + i * SC_LANES
                terms = [plsc.load_gather(part_v, [rows, jnp.full((SC_LANES,), l, jnp.int32)])
                         for l in range(SC_LANES)]
                while len(terms) > 1:
                    terms = [terms[j] + terms[j + 1] for j in range(0, len(terms), 2)]
                dots.append(terms[0])
            for i, a in enumerate(dots):
                z = gelu_c * (a + 0.044715 * (a * a * a))
                act = a / (1.0 + jnp.exp(-2.0 * z))
                slot = pl.multiple_of(tl * PEER_SLOTS + q * SC_UNIT_ROWS + i * SC_LANES, SC_LANES)
                bits = lax.bitcast_convert_type(gate_v[s, pl.ds(slot, SC_LANES)] * act, jnp.int32)
                top = (bits + 0x7FFF + ((bits >> 16) & 1)) & hi_mask
                coef_v[pl.ds(i * SC_LANES, SC_LANES)] = top | lax.shift_right_logical(top, 16)

            def out_pass(dq, carry):
                def row_quad(rq, accs):
                    per_vec = SC_LANES // SC_BF16_TERMS
                    cvec = coef_v[pl.ds(pl.multiple_of((rq // per_vec) * SC_LANES, SC_LANES), SC_LANES)]
                    cs = []
                    for j in range(SC_BF16_TERMS):
                        src = jnp.full((SC_LANES,), (rq % per_vec) * SC_BF16_TERMS + j, jnp.int32)
                        cs.append(as_pairs(cvec.at[src].get(mode="promise_in_bounds")))
                    new = []
                    for k in range(SC_OUT_VECS):
                        p = None
                        for j in range(SC_BF16_TERMS):
                            w = row_vec(b, rq * SC_BF16_TERMS + j, SC_HALF_VECS + dq * SC_OUT_VECS + k)
                            term = as_pairs(w) * cs[j]
                            p = term if p is None else p + term
                        lo, hi = unpack(plsc.bitcast(p, jnp.int32))
                        new.append(accs[2 * k] + lo)
                        new.append(accs[2 * k + 1] + hi)
                    return tuple(new)

                accs = lax.fori_loop(0, SC_UNIT_ROWS // SC_BF16_TERMS, row_quad, (zero,) * (2 * SC_OUT_VECS))
                for k in range(SC_OUT_VECS):
                    off = pl.multiple_of((dq * SC_OUT_VECS + k) * SC_LANES, SC_LANES)
                    plsc.addupdate(out_v.at[tl, pl.ds(off, SC_LANES)], accs[2 * k])
                    plsc.addupdate(out_v.at[tl, pl.ds(HALF_D + off, SC_LANES)], accs[2 * k + 1])
                return carry

            lax.fori_loop(0, SC_HALF_VECS // SC_OUT_VECS, out_pass, 0)

        n_blocks = tpw // token_block

        def block_inputs(blk, s):
            tok0 = wid * tpw + blk * token_block
            slots = pl.ds(tok0 * PEER_SLOTS, token_block * PEER_SLOTS)
            return (pltpu.make_async_copy(idx_hbm.at[slots], idx_v.at[s], in_sem.at[s]),
                    pltpu.make_async_copy(gate_hbm.at[slots], gate_v.at[s], in_sem.at[s]),
                    pltpu.make_async_copy(h_hbm.at[pl.ds(tok0, token_block)], h_v.at[s], in_sem.at[s]))

        def first_gathers(s):
            for u in range(SC_BUFFERS - 1):
                gather(u, u, s).start()

        for c in block_inputs(0, 0):
            c.start()
        for c in block_inputs(0, 0):
            c.wait()
        first_gathers(0)

        @pl.loop(0, n_blocks)
        def _(blk):
            s = blk % 2
            more = blk + 1 < n_blocks

            @pl.when(more)
            def _():
                for c in block_inputs(blk + 1, 1 - s):
                    c.start()

            @pl.loop(0, token_block)
            def _(tl):
                @pl.loop(0, D_MODEL // SC_LANES)
                def _(k):
                    out_v[tl, pl.ds(pl.multiple_of(k * SC_LANES, SC_LANES), SC_LANES)] = zero

            @pl.loop(0, units)
            def _(unit):
                ahead = unit + SC_BUFFERS - 1

                @pl.when(ahead < units)
                def _():
                    gather(ahead, ahead % SC_BUFFERS, s).start()

                b = unit % SC_BUFFERS
                gather(unit, b, s).wait()
                compute(unit, b, s)

            @pl.when(more)
            def _():
                for c in block_inputs(blk + 1, 1 - s):
                    c.wait()
                first_gathers(1 - s)

            tok0 = wid * tpw + blk * token_block
            pltpu.sync_copy(out_v, out_hbm.at[pl.ds(tok0, token_block)])

    return kern(table, idx, gates, hp)


TC_PEER_TOKENS = 8
TC_PEER_BUFFERS = 4
U_TILE_ROWS = SC_ROW_TILES // 2


def _tc_peer_kernel(idx_ref, idx_next_ref, gate_ref, hp_ref, table_hbm, o_ref, rows_buf, a_buf, c_buf, sem):
    step = pl.program_id(0)
    hi_mask = jnp.int32(-65536)

    def unpack(w):
        return lax.bitcast_convert_type(w << 16, F32), lax.bitcast_convert_type(w & hi_mask, F32)

    def row_copy(ids_ref, tok, e, slot):
        return pltpu.make_async_copy(table_hbm.at[ids_ref[tok * PEER_SLOTS + e]], rows_buf.at[slot, e], sem.at[slot])

    def wait_rows(slot):
        pltpu.make_async_copy(table_hbm.at[pl.ds(0, PEER_SLOTS)], rows_buf.at[slot], sem.at[slot]).wait()

    ahead = TC_PEER_BUFFERS - 1

    @pl.when(step == 0)
    def _():
        for tok in range(ahead):
            def first(e, c, tok=tok):
                row_copy(idx_ref, tok, e, tok).start()
                return c

            lax.fori_loop(0, PEER_SLOTS, first, 0, unroll=8)

    gates_t = gate_ref[...].T
    zeros_v = jnp.zeros((SC_ROW_TILES - U_TILE_ROWS, LANES), F32)
    for t in range(TC_PEER_TOKENS):
        slot = t % TC_PEER_BUFFERS
        nxt = t + ahead
        nxt_slot = nxt % TC_PEER_BUFFERS
        wait_rows(slot)
        h_lo, h_hi = unpack(hp_ref[t])
        h_lo = jnp.concatenate([h_lo, zeros_v], axis=0)
        h_hi = jnp.concatenate([h_hi, zeros_v], axis=0)

        def u_body(e, c, slot=slot, nxt=nxt, nxt_slot=nxt_slot, h_lo=h_lo, h_hi=h_hi):
            if nxt >= TC_PEER_TOKENS:
                @pl.when(step + 1 < pl.num_programs(0))
                def _():
                    row_copy(idx_next_ref, nxt - TC_PEER_TOKENS, e, nxt_slot).start()
            else:
                row_copy(idx_ref, nxt, e, nxt_slot).start()
            lo, hi = unpack(rows_buf[slot, e])
            a_buf[pl.ds(e, 1), :] = jnp.sum(lo * h_lo + hi * h_hi, axis=0, keepdims=True)
            return c

        lax.fori_loop(0, PEER_SLOTS, u_body, 0, unroll=8)
        a = jnp.sum(a_buf[...], axis=1, keepdims=True)
        c_buf[...] = jnp.broadcast_to(gates_t[:, t:t + 1] * _gelu(a), (PEER_SLOTS, LANES))

        def v_body(e, acc, slot=slot):
            lo, hi = unpack(rows_buf[slot, e])
            c = c_buf[pl.ds(e, 1), :]
            return acc[0] + c * lo, acc[1] + c * hi

        zero = jnp.zeros((SC_ROW_TILES, LANES), F32)
        acc_lo, acc_hi = lax.fori_loop(0, PEER_SLOTS, v_body, (zero, zero), unroll=8)
        o_ref[t, 0:U_TILE_ROWS, :] = acc_lo[U_TILE_ROWS:, :]
        o_ref[t, U_TILE_ROWS:, :] = acc_hi[U_TILE_ROWS:, :]


def tc_peer_experts(table, idx, gates, hp):
    t = hp.shape[0]
    assert t % TC_PEER_TOKENS == 0 and TC_PEER_TOKENS % TC_PEER_BUFFERS == 0
    n_steps = t // TC_PEER_TOKENS
    ids = TC_PEER_TOKENS * PEER_SLOTS
    smem = functools.partial(pl.BlockSpec, memory_space=pltpu.SMEM)
    out = pl.pallas_call(
        _tc_peer_kernel,
        grid=(n_steps,),
        in_specs=[
            smem((ids,), lambda i: (i,)),
            smem((ids,), lambda i: (jnp.minimum(i + 1, n_steps - 1),)),
            pl.BlockSpec((TC_PEER_TOKENS, PEER_SLOTS), lambda i: (i, 0)),
            pl.BlockSpec((TC_PEER_TOKENS, U_TILE_ROWS, LANES), lambda i: (i, 0, 0)),
            pl.BlockSpec(memory_space=pl.ANY),
        ],
        out_specs=pl.BlockSpec((TC_PEER_TOKENS, SC_ROW_TILES, LANES), lambda i: (i, 0, 0)),
        out_shape=jax.ShapeDtypeStruct((t, SC_ROW_TILES, LANES), F32),
        scratch_shapes=[
            pltpu.VMEM((TC_PEER_BUFFERS, PEER_SLOTS, SC_ROW_TILES, LANES), jnp.int32),
            pltpu.VMEM((PEER_SLOTS, LANES), F32),
            pltpu.VMEM((PEER_SLOTS, LANES), F32),
            pltpu.SemaphoreType.DMA((TC_PEER_BUFFERS,)),
        ],
        compiler_params=pltpu.CompilerParams(dimension_semantics=("arbitrary",), vmem_limit_bytes=VMEM_LIMIT,
                                             disable_bounds_checks=True),
        name="tc_peer_experts",
    )(idx, idx, gates, hp.reshape(t, U_TILE_ROWS, LANES), table)
    return out.reshape(t, D_MODEL)


TC_PEER_SHARE = 8
TC_PEER_QUANTUM = 1024


def peer_block(x, norm_g, wq, subkeys, table, prev_sc=None):
    t = x.shape[0]
    hp, idx3, gate3 = peer_query(x, norm_g, wq, subkeys)
    idx = jnp.transpose(idx3, (0, 2, 1)).reshape(t * PEER_SLOTS)
    gates = jnp.transpose(gate3, (0, 2, 1)).reshape(t * PEER_SLOTS)
    t_tc = (t // TC_PEER_SHARE) // TC_PEER_QUANTUM * TC_PEER_QUANTUM
    t_sc = t - t_tc
    idx_sc = idx[:t_sc * PEER_SLOTS]
    if prev_sc is not None:
        idx_sc, _ = lax.optimization_barrier((idx_sc, prev_sc))
    on_sc = sc_peer_experts(table, idx_sc, gates[:t_sc * PEER_SLOTS], hp[:t_sc])
    if t_tc == 0:
        return on_sc, (hp,), on_sc
    on_tc = tc_peer_experts(table, idx[t_sc * PEER_SLOTS:], gates[t_sc * PEER_SLOTS:].reshape(t_tc, PEER_SLOTS),
                            hp[t_sc:])
    return jnp.concatenate([on_sc, on_tc], axis=0), (hp, on_tc), on_sc


def _ple_kernel(x_ref, ffn_ref, p_ref, g_ref, wg_ref, wp_ref, gf_ref, o_ref, *, final):
    x = x_ref[...] + ffn_ref[...]
    gate = _sigmoid(jnp.dot(_rms(x, g_ref[...]).astype(BF16), wg_ref[...], preferred_element_type=F32))
    emb = jnp.dot(p_ref[...].astype(BF16), wp_ref[...], preferred_element_type=F32)
    y = x + gate * emb
    o_ref[...] = _rms(y, gf_ref[...]) if final else y


def ple_block(x, ffn, p, norm_g, w_gate, w_proj, norm_final, final, p_row0=0):
    t = x.shape[0]
    tm = min(t, 1024)
    assert p_row0 % tm == 0
    p_first = p_row0 // tm
    row = lambda i: (i, 0)
    full = lambda i: (0, 0)
    return pl.pallas_call(
        functools.partial(_ple_kernel, final=final),
        grid=(t // tm,),
        in_specs=[
            pl.BlockSpec((tm, D_MODEL), row),
            pl.BlockSpec((tm, D_MODEL), row),
            pl.BlockSpec((tm, PLE_DIM), lambda i: (p_first + i, 0)),
            pl.BlockSpec((1, D_MODEL), full),
            pl.BlockSpec((D_MODEL, D_MODEL), full),
            pl.BlockSpec((PLE_DIM, D_MODEL), full),
            pl.BlockSpec((1, D_MODEL), full),
        ],
        out_specs=pl.BlockSpec((tm, D_MODEL), row),
        out_shape=jax.ShapeDtypeStruct((t, D_MODEL), F32),
        compiler_params=_cparams("parallel"),
        name="ple_block",
    )(x, ffn, p, norm_g.reshape(1, D_MODEL), w_gate, w_proj, norm_final.reshape(1, D_MODEL))


def _prompt_groups(batch):
    sizes = []
    while sum(sizes) < batch:
        nxt = 1 if len(sizes) < 2 else -(-sizes[-1] * 7 // 5)
        sizes.append(min(nxt, batch - sum(sizes)))
    return sizes
def _trunk_layer(x, ple, lw, batch, seq, pool_prefix, pool_start, cache, final, after=None, prev_sc=None,
                 x_row0=0, ple_row0=0):
    if after is not None:
        x, _ = lax.optimization_barrier((x, after))
    uv, mid, gates = in_proj(x, lw["norm_mix"], lw["w_in"], x_row0, batch * seq)
    ya, vn = gmlp_mix(uv, lw["gmlp_ln_g"], lw["gmlp_ln_b"], lw["gmlp_ws"], lw["gmlp_bs"], min(seq, GMLP_CHUNK))
    yb, pool_state = pool_mix(mid, pool_prefix, pool_start, lw["pool_w"], lw["pool_scale"], batch, seq)
    if cache is None:
        yc = attn_prompt(mid, lw["rel_bias"], batch, seq)
    else:
        yc = attn_sample(mid, cache[0], cache[1], lw["rel_bias"], batch, seq)
    x = mixer_out(x, ya, yb, yc, gates, lw["w_branch"], lw["w_out"], x_row0)
    ffn, stage, sc_out = peer_block(x, lw["norm_ffn"], lw["peer_wq"], lw["peer_subkeys"], lw["peer_table"], prev_sc)
    x = ple_block(x, ffn, ple, lw["norm_ple"], lw["ple_gate"], lw["ple_proj"], lw["norm_final"], final, ple_row0)
    return x, mid, pool_state, vn, stage, sc_out


def kernel(x_prompt, x_sample, cache_attn_k, cache_attn_v, state_pool, p_prompt, p_sample, norm_mix, w_in, gmlp_ln_g, gmlp_ln_b, gmlp_ws, gmlp_bs, pool_w, pool_scale, attn_rel_bias, w_branch, w_out, norm_ffn, peer_wq, peer_subkeys, peer_u, peer_v, norm_ple, ple_gate, ple_proj, norm_final):
    bp, lp, _ = x_prompt.shape
    bs, ls, _ = x_sample.shape
    assert lp % BAND_PAST == 0 and lp % GMLP_CHUNK == 0 and ls <= CHUNK
    n_keep = min(BAND_PAST, lp)
    sizes = _prompt_groups(bp)
    starts = [sum(sizes[:g]) for g in range(len(sizes))]
    x_all = x_prompt.reshape(bp * lp, D_MODEL)
    ple_all = p_prompt.reshape(DEPTH * bp * lp, PLE_DIM)
    xg = [None] * len(sizes)
    xs = x_sample.reshape(bs * ls, D_MODEL)
    outs = {k: [] for k in ("pk", "pv", "pps", "sk", "sv", "sps", "sgv")}
    after = prev_sc = None
    stages = []
    for i in range(DEPTH):
        tab_u, tab_v = peer_u[i], peer_v[i]
        if stages:
            tab_u, tab_v, _ = lax.optimization_barrier((tab_u, tab_v, stages[len(stages) // 2]))
        stages = []
        lw = dict(
            norm_mix=norm_mix[i],
            w_in=w_in[i].astype(BF16),
            gmlp_ln_g=gmlp_ln_g[i], gmlp_ln_b=gmlp_ln_b[i], gmlp_ws=gmlp_ws[i], gmlp_bs=gmlp_bs[i],
            pool_w=pool_w[i], pool_scale=pool_scale[i], rel_bias=attn_rel_bias[i],
            w_branch=w_branch[i].astype(BF16), w_out=w_out[i].astype(BF16),
            norm_ffn=norm_ffn[i], peer_wq=peer_wq[i].astype(BF16),
            peer_subkeys=peer_subkeys[i].reshape(2 * PEER_HEADS, PEER_NKEYS, PEER_HALF).astype(BF16),
            peer_table=pack_expert_tables(tab_u, tab_v),
            norm_ple=norm_ple[i], ple_gate=ple_gate[i].astype(BF16), ple_proj=ple_proj[i].astype(BF16),
            norm_final=norm_final,
        )
        final = i == DEPTH - 1
        pk, pv, pps = [], [], []
        for g, (a, bg) in enumerate(zip(starts, sizes)):
            zero_prefix = jnp.zeros((bg, POOL_STATE, BW), F32)
            x_in, x_row0 = (x_all, a * lp) if i == 0 else (xg[g], 0)
            xg[g], mid_p, ps_p, _, after, prev_sc = _trunk_layer(x_in, ple_all, lw, bg, lp, zero_prefix, 0, None, final,
                                                                 after, prev_sc, x_row0, (i * bp + a) * lp)
            stages.append(after)
            mid_p = mid_p.reshape(bg, lp, 4 * BW)
            pk.append(mid_p[:, lp - n_keep:, 2 * BW:3 * BW].reshape(bg, n_keep, HEADS, HEAD_DIM))
            pv.append(mid_p[:, lp - n_keep:, 3 * BW:].reshape(bg, n_keep, HEADS, HEAD_DIM))
            pps.append(ps_p)
        outs["pk"].append(jnp.concatenate(pk, axis=0))
        outs["pv"].append(jnp.concatenate(pv, axis=0))
        outs["pps"].append(jnp.concatenate(pps, axis=0))
        xs, mid_s, ps_s, vn_s, _, _ = _trunk_layer(xs, p_sample[i].reshape(bs * ls, PLE_DIM), lw, bs, ls, state_pool[i],
                                                PAST_LEN, (cache_attn_k[i], cache_attn_v[i]), final)
        mid_s = mid_s.reshape(bs, ls, 4 * BW)
        outs["sk"].append(mid_s[:, :, 2 * BW:3 * BW].reshape(bs, ls, HEADS, HEAD_DIM))
        outs["sv"].append(mid_s[:, :, 3 * BW:].reshape(bs, ls, HEADS, HEAD_DIM))
        outs["sps"].append(ps_s)
        outs["sgv"].append(vn_s.reshape(bs, ls, BW))
    st = lambda k: jnp.stack(outs[k])
    y_prompt = jnp.concatenate(xg, axis=0).reshape(bp, lp, D_MODEL)
    return (y_prompt, xs.reshape(bs, ls, D_MODEL), st("pk"), st("pv"), st("pps"),
            st("sk"), st("sv"), st("sps"), st("sgv"))
```

```python
import functools
import math

import jax
import jax.numpy as jnp
import numpy as np
from jax import lax
from jax.experimental import pallas as pl
from jax.experimental.pallas import tpu as pltpu
from jax.experimental.pallas import tpu_sc as plsc

F32 = jnp.float32
BF16 = jnp.bfloat16

D_MODEL = 1024
DEPTH = 2
CHUNK = 64
EPS = 1e-6
BW = D_MODEL // 2
GMLP_CHUNK = 128
GROUPS = 4
GDIM = BW // GROUPS
POOL_WINDOWS = (2, 4, 8, 16)
POOL_STATE = 15
POOL_PAD = 16
HEADS = 8
HEAD_DIM = BW // HEADS
BAND_CHUNKS = 8
BAND_PAST = BAND_CHUNKS * CHUNK
REL_CLIP = 128
PAST_LEN = 4096
PEER_HEADS = 8
PEER_NKEYS = 128
PEER_HALF = 128
PEER_TOPK = 16
PEER_SLOTS = PEER_HEADS * PEER_TOPK
PLE_DIM = 256

LANES = 128
VMEM_LIMIT = 56 * 1024 * 1024
NEG = -1e30


def _cparams(*sem):
    return pltpu.CompilerParams(dimension_semantics=sem, vmem_limit_bytes=VMEM_LIMIT)


def _rms(x, g):
    ms = jnp.mean(x * x, axis=-1, keepdims=True)
    return x * lax.rsqrt(ms + EPS) * g


def _gelu(x):
    c = math.sqrt(2.0 / math.pi)
    return 0.5 * x * (1.0 + jnp.tanh(c * (x + 0.044715 * (x * x * x))))


def _sigmoid(x):
    return 1.0 / (1.0 + jnp.exp(-x))


IN_GELU_COLS = 2 * BW
IN_PLAIN_COLS = 4 * BW
IN_GATE_COLS = 3 * D_MODEL


def _in_proj_kernel(x_ref, g_ref, w_ref, uv_ref, mid_ref, gate_ref, h_ref):
    j = pl.program_id(1)

    @pl.when(j == 0)
    def _():
        h_ref[...] = _rms(x_ref[...], g_ref[...]).astype(BF16)

    z = jnp.dot(h_ref[...], w_ref[...], preferred_element_type=F32)
    n_gelu = IN_GELU_COLS // D_MODEL
    n_plain = IN_PLAIN_COLS // D_MODEL

    @pl.when(j < n_gelu)
    def _():
        uv_ref[...] = _gelu(z)

    @pl.when(jnp.logical_and(j >= n_gelu, j < n_gelu + n_plain))
    def _():
        mid_ref[...] = z

    @pl.when(j >= n_gelu + n_plain)
    def _():
        gate_ref[...] = _sigmoid(z).astype(gate_ref.dtype)


def in_proj(x, g, w, row0=0, rows=None):
    d = x.shape[1]
    t = x.shape[0] if rows is None else rows
    tm = min(t, 1024)
    assert row0 % tm == 0
    first = row0 // tm
    tn = D_MODEL
    n_gelu, n_plain, n_gate = IN_GELU_COLS // tn, IN_PLAIN_COLS // tn, IN_GATE_COLS // tn
    assert w.shape[1] == IN_GELU_COLS + IN_PLAIN_COLS + IN_GATE_COLS
    return pl.pallas_call(
        _in_proj_kernel,
        grid=(t // tm, n_gelu + n_plain + n_gate),
        in_specs=[
            pl.BlockSpec((tm, d), lambda i, j: (first + i, 0)),
            pl.BlockSpec((1, d), lambda i, j: (0, 0)),
            pl.BlockSpec((d, tn), lambda i, j: (0, j)),
        ],
        out_specs=[
            pl.BlockSpec((tm, tn), lambda i, j: (i, jnp.clip(j, 0, n_gelu - 1))),
            pl.BlockSpec((tm, tn), lambda i, j: (i, jnp.clip(j - n_gelu, 0, n_plain - 1))),
            pl.BlockSpec((tm, tn), lambda i, j: (i, jnp.clip(j - n_gelu - n_plain, 0, n_gate - 1))),
        ],
        out_shape=[
            jax.ShapeDtypeStruct((t, IN_GELU_COLS), F32),
            jax.ShapeDtypeStruct((t, IN_PLAIN_COLS), F32),
            jax.ShapeDtypeStruct((t, IN_GATE_COLS), BF16),
        ],
        scratch_shapes=[pltpu.VMEM((tm, d), BF16)],
        compiler_params=_cparams("parallel", "arbitrary"),
        name="in_proj",
    )(x, g.reshape(1, d), w)


def _gmlp_kernel(uv_ref, lng_ref, lnb_ref, ws_ref, bst_ref, y_ref, vn_ref, *, lc):
    u = uv_ref[:, :BW]
    v = uv_ref[:, BW:]
    mu = jnp.mean(v, axis=-1, keepdims=True)
    vc = v - mu
    var = jnp.mean(vc * vc, axis=-1, keepdims=True)
    vn = vc * lax.rsqrt(var + EPS) * lng_ref[...] + lnb_ref[...]
    vn_ref[...] = vn
    row = lax.broadcasted_iota(jnp.int32, (lc, lc), 0) // CHUNK
    col = lax.broadcasted_iota(jnp.int32, (lc, lc), 1) // CHUNK
    causal = col <= row
    vnb = vn.astype(BF16)
    for g in range(GROUPS):
        w = jnp.where(causal, ws_ref[g], 0.0).astype(BF16)
        s = jnp.dot(w, vnb[:, g * GDIM:(g + 1) * GDIM], preferred_element_type=F32)
        s = s + bst_ref[:, g:g + 1]
        y_ref[:, g * GDIM:(g + 1) * GDIM] = (u[:, g * GDIM:(g + 1) * GDIM] * s).astype(y_ref.dtype)


def gmlp_mix(uv, ln_g, ln_b, ws, bs, lc):
    t = uv.shape[0]
    return pl.pallas_call(
        functools.partial(_gmlp_kernel, lc=lc),
        grid=(t // lc,),
        in_specs=[
            pl.BlockSpec((lc, 2 * BW), lambda i: (i, 0)),
            pl.BlockSpec((1, BW), lambda i: (0, 0)),
            pl.BlockSpec((1, BW), lambda i: (0, 0)),
            pl.BlockSpec((GROUPS, lc, lc), lambda i: (0, 0, 0)),
            pl.BlockSpec((lc, GROUPS), lambda i: (0, 0)),
        ],
        out_specs=[
            pl.BlockSpec((lc, BW), lambda i: (i, 0)),
            pl.BlockSpec((lc, BW), lambda i: (i, 0)),
        ],
        out_shape=[
            jax.ShapeDtypeStruct((t, BW), BF16),
            jax.ShapeDtypeStruct((t, BW), F32),
        ],
        compiler_params=_cparams("parallel"),
        name="gmlp_mix",
    )(uv, ln_g.reshape(1, BW), ln_b.reshape(1, BW), ws[:, :lc, :lc], bs[:, :lc].T)


def _pool_kernel(u_ref, pre_ref, w_ref, sc_ref, y_ref, st_ref, pad_ref, *, seq, start_pos):
    pad_ref[0:POOL_PAD, :] = pre_ref[...]
    pad_ref[POOL_PAD:, :] = u_ref[...]
    pos = lax.broadcasted_iota(jnp.int32, (seq, 1), 0) + start_pos
    for g, win in enumerate(POOL_WINDOWS):
        cols = slice(g * GDIM, (g + 1) * GDIM)
        tok = pad_ref[POOL_PAD:, cols]
        acc = tok
        for k in range(1, win):
            acc = acc + pad_ref[POOL_PAD - k:POOL_PAD - k + seq, cols]
        cnt = jnp.minimum(pos + 1, win).astype(F32)
        d = acc / cnt - tok
        y = jnp.dot(d.astype(BF16), w_ref[g], preferred_element_type=F32)
        y_ref[:, cols] = (y * sc_ref[:, cols]).astype(y_ref.dtype)
    st_ref[...] = pad_ref[seq + 1:seq + POOL_PAD, :]


def pool_mix(mid, prefix, start_pos, pool_w, pool_scale, batch, seq):
    mid3 = mid.reshape(batch, seq, 4 * BW)
    pre = jnp.concatenate([jnp.zeros((batch, 1, BW), F32), prefix], axis=1)
    y, st = pl.pallas_call(
        functools.partial(_pool_kernel, seq=seq, start_pos=start_pos),
        grid=(batch,),
        in_specs=[
            pl.BlockSpec((None, seq, BW), lambda b: (b, 0, 0)),
            pl.BlockSpec((None, POOL_PAD, BW), lambda b: (b, 0, 0)),
            pl.BlockSpec((GROUPS, GDIM, GDIM), lambda b: (0, 0, 0)),
            pl.BlockSpec((1, BW), lambda b: (0, 0)),
        ],
        out_specs=[
            pl.BlockSpec((None, seq, BW), lambda b: (b, 0, 0)),
            pl.BlockSpec((None, POOL_STATE, BW), lambda b: (b, 0, 0)),
        ],
        out_shape=[
            jax.ShapeDtypeStruct((batch, seq, BW), BF16),
            jax.ShapeDtypeStruct((batch, POOL_STATE, BW), F32),
        ],
        scratch_shapes=[pltpu.VMEM((seq + POOL_PAD, BW), F32)],
        compiler_params=_cparams("parallel"),
        name="pool_mix",
    )(mid3, pre, pool_w.astype(BF16), pool_scale.reshape(1, BW))
    return y.reshape(batch * seq, BW), st


def _attn_chunks(q_ref, kcat_ref, vcat_ref, bias_ref, o_ref, *, n_chunks, cq, band, first_block):
    scale = HEAD_DIM ** -0.5
    heads_per_tile = LANES // HEAD_DIM
    lane_head = lax.broadcasted_iota(jnp.int32, (1, LANES), 1) // HEAD_DIM
    for tile in range(HEADS // heads_per_tile):
        cols = slice(tile * LANES, (tile + 1) * LANES)
        for i in range(n_chunks):
            rows = slice(i * cq, (i + 1) * cq)
            q = (q_ref[rows, cols] * scale).astype(BF16)
            k = kcat_ref[i * cq:i * cq + band, cols]
            v = vcat_ref[i * cq:i * cq + band, cols]
            out = None
            for j in range(heads_per_tile):
                own = lane_head == j
                s = lax.dot_general(jnp.where(own, q, jnp.zeros_like(q)), k, (((1,), (1,)), ((), ())),
                                    preferred_element_type=F32) + bias_ref[tile * heads_per_tile + j]
                if first_block is not None:
                    key = lax.broadcasted_iota(jnp.int32, (1, band), 1)
                    s = jnp.where(key >= first_block * (BAND_PAST - i * cq), s, NEG)
                m = jnp.max(s, axis=-1, keepdims=True)
                p = jnp.exp(s - m)
                l = jnp.sum(p, axis=-1, keepdims=True)
                o = jnp.dot(p.astype(BF16), v, preferred_element_type=F32) / l
                out = o if out is None else jnp.where(own, o, out)
            o_ref[rows, cols] = out.astype(o_ref.dtype)


def _attn_prompt_kernel(q_ref, kp_ref, ko_ref, vp_ref, vo_ref, bias_ref, o_ref, kcat_ref, vcat_ref):
    kcat_ref[0:BAND_PAST, :] = kp_ref[...].astype(BF16)
    kcat_ref[BAND_PAST:, :] = ko_ref[...].astype(BF16)
    vcat_ref[0:BAND_PAST, :] = vp_ref[...].astype(BF16)
    vcat_ref[BAND_PAST:, :] = vo_ref[...].astype(BF16)
    _attn_chunks(q_ref, kcat_ref, vcat_ref, bias_ref, o_ref, n_chunks=BAND_CHUNKS, cq=CHUNK,
                 band=BAND_PAST + CHUNK, first_block=(pl.program_id(1) == 0).astype(jnp.int32))


def _rel_bias_tile(rel_bias, qpos, kpos):
    lq, lk = len(qpos), len(kpos)
    assert (np.diff(qpos) == 1).all() and (np.diff(kpos) == 1).all()
    lr = lq + lk - 1
    dist = (qpos[0] - kpos[0]) - (np.arange(lr) - (lq - 1))
    diag = rel_bias[:, np.clip(dist, -REL_CLIP, REL_CLIP) + REL_CLIP]
    skewed = jnp.tile(diag, (1, lq))[:, lq - 1:lq - 1 + lq * (lr - 1)].reshape(-1, lq, lr - 1)
    return skewed[:, :, :lk]


def attn_prompt(mid, rel_bias, batch, seq):
    mid3 = mid.reshape(batch, seq, 4 * BW)
    blk = BAND_PAST
    bias = _rel_bias_tile(rel_bias, np.arange(CHUNK), np.arange(BAND_PAST + CHUNK) - BAND_PAST)
    prev = lambda b, j: jnp.maximum(j - 1, 0)
    y = pl.pallas_call(
        _attn_prompt_kernel,
        grid=(batch, seq // blk),
        in_specs=[
            pl.BlockSpec((None, blk, BW), lambda b, j: (b, j, 1)),
            pl.BlockSpec((None, blk, BW), lambda b, j: (b, prev(b, j), 2)),
            pl.BlockSpec((None, blk, BW), lambda b, j: (b, j, 2)),
            pl.BlockSpec((None, blk, BW), lambda b, j: (b, prev(b, j), 3)),
            pl.BlockSpec((None, blk, BW), lambda b, j: (b, j, 3)),
            pl.BlockSpec((HEADS, CHUNK, BAND_PAST + CHUNK), lambda b, j: (0, 0, 0)),
        ],
        out_specs=pl.BlockSpec((None, blk, BW), lambda b, j: (b, j, 0)),
        out_shape=jax.ShapeDtypeStruct((batch, seq, BW), BF16),
        scratch_shapes=[pltpu.VMEM((2 * blk, BW), BF16), pltpu.VMEM((2 * blk, BW), BF16)],
        compiler_params=_cparams("parallel", "parallel"),
        name="attn_prompt",
    )(mid3, mid3, mid3, mid3, mid3, bias)
    return y.reshape(batch * seq, BW)


def _attn_sample_kernel(q_ref, kc_ref, kn_ref, vc_ref, vn_ref, bias_ref, o_ref, kcat_ref, vcat_ref, *, n_cache):
    kcat_ref[0:n_cache, :] = kc_ref[...].astype(BF16)
    kcat_ref[n_cache:, :] = kn_ref[...].astype(BF16)
    vcat_ref[0:n_cache, :] = vc_ref[...].astype(BF16)
    vcat_ref[n_cache:, :] = vn_ref[...].astype(BF16)
    seq = q_ref.shape[0]
    _attn_chunks(q_ref, kcat_ref, vcat_ref, bias_ref, o_ref, n_chunks=1, cq=seq, band=n_cache + seq,
                 first_block=None)


def attn_sample(mid, cache_k, cache_v, rel_bias, batch, seq):
    n_cache = cache_k.shape[1]
    assert PAST_LEN >= n_cache
    mid3 = mid.reshape(batch, seq, 4 * BW)
    ck = cache_k.reshape(batch, n_cache, BW)
    cv = cache_v.reshape(batch, n_cache, BW)
    bias = _rel_bias_tile(rel_bias, PAST_LEN + np.arange(seq), PAST_LEN - n_cache + np.arange(n_cache + seq))
    y = pl.pallas_call(
        functools.partial(_attn_sample_kernel, n_cache=n_cache),
        grid=(batch,),
        in_specs=[
            pl.BlockSpec((None, seq, BW), lambda b: (b, 0, 1)),
            pl.BlockSpec((None, n_cache, BW), lambda b: (b, 0, 0)),
            pl.BlockSpec((None, seq, BW), lambda b: (b, 0, 2)),
            pl.BlockSpec((None, n_cache, BW), lambda b: (b, 0, 0)),
            pl.BlockSpec((None, seq, BW), lambda b: (b, 0, 3)),
            pl.BlockSpec((HEADS, seq, n_cache + seq), lambda b: (0, 0, 0)),
        ],
        out_specs=pl.BlockSpec((None, seq, BW), lambda b: (b, 0, 0)),
        out_shape=jax.ShapeDtypeStruct((batch, seq, BW), BF16),
        scratch_shapes=[pltpu.VMEM((n_cache + seq, BW), BF16), pltpu.VMEM((n_cache + seq, BW), BF16)],
        compiler_params=_cparams("parallel"),
        name="attn_sample",
    )(mid3, ck, mid3, cv, mid3, bias)
    return y.reshape(batch * seq, BW)


def _mixer_out_kernel(x_ref, ya_ref, yb_ref, yc_ref, gate_ref, wb_ref, wo_ref, o_ref):
    acc = None
    for n, y_ref in enumerate((ya_ref, yb_ref, yc_ref)):
        proj = jnp.dot(y_ref[...], wb_ref[n], preferred_element_type=F32)
        term = gate_ref[:, n * D_MODEL:(n + 1) * D_MODEL].astype(F32) * proj
        acc = term if acc is None else acc + term
    o_ref[...] = x_ref[...] + jnp.dot(acc.astype(BF16), wo_ref[...], preferred_element_type=F32)


def mixer_out(x, ya, yb, yc, gates, w_branch, w_out, row0=0):
    t = ya.shape[0]
    tm = min(t, 512)
    assert row0 % tm == 0
    first = row0 // tm
    row = lambda i: (i, 0)
    return pl.pallas_call(
        _mixer_out_kernel,
        grid=(t // tm,),
        in_specs=[
            pl.BlockSpec((tm, D_MODEL), lambda i: (first + i, 0)),
            pl.BlockSpec((tm, BW), row),
            pl.BlockSpec((tm, BW), row),
            pl.BlockSpec((tm, BW), row),
            pl.BlockSpec((tm, 3 * D_MODEL), row),
            pl.BlockSpec((3, BW, D_MODEL), lambda i: (0, 0, 0)),
            pl.BlockSpec((D_MODEL, D_MODEL), lambda i: (0, 0)),
        ],
        out_specs=pl.BlockSpec((tm, D_MODEL), row),
        out_shape=jax.ShapeDtypeStruct((t, D_MODEL), F32),
        compiler_params=_cparams("parallel"),
        name="mixer_out",
    )(x, ya, yb, yc, gates, w_branch, w_out)


def _extract_top(s, payload, k):
    r = float(s.shape[0])
    rows = lax.broadcasted_iota(jnp.int32, s.shape, 0).astype(F32)
    vals, pays = [], []
    for _ in range(k):
        m = jnp.max(s, axis=0, keepdims=True)
        idx = jnp.min(jnp.where(s == m, rows, r), axis=0, keepdims=True)
        sel = rows == idx
        vals.append(m)
        pays.append(idx if payload is None else jnp.max(jnp.where(sel, payload, -1.0), axis=0, keepdims=True))
        s = jnp.where(sel, -jnp.inf, s)
    return jnp.concatenate(vals, axis=0), jnp.concatenate(pays, axis=0)


def _pair_candidates(sv, si):
    k = PEER_TOPK
    sub = 8
    assert k == 2 * sub
    b_row = lax.broadcasted_iota(jnp.int32, (sub, LANES), 0)
    vals = [sv[0][0:1] + sv[1], sv[0][1:2] + sv[1][0:sub]]
    ids = [si[0][0:1] * PEER_NKEYS + si[1], si[0][1:2] * PEER_NKEYS + si[1][0:sub]]
    for a in range(2, sub):
        keep = b_row < k // (a + 1)
        vals.append(jnp.where(keep, sv[0][a:a + 1] + sv[1][0:sub], -jnp.inf))
        ids.append(si[0][a:a + 1] * PEER_NKEYS + si[1][0:sub])
    vals.append(sv[0][sub:k] + sv[1][0:1])
    ids.append(si[0][sub:k] * PEER_NKEYS + si[1][0:1])
    return jnp.concatenate(vals, axis=0), jnp.concatenate(ids, axis=0)


def _peer_query_kernel(x_ref, g_ref, wq_ref, sk_ref, hp_ref, idx_ref, gate_ref, q_ref, *, tm):
    hb = _rms(x_ref[...], g_ref[...]).astype(BF16)
    q_ref[...] = jnp.dot(hb, wq_ref[...], preferred_element_type=F32).astype(BF16)
    bits = lax.bitcast_convert_type(hb.astype(F32), jnp.int32)
    half = D_MODEL // 2
    hp_ref[...] = bits[:, half:] | lax.shift_right_logical(bits[:, :half], 16)

    def sub_block(sb, carry):
        tok = pl.ds(pl.multiple_of(sb * LANES, LANES), LANES)
        for hd in range(PEER_HEADS):
            sv, si = [], []
            for p in range(2):
                hp = hd * 2 + p
                q = q_ref[tok, hp * PEER_HALF:(hp + 1) * PEER_HALF]
                s = lax.dot_general(sk_ref[hp], q, (((1,), (1,)), ((), ())), preferred_element_type=F32)
                v, i = _extract_top(s, None, PEER_TOPK)
                sv.append(v)
                si.append(i)
            cand, eid = _pair_candidates(sv, si)
            tv, te = _extract_top(cand, eid, PEER_TOPK)
            e = jnp.exp(tv - tv[0:1])
            gate = e / jnp.sum(e, axis=0, keepdims=True)
            idx_ref[sb, hd * PEER_TOPK:(hd + 1) * PEER_TOPK, :] = te.astype(jnp.int32)
            gate_ref[sb, hd * PEER_TOPK:(hd + 1) * PEER_TOPK, :] = gate
        return carry

    lax.fori_loop(0, tm // LANES, sub_block, 0)


def peer_query(x, norm_g, wq, subkeys):
    t = x.shape[0]
    tm = min(t, 512)
    nq = wq.shape[1]
    nsb = tm // LANES
    return pl.pallas_call(
        functools.partial(_peer_query_kernel, tm=tm),
        grid=(t // tm,),
        in_specs=[
            pl.BlockSpec((tm, D_MODEL), lambda i: (i, 0)),
            pl.BlockSpec((1, D_MODEL), lambda i: (0, 0)),
            pl.BlockSpec((D_MODEL, nq), lambda i: (0, 0)),
            pl.BlockSpec((2 * PEER_HEADS, PEER_NKEYS, PEER_HALF), lambda i: (0, 0, 0)),
        ],
        out_specs=[
            pl.BlockSpec((tm, D_MODEL // 2), lambda i: (i, 0)),
            pl.BlockSpec((nsb, PEER_SLOTS, LANES), lambda i: (i, 0, 0)),
            pl.BlockSpec((nsb, PEER_SLOTS, LANES), lambda i: (i, 0, 0)),
        ],
        out_shape=[
            jax.ShapeDtypeStruct((t, D_MODEL // 2), jnp.int32),
            jax.ShapeDtypeStruct((t // LANES, PEER_SLOTS, LANES), jnp.int32),
            jax.ShapeDtypeStruct((t // LANES, PEER_SLOTS, LANES), F32),
        ],
        scratch_shapes=[pltpu.VMEM((tm, nq), BF16)],
        compiler_params=_cparams("parallel"),
        name="peer_query",
    )(x, norm_g.reshape(1, D_MODEL), wq, subkeys)


HALF_D = D_MODEL // 2
SC_LANES = 16
SC_UNIT_ROWS = 32
SC_UNITS_PER_TOKEN = PEER_SLOTS // SC_UNIT_ROWS
SC_MAX_TOKEN_BLOCK = 8
SC_BUFFERS = 3
SC_ROW_GROUP = 8
SC_HALF_VECS = HALF_D // SC_LANES
SC_OUT_VECS = 8
SC_BF16_TERMS = 4
SC_ROW_TILES = D_MODEL // LANES


PACK_ROWS = 256


def _pack_kernel(u_ref, v_ref, o_ref):
    def words(x):
        bits = lax.bitcast_convert_type(x.astype(BF16).astype(F32), jnp.int32)
        return bits[:, HALF_D:] | lax.shift_right_logical(bits[:, :HALF_D], 16)

    half_tiles = SC_ROW_TILES // 2
    for base, w in ((0, words(u_ref[...])), (half_tiles, words(v_ref[...]))):
        for j in range(half_tiles):
            o_ref[:, base + j, :] = w[:, j * LANES:(j + 1) * LANES]


def pack_expert_tables(peer_u, peer_v, layer):
    e = peer_u.shape[1]
    return pl.pallas_call(
        _pack_kernel,
        grid=(e // PACK_ROWS,),
        in_specs=[pl.BlockSpec((None, PACK_ROWS, D_MODEL), lambda i: (layer, i, 0))] * 2,
        out_specs=pl.BlockSpec((PACK_ROWS, SC_ROW_TILES, LANES), lambda i: (i, 0, 0)),
        out_shape=jax.ShapeDtypeStruct((e, SC_ROW_TILES, LANES), jnp.int32),
        compiler_params=_cparams("parallel"),
        name="pack_expert_tables",
    )(peer_u, peer_v)


def sc_peer_experts(table, idx, gates, hp):
    t = hp.shape[0]
    info = plsc.get_sparse_core_info()
    n_workers = info.num_cores * info.num_subcores
    tpw = t // n_workers
    token_block = min(tpw, SC_MAX_TOKEN_BLOCK)
    assert t % (n_workers * token_block) == 0
    units = token_block * SC_UNITS_PER_TOKEN
    mesh = plsc.VectorSubcoreMesh(core_axis_name="core", subcore_axis_name="subcore")
    hi_mask = jnp.int32(-65536)
    gelu_c = math.sqrt(2.0 / math.pi)

    @functools.partial(
        pl.kernel,
        out_type=jax.ShapeDtypeStruct((t, D_MODEL), F32),
        mesh=mesh,
        scratch_types=[
            pltpu.VMEM((2, token_block * PEER_SLOTS), jnp.int32),
            pltpu.VMEM((2, token_block * PEER_SLOTS), F32),
            pltpu.VMEM((2, token_block, HALF_D), jnp.int32),
            pltpu.VMEM((token_block, D_MODEL), F32),
            pltpu.VMEM((SC_BUFFERS, SC_UNIT_ROWS, SC_ROW_TILES, LANES), jnp.int32),
            pltpu.VMEM((SC_UNIT_ROWS, SC_LANES), F32),
            pltpu.VMEM((SC_UNIT_ROWS,), jnp.int32),
            pltpu.SemaphoreType.DMA((SC_BUFFERS,)),
            pltpu.SemaphoreType.DMA((2,)),
        ],
        compiler_params=pltpu.CompilerParams(needs_layout_passes=False),
        name="peer_sc_experts",
    )
    def kern(table_hbm, idx_hbm, gate_hbm, h_hbm, out_hbm, idx_v, gate_v, h_v, out_v, rows_v, part_v, coef_v, sem,
             in_sem):
        wid = lax.axis_index("subcore") * info.num_cores + lax.axis_index("core")
        lane = lax.iota(jnp.int32, SC_LANES)
        zero = jnp.zeros((SC_LANES,), F32)

        def gather(unit, b, s):
            rows = idx_v.at[s, pl.ds(unit * SC_UNIT_ROWS, SC_UNIT_ROWS)]
            return pltpu.make_async_copy(table_hbm.at[rows], rows_v.at[b], sem.at[b])

        def row_vec(b, r, vec):
            per_tile_row = LANES // SC_LANES
            lane0 = pl.multiple_of((vec % per_tile_row) * SC_LANES, SC_LANES)
            return rows_v[b, r, vec // per_tile_row, pl.ds(lane0, SC_LANES)]

        def unpack(w):
            return lax.bitcast_convert_type(w << 16, F32), lax.bitcast_convert_type(w & hi_mask, F32)

        def as_pairs(w):
            return plsc.bitcast(w, BF16)

        def compute(unit, b, s):
            tl = unit // SC_UNITS_PER_TOKEN
            q = unit % SC_UNITS_PER_TOKEN

            def row_group(rg, carry):
                def kstep(k, accs):
                    hs = []
                    for j in range(SC_BF16_TERMS):
                        off = pl.multiple_of((k * SC_BF16_TERMS + j) * SC_LANES, SC_LANES)
                        hs.append(as_pairs(h_v[s, tl, pl.ds(off, SC_LANES)]))
                    new = []
                    for r in range(SC_ROW_GROUP):
                        p = None
                        for j in range(SC_BF16_TERMS):
                            term = as_pairs(row_vec(b, rg * SC_ROW_GROUP + r, k * SC_BF16_TERMS + j)) * hs[j]
                            p = term if p is None else p + term
                        lo, hi = unpack(plsc.bitcast(p, jnp.int32))
                        new.append(accs[r] + lo + hi)
                    return tuple(new)

                accs = lax.fori_loop(0, SC_HALF_VECS // SC_BF16_TERMS, kstep, (zero,) * SC_ROW_GROUP)
                for r in range(SC_ROW_GROUP):
                    part_v[rg * SC_ROW_GROUP + r, :] = accs[r]
                return carry

            lax.fori_loop(0, SC_UNIT_ROWS // SC_ROW_GROUP, row_group, 0)

            dots = []
            for i in range(SC_UNIT_ROWS // SC_LANES):
                rows = lane + i * SC_LANES
                terms = [plsc.load_gather(part_v, [rows, jnp.full((SC_LANES,), l, jnp.int32)])
                         for l in range(SC_LANES)]
                while len(terms) > 1:
                    terms = [terms[j] + terms[j + 1] for j in range(0, len(terms), 2)]
                dots.append(terms[0])
            for i, a in enumerate(dots):
                z = gelu_c * (a + 0.044715 * (a * a * a))
                act = a / (1.0 + jnp.exp(-2.0 * z))
                slot = pl.multiple_of(tl * PEER_SLOTS + q * SC_UNIT_ROWS + i * SC_LANES, SC_LANES)
                bits = lax.bitcast_convert_type(gate_v[s, pl.ds(slot, SC_LANES)] * act, jnp.int32)
                top = (bits + 0x7FFF + ((bits >> 16) & 1)) & hi_mask
                coef_v[pl.ds(i * SC_LANES, SC_LANES)] = top | lax.shift_right_logical(top, 16)

            def out_pass(dq, carry):
                def row_quad(rq, accs):
                    per_vec = SC_LANES // SC_BF16_TERMS
                    cvec = coef_v[pl.ds(pl.multiple_of((rq // per_vec) * SC_LANES, SC_LANES), SC_LANES)]
                    cs = []
                    for j in range(SC_BF16_TERMS):
                        src = jnp.full((SC_LANES,), (rq % per_vec) * SC_BF16_TERMS + j, jnp.int32)
                        cs.append(as_pairs(cvec.at[src].get(mode="promise_in_bounds")))
                    new = []
                    for k in range(SC_OUT_VECS):
                        p = None
                        for j in range(SC_BF16_TERMS):
                            w = row_vec(b, rq * SC_BF16_TERMS + j, SC_HALF_VECS + dq * SC_OUT_VECS + k)
                            term = as_pairs(w) * cs[j]
                            p = term if p is None else p + term
                        lo, hi = unpack(plsc.bitcast(p, jnp.int32))
                        new.append(accs[2 * k] + lo)
                        new.append(accs[2 * k + 1] + hi)
                    return tuple(new)

                accs = lax.fori_loop(0, SC_UNIT_ROWS // SC_BF16_TERMS, row_quad, (zero,) * (2 * SC_OUT_VECS))
                for k in range(SC_OUT_VECS):
                    off = pl.multiple_of((dq * SC_OUT_VECS + k) * SC_LANES, SC_LANES)
                    plsc.addupdate(out_v.at[tl, pl.ds(off, SC_LANES)], accs[2 * k])
                    plsc.addupdate(out_v.at[tl, pl.ds(HALF_D + off, SC_LANES)], accs[2 * k + 1])
                return carry

            lax.fori_loop(0, SC_HALF_VECS // SC_OUT_VECS, out_pass, 0)

        n_blocks = tpw // token_block

        def block_inputs(blk, s):
            tok0 = wid * tpw + blk * token_block
            slots = pl.ds(tok0 * PEER_SLOTS, token_block * PEER_SLOTS)
            return (pltpu.make_async_copy(idx_hbm.at[slots], idx_v.at[s], in_sem.at[s]),
                    pltpu.make_async_copy(gate_hbm.at[slots], gate_v.at[s], in_sem.at[s]),
                    pltpu.make_async_copy(h_hbm.at[pl.ds(tok0, token_block)], h_v.at[s], in_sem.at[s]))

        def first_gathers(s):
            for u in range(SC_BUFFERS - 1):
                gather(u, u, s).start()

        for c in block_inputs(0, 0):
            c.start()
        for c in block_inputs(0, 0):
            c.wait()
        first_gathers(0)

        @pl.loop(0, n_blocks)
        def _(blk):
            s = blk % 2
            more = blk + 1 < n_blocks

            @pl.when(more)
            def _():
                for c in block_inputs(blk + 1, 1 - s):
                    c.start()

            @pl.loop(0, token_block)
            def _(tl):
                @pl.loop(0, D_MODEL // SC_LANES)
                def _(k):
                    out_v[tl, pl.ds(pl.multiple_of(k * SC_LANES, SC_LANES), SC_LANES)] = zero

            @pl.loop(0, units)
            def _(unit):
                ahead = unit + SC_BUFFERS - 1

                @pl.when(ahead < units)
                def _():
                    gather(ahead, ahead % SC_BUFFERS, s).start()

                b = unit % SC_BUFFERS
                gather(unit, b, s).wait()
                compute(unit, b, s)

            @pl.when(more)
            def _():
                for c in block_inputs(blk + 1, 1 - s):
                    c.wait()
                first_gathers(1 - s)

            tok0 = wid * tpw + blk * token_block
            pltpu.sync_copy(out_v, out_hbm.at[pl.ds(tok0, token_block)])

    return kern(table, idx, gates, hp)


TC_PEER_TOKENS = 8
TC_PEER_BUFFERS = 4
U_TILE_ROWS = SC_ROW_TILES // 2


def _tc_peer_kernel(idx_ref, idx_next_ref, gate_ref, hp_ref, table_hbm, o_ref, rows_buf, a_buf, c_buf, sem):
    step = pl.program_id(0)
    hi_mask = jnp.int32(-65536)

    def unpack(w):
        return lax.bitcast_convert_type(w << 16, F32), lax.bitcast_convert_type(w & hi_mask, F32)

    def row_copy(ids_ref, tok, e, slot):
        return pltpu.make_async_copy(table_hbm.at[ids_ref[tok * PEER_SLOTS + e]], rows_buf.at[slot, e], sem.at[slot])

    def wait_rows(slot):
        pltpu.make_async_copy(table_hbm.at[pl.ds(0, PEER_SLOTS)], rows_buf.at[slot], sem.at[slot]).wait()

    ahead = TC_PEER_BUFFERS - 1

    @pl.when(step == 0)
    def _():
        for tok in range(ahead):
            def first(e, c, tok=tok):
                row_copy(idx_ref, tok, e, tok).start()
                return c

            lax.fori_loop(0, PEER_SLOTS, first, 0, unroll=8)

    gates_t = gate_ref[...].T
    zeros_v = jnp.zeros((SC_ROW_TILES - U_TILE_ROWS, LANES), F32)
    for t in range(TC_PEER_TOKENS):
        slot = t % TC_PEER_BUFFERS
        nxt = t + ahead
        nxt_slot = nxt % TC_PEER_BUFFERS
        wait_rows(slot)
        h_lo, h_hi = unpack(hp_ref[t])
        h_lo = jnp.concatenate([h_lo, zeros_v], axis=0)
        h_hi = jnp.concatenate([h_hi, zeros_v], axis=0)

        def u_body(e, c, slot=slot, nxt=nxt, nxt_slot=nxt_slot, h_lo=h_lo, h_hi=h_hi):
            if nxt >= TC_PEER_TOKENS:
                @pl.when(step + 1 < pl.num_programs(0))
                def _():
                    row_copy(idx_next_ref, nxt - TC_PEER_TOKENS, e, nxt_slot).start()
            else:
                row_copy(idx_ref, nxt, e, nxt_slot).start()
            lo, hi = unpack(rows_buf[slot, e])
            a_buf[pl.ds(e, 1), :] = jnp.sum(lo * h_lo + hi * h_hi, axis=0, keepdims=True)
            return c

        lax.fori_loop(0, PEER_SLOTS, u_body, 0, unroll=8)
        a = jnp.sum(a_buf[...], axis=1, keepdims=True)
        c_buf[...] = jnp.broadcast_to(gates_t[:, t:t + 1] * _gelu(a), (PEER_SLOTS, LANES))

        def v_body(e, acc, slot=slot):
            lo, hi = unpack(rows_buf[slot, e])
            c = c_buf[pl.ds(e, 1), :]
            return acc[0] + c * lo, acc[1] + c * hi

        zero = jnp.zeros((SC_ROW_TILES, LANES), F32)
        acc_lo, acc_hi = lax.fori_loop(0, PEER_SLOTS, v_body, (zero, zero), unroll=8)
        o_ref[t, 0:U_TILE_ROWS, :] = acc_lo[U_TILE_ROWS:, :]
        o_ref[t, U_TILE_ROWS:, :] = acc_hi[U_TILE_ROWS:, :]


def tc_peer_experts(table, idx, gates, hp):
    t = hp.shape[0]
    assert t % TC_PEER_TOKENS == 0 and TC_PEER_TOKENS % TC_PEER_BUFFERS == 0
    n_steps = t // TC_PEER_TOKENS
    ids = TC_PEER_TOKENS * PEER_SLOTS
    smem = functools.partial(pl.BlockSpec, memory_space=pltpu.SMEM)
    out = pl.pallas_call(
        _tc_peer_kernel,
        grid=(n_steps,),
        in_specs=[
            smem((ids,), lambda i: (i,)),
            smem((ids,), lambda i: (jnp.minimum(i + 1, n_steps - 1),)),
            pl.BlockSpec((TC_PEER_TOKENS, PEER_SLOTS), lambda i: (i, 0)),
            pl.BlockSpec((TC_PEER_TOKENS, U_TILE_ROWS, LANES), lambda i: (i, 0, 0)),
            pl.BlockSpec(memory_space=pl.ANY),
        ],
        out_specs=pl.BlockSpec((TC_PEER_TOKENS, SC_ROW_TILES, LANES), lambda i: (i, 0, 0)),
        out_shape=jax.ShapeDtypeStruct((t, SC_ROW_TILES, LANES), F32),
        scratch_shapes=[
            pltpu.VMEM((TC_PEER_BUFFERS, PEER_SLOTS, SC_ROW_TILES, LANES), jnp.int32),
            pltpu.VMEM((PEER_SLOTS, LANES), F32),
            pltpu.VMEM((PEER_SLOTS, LANES), F32),
            pltpu.SemaphoreType.DMA((TC_PEER_BUFFERS,)),
        ],
        compiler_params=pltpu.CompilerParams(dimension_semantics=("arbitrary",), vmem_limit_bytes=VMEM_LIMIT,
                                             disable_bounds_checks=True),
        name="tc_peer_experts",
    )(idx, idx, gates, hp.reshape(t, U_TILE_ROWS, LANES), table)
    return out.reshape(t, D_MODEL)


TC_PEER_SHARE = 8
TC_PEER_QUANTUM = 1024


def peer_block(x, norm_g, wq, subkeys, table, prev_sc=None):
    t = x.shape[0]
    hp, idx3, gate3 = peer_query(x, norm_g, wq, subkeys)
    idx = jnp.transpose(idx3, (0, 2, 1)).reshape(t * PEER_SLOTS)
    gates = jnp.transpose(gate3, (0, 2, 1)).reshape(t * PEER_SLOTS)
    t_tc = (t // TC_PEER_SHARE) // TC_PEER_QUANTUM * TC_PEER_QUANTUM
    t_sc = t - t_tc
    idx_sc = idx[:t_sc * PEER_SLOTS]
    if prev_sc is not None:
        idx_sc, _ = lax.optimization_barrier((idx_sc, prev_sc))
    on_sc = sc_peer_experts(table, idx_sc, gates[:t_sc * PEER_SLOTS], hp[:t_sc])
    if t_tc == 0:
        return on_sc, (hp,), on_sc
    on_tc = tc_peer_experts(table, idx[t_sc * PEER_SLOTS:], gates[t_sc * PEER_SLOTS:].reshape(t_tc, PEER_SLOTS),
                            hp[t_sc:])
    return jnp.concatenate([on_sc, on_tc], axis=0), (hp, on_tc), on_sc


def _ple_kernel(x_ref, ffn_ref, p_ref, g_ref, wg_ref, wp_ref, gf_ref, o_ref, *, final):
    x = x_ref[...] + ffn_ref[...]
    gate = _sigmoid(jnp.dot(_rms(x, g_ref[...]).astype(BF16), wg_ref[...], preferred_element_type=F32))
    emb = jnp.dot(p_ref[...].astype(BF16), wp_ref[...], preferred_element_type=F32)
    y = x + gate * emb
    o_ref[...] = _rms(y, gf_ref[...]) if final else y


def ple_block(x, ffn, p, norm_g, w_gate, w_proj, norm_final, final, p_row0=0):
    t = x.shape[0]
    tm = min(t, 1024)
    assert p_row0 % tm == 0
    p_first = p_row0 // tm
    row = lambda i: (i, 0)
    full = lambda i: (0, 0)
    return pl.pallas_call(
        functools.partial(_ple_kernel, final=final),
        grid=(t // tm,),
        in_specs=[
            pl.BlockSpec((tm, D_MODEL), row),
            pl.BlockSpec((tm, D_MODEL), row),
            pl.BlockSpec((tm, PLE_DIM), lambda i: (p_first + i, 0)),
            pl.BlockSpec((1, D_MODEL), full),
            pl.BlockSpec((D_MODEL, D_MODEL), full),
            pl.BlockSpec((PLE_DIM, D_MODEL), full),
            pl.BlockSpec((1, D_MODEL), full),
        ],
        out_specs=pl.BlockSpec((tm, D_MODEL), row),
        out_shape=jax.ShapeDtypeStruct((t, D_MODEL), F32),
        compiler_params=_cparams("parallel"),
        name="ple_block",
    )(x, ffn, p, norm_g.reshape(1, D_MODEL), w_gate, w_proj, norm_final.reshape(1, D_MODEL))


def _prompt_groups(batch):
    sizes = []
    while sum(sizes) < batch:
        nxt = 1 if len(sizes) < 2 else -(-sizes[-1] * 7 // 5)
        sizes.append(min(nxt, batch - sum(sizes)))
    return sizes
def _trunk_layer(x, ple, lw, batch, seq, pool_prefix, pool_start, cache, final, after=None, prev_sc=None,
                 x_row0=0, ple_row0=0):
    if after is not None:
        x, _ = lax.optimization_barrier((x, after))
    uv, mid, gates = in_proj(x, lw["norm_mix"], lw["w_in"], x_row0, batch * seq)
    ya, vn = gmlp_mix(uv, lw["gmlp_ln_g"], lw["gmlp_ln_b"], lw["gmlp_ws"], lw["gmlp_bs"], min(seq, GMLP_CHUNK))
    yb, pool_state = pool_mix(mid, pool_prefix, pool_start, lw["pool_w"], lw["pool_scale"], batch, seq)
    if cache is None:
        yc = attn_prompt(mid, lw["rel_bias"], batch, seq)
    else:
        yc = attn_sample(mid, cache[0], cache[1], lw["rel_bias"], batch, seq)
    x = mixer_out(x, ya, yb, yc, gates, lw["w_branch"], lw["w_out"], x_row0)
    ffn, stage, sc_out = peer_block(x, lw["norm_ffn"], lw["peer_wq"], lw["peer_subkeys"], lw["peer_table"], prev_sc)
    x = ple_block(x, ffn, ple, lw["norm_ple"], lw["ple_gate"], lw["ple_proj"], lw["norm_final"], final, ple_row0)
    return x, mid, pool_state, vn, stage, sc_out


def kernel(x_prompt, x_sample, cache_attn_k, cache_attn_v, state_pool, p_prompt, p_sample, norm_mix, w_in, gmlp_ln_g, gmlp_ln_b, gmlp_ws, gmlp_bs, pool_w, pool_scale, attn_rel_bias, w_branch, w_out, norm_ffn, peer_wq, peer_subkeys, peer_u, peer_v, norm_ple, ple_gate, ple_proj, norm_final):
    bp, lp, _ = x_prompt.shape
    bs, ls, _ = x_sample.shape
    assert lp % BAND_PAST == 0 and lp % GMLP_CHUNK == 0 and ls <= CHUNK
    n_keep = min(BAND_PAST, lp)
    sizes = _prompt_groups(bp)
    starts = [sum(sizes[:g]) for g in range(len(sizes))]
    x_all = x_prompt.reshape(bp * lp, D_MODEL)
    ple_all = p_prompt.reshape(DEPTH * bp * lp, PLE_DIM)
    xg = [None] * len(sizes)
    xs = x_sample.reshape(bs * ls, D_MODEL)
    outs = {k: [] for k in ("pk", "pv", "pps", "sk", "sv", "sps", "sgv")}
    after = prev_sc = None
    stages = []
    for i in range(DEPTH):
        tab_u, tab_v = peer_u, peer_v
        if stages:
            tab_u, tab_v, _ = lax.optimization_barrier((tab_u, tab_v, stages[len(stages) // 2]))
        stages = []
        lw = dict(
            norm_mix=norm_mix[i],
            w_in=w_in[i].astype(BF16),
            gmlp_ln_g=gmlp_ln_g[i], gmlp_ln_b=gmlp_ln_b[i], gmlp_ws=gmlp_ws[i], gmlp_bs=gmlp_bs[i],
            pool_w=pool_w[i], pool_scale=pool_scale[i], rel_bias=attn_rel_bias[i],
            w_branch=w_branch[i].astype(BF16), w_out=w_out[i].astype(BF16),
            norm_ffn=norm_ffn[i], peer_wq=peer_wq[i].astype(BF16),
            peer_subkeys=peer_subkeys[i].reshape(2 * PEER_HEADS, PEER_NKEYS, PEER_HALF).astype(BF16),
            peer_table=pack_expert_tables(tab_u, tab_v, i),
            norm_ple=norm_ple[i], ple_gate=ple_gate[i].astype(BF16), ple_proj=ple_proj[i].astype(BF16),
            norm_final=norm_final,
        )
        final = i == DEPTH - 1
        pk, pv, pps = [], [], []
        for g, (a, bg) in enumerate(zip(starts, sizes)):
            zero_prefix = jnp.zeros((bg, POOL_STATE, BW), F32)
            x_in, x_row0 = (x_all, a * lp) if i == 0 else (xg[g], 0)
            xg[g], mid_p, ps_p, _, after, prev_sc = _trunk_layer(x_in, ple_all, lw, bg, lp, zero_prefix, 0, None, final,
                                                                 after, prev_sc, x_row0, (i * bp + a) * lp)
            stages.append(after)
            mid_p = mid_p.reshape(bg, lp, 4 * BW)
            pk.append(mid_p[:, lp - n_keep:, 2 * BW:3 * BW].reshape(bg, n_keep, HEADS, HEAD_DIM))
            pv.append(mid_p[:, lp - n_keep:, 3 * BW:].reshape(bg, n_keep, HEADS, HEAD_DIM))
            pps.append(ps_p)
        outs["pk"].append(jnp.concatenate(pk, axis=0))
        outs["pv"].append(jnp.concatenate(pv, axis=0))
        outs["pps"].append(jnp.concatenate(pps, axis=0))
        xs, mid_s, ps_s, vn_s, _, _ = _trunk_layer(xs, p_sample[i].reshape(bs * ls, PLE_DIM), lw, bs, ls, state_pool[i],
                                                PAST_LEN, (cache_attn_k[i], cache_attn_v[i]), final)
        mid_s = mid_s.reshape(bs, ls, 4 * BW)
        outs["sk"].append(mid_s[:, :, 2 * BW:3 * BW].reshape(bs, ls, HEADS, HEAD_DIM))
        outs["sv"].append(mid_s[:, :, 3 * BW:].reshape(bs, ls, HEADS, HEAD_DIM))
        outs["sps"].append(ps_s)
        outs["sgv"].append(vn_s.reshape(bs, ls, BW))
    st = lambda k: jnp.stack(outs[k])
    y_prompt = jnp.concatenate(xg, axis=0).reshape(bp, lp, D_MODEL)
    return (y_prompt, xs.reshape(bs, ls, D_MODEL), st("pk"), st("pv"), st("pps"),
            st("sk"), st("sv"), st("sps"), st("sgv"))
```

```python
import functools
import math

import jax
import jax.numpy as jnp
import numpy as np
from jax import lax
from jax.experimental import pallas as pl
from jax.experimental.pallas import tpu as pltpu
from jax.experimental.pallas import tpu_sc as plsc

F32 = jnp.float32
BF16 = jnp.bfloat16

D_MODEL = 1024
DEPTH = 2
CHUNK = 64
EPS = 1e-6
BW = D_MODEL // 2
GMLP_CHUNK = 128
GROUPS = 4
GDIM = BW // GROUPS
POOL_WINDOWS = (2, 4, 8, 16)
POOL_STATE = 15
POOL_PAD = 16
HEADS = 8
HEAD_DIM = BW // HEADS
BAND_CHUNKS = 8
BAND_PAST = BAND_CHUNKS * CHUNK
REL_CLIP = 128
PAST_LEN = 4096
PEER_HEADS = 8
PEER_NKEYS = 128
PEER_HALF = 128
PEER_TOPK = 16
PEER_SLOTS = PEER_HEADS * PEER_TOPK
PLE_DIM = 256

LANES = 128
VMEM_LIMIT = 56 * 1024 * 1024
NEG = -1e30


def _cparams(*sem):
    return pltpu.CompilerParams(dimension_semantics=sem, vmem_limit_bytes=VMEM_LIMIT)


def _rms(x, g):
    ms = jnp.mean(x * x, axis=-1, keepdims=True)
    return x * lax.rsqrt(ms + EPS) * g


def _gelu(x):
    c = math.sqrt(2.0 / math.pi)
    return 0.5 * x * (1.0 + jnp.tanh(c * (x + 0.044715 * (x * x * x))))


def _sigmoid(x):
    return 1.0 / (1.0 + jnp.exp(-x))


IN_GELU_COLS = 2 * BW
IN_PLAIN_COLS = 4 * BW
IN_GATE_COLS = 3 * D_MODEL


def _in_proj_kernel(x_ref, g_ref, w_ref, uv_ref, mid_ref, gate_ref, h_ref):
    j = pl.program_id(1)

    @pl.when(j == 0)
    def _():
        h_ref[...] = _rms(x_ref[...], g_ref[...]).astype(BF16)

    z = jnp.dot(h_ref[...], w_ref[...], preferred_element_type=F32)
    n_gelu = IN_GELU_COLS // D_MODEL
    n_plain = IN_PLAIN_COLS // D_MODEL

    @pl.when(j < n_gelu)
    def _():
        uv_ref[...] = _gelu(z)

    @pl.when(jnp.logical_and(j >= n_gelu, j < n_gelu + n_plain))
    def _():
        mid_ref[...] = z

    @pl.when(j >= n_gelu + n_plain)
    def _():
        gate_ref[...] = _sigmoid(z).astype(gate_ref.dtype)


def in_proj(x, g, w, row0=0, rows=None):
    d = x.shape[1]
    t = x.shape[0] if rows is None else rows
    tm = min(t, 1024)
    assert row0 % tm == 0
    first = row0 // tm
    tn = D_MODEL
    n_gelu, n_plain, n_gate = IN_GELU_COLS // tn, IN_PLAIN_COLS // tn, IN_GATE_COLS // tn
    assert w.shape[1] == IN_GELU_COLS + IN_PLAIN_COLS + IN_GATE_COLS
    return pl.pallas_call(
        _in_proj_kernel,
        grid=(t // tm, n_gelu + n_plain + n_gate),
        in_specs=[
            pl.BlockSpec((tm, d), lambda i, j: (first + i, 0)),
            pl.BlockSpec((1, d), lambda i, j: (0, 0)),
            pl.BlockSpec((d, tn), lambda i, j: (0, j)),
        ],
        out_specs=[
            pl.BlockSpec((tm, tn), lambda i, j: (i, jnp.clip(j, 0, n_gelu - 1))),
            pl.BlockSpec((tm, tn), lambda i, j: (i, jnp.clip(j - n_gelu, 0, n_plain - 1))),
            pl.BlockSpec((tm, tn), lambda i, j: (i, jnp.clip(j - n_gelu - n_plain, 0, n_gate - 1))),
        ],
        out_shape=[
            jax.ShapeDtypeStruct((t, IN_GELU_COLS), F32),
            jax.ShapeDtypeStruct((t, IN_PLAIN_COLS), F32),
            jax.ShapeDtypeStruct((t, IN_GATE_COLS), BF16),
        ],
        scratch_shapes=[pltpu.VMEM((tm, d), BF16)],
        compiler_params=_cparams("parallel", "arbitrary"),
        name="in_proj",
    )(x, g.reshape(1, d), w)


def _gmlp_kernel(uv_ref, lng_ref, lnb_ref, ws_ref, bst_ref, y_ref, vn_ref, *, lc):
    u = uv_ref[:, :BW]
    v = uv_ref[:, BW:]
    mu = jnp.mean(v, axis=-1, keepdims=True)
    vc = v - mu
    var = jnp.mean(vc * vc, axis=-1, keepdims=True)
    vn = vc * lax.rsqrt(var + EPS) * lng_ref[...] + lnb_ref[...]
    vn_ref[...] = vn
    row = lax.broadcasted_iota(jnp.int32, (lc, lc), 0) // CHUNK
    col = lax.broadcasted_iota(jnp.int32, (lc, lc), 1) // CHUNK
    causal = col <= row
    vnb = vn.astype(BF16)
    for g in range(GROUPS):
        w = jnp.where(causal, ws_ref[g], 0.0).astype(BF16)
        s = jnp.dot(w, vnb[:, g * GDIM:(g + 1) * GDIM], preferred_element_type=F32)
        s = s + bst_ref[:, g:g + 1]
        y_ref[:, g * GDIM:(g + 1) * GDIM] = (u[:, g * GDIM:(g + 1) * GDIM] * s).astype(y_ref.dtype)


def gmlp_mix(uv, ln_g, ln_b, ws, bs, lc):
    t = uv.shape[0]
    return pl.pallas_call(
        functools.partial(_gmlp_kernel, lc=lc),
        grid=(t // lc,),
        in_specs=[
            pl.BlockSpec((lc, 2 * BW), lambda i: (i, 0)),
            pl.BlockSpec((1, BW), lambda i: (0, 0)),
            pl.BlockSpec((1, BW), lambda i: (0, 0)),
            pl.BlockSpec((GROUPS, lc, lc), lambda i: (0, 0, 0)),
            pl.BlockSpec((lc, GROUPS), lambda i: (0, 0)),
        ],
        out_specs=[
            pl.BlockSpec((lc, BW), lambda i: (i, 0)),
            pl.BlockSpec((lc, BW), lambda i: (i, 0)),
        ],
        out_shape=[
            jax.ShapeDtypeStruct((t, BW), BF16),
            jax.ShapeDtypeStruct((t, BW), F32),
        ],
        compiler_params=_cparams("parallel"),
        name="gmlp_mix",
    )(uv, ln_g.reshape(1, BW), ln_b.reshape(1, BW), ws[:, :lc, :lc], bs[:, :lc].T)


def _pool_kernel(u_ref, pre_ref, w_ref, sc_ref, y_ref, st_ref, pad_ref, *, seq, start_pos):
    pad_ref[0:POOL_PAD, :] = pre_ref[...]
    pad_ref[POOL_PAD:, :] = u_ref[...]
    pos = lax.broadcasted_iota(jnp.int32, (seq, 1), 0) + start_pos
    for g, win in enumerate(POOL_WINDOWS):
        cols = slice(g * GDIM, (g + 1) * GDIM)
        tok = pad_ref[POOL_PAD:, cols]
        acc = tok
        for k in range(1, win):
            acc = acc + pad_ref[POOL_PAD - k:POOL_PAD - k + seq, cols]
        cnt = jnp.minimum(pos + 1, win).astype(F32)
        d = acc / cnt - tok
        y = jnp.dot(d.astype(BF16), w_ref[g], preferred_element_type=F32)
        y_ref[:, cols] = (y * sc_ref[:, cols]).astype(y_ref.dtype)
    st_ref[...] = pad_ref[seq + 1:seq + POOL_PAD, :]


def pool_mix(mid, prefix, start_pos, pool_w, pool_scale, batch, seq):
    mid3 = mid.reshape(batch, seq, 4 * BW)
    pre = jnp.concatenate([jnp.zeros((batch, 1, BW), F32), prefix], axis=1)
    y, st = pl.pallas_call(
        functools.partial(_pool_kernel, seq=seq, start_pos=start_pos),
        grid=(batch,),
        in_specs=[
            pl.BlockSpec((None, seq, BW), lambda b: (b, 0, 0)),
            pl.BlockSpec((None, POOL_PAD, BW), lambda b: (b, 0, 0)),
            pl.BlockSpec((GROUPS, GDIM, GDIM), lambda b: (0, 0, 0)),
            pl.BlockSpec((1, BW), lambda b: (0, 0)),
        ],
        out_specs=[
            pl.BlockSpec((None, seq, BW), lambda b: (b, 0, 0)),
            pl.BlockSpec((None, POOL_STATE, BW), lambda b: (b, 0, 0)),
        ],
        out_shape=[
            jax.ShapeDtypeStruct((batch, seq, BW), BF16),
            jax.ShapeDtypeStruct((batch, POOL_STATE, BW), F32),
        ],
        scratch_shapes=[pltpu.VMEM((seq + POOL_PAD, BW), F32)],
        compiler_params=_cparams("parallel"),
        name="pool_mix",
    )(mid3, pre, pool_w.astype(BF16), pool_scale.reshape(1, BW))
    return y.reshape(batch * seq, BW), st


def _attn_chunks(q_ref, kcat_ref, vcat_ref, bias_ref, o_ref, *, n_chunks, cq, band, first_block):
    scale = HEAD_DIM ** -0.5
    heads_per_tile = LANES // HEAD_DIM
    lane_head = lax.broadcasted_iota(jnp.int32, (1, LANES), 1) // HEAD_DIM
    for tile in range(HEADS // heads_per_tile):
        cols = slice(tile * LANES, (tile + 1) * LANES)
        for i in range(n_chunks):
            rows = slice(i * cq, (i + 1) * cq)
            q = (q_ref[rows, cols] * scale).astype(BF16)
            k = kcat_ref[i * cq:i * cq + band, cols]
            v = vcat_ref[i * cq:i * cq + band, cols]
            out = None
            for j in range(heads_per_tile):
                own = lane_head == j
                s = lax.dot_general(jnp.where(own, q, jnp.zeros_like(q)), k, (((1,), (1,)), ((), ())),
                                    preferred_element_type=F32) + bias_ref[tile * heads_per_tile + j]
                if first_block is not None:
                    key = lax.broadcasted_iota(jnp.int32, (1, band), 1)
                    s = jnp.where(key >= first_block * (BAND_PAST - i * cq), s, NEG)
                m = jnp.max(s, axis=-1, keepdims=True)
                p = jnp.exp(s - m)
                l = jnp.sum(p, axis=-1, keepdims=True)
                o = jnp.dot(p.astype(BF16), v, preferred_element_type=F32) / l
                out = o if out is None else jnp.where(own, o, out)
            o_ref[rows, cols] = out.astype(o_ref.dtype)


def _attn_prompt_kernel(q_ref, kp_ref, ko_ref, vp_ref, vo_ref, bias_ref, o_ref, kcat_ref, vcat_ref):
    kcat_ref[0:BAND_PAST, :] = kp_ref[...].astype(BF16)
    kcat_ref[BAND_PAST:, :] = ko_ref[...].astype(BF16)
    vcat_ref[0:BAND_PAST, :] = vp_ref[...].astype(BF16)
    vcat_ref[BAND_PAST:, :] = vo_ref[...].astype(BF16)
    _attn_chunks(q_ref, kcat_ref, vcat_ref, bias_ref, o_ref, n_chunks=BAND_CHUNKS, cq=CHUNK,
                 band=BAND_PAST + CHUNK, first_block=(pl.program_id(1) == 0).astype(jnp.int32))


def _rel_bias_tile(rel_bias, qpos, kpos):
    lq, lk = len(qpos), len(kpos)
    assert (np.diff(qpos) == 1).all() and (np.diff(kpos) == 1).all()
    lr = lq + lk - 1
    dist = (qpos[0] - kpos[0]) - (np.arange(lr) - (lq - 1))
    diag = rel_bias[:, np.clip(dist, -REL_CLIP, REL_CLIP) + REL_CLIP]
    skewed = jnp.tile(diag, (1, lq))[:, lq - 1:lq - 1 + lq * (lr - 1)].reshape(-1, lq, lr - 1)
    return skewed[:, :, :lk]


def attn_prompt(mid, rel_bias, batch, seq):
    mid3 = mid.reshape(batch, seq, 4 * BW)
    blk = BAND_PAST
    bias = _rel_bias_tile(rel_bias, np.arange(CHUNK), np.arange(BAND_PAST + CHUNK) - BAND_PAST)
    prev = lambda b, j: jnp.maximum(j - 1, 0)
    y = pl.pallas_call(
        _attn_prompt_kernel,
        grid=(batch, seq // blk),
        in_specs=[
            pl.BlockSpec((None, blk, BW), lambda b, j: (b, j, 1)),
            pl.BlockSpec((None, blk, BW), lambda b, j: (b, prev(b, j), 2)),
            pl.BlockSpec((None, blk, BW), lambda b, j: (b, j, 2)),
            pl.BlockSpec((None, blk, BW), lambda b, j: (b, prev(b, j), 3)),
            pl.BlockSpec((None, blk, BW), lambda b, j: (b, j, 3)),
            pl.BlockSpec((HEADS, CHUNK, BAND_PAST + CHUNK), lambda b, j: (0, 0, 0)),
        ],
        out_specs=pl.BlockSpec((None, blk, BW), lambda b, j: (b, j, 0)),
        out_shape=jax.ShapeDtypeStruct((batch, seq, BW), BF16),
        scratch_shapes=[pltpu.VMEM((2 * blk, BW), BF16), pltpu.VMEM((2 * blk, BW), BF16)],
        compiler_params=_cparams("parallel", "parallel"),
        name="attn_prompt",
    )(mid3, mid3, mid3, mid3, mid3, bias)
    return y.reshape(batch * seq, BW)


def _attn_sample_kernel(q_ref, kc_ref, kn_ref, vc_ref, vn_ref, bias_ref, o_ref, kcat_ref, vcat_ref, *, n_cache):
    kcat_ref[0:n_cache, :] = kc_ref[...].astype(BF16)
    kcat_ref[n_cache:, :] = kn_ref[...].astype(BF16)
    vcat_ref[0:n_cache, :] = vc_ref[...].astype(BF16)
    vcat_ref[n_cache:, :] = vn_ref[...].astype(BF16)
    seq = q_ref.shape[0]
    _attn_chunks(q_ref, kcat_ref, vcat_ref, bias_ref, o_ref, n_chunks=1, cq=seq, band=n_cache + seq,
                 first_block=None)


def attn_sample(mid, cache_k, cache_v, layer, rel_bias, batch, seq):
    n_cache = cache_k.shape[2]
    assert PAST_LEN >= n_cache
    mid3 = mid.reshape(batch, seq, 4 * BW)
    ck = cache_k.reshape(-1, n_cache, BW)
    cv = cache_v.reshape(-1, n_cache, BW)
    cached = lambda b: (layer * batch + b, 0, 0)
    bias = _rel_bias_tile(rel_bias, PAST_LEN + np.arange(seq), PAST_LEN - n_cache + np.arange(n_cache + seq))
    y = pl.pallas_call(
        functools.partial(_attn_sample_kernel, n_cache=n_cache),
        grid=(batch,),
        in_specs=[
            pl.BlockSpec((None, seq, BW), lambda b: (b, 0, 1)),
            pl.BlockSpec((None, n_cache, BW), cached),
            pl.BlockSpec((None, seq, BW), lambda b: (b, 0, 2)),
            pl.BlockSpec((None, n_cache, BW), cached),
            pl.BlockSpec((None, seq, BW), lambda b: (b, 0, 3)),
            pl.BlockSpec((HEADS, seq, n_cache + seq), lambda b: (0, 0, 0)),
        ],
        out_specs=pl.BlockSpec((None, seq, BW), lambda b: (b, 0, 0)),
        out_shape=jax.ShapeDtypeStruct((batch, seq, BW), BF16),
        scratch_shapes=[pltpu.VMEM((n_cache + seq, BW), BF16), pltpu.VMEM((n_cache + seq, BW), BF16)],
        compiler_params=_cparams("parallel"),
        name="attn_sample",
    )(mid3, ck, mid3, cv, mid3, bias)
    return y.reshape(batch * seq, BW)


def _mixer_out_kernel(x_ref, ya_ref, yb_ref, yc_ref, gate_ref, wb_ref, wo_ref, o_ref):
    acc = None
    for n, y_ref in enumerate((ya_ref, yb_ref, yc_ref)):
        proj = jnp.dot(y_ref[...], wb_ref[n], preferred_element_type=F32)
        term = gate_ref[:, n * D_MODEL:(n + 1) * D_MODEL].astype(F32) * proj
        acc = term if acc is None else acc + term
    o_ref[...] = x_ref[...] + jnp.dot(acc.astype(BF16), wo_ref[...], preferred_element_type=F32)


def mixer_out(x, ya, yb, yc, gates, w_branch, w_out, row0=0):
    t = ya.shape[0]
    tm = min(t, 512)
    assert row0 % tm == 0
    first = row0 // tm
    row = lambda i: (i, 0)
    return pl.pallas_call(
        _mixer_out_kernel,
        grid=(t // tm,),
        in_specs=[
            pl.BlockSpec((tm, D_MODEL), lambda i: (first + i, 0)),
            pl.BlockSpec((tm, BW), row),
            pl.BlockSpec((tm, BW), row),
            pl.BlockSpec((tm, BW), row),
            pl.BlockSpec((tm, 3 * D_MODEL), row),
            pl.BlockSpec((3, BW, D_MODEL), lambda i: (0, 0, 0)),
            pl.BlockSpec((D_MODEL, D_MODEL), lambda i: (0, 0)),
        ],
        out_specs=pl.BlockSpec((tm, D_MODEL), row),
        out_shape=jax.ShapeDtypeStruct((t, D_MODEL), F32),
        compiler_params=_cparams("parallel"),
        name="mixer_out",
    )(x, ya, yb, yc, gates, w_branch, w_out)


def _extract_top(s, payload, k):
    r = float(s.shape[0])
    rows = lax.broadcasted_iota(jnp.int32, s.shape, 0).astype(F32)
    vals, pays = [], []
    for _ in range(k):
        m = jnp.max(s, axis=0, keepdims=True)
        idx = jnp.min(jnp.where(s == m, rows, r), axis=0, keepdims=True)
        sel = rows == idx
        vals.append(m)
        pays.append(idx if payload is None else jnp.max(jnp.where(sel, payload, -1.0), axis=0, keepdims=True))
        s = jnp.where(sel, -jnp.inf, s)
    return jnp.concatenate(vals, axis=0), jnp.concatenate(pays, axis=0)


def _pair_candidates(sv, si):
    k = PEER_TOPK
    sub = 8
    assert k == 2 * sub
    b_row = lax.broadcasted_iota(jnp.int32, (sub, LANES), 0)
    vals = [sv[0][0:1] + sv[1], sv[0][1:2] + sv[1][0:sub]]
    ids = [si[0][0:1] * PEER_NKEYS + si[1], si[0][1:2] * PEER_NKEYS + si[1][0:sub]]
    for a in range(2, sub):
        keep = b_row < k // (a + 1)
        vals.append(jnp.where(keep, sv[0][a:a + 1] + sv[1][0:sub], -jnp.inf))
        ids.append(si[0][a:a + 1] * PEER_NKEYS + si[1][0:sub])
    vals.append(sv[0][sub:k] + sv[1][0:1])
    ids.append(si[0][sub:k] * PEER_NKEYS + si[1][0:1])
    return jnp.concatenate(vals, axis=0), jnp.concatenate(ids, axis=0)


def _peer_query_kernel(x_ref, g_ref, wq_ref, sk_ref, hp_ref, idx_ref, gate_ref, q_ref, *, tm):
    hb = _rms(x_ref[...], g_ref[...]).astype(BF16)
    q_ref[...] = jnp.dot(hb, wq_ref[...], preferred_element_type=F32).astype(BF16)
    bits = lax.bitcast_convert_type(hb.astype(F32), jnp.int32)
    half = D_MODEL // 2
    hp_ref[...] = bits[:, half:] | lax.shift_right_logical(bits[:, :half], 16)

    def sub_block(sb, carry):
        tok = pl.ds(pl.multiple_of(sb * LANES, LANES), LANES)
        for hd in range(PEER_HEADS):
            sv, si = [], []
            for p in range(2):
                hp = hd * 2 + p
                q = q_ref[tok, hp * PEER_HALF:(hp + 1) * PEER_HALF]
                s = lax.dot_general(sk_ref[hp], q, (((1,), (1,)), ((), ())), preferred_element_type=F32)
                v, i = _extract_top(s, None, PEER_TOPK)
                sv.append(v)
                si.append(i)
            cand, eid = _pair_candidates(sv, si)
            tv, te = _extract_top(cand, eid, PEER_TOPK)
            e = jnp.exp(tv - tv[0:1])
            gate = e / jnp.sum(e, axis=0, keepdims=True)
            idx_ref[sb, hd * PEER_TOPK:(hd + 1) * PEER_TOPK, :] = te.astype(jnp.int32)
            gate_ref[sb, hd * PEER_TOPK:(hd + 1) * PEER_TOPK, :] = gate
        return carry

    lax.fori_loop(0, tm // LANES, sub_block, 0)


def peer_query(x, norm_g, wq, subkeys):
    t = x.shape[0]
    tm = min(t, 512)
    nq = wq.shape[1]
    nsb = tm // LANES
    return pl.pallas_call(
        functools.partial(_peer_query_kernel, tm=tm),
        grid=(t // tm,),
        in_specs=[
            pl.BlockSpec((tm, D_MODEL), lambda i: (i, 0)),
            pl.BlockSpec((1, D_MODEL), lambda i: (0, 0)),
            pl.BlockSpec((D_MODEL, nq), lambda i: (0, 0)),
            pl.BlockSpec((2 * PEER_HEADS, PEER_NKEYS, PEER_HALF), lambda i: (0, 0, 0)),
        ],
        out_specs=[
            pl.BlockSpec((tm, D_MODEL // 2), lambda i: (i, 0)),
            pl.BlockSpec((nsb, PEER_SLOTS, LANES), lambda i: (i, 0, 0)),
            pl.BlockSpec((nsb, PEER_SLOTS, LANES), lambda i: (i, 0, 0)),
        ],
        out_shape=[
            jax.ShapeDtypeStruct((t, D_MODEL // 2), jnp.int32),
            jax.ShapeDtypeStruct((t // LANES, PEER_SLOTS, LANES), jnp.int32),
            jax.ShapeDtypeStruct((t // LANES, PEER_SLOTS, LANES), F32),
        ],
        scratch_shapes=[pltpu.VMEM((tm, nq), BF16)],
        compiler_params=_cparams("parallel"),
        name="peer_query",
    )(x, norm_g.reshape(1, D_MODEL), wq, subkeys)


HALF_D = D_MODEL // 2
SC_LANES = 16
SC_UNIT_ROWS = 32
SC_UNITS_PER_TOKEN = PEER_SLOTS // SC_UNIT_ROWS
SC_MAX_TOKEN_BLOCK = 8
SC_BUFFERS = 3
SC_ROW_GROUP = 8
SC_HALF_VECS = HALF_D // SC_LANES
SC_OUT_VECS = 8
SC_BF16_TERMS = 4
SC_ROW_TILES = D_MODEL // LANES


PACK_ROWS = 256


def _pack_kernel(u_ref, v_ref, o_ref):
    def words(x):
        bits = lax.bitcast_convert_type(x.astype(BF16).astype(F32), jnp.int32)
        return bits[:, HALF_D:] | lax.shift_right_logical(bits[:, :HALF_D], 16)

    half_tiles = SC_ROW_TILES // 2
    for base, w in ((0, words(u_ref[...])), (half_tiles, words(v_ref[...]))):
        for j in range(half_tiles):
            o_ref[:, base + j, :] = w[:, j * LANES:(j + 1) * LANES]


def pack_expert_tables(peer_u, peer_v, layer):
    e = peer_u.shape[1]
    return pl.pallas_call(
        _pack_kernel,
        grid=(e // PACK_ROWS,),
        in_specs=[pl.BlockSpec((None, PACK_ROWS, D_MODEL), lambda i: (layer, i, 0))] * 2,
        out_specs=pl.BlockSpec((PACK_ROWS, SC_ROW_TILES, LANES), lambda i: (i, 0, 0)),
        out_shape=jax.ShapeDtypeStruct((e, SC_ROW_TILES, LANES), jnp.int32),
        compiler_params=_cparams("parallel"),
        name="pack_expert_tables",
    )(peer_u, peer_v)


def sc_peer_experts(table, idx, gates, hp):
    t = hp.shape[0]
    info = plsc.get_sparse_core_info()
    n_workers = info.num_cores * info.num_subcores
    tpw = t // n_workers
    token_block = min(tpw, SC_MAX_TOKEN_BLOCK)
    assert t % (n_workers * token_block) == 0
    units = token_block * SC_UNITS_PER_TOKEN
    mesh = plsc.VectorSubcoreMesh(core_axis_name="core", subcore_axis_name="subcore")
    hi_mask = jnp.int32(-65536)
    gelu_c = math.sqrt(2.0 / math.pi)

    @functools.partial(
        pl.kernel,
        out_type=jax.ShapeDtypeStruct((t, D_MODEL), F32),
        mesh=mesh,
        scratch_types=[
            pltpu.VMEM((2, token_block * PEER_SLOTS), jnp.int32),
            pltpu.VMEM((2, token_block * PEER_SLOTS), F32),
            pltpu.VMEM((2, token_block, HALF_D), jnp.int32),
            pltpu.VMEM((token_block, D_MODEL), F32),
            pltpu.VMEM((SC_BUFFERS, SC_UNIT_ROWS, SC_ROW_TILES, LANES), jnp.int32),
            pltpu.VMEM((SC_UNIT_ROWS, SC_LANES), F32),
            pltpu.VMEM((SC_UNIT_ROWS,), jnp.int32),
            pltpu.SemaphoreType.DMA((SC_BUFFERS,)),
            pltpu.SemaphoreType.DMA((2,)),
        ],
        compiler_params=pltpu.CompilerParams(needs_layout_passes=False),
        name="peer_sc_experts",
    )
    def kern(table_hbm, idx_hbm, gate_hbm, h_hbm, out_hbm, idx_v, gate_v, h_v, out_v, rows_v, part_v, coef_v, sem,
             in_sem):
        wid = lax.axis_index("subcore") * info.num_cores + lax.axis_index("core")
        lane = lax.iota(jnp.int32, SC_LANES)
        zero = jnp.zeros((SC_LANES,), F32)

        def gather(unit, b, s):
            rows = idx_v.at[s, pl.ds(unit * SC_UNIT_ROWS, SC_UNIT_ROWS)]
            return pltpu.make_async_copy(table_hbm.at[rows], rows_v.at[b], sem.at[b])

        def row_vec(b, r, vec):
            per_tile_row = LANES // SC_LANES
            lane0 = pl.multiple_of((vec % per_tile_row) * SC_LANES, SC_LANES)
            return rows_v[b, r, vec // per_tile_row, pl.ds(lane0, SC_LANES)]

        def unpack(w):
            return lax.bitcast_convert_type(w << 16, F32), lax.bitcast_convert_type(w & hi_mask, F32)

        def as_pairs(w):
            return plsc.bitcast(w, BF16)

        def compute(unit, b, s):
            tl = unit // SC_UNITS_PER_TOKEN
            q = unit % SC_UNITS_PER_TOKEN

            def row_group(rg, carry):
                def kstep(k, accs):
                    hs = []
                    for j in range(SC_BF16_TERMS):
                        off = pl.multiple_of((k * SC_BF16_TERMS + j) * SC_LANES, SC_LANES)
                        hs.append(as_pairs(h_v[s, tl, pl.ds(off, SC_LANES)]))
                    new = []
                    for r in range(SC_ROW_GROUP):
                        p = None
                        for j in range(SC_BF16_TERMS):
                            term = as_pairs(row_vec(b, rg * SC_ROW_GROUP + r, k * SC_BF16_TERMS + j)) * hs[j]
                            p = term if p is None else p + term
                        lo, hi = unpack(plsc.bitcast(p, jnp.int32))
                        new.append(accs[r] + lo + hi)
                    return tuple(new)

                accs = lax.fori_loop(0, SC_HALF_VECS // SC_BF16_TERMS, kstep, (zero,) * SC_ROW_GROUP)
                for r in range(SC_ROW_GROUP):
                    part_v[rg * SC_ROW_GROUP + r, :] = accs[r]
                return carry

            lax.fori_loop(0, SC_UNIT_ROWS // SC_ROW_GROUP, row_group, 0)

            dots = []
            for i in range(SC_UNIT_ROWS // SC_LANES):
                rows = lane + i * SC_LANES
                terms = [plsc.load_gather(part_v, [rows, jnp.full((SC_LANES,), l, jnp.int32)])
                         for l in range(SC_LANES)]
                while len(terms) > 1:
                    terms = [terms[j] + terms[j + 1] for j in range(0, len(terms), 2)]
                dots.append(terms[0])
            for i, a in enumerate(dots):
                z = gelu_c * (a + 0.044715 * (a * a * a))
                act = a / (1.0 + jnp.exp(-2.0 * z))
                slot = pl.multiple_of(tl * PEER_SLOTS + q * SC_UNIT_ROWS + i * SC_LANES, SC_LANES)
                bits = lax.bitcast_convert_type(gate_v[s, pl.ds(slot, SC_LANES)] * act, jnp.int32)
                top = (bits + 0x7FFF + ((bits >> 16) & 1)) & hi_mask
                coef_v[pl.ds(i * SC_LANES, SC_LANES)] = top | lax.shift_right_logical(top, 16)

            def out_pass(dq, carry):
                def row_quad(rq, accs):
                    per_vec = SC_LANES // SC_BF16_TERMS
                    cvec = coef_v[pl.ds(pl.multiple_of((rq // per_vec) * SC_LANES, SC_LANES), SC_LANES)]
                    cs = []
                    for j in range(SC_BF16_TERMS):
                        src = jnp.full((SC_LANES,), (rq % per_vec) * SC_BF16_TERMS + j, jnp.int32)
                        cs.append(as_pairs(cvec.at[src].get(mode="promise_in_bounds")))
                    new = []
                    for k in range(SC_OUT_VECS):
                        p = None
                        for j in range(SC_BF16_TERMS):
                            w = row_vec(b, rq * SC_BF16_TERMS + j, SC_HALF_VECS + dq * SC_OUT_VECS + k)
                            term = as_pairs(w) * cs[j]
                            p = term if p is None else p + term
                        lo, hi = unpack(plsc.bitcast(p, jnp.int32))
                        new.append(accs[2 * k] + lo)
                        new.append(accs[2 * k + 1] + hi)
                    return tuple(new)

                accs = lax.fori_loop(0, SC_UNIT_ROWS // SC_BF16_TERMS, row_quad, (zero,) * (2 * SC_OUT_VECS))
                for k in range(SC_OUT_VECS):
                    off = pl.multiple_of((dq * SC_OUT_VECS + k) * SC_LANES, SC_LANES)
                    plsc.addupdate(out_v.at[tl, pl.ds(off, SC_LANES)], accs[2 * k])
                    plsc.addupdate(out_v.at[tl, pl.ds(HALF_D + off, SC_LANES)], accs[2 * k + 1])
                return carry

            lax.fori_loop(0, SC_HALF_VECS // SC_OUT_VECS, out_pass, 0)

        n_blocks = tpw // token_block

        def block_inputs(blk, s):
            tok0 = wid * tpw + blk * token_block
            slots = pl.ds(tok0 * PEER_SLOTS, token_block * PEER_SLOTS)
            return (pltpu.make_async_copy(idx_hbm.at[slots], idx_v.at[s], in_sem.at[s]),
                    pltpu.make_async_copy(gate_hbm.at[slots], gate_v.at[s], in_sem.at[s]),
                    pltpu.make_async_copy(h_hbm.at[pl.ds(tok0, token_block)], h_v.at[s], in_sem.at[s]))

        def first_gathers(s):
            for u in range(SC_BUFFERS - 1):
                gather(u, u, s).start()

        for c in block_inputs(0, 0):
            c.start()
        for c in block_inputs(0, 0):
            c.wait()
        first_gathers(0)

        @pl.loop(0, n_blocks)
        def _(blk):
            s = blk % 2
            more = blk + 1 < n_blocks

            @pl.when(more)
            def _():
                for c in block_inputs(blk + 1, 1 - s):
                    c.start()

            @pl.loop(0, token_block)
            def _(tl):
                @pl.loop(0, D_MODEL // SC_LANES)
                def _(k):
                    out_v[tl, pl.ds(pl.multiple_of(k * SC_LANES, SC_LANES), SC_LANES)] = zero

            @pl.loop(0, units)
            def _(unit):
                ahead = unit + SC_BUFFERS - 1

                @pl.when(ahead < units)
                def _():
                    gather(ahead, ahead % SC_BUFFERS, s).start()

                b = unit % SC_BUFFERS
                gather(unit, b, s).wait()
                compute(unit, b, s)

            @pl.when(more)
            def _():
                for c in block_inputs(blk + 1, 1 - s):
                    c.wait()
                first_gathers(1 - s)

            tok0 = wid * tpw + blk * token_block
            pltpu.sync_copy(out_v, out_hbm.at[pl.ds(tok0, token_block)])

    return kern(table, idx, gates, hp)


TC_PEER_TOKENS = 8
TC_PEER_BUFFERS = 4
U_TILE_ROWS = SC_ROW_TILES // 2


def _tc_peer_kernel(idx_ref, idx_next_ref, gate_ref, hp_ref, table_hbm, o_ref, rows_buf, a_buf, c_buf, sem):
    step = pl.program_id(0)
    hi_mask = jnp.int32(-65536)

    def unpack(w):
        return lax.bitcast_convert_type(w << 16, F32), lax.bitcast_convert_type(w & hi_mask, F32)

    def row_copy(ids_ref, tok, e, slot):
        return pltpu.make_async_copy(table_hbm.at[ids_ref[tok * PEER_SLOTS + e]], rows_buf.at[slot, e], sem.at[slot])

    def wait_rows(slot):
        pltpu.make_async_copy(table_hbm.at[pl.ds(0, PEER_SLOTS)], rows_buf.at[slot], sem.at[slot]).wait()

    ahead = TC_PEER_BUFFERS - 1

    @pl.when(step == 0)
    def _():
        for tok in range(ahead):
            def first(e, c, tok=tok):
                row_copy(idx_ref, tok, e, tok).start()
                return c

            lax.fori_loop(0, PEER_SLOTS, first, 0, unroll=8)

    gates_t = gate_ref[...].T
    zeros_v = jnp.zeros((SC_ROW_TILES - U_TILE_ROWS, LANES), F32)
    for t in range(TC_PEER_TOKENS):
        slot = t % TC_PEER_BUFFERS
        nxt = t + ahead
        nxt_slot = nxt % TC_PEER_BUFFERS
        wait_rows(slot)
        h_lo, h_hi = unpack(hp_ref[t])
        h_lo = jnp.concatenate([h_lo, zeros_v], axis=0)
        h_hi = jnp.concatenate([h_hi, zeros_v], axis=0)

        def u_body(e, c, slot=slot, nxt=nxt, nxt_slot=nxt_slot, h_lo=h_lo, h_hi=h_hi):
            if nxt >= TC_PEER_TOKENS:
                @pl.when(step + 1 < pl.num_programs(0))
                def _():
                    row_copy(idx_next_ref, nxt - TC_PEER_TOKENS, e, nxt_slot).start()
            else:
                row_copy(idx_ref, nxt, e, nxt_slot).start()
            lo, hi = unpack(rows_buf[slot, e])
            a_buf[pl.ds(e, 1), :] = jnp.sum(lo * h_lo + hi * h_hi, axis=0, keepdims=True)
            return c

        lax.fori_loop(0, PEER_SLOTS, u_body, 0, unroll=8)
        a = jnp.sum(a_buf[...], axis=1, keepdims=True)
        c_buf[...] = jnp.broadcast_to(gates_t[:, t:t + 1] * _gelu(a), (PEER_SLOTS, LANES))

        def v_body(e, acc, slot=slot):
            lo, hi = unpack(rows_buf[slot, e])
            c = c_buf[pl.ds(e, 1), :]
            return acc[0] + c * lo, acc[1] + c * hi

        zero = jnp.zeros((SC_ROW_TILES, LANES), F32)
        acc_lo, acc_hi = lax.fori_loop(0, PEER_SLOTS, v_body, (zero, zero), unroll=8)
        o_ref[t, 0:U_TILE_ROWS, :] = acc_lo[U_TILE_ROWS:, :]
        o_ref[t, U_TILE_ROWS:, :] = acc_hi[U_TILE_ROWS:, :]


def tc_peer_experts(table, idx, gates, hp):
    t = hp.shape[0]
    assert t % TC_PEER_TOKENS == 0 and TC_PEER_TOKENS % TC_PEER_BUFFERS == 0
    n_steps = t // TC_PEER_TOKENS
    ids = TC_PEER_TOKENS * PEER_SLOTS
    smem = functools.partial(pl.BlockSpec, memory_space=pltpu.SMEM)
    out = pl.pallas_call(
        _tc_peer_kernel,
        grid=(n_steps,),
        in_specs=[
            smem((ids,), lambda i: (i,)),
            smem((ids,), lambda i: (jnp.minimum(i + 1, n_steps - 1),)),
            pl.BlockSpec((TC_PEER_TOKENS, PEER_SLOTS), lambda i: (i, 0)),
            pl.BlockSpec((TC_PEER_TOKENS, U_TILE_ROWS, LANES), lambda i: (i, 0, 0)),
            pl.BlockSpec(memory_space=pl.ANY),
        ],
        out_specs=pl.BlockSpec((TC_PEER_TOKENS, SC_ROW_TILES, LANES), lambda i: (i, 0, 0)),
        out_shape=jax.ShapeDtypeStruct((t, SC_ROW_TILES, LANES), F32),
        scratch_shapes=[
            pltpu.VMEM((TC_PEER_BUFFERS, PEER_SLOTS, SC_ROW_TILES, LANES), jnp.int32),
            pltpu.VMEM((PEER_SLOTS, LANES), F32),
            pltpu.VMEM((PEER_SLOTS, LANES), F32),
            pltpu.SemaphoreType.DMA((TC_PEER_BUFFERS,)),
        ],
        compiler_params=pltpu.CompilerParams(dimension_semantics=("arbitrary",), vmem_limit_bytes=VMEM_LIMIT,
                                             disable_bounds_checks=True),
        name="tc_peer_experts",
    )(idx, idx, gates, hp.reshape(t, U_TILE_ROWS, LANES), table)
    return out.reshape(t, D_MODEL)


TC_PEER_SHARE = 8
TC_PEER_QUANTUM = 1024


def peer_block(x, norm_g, wq, subkeys, table, prev_sc=None):
    t = x.shape[0]
    hp, idx3, gate3 = peer_query(x, norm_g, wq, subkeys)
    idx = jnp.transpose(idx3, (0, 2, 1)).reshape(t * PEER_SLOTS)
    gates = jnp.transpose(gate3, (0, 2, 1)).reshape(t * PEER_SLOTS)
    t_tc = (t // TC_PEER_SHARE) // TC_PEER_QUANTUM * TC_PEER_QUANTUM
    t_sc = t - t_tc
    idx_sc = idx[:t_sc * PEER_SLOTS]
    if prev_sc is not None:
        idx_sc, _ = lax.optimization_barrier((idx_sc, prev_sc))
    on_sc = sc_peer_experts(table, idx_sc, gates[:t_sc * PEER_SLOTS], hp[:t_sc])
    if t_tc == 0:
        return on_sc, (hp,), on_sc
    on_tc = tc_peer_experts(table, idx[t_sc * PEER_SLOTS:], gates[t_sc * PEER_SLOTS:].reshape(t_tc, PEER_SLOTS),
                            hp[t_sc:])
    return jnp.concatenate([on_sc, on_tc], axis=0), (hp, on_tc), on_sc


def _ple_kernel(x_ref, ffn_ref, p_ref, g_ref, wg_ref, wp_ref, gf_ref, o_ref, *, final):
    x = x_ref[...] + ffn_ref[...]
    gate = _sigmoid(jnp.dot(_rms(x, g_ref[...]).astype(BF16), wg_ref[...], preferred_element_type=F32))
    emb = jnp.dot(p_ref[...].astype(BF16), wp_ref[...], preferred_element_type=F32)
    y = x + gate * emb
    o_ref[...] = _rms(y, gf_ref[...]) if final else y


def ple_block(x, ffn, p, norm_g, w_gate, w_proj, norm_final, final, p_row0=0):
    t = x.shape[0]
    tm = min(t, 1024)
    assert p_row0 % tm == 0
    p_first = p_row0 // tm
    row = lambda i: (i, 0)
    full = lambda i: (0, 0)
    return pl.pallas_call(
        functools.partial(_ple_kernel, final=final),
        grid=(t // tm,),
        in_specs=[
            pl.BlockSpec((tm, D_MODEL), row),
            pl.BlockSpec((tm, D_MODEL), row),
            pl.BlockSpec((tm, PLE_DIM), lambda i: (p_first + i, 0)),
            pl.BlockSpec((1, D_MODEL), full),
            pl.BlockSpec((D_MODEL, D_MODEL), full),
            pl.BlockSpec((PLE_DIM, D_MODEL), full),
            pl.BlockSpec((1, D_MODEL), full),
        ],
        out_specs=pl.BlockSpec((tm, D_MODEL), row),
        out_shape=jax.ShapeDtypeStruct((t, D_MODEL), F32),
        compiler_params=_cparams("parallel"),
        name="ple_block",
    )(x, ffn, p, norm_g.reshape(1, D_MODEL), w_gate, w_proj, norm_final.reshape(1, D_MODEL))


def _prompt_groups(batch):
    sizes = []
    while sum(sizes) < batch:
        nxt = 1 if len(sizes) < 2 else -(-sizes[-1] * 7 // 5)
        sizes.append(min(nxt, batch - sum(sizes)))
    return sizes
def _trunk_layer(x, ple, lw, batch, seq, pool_prefix, pool_start, cache, final, after=None, prev_sc=None,
                 x_row0=0, ple_row0=0):
    if after is not None:
        x, _ = lax.optimization_barrier((x, after))
    uv, mid, gates = in_proj(x, lw["norm_mix"], lw["w_in"], x_row0, batch * seq)
    ya, vn = gmlp_mix(uv, lw["gmlp_ln_g"], lw["gmlp_ln_b"], lw["gmlp_ws"], lw["gmlp_bs"], min(seq, GMLP_CHUNK))
    yb, pool_state = pool_mix(mid, pool_prefix, pool_start, lw["pool_w"], lw["pool_scale"], batch, seq)
    if cache is None:
        yc = attn_prompt(mid, lw["rel_bias"], batch, seq)
    else:
        yc = attn_sample(mid, cache[0], cache[1], cache[2], lw["rel_bias"], batch, seq)
    x = mixer_out(x, ya, yb, yc, gates, lw["w_branch"], lw["w_out"], x_row0)
    ffn, stage, sc_out = peer_block(x, lw["norm_ffn"], lw["peer_wq"], lw["peer_subkeys"], lw["peer_table"], prev_sc)
    x = ple_block(x, ffn, ple, lw["norm_ple"], lw["ple_gate"], lw["ple_proj"], lw["norm_final"], final, ple_row0)
    return x, mid, pool_state, vn, stage, sc_out


def kernel(x_prompt, x_sample, cache_attn_k, cache_attn_v, state_pool, p_prompt, p_sample, norm_mix, w_in, gmlp_ln_g, gmlp_ln_b, gmlp_ws, gmlp_bs, pool_w, pool_scale, attn_rel_bias, w_branch, w_out, norm_ffn, peer_wq, peer_subkeys, peer_u, peer_v, norm_ple, ple_gate, ple_proj, norm_final):
    bp, lp, _ = x_prompt.shape
    bs, ls, _ = x_sample.shape
    assert lp % BAND_PAST == 0 and lp % GMLP_CHUNK == 0 and ls <= CHUNK
    n_keep = min(BAND_PAST, lp)
    sizes = _prompt_groups(bp)
    starts = [sum(sizes[:g]) for g in range(len(sizes))]
    x_all = x_prompt.reshape(bp * lp, D_MODEL)
    ple_all = p_prompt.reshape(DEPTH * bp * lp, PLE_DIM)
    xg = [None] * len(sizes)
    xs = x_sample.reshape(bs * ls, D_MODEL)
    outs = {k: [] for k in ("pk", "pv", "pps", "sk", "sv", "sps", "sgv")}
    after = prev_sc = None
    stages = []
    for i in range(DEPTH):
        tab_u, tab_v = peer_u, peer_v
        if stages:
            tab_u, tab_v, _ = lax.optimization_barrier((tab_u, tab_v, stages[len(stages) // 2]))
        stages = []
        lw = dict(
            norm_mix=norm_mix[i],
            w_in=w_in[i].astype(BF16),
            gmlp_ln_g=gmlp_ln_g[i], gmlp_ln_b=gmlp_ln_b[i], gmlp_ws=gmlp_ws[i], gmlp_bs=gmlp_bs[i],
            pool_w=pool_w[i], pool_scale=pool_scale[i], rel_bias=attn_rel_bias[i],
            w_branch=w_branch[i].astype(BF16), w_out=w_out[i].astype(BF16),
            norm_ffn=norm_ffn[i], peer_wq=peer_wq[i].astype(BF16),
            peer_subkeys=peer_subkeys[i].reshape(2 * PEER_HEADS, PEER_NKEYS, PEER_HALF).astype(BF16),
            peer_table=pack_expert_tables(tab_u, tab_v, i),
            norm_ple=norm_ple[i], ple_gate=ple_gate[i].astype(BF16), ple_proj=ple_proj[i].astype(BF16),
            norm_final=norm_final,
        )
        final = i == DEPTH - 1
        pk, pv, pps = [], [], []
        for g, (a, bg) in enumerate(zip(starts, sizes)):
            zero_prefix = jnp.zeros((bg, POOL_STATE, BW), F32)
            x_in, x_row0 = (x_all, a * lp) if i == 0 else (xg[g], 0)
            xg[g], mid_p, ps_p, _, after, prev_sc = _trunk_layer(x_in, ple_all, lw, bg, lp, zero_prefix, 0, None, final,
                                                                 after, prev_sc, x_row0, (i * bp + a) * lp)
            stages.append(after)
            mid_p = mid_p.reshape(bg, lp, 4 * BW)
            pk.append(mid_p[:, lp - n_keep:, 2 * BW:3 * BW].reshape(bg, n_keep, HEADS, HEAD_DIM))
            pv.append(mid_p[:, lp - n_keep:, 3 * BW:].reshape(bg, n_keep, HEADS, HEAD_DIM))
            pps.append(ps_p)
        outs["pk"].append(jnp.concatenate(pk, axis=0))
        outs["pv"].append(jnp.concatenate(pv, axis=0))
        outs["pps"].append(jnp.concatenate(pps, axis=0))
        xs, mid_s, ps_s, vn_s, _, _ = _trunk_layer(xs, p_sample[i].reshape(bs * ls, PLE_DIM), lw, bs, ls, state_pool[i],
                                                PAST_LEN, (cache_attn_k, cache_attn_v, i), final)
        mid_s = mid_s.reshape(bs, ls, 4 * BW)
        outs["sk"].append(mid_s[:, :, 2 * BW:3 * BW].reshape(bs, ls, HEADS, HEAD_DIM))
        outs["sv"].append(mid_s[:, :, 3 * BW:].reshape(bs, ls, HEADS, HEAD_DIM))
        outs["sps"].append(ps_s)
        outs["sgv"].append(vn_s.reshape(bs, ls, BW))
    st = lambda k: jnp.stack(outs[k])
    y_prompt = jnp.concatenate(xg, axis=0).reshape(bp, lp, D_MODEL)
    return (y_prompt, xs.reshape(bs, ls, D_MODEL), st("pk"), st("pv"), st("pps"),
            st("sk"), st("sv"), st("sps"), st("sgv"))
```
